```python
import jax, jax.numpy as jnp
from jax import lax
import numpy as np

D_MODEL = 1024
BATCH = 8
SEQ = 4096
DEPTH = 2

N_MIXERS = 2
DN_HEADS = 8
DN_HEAD_DIM = 128
DN_INNER = DN_HEADS * DN_HEAD_DIM
DN_CONV = 4
DN_CHUNK = 64
DN_IN_COLS = 4 * DN_INNER + 2 * DN_HEADS
CV_WIDTH = 31
MEM_LEN = 256
XA_HEADS = 4
XA_HEAD_DIM = D_MODEL // XA_HEADS
D_FF = 4 * D_MODEL
N_DN_LAYERS = (DEPTH + 1) // 2
N_CV_LAYERS = DEPTH // 2
RMS_EPS = 1e-6
LN_EPS = 1e-5

kernel_name = "hybrid_deltanet_conformer_xattn_trunk"


def rms_norm(x, g):
    xf = x.astype(jnp.float32)
    y = xf * lax.rsqrt(jnp.mean(xf * xf, axis=-1, keepdims=True) + RMS_EPS)
    return (y * g.astype(jnp.float32)).astype(x.dtype)


def layer_norm(x, g, b):
    xf = x.astype(jnp.float32)
    mu = jnp.mean(xf, axis=-1, keepdims=True)
    xc = xf - mu
    y = xc * lax.rsqrt(jnp.mean(xc * xc, axis=-1, keepdims=True) + LN_EPS)
    return (y * g.astype(jnp.float32) + b.astype(jnp.float32)).astype(x.dtype)


def l2_normalize(x):
    return x * lax.rsqrt(jnp.sum(x * x, axis=-1, keepdims=True) + 1e-6)


def causal_depthwise_conv(x, w):
    width = w.shape[0]
    return lax.conv_general_dilated(
        x, w[:, None, :].astype(x.dtype), window_strides=(1,), padding=[(width - 1, 0)],
        dimension_numbers=("NWC", "WIO", "NWC"), feature_group_count=x.shape[-1])


def chunk_gated_delta_rule(q, k, v, g, beta):
    B, S, H, dk = q.shape
    dv = v.shape[-1]
    n = S // DN_CHUNK

    def blocks(t):
        return t.reshape(B, n, DN_CHUNK, H, t.shape[-1]).transpose(0, 3, 1, 2, 4)

    q, k, v = blocks(q), blocks(k), blocks(v)
    g = g.reshape(B, n, DN_CHUNK, H).transpose(0, 3, 1, 2)
    beta = beta.reshape(B, n, DN_CHUNK, H).transpose(0, 3, 1, 2)
    g_cum = jnp.cumsum(g, axis=-1)

    idx = jnp.arange(DN_CHUNK)
    causal = idx[:, None] >= idx[None, :]
    strict = idx[:, None] > idx[None, :]
    diff = g_cum[..., :, None] - g_cum[..., None, :]
    decay = jnp.exp(jnp.where(causal, diff, -jnp.inf))

    k_beta = k * beta[..., None]
    lower = jnp.where(strict, jnp.einsum("bhnik,bhnjk->bhnij", k_beta, k) * decay, 0.0)
    a_mat = jnp.eye(DN_CHUNK, dtype=q.dtype) + lower
    rhs = jnp.concatenate([v * beta[..., None], k_beta * jnp.exp(g_cum)[..., None]], axis=-1)
    sol = lax.linalg.triangular_solve(a_mat, rhs, left_side=True, lower=True, unit_diagonal=True)
    u, w = sol[..., :dv], sol[..., dv:]

    attn_intra = jnp.einsum("bhnik,bhnjk->bhnij", q, k) * decay
    q_dec = q * jnp.exp(g_cum)[..., None]
    g_last = g_cum[..., -1]
    k_dec = k * jnp.exp(g_last[..., None] - g_cum)[..., None]

    xs = tuple(jnp.moveaxis(t, 2, 0) for t in (q_dec, k_dec, u, w, attn_intra, g_last))

    def step(state, inp):
        qd, kd, uc, wc, ai, gl = inp
        v_new = uc - jnp.einsum("bhck,bhkv->bhcv", wc, state)
        o = jnp.einsum("bhck,bhkv->bhcv", qd, state) + jnp.einsum("bhcj,bhjv->bhcv", ai, v_new)
        state = state * jnp.exp(gl)[..., None, None] + jnp.einsum("bhck,bhcv->bhkv", kd, v_new)
        return state, o

    s0 = jnp.zeros((B, H, dk, dv), jnp.float32)
    _, o = lax.scan(step, s0, xs)
    return o.transpose(1, 0, 3, 2, 4).reshape(B, S, H, dv)


def gated_deltanet(h, w_in, w_conv, a_log, dt_bias, out_norm, w_out):
    B, S, _ = h.shape
    proj = h @ w_in
    qkv = proj[..., :3 * DN_INNER]
    z = proj[..., 3 * DN_INNER:4 * DN_INNER]
    b_raw = proj[..., 4 * DN_INNER:4 * DN_INNER + DN_HEADS].astype(jnp.float32)
    a_raw = proj[..., 4 * DN_INNER + DN_HEADS:].astype(jnp.float32)
    qkv = jax.nn.silu(causal_depthwise_conv(qkv, w_conv)).astype(jnp.float32)
    q = qkv[..., :DN_INNER].reshape(B, S, DN_HEADS, DN_HEAD_DIM)
    k = qkv[..., DN_INNER:2 * DN_INNER].reshape(B, S, DN_HEADS, DN_HEAD_DIM)
    v = qkv[..., 2 * DN_INNER:].reshape(B, S, DN_HEADS, DN_HEAD_DIM)
    q = l2_normalize(q) * (DN_HEAD_DIM ** -0.5)
    k = l2_normalize(k)
    beta = jax.nn.sigmoid(b_raw)
    g = -jnp.exp(a_log.astype(jnp.float32)) * jax.nn.softplus(a_raw + dt_bias.astype(jnp.float32))
    o = chunk_gated_delta_rule(q, k, v, g, beta)
    o = o * lax.rsqrt(jnp.mean(o * o, axis=-1, keepdims=True) + RMS_EPS) * out_norm.astype(jnp.float32)
    o = o * jax.nn.silu(z.astype(jnp.float32).reshape(B, S, DN_HEADS, DN_HEAD_DIM))
    return o.astype(h.dtype).reshape(B, S, DN_INNER) @ w_out


def conformer_conv(h, w_pw1, b_pw1, w_dw, b_dw, ln_g, ln_b, w_pw2, b_pw2):
    u = h @ w_pw1 + b_pw1
    u = u[..., :D_MODEL] * jax.nn.sigmoid(u[..., D_MODEL:])
    c = causal_depthwise_conv(u, w_dw) + b_dw
    c = jax.nn.silu(layer_norm(c, ln_g, ln_b))
    return c @ w_pw2 + b_pw2


def memory_cross_attention(h, mem_h, w_q, w_kv, w_o):
    B, S, _ = h.shape
    M = mem_h.shape[1]
    q = (h @ w_q).reshape(B, S, XA_HEADS, XA_HEAD_DIM)
    kv = (mem_h @ w_kv).reshape(B, M, 2, XA_HEADS, XA_HEAD_DIM)
    k, v = kv[:, :, 0], kv[:, :, 1]
    s = jnp.einsum("bshd,bmhd->bhsm", q, k).astype(jnp.float32) * (XA_HEAD_DIM ** -0.5)
    p = jax.nn.softmax(s, axis=-1).astype(v.dtype)
    o = jnp.einsum("bhsm,bmhd->bshd", p, v).reshape(B, S, D_MODEL)
    return o @ w_o


def sq_relu_mlp(h, w_up, w_down):
    return jnp.square(jax.nn.relu(h @ w_up)) @ w_down


def _fwd_setup_inputs(seed: int = 0) -> dict:
    key = jax.random.key(seed)
    ks = jax.random.split(key, 32)
    f32 = jnp.float32

    def nrm(k, shape, scale):
        return jax.random.normal(k, shape, f32) * scale

    def gain(k, shape):
        return 1.0 + 0.02 * jax.random.normal(k, shape, f32)

    dt = jax.random.uniform(ks[6], (N_DN_LAYERS, DN_HEADS), f32, 1e-3, 1e-1)
    return {
        "x": jax.random.normal(ks[0], (BATCH, SEQ, D_MODEL), f32),
        "mem": jax.random.normal(ks[1], (BATCH, MEM_LEN, D_MODEL), f32),
        "dn_norm": gain(ks[2], (N_DN_LAYERS, D_MODEL)),
        "dn_w_in": nrm(ks[3], (N_DN_LAYERS, D_MODEL, DN_IN_COLS), D_MODEL ** -0.5),
        "dn_w_conv": nrm(ks[4], (N_DN_LAYERS, DN_CONV, 3 * DN_INNER), DN_CONV ** -0.5),
        "dn_a_log": jnp.log(jax.random.uniform(ks[5], (N_DN_LAYERS, DN_HEADS), f32, 1.0, 16.0)),
        "dn_dt_bias": dt + jnp.log(-jnp.expm1(-dt)),
        "dn_out_norm": gain(ks[7], (N_DN_LAYERS, DN_HEAD_DIM)),
        "dn_w_out": nrm(ks[8], (N_DN_LAYERS, DN_INNER, D_MODEL), DN_INNER ** -0.5),
        "cv_norm": gain(ks[9], (N_CV_LAYERS, D_MODEL)),
        "cv_w_pw1": nrm(ks[10], (N_CV_LAYERS, D_MODEL, 2 * D_MODEL), D_MODEL ** -0.5),
        "cv_b_pw1": nrm(ks[11], (N_CV_LAYERS, 2 * D_MODEL), 0.02),
        "cv_w_dw": nrm(ks[12], (N_CV_LAYERS, CV_WIDTH, D_MODEL), CV_WIDTH ** -0.5),
        "cv_b_dw": nrm(ks[13], (N_CV_LAYERS, D_MODEL), 0.02),
        "cv_ln_g": gain(ks[14], (N_CV_LAYERS, D_MODEL)),
        "cv_ln_b": nrm(ks[15], (N_CV_LAYERS, D_MODEL), 0.02),
        "cv_w_pw2": nrm(ks[16], (N_CV_LAYERS, D_MODEL, D_MODEL), D_MODEL ** -0.5),
        "cv_b_pw2": nrm(ks[17], (N_CV_LAYERS, D_MODEL), 0.02),
        "xa_norm": gain(ks[18], (DEPTH, D_MODEL)),
        "xa_mem_norm": gain(ks[19], (DEPTH, D_MODEL)),
        "xa_w_q": nrm(ks[20], (DEPTH, D_MODEL, D_MODEL), D_MODEL ** -0.5),
        "xa_w_kv": nrm(ks[21], (DEPTH, D_MODEL, 2 * D_MODEL), D_MODEL ** -0.5),
        "xa_w_o": nrm(ks[22], (DEPTH, D_MODEL, D_MODEL), D_MODEL ** -0.5),
        "mlp_norm": gain(ks[23], (DEPTH, D_MODEL)),
        "mlp_w_up": nrm(ks[24], (DEPTH, D_MODEL, D_FF), D_MODEL ** -0.5),
        "mlp_w_down": nrm(ks[25], (DEPTH, D_FF, D_MODEL), D_FF ** -0.5),
        "final_norm": gain(ks[26], (D_MODEL,)),
    }


def _fwd_reference(x, mem, dn_norm, dn_w_in, dn_w_conv, dn_a_log, dn_dt_bias, dn_out_norm, dn_w_out,
              cv_norm, cv_w_pw1, cv_b_pw1, cv_w_dw, cv_b_dw, cv_ln_g, cv_ln_b, cv_w_pw2, cv_b_pw2,
              xa_norm, xa_mem_norm, xa_w_q, xa_w_kv, xa_w_o, mlp_norm, mlp_w_up, mlp_w_down,
              final_norm):
    h = x
    for layer in range(DEPTH):
        j = layer // N_MIXERS
        if layer % N_MIXERS == 0:
            h = h + gated_deltanet(rms_norm(h, dn_norm[j]), dn_w_in[j], dn_w_conv[j], dn_a_log[j],
                                   dn_dt_bias[j], dn_out_norm[j], dn_w_out[j])
        else:
            h = h + conformer_conv(rms_norm(h, cv_norm[j]), cv_w_pw1[j], cv_b_pw1[j], cv_w_dw[j],
                                   cv_b_dw[j], cv_ln_g[j], cv_ln_b[j], cv_w_pw2[j], cv_b_pw2[j])
        h = h + memory_cross_attention(rms_norm(h, xa_norm[layer]), rms_norm(mem, xa_mem_norm[layer]),
                                       xa_w_q[layer], xa_w_kv[layer], xa_w_o[layer])
        h = h + sq_relu_mlp(rms_norm(h, mlp_norm[layer]), mlp_w_up[layer], mlp_w_down[layer])
    return rms_norm(h, final_norm)


import jax as _jax
import jax.numpy as _jnp

TWIN_FORMAT = 'train_step'
FWD_PARAMS = ['x', 'mem', 'dn_norm', 'dn_w_in', 'dn_w_conv', 'dn_a_log', 'dn_dt_bias', 'dn_out_norm', 'dn_w_out', 'cv_norm', 'cv_w_pw1', 'cv_b_pw1', 'cv_w_dw', 'cv_b_dw', 'cv_ln_g', 'cv_ln_b', 'cv_w_pw2', 'cv_b_pw2', 'xa_norm', 'xa_mem_norm', 'xa_w_q', 'xa_w_kv', 'xa_w_o', 'mlp_norm', 'mlp_w_up', 'mlp_w_down', 'final_norm']
TWIN_WEIGHTS = ['dn_norm', 'dn_w_in', 'dn_w_conv', 'dn_a_log', 'dn_dt_bias', 'dn_out_norm', 'dn_w_out', 'cv_norm', 'cv_w_pw1', 'cv_b_pw1', 'cv_w_dw', 'cv_b_dw', 'cv_ln_g', 'cv_ln_b', 'cv_w_pw2', 'cv_b_pw2', 'xa_norm', 'xa_mem_norm', 'xa_w_q', 'xa_w_kv', 'xa_w_o', 'mlp_norm', 'mlp_w_up', 'mlp_w_down', 'final_norm']
TWIN_DIFF_INPUT = 'x'
TWIN_INPUTS = ['x', 'mem', 'dn_norm', 'dn_w_in', 'dn_w_conv', 'dn_a_log', 'dn_dt_bias', 'dn_out_norm', 'dn_w_out', 'cv_norm', 'cv_w_pw1', 'cv_b_pw1', 'cv_w_dw', 'cv_b_dw', 'cv_ln_g', 'cv_ln_b', 'cv_w_pw2', 'cv_b_pw2', 'xa_norm', 'xa_mem_norm', 'xa_w_q', 'xa_w_kv', 'xa_w_o', 'mlp_norm', 'mlp_w_up', 'mlp_w_down', 'final_norm', 'loss_target', 'm_dn_norm', 'm_dn_w_in', 'm_dn_w_conv', 'm_dn_a_log', 'm_dn_dt_bias', 'm_dn_out_norm', 'm_dn_w_out', 'm_cv_norm', 'm_cv_w_pw1', 'm_cv_b_pw1', 'm_cv_w_dw', 'm_cv_b_dw', 'm_cv_ln_g', 'm_cv_ln_b', 'm_cv_w_pw2', 'm_cv_b_pw2', 'm_xa_norm', 'm_xa_mem_norm', 'm_xa_w_q', 'm_xa_w_kv', 'm_xa_w_o', 'm_mlp_norm', 'm_mlp_w_up', 'm_mlp_w_down', 'm_final_norm', 'v_dn_norm', 'v_dn_w_in', 'v_dn_w_conv', 'v_dn_a_log', 'v_dn_dt_bias', 'v_dn_out_norm', 'v_dn_w_out', 'v_cv_norm', 'v_cv_w_pw1', 'v_cv_b_pw1', 'v_cv_w_dw', 'v_cv_b_dw', 'v_cv_ln_g', 'v_cv_ln_b', 'v_cv_w_pw2', 'v_cv_b_pw2', 'v_xa_norm', 'v_xa_mem_norm', 'v_xa_w_q', 'v_xa_w_kv', 'v_xa_w_o', 'v_mlp_norm', 'v_mlp_w_up', 'v_mlp_w_down', 'v_final_norm']
TWIN_OUTPUTS = ['loss', 'grad_x', 'grad_dn_norm', 'grad_dn_w_in', 'grad_dn_w_conv', 'grad_dn_a_log', 'grad_dn_dt_bias', 'grad_dn_out_norm', 'grad_dn_w_out', 'grad_cv_norm', 'grad_cv_w_pw1', 'grad_cv_b_pw1', 'grad_cv_w_dw', 'grad_cv_b_dw', 'grad_cv_ln_g', 'grad_cv_ln_b', 'grad_cv_w_pw2', 'grad_cv_b_pw2', 'grad_xa_norm', 'grad_xa_mem_norm', 'grad_xa_w_q', 'grad_xa_w_kv', 'grad_xa_w_o', 'grad_mlp_norm', 'grad_mlp_w_up', 'grad_mlp_w_down', 'grad_final_norm', 'delta_dn_norm', 'delta_dn_w_in', 'delta_dn_w_conv', 'delta_dn_a_log', 'delta_dn_dt_bias', 'delta_dn_out_norm', 'delta_dn_w_out', 'delta_cv_norm', 'delta_cv_w_pw1', 'delta_cv_b_pw1', 'delta_cv_w_dw', 'delta_cv_b_dw', 'delta_cv_ln_g', 'delta_cv_ln_b', 'delta_cv_w_pw2', 'delta_cv_b_pw2', 'delta_xa_norm', 'delta_xa_mem_norm', 'delta_xa_w_q', 'delta_xa_w_kv', 'delta_xa_w_o', 'delta_mlp_norm', 'delta_mlp_w_up', 'delta_mlp_w_down', 'delta_final_norm', 'new_m_dn_norm', 'new_m_dn_w_in', 'new_m_dn_w_conv', 'new_m_dn_a_log', 'new_m_dn_dt_bias', 'new_m_dn_out_norm', 'new_m_dn_w_out', 'new_m_cv_norm', 'new_m_cv_w_pw1', 'new_m_cv_b_pw1', 'new_m_cv_w_dw', 'new_m_cv_b_dw', 'new_m_cv_ln_g', 'new_m_cv_ln_b', 'new_m_cv_w_pw2', 'new_m_cv_b_pw2', 'new_m_xa_norm', 'new_m_xa_mem_norm', 'new_m_xa_w_q', 'new_m_xa_w_kv', 'new_m_xa_w_o', 'new_m_mlp_norm', 'new_m_mlp_w_up', 'new_m_mlp_w_down', 'new_m_final_norm', 'new_v_dn_norm', 'new_v_dn_w_in', 'new_v_dn_w_conv', 'new_v_dn_a_log', 'new_v_dn_dt_bias', 'new_v_dn_out_norm', 'new_v_dn_w_out', 'new_v_cv_norm', 'new_v_cv_w_pw1', 'new_v_cv_b_pw1', 'new_v_cv_w_dw', 'new_v_cv_b_dw', 'new_v_cv_ln_g', 'new_v_cv_ln_b', 'new_v_cv_w_pw2', 'new_v_cv_b_pw2', 'new_v_xa_norm', 'new_v_xa_mem_norm', 'new_v_xa_w_q', 'new_v_xa_w_kv', 'new_v_xa_w_o', 'new_v_mlp_norm', 'new_v_mlp_w_up', 'new_v_mlp_w_down', 'new_v_final_norm']
TWIN_LEAF_KINDS = {'loss': 'loss', 'grad_x': 'grad_x', 'grad_dn_norm': 'grad_w', 'grad_dn_w_in': 'grad_w', 'grad_dn_w_conv': 'grad_w', 'grad_dn_a_log': 'grad_w', 'grad_dn_dt_bias': 'grad_w', 'grad_dn_out_norm': 'grad_w', 'grad_dn_w_out': 'grad_w', 'grad_cv_norm': 'grad_w', 'grad_cv_w_pw1': 'grad_w', 'grad_cv_b_pw1': 'grad_w', 'grad_cv_w_dw': 'grad_w', 'grad_cv_b_dw': 'grad_w', 'grad_cv_ln_g': 'grad_w', 'grad_cv_ln_b': 'grad_w', 'grad_cv_w_pw2': 'grad_w', 'grad_cv_b_pw2': 'grad_w', 'grad_xa_norm': 'grad_w', 'grad_xa_mem_norm': 'grad_w', 'grad_xa_w_q': 'grad_w', 'grad_xa_w_kv': 'grad_w', 'grad_xa_w_o': 'grad_w', 'grad_mlp_norm': 'grad_w', 'grad_mlp_w_up': 'grad_w', 'grad_mlp_w_down': 'grad_w', 'grad_final_norm': 'grad_w', 'delta_dn_norm': 'delta_w', 'delta_dn_w_in': 'delta_w', 'delta_dn_w_conv': 'delta_w', 'delta_dn_a_log': 'delta_w', 'delta_dn_dt_bias': 'delta_w', 'delta_dn_out_norm': 'delta_w', 'delta_dn_w_out': 'delta_w', 'delta_cv_norm': 'delta_w', 'delta_cv_w_pw1': 'delta_w', 'delta_cv_b_pw1': 'delta_w', 'delta_cv_w_dw': 'delta_w', 'delta_cv_b_dw': 'delta_w', 'delta_cv_ln_g': 'delta_w', 'delta_cv_ln_b': 'delta_w', 'delta_cv_w_pw2': 'delta_w', 'delta_cv_b_pw2': 'delta_w', 'delta_xa_norm': 'delta_w', 'delta_xa_mem_norm': 'delta_w', 'delta_xa_w_q': 'delta_w', 'delta_xa_w_kv': 'delta_w', 'delta_xa_w_o': 'delta_w', 'delta_mlp_norm': 'delta_w', 'delta_mlp_w_up': 'delta_w', 'delta_mlp_w_down': 'delta_w', 'delta_final_norm': 'delta_w', 'new_m_dn_norm': 'new_m', 'new_m_dn_w_in': 'new_m', 'new_m_dn_w_conv': 'new_m', 'new_m_dn_a_log': 'new_m', 'new_m_dn_dt_bias': 'new_m', 'new_m_dn_out_norm': 'new_m', 'new_m_dn_w_out': 'new_m', 'new_m_cv_norm': 'new_m', 'new_m_cv_w_pw1': 'new_m', 'new_m_cv_b_pw1': 'new_m', 'new_m_cv_w_dw': 'new_m', 'new_m_cv_b_dw': 'new_m', 'new_m_cv_ln_g': 'new_m', 'new_m_cv_ln_b': 'new_m', 'new_m_cv_w_pw2': 'new_m', 'new_m_cv_b_pw2': 'new_m', 'new_m_xa_norm': 'new_m', 'new_m_xa_mem_norm': 'new_m', 'new_m_xa_w_q': 'new_m', 'new_m_xa_w_kv': 'new_m', 'new_m_xa_w_o': 'new_m', 'new_m_mlp_norm': 'new_m', 'new_m_mlp_w_up': 'new_m', 'new_m_mlp_w_down': 'new_m', 'new_m_final_norm': 'new_m', 'new_v_dn_norm': 'new_v', 'new_v_dn_w_in': 'new_v', 'new_v_dn_w_conv': 'new_v', 'new_v_dn_a_log': 'new_v', 'new_v_dn_dt_bias': 'new_v', 'new_v_dn_out_norm': 'new_v', 'new_v_dn_w_out': 'new_v', 'new_v_cv_norm': 'new_v', 'new_v_cv_w_pw1': 'new_v', 'new_v_cv_b_pw1': 'new_v', 'new_v_cv_w_dw': 'new_v', 'new_v_cv_b_dw': 'new_v', 'new_v_cv_ln_g': 'new_v', 'new_v_cv_ln_b': 'new_v', 'new_v_cv_w_pw2': 'new_v', 'new_v_cv_b_pw2': 'new_v', 'new_v_xa_norm': 'new_v', 'new_v_xa_mem_norm': 'new_v', 'new_v_xa_w_q': 'new_v', 'new_v_xa_w_kv': 'new_v', 'new_v_xa_w_o': 'new_v', 'new_v_mlp_norm': 'new_v', 'new_v_mlp_w_up': 'new_v', 'new_v_mlp_w_down': 'new_v', 'new_v_final_norm': 'new_v'}


def _forward(args):
    return _fwd_reference(*[args[k] for k in FWD_PARAMS])


def _output_shape():
    out = _jax.eval_shape(lambda: _forward(_fwd_setup_inputs(0)))
    return out.shape, out.dtype

N_MICROBATCH = 1
ADAM_LR = 0.001
ADAM_B1 = 0.9
ADAM_B2 = 0.999
ADAM_EPS = 1e-08
ADAM_WD = 0.01
ADAM_STEP = 10
PER_EXAMPLE_BATCH_AXIS = {'x': 0, 'mem': 0, 'loss_target': 0}
SHARED_INPUTS = []
_WEIGHT_DTYPES = {'dn_norm': _jnp.float32, 'dn_w_in': _jnp.float32, 'dn_w_conv': _jnp.float32, 'dn_a_log': _jnp.float32, 'dn_dt_bias': _jnp.float32, 'dn_out_norm': _jnp.float32, 'dn_w_out': _jnp.float32, 'cv_norm': _jnp.float32, 'cv_w_pw1': _jnp.float32, 'cv_b_pw1': _jnp.float32, 'cv_w_dw': _jnp.float32, 'cv_b_dw': _jnp.float32, 'cv_ln_g': _jnp.float32, 'cv_ln_b': _jnp.float32, 'cv_w_pw2': _jnp.float32, 'cv_b_pw2': _jnp.float32, 'xa_norm': _jnp.float32, 'xa_mem_norm': _jnp.float32, 'xa_w_q': _jnp.float32, 'xa_w_kv': _jnp.float32, 'xa_w_o': _jnp.float32, 'mlp_norm': _jnp.float32, 'mlp_w_up': _jnp.float32, 'mlp_w_down': _jnp.float32, 'final_norm': _jnp.float32}
MOMENT_SCALE = {'dn_norm': 1.835200e-01, 'dn_w_in': 8.884295e-02, 'dn_w_conv': 8.417713e-02, 'dn_a_log': 5.169589e-01, 'dn_dt_bias': 4.908560e-01, 'dn_out_norm': 3.013853e-01, 'dn_w_out': 1.000050e-01, 'cv_norm': 7.672462e-02, 'cv_w_pw1': 5.181241e-02, 'cv_b_pw1': 7.005779e-02, 'cv_w_dw': 6.760287e-02, 'cv_b_dw': 1.724007e-01, 'cv_ln_g': 9.011081e-02, 'cv_ln_b': 1.087124e-01, 'cv_w_pw2': 7.169738e-02, 'cv_b_pw2': 1.830115e-01, 'xa_norm': 1.540362e-02, 'xa_mem_norm': 2.263121e-02, 'xa_w_q': 1.550536e-02, 'xa_w_kv': 1.588887e-02, 'xa_w_o': 1.615875e-02, 'mlp_norm': 1.386024e-01, 'mlp_w_up': 7.081796e-02, 'mlp_w_down': 1.400493e-01, 'final_norm': 3.265024e+01}


def _to_microbatches(a, axis):
    t = _jnp.moveaxis(a, axis, 0)
    t = t.reshape((N_MICROBATCH, t.shape[0] // N_MICROBATCH) + t.shape[1:])
    return _jnp.moveaxis(t, 1, axis + 1)


def setup_inputs(seed: int = 0) -> dict:
    inp = _fwd_setup_inputs(seed)
    key = _jax.random.fold_in(_jax.random.key(seed), 7919)
    shape, _ = _output_shape()
    out = dict(inp)
    out["loss_target"] = _jax.random.normal(_jax.random.fold_in(key, 0), shape, _jnp.float32)
    for i, name in enumerate(TWIN_WEIGHTS):
        w = inp[name].astype(_jnp.float32)
        if MOMENT_SCALE is None:
            s = _jnp.sqrt(_jnp.mean(_jnp.square(w)) + 1e-30)
        else:
            s = MOMENT_SCALE[name]
        km, kv = _jax.random.split(_jax.random.fold_in(key, i + 1))
        out[name] = w
        out["m_" + name] = s * _jax.random.normal(km, w.shape, _jnp.float32)
        out["v_" + name] = (s * s) * _jax.random.uniform(kv, w.shape, _jnp.float32, 0.5, 1.5)
    if N_MICROBATCH > 1:
        for name, axis in PER_EXAMPLE_BATCH_AXIS.items():
            out[name] = _to_microbatches(out[name], axis)
    return {'x': out['x'], 'mem': out['mem'], 'dn_norm': out['dn_norm'], 'dn_w_in': out['dn_w_in'], 'dn_w_conv': out['dn_w_conv'], 'dn_a_log': out['dn_a_log'], 'dn_dt_bias': out['dn_dt_bias'], 'dn_out_norm': out['dn_out_norm'], 'dn_w_out': out['dn_w_out'], 'cv_norm': out['cv_norm'], 'cv_w_pw1': out['cv_w_pw1'], 'cv_b_pw1': out['cv_b_pw1'], 'cv_w_dw': out['cv_w_dw'], 'cv_b_dw': out['cv_b_dw'], 'cv_ln_g': out['cv_ln_g'], 'cv_ln_b': out['cv_ln_b'], 'cv_w_pw2': out['cv_w_pw2'], 'cv_b_pw2': out['cv_b_pw2'], 'xa_norm': out['xa_norm'], 'xa_mem_norm': out['xa_mem_norm'], 'xa_w_q': out['xa_w_q'], 'xa_w_kv': out['xa_w_kv'], 'xa_w_o': out['xa_w_o'], 'mlp_norm': out['mlp_norm'], 'mlp_w_up': out['mlp_w_up'], 'mlp_w_down': out['mlp_w_down'], 'final_norm': out['final_norm'], 'loss_target': out['loss_target'], 'm_dn_norm': out['m_dn_norm'], 'm_dn_w_in': out['m_dn_w_in'], 'm_dn_w_conv': out['m_dn_w_conv'], 'm_dn_a_log': out['m_dn_a_log'], 'm_dn_dt_bias': out['m_dn_dt_bias'], 'm_dn_out_norm': out['m_dn_out_norm'], 'm_dn_w_out': out['m_dn_w_out'], 'm_cv_norm': out['m_cv_norm'], 'm_cv_w_pw1': out['m_cv_w_pw1'], 'm_cv_b_pw1': out['m_cv_b_pw1'], 'm_cv_w_dw': out['m_cv_w_dw'], 'm_cv_b_dw': out['m_cv_b_dw'], 'm_cv_ln_g': out['m_cv_ln_g'], 'm_cv_ln_b': out['m_cv_ln_b'], 'm_cv_w_pw2': out['m_cv_w_pw2'], 'm_cv_b_pw2': out['m_cv_b_pw2'], 'm_xa_norm': out['m_xa_norm'], 'm_xa_mem_norm': out['m_xa_mem_norm'], 'm_xa_w_q': out['m_xa_w_q'], 'm_xa_w_kv': out['m_xa_w_kv'], 'm_xa_w_o': out['m_xa_w_o'], 'm_mlp_norm': out['m_mlp_norm'], 'm_mlp_w_up': out['m_mlp_w_up'], 'm_mlp_w_down': out['m_mlp_w_down'], 'm_final_norm': out['m_final_norm'], 'v_dn_norm': out['v_dn_norm'], 'v_dn_w_in': out['v_dn_w_in'], 'v_dn_w_conv': out['v_dn_w_conv'], 'v_dn_a_log': out['v_dn_a_log'], 'v_dn_dt_bias': out['v_dn_dt_bias'], 'v_dn_out_norm': out['v_dn_out_norm'], 'v_dn_w_out': out['v_dn_w_out'], 'v_cv_norm': out['v_cv_norm'], 'v_cv_w_pw1': out['v_cv_w_pw1'], 'v_cv_b_pw1': out['v_cv_b_pw1'], 'v_cv_w_dw': out['v_cv_w_dw'], 'v_cv_b_dw': out['v_cv_b_dw'], 'v_cv_ln_g': out['v_cv_ln_g'], 'v_cv_ln_b': out['v_cv_ln_b'], 'v_cv_w_pw2': out['v_cv_w_pw2'], 'v_cv_b_pw2': out['v_cv_b_pw2'], 'v_xa_norm': out['v_xa_norm'], 'v_xa_mem_norm': out['v_xa_mem_norm'], 'v_xa_w_q': out['v_xa_w_q'], 'v_xa_w_kv': out['v_xa_w_kv'], 'v_xa_w_o': out['v_xa_w_o'], 'v_mlp_norm': out['v_mlp_norm'], 'v_mlp_w_up': out['v_mlp_w_up'], 'v_mlp_w_down': out['v_mlp_w_down'], 'v_final_norm': out['v_final_norm']}


def _loss(weights, diff, rest, loss_target):
    with _jax.named_scope("forward"):
        args = {**rest, TWIN_DIFF_INPUT: diff, **{k: w.astype(_WEIGHT_DTYPES[k]) for k, w in weights.items()}}
        y = _forward(args)
    with _jax.named_scope("loss_head"):
        err = _jnp.square(y.astype(_jnp.float32) - loss_target)
        return 0.5 * _jnp.sum(_jnp.mean(err, axis=-1)) if err.ndim else 0.5 * err


def _adamw(w, g, m, v):
    m = ADAM_B1 * m + (1.0 - ADAM_B1) * g
    v = ADAM_B2 * v + (1.0 - ADAM_B2) * _jnp.square(g)
    m_hat = m / (1.0 - ADAM_B1 ** ADAM_STEP)
    v_hat = v / (1.0 - ADAM_B2 ** ADAM_STEP)
    delta = -ADAM_LR * (m_hat / (_jnp.sqrt(v_hat) + ADAM_EPS) + ADAM_WD * w)
    return delta, m, v


def reference(x, mem, dn_norm, dn_w_in, dn_w_conv, dn_a_log, dn_dt_bias, dn_out_norm, dn_w_out, cv_norm, cv_w_pw1, cv_b_pw1, cv_w_dw, cv_b_dw, cv_ln_g, cv_ln_b, cv_w_pw2, cv_b_pw2, xa_norm, xa_mem_norm, xa_w_q, xa_w_kv, xa_w_o, mlp_norm, mlp_w_up, mlp_w_down, final_norm, loss_target, m_dn_norm, m_dn_w_in, m_dn_w_conv, m_dn_a_log, m_dn_dt_bias, m_dn_out_norm, m_dn_w_out, m_cv_norm, m_cv_w_pw1, m_cv_b_pw1, m_cv_w_dw, m_cv_b_dw, m_cv_ln_g, m_cv_ln_b, m_cv_w_pw2, m_cv_b_pw2, m_xa_norm, m_xa_mem_norm, m_xa_w_q, m_xa_w_kv, m_xa_w_o, m_mlp_norm, m_mlp_w_up, m_mlp_w_down, m_final_norm, v_dn_norm, v_dn_w_in, v_dn_w_conv, v_dn_a_log, v_dn_dt_bias, v_dn_out_norm, v_dn_w_out, v_cv_norm, v_cv_w_pw1, v_cv_b_pw1, v_cv_w_dw, v_cv_b_dw, v_cv_ln_g, v_cv_ln_b, v_cv_w_pw2, v_cv_b_pw2, v_xa_norm, v_xa_mem_norm, v_xa_w_q, v_xa_w_kv, v_xa_w_o, v_mlp_norm, v_mlp_w_up, v_mlp_w_down, v_final_norm):
    given = dict(x=x, mem=mem, dn_norm=dn_norm, dn_w_in=dn_w_in, dn_w_conv=dn_w_conv, dn_a_log=dn_a_log, dn_dt_bias=dn_dt_bias, dn_out_norm=dn_out_norm, dn_w_out=dn_w_out, cv_norm=cv_norm, cv_w_pw1=cv_w_pw1, cv_b_pw1=cv_b_pw1, cv_w_dw=cv_w_dw, cv_b_dw=cv_b_dw, cv_ln_g=cv_ln_g, cv_ln_b=cv_ln_b, cv_w_pw2=cv_w_pw2, cv_b_pw2=cv_b_pw2, xa_norm=xa_norm, xa_mem_norm=xa_mem_norm, xa_w_q=xa_w_q, xa_w_kv=xa_w_kv, xa_w_o=xa_w_o, mlp_norm=mlp_norm, mlp_w_up=mlp_w_up, mlp_w_down=mlp_w_down, final_norm=final_norm, loss_target=loss_target, m_dn_norm=m_dn_norm, m_dn_w_in=m_dn_w_in, m_dn_w_conv=m_dn_w_conv, m_dn_a_log=m_dn_a_log, m_dn_dt_bias=m_dn_dt_bias, m_dn_out_norm=m_dn_out_norm, m_dn_w_out=m_dn_w_out, m_cv_norm=m_cv_norm, m_cv_w_pw1=m_cv_w_pw1, m_cv_b_pw1=m_cv_b_pw1, m_cv_w_dw=m_cv_w_dw, m_cv_b_dw=m_cv_b_dw, m_cv_ln_g=m_cv_ln_g, m_cv_ln_b=m_cv_ln_b, m_cv_w_pw2=m_cv_w_pw2, m_cv_b_pw2=m_cv_b_pw2, m_xa_norm=m_xa_norm, m_xa_mem_norm=m_xa_mem_norm, m_xa_w_q=m_xa_w_q, m_xa_w_kv=m_xa_w_kv, m_xa_w_o=m_xa_w_o, m_mlp_norm=m_mlp_norm, m_mlp_w_up=m_mlp_w_up, m_mlp_w_down=m_mlp_w_down, m_final_norm=m_final_norm, v_dn_norm=v_dn_norm, v_dn_w_in=v_dn_w_in, v_dn_w_conv=v_dn_w_conv, v_dn_a_log=v_dn_a_log, v_dn_dt_bias=v_dn_dt_bias, v_dn_out_norm=v_dn_out_norm, v_dn_w_out=v_dn_w_out, v_cv_norm=v_cv_norm, v_cv_w_pw1=v_cv_w_pw1, v_cv_b_pw1=v_cv_b_pw1, v_cv_w_dw=v_cv_w_dw, v_cv_b_dw=v_cv_b_dw, v_cv_ln_g=v_cv_ln_g, v_cv_ln_b=v_cv_ln_b, v_cv_w_pw2=v_cv_w_pw2, v_cv_b_pw2=v_cv_b_pw2, v_xa_norm=v_xa_norm, v_xa_mem_norm=v_xa_mem_norm, v_xa_w_q=v_xa_w_q, v_xa_w_kv=v_xa_w_kv, v_xa_w_o=v_xa_w_o, v_mlp_norm=v_mlp_norm, v_mlp_w_up=v_mlp_w_up, v_mlp_w_down=v_mlp_w_down, v_final_norm=v_final_norm)
    weights = {n: given[n] for n in TWIN_WEIGHTS}
    shared = {n: given[n] for n in SHARED_INPUTS}
    per_example = {n: given[n] for n in ['x', 'mem']}
    grad_fn = _jax.value_and_grad(_loss, argnums=(0, 1))

    def one_microbatch(ex, loss_target):
        ex = dict(ex)
        diff = ex.pop(TWIN_DIFF_INPUT)
        return grad_fn(weights, diff, {**shared, **ex}, loss_target)

    if N_MICROBATCH == 1:
        loss, (grad_w, grad_x) = one_microbatch(per_example, given["loss_target"])
    else:
        def body(carry, xs):
            loss_sum, grad_sum = carry
            l_k, (gw_k, gx_k) = one_microbatch(xs[0], xs[1])
            with _jax.named_scope("update"):
                return (loss_sum + l_k, _jax.tree.map(_jnp.add, grad_sum, gw_k)), gx_k

        init = (_jnp.zeros((), _jnp.float32), _jax.tree.map(_jnp.zeros_like, weights))
        (loss, grad_w), grad_x = _jax.lax.scan(body, init, (per_example, given["loss_target"]))
    with _jax.named_scope("update"):
        delta_w, new_m, new_v = {}, {}, {}
        for n in TWIN_WEIGHTS:
            delta_w[n], new_m[n], new_v[n] = _adamw(weights[n], grad_w[n], given["m_" + n], given["v_" + n])
    return (loss, grad_x, *[grad_w[n] for n in TWIN_WEIGHTS], *[delta_w[n] for n in TWIN_WEIGHTS],
            *[new_m[n] for n in TWIN_WEIGHTS], *[new_v[n] for n in TWIN_WEIGHTS])
```

```python
import functools

import jax
import jax.numpy as jnp
from jax import lax
from jax.experimental import pallas as pl
from jax.experimental.pallas import tpu as pltpu

F32 = jnp.float32
BF16 = jnp.bfloat16
HIGHEST = lax.Precision.HIGHEST
MESH = pl.DeviceIdType.MESH

D_MODEL = 1024
DN_HEADS = 8
DN_HEAD_DIM = 128
DN_CONV = 4
DN_CHUNK = 64
CV_WIDTH = 31
XA_HEADS = 4
XA_HEAD_DIM = 256
RMS_EPS = 1e-6
LN_EPS = 1e-5
L2_EPS = 1e-6

ADAM_LR = 0.001
ADAM_B1 = 0.9
ADAM_B2 = 0.999
ADAM_EPS = 1e-08
ADAM_WD = 0.01
ADAM_STEP = 10

LANES = 128
ROW_TILE = 512
CONV_ROW_TILE = 256
DN_ROW_TILE = 256
CHUNK_SHIFT = 6
DN_HALO = 8
CV_HALO = 32
VMEM_LIMIT = 48 * 1024 * 1024
N_CHIPS = 4


def _cparams(sem):
    return pltpu.CompilerParams(dimension_semantics=sem, vmem_limit_bytes=VMEM_LIMIT)


def _dot(a, b, dims=(((1,), (0,)), ((), ()))):
    return lax.dot_general(a.astype(BF16), b.astype(BF16), dims, preferred_element_type=F32)


def _dot_nt(a, b):
    return _dot(a, b, (((1,), (1,)), ((), ())))


def _dot_tn(a, b):
    return _dot(a, b, (((0,), (0,)), ((), ())))


def _dot_hi(a, b, dims=(((1,), (0,)), ((), ()))):
    return lax.dot_general(a.astype(F32), b.astype(F32), dims, precision=HIGHEST, preferred_element_type=F32)


def _sigmoid(x):
    return 1.0 / (1.0 + jnp.exp(-x))


def _silu(x):
    return x * _sigmoid(x)


def _silu_grad(x):
    s = _sigmoid(x)
    return s * (1.0 + x * (1.0 - s))


def _softplus(x):
    return jnp.maximum(x, 0.0) + jnp.log(1.0 + jnp.exp(-jnp.abs(x)))


def _iota(shape, dim):
    return lax.broadcasted_iota(jnp.int32, shape, dim)


def _lane_col(vals, lane, idx):
    return jnp.sum(jnp.where(lane == idx, vals, 0.0), axis=1, keepdims=True)


def _pick_tile(rows, cap):
    best = rows
    for t in range(16, min(rows, cap) + 1, 16):
        if rows % t == 0:
            best = t
    return best


def mm(name, a, b, *, ta=False, tb=False, out_dtype=F32, pro=None, epi=None, epi_tiles=(), epi_rows=(),
       tm=512, tn=512, tk=1024):
    m, k = (a.shape[1], a.shape[0]) if ta else a.shape
    n = b.shape[0] if tb else b.shape[1]
    assert (b.shape[1] if tb else b.shape[0]) == k
    tm, tn, tk = min(tm, m), min(tn, n), min(tk, k)
    assert m % tm == 0 and n % tn == 0 and k % tk == 0
    nk = k // tk
    a_spec = pl.BlockSpec((tk, tm), lambda i, j, kk: (kk, i)) if ta else pl.BlockSpec((tm, tk), lambda i, j, kk: (i, kk))
    b_spec = pl.BlockSpec((tn, tk), lambda i, j, kk: (j, kk)) if tb else pl.BlockSpec((tk, tn), lambda i, j, kk: (kk, j))
    in_specs = [a_spec, b_spec]
    in_specs += [pl.BlockSpec((tm, tn), lambda i, j, kk: (i, j)) for _ in epi_tiles]
    in_specs += [pl.BlockSpec((1, tn), lambda i, j, kk: (0, j)) for _ in epi_rows]
    n_t, n_r = len(epi_tiles), len(epi_rows)
    dims = (((0 if ta else 1,), (1 if tb else 0,)), ((), ()))

    def body(a_ref, b_ref, *rest):
        tiles = rest[:n_t]
        rows = rest[n_t:n_t + n_r]
        o_ref, acc_ref = rest[n_t + n_r], rest[n_t + n_r + 1]
        kk = pl.program_id(2)

        @pl.when(kk == 0)
        def _():
            acc_ref[...] = jnp.zeros_like(acc_ref)

        av = a_ref[...]
        if pro is not None:
            av = pro(av)
        acc_ref[...] += _dot(av, b_ref[...], dims)

        @pl.when(kk == nk - 1)
        def _():
            out = acc_ref[...]
            if epi is not None:
                out = epi(out, *[t[...] for t in tiles], *[r[...] for r in rows])
            o_ref[...] = out.astype(out_dtype)

    return pl.pallas_call(
        body, name=name, grid=(m // tm, n // tn, nk),
        in_specs=in_specs, out_specs=pl.BlockSpec((tm, tn), lambda i, j, kk: (i, j)),
        out_shape=jax.ShapeDtypeStruct((m, n), out_dtype),
        scratch_shapes=[pltpu.VMEM((tm, tn), F32)],
        compiler_params=_cparams(("parallel", "parallel", "arbitrary")),
    )(a, b, *epi_tiles, *epi_rows)


def row_call(name, body, n_rows, tm, ins, outs, accs=()):
    tm = _pick_tile(n_rows, tm)
    in_specs = []
    for arr, kind in ins:
        if kind == "tile":
            if arr.ndim == 2:
                in_specs.append(pl.BlockSpec((tm, arr.shape[1]), lambda i: (i, 0)))
            else:
                in_specs.append(pl.BlockSpec((arr.shape[0], tm, arr.shape[2]), lambda i: (0, i, 0)))
        elif kind == "full":
            in_specs.append(pl.BlockSpec(arr.shape, functools.partial(lambda i, nd: (0,) * nd, nd=arr.ndim)))
        else:
            where, h = kind
            per = tm // h
            if where == "prev":
                in_specs.append(pl.BlockSpec((h, arr.shape[1]), functools.partial(
                    lambda i, per: (jnp.maximum(i * per - 1, 0), 0), per=per)))
            else:
                last = n_rows // h - 1
                in_specs.append(pl.BlockSpec((h, arr.shape[1]), functools.partial(
                    lambda i, per, last: (jnp.minimum((i + 1) * per, last), 0), per=per, last=last)))
    out_shape, out_specs = [], []
    for shape, dtype in outs:
        out_shape.append(jax.ShapeDtypeStruct(shape, dtype))
        if len(shape) == 2:
            out_specs.append(pl.BlockSpec((tm, shape[1]), lambda i: (i, 0)))
        else:
            out_specs.append(pl.BlockSpec((shape[0], tm, shape[2]), lambda i: (0, i, 0)))
    for shape in accs:
        out_shape.append(jax.ShapeDtypeStruct(shape, F32))
        out_specs.append(pl.BlockSpec(shape, lambda i: (0, 0)))
    n_in, n_out, n_acc = len(ins), len(outs), len(accs)

    def kern(*refs):
        i = pl.program_id(0)
        in_refs = refs[:n_in]
        out_refs = refs[n_in:n_in + n_out]
        acc_refs = refs[n_in + n_out:n_in + n_out + n_acc]
        if n_acc:
            @pl.when(i == 0)
            def _():
                for r in acc_refs:
                    r[...] = jnp.zeros_like(r)
        body(i, in_refs, out_refs, acc_refs)

    res = pl.pallas_call(
        kern, name=name, grid=(n_rows // tm,), in_specs=in_specs, out_specs=out_specs, out_shape=out_shape,
        compiler_params=_cparams(("arbitrary",) if n_acc else ("parallel",)),
    )(*[a for a, _ in ins])
    return list(res)


def _rms_stats(h):
    r = lax.rsqrt(jnp.mean(h * h, axis=-1, keepdims=True) + RMS_EPS)
    return h * r, r


def rms_fwd(name, h, g):
    def body(i, ins, outs, accs):
        xhat, _ = _rms_stats(ins[0][...])
        outs[0][...] = (xhat * ins[1][...]).astype(BF16)

    return row_call(name, body, h.shape[0], ROW_TILE, [(h, "tile"), (g, "full")], [(h.shape, BF16)])[0]


def _rms_bwd_tile(dn, h, g):
    xhat, r = _rms_stats(h)
    dxhat = dn * g
    dh = r * (dxhat - xhat * jnp.mean(dxhat * xhat, axis=-1, keepdims=True))
    dg = jnp.sum(dn * xhat, axis=0, keepdims=True)
    return dh, dg


def rms_bwd(name, dn, h, g, dres):
    def body(i, ins, outs, accs):
        dh, dg = _rms_bwd_tile(ins[0][...].astype(F32), ins[1][...], ins[2][...])
        outs[0][...] = ins[3][...] + dh
        accs[0][...] += dg

    d = h.shape[1]
    out, dg = row_call(name, body, h.shape[0], ROW_TILE,
                       [(dn, "tile"), (h, "tile"), (g, "full"), (dres, "tile")], [(h.shape, F32)], [(1, d)])
    return out, dg


def mem_norm_bwd(name, dn, mem, g):
    def body(i, ins, outs, accs):
        _, dg = _rms_bwd_tile(ins[0][...].astype(F32), ins[1][...], ins[2][...])
        accs[0][...] += dg

    return row_call(name, body, mem.shape[0], ROW_TILE, [(dn, "tile"), (mem, "tile"), (g, "full")], [],
                    [(1, mem.shape[1])])[0]


def loss_head(name, h, g, target):
    d = h.shape[1]

    def body(i, ins, outs, accs):
        hv, gv = ins[0][...], ins[1][...]
        xhat, _ = _rms_stats(hv)
        err = xhat * gv - ins[2][...]
        dy = err * (1.0 / d)
        dh, dg = _rms_bwd_tile(dy, hv, gv)
        outs[0][...] = dh
        accs[0][...] += jnp.full((8, LANES), 0.5 / d, F32) * jnp.sum(err * err)
        accs[1][...] += dg

    dh, loss, dg = row_call(name, body, h.shape[0], ROW_TILE, [(h, "tile"), (g, "full"), (target, "tile")],
                            [(h.shape, F32)], [(8, LANES), (1, d)])
    return dh, loss, dg


def col_sum(name, x):
    def body(i, ins, outs, accs):
        accs[0][...] += jnp.sum(ins[0][...].astype(F32), axis=0, keepdims=True)

    return row_call(name, body, x.shape[0], ROW_TILE, [(x, "tile")], [], [(1, x.shape[1])])[0]


def _conv_taps(xcat, w_ref, cols, width, halo, tm):
    rows = halo + tm
    acc = None
    for j in range(width):
        s = width - 1 - j
        xs = xcat if s == 0 else pltpu.roll(xcat, s, 0)
        term = xs[halo:rows] * w_ref[j:j + 1, cols]
        acc = term if acc is None else acc + term
    return acc


def _conv_taps_bwd_x(dcat, w_ref, cols, width, halo, tm):
    rows = halo + tm
    acc = None
    for j in range(width):
        s = width - 1 - j
        ds = dcat if s == 0 else pltpu.roll(dcat, rows - s, 0)
        term = ds[0:tm] * w_ref[j:j + 1, cols]
        acc = term if acc is None else acc + term
    return acc


def _conv_taps_bwd_w(dy, xcat, width, halo, tm, wrows):
    rows = halo + tm
    rid = _iota((wrows, dy.shape[1]), 0)
    out = jnp.zeros((wrows, dy.shape[1]), F32)
    for j in range(width):
        s = width - 1 - j
        xs = xcat if s == 0 else pltpu.roll(xcat, s, 0)
        v = jnp.sum(dy * xs[halo:rows], axis=0, keepdims=True)
        out = out + jnp.where(rid == j, v, 0.0)
    return out


def dn_pre(qkv_raw, ba, w_conv, gate):
    s_len = qkv_raw.shape[0]
    tm = min(DN_ROW_TILE, s_len)
    n_blk = qkv_raw.shape[1] // LANES

    def body(i, ins, outs, accs):
        x_ref, xp_ref, ba_ref, w_ref, gate_ref = ins
        qkv_ref, hs_ref = outs

        def blk(cb, carry):
            cols = pl.ds(pl.multiple_of(cb * LANES, LANES), LANES)
            prev = jnp.where(i > 0, xp_ref[:, cols], 0.0)
            xcat = jnp.concatenate([prev, x_ref[:, cols]], axis=0)
            c = _conv_taps(xcat, w_ref, cols, DN_CONV, DN_HALO, tm)
            y = _silu(c)
            rs = lax.rsqrt(jnp.sum(y * y, axis=-1, keepdims=True) + L2_EPS)
            fac = jnp.where(cb < DN_HEADS, DN_HEAD_DIM ** -0.5, 1.0)
            qkv_ref[:, cols] = jnp.where(cb < 2 * DN_HEADS, y * (rs * fac), y)
            return carry

        lax.fori_loop(0, n_blk, blk, 0)

        bav = ba_ref[...]
        beta = _sigmoid(bav)
        g = -jnp.exp(gate_ref[0:1, :]) * _softplus(bav + gate_ref[1:2, :])
        lane = _iota((tm, LANES), 1)
        g = jnp.where((lane >= DN_HEADS) & (lane < 2 * DN_HEADS), g, 0.0)
        r = _iota((tm, tm), 0)
        c = _iota((tm, tm), 1)
        tri = jnp.where((r >= c) & ((r >> CHUNK_SHIFT) == (c >> CHUNK_SHIFT)), 1.0, 0.0)
        gc = _dot_hi(tri, g)
        for h in range(DN_HEADS):
            hs_ref[h] = jnp.where(lane == 0, _lane_col(beta, lane, h),
                                  jnp.where(lane == 1, _lane_col(g, lane, DN_HEADS + h),
                                            jnp.where(lane == 2, _lane_col(gc, lane, DN_HEADS + h), 0.0)))

    return row_call("dn_pre", body, s_len, tm,
                    [(qkv_raw, "tile"), (qkv_raw, ("prev", DN_HALO)), (ba, "tile"), (w_conv, "full"), (gate, "full")],
                    [(qkv_raw.shape, F32), ((DN_HEADS, s_len, LANES), F32)])


def _chunk_masks():
    r = _iota((DN_CHUNK, DN_CHUNK), 0)
    c = _iota((DN_CHUNK, DN_CHUNK), 1)
    return r, c


def _decay_matrix(gc, r, c):
    lane = _iota((DN_CHUNK, LANES), 1)
    a = jnp.where(lane == 0, gc, jnp.where(lane == 1, 1.0, 0.0))
    b = jnp.where(lane == 0, 1.0, jnp.where(lane == 1, -gc, 0.0))
    diff = _dot_hi(a, b, (((1,), (1,)), ((), ())))
    causal = r >= c
    return jnp.where(causal, jnp.exp(jnp.where(causal, diff, 0.0)), 0.0)


def _tri_inverse(low, r, c):
    t = jnp.where(r == c, 1.0, 0.0)
    b = 1
    while b < DN_CHUNK:
        shift = b.bit_length()
        sel = ((r >> shift) == (c >> shift)) & ((r & b) != 0) & ((c & b) == 0)
        lm = jnp.where(sel, low, 0.0)
        t = t - (lm if b == 1 else _dot_hi(_dot_hi(t, lm), t))
        b *= 2
    return t


def dn_solve(qkv, hs):
    s_len = qkv.shape[0]
    rb = min(ROW_TILE, s_len)
    n_chunk = rb // DN_CHUNK

    def body(k_ref, v_ref, hs_ref, u_ref, w_ref, t_ref):
        r, c = _chunk_masks()

        def chunk(n, carry):
            rows = pl.ds(pl.multiple_of(n * DN_CHUNK, DN_CHUNK), DN_CHUNK)
            k, v = k_ref[rows, :], v_ref[rows, :]
            beta, gc = hs_ref[rows, 0:1], hs_ref[rows, 2:3]
            kb = k * beta
            decay = _decay_matrix(gc, r, c)
            low = jnp.where(r > c, _dot_nt(kb, k) * decay, 0.0)
            t = _tri_inverse(low, r, c)
            u_ref[rows, :] = _dot_hi(t, v * beta)
            w_ref[rows, :] = _dot_hi(t, kb * jnp.exp(gc)).astype(BF16)
            t_ref[rows, :] = t
            return carry

        lax.fori_loop(0, n_chunk, chunk, 0)

    return pl.pallas_call(
        body, name="dn_solve", grid=(DN_HEADS, s_len // rb),
        in_specs=[pl.BlockSpec((rb, LANES), lambda h, i: (i, DN_HEADS + h)),
                  pl.BlockSpec((rb, LANES), lambda h, i: (i, 2 * DN_HEADS + h)),
                  pl.BlockSpec((None, rb, LANES), lambda h, i: (h, i, 0))],
        out_specs=[pl.BlockSpec((rb, LANES), lambda h, i: (i, h)),
                   pl.BlockSpec((rb, LANES), lambda h, i: (i, h)),
                   pl.BlockSpec((None, rb, DN_CHUNK), lambda h, i: (h, i, 0))],
        out_shape=[jax.ShapeDtypeStruct((s_len, DN_HEADS * LANES), F32),
                   jax.ShapeDtypeStruct((s_len, DN_HEADS * LANES), BF16),
                   jax.ShapeDtypeStruct((DN_HEADS, s_len, DN_CHUNK), F32)],
        compiler_params=_cparams(("parallel", "parallel")),
    )(qkv, qkv, hs)


def dn_scan_fwd(qkv, u, w, hs):
    s_len = qkv.shape[0]
    rb = min(ROW_TILE, s_len)
    n_chunk = rb // DN_CHUNK
    total_chunks = s_len // DN_CHUNK

    def body(q_ref, k_ref, u_ref, w_ref, hs_ref, o_ref, st_ref, state):
        @pl.when(pl.program_id(1) == 0)
        def _():
            state[...] = jnp.zeros_like(state)

        r, c = _chunk_masks()

        def chunk(n, carry):
            rows = pl.ds(pl.multiple_of(n * DN_CHUNK, DN_CHUNK), DN_CHUNK)
            q, k = q_ref[rows, :], k_ref[rows, :]
            gc = hs_ref[rows, 2:3]
            gl = jnp.min(gc, axis=0, keepdims=True)
            st = state[...]
            st_ref[n] = st
            decay = _decay_matrix(gc, r, c)
            vn = u_ref[rows, :] - _dot(w_ref[rows, :], st)
            ai = _dot_nt(q, k) * decay
            o_ref[rows, :] = _dot(q * jnp.exp(gc), st) + _dot(ai, vn)
            kd = k * jnp.exp(gl - gc)
            state[...] = st * jnp.exp(gl) + _dot_tn(kd, vn)
            return carry

        lax.fori_loop(0, n_chunk, chunk, 0)

    blk = lambda off: pl.BlockSpec((rb, LANES), lambda h, i: (i, off + h))
    return pl.pallas_call(
        body, name="dn_scan_fwd", grid=(DN_HEADS, s_len // rb),
        in_specs=[blk(0), blk(DN_HEADS), pl.BlockSpec((rb, LANES), lambda h, i: (i, h)),
                  pl.BlockSpec((rb, LANES), lambda h, i: (i, h)),
                  pl.BlockSpec((None, rb, LANES), lambda h, i: (h, i, 0))],
        out_specs=[pl.BlockSpec((rb, LANES), lambda h, i: (i, h)),
                   pl.BlockSpec((None, n_chunk, LANES, LANES), lambda h, i: (h, i, 0, 0))],
        out_shape=[jax.ShapeDtypeStruct((s_len, DN_HEADS * LANES), F32),
                   jax.ShapeDtypeStruct((DN_HEADS, total_chunks, LANES, LANES), F32)],
        scratch_shapes=[pltpu.VMEM((LANES, LANES), F32)],
        compiler_params=_cparams(("parallel", "arbitrary")),
    )(qkv, qkv, u, w, hs)


def dn_scan_bwd(qkv, u, w, t_inv, hs, states, d_o):
    s_len = qkv.shape[0]
    rb = min(ROW_TILE, s_len)
    n_chunk = rb // DN_CHUNK
    n_blk = s_len // rb

    def body(q_ref, k_ref, v_ref, u_ref, w_ref, t_ref, hs_ref, st_ref, do_ref,
             dq_ref, dk_ref, dv_ref, dhs_ref, dstate):
        @pl.when(pl.program_id(1) == 0)
        def _():
            dstate[...] = jnp.zeros_like(dstate)

        r, c = _chunk_masks()
        causal = r >= c
        strict = r > c
        lane = _iota((DN_CHUNK, LANES), 1)
        upper = jnp.where(r <= c, 1.0, 0.0)
        last_row = _iota((DN_CHUNK, 1), 0) == DN_CHUNK - 1

        def chunk(m, carry):
            n = n_chunk - 1 - m
            rows = pl.ds(pl.multiple_of(n * DN_CHUNK, DN_CHUNK), DN_CHUNK)
            q, k, v = q_ref[rows, :], k_ref[rows, :], v_ref[rows, :]
            uu, ww, tt, do = u_ref[rows, :], w_ref[rows, :], t_ref[rows, :], do_ref[rows, :]
            beta, gc = hs_ref[rows, 0:1], hs_ref[rows, 2:3]
            gl = jnp.min(gc, axis=0, keepdims=True)
            st = st_ref[n]
            dst = dstate[...]
            egc = jnp.exp(gc)
            egl = jnp.exp(gl)
            ekd = jnp.exp(gl - gc)
            decay = _decay_matrix(gc, r, c)
            qd = q * egc
            kd = k * ekd
            kb = k * beta
            vn = uu - _dot(ww, st)
            ai = _dot_nt(q, k) * decay
            dqd = _dot_nt(do, st)
            dai = jnp.where(causal, _dot_nt(do, vn), 0.0)
            dvn = _dot_tn(ai, do) + _dot(kd, dst)
            dkd = _dot_nt(vn, dst)
            dw = -_dot_nt(dvn, st)
            dgl = jnp.sum(dst * st) * egl
            dstate[...] = dst * egl + _dot_tn(qd, do) - _dot_tn(ww, dvn)
            dp = dai * decay
            dq = dqd * egc + _dot(dp, k)
            dk_a = dkd * ekd + _dot_tn(dp, q)
            rkd = jnp.sum(dkd * kd, axis=-1, keepdims=True)
            dims_tn = (((0,), (0,)), ((), ()))
            drhs_u = _dot_hi(tt, dvn, dims_tn)
            drhs_w = _dot_hi(tt, dw, dims_tn)
            dlow = jnp.where(strict, -(_dot_nt(drhs_u, uu) + _dot_nt(drhs_w, ww)), 0.0)
            dqm = dlow * decay
            dkb_w = drhs_w * egc
            dkb = dkb_w + _dot(dqm, k)
            dk_l = _dot_tn(dqm, kb)
            dq_ref[rows, :] = dq
            dk_ref[rows, :] = dk_a + dk_l + dkb * beta
            dv_ref[rows, :] = drhs_u * beta
            dbeta = jnp.sum(drhs_u * v, axis=-1, keepdims=True) + jnp.sum(dkb * k, axis=-1, keepdims=True)
            low = jnp.where(strict, _dot_nt(kb, k) * decay, 0.0)
            m_tot = dai * ai + dlow * low
            col_sums = _dot_hi(m_tot, jnp.ones((DN_CHUNK, LANES), F32), dims_tn)
            dgc = (jnp.sum(dqd * qd, axis=-1, keepdims=True) - rkd + jnp.where(last_row, jnp.sum(rkd) + dgl, 0.0)
                   + jnp.sum(m_tot, axis=-1, keepdims=True) + jnp.sum(dkb_w * kb, axis=-1, keepdims=True))
            dg = _dot_hi(upper, jnp.where(lane == 1, dgc - col_sums, 0.0))
            dhs_ref[rows, :] = jnp.where(lane == 0, dbeta, dg)
            return carry

        lax.fori_loop(0, n_chunk, chunk, 0)

    blk = lambda off: pl.BlockSpec((rb, LANES), lambda h, i: (n_blk - 1 - i, off + h))
    head = pl.BlockSpec((rb, LANES), lambda h, i: (n_blk - 1 - i, h))
    hs_spec = pl.BlockSpec((None, rb, LANES), lambda h, i: (h, n_blk - 1 - i, 0))
    full = jax.ShapeDtypeStruct((s_len, DN_HEADS * LANES), F32)
    return pl.pallas_call(
        body, name="dn_scan_bwd", grid=(DN_HEADS, n_blk),
        in_specs=[blk(0), blk(DN_HEADS), blk(2 * DN_HEADS), head, head,
                  pl.BlockSpec((None, rb, DN_CHUNK), lambda h, i: (h, n_blk - 1 - i, 0)), hs_spec,
                  pl.BlockSpec((None, n_chunk, LANES, LANES), lambda h, i: (h, n_blk - 1 - i, 0, 0)), head],
        out_specs=[head, head, head, hs_spec],
        out_shape=[full, full, full, jax.ShapeDtypeStruct((DN_HEADS, s_len, LANES), F32)],
        scratch_shapes=[pltpu.VMEM((LANES, LANES), F32)],
        compiler_params=_cparams(("parallel", "arbitrary")),
    )(qkv, qkv, qkv, u, w, t_inv, hs, states, d_o)


def dn_post(o, z, out_norm):
    def body(i, ins, outs, accs):
        gn = ins[2][...]
        for h in range(DN_HEADS):
            cols = slice(h * LANES, (h + 1) * LANES)
            xhat, _ = _rms_stats(ins[0][:, cols])
            outs[0][:, cols] = (xhat * gn * _silu(ins[1][:, cols])).astype(BF16)

    return row_call("dn_post", body, o.shape[0], ROW_TILE, [(o, "tile"), (z, "tile"), (out_norm, "full")],
                    [(o.shape, BF16)])[0]


def dn_post_bwd(d_og, o, z, out_norm):
    def body(i, ins, outs, accs):
        gn = ins[3][...]
        dgn = jnp.zeros((1, LANES), F32)
        for h in range(DN_HEADS):
            cols = slice(h * LANES, (h + 1) * LANES)
            dy, zh = ins[0][:, cols].astype(F32), ins[2][:, cols]
            xhat, r = _rms_stats(ins[1][:, cols])
            sz = _silu(zh)
            dgn = dgn + jnp.sum(dy * xhat * sz, axis=0, keepdims=True)
            outs[1][:, cols] = (dy * xhat * gn * _silu_grad(zh)).astype(BF16)
            dxhat = dy * gn * sz
            outs[0][:, cols] = r * (dxhat - xhat * jnp.mean(dxhat * xhat, axis=-1, keepdims=True))
        accs[0][...] += dgn

    return row_call("dn_post_bwd", body, o.shape[0], ROW_TILE,
                    [(d_og, "tile"), (o, "tile"), (z, "tile"), (out_norm, "full")],
                    [(o.shape, F32), (o.shape, BF16)], [(1, LANES)])


def dn_pre_bwd(dq, dk, dv, dhs, qkv_raw, ba, w_conv, gate):
    s_len = qkv_raw.shape[0]
    tm = min(DN_ROW_TILE, s_len)

    def body(i, ins, outs, accs):
        dq_ref, dk_ref, dv_ref, dhs_ref, x_ref, xp_ref, ba_ref, w_ref, gate_ref = ins
        dc_ref, dba_ref = outs

        def blk(cb, carry):
            cols = pl.ds(pl.multiple_of(cb * LANES, LANES), LANES)
            hcols = pl.ds(pl.multiple_of((cb & (DN_HEADS - 1)) * LANES, LANES), LANES)
            prev = jnp.where(i > 0, xp_ref[:, cols], 0.0)
            xcat = jnp.concatenate([prev, x_ref[:, cols]], axis=0)
            c = _conv_taps(xcat, w_ref, cols, DN_CONV, DN_HALO, tm)
            y = _silu(c)
            dy = jnp.where(cb < DN_HEADS, dq_ref[:, hcols],
                           jnp.where(cb < 2 * DN_HEADS, dk_ref[:, hcols], dv_ref[:, hcols]))
            rs = lax.rsqrt(jnp.sum(y * y, axis=-1, keepdims=True) + L2_EPS)
            fac = jnp.where(cb < DN_HEADS, DN_HEAD_DIM ** -0.5, 1.0)
            nrm = y * rs
            dn = dy * fac
            dy_norm = rs * (dn - nrm * jnp.sum(dn * nrm, axis=-1, keepdims=True))
            dc_ref[:, cols] = jnp.where(cb < 2 * DN_HEADS, dy_norm, dy) * _silu_grad(c)
            return carry

        lax.fori_loop(0, qkv_raw.shape[1] // LANES, blk, 0)

        lane = _iota((tm, LANES), 1)
        dbeta = jnp.zeros((tm, LANES), F32)
        dg = jnp.zeros((tm, LANES), F32)
        for h in range(DN_HEADS):
            dbeta = dbeta + jnp.where(lane == h, dhs_ref[h, :, 0:1], 0.0)
            dg = dg + jnp.where(lane == DN_HEADS + h, dhs_ref[h, :, 1:2], 0.0)
        bav = ba_ref[...]
        beta = _sigmoid(bav)
        ea = jnp.exp(gate_ref[0:1, :])
        pre = bav + gate_ref[1:2, :]
        g = -ea * _softplus(pre)
        da = dg * (-ea) * _sigmoid(pre)
        dba_ref[...] = (dbeta * beta * (1.0 - beta) + da).astype(BF16)
        rid = _iota((8, LANES), 0)
        accs[0][...] += (jnp.where(rid == 0, jnp.sum(dg * g, axis=0, keepdims=True), 0.0)
                         + jnp.where(rid == 1, jnp.sum(da, axis=0, keepdims=True), 0.0))

    return row_call("dn_pre_bwd", body, s_len, tm,
                    [(dq, "tile"), (dk, "tile"), (dv, "tile"), (dhs, "tile"), (qkv_raw, "tile"),
                     (qkv_raw, ("prev", DN_HALO)), (ba, "tile"), (w_conv, "full"), (gate, "full")],
                    [(qkv_raw.shape, F32), (ba.shape, BF16)], [(8, LANES)])


def dn_conv_bwd(dc, qkv_raw, w_conv):
    s_len = dc.shape[0]
    tm = min(DN_ROW_TILE, s_len)
    nt = s_len // tm

    def body(i, ins, outs, accs):
        dc_ref, dn_ref, x_ref, xp_ref, w_ref = ins

        def blk(cb, carry):
            cols = pl.ds(pl.multiple_of(cb * LANES, LANES), LANES)
            dy = dc_ref[:, cols]
            nxt = jnp.where(i < nt - 1, dn_ref[:, cols], 0.0)
            dcat = jnp.concatenate([dy, nxt], axis=0)
            outs[0][:, cols] = _conv_taps_bwd_x(dcat, w_ref, cols, DN_CONV, DN_HALO, tm).astype(BF16)
            prev = jnp.where(i > 0, xp_ref[:, cols], 0.0)
            xcat = jnp.concatenate([prev, x_ref[:, cols]], axis=0)
            accs[0][:, cols] += _conv_taps_bwd_w(dy, xcat, DN_CONV, DN_HALO, tm, 8)
            return carry

        lax.fori_loop(0, dc.shape[1] // LANES, blk, 0)

    return row_call("dn_conv_bwd", body, s_len, tm,
                    [(dc, "tile"), (dc, ("next", DN_HALO)), (qkv_raw, "tile"), (qkv_raw, ("prev", DN_HALO)),
                     (w_conv, "full")],
                    [(dc.shape, BF16)], [(8, dc.shape[1])])


def _glu(u_ref, cols, d):
    return u_ref[:, cols] * _sigmoid(u_ref[:, pl.ds(pl.multiple_of(d + cols.start, LANES), cols.size)])


def cv_core_fwd(u, w_dw, b_dw, ln_g, ln_b):
    s_len, d = u.shape[0], u.shape[1] // 2
    tm = min(CONV_ROW_TILE, s_len)

    def body(i, ins, outs, accs):
        u_ref, up_ref, w_ref, bdw_ref, g_ref, b_ref = ins
        s_ref, c_ref = outs

        def blk(cb, carry):
            cols = pl.ds(pl.multiple_of(cb * LANES, LANES), LANES)
            prev = jnp.where(i > 0, _glu(up_ref, cols, d), 0.0)
            xcat = jnp.concatenate([prev, _glu(u_ref, cols, d)], axis=0)
            c_ref[:, cols] = _conv_taps(xcat, w_ref, cols, CV_WIDTH, CV_HALO, tm) + bdw_ref[:, cols]
            return carry

        lax.fori_loop(0, d // LANES, blk, 0)
        c = c_ref[...]
        mu = jnp.mean(c, axis=-1, keepdims=True)
        xc = c - mu
        rstd = lax.rsqrt(jnp.mean(xc * xc, axis=-1, keepdims=True) + LN_EPS)
        s_ref[...] = _silu(xc * rstd * g_ref[...] + b_ref[...]).astype(BF16)

    return row_call("cv_core_fwd", body, s_len, tm,
                    [(u, "tile"), (u, ("prev", CV_HALO)), (w_dw, "full"), (b_dw, "full"), (ln_g, "full"),
                     (ln_b, "full")],
                    [((s_len, d), BF16), ((s_len, d), F32)])


def cv_ln_bwd(ds, c, ln_g, ln_b):
    def body(i, ins, outs, accs):
        cv, g = ins[1][...], ins[2][...]
        mu = jnp.mean(cv, axis=-1, keepdims=True)
        xc = cv - mu
        rstd = lax.rsqrt(jnp.mean(xc * xc, axis=-1, keepdims=True) + LN_EPS)
        xhat = xc * rstd
        dl = ins[0][...].astype(F32) * _silu_grad(xhat * g + ins[3][...])
        dxhat = dl * g
        dc = rstd * (dxhat - jnp.mean(dxhat, axis=-1, keepdims=True)
                     - xhat * jnp.mean(dxhat * xhat, axis=-1, keepdims=True))
        outs[0][...] = dc
        rid = _iota((8, cv.shape[1]), 0)
        accs[0][...] += (jnp.where(rid == 0, jnp.sum(dl * xhat, axis=0, keepdims=True), 0.0)
                         + jnp.where(rid == 1, jnp.sum(dl, axis=0, keepdims=True), 0.0)
                         + jnp.where(rid == 2, jnp.sum(dc, axis=0, keepdims=True), 0.0))

    return row_call("cv_ln_bwd", body, c.shape[0], ROW_TILE,
                    [(ds, "tile"), (c, "tile"), (ln_g, "full"), (ln_b, "full")], [(c.shape, F32)], [(8, c.shape[1])])


def cv_conv_bwd(dc, u, w_dw):
    s_len, d = dc.shape
    tm = min(CONV_ROW_TILE, s_len)
    nt = s_len // tm

    def body(i, ins, outs, accs):
        dc_ref, dn_ref, u_ref, up_ref, w_ref = ins

        def blk(cb, carry):
            cols = pl.ds(pl.multiple_of(cb * LANES, LANES), LANES)
            gcols = pl.ds(pl.multiple_of(d + cb * LANES, LANES), LANES)
            dy = dc_ref[:, cols]
            nxt = jnp.where(i < nt - 1, dn_ref[:, cols], 0.0)
            dgl = _conv_taps_bwd_x(jnp.concatenate([dy, nxt], axis=0), w_ref, cols, CV_WIDTH, CV_HALO, tm)
            u1, sg = u_ref[:, cols], _sigmoid(u_ref[:, gcols])
            du1 = dgl * sg
            du2 = dgl * u1 * sg * (1.0 - sg)
            outs[0][:, cols] = du1.astype(BF16)
            outs[0][:, gcols] = du2.astype(BF16)
            accs[1][:, cols] += jnp.sum(du1, axis=0, keepdims=True)
            accs[1][:, gcols] += jnp.sum(du2, axis=0, keepdims=True)
            prev = jnp.where(i > 0, _glu(up_ref, cols, d), 0.0)
            xcat = jnp.concatenate([prev, u1 * sg], axis=0)
            accs[0][:, cols] += _conv_taps_bwd_w(dy, xcat, CV_WIDTH, CV_HALO, tm, CV_HALO)
            return carry

        lax.fori_loop(0, d // LANES, blk, 0)

    return row_call("cv_conv_bwd", body, s_len, tm,
                    [(dc, "tile"), (dc, ("next", CV_HALO)), (u, "tile"), (u, ("prev", CV_HALO)), (w_dw, "full")],
                    [(u.shape, BF16)], [(CV_HALO, d), (1, 2 * d)])


def xa_core_fwd(name, q, kv):
    d = q.shape[1]

    def body(i, ins, outs, accs):
        for h in range(XA_HEADS):
            cols = slice(h * XA_HEAD_DIM, (h + 1) * XA_HEAD_DIM)
            vcols = slice(d + h * XA_HEAD_DIM, d + (h + 1) * XA_HEAD_DIM)
            s = _dot_nt(ins[0][:, cols], ins[1][:, cols]) * (XA_HEAD_DIM ** -0.5)
            e = jnp.exp(s - jnp.max(s, axis=-1, keepdims=True))
            p = e / jnp.sum(e, axis=-1, keepdims=True)
            outs[0][:, cols] = _dot(p, ins[1][:, vcols]).astype(BF16)

    return row_call(name, body, q.shape[0], ROW_TILE, [(q, "tile"), (kv, "full")], [(q.shape, BF16)])[0]


def xa_core_bwd(name, d_o, q, kv):
    d = q.shape[1]

    def body(i, ins, outs, accs):
        for h in range(XA_HEADS):
            cols = slice(h * XA_HEAD_DIM, (h + 1) * XA_HEAD_DIM)
            vcols = slice(d + h * XA_HEAD_DIM, d + (h + 1) * XA_HEAD_DIM)
            qh, kh, vh, doh = ins[1][:, cols], ins[2][:, cols], ins[2][:, vcols], ins[0][:, cols]
            s = _dot_nt(qh, kh) * (XA_HEAD_DIM ** -0.5)
            e = jnp.exp(s - jnp.max(s, axis=-1, keepdims=True))
            p = e / jnp.sum(e, axis=-1, keepdims=True)
            dp = _dot_nt(doh, vh)
            ds = p * (dp - jnp.sum(dp * p, axis=-1, keepdims=True)) * (XA_HEAD_DIM ** -0.5)
            outs[0][:, cols] = _dot(ds, kh).astype(BF16)
            accs[0][:, cols] += _dot_tn(ds, qh)
            accs[0][:, vcols] += _dot_tn(p, doh)

    return row_call(name, body, q.shape[0], ROW_TILE, [(d_o, "tile"), (q, "tile"), (kv, "full")],
                    [(q.shape, BF16)], [kv.shape])


def adamw(name, w, g, m, v):
    def body(i, ins, outs, accs):
        wv, gv = ins[0][...], ins[1][...]
        mn = ADAM_B1 * ins[2][...] + (1.0 - ADAM_B1) * gv
        vn = ADAM_B2 * ins[3][...] + (1.0 - ADAM_B2) * jnp.square(gv)
        m_hat = mn / (1.0 - ADAM_B1 ** ADAM_STEP)
        v_hat = vn / (1.0 - ADAM_B2 ** ADAM_STEP)
        outs[0][...] = -ADAM_LR * (m_hat / (jnp.sqrt(v_hat) + ADAM_EPS) + ADAM_WD * wv)
        outs[1][...] = mn
        outs[2][...] = vn

    return row_call(name, body, w.shape[0], ROW_TILE, [(w, "tile"), (g, "tile"), (m, "tile"), (v, "tile")],
                    [(w.shape, F32)] * 3)


HBM_SPEC = pl.BlockSpec(memory_space=pltpu.HBM)


def _position():
    return lax.axis_index("x"), lax.axis_index("y"), lax.axis_index("c")


def _other_chips(x, y):
    return [(1 - x, y), (x, 1 - y), (1 - x, 1 - y)]


def gather_shards(packs):
    n = len(packs)

    def body(*refs):
        srcs, outs = refs[:n], refs[n:2 * n]
        send_sems, recv_sems, local_sems = refs[2 * n:]
        x, y, c = _position()
        me = 2 * x + y
        chips = _other_chips(x, y)
        sibling = (x, y, 1 - c)

        def half(ref, chip, cc):
            rows = ref.shape[1] // 2
            return ref.at[chip, pl.ds(cc * rows, rows), :]

        local, first, passed = [], [], []
        for a in range(n):
            cp = pltpu.make_async_copy(srcs[a], outs[a].at[me], local_sems.at[a])
            cp.start()
            local.append(cp)
            rows = srcs[a].shape[0] // 2
            for j, (px, py) in enumerate(chips):
                cp = pltpu.make_async_remote_copy(
                    src_ref=srcs[a].at[pl.ds(c * rows, rows), :], dst_ref=half(outs[a], me, c),
                    send_sem=send_sems.at[a, j], recv_sem=recv_sems.at[a, j],
                    device_id=(px, py, c), device_id_type=MESH)
                cp.start()
                first.append(cp)
        for a in range(n):
            for j, (px, py) in enumerate(chips):
                chip = 2 * px + py
                landed = half(outs[a], chip, c)
                pltpu.make_async_remote_copy(
                    src_ref=landed, dst_ref=landed, send_sem=send_sems.at[a, j], recv_sem=recv_sems.at[a, j],
                    device_id=(px, py, c), device_id_type=MESH).wait_recv()
                cp = pltpu.make_async_remote_copy(
                    src_ref=landed, dst_ref=landed, send_sem=send_sems.at[a, 3 + j], recv_sem=recv_sems.at[a, 3 + j],
                    device_id=sibling, device_id_type=MESH)
                cp.start()
                passed.append(cp)
        for a in range(n):
            for j, (px, py) in enumerate(chips):
                other = half(outs[a], 2 * px + py, 1 - c)
                pltpu.make_async_remote_copy(
                    src_ref=other, dst_ref=other, send_sem=send_sems.at[a, 3 + j], recv_sem=recv_sems.at[a, 3 + j],
                    device_id=sibling, device_id_type=MESH).wait_recv()
        for cp in first + passed:
            cp.wait_send()
        for cp in local:
            cp.wait()

    return pl.pallas_call(
        body, name="gather_shards",
        in_specs=[HBM_SPEC] * n, out_specs=[HBM_SPEC] * n,
        out_shape=[jax.ShapeDtypeStruct((N_CHIPS,) + p.shape, p.dtype) for p in packs],
        scratch_shapes=[pltpu.SemaphoreType.DMA((n, 6)), pltpu.SemaphoreType.DMA((n, 6)),
                        pltpu.SemaphoreType.DMA((n,))],
    )(*packs)


def pair_split(packs):
    n = len(packs)

    def body(*refs):
        srcs, outs = refs[:n], refs[n:3 * n]
        send_sems, recv_sems, local_sems = refs[3 * n:]
        x, y, c = _position()
        copies = []
        for a in range(n):
            rows = srcs[a].shape[1] // 2
            mine, theirs = outs[2 * a], outs[2 * a + 1]
            cp = pltpu.make_async_copy(srcs[a].at[:, pl.ds(c * rows, rows), :], mine, local_sems.at[a])
            cp.start()
            copies.append(cp)
            cp = pltpu.make_async_remote_copy(
                src_ref=srcs[a].at[:, pl.ds((1 - c) * rows, rows), :], dst_ref=theirs,
                send_sem=send_sems.at[a], recv_sem=recv_sems.at[a], device_id=(x, y, 1 - c), device_id_type=MESH)
            cp.start()
            copies.append(cp)
        for cp in copies:
            cp.wait()

    out_shape = []
    for p in packs:
        half = jax.ShapeDtypeStruct((p.shape[0], p.shape[1] // 2, p.shape[2]), p.dtype)
        out_shape += [half, half]
    res = pl.pallas_call(
        body, name="pair_split", in_specs=[HBM_SPEC] * n, out_specs=[HBM_SPEC] * (2 * n), out_shape=out_shape,
        scratch_shapes=[pltpu.SemaphoreType.DMA((n,)), pltpu.SemaphoreType.DMA((n,)), pltpu.SemaphoreType.DMA((n,))],
    )(*packs)
    return [(res[2 * a], res[2 * a + 1]) for a in range(n)]


def chip_scatter(packs):
    n = len(packs)

    def body(*refs):
        srcs, outs = refs[:n], refs[n:3 * n]
        send_sems, recv_sems, local_sems = refs[3 * n:]
        x, y, c = _position()
        me = 2 * x + y
        copies = []
        for a in range(n):
            mine, theirs = outs[2 * a], outs[2 * a + 1]
            cp = pltpu.make_async_copy(srcs[a].at[me], mine, local_sems.at[a])
            cp.start()
            copies.append(cp)
            for j, (px, py) in enumerate(_other_chips(x, y)):
                cp = pltpu.make_async_remote_copy(
                    src_ref=srcs[a].at[2 * px + py], dst_ref=theirs.at[j],
                    send_sem=send_sems.at[a, j], recv_sem=recv_sems.at[a, j],
                    device_id=(px, py, c), device_id_type=MESH)
                cp.start()
                copies.append(cp)
        for cp in copies:
            cp.wait()

    out_shape = []
    for p in packs:
        out_shape += [jax.ShapeDtypeStruct(p.shape[1:], p.dtype),
                      jax.ShapeDtypeStruct((N_CHIPS - 1,) + p.shape[1:], p.dtype)]
    res = pl.pallas_call(
        body, name="chip_scatter", in_specs=[HBM_SPEC] * n, out_specs=[HBM_SPEC] * (2 * n), out_shape=out_shape,
        scratch_shapes=[pltpu.SemaphoreType.DMA((n, 3)), pltpu.SemaphoreType.DMA((n, 3)),
                        pltpu.SemaphoreType.DMA((n,))],
    )(*packs)
    return [(res[2 * a], res[2 * a + 1]) for a in range(n)]


def pair_join(halves):
    n = len(halves)

    def body(*refs):
        srcs, outs = refs[:n], refs[n:2 * n]
        send_sems, recv_sems, local_sems = refs[2 * n:]
        x, y, c = _position()
        copies = []
        for a in range(n):
            cp = pltpu.make_async_copy(srcs[a], outs[a].at[c], local_sems.at[a])
            cp.start()
            copies.append(cp)
            cp = pltpu.make_async_remote_copy(
                src_ref=srcs[a], dst_ref=outs[a].at[c], send_sem=send_sems.at[a], recv_sem=recv_sems.at[a],
                device_id=(x, y, 1 - c), device_id_type=MESH)
            cp.start()
            copies.append(cp)
        for cp in copies:
            cp.wait()

    return pl.pallas_call(
        body, name="pair_join", in_specs=[HBM_SPEC] * n, out_specs=[HBM_SPEC] * n,
        out_shape=[jax.ShapeDtypeStruct((2,) + p.shape, p.dtype) for p in halves],
        scratch_shapes=[pltpu.SemaphoreType.DMA((n,)), pltpu.SemaphoreType.DMA((n,)), pltpu.SemaphoreType.DMA((n,))],
    )(*halves)


def all_sum_small(part):
    n_dev = 8
    rows = part.shape[0]

    def body(src, out, buf, send_sems, recv_sems):
        x, y, c = _position()
        me = 4 * x + 2 * y + c
        buf[me] = src[...]
        copies = []
        for k in range(1, n_dev):
            px, py, pc = x ^ ((k >> 2) & 1), y ^ ((k >> 1) & 1), c ^ (k & 1)
            cp = pltpu.make_async_remote_copy(
                src_ref=src, dst_ref=buf.at[me], send_sem=send_sems.at[k - 1], recv_sem=recv_sems.at[k - 1],
                device_id=(px, py, pc), device_id_type=MESH)
            cp.start()
            copies.append(cp)
        for cp in copies:
            cp.wait()
        acc = buf[0]
        for k in range(1, n_dev):
            acc = acc + buf[k]
        out[...] = acc

    return pl.pallas_call(
        body, name="all_sum_small",
        in_specs=[pl.BlockSpec(memory_space=pltpu.VMEM)], out_specs=pl.BlockSpec(memory_space=pltpu.VMEM),
        out_shape=jax.ShapeDtypeStruct(part.shape, F32),
        scratch_shapes=[pltpu.VMEM((n_dev, rows, part.shape[1]), F32),
                        pltpu.SemaphoreType.DMA((n_dev - 1,)), pltpu.SemaphoreType.DMA((n_dev - 1,))],
    )(part)


def add_pairs(name, a, b, out_dtype):
    shape = a.shape
    a2, b2 = a.reshape(-1, shape[-1]), b.reshape(-1, shape[-1])

    def body(i, ins, outs, accs):
        outs[0][...] = (ins[0][...].astype(F32) + ins[1][...].astype(F32)).astype(out_dtype)

    return row_call(name, body, a2.shape[0], ROW_TILE, [(a2, "tile"), (b2, "tile")],
                    [(a2.shape, out_dtype)])[0].reshape(shape)


def add_four(name, mine, theirs):
    def body(i, ins, outs, accs):
        acc = ins[0][...].astype(F32)
        for j in range(N_CHIPS - 1):
            acc = acc + ins[1][j].astype(F32)
        outs[0][...] = acc

    return row_call(name, body, mine.shape[0], ROW_TILE, [(mine, "tile"), (theirs, "tile")], [(mine.shape, F32)])[0]


PACK_COLS = 1024
BIG_ROW_MULTIPLE = 512
SMALL_ROW_MULTIPLE = 32
BIG = ["dn_w_in", "dn_w_out", "cv_w_pw1", "cv_w_pw2", "xa_w_q", "xa_w_kv", "xa_w_o", "mlp_w_up", "mlp_w_down"]
SMALL = ["dn_w_conv", "cv_norm", "cv_b_pw1", "cv_w_dw", "cv_b_dw", "cv_ln_g", "cv_ln_b", "cv_b_pw2"]
SHARD_AXIS = {"dn_w_in": 2, "dn_w_conv": 2, "dn_w_out": 1, "cv_norm": 1, "cv_w_pw1": 2, "cv_b_pw1": 1,
              "cv_w_dw": 2, "cv_b_dw": 1, "cv_ln_g": 1, "cv_ln_b": 1, "cv_w_pw2": 1, "cv_b_pw2": 1,
              "xa_w_q": 1, "xa_w_kv": 2, "xa_w_o": 1, "mlp_w_up": 2, "mlp_w_down": 1}
REPLICATED = ["dn_norm", "dn_a_log", "dn_dt_bias", "dn_out_norm", "xa_norm", "xa_mem_norm", "mlp_norm", "final_norm"]


def _pack_rows(size):
    return -(-size // PACK_COLS)


SHARD_SHAPES = {
    "dn_w_in": (1, 1024, 1028), "dn_w_conv": (1, 4, 768), "dn_w_out": (1, 256, 1024), "cv_norm": (1, 256),
    "cv_w_pw1": (1, 1024, 512), "cv_b_pw1": (1, 512), "cv_w_dw": (1, 31, 256), "cv_b_dw": (1, 256),
    "cv_ln_g": (1, 256), "cv_ln_b": (1, 256), "cv_w_pw2": (1, 256, 1024), "cv_b_pw2": (1, 256),
    "xa_w_q": (2, 256, 1024), "xa_w_kv": (2, 1024, 512), "xa_w_o": (2, 256, 1024),
    "mlp_w_up": (2, 1024, 1024), "mlp_w_down": (2, 1024, 1024)}


def _shard_shape(nm):
    return SHARD_SHAPES[nm]


def _pack(tensors, names, dtype, row_multiple):
    pieces = []
    for nm in names:
        t = tensors[nm]
        flat = t.reshape(t.shape[0], -1) if t.ndim > len(_shard_shape(nm)) else t.reshape(1, -1)
        pad = _pack_rows(flat.shape[1]) * PACK_COLS - flat.shape[1]
        pieces.append(jnp.pad(flat.astype(dtype), ((0, 0), (0, pad))))
    cat = jnp.concatenate(pieces, axis=1)
    rows = cat.shape[1] // PACK_COLS
    total = -(-rows // row_multiple) * row_multiple
    cat = jnp.pad(cat, ((0, 0), (0, (total - rows) * PACK_COLS)))
    return cat.reshape(cat.shape[0], total, PACK_COLS)


def _unpack(pack, names):
    lead = pack.shape[:-2]
    flat = pack.reshape(lead + (-1,))
    out, off = {}, 0
    for nm in names:
        shp = _shard_shape(nm)
        size = 1
        for s in shp:
            size *= s
        out[nm] = flat[..., off:off + size].reshape(lead + shp)
        off += _pack_rows(size) * PACK_COLS
    return out


def _to_full(nm, stacked):
    ax = SHARD_AXIS[nm]
    moved = jnp.moveaxis(stacked, 0, ax)
    shp = list(_shard_shape(nm))
    shp[ax] *= N_CHIPS
    return moved.reshape(shp)


def _to_shards(nm, full):
    ax = SHARD_AXIS[nm]
    shp = list(_shard_shape(nm))
    split = full.reshape(shp[:ax] + [N_CHIPS, shp[ax]] + shp[ax + 1:])
    return jnp.moveaxis(split, ax, 0)


def _row(v):
    return v.reshape(1, -1)


def mlp_fwd(tag, h, g, w_up, w_down):
    n = rms_fwd(tag + "_norm", h, g)
    up = mm(tag + "_up", n, w_up)
    out = mm(tag + "_down", up, w_down, pro=lambda t: jnp.square(jnp.maximum(t, 0.0)),
             epi=lambda acc, res: acc + res, epi_tiles=(h,))
    return out, (n, up)


def mlp_bwd(tag, dh, h, g, w_up, w_down, saved):
    n, up = saved
    dup = mm(tag + "_d_act", dh, w_down, tb=True, out_dtype=BF16,
             epi=lambda acc, t: acc * (2.0 * jnp.maximum(t, 0.0)), epi_tiles=(up,))
    dw_down = mm(tag + "_dw_down", up, dh, ta=True, pro=lambda t: jnp.square(jnp.maximum(t, 0.0)), tk=512)
    dn = mm(tag + "_dn", dup, w_up, tb=True)
    dw_up = mm(tag + "_dw_up", n, dup, ta=True, tk=512)
    dh_in, dg = rms_bwd(tag + "_norm_bwd", dn, h, g, dh)
    return dh_in, dg, dw_up, dw_down


def xa_fwd(tag, h, mem, g, g_mem, w_q, w_kv, w_o):
    n = rms_fwd(tag + "_norm", h, g)
    mem_n = rms_fwd(tag + "_mem_norm", mem, g_mem)
    q = mm(tag + "_q", n, w_q, out_dtype=BF16)
    kv = mm(tag + "_kv", mem_n, w_kv, out_dtype=BF16)
    o = xa_core_fwd(tag + "_core", q, kv)
    out = mm(tag + "_o", o, w_o, epi=lambda acc, res: acc + res, epi_tiles=(h,))
    return out, (n, mem_n, q, kv, o)


def xa_bwd(tag, dh, h, mem, g, g_mem, w_q, w_kv, w_o, saved):
    n, mem_n, q, kv, o = saved
    d_o = mm(tag + "_d_o", dh, w_o, tb=True, out_dtype=BF16)
    dw_o = mm(tag + "_dw_o", o, dh, ta=True, tk=512)
    dq, dkv = xa_core_bwd(tag + "_core_bwd", d_o, q, kv)
    dn = mm(tag + "_dn", dq, w_q, tb=True)
    dw_q = mm(tag + "_dw_q", n, dq, ta=True, tk=512)
    dh_in, dg = rms_bwd(tag + "_norm_bwd", dn, h, g, dh)
    dw_kv = mm(tag + "_dw_kv", mem_n, dkv, ta=True)
    dmem_n = mm(tag + "_dmem", dkv, w_kv, tb=True)
    dg_mem = mem_norm_bwd(tag + "_mem_norm_bwd", dmem_n, mem, g_mem)
    return dh_in, dg, dg_mem, dw_q, dw_kv, dw_o


def _gate_tile(a_log, dt_bias):
    t = jnp.zeros((8, LANES), F32)
    t = t.at[0, DN_HEADS:2 * DN_HEADS].set(a_log.reshape(-1))
    return t.at[1, DN_HEADS:2 * DN_HEADS].set(dt_bias.reshape(-1))


def dn_fwd(h, g, w_qkv, w_z, w_ba, w_conv, gate, out_norm, w_out):
    n = rms_fwd("dn_norm", h, g)
    qkv_raw = mm("dn_proj_qkv", n, w_qkv)
    z = mm("dn_proj_z", n, w_z)
    ba = mm("dn_proj_ba", n, w_ba)
    qkv, hs = dn_pre(qkv_raw, ba, w_conv, gate)
    u, w, t_inv = dn_solve(qkv, hs)
    o, states = dn_scan_fwd(qkv, u, w, hs)
    og = dn_post(o, z, out_norm)
    out = mm("dn_out", og, w_out, epi=lambda acc, res: acc + res, epi_tiles=(h,))
    return out, (n, qkv_raw, z, ba, qkv, hs, u, w, t_inv, o, states, og)


def dn_bwd(dh, h, g, w_qkv, w_z, w_ba, w_conv, gate, out_norm, w_out, saved):
    n, qkv_raw, z, ba, qkv, hs, u, w, t_inv, o, states, og = saved
    d_og = mm("dn_d_og", dh, w_out, tb=True, out_dtype=BF16)
    dw_out = mm("dn_dw_out", og, dh, ta=True, tk=512)
    d_o, dz, d_out_norm = dn_post_bwd(d_og, o, z, out_norm)
    dq, dk, dv, dhs = dn_scan_bwd(qkv, u, w, t_inv, hs, states, d_o)
    dc, dba, d_gate = dn_pre_bwd(dq, dk, dv, dhs, qkv_raw, ba, w_conv, gate)
    dqkv_raw, dw_conv = dn_conv_bwd(dc, qkv_raw, w_conv)
    dn = mm("dn_dn_qkv", dqkv_raw, w_qkv, tb=True)
    dn = mm("dn_dn_z", dz, w_z, tb=True, epi=lambda acc, t: acc + t, epi_tiles=(dn,))
    dn = mm("dn_dn_ba", dba, w_ba, tb=True, epi=lambda acc, t: acc + t, epi_tiles=(dn,))
    dw_qkv = mm("dn_dw_qkv", n, dqkv_raw, ta=True, tk=512)
    dw_z = mm("dn_dw_z", n, dz, ta=True, tk=512)
    dw_ba = mm("dn_dw_ba", n, dba, ta=True, tk=512)
    dh_in, dg = rms_bwd("dn_norm_bwd", dn, h, g, dh)
    return dh_in, dg, dw_qkv, dw_z, dw_ba, dw_conv, d_gate, d_out_norm, dw_out


def cv_fwd(h, g, w_pw1, b_pw1, w_dw, b_dw, ln_g, ln_b, w_pw2, b_pw2):
    n = rms_fwd("cv_norm", h, g)
    u = mm("cv_pw1", n, w_pw1, epi=lambda acc, b: acc + b, epi_rows=(b_pw1,))
    s, c = cv_core_fwd(u, w_dw, b_dw, ln_g, ln_b)
    out = mm("cv_pw2", s, w_pw2, epi=lambda acc, res, b: acc + res + b, epi_tiles=(h,), epi_rows=(b_pw2,))
    return out, (n, u, s, c)


def cv_bwd(dh, h, g, w_pw1, w_dw, ln_g, ln_b, w_pw2, saved):
    n, u, s, c = saved
    ds = mm("cv_d_s", dh, w_pw2, tb=True, out_dtype=BF16)
    dw_pw2 = mm("cv_dw_pw2", s, dh, ta=True, tk=512)
    db_pw2 = col_sum("cv_db_pw2", dh)
    dc, ln_acc = cv_ln_bwd(ds, c, ln_g, ln_b)
    du, dw_dw, db_pw1 = cv_conv_bwd(dc, u, w_dw)
    dn = mm("cv_dn", du, w_pw1, tb=True)
    dw_pw1 = mm("cv_dw_pw1", n, du, ta=True, tk=512)
    dh_in, dg = rms_bwd("cv_norm_bwd", dn, h, g, dh)
    return dh_in, dg, dw_pw1, db_pw1, dw_dw, ln_acc, dw_pw2, db_pw2


WEIGHTS = ["dn_norm", "dn_w_in", "dn_w_conv", "dn_a_log", "dn_dt_bias", "dn_out_norm", "dn_w_out", "cv_norm",
           "cv_w_pw1", "cv_b_pw1", "cv_w_dw", "cv_b_dw", "cv_ln_g", "cv_ln_b", "cv_w_pw2", "cv_b_pw2", "xa_norm",
           "xa_mem_norm", "xa_w_q", "xa_w_kv", "xa_w_o", "mlp_norm", "mlp_w_up", "mlp_w_down", "final_norm"]


def _as_2d(t):
    if t.ndim == 1:
        return t.reshape(1, -1)
    return t.reshape(-1, t.shape[-1])


def kernel(x, mem, dn_norm, dn_w_in, dn_w_conv, dn_a_log, dn_dt_bias, dn_out_norm, dn_w_out, cv_norm, cv_w_pw1, cv_b_pw1, cv_w_dw, cv_b_dw, cv_ln_g, cv_ln_b, cv_w_pw2, cv_b_pw2, xa_norm, xa_mem_norm, xa_w_q, xa_w_kv, xa_w_o, mlp_norm, mlp_w_up, mlp_w_down, final_norm, loss_target, m_dn_norm, m_dn_w_in, m_dn_w_conv, m_dn_a_log, m_dn_dt_bias, m_dn_out_norm, m_dn_w_out, m_cv_norm, m_cv_w_pw1, m_cv_b_pw1, m_cv_w_dw, m_cv_b_dw, m_cv_ln_g, m_cv_ln_b, m_cv_w_pw2, m_cv_b_pw2, m_xa_norm, m_xa_mem_norm, m_xa_w_q, m_xa_w_kv, m_xa_w_o, m_mlp_norm, m_mlp_w_up, m_mlp_w_down, m_final_norm, v_dn_norm, v_dn_w_in, v_dn_w_conv, v_dn_a_log, v_dn_dt_bias, v_dn_out_norm, v_dn_w_out, v_cv_norm, v_cv_w_pw1, v_cv_b_pw1, v_cv_w_dw, v_cv_b_dw, v_cv_ln_g, v_cv_ln_b, v_cv_w_pw2, v_cv_b_pw2, v_xa_norm, v_xa_mem_norm, v_xa_w_q, v_xa_w_kv, v_xa_w_o, v_mlp_norm, v_mlp_w_up, v_mlp_w_down, v_final_norm):
    args = dict(locals())
    wts = {nm: args[nm] for nm in WEIGHTS}
    mom = {nm: args["m_" + nm] for nm in WEIGHTS}
    var = {nm: args["v_" + nm] for nm in WEIGHTS}
    big_pack = _pack(wts, BIG, BF16, BIG_ROW_MULTIPLE)[0]
    small_pack = _pack(wts, SMALL, F32, SMALL_ROW_MULTIPLE)[0]
    big_all, small_all = gather_shards([big_pack, small_pack])
    full = {nm: _to_full(nm, t) for nm, t in _unpack(big_all, BIG).items()}
    full.update({nm: _to_full(nm, t) for nm, t in _unpack(small_all, SMALL).items()})
    full.update({nm: wts[nm] for nm in REPLICATED})

    dh, grads, rep = local_step(x[0], mem[0], loss_target[0], full)

    shards = {nm: _to_shards(nm, grads[nm]) for nm in BIG + SMALL}
    big_g = _pack(shards, BIG, BF16, BIG_ROW_MULTIPLE)
    small_g = _pack(shards, SMALL, F32, SMALL_ROW_MULTIPLE)
    (big_mine, big_theirs), (small_mine, small_theirs) = pair_split([big_g, small_g])
    big_pair = add_pairs("pair_add_big", big_mine, big_theirs, BF16)
    small_pair = add_pairs("pair_add_small", small_mine, small_theirs, F32)
    (bm, bt), (sm, st) = chip_scatter([big_pair, small_pair])
    big_half = add_four("chip_add_big", bm, bt)
    small_half = add_four("chip_add_small", sm, st)
    big_red, small_red = pair_join([big_half, small_half])
    red = _unpack(big_red.reshape(-1, PACK_COLS), BIG)
    red.update(_unpack(small_red.reshape(-1, PACK_COLS), SMALL))

    rep = all_sum_small(rep)
    red["dn_norm"] = rep[0:1]
    red["dn_a_log"] = rep[1:2, DN_HEADS:2 * DN_HEADS]
    red["dn_dt_bias"] = rep[2:3, DN_HEADS:2 * DN_HEADS]
    red["dn_out_norm"] = rep[3:4, :LANES]
    red["xa_norm"], red["xa_mem_norm"], red["mlp_norm"] = rep[4:6], rep[6:8], rep[8:10]
    red["final_norm"] = rep[10]
    loss = rep[11, 0]

    delta, new_m, new_v = {}, {}, {}
    for nm in WEIGHTS:
        shp = wts[nm].shape
        res = adamw("adamw_" + nm, _as_2d(wts[nm]), _as_2d(red[nm].reshape(shp)), _as_2d(mom[nm]), _as_2d(var[nm]))
        delta[nm], new_m[nm], new_v[nm] = (r.reshape(shp) for r in res)
        red[nm] = red[nm].reshape(shp)

    grad_x = dh[None]
    return (loss, grad_x, *[red[nm] for nm in WEIGHTS], *[delta[nm] for nm in WEIGHTS],
            *[new_m[nm] for nm in WEIGHTS], *[new_v[nm] for nm in WEIGHTS])


def local_step(h0, mem0, target, full):
    d = h0.shape[1]
    dn_norm, dn_a_log, dn_dt_bias, dn_out_norm = (full[nm] for nm in REPLICATED[:4])
    xa_norm, xa_mem_norm, mlp_norm, final_norm = (full[nm] for nm in REPLICATED[4:])
    inner = DN_HEADS * DN_HEAD_DIM
    w_in = full["dn_w_in"][0]
    w_qkv, w_z = w_in[:, :3 * inner], w_in[:, 3 * inner:4 * inner]
    w_ba = jnp.pad(w_in[:, 4 * inner:], ((0, 0), (0, LANES - 2 * DN_HEADS)))
    w_conv = jnp.pad(full["dn_w_conv"][0], ((0, 8 - DN_CONV), (0, 0)))
    gate = _gate_tile(dn_a_log, dn_dt_bias)
    w_dw = jnp.pad(full["cv_w_dw"][0], ((0, CV_HALO - CV_WIDTH), (0, 0)))

    dn_args = (_row(dn_norm), w_qkv, w_z, w_ba, w_conv, gate, _row(dn_out_norm), full["dn_w_out"][0])
    h1, dn_saved = dn_fwd(h0, *dn_args)
    xa_args = [(_row(xa_norm[l]), _row(xa_mem_norm[l]), full["xa_w_q"][l], full["xa_w_kv"][l], full["xa_w_o"][l])
               for l in range(2)]
    mlp_args = [(_row(mlp_norm[l]), full["mlp_w_up"][l], full["mlp_w_down"][l]) for l in range(2)]
    h2, xa0_saved = xa_fwd("xa0", h1, mem0, *xa_args[0])
    h3, mlp0_saved = mlp_fwd("mlp0", h2, *mlp_args[0])
    cv_args = (_row(full["cv_norm"][0]), full["cv_w_pw1"][0], full["cv_b_pw1"], w_dw, full["cv_b_dw"],
               full["cv_ln_g"], full["cv_ln_b"], full["cv_w_pw2"][0], full["cv_b_pw2"])
    h4, cv_saved = cv_fwd(h3, *cv_args)
    h5, xa1_saved = xa_fwd("xa1", h4, mem0, *xa_args[1])
    h6, mlp1_saved = mlp_fwd("mlp1", h5, *mlp_args[1])

    dh, loss_tile, d_final = loss_head("loss_head", h6, _row(final_norm), target)
    grads = {}
    dg_mlp, dg_xa, dg_xa_mem = [None, None], [None, None], [None, None]
    dw_up, dw_down, dw_q, dw_kv, dw_o = ([None, None] for _ in range(5))
    dh, dg_mlp[1], dw_up[1], dw_down[1] = mlp_bwd("mlp1", dh, h5, *mlp_args[1], mlp1_saved)
    dh, dg_xa[1], dg_xa_mem[1], dw_q[1], dw_kv[1], dw_o[1] = xa_bwd("xa1", dh, h4, mem0, *xa_args[1], xa1_saved)
    (dh, grads["cv_norm"], grads["cv_w_pw1"], grads["cv_b_pw1"], dw_dw, ln_acc, grads["cv_w_pw2"],
     grads["cv_b_pw2"]) = cv_bwd(dh, h3, cv_args[0], cv_args[1], w_dw, cv_args[5], cv_args[6], cv_args[7], cv_saved)
    dh, dg_mlp[0], dw_up[0], dw_down[0] = mlp_bwd("mlp0", dh, h2, *mlp_args[0], mlp0_saved)
    dh, dg_xa[0], dg_xa_mem[0], dw_q[0], dw_kv[0], dw_o[0] = xa_bwd("xa0", dh, h1, mem0, *xa_args[0], xa0_saved)
    dh, dg_dn, dw_qkv, dw_z, dw_ba, dw_conv, d_gate, d_out_norm, dw_out = dn_bwd(dh, h0, *dn_args, dn_saved)

    grads["dn_w_in"] = jnp.concatenate([dw_qkv, dw_z, dw_ba[:, :2 * DN_HEADS]], axis=1)[None]
    grads["dn_w_conv"] = dw_conv[None, :DN_CONV]
    grads["dn_w_out"] = dw_out[None]
    grads["cv_norm"] = grads["cv_norm"]
    grads["cv_w_pw1"] = grads["cv_w_pw1"][None]
    grads["cv_w_dw"] = dw_dw[None, :CV_WIDTH]
    grads["cv_ln_g"], grads["cv_ln_b"], grads["cv_b_dw"] = ln_acc[0:1], ln_acc[1:2], ln_acc[2:3]
    grads["cv_w_pw2"] = grads["cv_w_pw2"][None]
    grads["xa_w_q"], grads["xa_w_kv"], grads["xa_w_o"] = jnp.stack(dw_q), jnp.stack(dw_kv), jnp.stack(dw_o)
    grads["mlp_w_up"], grads["mlp_w_down"] = jnp.stack(dw_up), jnp.stack(dw_down)

    rep = jnp.zeros((16, d), F32)
    rep = rep.at[0].set(dg_dn[0])
    rep = rep.at[1, :LANES].set(d_gate[0])
    rep = rep.at[2, :LANES].set(d_gate[1])
    rep = rep.at[3, :LANES].set(d_out_norm[0])
    rep = rep.at[4].set(dg_xa[0][0]).at[5].set(dg_xa[1][0])
    rep = rep.at[6].set(dg_xa_mem[0][0]).at[7].set(dg_xa_mem[1][0])
    rep = rep.at[8].set(dg_mlp[0][0]).at[9].set(dg_mlp[1][0])
    rep = rep.at[10].set(d_final[0])
    rep = rep.at[11, :LANES].set(loss_tile[0])
    return dh, grads, rep
```

```python
import functools

import jax
import jax.numpy as jnp
from jax import lax
from jax.experimental import pallas as pl
from jax.experimental.pallas import tpu as pltpu

F32 = jnp.float32
BF16 = jnp.bfloat16
HIGHEST = lax.Precision.HIGHEST
MESH = pl.DeviceIdType.MESH

D_MODEL = 1024
DN_HEADS = 8
DN_HEAD_DIM = 128
DN_CONV = 4
DN_CHUNK = 64
CV_WIDTH = 31
XA_HEADS = 4
XA_HEAD_DIM = 256
RMS_EPS = 1e-6
LN_EPS = 1e-5
L2_EPS = 1e-6

ADAM_LR = 0.001
ADAM_B1 = 0.9
ADAM_B2 = 0.999
ADAM_EPS = 1e-08
ADAM_WD = 0.01
ADAM_STEP = 10

LANES = 128
ROW_TILE = 512
CONV_ROW_TILE = 256
DN_ROW_TILE = 256
CHUNK_SHIFT = 6
SOLVE_INTERLEAVE = 4
FWD_HEADS_PER_STEP = 4
BWD_HEADS_PER_STEP = 2
DN_HALO = 8
CV_HALO = 32
VMEM_LIMIT = 48 * 1024 * 1024
N_CHIPS = 4
D2D_CHUNK_ROWS = 256


def _cparams(sem):
    return pltpu.CompilerParams(dimension_semantics=sem, vmem_limit_bytes=VMEM_LIMIT)


def _dot(a, b, dims=(((1,), (0,)), ((), ()))):
    return lax.dot_general(a.astype(BF16), b.astype(BF16), dims, preferred_element_type=F32)


def _dot_nt(a, b):
    return _dot(a, b, (((1,), (1,)), ((), ())))


def _dot_tn(a, b):
    return _dot(a, b, (((0,), (0,)), ((), ())))


def _dot_hi(a, b, dims=(((1,), (0,)), ((), ()))):
    return lax.dot_general(a.astype(F32), b.astype(F32), dims, precision=HIGHEST, preferred_element_type=F32)


def _sigmoid(x):
    return 1.0 / (1.0 + jnp.exp(-x))


def _silu(x):
    return x * _sigmoid(x)


def _silu_grad(x):
    s = _sigmoid(x)
    return s * (1.0 + x * (1.0 - s))


def _softplus(x):
    return jnp.maximum(x, 0.0) + jnp.log(1.0 + jnp.exp(-jnp.abs(x)))


def _iota(shape, dim):
    return lax.broadcasted_iota(jnp.int32, shape, dim)


def _lane_col(vals, lane, idx):
    return jnp.sum(jnp.where(lane == idx, vals, 0.0), axis=1, keepdims=True)


def _pick_tile(rows, cap):
    best = rows
    for t in range(16, min(rows, cap) + 1, 16):
        if rows % t == 0:
            best = t
    return best


def mm(name, a, b, *, ta=False, tb=False, out_dtype=F32, pro=None, epi=None, epi_tiles=(), epi_rows=(),
       tm=512, tn=512, tk=1024):
    m, k = (a.shape[1], a.shape[0]) if ta else a.shape
    n = b.shape[0] if tb else b.shape[1]
    assert (b.shape[1] if tb else b.shape[0]) == k
    tm, tn, tk = min(tm, m), min(tn, n), min(tk, k)
    assert m % tm == 0 and n % tn == 0 and k % tk == 0
    nk = k // tk
    a_spec = pl.BlockSpec((tk, tm), lambda i, j, kk: (kk, i)) if ta else pl.BlockSpec((tm, tk), lambda i, j, kk: (i, kk))
    b_spec = pl.BlockSpec((tn, tk), lambda i, j, kk: (j, kk)) if tb else pl.BlockSpec((tk, tn), lambda i, j, kk: (kk, j))
    in_specs = [a_spec, b_spec]
    in_specs += [pl.BlockSpec((tm, tn), lambda i, j, kk: (i, j)) for _ in epi_tiles]
    in_specs += [pl.BlockSpec((1, tn), lambda i, j, kk: (0, j)) for _ in epi_rows]
    n_t, n_r = len(epi_tiles), len(epi_rows)
    dims = (((0 if ta else 1,), (1 if tb else 0,)), ((), ()))

    def body(a_ref, b_ref, *rest):
        tiles = rest[:n_t]
        rows = rest[n_t:n_t + n_r]
        o_ref, acc_ref = rest[n_t + n_r], rest[n_t + n_r + 1]
        kk = pl.program_id(2)

        @pl.when(kk == 0)
        def _():
            acc_ref[...] = jnp.zeros_like(acc_ref)

        av = a_ref[...]
        if pro is not None:
            av = pro(av)
        acc_ref[...] += _dot(av, b_ref[...], dims)

        @pl.when(kk == nk - 1)
        def _():
            out = acc_ref[...]
            if epi is not None:
                out = epi(out, *[t[...] for t in tiles], *[r[...] for r in rows])
            o_ref[...] = out.astype(out_dtype)

    return pl.pallas_call(
        body, name=name, grid=(m // tm, n // tn, nk),
        in_specs=in_specs, out_specs=pl.BlockSpec((tm, tn), lambda i, j, kk: (i, j)),
        out_shape=jax.ShapeDtypeStruct((m, n), out_dtype),
        scratch_shapes=[pltpu.VMEM((tm, tn), F32)],
        compiler_params=_cparams(("parallel", "parallel", "arbitrary")),
    )(a, b, *epi_tiles, *epi_rows)


def row_call(name, body, n_rows, tm, ins, outs, accs=()):
    tm = _pick_tile(n_rows, tm)
    in_specs = []
    for arr, kind in ins:
        if kind == "tile":
            if arr.ndim == 2:
                in_specs.append(pl.BlockSpec((tm, arr.shape[1]), lambda i: (i, 0)))
            else:
                in_specs.append(pl.BlockSpec((arr.shape[0], tm, arr.shape[2]), lambda i: (0, i, 0)))
        elif kind == "full":
            in_specs.append(pl.BlockSpec(arr.shape, functools.partial(lambda i, nd: (0,) * nd, nd=arr.ndim)))
        else:
            where, h = kind
            per = tm // h
            if where == "prev":
                in_specs.append(pl.BlockSpec((h, arr.shape[1]), functools.partial(
                    lambda i, per: (jnp.maximum(i * per - 1, 0), 0), per=per)))
            else:
                last = n_rows // h - 1
                in_specs.append(pl.BlockSpec((h, arr.shape[1]), functools.partial(
                    lambda i, per, last: (jnp.minimum((i + 1) * per, last), 0), per=per, last=last)))
    out_shape, out_specs = [], []
    for shape, dtype in outs:
        out_shape.append(jax.ShapeDtypeStruct(shape, dtype))
        if len(shape) == 2:
            out_specs.append(pl.BlockSpec((tm, shape[1]), lambda i: (i, 0)))
        else:
            out_specs.append(pl.BlockSpec((shape[0], tm, shape[2]), lambda i: (0, i, 0)))
    for shape in accs:
        out_shape.append(jax.ShapeDtypeStruct(shape, F32))
        out_specs.append(pl.BlockSpec(shape, lambda i: (0, 0)))
    n_in, n_out, n_acc = len(ins), len(outs), len(accs)

    def kern(*refs):
        i = pl.program_id(0)
        in_refs = refs[:n_in]
        out_refs = refs[n_in:n_in + n_out]
        acc_refs = refs[n_in + n_out:n_in + n_out + n_acc]
        if n_acc:
            @pl.when(i == 0)
            def _():
                for r in acc_refs:
                    r[...] = jnp.zeros_like(r)
        body(i, in_refs, out_refs, acc_refs)

    res = pl.pallas_call(
        kern, name=name, grid=(n_rows // tm,), in_specs=in_specs, out_specs=out_specs, out_shape=out_shape,
        compiler_params=_cparams(("arbitrary",) if n_acc else ("parallel",)),
    )(*[a for a, _ in ins])
    return list(res)


def _rms_stats(h):
    r = lax.rsqrt(jnp.mean(h * h, axis=-1, keepdims=True) + RMS_EPS)
    return h * r, r


def rms_fwd(name, h, g):
    def body(i, ins, outs, accs):
        xhat, _ = _rms_stats(ins[0][...])
        outs[0][...] = (xhat * ins[1][...]).astype(BF16)

    return row_call(name, body, h.shape[0], ROW_TILE, [(h, "tile"), (g, "full")], [(h.shape, BF16)])[0]


def _rms_bwd_tile(dn, h, g):
    xhat, r = _rms_stats(h)
    dxhat = dn * g
    dh = r * (dxhat - xhat * jnp.mean(dxhat * xhat, axis=-1, keepdims=True))
    dg = jnp.sum(dn * xhat, axis=0, keepdims=True)
    return dh, dg


def rms_bwd(name, dn, h, g, dres):
    def body(i, ins, outs, accs):
        dh, dg = _rms_bwd_tile(ins[0][...].astype(F32), ins[1][...], ins[2][...])
        outs[0][...] = ins[3][...] + dh
        accs[0][...] += dg

    d = h.shape[1]
    out, dg = row_call(name, body, h.shape[0], ROW_TILE,
                       [(dn, "tile"), (h, "tile"), (g, "full"), (dres, "tile")], [(h.shape, F32)], [(1, d)])
    return out, dg


def mem_norm_bwd(name, dn, mem, g):
    def body(i, ins, outs, accs):
        _, dg = _rms_bwd_tile(ins[0][...].astype(F32), ins[1][...], ins[2][...])
        accs[0][...] += dg

    return row_call(name, body, mem.shape[0], ROW_TILE, [(dn, "tile"), (mem, "tile"), (g, "full")], [],
                    [(1, mem.shape[1])])[0]


def loss_head(name, h, g, target):
    d = h.shape[1]

    def body(i, ins, outs, accs):
        hv, gv = ins[0][...], ins[1][...]
        xhat, _ = _rms_stats(hv)
        err = xhat * gv - ins[2][...]
        dy = err * (1.0 / d)
        dh, dg = _rms_bwd_tile(dy, hv, gv)
        outs[0][...] = dh
        accs[0][...] += jnp.full((8, LANES), 0.5 / d, F32) * jnp.sum(err * err)
        accs[1][...] += dg

    dh, loss, dg = row_call(name, body, h.shape[0], ROW_TILE, [(h, "tile"), (g, "full"), (target, "tile")],
                            [(h.shape, F32)], [(8, LANES), (1, d)])
    return dh, loss, dg


def col_sum(name, x):
    def body(i, ins, outs, accs):
        accs[0][...] += jnp.sum(ins[0][...].astype(F32), axis=0, keepdims=True)

    return row_call(name, body, x.shape[0], ROW_TILE, [(x, "tile")], [], [(1, x.shape[1])])[0]


def _conv_taps(xcat, w_ref, cols, width, halo, tm):
    rows = halo + tm
    acc = None
    for j in range(width):
        s = width - 1 - j
        xs = xcat if s == 0 else pltpu.roll(xcat, s, 0)
        term = xs[halo:rows] * w_ref[j:j + 1, cols]
        acc = term if acc is None else acc + term
    return acc


def _conv_taps_bwd_x(dcat, w_ref, cols, width, halo, tm):
    rows = halo + tm
    acc = None
    for j in range(width):
        s = width - 1 - j
        ds = dcat if s == 0 else pltpu.roll(dcat, rows - s, 0)
        term = ds[0:tm] * w_ref[j:j + 1, cols]
        acc = term if acc is None else acc + term
    return acc


def _conv_taps_bwd_w(dy, xcat, width, halo, tm, wrows):
    rows = halo + tm
    rid = _iota((wrows, dy.shape[1]), 0)
    out = jnp.zeros((wrows, dy.shape[1]), F32)
    for j in range(width):
        s = width - 1 - j
        xs = xcat if s == 0 else pltpu.roll(xcat, s, 0)
        v = jnp.sum(dy * xs[halo:rows], axis=0, keepdims=True)
        out = out + jnp.where(rid == j, v, 0.0)
    return out


def dn_pre(qkv_raw, ba, w_conv, gate):
    s_len = qkv_raw.shape[0]
    tm = min(DN_ROW_TILE, s_len)
    n_blk = qkv_raw.shape[1] // LANES

    def body(i, ins, outs, accs):
        x_ref, xp_ref, ba_ref, w_ref, gate_ref = ins
        qkv_ref, hs_ref = outs

        def blk(cb, carry):
            cols = pl.ds(pl.multiple_of(cb * LANES, LANES), LANES)
            prev = jnp.where(i > 0, xp_ref[:, cols], 0.0)
            xcat = jnp.concatenate([prev, x_ref[:, cols]], axis=0)
            c = _conv_taps(xcat, w_ref, cols, DN_CONV, DN_HALO, tm)
            y = _silu(c)
            rs = lax.rsqrt(jnp.sum(y * y, axis=-1, keepdims=True) + L2_EPS)
            fac = jnp.where(cb < DN_HEADS, DN_HEAD_DIM ** -0.5, 1.0)
            qkv_ref[:, cols] = jnp.where(cb < 2 * DN_HEADS, y * (rs * fac), y)
            return carry

        lax.fori_loop(0, n_blk, blk, 0)

        bav = ba_ref[...]
        beta = _sigmoid(bav)
        g = -jnp.exp(gate_ref[0:1, :]) * _softplus(bav + gate_ref[1:2, :])
        lane = _iota((tm, LANES), 1)
        g = jnp.where((lane >= DN_HEADS) & (lane < 2 * DN_HEADS), g, 0.0)
        r = _iota((tm, tm), 0)
        c = _iota((tm, tm), 1)
        tri = jnp.where((r >= c) & ((r >> CHUNK_SHIFT) == (c >> CHUNK_SHIFT)), 1.0, 0.0)
        gc = _dot_hi(tri, g)
        for h in range(DN_HEADS):
            hs_ref[h] = jnp.where(lane == 0, _lane_col(beta, lane, h),
                                  jnp.where(lane == 1, _lane_col(g, lane, DN_HEADS + h),
                                            jnp.where(lane == 2, _lane_col(gc, lane, DN_HEADS + h), 0.0)))

    return row_call("dn_pre", body, s_len, tm,
                    [(qkv_raw, "tile"), (qkv_raw, ("prev", DN_HALO)), (ba, "tile"), (w_conv, "full"), (gate, "full")],
                    [(qkv_raw.shape, F32), ((DN_HEADS, s_len, LANES), F32)])


def _chunk_masks():
    r = _iota((DN_CHUNK, DN_CHUNK), 0)
    c = _iota((DN_CHUNK, DN_CHUNK), 1)
    return r, c


def _decay_matrix(gc, r, c):
    lane = _iota((DN_CHUNK, LANES), 1)
    a = jnp.where(lane == 0, gc, jnp.where(lane == 1, 1.0, 0.0))
    b = jnp.where(lane == 0, 1.0, jnp.where(lane == 1, -gc, 0.0))
    diff = _dot_hi(a, b, (((1,), (1,)), ((), ())))
    causal = r >= c
    return jnp.where(causal, jnp.exp(jnp.where(causal, diff, 0.0)), 0.0)


def _tri_inverse(low, r, c):
    t = jnp.where(r == c, 1.0, 0.0)
    b = 1
    while b < DN_CHUNK:
        shift = b.bit_length()
        sel = ((r >> shift) == (c >> shift)) & ((r & b) != 0) & ((c & b) == 0)
        lm = jnp.where(sel, low, 0.0)
        t = t - (lm if b == 1 else _dot_hi(_dot_hi(t, lm), t))
        b *= 2
    return t


def dn_solve(qkv, hs):
    s_len = qkv.shape[0]
    rb = min(ROW_TILE, s_len)
    n_chunk = rb // DN_CHUNK
    interleave = min(SOLVE_INTERLEAVE, n_chunk)

    def body(k_ref, v_ref, hs_ref, u_ref, w_ref, t_ref):
        r, c = _chunk_masks()

        def chunk(n):
            rows = pl.ds(pl.multiple_of(n * DN_CHUNK, DN_CHUNK), DN_CHUNK)
            k, v = k_ref[rows, :], v_ref[rows, :]
            beta, gc = hs_ref[rows, 0:1], hs_ref[rows, 2:3]
            kb = k * beta
            decay = _decay_matrix(gc, r, c)
            low = jnp.where(r > c, _dot_nt(kb, k) * decay, 0.0)
            t = _tri_inverse(low, r, c)
            u_ref[rows, :] = _dot_hi(t, v * beta)
            w_ref[rows, :] = _dot_hi(t, kb * jnp.exp(gc)).astype(BF16)
            t_ref[rows, :] = t

        def group(gi, carry):
            for j in range(interleave):
                chunk(gi * interleave + j)
            return carry

        lax.fori_loop(0, n_chunk // interleave, group, 0)

    return pl.pallas_call(
        body, name="dn_solve", grid=(DN_HEADS, s_len // rb),
        in_specs=[pl.BlockSpec((rb, LANES), lambda h, i: (i, DN_HEADS + h)),
                  pl.BlockSpec((rb, LANES), lambda h, i: (i, 2 * DN_HEADS + h)),
                  pl.BlockSpec((None, rb, LANES), lambda h, i: (h, i, 0))],
        out_specs=[pl.BlockSpec((rb, LANES), lambda h, i: (i, h)),
                   pl.BlockSpec((rb, LANES), lambda h, i: (i, h)),
                   pl.BlockSpec((None, rb, DN_CHUNK), lambda h, i: (h, i, 0))],
        out_shape=[jax.ShapeDtypeStruct((s_len, DN_HEADS * LANES), F32),
                   jax.ShapeDtypeStruct((s_len, DN_HEADS * LANES), BF16),
                   jax.ShapeDtypeStruct((DN_HEADS, s_len, DN_CHUNK), F32)],
        compiler_params=_cparams(("parallel", "parallel")),
    )(qkv, qkv, hs)


def dn_scan_fwd(qkv, u, w, hs):
    s_len = qkv.shape[0]
    rb = min(ROW_TILE, s_len)
    n_chunk = rb // DN_CHUNK
    total_chunks = s_len // DN_CHUNK

    hps = FWD_HEADS_PER_STEP
    groups = DN_HEADS // hps

    def body(q_ref, k_ref, u_ref, w_ref, hs_ref, o_ref, st_ref, state):
        @pl.when(pl.program_id(1) == 0)
        def _():
            state[...] = jnp.zeros_like(state)

        r, c = _chunk_masks()

        def chunk(n, carry):
            rows = pl.ds(pl.multiple_of(n * DN_CHUNK, DN_CHUNK), DN_CHUNK)
            for hh in range(hps):
                cols = slice(hh * LANES, (hh + 1) * LANES)
                q, k = q_ref[rows, cols], k_ref[rows, cols]
                gc = hs_ref[hh, rows, 2:3]
                gl = jnp.min(gc, axis=0, keepdims=True)
                st = state[hh]
                st_ref[hh, n] = st
                decay = _decay_matrix(gc, r, c)
                vn = u_ref[rows, cols] - _dot(w_ref[rows, cols], st)
                ai = _dot_nt(q, k) * decay
                o_ref[rows, cols] = _dot(q * jnp.exp(gc), st) + _dot(ai, vn)
                kd = k * jnp.exp(gl - gc)
                state[hh] = st * jnp.exp(gl) + _dot_tn(kd, vn)
            return carry

        lax.fori_loop(0, n_chunk, chunk, 0)

    wide = hps * LANES
    blk = lambda off: pl.BlockSpec((rb, wide), lambda h, i: (i, off + h))
    return pl.pallas_call(
        body, name="dn_scan_fwd", grid=(groups, s_len // rb),
        in_specs=[blk(0), blk(groups), blk(0), blk(0),
                  pl.BlockSpec((hps, rb, LANES), lambda h, i: (h, i, 0))],
        out_specs=[blk(0),
                   pl.BlockSpec((hps, n_chunk, LANES, LANES), lambda h, i: (h, i, 0, 0))],
        out_shape=[jax.ShapeDtypeStruct((s_len, DN_HEADS * LANES), F32),
                   jax.ShapeDtypeStruct((DN_HEADS, total_chunks, LANES, LANES), F32)],
        scratch_shapes=[pltpu.VMEM((hps, LANES, LANES), F32)],
        compiler_params=_cparams(("parallel", "arbitrary")),
    )(qkv, qkv, u, w, hs)


def dn_scan_bwd(qkv, u, w, t_inv, hs, states, d_o):
    s_len = qkv.shape[0]
    rb = min(ROW_TILE, s_len)
    n_chunk = rb // DN_CHUNK
    n_blk = s_len // rb
    hps = BWD_HEADS_PER_STEP
    groups = DN_HEADS // hps

    def body(q_ref, k_ref, v_ref, u_ref, w_ref, t_ref, hs_ref, st_ref, do_ref,
             dq_ref, dk_ref, dv_ref, dhs_ref, dstate):
        @pl.when(pl.program_id(1) == 0)
        def _():
            dstate[...] = jnp.zeros_like(dstate)

        r, c = _chunk_masks()
        causal = r >= c
        strict = r > c
        lane = _iota((DN_CHUNK, LANES), 1)
        upper = jnp.where(r <= c, 1.0, 0.0)
        last_row = _iota((DN_CHUNK, 1), 0) == DN_CHUNK - 1

        def chunk(m, carry):
            n = n_chunk - 1 - m
            rows = pl.ds(pl.multiple_of(n * DN_CHUNK, DN_CHUNK), DN_CHUNK)
            for hh in range(hps):
                cols = slice(hh * LANES, (hh + 1) * LANES)
                q, k, v = q_ref[rows, cols], k_ref[rows, cols], v_ref[rows, cols]
                uu, ww, tt, do = u_ref[rows, cols], w_ref[rows, cols], t_ref[hh, rows, :], do_ref[rows, cols]
                beta, gc = hs_ref[hh, rows, 0:1], hs_ref[hh, rows, 2:3]
                gl = jnp.min(gc, axis=0, keepdims=True)
                st = st_ref[hh, n]
                dst = dstate[hh]
                egc = jnp.exp(gc)
                egl = jnp.exp(gl)
                ekd = jnp.exp(gl - gc)
                decay = _decay_matrix(gc, r, c)
                qd = q * egc
                kd = k * ekd
                kb = k * beta
                vn = uu - _dot(ww, st)
                ai = _dot_nt(q, k) * decay
                dqd = _dot_nt(do, st)
                dai = jnp.where(causal, _dot_nt(do, vn), 0.0)
                dvn = _dot_tn(ai, do) + _dot(kd, dst)
                dkd = _dot_nt(vn, dst)
                dw = -_dot_nt(dvn, st)
                dgl = jnp.sum(dst * st) * egl
                dstate[hh] = dst * egl + _dot_tn(qd, do) - _dot_tn(ww, dvn)
                dp = dai * decay
                dq = dqd * egc + _dot(dp, k)
                dk_a = dkd * ekd + _dot_tn(dp, q)
                rkd = jnp.sum(dkd * kd, axis=-1, keepdims=True)
                dims_tn = (((0,), (0,)), ((), ()))
                drhs_u = _dot_hi(tt, dvn, dims_tn)
                drhs_w = _dot_hi(tt, dw, dims_tn)
                dlow = jnp.where(strict, -(_dot_nt(drhs_u, uu) + _dot_nt(drhs_w, ww)), 0.0)
                dqm = dlow * decay
                dkb_w = drhs_w * egc
                dkb = dkb_w + _dot(dqm, k)
                dk_l = _dot_tn(dqm, kb)
                dq_ref[rows, cols] = dq
                dk_ref[rows, cols] = dk_a + dk_l + dkb * beta
                dv_ref[rows, cols] = drhs_u * beta
                dbeta = jnp.sum(drhs_u * v, axis=-1, keepdims=True) + jnp.sum(dkb * k, axis=-1, keepdims=True)
                low = jnp.where(strict, _dot_nt(kb, k) * decay, 0.0)
                m_tot = dai * ai + dlow * low
                col_sums = _dot_hi(m_tot, jnp.ones((DN_CHUNK, LANES), F32), dims_tn)
                dgc = (jnp.sum(dqd * qd, axis=-1, keepdims=True) - rkd
                       + jnp.where(last_row, jnp.sum(rkd) + dgl, 0.0)
                       + jnp.sum(m_tot, axis=-1, keepdims=True) + jnp.sum(dkb_w * kb, axis=-1, keepdims=True))
                dg = _dot_hi(upper, jnp.where(lane == 1, dgc - col_sums, 0.0))
                dhs_ref[hh, rows, :] = jnp.where(lane == 0, dbeta, dg)
            return carry

        lax.fori_loop(0, n_chunk, chunk, 0)

    wide = hps * LANES
    blk = lambda off: pl.BlockSpec((rb, wide), lambda h, i: (n_blk - 1 - i, off + h))
    head = blk(0)
    hs_spec = pl.BlockSpec((hps, rb, LANES), lambda h, i: (h, n_blk - 1 - i, 0))
    full = jax.ShapeDtypeStruct((s_len, DN_HEADS * LANES), F32)
    return pl.pallas_call(
        body, name="dn_scan_bwd", grid=(groups, n_blk),
        in_specs=[blk(0), blk(groups), blk(2 * groups), head, head,
                  pl.BlockSpec((hps, rb, DN_CHUNK), lambda h, i: (h, n_blk - 1 - i, 0)), hs_spec,
                  pl.BlockSpec((hps, n_chunk, LANES, LANES), lambda h, i: (h, n_blk - 1 - i, 0, 0)), head],
        out_specs=[head, head, head, hs_spec],
        out_shape=[full, full, full, jax.ShapeDtypeStruct((DN_HEADS, s_len, LANES), F32)],
        scratch_shapes=[pltpu.VMEM((hps, LANES, LANES), F32)],
        compiler_params=_cparams(("parallel", "arbitrary")),
    )(qkv, qkv, qkv, u, w, t_inv, hs, states, d_o)


def dn_post(o, z, out_norm):
    def body(i, ins, outs, accs):
        gn = ins[2][...]
        for h in range(DN_HEADS):
            cols = slice(h * LANES, (h + 1) * LANES)
            xhat, _ = _rms_stats(ins[0][:, cols])
            outs[0][:, cols] = (xhat * gn * _silu(ins[1][:, cols])).astype(BF16)

    return row_call("dn_post", body, o.shape[0], ROW_TILE, [(o, "tile"), (z, "tile"), (out_norm, "full")],
                    [(o.shape, BF16)])[0]


def dn_post_bwd(d_og, o, z, out_norm):
    def body(i, ins, outs, accs):
        gn = ins[3][...]
        dgn = jnp.zeros((1, LANES), F32)
        for h in range(DN_HEADS):
            cols = slice(h * LANES, (h + 1) * LANES)
            dy, zh = ins[0][:, cols].astype(F32), ins[2][:, cols]
            xhat, r = _rms_stats(ins[1][:, cols])
            sz = _silu(zh)
            dgn = dgn + jnp.sum(dy * xhat * sz, axis=0, keepdims=True)
            outs[1][:, cols] = (dy * xhat * gn * _silu_grad(zh)).astype(BF16)
            dxhat = dy * gn * sz
            outs[0][:, cols] = r * (dxhat - xhat * jnp.mean(dxhat * xhat, axis=-1, keepdims=True))
        accs[0][...] += dgn

    return row_call("dn_post_bwd", body, o.shape[0], ROW_TILE,
                    [(d_og, "tile"), (o, "tile"), (z, "tile"), (out_norm, "full")],
                    [(o.shape, F32), (o.shape, BF16)], [(1, LANES)])


def dn_pre_bwd(dq, dk, dv, dhs, qkv_raw, ba, w_conv, gate):
    s_len = qkv_raw.shape[0]
    tm = min(DN_ROW_TILE, s_len)

    def body(i, ins, outs, accs):
        dq_ref, dk_ref, dv_ref, dhs_ref, x_ref, xp_ref, ba_ref, w_ref, gate_ref = ins
        dc_ref, dba_ref = outs

        def blk(cb, carry):
            cols = pl.ds(pl.multiple_of(cb * LANES, LANES), LANES)
            hcols = pl.ds(pl.multiple_of((cb & (DN_HEADS - 1)) * LANES, LANES), LANES)
            prev = jnp.where(i > 0, xp_ref[:, cols], 0.0)
            xcat = jnp.concatenate([prev, x_ref[:, cols]], axis=0)
            c = _conv_taps(xcat, w_ref, cols, DN_CONV, DN_HALO, tm)
            y = _silu(c)
            dy = jnp.where(cb < DN_HEADS, dq_ref[:, hcols],
                           jnp.where(cb < 2 * DN_HEADS, dk_ref[:, hcols], dv_ref[:, hcols]))
            rs = lax.rsqrt(jnp.sum(y * y, axis=-1, keepdims=True) + L2_EPS)
            fac = jnp.where(cb < DN_HEADS, DN_HEAD_DIM ** -0.5, 1.0)
            nrm = y * rs
            dn = dy * fac
            dy_norm = rs * (dn - nrm * jnp.sum(dn * nrm, axis=-1, keepdims=True))
            dc_ref[:, cols] = jnp.where(cb < 2 * DN_HEADS, dy_norm, dy) * _silu_grad(c)
            return carry

        lax.fori_loop(0, qkv_raw.shape[1] // LANES, blk, 0)

        lane = _iota((tm, LANES), 1)
        dbeta = jnp.zeros((tm, LANES), F32)
        dg = jnp.zeros((tm, LANES), F32)
        for h in range(DN_HEADS):
            dbeta = dbeta + jnp.where(lane == h, dhs_ref[h, :, 0:1], 0.0)
            dg = dg + jnp.where(lane == DN_HEADS + h, dhs_ref[h, :, 1:2], 0.0)
        bav = ba_ref[...]
        beta = _sigmoid(bav)
        ea = jnp.exp(gate_ref[0:1, :])
        pre = bav + gate_ref[1:2, :]
        g = -ea * _softplus(pre)
        da = dg * (-ea) * _sigmoid(pre)
        dba_ref[...] = (dbeta * beta * (1.0 - beta) + da).astype(BF16)
        rid = _iota((8, LANES), 0)
        accs[0][...] += (jnp.where(rid == 0, jnp.sum(dg * g, axis=0, keepdims=True), 0.0)
                         + jnp.where(rid == 1, jnp.sum(da, axis=0, keepdims=True), 0.0))

    return row_call("dn_pre_bwd", body, s_len, tm,
                    [(dq, "tile"), (dk, "tile"), (dv, "tile"), (dhs, "tile"), (qkv_raw, "tile"),
                     (qkv_raw, ("prev", DN_HALO)), (ba, "tile"), (w_conv, "full"), (gate, "full")],
                    [(qkv_raw.shape, F32), (ba.shape, BF16)], [(8, LANES)])


def dn_conv_bwd(dc, qkv_raw, w_conv):
    s_len = dc.shape[0]
    tm = min(DN_ROW_TILE, s_len)
    nt = s_len // tm

    def body(i, ins, outs, accs):
        dc_ref, dn_ref, x_ref, xp_ref, w_ref = ins

        def blk(cb, carry):
            cols = pl.ds(pl.multiple_of(cb * LANES, LANES), LANES)
            dy = dc_ref[:, cols]
            nxt = jnp.where(i < nt - 1, dn_ref[:, cols], 0.0)
            dcat = jnp.concatenate([dy, nxt], axis=0)
            outs[0][:, cols] = _conv_taps_bwd_x(dcat, w_ref, cols, DN_CONV, DN_HALO, tm).astype(BF16)
            prev = jnp.where(i > 0, xp_ref[:, cols], 0.0)
            xcat = jnp.concatenate([prev, x_ref[:, cols]], axis=0)
            accs[0][:, cols] += _conv_taps_bwd_w(dy, xcat, DN_CONV, DN_HALO, tm, 8)
            return carry

        lax.fori_loop(0, dc.shape[1] // LANES, blk, 0)

    return row_call("dn_conv_bwd", body, s_len, tm,
                    [(dc, "tile"), (dc, ("next", DN_HALO)), (qkv_raw, "tile"), (qkv_raw, ("prev", DN_HALO)),
                     (w_conv, "full")],
                    [(dc.shape, BF16)], [(8, dc.shape[1])])


def _glu(u_ref, cols, d):
    return u_ref[:, cols] * _sigmoid(u_ref[:, pl.ds(pl.multiple_of(d + cols.start, LANES), cols.size)])


def cv_core_fwd(u, w_dw, b_dw, ln_g, ln_b):
    s_len, d = u.shape[0], u.shape[1] // 2
    tm = min(CONV_ROW_TILE, s_len)

    def body(i, ins, outs, accs):
        u_ref, up_ref, w_ref, bdw_ref, g_ref, b_ref = ins
        s_ref, c_ref = outs

        def blk(cb, carry):
            cols = pl.ds(pl.multiple_of(cb * LANES, LANES), LANES)
            prev = jnp.where(i > 0, _glu(up_ref, cols, d), 0.0)
            xcat = jnp.concatenate([prev, _glu(u_ref, cols, d)], axis=0)
            c_ref[:, cols] = _conv_taps(xcat, w_ref, cols, CV_WIDTH, CV_HALO, tm) + bdw_ref[:, cols]
            return carry

        lax.fori_loop(0, d // LANES, blk, 0)
        c = c_ref[...]
        mu = jnp.mean(c, axis=-1, keepdims=True)
        xc = c - mu
        rstd = lax.rsqrt(jnp.mean(xc * xc, axis=-1, keepdims=True) + LN_EPS)
        s_ref[...] = _silu(xc * rstd * g_ref[...] + b_ref[...]).astype(BF16)

    return row_call("cv_core_fwd", body, s_len, tm,
                    [(u, "tile"), (u, ("prev", CV_HALO)), (w_dw, "full"), (b_dw, "full"), (ln_g, "full"),
                     (ln_b, "full")],
                    [((s_len, d), BF16), ((s_len, d), F32)])


def cv_ln_bwd(ds, c, ln_g, ln_b):
    def body(i, ins, outs, accs):
        cv, g = ins[1][...], ins[2][...]
        mu = jnp.mean(cv, axis=-1, keepdims=True)
        xc = cv - mu
        rstd = lax.rsqrt(jnp.mean(xc * xc, axis=-1, keepdims=True) + LN_EPS)
        xhat = xc * rstd
        dl = ins[0][...].astype(F32) * _silu_grad(xhat * g + ins[3][...])
        dxhat = dl * g
        dc = rstd * (dxhat - jnp.mean(dxhat, axis=-1, keepdims=True)
                     - xhat * jnp.mean(dxhat * xhat, axis=-1, keepdims=True))
        outs[0][...] = dc
        rid = _iota((8, cv.shape[1]), 0)
        accs[0][...] += (jnp.where(rid == 0, jnp.sum(dl * xhat, axis=0, keepdims=True), 0.0)
                         + jnp.where(rid == 1, jnp.sum(dl, axis=0, keepdims=True), 0.0)
                         + jnp.where(rid == 2, jnp.sum(dc, axis=0, keepdims=True), 0.0))

    return row_call("cv_ln_bwd", body, c.shape[0], ROW_TILE,
                    [(ds, "tile"), (c, "tile"), (ln_g, "full"), (ln_b, "full")], [(c.shape, F32)], [(8, c.shape[1])])


def cv_conv_bwd(dc, u, w_dw):
    s_len, d = dc.shape
    tm = min(CONV_ROW_TILE, s_len)
    nt = s_len // tm

    def body(i, ins, outs, accs):
        dc_ref, dn_ref, u_ref, up_ref, w_ref = ins

        def blk(cb, carry):
            cols = pl.ds(pl.multiple_of(cb * LANES, LANES), LANES)
            gcols = pl.ds(pl.multiple_of(d + cb * LANES, LANES), LANES)
            dy = dc_ref[:, cols]
            nxt = jnp.where(i < nt - 1, dn_ref[:, cols], 0.0)
            dgl = _conv_taps_bwd_x(jnp.concatenate([dy, nxt], axis=0), w_ref, cols, CV_WIDTH, CV_HALO, tm)
            u1, sg = u_ref[:, cols], _sigmoid(u_ref[:, gcols])
            du1 = dgl * sg
            du2 = dgl * u1 * sg * (1.0 - sg)
            outs[0][:, cols] = du1.astype(BF16)
            outs[0][:, gcols] = du2.astype(BF16)
            accs[1][:, cols] += jnp.sum(du1, axis=0, keepdims=True)
            accs[1][:, gcols] += jnp.sum(du2, axis=0, keepdims=True)
            prev = jnp.where(i > 0, _glu(up_ref, cols, d), 0.0)
            xcat = jnp.concatenate([prev, u1 * sg], axis=0)
            accs[0][:, cols] += _conv_taps_bwd_w(dy, xcat, CV_WIDTH, CV_HALO, tm, CV_HALO)
            return carry

        lax.fori_loop(0, d // LANES, blk, 0)

    return row_call("cv_conv_bwd", body, s_len, tm,
                    [(dc, "tile"), (dc, ("next", CV_HALO)), (u, "tile"), (u, ("prev", CV_HALO)), (w_dw, "full")],
                    [(u.shape, BF16)], [(CV_HALO, d), (1, 2 * d)])


def xa_core_fwd(name, q, kv):
    d = q.shape[1]

    def body(i, ins, outs, accs):
        for h in range(XA_HEADS):
            cols = slice(h * XA_HEAD_DIM, (h + 1) * XA_HEAD_DIM)
            vcols = slice(d + h * XA_HEAD_DIM, d + (h + 1) * XA_HEAD_DIM)
            s = _dot_nt(ins[0][:, cols], ins[1][:, cols]) * (XA_HEAD_DIM ** -0.5)
            e = jnp.exp(s - jnp.max(s, axis=-1, keepdims=True))
            p = e / jnp.sum(e, axis=-1, keepdims=True)
            outs[0][:, cols] = _dot(p, ins[1][:, vcols]).astype(BF16)

    return row_call(name, body, q.shape[0], ROW_TILE, [(q, "tile"), (kv, "full")], [(q.shape, BF16)])[0]


def xa_core_bwd(name, d_o, q, kv):
    d = q.shape[1]

    def body(i, ins, outs, accs):
        for h in range(XA_HEADS):
            cols = slice(h * XA_HEAD_DIM, (h + 1) * XA_HEAD_DIM)
            vcols = slice(d + h * XA_HEAD_DIM, d + (h + 1) * XA_HEAD_DIM)
            qh, kh, vh, doh = ins[1][:, cols], ins[2][:, cols], ins[2][:, vcols], ins[0][:, cols]
            s = _dot_nt(qh, kh) * (XA_HEAD_DIM ** -0.5)
            e = jnp.exp(s - jnp.max(s, axis=-1, keepdims=True))
            p = e / jnp.sum(e, axis=-1, keepdims=True)
            dp = _dot_nt(doh, vh)
            ds = p * (dp - jnp.sum(dp * p, axis=-1, keepdims=True)) * (XA_HEAD_DIM ** -0.5)
            outs[0][:, cols] = _dot(ds, kh).astype(BF16)
            accs[0][:, cols] += _dot_tn(ds, qh)
            accs[0][:, vcols] += _dot_tn(p, doh)

    return row_call(name, body, q.shape[0], ROW_TILE, [(d_o, "tile"), (q, "tile"), (kv, "full")],
                    [(q.shape, BF16)], [kv.shape])


def adamw(name, w, g, m, v):
    def body(i, ins, outs, accs):
        wv, gv = ins[0][...], ins[1][...]
        mn = ADAM_B1 * ins[2][...] + (1.0 - ADAM_B1) * gv
        vn = ADAM_B2 * ins[3][...] + (1.0 - ADAM_B2) * jnp.square(gv)
        m_hat = mn / (1.0 - ADAM_B1 ** ADAM_STEP)
        v_hat = vn / (1.0 - ADAM_B2 ** ADAM_STEP)
        outs[0][...] = -ADAM_LR * (m_hat / (jnp.sqrt(v_hat) + ADAM_EPS) + ADAM_WD * wv)
        outs[1][...] = mn
        outs[2][...] = vn

    return row_call(name, body, w.shape[0], ROW_TILE, [(w, "tile"), (g, "tile"), (m, "tile"), (v, "tile")],
                    [(w.shape, F32)] * 3)


HBM_SPEC = pl.BlockSpec(memory_space=pltpu.HBM)


def _position():
    return lax.axis_index("x"), lax.axis_index("y"), lax.axis_index("c")


def _other_chips(x, y):
    return [(1 - x, y), (x, 1 - y), (1 - x, 1 - y)]


def _row_chunks(rows):
    return rows // D2D_CHUNK_ROWS if rows % D2D_CHUNK_ROWS == 0 else 1


def _start_chunked(make, rows):
    k = _row_chunks(rows)
    for i in range(k):
        make(i * (rows // k), rows // k).start()


def gather_shards(packs):
    n = len(packs)

    def body(*refs):
        srcs, outs = refs[:n], refs[n:2 * n]
        send_sems, recv_sems, local_sems = refs[2 * n:]
        x, y, c = _position()
        me = 2 * x + y
        chips = _other_chips(x, y)
        sibling = (x, y, 1 - c)

        def local(a, off, size):
            return pltpu.make_async_copy(srcs[a].at[pl.ds(off, size), :], outs[a].at[me, pl.ds(off, size), :],
                                         local_sems.at[a])

        def over_ici(a, j):
            px, py = chips[j]
            rows = srcs[a].shape[0] // 2
            return pltpu.make_async_remote_copy(
                src_ref=srcs[a].at[pl.ds(c * rows, rows), :], dst_ref=outs[a].at[me, pl.ds(c * rows, rows), :],
                send_sem=send_sems.at[a, j], recv_sem=recv_sems.at[a, j], device_id=(px, py, c), device_id_type=MESH)

        def landed(a, j):
            px, py = chips[j]
            rows = srcs[a].shape[0] // 2
            part = outs[a].at[2 * px + py, pl.ds(c * rows, rows), :]
            return pltpu.make_async_remote_copy(
                src_ref=part, dst_ref=part, send_sem=send_sems.at[a, j], recv_sem=recv_sems.at[a, j],
                device_id=(px, py, c), device_id_type=MESH)

        def over_d2d(a, j, cc, off, size):
            px, py = chips[j]
            rows = srcs[a].shape[0] // 2
            part = outs[a].at[2 * px + py, pl.ds(cc * rows + off, size), :]
            return pltpu.make_async_remote_copy(
                src_ref=part, dst_ref=part, send_sem=send_sems.at[a, 3 + j], recv_sem=recv_sems.at[a, 3 + j],
                device_id=sibling, device_id_type=MESH)

        for a in range(n):
            _start_chunked(functools.partial(local, a), srcs[a].shape[0])
            for j in range(3):
                over_ici(a, j).start()
        for a in range(n):
            for j in range(3):
                landed(a, j).wait_recv()
                _start_chunked(functools.partial(over_d2d, a, j, c), srcs[a].shape[0] // 2)
        for a in range(n):
            rows = srcs[a].shape[0] // 2
            for j in range(3):
                over_d2d(a, j, 1 - c, 0, rows).wait_recv()
                over_d2d(a, j, c, 0, rows).wait_send()
                over_ici(a, j).wait_send()
            local(a, 0, srcs[a].shape[0]).wait()

    return pl.pallas_call(
        body, name="gather_shards",
        in_specs=[HBM_SPEC] * n, out_specs=[HBM_SPEC] * n,
        out_shape=[jax.ShapeDtypeStruct((N_CHIPS,) + p.shape, p.dtype) for p in packs],
        scratch_shapes=[pltpu.SemaphoreType.DMA((n, 6)), pltpu.SemaphoreType.DMA((n, 6)),
                        pltpu.SemaphoreType.DMA((n,))],
    )(*packs)


def pair_split(packs):
    n = len(packs)

    def body(*refs):
        srcs, outs = refs[:n], refs[n:3 * n]
        send_sems, recv_sems, local_sems = refs[3 * n:]
        x, y, c = _position()

        def local(a, off, size):
            rows = srcs[a].shape[1] // 2
            return pltpu.make_async_copy(srcs[a].at[:, pl.ds(c * rows + off, size), :],
                                         outs[2 * a].at[:, pl.ds(off, size), :], local_sems.at[a])

        def remote(a, off, size):
            rows = srcs[a].shape[1] // 2
            return pltpu.make_async_remote_copy(
                src_ref=srcs[a].at[:, pl.ds((1 - c) * rows + off, size), :],
                dst_ref=outs[2 * a + 1].at[:, pl.ds(off, size), :],
                send_sem=send_sems.at[a], recv_sem=recv_sems.at[a], device_id=(x, y, 1 - c), device_id_type=MESH)

        for a in range(n):
            rows = srcs[a].shape[1] // 2
            _start_chunked(functools.partial(remote, a), rows)
            _start_chunked(functools.partial(local, a), rows)
        for a in range(n):
            rows = srcs[a].shape[1] // 2
            remote(a, 0, rows).wait()
            local(a, 0, rows).wait()

    out_shape = []
    for p in packs:
        half = jax.ShapeDtypeStruct((p.shape[0], p.shape[1] // 2, p.shape[2]), p.dtype)
        out_shape += [half, half]
    res = pl.pallas_call(
        body, name="pair_split", in_specs=[HBM_SPEC] * n, out_specs=[HBM_SPEC] * (2 * n), out_shape=out_shape,
        scratch_shapes=[pltpu.SemaphoreType.DMA((n,)), pltpu.SemaphoreType.DMA((n,)), pltpu.SemaphoreType.DMA((n,))],
    )(*packs)
    return [(res[2 * a], res[2 * a + 1]) for a in range(n)]


def chip_scatter(packs):
    n = len(packs)

    def body(*refs):
        srcs, outs = refs[:n], refs[n:3 * n]
        send_sems, recv_sems, local_sems = refs[3 * n:]
        x, y, c = _position()
        me = 2 * x + y

        def local(a, off, size):
            return pltpu.make_async_copy(srcs[a].at[me, pl.ds(off, size), :], outs[2 * a].at[pl.ds(off, size), :],
                                         local_sems.at[a])

        copies = []
        for a in range(n):
            for j, (px, py) in enumerate(_other_chips(x, y)):
                cp = pltpu.make_async_remote_copy(
                    src_ref=srcs[a].at[2 * px + py], dst_ref=outs[2 * a + 1].at[j],
                    send_sem=send_sems.at[a, j], recv_sem=recv_sems.at[a, j],
                    device_id=(px, py, c), device_id_type=MESH)
                cp.start()
                copies.append(cp)
            _start_chunked(functools.partial(local, a), srcs[a].shape[1])
        for cp in copies:
            cp.wait()
        for a in range(n):
            local(a, 0, srcs[a].shape[1]).wait()

    out_shape = []
    for p in packs:
        out_shape += [jax.ShapeDtypeStruct(p.shape[1:], p.dtype),
                      jax.ShapeDtypeStruct((N_CHIPS - 1,) + p.shape[1:], p.dtype)]
    res = pl.pallas_call(
        body, name="chip_scatter", in_specs=[HBM_SPEC] * n, out_specs=[HBM_SPEC] * (2 * n), out_shape=out_shape,
        scratch_shapes=[pltpu.SemaphoreType.DMA((n, 3)), pltpu.SemaphoreType.DMA((n, 3)),
                        pltpu.SemaphoreType.DMA((n,))],
    )(*packs)
    return [(res[2 * a], res[2 * a + 1]) for a in range(n)]


def pair_join(halves):
    n = len(halves)

    def body(*refs):
        srcs, outs = refs[:n], refs[n:2 * n]
        send_sems, recv_sems, local_sems = refs[2 * n:]
        x, y, c = _position()

        def local(a, off, size):
            return pltpu.make_async_copy(srcs[a].at[pl.ds(off, size), :], outs[a].at[c, pl.ds(off, size), :],
                                         local_sems.at[a])

        def remote(a, off, size):
            return pltpu.make_async_remote_copy(
                src_ref=srcs[a].at[pl.ds(off, size), :], dst_ref=outs[a].at[c, pl.ds(off, size), :],
                send_sem=send_sems.at[a], recv_sem=recv_sems.at[a], device_id=(x, y, 1 - c), device_id_type=MESH)

        for a in range(n):
            _start_chunked(functools.partial(remote, a), srcs[a].shape[0])
            _start_chunked(functools.partial(local, a), srcs[a].shape[0])
        for a in range(n):
            remote(a, 0, srcs[a].shape[0]).wait()
            local(a, 0, srcs[a].shape[0]).wait()

    return pl.pallas_call(
        body, name="pair_join", in_specs=[HBM_SPEC] * n, out_specs=[HBM_SPEC] * n,
        out_shape=[jax.ShapeDtypeStruct((2,) + p.shape, p.dtype) for p in halves],
        scratch_shapes=[pltpu.SemaphoreType.DMA((n,)), pltpu.SemaphoreType.DMA((n,)), pltpu.SemaphoreType.DMA((n,))],
    )(*halves)


def all_sum_small(part):
    n_dev = 8
    rows = part.shape[0]

    def body(src, out, buf, send_sems, recv_sems):
        x, y, c = _position()
        me = 4 * x + 2 * y + c
        buf[me] = src[...]
        copies = []
        for k in range(1, n_dev):
            px, py, pc = x ^ ((k >> 2) & 1), y ^ ((k >> 1) & 1), c ^ (k & 1)
            cp = pltpu.make_async_remote_copy(
                src_ref=src, dst_ref=buf.at[me], send_sem=send_sems.at[k - 1], recv_sem=recv_sems.at[k - 1],
                device_id=(px, py, pc), device_id_type=MESH)
            cp.start()
            copies.append(cp)
        for cp in copies:
            cp.wait()
        acc = buf[0]
        for k in range(1, n_dev):
            acc = acc + buf[k]
        out[...] = acc

    return pl.pallas_call(
        body, name="all_sum_small",
        in_specs=[pl.BlockSpec(memory_space=pltpu.VMEM)], out_specs=pl.BlockSpec(memory_space=pltpu.VMEM),
        out_shape=jax.ShapeDtypeStruct(part.shape, F32),
        scratch_shapes=[pltpu.VMEM((n_dev, rows, part.shape[1]), F32),
                        pltpu.SemaphoreType.DMA((n_dev - 1,)), pltpu.SemaphoreType.DMA((n_dev - 1,))],
    )(part)


def add_pairs(name, a, b, out_dtype):
    shape = a.shape
    a2, b2 = a.reshape(-1, shape[-1]), b.reshape(-1, shape[-1])

    def body(i, ins, outs, accs):
        outs[0][...] = (ins[0][...].astype(F32) + ins[1][...].astype(F32)).astype(out_dtype)

    return row_call(name, body, a2.shape[0], ROW_TILE, [(a2, "tile"), (b2, "tile")],
                    [(a2.shape, out_dtype)])[0].reshape(shape)


def add_four(name, mine, theirs):
    def body(i, ins, outs, accs):
        acc = ins[0][...].astype(F32)
        for j in range(N_CHIPS - 1):
            acc = acc + ins[1][j].astype(F32)
        outs[0][...] = acc

    return row_call(name, body, mine.shape[0], ROW_TILE, [(mine, "tile"), (theirs, "tile")], [(mine.shape, F32)])[0]


PACK_COLS = 1024
BIG_ROW_MULTIPLE = 512
SMALL_ROW_MULTIPLE = 32
BIG = ["dn_w_in", "dn_w_out", "cv_w_pw1", "cv_w_pw2", "xa_w_q", "xa_w_kv", "xa_w_o", "mlp_w_up", "mlp_w_down"]
SMALL = ["dn_w_conv", "cv_norm", "cv_b_pw1", "cv_w_dw", "cv_b_dw", "cv_ln_g", "cv_ln_b", "cv_b_pw2"]
SHARD_AXIS = {"dn_w_in": 2, "dn_w_conv": 2, "dn_w_out": 1, "cv_norm": 1, "cv_w_pw1": 2, "cv_b_pw1": 1,
              "cv_w_dw": 2, "cv_b_dw": 1, "cv_ln_g": 1, "cv_ln_b": 1, "cv_w_pw2": 1, "cv_b_pw2": 1,
              "xa_w_q": 1, "xa_w_kv": 2, "xa_w_o": 1, "mlp_w_up": 2, "mlp_w_down": 1}
REPLICATED = ["dn_norm", "dn_a_log", "dn_dt_bias", "dn_out_norm", "xa_norm", "xa_mem_norm", "mlp_norm", "final_norm"]


def _pack_rows(size):
    return -(-size // PACK_COLS)


SHARD_SHAPES = {
    "dn_w_in": (1, 1024, 1028), "dn_w_conv": (1, 4, 768), "dn_w_out": (1, 256, 1024), "cv_norm": (1, 256),
    "cv_w_pw1": (1, 1024, 512), "cv_b_pw1": (1, 512), "cv_w_dw": (1, 31, 256), "cv_b_dw": (1, 256),
    "cv_ln_g": (1, 256), "cv_ln_b": (1, 256), "cv_w_pw2": (1, 256, 1024), "cv_b_pw2": (1, 256),
    "xa_w_q": (2, 256, 1024), "xa_w_kv": (2, 1024, 512), "xa_w_o": (2, 256, 1024),
    "mlp_w_up": (2, 1024, 1024), "mlp_w_down": (2, 1024, 1024)}


def _shard_shape(nm):
    return SHARD_SHAPES[nm]


def _pack(tensors, names, dtype, row_multiple):
    pieces = []
    for nm in names:
        t = tensors[nm]
        flat = t.reshape(t.shape[0], -1) if t.ndim > len(_shard_shape(nm)) else t.reshape(1, -1)
        pad = _pack_rows(flat.shape[1]) * PACK_COLS - flat.shape[1]
        pieces.append(jnp.pad(flat.astype(dtype), ((0, 0), (0, pad))))
    cat = jnp.concatenate(pieces, axis=1)
    rows = cat.shape[1] // PACK_COLS
    total = -(-rows // row_multiple) * row_multiple
    cat = jnp.pad(cat, ((0, 0), (0, (total - rows) * PACK_COLS)))
    return cat.reshape(cat.shape[0], total, PACK_COLS)


def _unpack(pack, names):
    lead = pack.shape[:-2]
    flat = pack.reshape(lead + (-1,))
    out, off = {}, 0
    for nm in names:
        shp = _shard_shape(nm)
        size = 1
        for s in shp:
            size *= s
        out[nm] = flat[..., off:off + size].reshape(lead + shp)
        off += _pack_rows(size) * PACK_COLS
    return out


def _to_full(nm, stacked):
    ax = SHARD_AXIS[nm]
    moved = jnp.moveaxis(stacked, 0, ax)
    shp = list(_shard_shape(nm))
    shp[ax] *= N_CHIPS
    return moved.reshape(shp)


def _to_shards(nm, full):
    ax = SHARD_AXIS[nm]
    shp = list(_shard_shape(nm))
    split = full.reshape(shp[:ax] + [N_CHIPS, shp[ax]] + shp[ax + 1:])
    return jnp.moveaxis(split, ax, 0)


def _row(v):
    return v.reshape(1, -1)


def mlp_fwd(tag, h, g, w_up, w_down):
    n = rms_fwd(tag + "_norm", h, g)
    up = mm(tag + "_up", n, w_up)
    out = mm(tag + "_down", up, w_down, pro=lambda t: jnp.square(jnp.maximum(t, 0.0)),
             epi=lambda acc, res: acc + res, epi_tiles=(h,))
    return out, (n, up)


def mlp_bwd(tag, dh, h, g, w_up, w_down, saved):
    n, up = saved
    dup = mm(tag + "_d_act", dh, w_down, tb=True, out_dtype=BF16,
             epi=lambda acc, t: acc * (2.0 * jnp.maximum(t, 0.0)), epi_tiles=(up,))
    dw_down = mm(tag + "_dw_down", up, dh, ta=True, pro=lambda t: jnp.square(jnp.maximum(t, 0.0)), tk=512)
    dn = mm(tag + "_dn", dup, w_up, tb=True)
    dw_up = mm(tag + "_dw_up", n, dup, ta=True, tk=512)
    dh_in, dg = rms_bwd(tag + "_norm_bwd", dn, h, g, dh)
    return dh_in, dg, dw_up, dw_down


def xa_fwd(tag, h, mem, g, g_mem, w_q, w_kv, w_o):
    n = rms_fwd(tag + "_norm", h, g)
    mem_n = rms_fwd(tag + "_mem_norm", mem, g_mem)
    q = mm(tag + "_q", n, w_q, out_dtype=BF16)
    kv = mm(tag + "_kv", mem_n, w_kv, out_dtype=BF16)
    o = xa_core_fwd(tag + "_core", q, kv)
    out = mm(tag + "_o", o, w_o, epi=lambda acc, res: acc + res, epi_tiles=(h,))
    return out, (n, mem_n, q, kv, o)


def xa_bwd(tag, dh, h, mem, g, g_mem, w_q, w_kv, w_o, saved):
    n, mem_n, q, kv, o = saved
    d_o = mm(tag + "_d_o", dh, w_o, tb=True, out_dtype=BF16)
    dw_o = mm(tag + "_dw_o", o, dh, ta=True, tk=512)
    dq, dkv = xa_core_bwd(tag + "_core_bwd", d_o, q, kv)
    dn = mm(tag + "_dn", dq, w_q, tb=True)
    dw_q = mm(tag + "_dw_q", n, dq, ta=True, tk=512)
    dh_in, dg = rms_bwd(tag + "_norm_bwd", dn, h, g, dh)
    dw_kv = mm(tag + "_dw_kv", mem_n, dkv, ta=True)
    dmem_n = mm(tag + "_dmem", dkv, w_kv, tb=True)
    dg_mem = mem_norm_bwd(tag + "_mem_norm_bwd", dmem_n, mem, g_mem)
    return dh_in, dg, dg_mem, dw_q, dw_kv, dw_o


def _gate_tile(a_log, dt_bias):
    t = jnp.zeros((8, LANES), F32)
    t = t.at[0, DN_HEADS:2 * DN_HEADS].set(a_log.reshape(-1))
    return t.at[1, DN_HEADS:2 * DN_HEADS].set(dt_bias.reshape(-1))


def dn_fwd(h, g, w_qkv, w_z, w_ba, w_conv, gate, out_norm, w_out):
    n = rms_fwd("dn_norm", h, g)
    qkv_raw = mm("dn_proj_qkv", n, w_qkv)
    z = mm("dn_proj_z", n, w_z)
    ba = mm("dn_proj_ba", n, w_ba)
    qkv, hs = dn_pre(qkv_raw, ba, w_conv, gate)
    u, w, t_inv = dn_solve(qkv, hs)
    o, states = dn_scan_fwd(qkv, u, w, hs)
    og = dn_post(o, z, out_norm)
    out = mm("dn_out", og, w_out, epi=lambda acc, res: acc + res, epi_tiles=(h,))
    return out, (n, qkv_raw, z, ba, qkv, hs, u, w, t_inv, o, states, og)


def dn_bwd(dh, h, g, w_qkv, w_z, w_ba, w_conv, gate, out_norm, w_out, saved):
    n, qkv_raw, z, ba, qkv, hs, u, w, t_inv, o, states, og = saved
    d_og = mm("dn_d_og", dh, w_out, tb=True, out_dtype=BF16)
    dw_out = mm("dn_dw_out", og, dh, ta=True, tk=512)
    d_o, dz, d_out_norm = dn_post_bwd(d_og, o, z, out_norm)
    dq, dk, dv, dhs = dn_scan_bwd(qkv, u, w, t_inv, hs, states, d_o)
    dc, dba, d_gate = dn_pre_bwd(dq, dk, dv, dhs, qkv_raw, ba, w_conv, gate)
    dqkv_raw, dw_conv = dn_conv_bwd(dc, qkv_raw, w_conv)
    dn = mm("dn_dn_qkv", dqkv_raw, w_qkv, tb=True)
    dn = mm("dn_dn_z", dz, w_z, tb=True, epi=lambda acc, t: acc + t, epi_tiles=(dn,))
    dn = mm("dn_dn_ba", dba, w_ba, tb=True, epi=lambda acc, t: acc + t, epi_tiles=(dn,))
    dw_qkv = mm("dn_dw_qkv", n, dqkv_raw, ta=True, tk=512)
    dw_z = mm("dn_dw_z", n, dz, ta=True, tk=512)
    dw_ba = mm("dn_dw_ba", n, dba, ta=True, tk=512)
    dh_in, dg = rms_bwd("dn_norm_bwd", dn, h, g, dh)
    return dh_in, dg, dw_qkv, dw_z, dw_ba, dw_conv, d_gate, d_out_norm, dw_out


def cv_fwd(h, g, w_pw1, b_pw1, w_dw, b_dw, ln_g, ln_b, w_pw2, b_pw2):
    n = rms_fwd("cv_norm", h, g)
    u = mm("cv_pw1", n, w_pw1, epi=lambda acc, b: acc + b, epi_rows=(b_pw1,))
    s, c = cv_core_fwd(u, w_dw, b_dw, ln_g, ln_b)
    out = mm("cv_pw2", s, w_pw2, epi=lambda acc, res, b: acc + res + b, epi_tiles=(h,), epi_rows=(b_pw2,))
    return out, (n, u, s, c)


def cv_bwd(dh, h, g, w_pw1, w_dw, ln_g, ln_b, w_pw2, saved):
    n, u, s, c = saved
    ds = mm("cv_d_s", dh, w_pw2, tb=True, out_dtype=BF16)
    dw_pw2 = mm("cv_dw_pw2", s, dh, ta=True, tk=512)
    db_pw2 = col_sum("cv_db_pw2", dh)
    dc, ln_acc = cv_ln_bwd(ds, c, ln_g, ln_b)
    du, dw_dw, db_pw1 = cv_conv_bwd(dc, u, w_dw)
    dn = mm("cv_dn", du, w_pw1, tb=True)
    dw_pw1 = mm("cv_dw_pw1", n, du, ta=True, tk=512)
    dh_in, dg = rms_bwd("cv_norm_bwd", dn, h, g, dh)
    return dh_in, dg, dw_pw1, db_pw1, dw_dw, ln_acc, dw_pw2, db_pw2


WEIGHTS = ["dn_norm", "dn_w_in", "dn_w_conv", "dn_a_log", "dn_dt_bias", "dn_out_norm", "dn_w_out", "cv_norm",
           "cv_w_pw1", "cv_b_pw1", "cv_w_dw", "cv_b_dw", "cv_ln_g", "cv_ln_b", "cv_w_pw2", "cv_b_pw2", "xa_norm",
           "xa_mem_norm", "xa_w_q", "xa_w_kv", "xa_w_o", "mlp_norm", "mlp_w_up", "mlp_w_down", "final_norm"]


def _as_2d(t):
    if t.ndim == 1:
        return t.reshape(1, -1)
    return t.reshape(-1, t.shape[-1])


def kernel(x, mem, dn_norm, dn_w_in, dn_w_conv, dn_a_log, dn_dt_bias, dn_out_norm, dn_w_out, cv_norm, cv_w_pw1, cv_b_pw1, cv_w_dw, cv_b_dw, cv_ln_g, cv_ln_b, cv_w_pw2, cv_b_pw2, xa_norm, xa_mem_norm, xa_w_q, xa_w_kv, xa_w_o, mlp_norm, mlp_w_up, mlp_w_down, final_norm, loss_target, m_dn_norm, m_dn_w_in, m_dn_w_conv, m_dn_a_log, m_dn_dt_bias, m_dn_out_norm, m_dn_w_out, m_cv_norm, m_cv_w_pw1, m_cv_b_pw1, m_cv_w_dw, m_cv_b_dw, m_cv_ln_g, m_cv_ln_b, m_cv_w_pw2, m_cv_b_pw2, m_xa_norm, m_xa_mem_norm, m_xa_w_q, m_xa_w_kv, m_xa_w_o, m_mlp_norm, m_mlp_w_up, m_mlp_w_down, m_final_norm, v_dn_norm, v_dn_w_in, v_dn_w_conv, v_dn_a_log, v_dn_dt_bias, v_dn_out_norm, v_dn_w_out, v_cv_norm, v_cv_w_pw1, v_cv_b_pw1, v_cv_w_dw, v_cv_b_dw, v_cv_ln_g, v_cv_ln_b, v_cv_w_pw2, v_cv_b_pw2, v_xa_norm, v_xa_mem_norm, v_xa_w_q, v_xa_w_kv, v_xa_w_o, v_mlp_norm, v_mlp_w_up, v_mlp_w_down, v_final_norm):
    args = dict(locals())
    wts = {nm: args[nm] for nm in WEIGHTS}
    mom = {nm: args["m_" + nm] for nm in WEIGHTS}
    var = {nm: args["v_" + nm] for nm in WEIGHTS}
    big_pack = _pack(wts, BIG, BF16, BIG_ROW_MULTIPLE)[0]
    small_pack = _pack(wts, SMALL, F32, SMALL_ROW_MULTIPLE)[0]
    big_all, small_all = gather_shards([big_pack, small_pack])
    full = {nm: _to_full(nm, t) for nm, t in _unpack(big_all, BIG).items()}
    full.update({nm: _to_full(nm, t) for nm, t in _unpack(small_all, SMALL).items()})
    full.update({nm: wts[nm] for nm in REPLICATED})

    dh, grads, rep = local_step(x[0], mem[0], loss_target[0], full)

    shards = {nm: _to_shards(nm, grads[nm]) for nm in BIG + SMALL}
    big_g = _pack(shards, BIG, BF16, BIG_ROW_MULTIPLE)
    small_g = _pack(shards, SMALL, F32, SMALL_ROW_MULTIPLE)
    (big_mine, big_theirs), (small_mine, small_theirs) = pair_split([big_g, small_g])
    big_pair = add_pairs("pair_add_big", big_mine, big_theirs, BF16)
    small_pair = add_pairs("pair_add_small", small_mine, small_theirs, F32)
    (bm, bt), (sm, st) = chip_scatter([big_pair, small_pair])
    big_half = add_four("chip_add_big", bm, bt)
    small_half = add_four("chip_add_small", sm, st)
    big_red, small_red = pair_join([big_half, small_half])
    red = _unpack(big_red.reshape(-1, PACK_COLS), BIG)
    red.update(_unpack(small_red.reshape(-1, PACK_COLS), SMALL))

    rep = all_sum_small(rep)
    red["dn_norm"] = rep[0:1]
    red["dn_a_log"] = rep[1:2, DN_HEADS:2 * DN_HEADS]
    red["dn_dt_bias"] = rep[2:3, DN_HEADS:2 * DN_HEADS]
    red["dn_out_norm"] = rep[3:4, :LANES]
    red["xa_norm"], red["xa_mem_norm"], red["mlp_norm"] = rep[4:6], rep[6:8], rep[8:10]
    red["final_norm"] = rep[10]
    loss = rep[11, 0]

    delta, new_m, new_v = {}, {}, {}
    for nm in WEIGHTS:
        shp = wts[nm].shape
        res = adamw("adamw_" + nm, _as_2d(wts[nm]), _as_2d(red[nm].reshape(shp)), _as_2d(mom[nm]), _as_2d(var[nm]))
        delta[nm], new_m[nm], new_v[nm] = (r.reshape(shp) for r in res)
        red[nm] = red[nm].reshape(shp)

    grad_x = dh[None]
    return (loss, grad_x, *[red[nm] for nm in WEIGHTS], *[delta[nm] for nm in WEIGHTS],
            *[new_m[nm] for nm in WEIGHTS], *[new_v[nm] for nm in WEIGHTS])


def local_step(h0, mem0, target, full):
    d = h0.shape[1]
    dn_norm, dn_a_log, dn_dt_bias, dn_out_norm = (full[nm] for nm in REPLICATED[:4])
    xa_norm, xa_mem_norm, mlp_norm, final_norm = (full[nm] for nm in REPLICATED[4:])
    inner = DN_HEADS * DN_HEAD_DIM
    w_in = full["dn_w_in"][0]
    w_qkv, w_z = w_in[:, :3 * inner], w_in[:, 3 * inner:4 * inner]
    w_ba = jnp.pad(w_in[:, 4 * inner:], ((0, 0), (0, LANES - 2 * DN_HEADS)))
    w_conv = jnp.pad(full["dn_w_conv"][0], ((0, 8 - DN_CONV), (0, 0)))
    gate = _gate_tile(dn_a_log, dn_dt_bias)
    w_dw = jnp.pad(full["cv_w_dw"][0], ((0, CV_HALO - CV_WIDTH), (0, 0)))

    dn_args = (_row(dn_norm), w_qkv, w_z, w_ba, w_conv, gate, _row(dn_out_norm), full["dn_w_out"][0])
    h1, dn_saved = dn_fwd(h0, *dn_args)
    xa_args = [(_row(xa_norm[l]), _row(xa_mem_norm[l]), full["xa_w_q"][l], full["xa_w_kv"][l], full["xa_w_o"][l])
               for l in range(2)]
    mlp_args = [(_row(mlp_norm[l]), full["mlp_w_up"][l], full["mlp_w_down"][l]) for l in range(2)]
    h2, xa0_saved = xa_fwd("xa0", h1, mem0, *xa_args[0])
    h3, mlp0_saved = mlp_fwd("mlp0", h2, *mlp_args[0])
    cv_args = (_row(full["cv_norm"][0]), full["cv_w_pw1"][0], full["cv_b_pw1"], w_dw, full["cv_b_dw"],
               full["cv_ln_g"], full["cv_ln_b"], full["cv_w_pw2"][0], full["cv_b_pw2"])
    h4, cv_saved = cv_fwd(h3, *cv_args)
    h5, xa1_saved = xa_fwd("xa1", h4, mem0, *xa_args[1])
    h6, mlp1_saved = mlp_fwd("mlp1", h5, *mlp_args[1])

    dh, loss_tile, d_final = loss_head("loss_head", h6, _row(final_norm), target)
    grads = {}
    dg_mlp, dg_xa, dg_xa_mem = [None, None], [None, None], [None, None]
    dw_up, dw_down, dw_q, dw_kv, dw_o = ([None, None] for _ in range(5))
    dh, dg_mlp[1], dw_up[1], dw_down[1] = mlp_bwd("mlp1", dh, h5, *mlp_args[1], mlp1_saved)
    dh, dg_xa[1], dg_xa_mem[1], dw_q[1], dw_kv[1], dw_o[1] = xa_bwd("xa1", dh, h4, mem0, *xa_args[1], xa1_saved)
    (dh, grads["cv_norm"], grads["cv_w_pw1"], grads["cv_b_pw1"], dw_dw, ln_acc, grads["cv_w_pw2"],
     grads["cv_b_pw2"]) = cv_bwd(dh, h3, cv_args[0], cv_args[1], w_dw, cv_args[5], cv_args[6], cv_args[7], cv_saved)
    dh, dg_mlp[0], dw_up[0], dw_down[0] = mlp_bwd("mlp0", dh, h2, *mlp_args[0], mlp0_saved)
    dh, dg_xa[0], dg_xa_mem[0], dw_q[0], dw_kv[0], dw_o[0] = xa_bwd("xa0", dh, h1, mem0, *xa_args[0], xa0_saved)
    dh, dg_dn, dw_qkv, dw_z, dw_ba, dw_conv, d_gate, d_out_norm, dw_out = dn_bwd(dh, h0, *dn_args, dn_saved)

    grads["dn_w_in"] = jnp.concatenate([dw_qkv, dw_z, dw_ba[:, :2 * DN_HEADS]], axis=1)[None]
    grads["dn_w_conv"] = dw_conv[None, :DN_CONV]
    grads["dn_w_out"] = dw_out[None]
    grads["cv_norm"] = grads["cv_norm"]
    grads["cv_w_pw1"] = grads["cv_w_pw1"][None]
    grads["cv_w_dw"] = dw_dw[None, :CV_WIDTH]
    grads["cv_ln_g"], grads["cv_ln_b"], grads["cv_b_dw"] = ln_acc[0:1], ln_acc[1:2], ln_acc[2:3]
    grads["cv_w_pw2"] = grads["cv_w_pw2"][None]
    grads["xa_w_q"], grads["xa_w_kv"], grads["xa_w_o"] = jnp.stack(dw_q), jnp.stack(dw_kv), jnp.stack(dw_o)
    grads["mlp_w_up"], grads["mlp_w_down"] = jnp.stack(dw_up), jnp.stack(dw_down)

    rep = jnp.zeros((16, d), F32)
    rep = rep.at[0].set(dg_dn[0])
    rep = rep.at[1, :LANES].set(d_gate[0])
    rep = rep.at[2, :LANES].set(d_gate[1])
    rep = rep.at[3, :LANES].set(d_out_norm[0])
    rep = rep.at[4].set(dg_xa[0][0]).at[5].set(dg_xa[1][0])
    rep = rep.at[6].set(dg_xa_mem[0][0]).at[7].set(dg_xa_mem[1][0])
    rep = rep.at[8].set(dg_mlp[0][0]).at[9].set(dg_mlp[1][0])
    rep = rep.at[10].set(d_final[0])
    rep = rep.at[11, :LANES].set(loss_tile[0])
    return dh, grads, rep
```

```python
import functools

import jax
import jax.numpy as jnp
from jax import lax
from jax.experimental import pallas as pl
from jax.experimental.pallas import tpu as pltpu

F32 = jnp.float32
BF16 = jnp.bfloat16
HIGHEST = lax.Precision.HIGHEST
MESH = pl.DeviceIdType.MESH

D_MODEL = 1024
DN_HEADS = 8
DN_HEAD_DIM = 128
DN_CONV = 4
DN_CHUNK = 64
CV_WIDTH = 31
XA_HEADS = 4
XA_HEAD_DIM = 256
RMS_EPS = 1e-6
LN_EPS = 1e-5
L2_EPS = 1e-6

ADAM_LR = 0.001
ADAM_B1 = 0.9
ADAM_B2 = 0.999
ADAM_EPS = 1e-08
ADAM_WD = 0.01
ADAM_STEP = 10

LANES = 128
ROW_TILE = 512
CONV_ROW_TILE = 256
DN_ROW_TILE = 256
CHUNK_SHIFT = 6
SOLVE_INTERLEAVE = 8
FWD_HEADS_PER_STEP = 4
BWD_HEADS_PER_STEP = 4
DN_HALO = 8
CV_HALO = 32
VMEM_LIMIT = 48 * 1024 * 1024
N_CHIPS = 4
D2D_CHUNK_ROWS = 256


def _cparams(sem):
    return pltpu.CompilerParams(dimension_semantics=sem, vmem_limit_bytes=VMEM_LIMIT)


def _dot(a, b, dims=(((1,), (0,)), ((), ()))):
    return lax.dot_general(a.astype(BF16), b.astype(BF16), dims, preferred_element_type=F32)


def _dot_nt(a, b):
    return _dot(a, b, (((1,), (1,)), ((), ())))


def _dot_tn(a, b):
    return _dot(a, b, (((0,), (0,)), ((), ())))


def _dot_hi(a, b, dims=(((1,), (0,)), ((), ()))):
    return lax.dot_general(a.astype(F32), b.astype(F32), dims, precision=HIGHEST, preferred_element_type=F32)


def _sigmoid(x):
    return 1.0 / (1.0 + jnp.exp(-x))


def _silu(x):
    return x * _sigmoid(x)


def _silu_grad(x):
    s = _sigmoid(x)
    return s * (1.0 + x * (1.0 - s))


def _softplus(x):
    return jnp.maximum(x, 0.0) + jnp.log(1.0 + jnp.exp(-jnp.abs(x)))


def _iota(shape, dim):
    return lax.broadcasted_iota(jnp.int32, shape, dim)


def _lane_col(vals, lane, idx):
    return jnp.sum(jnp.where(lane == idx, vals, 0.0), axis=1, keepdims=True)


def _pick_tile(rows, cap):
    best = rows
    for t in range(16, min(rows, cap) + 1, 16):
        if rows % t == 0:
            best = t
    return best


def mm(name, a, b, *, ta=False, tb=False, out_dtype=F32, pro=None, epi=None, epi_tiles=(), epi_rows=(),
       tm=512, tn=512, tk=1024):
    m, k = (a.shape[1], a.shape[0]) if ta else a.shape
    n = b.shape[0] if tb else b.shape[1]
    assert (b.shape[1] if tb else b.shape[0]) == k
    tm, tn, tk = min(tm, m), min(tn, n), min(tk, k)
    assert m % tm == 0 and n % tn == 0 and k % tk == 0
    nk = k // tk
    a_spec = pl.BlockSpec((tk, tm), lambda i, j, kk: (kk, i)) if ta else pl.BlockSpec((tm, tk), lambda i, j, kk: (i, kk))
    b_spec = pl.BlockSpec((tn, tk), lambda i, j, kk: (j, kk)) if tb else pl.BlockSpec((tk, tn), lambda i, j, kk: (kk, j))
    in_specs = [a_spec, b_spec]
    in_specs += [pl.BlockSpec((tm, tn), lambda i, j, kk: (i, j)) for _ in epi_tiles]
    in_specs += [pl.BlockSpec((1, tn), lambda i, j, kk: (0, j)) for _ in epi_rows]
    n_t, n_r = len(epi_tiles), len(epi_rows)
    dims = (((0 if ta else 1,), (1 if tb else 0,)), ((), ()))

    def body(a_ref, b_ref, *rest):
        tiles = rest[:n_t]
        rows = rest[n_t:n_t + n_r]
        o_ref, acc_ref = rest[n_t + n_r], rest[n_t + n_r + 1]
        kk = pl.program_id(2)

        @pl.when(kk == 0)
        def _():
            acc_ref[...] = jnp.zeros_like(acc_ref)

        av = a_ref[...]
        if pro is not None:
            av = pro(av)
        acc_ref[...] += _dot(av, b_ref[...], dims)

        @pl.when(kk == nk - 1)
        def _():
            out = acc_ref[...]
            if epi is not None:
                out = epi(out, *[t[...] for t in tiles], *[r[...] for r in rows])
            o_ref[...] = out.astype(out_dtype)

    return pl.pallas_call(
        body, name=name, grid=(m // tm, n // tn, nk),
        in_specs=in_specs, out_specs=pl.BlockSpec((tm, tn), lambda i, j, kk: (i, j)),
        out_shape=jax.ShapeDtypeStruct((m, n), out_dtype),
        scratch_shapes=[pltpu.VMEM((tm, tn), F32)],
        compiler_params=_cparams(("parallel", "parallel", "arbitrary")),
    )(a, b, *epi_tiles, *epi_rows)


def row_call(name, body, n_rows, tm, ins, outs, accs=()):
    tm = _pick_tile(n_rows, tm)
    in_specs = []
    for arr, kind in ins:
        if kind == "tile":
            if arr.ndim == 2:
                in_specs.append(pl.BlockSpec((tm, arr.shape[1]), lambda i: (i, 0)))
            else:
                in_specs.append(pl.BlockSpec((arr.shape[0], tm, arr.shape[2]), lambda i: (0, i, 0)))
        elif kind == "full":
            in_specs.append(pl.BlockSpec(arr.shape, functools.partial(lambda i, nd: (0,) * nd, nd=arr.ndim)))
        else:
            where, h = kind
            per = tm // h
            if where == "prev":
                in_specs.append(pl.BlockSpec((h, arr.shape[1]), functools.partial(
                    lambda i, per: (jnp.maximum(i * per - 1, 0), 0), per=per)))
            else:
                last = n_rows // h - 1
                in_specs.append(pl.BlockSpec((h, arr.shape[1]), functools.partial(
                    lambda i, per, last: (jnp.minimum((i + 1) * per, last), 0), per=per, last=last)))
    out_shape, out_specs = [], []
    for shape, dtype in outs:
        out_shape.append(jax.ShapeDtypeStruct(shape, dtype))
        if len(shape) == 2:
            out_specs.append(pl.BlockSpec((tm, shape[1]), lambda i: (i, 0)))
        else:
            out_specs.append(pl.BlockSpec((shape[0], tm, shape[2]), lambda i: (0, i, 0)))
    for shape in accs:
        out_shape.append(jax.ShapeDtypeStruct(shape, F32))
        out_specs.append(pl.BlockSpec(shape, lambda i: (0, 0)))
    n_in, n_out, n_acc = len(ins), len(outs), len(accs)

    def kern(*refs):
        i = pl.program_id(0)
        in_refs = refs[:n_in]
        out_refs = refs[n_in:n_in + n_out]
        acc_refs = refs[n_in + n_out:n_in + n_out + n_acc]
        if n_acc:
            @pl.when(i == 0)
            def _():
                for r in acc_refs:
                    r[...] = jnp.zeros_like(r)
        body(i, in_refs, out_refs, acc_refs)

    res = pl.pallas_call(
        kern, name=name, grid=(n_rows // tm,), in_specs=in_specs, out_specs=out_specs, out_shape=out_shape,
        compiler_params=_cparams(("arbitrary",) if n_acc else ("parallel",)),
    )(*[a for a, _ in ins])
    return list(res)


def _rms_stats(h):
    r = lax.rsqrt(jnp.mean(h * h, axis=-1, keepdims=True) + RMS_EPS)
    return h * r, r


def rms_fwd(name, h, g):
    def body(i, ins, outs, accs):
        xhat, _ = _rms_stats(ins[0][...])
        outs[0][...] = (xhat * ins[1][...]).astype(BF16)

    return row_call(name, body, h.shape[0], ROW_TILE, [(h, "tile"), (g, "full")], [(h.shape, BF16)])[0]


def _rms_bwd_tile(dn, h, g):
    xhat, r = _rms_stats(h)
    dxhat = dn * g
    dh = r * (dxhat - xhat * jnp.mean(dxhat * xhat, axis=-1, keepdims=True))
    dg = jnp.sum(dn * xhat, axis=0, keepdims=True)
    return dh, dg


def rms_bwd(name, dn, h, g, dres):
    def body(i, ins, outs, accs):
        dh, dg = _rms_bwd_tile(ins[0][...].astype(F32), ins[1][...], ins[2][...])
        outs[0][...] = ins[3][...] + dh
        accs[0][...] += dg

    d = h.shape[1]
    out, dg = row_call(name, body, h.shape[0], ROW_TILE,
                       [(dn, "tile"), (h, "tile"), (g, "full"), (dres, "tile")], [(h.shape, F32)], [(1, d)])
    return out, dg


def mem_norm_bwd(name, dn, mem, g):
    def body(i, ins, outs, accs):
        _, dg = _rms_bwd_tile(ins[0][...].astype(F32), ins[1][...], ins[2][...])
        accs[0][...] += dg

    return row_call(name, body, mem.shape[0], ROW_TILE, [(dn, "tile"), (mem, "tile"), (g, "full")], [],
                    [(1, mem.shape[1])])[0]


def loss_head(name, h, g, target):
    d = h.shape[1]

    def body(i, ins, outs, accs):
        hv, gv = ins[0][...], ins[1][...]
        xhat, _ = _rms_stats(hv)
        err = xhat * gv - ins[2][...]
        dy = err * (1.0 / d)
        dh, dg = _rms_bwd_tile(dy, hv, gv)
        outs[0][...] = dh
        accs[0][...] += jnp.full((8, LANES), 0.5 / d, F32) * jnp.sum(err * err)
        accs[1][...] += dg

    dh, loss, dg = row_call(name, body, h.shape[0], ROW_TILE, [(h, "tile"), (g, "full"), (target, "tile")],
                            [(h.shape, F32)], [(8, LANES), (1, d)])
    return dh, loss, dg


def col_sum(name, x):
    def body(i, ins, outs, accs):
        accs[0][...] += jnp.sum(ins[0][...].astype(F32), axis=0, keepdims=True)

    return row_call(name, body, x.shape[0], ROW_TILE, [(x, "tile")], [], [(1, x.shape[1])])[0]


def _conv_taps(xcat, w_ref, cols, width, halo, tm):
    rows = halo + tm
    acc = None
    for j in range(width):
        s = width - 1 - j
        xs = xcat if s == 0 else pltpu.roll(xcat, s, 0)
        term = xs[halo:rows] * w_ref[j:j + 1, cols]
        acc = term if acc is None else acc + term
    return acc


def _conv_taps_bwd_x(dcat, w_ref, cols, width, halo, tm):
    rows = halo + tm
    acc = None
    for j in range(width):
        s = width - 1 - j
        ds = dcat if s == 0 else pltpu.roll(dcat, rows - s, 0)
        term = ds[0:tm] * w_ref[j:j + 1, cols]
        acc = term if acc is None else acc + term
    return acc


def _conv_taps_bwd_w(dy, xcat, width, halo, tm, wrows):
    rows = halo + tm
    rid = _iota((wrows, dy.shape[1]), 0)
    out = jnp.zeros((wrows, dy.shape[1]), F32)
    for j in range(width):
        s = width - 1 - j
        xs = xcat if s == 0 else pltpu.roll(xcat, s, 0)
        v = jnp.sum(dy * xs[halo:rows], axis=0, keepdims=True)
        out = out + jnp.where(rid == j, v, 0.0)
    return out


def dn_pre(qkv_raw, ba, w_conv, gate):
    s_len = qkv_raw.shape[0]
    tm = min(DN_ROW_TILE, s_len)
    n_blk = qkv_raw.shape[1] // LANES

    def body(i, ins, outs, accs):
        x_ref, xp_ref, ba_ref, w_ref, gate_ref = ins
        qkv_ref, hs_ref = outs

        def blk(cb, carry):
            cols = pl.ds(pl.multiple_of(cb * LANES, LANES), LANES)
            prev = jnp.where(i > 0, xp_ref[:, cols], 0.0)
            xcat = jnp.concatenate([prev, x_ref[:, cols]], axis=0)
            c = _conv_taps(xcat, w_ref, cols, DN_CONV, DN_HALO, tm)
            y = _silu(c)
            rs = lax.rsqrt(jnp.sum(y * y, axis=-1, keepdims=True) + L2_EPS)
            fac = jnp.where(cb < DN_HEADS, DN_HEAD_DIM ** -0.5, 1.0)
            qkv_ref[:, cols] = jnp.where(cb < 2 * DN_HEADS, y * (rs * fac), y)
            return carry

        lax.fori_loop(0, n_blk, blk, 0)

        bav = ba_ref[...]
        beta = _sigmoid(bav)
        g = -jnp.exp(gate_ref[0:1, :]) * _softplus(bav + gate_ref[1:2, :])
        lane = _iota((tm, LANES), 1)
        g = jnp.where((lane >= DN_HEADS) & (lane < 2 * DN_HEADS), g, 0.0)
        r = _iota((tm, tm), 0)
        c = _iota((tm, tm), 1)
        tri = jnp.where((r >= c) & ((r >> CHUNK_SHIFT) == (c >> CHUNK_SHIFT)), 1.0, 0.0)
        gc = _dot_hi(tri, g)
        for h in range(DN_HEADS):
            hs_ref[h] = jnp.where(lane == 0, _lane_col(beta, lane, h),
                                  jnp.where(lane == 1, _lane_col(g, lane, DN_HEADS + h),
                                            jnp.where(lane == 2, _lane_col(gc, lane, DN_HEADS + h), 0.0)))

    return row_call("dn_pre", body, s_len, tm,
                    [(qkv_raw, "tile"), (qkv_raw, ("prev", DN_HALO)), (ba, "tile"), (w_conv, "full"), (gate, "full")],
                    [(qkv_raw.shape, F32), ((DN_HEADS, s_len, LANES), F32)])


def _chunk_masks():
    r = _iota((DN_CHUNK, DN_CHUNK), 0)
    c = _iota((DN_CHUNK, DN_CHUNK), 1)
    return r, c


def _decay_matrix(gc, r, c):
    lane = _iota((DN_CHUNK, LANES), 1)
    a = jnp.where(lane == 0, gc, jnp.where(lane == 1, 1.0, 0.0))
    b = jnp.where(lane == 0, 1.0, jnp.where(lane == 1, -gc, 0.0))
    diff = _dot_hi(a, b, (((1,), (1,)), ((), ())))
    causal = r >= c
    return jnp.where(causal, jnp.exp(jnp.where(causal, diff, 0.0)), 0.0)


def _tri_inverse(lows, r, c):
    eye = jnp.where(r == c, 1.0, 0.0)
    ts = [eye for _ in lows]
    b = 1
    while b < DN_CHUNK:
        shift = b.bit_length()
        sel = ((r >> shift) == (c >> shift)) & ((r & b) != 0) & ((c & b) == 0)
        lms = [jnp.where(sel, low, 0.0) for low in lows]
        if b == 1:
            ts = [t - lm for t, lm in zip(ts, lms)]
        else:
            t_lm = [_dot_hi(t, lm) for t, lm in zip(ts, lms)]
            t_lm_t = [_dot_hi(x, t) for x, t in zip(t_lm, ts)]
            ts = [t - x for t, x in zip(ts, t_lm_t)]
        b *= 2
    return ts


def dn_solve(qkv, hs):
    s_len = qkv.shape[0]
    rb = min(ROW_TILE, s_len)
    n_chunk = rb // DN_CHUNK
    interleave = min(SOLVE_INTERLEAVE, n_chunk)

    def body(k_ref, v_ref, hs_ref, u_ref, w_ref, t_ref):
        r, c = _chunk_masks()

        def group(gi, carry):
            rows = [pl.ds(pl.multiple_of((gi * interleave + j) * DN_CHUNK, DN_CHUNK), DN_CHUNK)
                    for j in range(interleave)]
            k = [k_ref[rw, :] for rw in rows]
            beta = [hs_ref[rw, 0:1] for rw in rows]
            gc = [hs_ref[rw, 2:3] for rw in rows]
            kb = [a * b for a, b in zip(k, beta)]
            decay = [_decay_matrix(g, r, c) for g in gc]
            lows = [jnp.where(r > c, _dot_nt(a, b) * d, 0.0) for a, b, d in zip(kb, k, decay)]
            ts = _tri_inverse(lows, r, c)
            us = [_dot_hi(t, v_ref[rw, :] * b) for t, rw, b in zip(ts, rows, beta)]
            ws = [_dot_hi(t, a * jnp.exp(g)) for t, a, g in zip(ts, kb, gc)]
            for j, rw in enumerate(rows):
                u_ref[rw, :] = us[j]
                w_ref[rw, :] = ws[j].astype(BF16)
                t_ref[rw, :] = ts[j]
            return carry

        lax.fori_loop(0, n_chunk // interleave, group, 0)

    return pl.pallas_call(
        body, name="dn_solve", grid=(DN_HEADS, s_len // rb),
        in_specs=[pl.BlockSpec((rb, LANES), lambda h, i: (i, DN_HEADS + h)),
                  pl.BlockSpec((rb, LANES), lambda h, i: (i, 2 * DN_HEADS + h)),
                  pl.BlockSpec((None, rb, LANES), lambda h, i: (h, i, 0))],
        out_specs=[pl.BlockSpec((rb, LANES), lambda h, i: (i, h)),
                   pl.BlockSpec((rb, LANES), lambda h, i: (i, h)),
                   pl.BlockSpec((None, rb, DN_CHUNK), lambda h, i: (h, i, 0))],
        out_shape=[jax.ShapeDtypeStruct((s_len, DN_HEADS * LANES), F32),
                   jax.ShapeDtypeStruct((s_len, DN_HEADS * LANES), BF16),
                   jax.ShapeDtypeStruct((DN_HEADS, s_len, DN_CHUNK), F32)],
        compiler_params=_cparams(("parallel", "parallel")),
    )(qkv, qkv, hs)


def dn_scan_fwd(qkv, u, w, hs):
    s_len = qkv.shape[0]
    rb = min(ROW_TILE, s_len)
    n_chunk = rb // DN_CHUNK
    total_chunks = s_len // DN_CHUNK

    hps = FWD_HEADS_PER_STEP
    groups = DN_HEADS // hps

    def body(q_ref, k_ref, u_ref, w_ref, hs_ref, o_ref, st_ref, state):
        @pl.when(pl.program_id(1) == 0)
        def _():
            state[...] = jnp.zeros_like(state)

        r, c = _chunk_masks()

        def chunk(n, carry):
            rows = pl.ds(pl.multiple_of(n * DN_CHUNK, DN_CHUNK), DN_CHUNK)
            heads = range(hps)
            cols = [slice(h * LANES, (h + 1) * LANES) for h in heads]
            each = lambda f, *xs: [f(*a) for a in zip(*xs)]
            q = [q_ref[rows, cl] for cl in cols]
            k = [k_ref[rows, cl] for cl in cols]
            gc = [hs_ref[h, rows, 2:3] for h in heads]
            st = [state[h] for h in heads]
            for h in heads:
                st_ref[h, n] = st[h]
            gl = each(lambda g: jnp.min(g, axis=0, keepdims=True), gc)
            decay = each(lambda g: _decay_matrix(g, r, c), gc)
            w_st = [_dot(w_ref[rows, cols[h]], st[h]) for h in heads]
            qk = each(_dot_nt, q, k)
            q_st = each(lambda a, g, s: _dot(a * jnp.exp(g), s), q, gc, st)
            vn = [u_ref[rows, cols[h]] - w_st[h] for h in heads]
            ai_vn = each(lambda a, d, b: _dot(a * d, b), qk, decay, vn)
            kd_vn = each(lambda a, g0, g, b: _dot_tn(a * jnp.exp(g0 - g), b), k, gl, gc, vn)
            for h in heads:
                o_ref[rows, cols[h]] = q_st[h] + ai_vn[h]
                state[h] = st[h] * jnp.exp(gl[h]) + kd_vn[h]
            return carry

        lax.fori_loop(0, n_chunk, chunk, 0)

    wide = hps * LANES
    blk = lambda off: pl.BlockSpec((rb, wide), lambda h, i: (i, off + h))
    return pl.pallas_call(
        body, name="dn_scan_fwd", grid=(groups, s_len // rb),
        in_specs=[blk(0), blk(groups), blk(0), blk(0),
                  pl.BlockSpec((hps, rb, LANES), lambda h, i: (h, i, 0))],
        out_specs=[blk(0),
                   pl.BlockSpec((hps, n_chunk, LANES, LANES), lambda h, i: (h, i, 0, 0))],
        out_shape=[jax.ShapeDtypeStruct((s_len, DN_HEADS * LANES), F32),
                   jax.ShapeDtypeStruct((DN_HEADS, total_chunks, LANES, LANES), F32)],
        scratch_shapes=[pltpu.VMEM((hps, LANES, LANES), F32)],
        compiler_params=_cparams(("parallel", "arbitrary")),
    )(qkv, qkv, u, w, hs)


def dn_scan_bwd(qkv, u, w, t_inv, hs, states, d_o):
    s_len = qkv.shape[0]
    rb = min(ROW_TILE, s_len)
    n_chunk = rb // DN_CHUNK
    n_blk = s_len // rb
    hps = BWD_HEADS_PER_STEP
    groups = DN_HEADS // hps

    def body(q_ref, k_ref, v_ref, u_ref, w_ref, t_ref, hs_ref, st_ref, do_ref,
             dq_ref, dk_ref, dv_ref, dhs_ref, dstate):
        @pl.when(pl.program_id(1) == 0)
        def _():
            dstate[...] = jnp.zeros_like(dstate)

        r, c = _chunk_masks()
        causal = r >= c
        strict = r > c
        lane = _iota((DN_CHUNK, LANES), 1)
        upper = jnp.where(r <= c, 1.0, 0.0)
        last_row = _iota((DN_CHUNK, 1), 0) == DN_CHUNK - 1

        def chunk(m, carry):
            n = n_chunk - 1 - m
            rows = pl.ds(pl.multiple_of(n * DN_CHUNK, DN_CHUNK), DN_CHUNK)
            heads = range(hps)
            cols = [slice(h * LANES, (h + 1) * LANES) for h in heads]
            each = lambda f, *xs: [f(*a) for a in zip(*xs)]
            rsum = lambda x: jnp.sum(x, axis=-1, keepdims=True)
            dims_tn = (((0,), (0,)), ((), ()))
            ones = jnp.ones((DN_CHUNK, LANES), F32)
            q = [q_ref[rows, cl] for cl in cols]
            k = [k_ref[rows, cl] for cl in cols]
            v = [v_ref[rows, cl] for cl in cols]
            uu = [u_ref[rows, cl] for cl in cols]
            ww = [w_ref[rows, cl] for cl in cols]
            do = [do_ref[rows, cl] for cl in cols]
            tt = [t_ref[h, rows, :] for h in heads]
            beta = [hs_ref[h, rows, 0:1] for h in heads]
            gc = [hs_ref[h, rows, 2:3] for h in heads]
            st = [st_ref[h, n] for h in heads]
            dst = [dstate[h] for h in heads]
            gl = each(lambda g: jnp.min(g, axis=0, keepdims=True), gc)
            egc = each(jnp.exp, gc)
            egl = each(jnp.exp, gl)
            ekd = each(lambda a, b: jnp.exp(a - b), gl, gc)
            decay = each(lambda g: _decay_matrix(g, r, c), gc)
            qd = each(jnp.multiply, q, egc)
            kd = each(jnp.multiply, k, ekd)
            kb = each(jnp.multiply, k, beta)
            w_st = each(_dot, ww, st)
            qk = each(_dot_nt, q, k)
            dqd = each(_dot_nt, do, st)
            kd_dst = each(_dot, kd, dst)
            qd_do = each(_dot_tn, qd, do)
            kbk = each(_dot_nt, kb, k)
            vn = each(jnp.subtract, uu, w_st)
            ai = each(jnp.multiply, qk, decay)
            low = each(lambda a, d: jnp.where(strict, a * d, 0.0), kbk, decay)
            dai = each(lambda a, b: jnp.where(causal, _dot_nt(a, b), 0.0), do, vn)
            ai_do = each(_dot_tn, ai, do)
            dkd = each(_dot_nt, vn, dst)
            dvn = each(jnp.add, ai_do, kd_dst)
            dp = each(jnp.multiply, dai, decay)
            dw = each(lambda a, b: -_dot_nt(a, b), dvn, st)
            w_dvn = each(_dot_tn, ww, dvn)
            dp_k = each(_dot, dp, k)
            dp_q = each(_dot_tn, dp, q)
            drhs_u = each(lambda a, b: _dot_hi(a, b, dims_tn), tt, dvn)
            dgl = each(lambda a, b, e: jnp.sum(a * b) * e, dst, st, egl)
            for h in heads:
                dstate[h] = dst[h] * egl[h] + qd_do[h] - w_dvn[h]
            dq = each(lambda a, e, b: a * e + b, dqd, egc, dp_k)
            dk_a = each(lambda a, e, b: a * e + b, dkd, ekd, dp_q)
            rkd = each(lambda a, b: rsum(a * b), dkd, kd)
            drhs_w = each(lambda a, b: _dot_hi(a, b, dims_tn), tt, dw)
            dl_u = each(_dot_nt, drhs_u, uu)
            dl_w = each(_dot_nt, drhs_w, ww)
            dlow = each(lambda a, b: jnp.where(strict, -(a + b), 0.0), dl_u, dl_w)
            dqm = each(jnp.multiply, dlow, decay)
            m_tot = each(lambda a, b, d, e: a * b + d * e, dai, ai, dlow, low)
            dqm_k = each(_dot, dqm, k)
            dk_l = each(_dot_tn, dqm, kb)
            col_sums = each(lambda m: _dot_hi(m, ones, dims_tn), m_tot)
            dkb_w = each(jnp.multiply, drhs_w, egc)
            dkb = each(jnp.add, dkb_w, dqm_k)
            dgc = [rsum(dqd[h] * qd[h]) - rkd[h] + jnp.where(last_row, jnp.sum(rkd[h]) + dgl[h], 0.0)
                   + rsum(m_tot[h]) + rsum(dkb_w[h] * kb[h]) for h in heads]
            dg = each(lambda a, b: _dot_hi(upper, jnp.where(lane == 1, a - b, 0.0)), dgc, col_sums)
            for h in heads:
                dq_ref[rows, cols[h]] = dq[h]
                dk_ref[rows, cols[h]] = dk_a[h] + dk_l[h] + dkb[h] * beta[h]
                dv_ref[rows, cols[h]] = drhs_u[h] * beta[h]
                dbeta = rsum(drhs_u[h] * v[h]) + rsum(dkb[h] * k[h])
                dhs_ref[h, rows, :] = jnp.where(lane == 0, dbeta, dg[h])
            return carry

        lax.fori_loop(0, n_chunk, chunk, 0)

    wide = hps * LANES
    blk = lambda off: pl.BlockSpec((rb, wide), lambda h, i: (n_blk - 1 - i, off + h))
    head = blk(0)
    hs_spec = pl.BlockSpec((hps, rb, LANES), lambda h, i: (h, n_blk - 1 - i, 0))
    full = jax.ShapeDtypeStruct((s_len, DN_HEADS * LANES), F32)
    return pl.pallas_call(
        body, name="dn_scan_bwd", grid=(groups, n_blk),
        in_specs=[blk(0), blk(groups), blk(2 * groups), head, head,
                  pl.BlockSpec((hps, rb, DN_CHUNK), lambda h, i: (h, n_blk - 1 - i, 0)), hs_spec,
                  pl.BlockSpec((hps, n_chunk, LANES, LANES), lambda h, i: (h, n_blk - 1 - i, 0, 0)), head],
        out_specs=[head, head, head, hs_spec],
        out_shape=[full, full, full, jax.ShapeDtypeStruct((DN_HEADS, s_len, LANES), F32)],
        scratch_shapes=[pltpu.VMEM((hps, LANES, LANES), F32)],
        compiler_params=_cparams(("parallel", "arbitrary")),
    )(qkv, qkv, qkv, u, w, t_inv, hs, states, d_o)


def dn_post(o, z, out_norm):
    def body(i, ins, outs, accs):
        gn = ins[2][...]
        for h in range(DN_HEADS):
            cols = slice(h * LANES, (h + 1) * LANES)
            xhat, _ = _rms_stats(ins[0][:, cols])
            outs[0][:, cols] = (xhat * gn * _silu(ins[1][:, cols])).astype(BF16)

    return row_call("dn_post", body, o.shape[0], ROW_TILE, [(o, "tile"), (z, "tile"), (out_norm, "full")],
                    [(o.shape, BF16)])[0]


def dn_post_bwd(d_og, o, z, out_norm):
    def body(i, ins, outs, accs):
        gn = ins[3][...]
        dgn = jnp.zeros((1, LANES), F32)
        for h in range(DN_HEADS):
            cols = slice(h * LANES, (h + 1) * LANES)
            dy, zh = ins[0][:, cols].astype(F32), ins[2][:, cols]
            xhat, r = _rms_stats(ins[1][:, cols])
            sz = _silu(zh)
            dgn = dgn + jnp.sum(dy * xhat * sz, axis=0, keepdims=True)
            outs[1][:, cols] = (dy * xhat * gn * _silu_grad(zh)).astype(BF16)
            dxhat = dy * gn * sz
            outs[0][:, cols] = r * (dxhat - xhat * jnp.mean(dxhat * xhat, axis=-1, keepdims=True))
        accs[0][...] += dgn

    return row_call("dn_post_bwd", body, o.shape[0], ROW_TILE,
                    [(d_og, "tile"), (o, "tile"), (z, "tile"), (out_norm, "full")],
                    [(o.shape, F32), (o.shape, BF16)], [(1, LANES)])


def dn_pre_bwd(dq, dk, dv, dhs, qkv_raw, ba, w_conv, gate):
    s_len = qkv_raw.shape[0]
    tm = min(DN_ROW_TILE, s_len)

    def body(i, ins, outs, accs):
        dq_ref, dk_ref, dv_ref, dhs_ref, x_ref, xp_ref, ba_ref, w_ref, gate_ref = ins
        dc_ref, dba_ref = outs

        def blk(cb, carry):
            cols = pl.ds(pl.multiple_of(cb * LANES, LANES), LANES)
            hcols = pl.ds(pl.multiple_of((cb & (DN_HEADS - 1)) * LANES, LANES), LANES)
            prev = jnp.where(i > 0, xp_ref[:, cols], 0.0)
            xcat = jnp.concatenate([prev, x_ref[:, cols]], axis=0)
            c = _conv_taps(xcat, w_ref, cols, DN_CONV, DN_HALO, tm)
            y = _silu(c)
            dy = jnp.where(cb < DN_HEADS, dq_ref[:, hcols],
                           jnp.where(cb < 2 * DN_HEADS, dk_ref[:, hcols], dv_ref[:, hcols]))
            rs = lax.rsqrt(jnp.sum(y * y, axis=-1, keepdims=True) + L2_EPS)
            fac = jnp.where(cb < DN_HEADS, DN_HEAD_DIM ** -0.5, 1.0)
            nrm = y * rs
            dn = dy * fac
            dy_norm = rs * (dn - nrm * jnp.sum(dn * nrm, axis=-1, keepdims=True))
            dc_ref[:, cols] = jnp.where(cb < 2 * DN_HEADS, dy_norm, dy) * _silu_grad(c)
            return carry

        lax.fori_loop(0, qkv_raw.shape[1] // LANES, blk, 0)

        lane = _iota((tm, LANES), 1)
        dbeta = jnp.zeros((tm, LANES), F32)
        dg = jnp.zeros((tm, LANES), F32)
        for h in range(DN_HEADS):
            dbeta = dbeta + jnp.where(lane == h, dhs_ref[h, :, 0:1], 0.0)
            dg = dg + jnp.where(lane == DN_HEADS + h, dhs_ref[h, :, 1:2], 0.0)
        bav = ba_ref[...]
        beta = _sigmoid(bav)
        ea = jnp.exp(gate_ref[0:1, :])
        pre = bav + gate_ref[1:2, :]
        g = -ea * _softplus(pre)
        da = dg * (-ea) * _sigmoid(pre)
        dba_ref[...] = (dbeta * beta * (1.0 - beta) + da).astype(BF16)
        rid = _iota((8, LANES), 0)
        accs[0][...] += (jnp.where(rid == 0, jnp.sum(dg * g, axis=0, keepdims=True), 0.0)
                         + jnp.where(rid == 1, jnp.sum(da, axis=0, keepdims=True), 0.0))

    return row_call("dn_pre_bwd", body, s_len, tm,
                    [(dq, "tile"), (dk, "tile"), (dv, "tile"), (dhs, "tile"), (qkv_raw, "tile"),
                     (qkv_raw, ("prev", DN_HALO)), (ba, "tile"), (w_conv, "full"), (gate, "full")],
                    [(qkv_raw.shape, F32), (ba.shape, BF16)], [(8, LANES)])


def dn_conv_bwd(dc, qkv_raw, w_conv):
    s_len = dc.shape[0]
    tm = min(DN_ROW_TILE, s_len)
    nt = s_len // tm

    def body(i, ins, outs, accs):
        dc_ref, dn_ref, x_ref, xp_ref, w_ref = ins

        def blk(cb, carry):
            cols = pl.ds(pl.multiple_of(cb * LANES, LANES), LANES)
            dy = dc_ref[:, cols]
            nxt = jnp.where(i < nt - 1, dn_ref[:, cols], 0.0)
            dcat = jnp.concatenate([dy, nxt], axis=0)
            outs[0][:, cols] = _conv_taps_bwd_x(dcat, w_ref, cols, DN_CONV, DN_HALO, tm).astype(BF16)
            prev = jnp.where(i > 0, xp_ref[:, cols], 0.0)
            xcat = jnp.concatenate([prev, x_ref[:, cols]], axis=0)
            accs[0][:, cols] += _conv_taps_bwd_w(dy, xcat, DN_CONV, DN_HALO, tm, 8)
            return carry

        lax.fori_loop(0, dc.shape[1] // LANES, blk, 0)

    return row_call("dn_conv_bwd", body, s_len, tm,
                    [(dc, "tile"), (dc, ("next", DN_HALO)), (qkv_raw, "tile"), (qkv_raw, ("prev", DN_HALO)),
                     (w_conv, "full")],
                    [(dc.shape, BF16)], [(8, dc.shape[1])])


def _glu(u_ref, cols, d):
    return u_ref[:, cols] * _sigmoid(u_ref[:, pl.ds(pl.multiple_of(d + cols.start, LANES), cols.size)])


def cv_core_fwd(u, w_dw, b_dw, ln_g, ln_b):
    s_len, d = u.shape[0], u.shape[1] // 2
    tm = min(CONV_ROW_TILE, s_len)

    def body(i, ins, outs, accs):
        u_ref, up_ref, w_ref, bdw_ref, g_ref, b_ref = ins
        s_ref, c_ref = outs

        def blk(cb, carry):
            cols = pl.ds(pl.multiple_of(cb * LANES, LANES), LANES)
            prev = jnp.where(i > 0, _glu(up_ref, cols, d), 0.0)
            xcat = jnp.concatenate([prev, _glu(u_ref, cols, d)], axis=0)
            c_ref[:, cols] = _conv_taps(xcat, w_ref, cols, CV_WIDTH, CV_HALO, tm) + bdw_ref[:, cols]
            return carry

        lax.fori_loop(0, d // LANES, blk, 0)
        c = c_ref[...]
        mu = jnp.mean(c, axis=-1, keepdims=True)
        xc = c - mu
        rstd = lax.rsqrt(jnp.mean(xc * xc, axis=-1, keepdims=True) + LN_EPS)
        s_ref[...] = _silu(xc * rstd * g_ref[...] + b_ref[...]).astype(BF16)

    return row_call("cv_core_fwd", body, s_len, tm,
                    [(u, "tile"), (u, ("prev", CV_HALO)), (w_dw, "full"), (b_dw, "full"), (ln_g, "full"),
                     (ln_b, "full")],
                    [((s_len, d), BF16), ((s_len, d), F32)])


def cv_ln_bwd(ds, c, ln_g, ln_b):
    def body(i, ins, outs, accs):
        cv, g = ins[1][...], ins[2][...]
        mu = jnp.mean(cv, axis=-1, keepdims=True)
        xc = cv - mu
        rstd = lax.rsqrt(jnp.mean(xc * xc, axis=-1, keepdims=True) + LN_EPS)
        xhat = xc * rstd
        dl = ins[0][...].astype(F32) * _silu_grad(xhat * g + ins[3][...])
        dxhat = dl * g
        dc = rstd * (dxhat - jnp.mean(dxhat, axis=-1, keepdims=True)
                     - xhat * jnp.mean(dxhat * xhat, axis=-1, keepdims=True))
        outs[0][...] = dc
        rid = _iota((8, cv.shape[1]), 0)
        accs[0][...] += (jnp.where(rid == 0, jnp.sum(dl * xhat, axis=0, keepdims=True), 0.0)
                         + jnp.where(rid == 1, jnp.sum(dl, axis=0, keepdims=True), 0.0)
                         + jnp.where(rid == 2, jnp.sum(dc, axis=0, keepdims=True), 0.0))

    return row_call("cv_ln_bwd", body, c.shape[0], ROW_TILE,
                    [(ds, "tile"), (c, "tile"), (ln_g, "full"), (ln_b, "full")], [(c.shape, F32)], [(8, c.shape[1])])


def cv_conv_bwd(dc, u, w_dw):
    s_len, d = dc.shape
    tm = min(CONV_ROW_TILE, s_len)
    nt = s_len // tm

    def body(i, ins, outs, accs):
        dc_ref, dn_ref, u_ref, up_ref, w_ref = ins

        def blk(cb, carry):
            cols = pl.ds(pl.multiple_of(cb * LANES, LANES), LANES)
            gcols = pl.ds(pl.multiple_of(d + cb * LANES, LANES), LANES)
            dy = dc_ref[:, cols]
            nxt = jnp.where(i < nt - 1, dn_ref[:, cols], 0.0)
            dgl = _conv_taps_bwd_x(jnp.concatenate([dy, nxt], axis=0), w_ref, cols, CV_WIDTH, CV_HALO, tm)
            u1, sg = u_ref[:, cols], _sigmoid(u_ref[:, gcols])
            du1 = dgl * sg
            du2 = dgl * u1 * sg * (1.0 - sg)
            outs[0][:, cols] = du1.astype(BF16)
            outs[0][:, gcols] = du2.astype(BF16)
            accs[1][:, cols] += jnp.sum(du1, axis=0, keepdims=True)
            accs[1][:, gcols] += jnp.sum(du2, axis=0, keepdims=True)
            prev = jnp.where(i > 0, _glu(up_ref, cols, d), 0.0)
            xcat = jnp.concatenate([prev, u1 * sg], axis=0)
            accs[0][:, cols] += _conv_taps_bwd_w(dy, xcat, CV_WIDTH, CV_HALO, tm, CV_HALO)
            return carry

        lax.fori_loop(0, d // LANES, blk, 0)

    return row_call("cv_conv_bwd", body, s_len, tm,
                    [(dc, "tile"), (dc, ("next", CV_HALO)), (u, "tile"), (u, ("prev", CV_HALO)), (w_dw, "full")],
                    [(u.shape, BF16)], [(CV_HALO, d), (1, 2 * d)])


def xa_core_fwd(name, q, kv):
    d = q.shape[1]

    def body(i, ins, outs, accs):
        for h in range(XA_HEADS):
            cols = slice(h * XA_HEAD_DIM, (h + 1) * XA_HEAD_DIM)
            vcols = slice(d + h * XA_HEAD_DIM, d + (h + 1) * XA_HEAD_DIM)
            s = _dot_nt(ins[0][:, cols], ins[1][:, cols]) * (XA_HEAD_DIM ** -0.5)
            e = jnp.exp(s - jnp.max(s, axis=-1, keepdims=True))
            p = e / jnp.sum(e, axis=-1, keepdims=True)
            outs[0][:, cols] = _dot(p, ins[1][:, vcols]).astype(BF16)

    return row_call(name, body, q.shape[0], ROW_TILE, [(q, "tile"), (kv, "full")], [(q.shape, BF16)])[0]


def xa_core_bwd(name, d_o, q, kv):
    d = q.shape[1]

    def body(i, ins, outs, accs):
        for h in range(XA_HEADS):
            cols = slice(h * XA_HEAD_DIM, (h + 1) * XA_HEAD_DIM)
            vcols = slice(d + h * XA_HEAD_DIM, d + (h + 1) * XA_HEAD_DIM)
            qh, kh, vh, doh = ins[1][:, cols], ins[2][:, cols], ins[2][:, vcols], ins[0][:, cols]
            s = _dot_nt(qh, kh) * (XA_HEAD_DIM ** -0.5)
            e = jnp.exp(s - jnp.max(s, axis=-1, keepdims=True))
            p = e / jnp.sum(e, axis=-1, keepdims=True)
            dp = _dot_nt(doh, vh)
            ds = p * (dp - jnp.sum(dp * p, axis=-1, keepdims=True)) * (XA_HEAD_DIM ** -0.5)
            outs[0][:, cols] = _dot(ds, kh).astype(BF16)
            accs[0][:, cols] += _dot_tn(ds, qh)
            accs[0][:, vcols] += _dot_tn(p, doh)

    return row_call(name, body, q.shape[0], ROW_TILE, [(d_o, "tile"), (q, "tile"), (kv, "full")],
                    [(q.shape, BF16)], [kv.shape])


def adamw(name, w, g, m, v):
    def body(i, ins, outs, accs):
        wv, gv = ins[0][...], ins[1][...]
        mn = ADAM_B1 * ins[2][...] + (1.0 - ADAM_B1) * gv
        vn = ADAM_B2 * ins[3][...] + (1.0 - ADAM_B2) * jnp.square(gv)
        m_hat = mn / (1.0 - ADAM_B1 ** ADAM_STEP)
        v_hat = vn / (1.0 - ADAM_B2 ** ADAM_STEP)
        outs[0][...] = -ADAM_LR * (m_hat / (jnp.sqrt(v_hat) + ADAM_EPS) + ADAM_WD * wv)
        outs[1][...] = mn
        outs[2][...] = vn

    return row_call(name, body, w.shape[0], ROW_TILE, [(w, "tile"), (g, "tile"), (m, "tile"), (v, "tile")],
                    [(w.shape, F32)] * 3)


HBM_SPEC = pl.BlockSpec(memory_space=pltpu.HBM)


def _position():
    return lax.axis_index("x"), lax.axis_index("y"), lax.axis_index("c")


def _other_chips(x, y):
    return [(1 - x, y), (x, 1 - y), (1 - x, 1 - y)]


def _row_chunks(rows):
    return rows // D2D_CHUNK_ROWS if rows % D2D_CHUNK_ROWS == 0 else 1


def _start_chunked(make, rows):
    k = _row_chunks(rows)
    for i in range(k):
        make(i * (rows // k), rows // k).start()


def gather_shards(packs):
    n = len(packs)

    def body(*refs):
        srcs, outs = refs[:n], refs[n:2 * n]
        send_sems, recv_sems = refs[2 * n:]
        x, y, c = _position()
        me = 2 * x + y
        chips = _other_chips(x, y)
        sibling = (x, y, 1 - c)

        def over_ici(a, j):
            px, py = chips[j]
            rows = srcs[a].shape[0] // 2
            return pltpu.make_async_remote_copy(
                src_ref=srcs[a].at[pl.ds(c * rows, rows), :], dst_ref=outs[a].at[me, pl.ds(c * rows, rows), :],
                send_sem=send_sems.at[a, j], recv_sem=recv_sems.at[a, j], device_id=(px, py, c), device_id_type=MESH)

        def landed(a, j):
            px, py = chips[j]
            rows = srcs[a].shape[0] // 2
            part = outs[a].at[2 * px + py, pl.ds(c * rows, rows), :]
            return pltpu.make_async_remote_copy(
                src_ref=part, dst_ref=part, send_sem=send_sems.at[a, j], recv_sem=recv_sems.at[a, j],
                device_id=(px, py, c), device_id_type=MESH)

        def over_d2d(a, j, cc, off, size):
            px, py = chips[j]
            rows = srcs[a].shape[0] // 2
            part = outs[a].at[2 * px + py, pl.ds(cc * rows + off, size), :]
            return pltpu.make_async_remote_copy(
                src_ref=part, dst_ref=part, send_sem=send_sems.at[a, 3 + j], recv_sem=recv_sems.at[a, 3 + j],
                device_id=sibling, device_id_type=MESH)

        for a in range(n):
            for j in range(3):
                over_ici(a, j).start()
        for a in range(n):
            for j in range(3):
                landed(a, j).wait_recv()
                _start_chunked(functools.partial(over_d2d, a, j, c), srcs[a].shape[0] // 2)
        for a in range(n):
            rows = srcs[a].shape[0] // 2
            for j in range(3):
                over_d2d(a, j, 1 - c, 0, rows).wait_recv()
                over_d2d(a, j, c, 0, rows).wait_send()
                over_ici(a, j).wait_send()

    return pl.pallas_call(
        body, name="gather_shards",
        in_specs=[HBM_SPEC] * n, out_specs=[HBM_SPEC] * n,
        out_shape=[jax.ShapeDtypeStruct((N_CHIPS,) + p.shape, p.dtype) for p in packs],
        scratch_shapes=[pltpu.SemaphoreType.DMA((n, 6)), pltpu.SemaphoreType.DMA((n, 6))],
    )(*packs)


def pair_split(packs):
    n = len(packs)

    def body(*refs):
        srcs, outs = refs[:n], refs[n:2 * n]
        send_sems, recv_sems = refs[2 * n:]
        x, y, c = _position()

        def remote(a, off, size):
            rows = srcs[a].shape[1] // 2
            return pltpu.make_async_remote_copy(
                src_ref=srcs[a].at[:, pl.ds((1 - c) * rows + off, size), :],
                dst_ref=outs[a].at[:, pl.ds(off, size), :],
                send_sem=send_sems.at[a], recv_sem=recv_sems.at[a], device_id=(x, y, 1 - c), device_id_type=MESH)

        for a in range(n):
            _start_chunked(functools.partial(remote, a), srcs[a].shape[1] // 2)
        for a in range(n):
            remote(a, 0, srcs[a].shape[1] // 2).wait()

    return pl.pallas_call(
        body, name="pair_split", in_specs=[HBM_SPEC] * n, out_specs=[HBM_SPEC] * n,
        out_shape=[jax.ShapeDtypeStruct((p.shape[0], p.shape[1] // 2, p.shape[2]), p.dtype) for p in packs],
        scratch_shapes=[pltpu.SemaphoreType.DMA((n,)), pltpu.SemaphoreType.DMA((n,))],
    )(*packs)


def chip_scatter(packs):
    n = len(packs)

    def body(*refs):
        srcs, outs = refs[:n], refs[n:2 * n]
        send_sems, recv_sems = refs[2 * n:]
        x, y, c = _position()
        copies = []
        for a in range(n):
            for j, (px, py) in enumerate(_other_chips(x, y)):
                cp = pltpu.make_async_remote_copy(
                    src_ref=srcs[a].at[2 * px + py], dst_ref=outs[a].at[j],
                    send_sem=send_sems.at[a, j], recv_sem=recv_sems.at[a, j],
                    device_id=(px, py, c), device_id_type=MESH)
                cp.start()
                copies.append(cp)
        for cp in copies:
            cp.wait()

    return pl.pallas_call(
        body, name="chip_scatter", in_specs=[HBM_SPEC] * n, out_specs=[HBM_SPEC] * n,
        out_shape=[jax.ShapeDtypeStruct((N_CHIPS - 1,) + p.shape[1:], p.dtype) for p in packs],
        scratch_shapes=[pltpu.SemaphoreType.DMA((n, 3)), pltpu.SemaphoreType.DMA((n, 3))],
    )(*packs)


def pair_join(halves):
    n = len(halves)

    def body(*refs):
        srcs, outs = refs[:n], refs[n:2 * n]
        send_sems, recv_sems = refs[2 * n:]
        x, y, c = _position()

        def remote(a, off, size):
            return pltpu.make_async_remote_copy(
                src_ref=srcs[a].at[pl.ds(off, size), :], dst_ref=outs[a].at[pl.ds(off, size), :],
                send_sem=send_sems.at[a], recv_sem=recv_sems.at[a], device_id=(x, y, 1 - c), device_id_type=MESH)

        for a in range(n):
            _start_chunked(functools.partial(remote, a), srcs[a].shape[0])
        for a in range(n):
            remote(a, 0, srcs[a].shape[0]).wait()

    return pl.pallas_call(
        body, name="pair_join", in_specs=[HBM_SPEC] * n, out_specs=[HBM_SPEC] * n,
        out_shape=[jax.ShapeDtypeStruct(p.shape, p.dtype) for p in halves],
        scratch_shapes=[pltpu.SemaphoreType.DMA((n,)), pltpu.SemaphoreType.DMA((n,))],
    )(*halves)


def all_sum_small(part):
    n_dev = 8
    rows = part.shape[0]

    def body(src, out, buf, send_sems, recv_sems):
        x, y, c = _position()
        me = 4 * x + 2 * y + c
        buf[me] = src[...]
        copies = []
        for k in range(1, n_dev):
            px, py, pc = x ^ ((k >> 2) & 1), y ^ ((k >> 1) & 1), c ^ (k & 1)
            cp = pltpu.make_async_remote_copy(
                src_ref=src, dst_ref=buf.at[me], send_sem=send_sems.at[k - 1], recv_sem=recv_sems.at[k - 1],
                device_id=(px, py, pc), device_id_type=MESH)
            cp.start()
            copies.append(cp)
        for cp in copies:
            cp.wait()
        acc = buf[0]
        for k in range(1, n_dev):
            acc = acc + buf[k]
        out[...] = acc

    return pl.pallas_call(
        body, name="all_sum_small",
        in_specs=[pl.BlockSpec(memory_space=pltpu.VMEM)], out_specs=pl.BlockSpec(memory_space=pltpu.VMEM),
        out_shape=jax.ShapeDtypeStruct(part.shape, F32),
        scratch_shapes=[pltpu.VMEM((n_dev, rows, part.shape[1]), F32),
                        pltpu.SemaphoreType.DMA((n_dev - 1,)), pltpu.SemaphoreType.DMA((n_dev - 1,))],
    )(part)


def add_pairs(name, src, theirs, core, out_dtype):
    slabs, rows, cols = theirs.shape
    tm = _pick_tile(rows, ROW_TILE)
    nb = rows // tm

    def body(core_ref, a_ref, b_ref, o_ref):
        o_ref[...] = (a_ref[...].astype(F32) + b_ref[...].astype(F32)).astype(out_dtype)

    return pl.pallas_call(
        body, name=name,
        grid_spec=pltpu.PrefetchScalarGridSpec(
            num_scalar_prefetch=1, grid=(slabs, nb),
            in_specs=[pl.BlockSpec((None, tm, cols), lambda s, i, core_ref: (s, core_ref[0] * nb + i, 0)),
                      pl.BlockSpec((None, tm, cols), lambda s, i, core_ref: (s, i, 0))],
            out_specs=pl.BlockSpec((None, tm, cols), lambda s, i, core_ref: (s, i, 0))),
        out_shape=jax.ShapeDtypeStruct(theirs.shape, out_dtype),
        compiler_params=_cparams(("parallel", "parallel")),
    )(core, src, theirs)


def add_four(name, src, theirs, chip):
    _, rows, cols = theirs.shape
    tm = _pick_tile(rows, ROW_TILE)

    def body(chip_ref, a_ref, b_ref, o_ref):
        acc = a_ref[...].astype(F32)
        for j in range(N_CHIPS - 1):
            acc = acc + b_ref[j].astype(F32)
        o_ref[...] = acc

    return pl.pallas_call(
        body, name=name,
        grid_spec=pltpu.PrefetchScalarGridSpec(
            num_scalar_prefetch=1, grid=(rows // tm,),
            in_specs=[pl.BlockSpec((None, tm, cols), lambda i, chip_ref: (chip_ref[0], i, 0)),
                      pl.BlockSpec((N_CHIPS - 1, tm, cols), lambda i, chip_ref: (0, i, 0))],
            out_specs=pl.BlockSpec((tm, cols), lambda i, chip_ref: (i, 0))),
        out_shape=jax.ShapeDtypeStruct((rows, cols), F32),
        compiler_params=_cparams(("parallel",)),
    )(chip, src, theirs)


PACK_COLS = 1024
BIG_ROW_MULTIPLE = 512
SMALL_ROW_MULTIPLE = 32
BIG = ["dn_w_in", "dn_w_out", "cv_w_pw1", "cv_w_pw2", "xa_w_q", "xa_w_kv", "xa_w_o", "mlp_w_up", "mlp_w_down"]
SMALL = ["dn_w_conv", "cv_norm", "cv_b_pw1", "cv_w_dw", "cv_b_dw", "cv_ln_g", "cv_ln_b", "cv_b_pw2"]
SHARD_AXIS = {"dn_w_in": 2, "dn_w_conv": 2, "dn_w_out": 1, "cv_norm": 1, "cv_w_pw1": 2, "cv_b_pw1": 1,
              "cv_w_dw": 2, "cv_b_dw": 1, "cv_ln_g": 1, "cv_ln_b": 1, "cv_w_pw2": 1, "cv_b_pw2": 1,
              "xa_w_q": 1, "xa_w_kv": 2, "xa_w_o": 1, "mlp_w_up": 2, "mlp_w_down": 1}
REPLICATED = ["dn_norm", "dn_a_log", "dn_dt_bias", "dn_out_norm", "xa_norm", "xa_mem_norm", "mlp_norm", "final_norm"]


def _pack_rows(size):
    return -(-size // PACK_COLS)


SHARD_SHAPES = {
    "dn_w_in": (1, 1024, 1028), "dn_w_conv": (1, 4, 768), "dn_w_out": (1, 256, 1024), "cv_norm": (1, 256),
    "cv_w_pw1": (1, 1024, 512), "cv_b_pw1": (1, 512), "cv_w_dw": (1, 31, 256), "cv_b_dw": (1, 256),
    "cv_ln_g": (1, 256), "cv_ln_b": (1, 256), "cv_w_pw2": (1, 256, 1024), "cv_b_pw2": (1, 256),
    "xa_w_q": (2, 256, 1024), "xa_w_kv": (2, 1024, 512), "xa_w_o": (2, 256, 1024),
    "mlp_w_up": (2, 1024, 1024), "mlp_w_down": (2, 1024, 1024)}


def _shard_shape(nm):
    return SHARD_SHAPES[nm]


def _pack(tensors, names, dtype, row_multiple):
    pieces = []
    for nm in names:
        t = tensors[nm]
        flat = t.reshape(t.shape[0], -1) if t.ndim > len(_shard_shape(nm)) else t.reshape(1, -1)
        pad = _pack_rows(flat.shape[1]) * PACK_COLS - flat.shape[1]
        pieces.append(jnp.pad(flat.astype(dtype), ((0, 0), (0, pad))))
    cat = jnp.concatenate(pieces, axis=1)
    rows = cat.shape[1] // PACK_COLS
    total = -(-rows // row_multiple) * row_multiple
    cat = jnp.pad(cat, ((0, 0), (0, (total - rows) * PACK_COLS)))
    return cat.reshape(cat.shape[0], total, PACK_COLS)


def _unpack(pack, names):
    lead = pack.shape[:-2]
    flat = pack.reshape(lead + (-1,))
    out, off = {}, 0
    for nm in names:
        shp = _shard_shape(nm)
        size = 1
        for s in shp:
            size *= s
        out[nm] = flat[..., off:off + size].reshape(lead + shp)
        off += _pack_rows(size) * PACK_COLS
    return out


def _to_full(nm, stacked):
    ax = SHARD_AXIS[nm]
    moved = jnp.moveaxis(stacked, 0, ax)
    shp = list(_shard_shape(nm))
    shp[ax] *= N_CHIPS
    return moved.reshape(shp)


def _to_shards(nm, full):
    ax = SHARD_AXIS[nm]
    shp = list(_shard_shape(nm))
    split = full.reshape(shp[:ax] + [N_CHIPS, shp[ax]] + shp[ax + 1:])
    return jnp.moveaxis(split, ax, 0)


def _row(v):
    return v.reshape(1, -1)


def mlp_fwd(tag, h, g, w_up, w_down):
    n = rms_fwd(tag + "_norm", h, g)
    up = mm(tag + "_up", n, w_up)
    out = mm(tag + "_down", up, w_down, pro=lambda t: jnp.square(jnp.maximum(t, 0.0)),
             epi=lambda acc, res: acc + res, epi_tiles=(h,))
    return out, (n, up)


def mlp_bwd(tag, dh, h, g, w_up, w_down, saved):
    n, up = saved
    dup = mm(tag + "_d_act", dh, w_down, tb=True, out_dtype=BF16,
             epi=lambda acc, t: acc * (2.0 * jnp.maximum(t, 0.0)), epi_tiles=(up,))
    dw_down = mm(tag + "_dw_down", up, dh, ta=True, pro=lambda t: jnp.square(jnp.maximum(t, 0.0)), tk=512)
    dn = mm(tag + "_dn", dup, w_up, tb=True)
    dw_up = mm(tag + "_dw_up", n, dup, ta=True, tk=512)
    dh_in, dg = rms_bwd(tag + "_norm_bwd", dn, h, g, dh)
    return dh_in, dg, dw_up, dw_down


def xa_fwd(tag, h, mem, g, g_mem, w_q, w_kv, w_o):
    n = rms_fwd(tag + "_norm", h, g)
    mem_n = rms_fwd(tag + "_mem_norm", mem, g_mem)
    q = mm(tag + "_q", n, w_q, out_dtype=BF16)
    kv = mm(tag + "_kv", mem_n, w_kv, out_dtype=BF16)
    o = xa_core_fwd(tag + "_core", q, kv)
    out = mm(tag + "_o", o, w_o, epi=lambda acc, res: acc + res, epi_tiles=(h,))
    return out, (n, mem_n, q, kv, o)


def xa_bwd(tag, dh, h, mem, g, g_mem, w_q, w_kv, w_o, saved):
    n, mem_n, q, kv, o = saved
    d_o = mm(tag + "_d_o", dh, w_o, tb=True, out_dtype=BF16)
    dw_o = mm(tag + "_dw_o", o, dh, ta=True, tk=512)
    dq, dkv = xa_core_bwd(tag + "_core_bwd", d_o, q, kv)
    dn = mm(tag + "_dn", dq, w_q, tb=True)
    dw_q = mm(tag + "_dw_q", n, dq, ta=True, tk=512)
    dh_in, dg = rms_bwd(tag + "_norm_bwd", dn, h, g, dh)
    dw_kv = mm(tag + "_dw_kv", mem_n, dkv, ta=True)
    dmem_n = mm(tag + "_dmem", dkv, w_kv, tb=True)
    dg_mem = mem_norm_bwd(tag + "_mem_norm_bwd", dmem_n, mem, g_mem)
    return dh_in, dg, dg_mem, dw_q, dw_kv, dw_o


def _gate_tile(a_log, dt_bias):
    t = jnp.zeros((8, LANES), F32)
    t = t.at[0, DN_HEADS:2 * DN_HEADS].set(a_log.reshape(-1))
    return t.at[1, DN_HEADS:2 * DN_HEADS].set(dt_bias.reshape(-1))


def dn_fwd(h, g, w_qkv, w_z, w_ba, w_conv, gate, out_norm, w_out):
    n = rms_fwd("dn_norm", h, g)
    qkv_raw = mm("dn_proj_qkv", n, w_qkv)
    z = mm("dn_proj_z", n, w_z)
    ba = mm("dn_proj_ba", n, w_ba)
    qkv, hs = dn_pre(qkv_raw, ba, w_conv, gate)
    u, w, t_inv = dn_solve(qkv, hs)
    o, states = dn_scan_fwd(qkv, u, w, hs)
    og = dn_post(o, z, out_norm)
    out = mm("dn_out", og, w_out, epi=lambda acc, res: acc + res, epi_tiles=(h,))
    return out, (n, qkv_raw, z, ba, qkv, hs, u, w, t_inv, o, states, og)


def dn_bwd(dh, h, g, w_qkv, w_z, w_ba, w_conv, gate, out_norm, w_out, saved):
    n, qkv_raw, z, ba, qkv, hs, u, w, t_inv, o, states, og = saved
    d_og = mm("dn_d_og", dh, w_out, tb=True, out_dtype=BF16)
    dw_out = mm("dn_dw_out", og, dh, ta=True, tk=512)
    d_o, dz, d_out_norm = dn_post_bwd(d_og, o, z, out_norm)
    dq, dk, dv, dhs = dn_scan_bwd(qkv, u, w, t_inv, hs, states, d_o)
    dc, dba, d_gate = dn_pre_bwd(dq, dk, dv, dhs, qkv_raw, ba, w_conv, gate)
    dqkv_raw, dw_conv = dn_conv_bwd(dc, qkv_raw, w_conv)
    dn = mm("dn_dn_qkv", dqkv_raw, w_qkv, tb=True)
    dn = mm("dn_dn_z", dz, w_z, tb=True, epi=lambda acc, t: acc + t, epi_tiles=(dn,))
    dn = mm("dn_dn_ba", dba, w_ba, tb=True, epi=lambda acc, t: acc + t, epi_tiles=(dn,))
    dw_qkv = mm("dn_dw_qkv", n, dqkv_raw, ta=True, tk=512)
    dw_z = mm("dn_dw_z", n, dz, ta=True, tk=512)
    dw_ba = mm("dn_dw_ba", n, dba, ta=True, tk=512)
    dh_in, dg = rms_bwd("dn_norm_bwd", dn, h, g, dh)
    return dh_in, dg, dw_qkv, dw_z, dw_ba, dw_conv, d_gate, d_out_norm, dw_out


def cv_fwd(h, g, w_pw1, b_pw1, w_dw, b_dw, ln_g, ln_b, w_pw2, b_pw2):
    n = rms_fwd("cv_norm", h, g)
    u = mm("cv_pw1", n, w_pw1, epi=lambda acc, b: acc + b, epi_rows=(b_pw1,))
    s, c = cv_core_fwd(u, w_dw, b_dw, ln_g, ln_b)
    out = mm("cv_pw2", s, w_pw2, epi=lambda acc, res, b: acc + res + b, epi_tiles=(h,), epi_rows=(b_pw2,))
    return out, (n, u, s, c)


def cv_bwd(dh, h, g, w_pw1, w_dw, ln_g, ln_b, w_pw2, saved):
    n, u, s, c = saved
    ds = mm("cv_d_s", dh, w_pw2, tb=True, out_dtype=BF16)
    dw_pw2 = mm("cv_dw_pw2", s, dh, ta=True, tk=512)
    db_pw2 = col_sum("cv_db_pw2", dh)
    dc, ln_acc = cv_ln_bwd(ds, c, ln_g, ln_b)
    du, dw_dw, db_pw1 = cv_conv_bwd(dc, u, w_dw)
    dn = mm("cv_dn", du, w_pw1, tb=True)
    dw_pw1 = mm("cv_dw_pw1", n, du, ta=True, tk=512)
    dh_in, dg = rms_bwd("cv_norm_bwd", dn, h, g, dh)
    return dh_in, dg, dw_pw1, db_pw1, dw_dw, ln_acc, dw_pw2, db_pw2


WEIGHTS = ["dn_norm", "dn_w_in", "dn_w_conv", "dn_a_log", "dn_dt_bias", "dn_out_norm", "dn_w_out", "cv_norm",
           "cv_w_pw1", "cv_b_pw1", "cv_w_dw", "cv_b_dw", "cv_ln_g", "cv_ln_b", "cv_w_pw2", "cv_b_pw2", "xa_norm",
           "xa_mem_norm", "xa_w_q", "xa_w_kv", "xa_w_o", "mlp_norm", "mlp_w_up", "mlp_w_down", "final_norm"]


def _as_2d(t):
    if t.ndim == 1:
        return t.reshape(1, -1)
    return t.reshape(-1, t.shape[-1])


def kernel(x, mem, dn_norm, dn_w_in, dn_w_conv, dn_a_log, dn_dt_bias, dn_out_norm, dn_w_out, cv_norm, cv_w_pw1, cv_b_pw1, cv_w_dw, cv_b_dw, cv_ln_g, cv_ln_b, cv_w_pw2, cv_b_pw2, xa_norm, xa_mem_norm, xa_w_q, xa_w_kv, xa_w_o, mlp_norm, mlp_w_up, mlp_w_down, final_norm, loss_target, m_dn_norm, m_dn_w_in, m_dn_w_conv, m_dn_a_log, m_dn_dt_bias, m_dn_out_norm, m_dn_w_out, m_cv_norm, m_cv_w_pw1, m_cv_b_pw1, m_cv_w_dw, m_cv_b_dw, m_cv_ln_g, m_cv_ln_b, m_cv_w_pw2, m_cv_b_pw2, m_xa_norm, m_xa_mem_norm, m_xa_w_q, m_xa_w_kv, m_xa_w_o, m_mlp_norm, m_mlp_w_up, m_mlp_w_down, m_final_norm, v_dn_norm, v_dn_w_in, v_dn_w_conv, v_dn_a_log, v_dn_dt_bias, v_dn_out_norm, v_dn_w_out, v_cv_norm, v_cv_w_pw1, v_cv_b_pw1, v_cv_w_dw, v_cv_b_dw, v_cv_ln_g, v_cv_ln_b, v_cv_w_pw2, v_cv_b_pw2, v_xa_norm, v_xa_mem_norm, v_xa_w_q, v_xa_w_kv, v_xa_w_o, v_mlp_norm, v_mlp_w_up, v_mlp_w_down, v_final_norm):
    args = dict(locals())
    wts = {nm: args[nm] for nm in WEIGHTS}
    mom = {nm: args["m_" + nm] for nm in WEIGHTS}
    var = {nm: args["v_" + nm] for nm in WEIGHTS}
    big_pack = _pack(wts, BIG, BF16, BIG_ROW_MULTIPLE)[0]
    small_pack = _pack(wts, SMALL, F32, SMALL_ROW_MULTIPLE)[0]
    core = lax.axis_index("c").astype(jnp.int32)
    chip = (2 * lax.axis_index("x") + lax.axis_index("y")).astype(jnp.int32)
    big_all, small_all = gather_shards([big_pack, small_pack])
    big_all = lax.dynamic_update_slice(big_all, big_pack[None], (chip, 0, 0))
    small_all = lax.dynamic_update_slice(small_all, small_pack[None], (chip, 0, 0))
    full = {nm: _to_full(nm, t) for nm, t in _unpack(big_all, BIG).items()}
    full.update({nm: _to_full(nm, t) for nm, t in _unpack(small_all, SMALL).items()})
    full.update({nm: wts[nm] for nm in REPLICATED})

    dh, grads, rep = local_step(x[0], mem[0], loss_target[0], full)

    shards = {nm: _to_shards(nm, grads[nm]) for nm in BIG + SMALL}
    big_g = _pack(shards, BIG, BF16, BIG_ROW_MULTIPLE)
    small_g = _pack(shards, SMALL, F32, SMALL_ROW_MULTIPLE)
    big_theirs, small_theirs = pair_split([big_g, small_g])
    big_pair = add_pairs("pair_add_big", big_g, big_theirs, core.reshape(1), BF16)
    small_pair = add_pairs("pair_add_small", small_g, small_theirs, core.reshape(1), F32)
    big_others, small_others = chip_scatter([big_pair, small_pair])
    big_half = add_four("chip_add_big", big_pair, big_others, chip.reshape(1))
    small_half = add_four("chip_add_small", small_pair, small_others, chip.reshape(1))
    big_sib, small_sib = pair_join([big_half, small_half])

    def both_halves(mine, sibling):
        south = core == 0
        return jnp.concatenate([jnp.where(south, mine, sibling), jnp.where(south, sibling, mine)], axis=0)

    red = _unpack(both_halves(big_half, big_sib), BIG)
    red.update(_unpack(both_halves(small_half, small_sib), SMALL))

    rep = all_sum_small(rep)
    red["dn_norm"] = rep[0:1]
    red["dn_a_log"] = rep[1:2, DN_HEADS:2 * DN_HEADS]
    red["dn_dt_bias"] = rep[2:3, DN_HEADS:2 * DN_HEADS]
    red["dn_out_norm"] = rep[3:4, :LANES]
    red["xa_norm"], red["xa_mem_norm"], red["mlp_norm"] = rep[4:6], rep[6:8], rep[8:10]
    red["final_norm"] = rep[10]
    loss = rep[11, 0]

    delta, new_m, new_v = {}, {}, {}
    for nm in WEIGHTS:
        shp = wts[nm].shape
        res = adamw("adamw_" + nm, _as_2d(wts[nm]), _as_2d(red[nm].reshape(shp)), _as_2d(mom[nm]), _as_2d(var[nm]))
        delta[nm], new_m[nm], new_v[nm] = (r.reshape(shp) for r in res)
        red[nm] = red[nm].reshape(shp)

    grad_x = dh[None]
    return (loss, grad_x, *[red[nm] for nm in WEIGHTS], *[delta[nm] for nm in WEIGHTS],
            *[new_m[nm] for nm in WEIGHTS], *[new_v[nm] for nm in WEIGHTS])


def local_step(h0, mem0, target, full):
    d = h0.shape[1]
    dn_norm, dn_a_log, dn_dt_bias, dn_out_norm = (full[nm] for nm in REPLICATED[:4])
    xa_norm, xa_mem_norm, mlp_norm, final_norm = (full[nm] for nm in REPLICATED[4:])
    inner = DN_HEADS * DN_HEAD_DIM
    w_in = full["dn_w_in"][0]
    w_qkv, w_z = w_in[:, :3 * inner], w_in[:, 3 * inner:4 * inner]
    w_ba = jnp.pad(w_in[:, 4 * inner:], ((0, 0), (0, LANES - 2 * DN_HEADS)))
    w_conv = jnp.pad(full["dn_w_conv"][0], ((0, 8 - DN_CONV), (0, 0)))
    gate = _gate_tile(dn_a_log, dn_dt_bias)
    w_dw = jnp.pad(full["cv_w_dw"][0], ((0, CV_HALO - CV_WIDTH), (0, 0)))

    dn_args = (_row(dn_norm), w_qkv, w_z, w_ba, w_conv, gate, _row(dn_out_norm), full["dn_w_out"][0])
    h1, dn_saved = dn_fwd(h0, *dn_args)
    xa_args = [(_row(xa_norm[l]), _row(xa_mem_norm[l]), full["xa_w_q"][l], full["xa_w_kv"][l], full["xa_w_o"][l])
               for l in range(2)]
    mlp_args = [(_row(mlp_norm[l]), full["mlp_w_up"][l], full["mlp_w_down"][l]) for l in range(2)]
    h2, xa0_saved = xa_fwd("xa0", h1, mem0, *xa_args[0])
    h3, mlp0_saved = mlp_fwd("mlp0", h2, *mlp_args[0])
    cv_args = (_row(full["cv_norm"][0]), full["cv_w_pw1"][0], full["cv_b_pw1"], w_dw, full["cv_b_dw"],
               full["cv_ln_g"], full["cv_ln_b"], full["cv_w_pw2"][0], full["cv_b_pw2"])
    h4, cv_saved = cv_fwd(h3, *cv_args)
    h5, xa1_saved = xa_fwd("xa1", h4, mem0, *xa_args[1])
    h6, mlp1_saved = mlp_fwd("mlp1", h5, *mlp_args[1])

    dh, loss_tile, d_final = loss_head("loss_head", h6, _row(final_norm), target)
    grads = {}
    dg_mlp, dg_xa, dg_xa_mem = [None, None], [None, None], [None, None]
    dw_up, dw_down, dw_q, dw_kv, dw_o = ([None, None] for _ in range(5))
    dh, dg_mlp[1], dw_up[1], dw_down[1] = mlp_bwd("mlp1", dh, h5, *mlp_args[1], mlp1_saved)
    dh, dg_xa[1], dg_xa_mem[1], dw_q[1], dw_kv[1], dw_o[1] = xa_bwd("xa1", dh, h4, mem0, *xa_args[1], xa1_saved)
    (dh, grads["cv_norm"], grads["cv_w_pw1"], grads["cv_b_pw1"], dw_dw, ln_acc, grads["cv_w_pw2"],
     grads["cv_b_pw2"]) = cv_bwd(dh, h3, cv_args[0], cv_args[1], w_dw, cv_args[5], cv_args[6], cv_args[7], cv_saved)
    dh, dg_mlp[0], dw_up[0], dw_down[0] = mlp_bwd("mlp0", dh, h2, *mlp_args[0], mlp0_saved)
    dh, dg_xa[0], dg_xa_mem[0], dw_q[0], dw_kv[0], dw_o[0] = xa_bwd("xa0", dh, h1, mem0, *xa_args[0], xa0_saved)
    dh, dg_dn, dw_qkv, dw_z, dw_ba, dw_conv, d_gate, d_out_norm, dw_out = dn_bwd(dh, h0, *dn_args, dn_saved)

    grads["dn_w_in"] = jnp.concatenate([dw_qkv, dw_z, dw_ba[:, :2 * DN_HEADS]], axis=1)[None]
    grads["dn_w_conv"] = dw_conv[None, :DN_CONV]
    grads["dn_w_out"] = dw_out[None]
    grads["cv_norm"] = grads["cv_norm"]
    grads["cv_w_pw1"] = grads["cv_w_pw1"][None]
    grads["cv_w_dw"] = dw_dw[None, :CV_WIDTH]
    grads["cv_ln_g"], grads["cv_ln_b"], grads["cv_b_dw"] = ln_acc[0:1], ln_acc[1:2], ln_acc[2:3]
    grads["cv_w_pw2"] = grads["cv_w_pw2"][None]
    grads["xa_w_q"], grads["xa_w_kv"], grads["xa_w_o"] = jnp.stack(dw_q), jnp.stack(dw_kv), jnp.stack(dw_o)
    grads["mlp_w_up"], grads["mlp_w_down"] = jnp.stack(dw_up), jnp.stack(dw_down)

    rep = jnp.zeros((16, d), F32)
    rep = rep.at[0].set(dg_dn[0])
    rep = rep.at[1, :LANES].set(d_gate[0])
    rep = rep.at[2, :LANES].set(d_gate[1])
    rep = rep.at[3, :LANES].set(d_out_norm[0])
    rep = rep.at[4].set(dg_xa[0][0]).at[5].set(dg_xa[1][0])
    rep = rep.at[6].set(dg_xa_mem[0][0]).at[7].set(dg_xa_mem[1][0])
    rep = rep.at[8].set(dg_mlp[0][0]).at[9].set(dg_mlp[1][0])
    rep = rep.at[10].set(d_final[0])
    rep = rep.at[11, :LANES].set(loss_tile[0])
    return dh, grads, rep
```

```python
import functools

import jax
import jax.numpy as jnp
from jax import lax
from jax.experimental import pallas as pl
from jax.experimental.pallas import tpu as pltpu

F32 = jnp.float32
BF16 = jnp.bfloat16
HIGHEST = lax.Precision.HIGHEST
MESH = pl.DeviceIdType.MESH

D_MODEL = 1024
DN_HEADS = 8
DN_HEAD_DIM = 128
DN_CONV = 4
DN_CHUNK = 64
CV_WIDTH = 31
XA_HEADS = 4
XA_HEAD_DIM = 256
RMS_EPS = 1e-6
LN_EPS = 1e-5
L2_EPS = 1e-6

ADAM_LR = 0.001
ADAM_B1 = 0.9
ADAM_B2 = 0.999
ADAM_EPS = 1e-08
ADAM_WD = 0.01
ADAM_STEP = 10

LANES = 128
ROW_TILE = 512
CONV_ROW_TILE = 256
ADAMW_ROW_TILE = 256
DN_ROW_TILE = 256
CHUNK_SHIFT = 6
SOLVE_INTERLEAVE = 8
FWD_HEADS_PER_STEP = 4
BWD_HEADS_PER_STEP = 4
DN_HALO = 8
CV_HALO = 32
VMEM_LIMIT = 48 * 1024 * 1024
N_CHIPS = 4
D2D_CHUNK_ROWS = 256


def _cparams(sem):
    return pltpu.CompilerParams(dimension_semantics=sem, vmem_limit_bytes=VMEM_LIMIT)


def _dot(a, b, dims=(((1,), (0,)), ((), ()))):
    return lax.dot_general(a.astype(BF16), b.astype(BF16), dims, preferred_element_type=F32)


def _dot_nt(a, b):
    return _dot(a, b, (((1,), (1,)), ((), ())))


def _dot_tn(a, b):
    return _dot(a, b, (((0,), (0,)), ((), ())))


def _dot_hi(a, b, dims=(((1,), (0,)), ((), ()))):
    return lax.dot_general(a.astype(F32), b.astype(F32), dims, precision=HIGHEST, preferred_element_type=F32)


def _sigmoid(x):
    return 1.0 / (1.0 + jnp.exp(-x))


def _silu(x):
    return x * _sigmoid(x)


def _silu_grad(x):
    s = _sigmoid(x)
    return s * (1.0 + x * (1.0 - s))


def _softplus(x):
    return jnp.maximum(x, 0.0) + jnp.log(1.0 + jnp.exp(-jnp.abs(x)))


def _iota(shape, dim):
    return lax.broadcasted_iota(jnp.int32, shape, dim)


def _lane_col(vals, lane, idx):
    return jnp.sum(jnp.where(lane == idx, vals, 0.0), axis=1, keepdims=True)


def _pick_tile(rows, cap):
    best = rows
    for t in range(16, min(rows, cap) + 1, 16):
        if rows % t == 0:
            best = t
    return best


def _stacked_spec(shape, split, layer, rows, cols, block_index):
    per_chip = (shape[-2] // rows) if split == "rows" else (shape[-1] // cols)
    assert (shape[-2] % rows == 0) and (shape[-1] % cols == 0)
    lead = (None,) if layer is None else (None, None)

    def index(i, j, kk):
        bi, bj = block_index(i, j, kk)
        mid = () if layer is None else (layer,)
        if split == "rows":
            return (bi // per_chip,) + mid + (bi % per_chip, bj)
        return (bj // per_chip,) + mid + (bi, bj % per_chip)

    return pl.BlockSpec(lead + (rows, cols), index)


def mm(name, a, b, *, ta=False, tb=False, out_dtype=F32, pro=None, epi=None, epi_tiles=(), epi_rows=(),
       tm=512, tn=512, tk=1024, b_split=None, b_layer=None, out_split=None, out_layer=None, out_into=None):
    m, k = (a.shape[1], a.shape[0]) if ta else a.shape
    b_rows, b_cols = b.shape[-2], b.shape[-1]
    if b_split == "rows":
        b_rows *= N_CHIPS
    elif b_split == "cols":
        b_cols *= N_CHIPS
    n = b_rows if tb else b_cols
    assert (b_cols if tb else b_rows) == k
    tm, tn, tk = min(tm, m), min(tn, n), min(tk, k)
    if b_split is not None:
        lim_r, lim_c = b.shape[-2], b.shape[-1]
        if tb:
            tn, tk = min(tn, lim_r), min(tk, lim_c)
        else:
            tk, tn = min(tk, lim_r), min(tn, lim_c)
    if out_split == "rows":
        tm = min(tm, m // N_CHIPS)
    elif out_split == "cols":
        tn = min(tn, n // N_CHIPS)
    assert m % tm == 0 and n % tn == 0 and k % tk == 0
    nk = k // tk
    a_spec = pl.BlockSpec((tk, tm), lambda i, j, kk: (kk, i)) if ta else pl.BlockSpec((tm, tk), lambda i, j, kk: (i, kk))
    b_block = (tn, tk) if tb else (tk, tn)
    b_index = (lambda i, j, kk: (j, kk)) if tb else (lambda i, j, kk: (kk, j))
    if b_split is None:
        b_spec = pl.BlockSpec(b_block, b_index)
    else:
        b_spec = _stacked_spec(b.shape, b_split, b_layer, b_block[0], b_block[1], b_index)
    in_specs = [a_spec, b_spec]
    in_specs += [pl.BlockSpec((tm, tn), lambda i, j, kk: (i, j)) for _ in epi_tiles]
    in_specs += [pl.BlockSpec((1, tn), lambda i, j, kk: (0, j)) for _ in epi_rows]
    n_t, n_r = len(epi_tiles), len(epi_rows)
    dims = (((0 if ta else 1,), (1 if tb else 0,)), ((), ()))
    if out_split is None:
        out_shape = jax.ShapeDtypeStruct((m, n), out_dtype)
        out_spec = pl.BlockSpec((tm, tn), lambda i, j, kk: (i, j))
    else:
        shard = (m // N_CHIPS, n) if out_split == "rows" else (m, n // N_CHIPS)
        layers = () if out_layer is None else (out_layer[1],)
        out_shape = jax.ShapeDtypeStruct((N_CHIPS,) + layers + shard, out_dtype)
        out_spec = _stacked_spec(out_shape.shape, out_split, None if out_layer is None else out_layer[0], tm, tn,
                                 lambda i, j, kk: (i, j))
    extra, aliases = [], {}
    if out_into is not None:
        extra = [out_into]
        in_specs.append(pl.BlockSpec(memory_space=pl.ANY))
        aliases = {2 + n_t + n_r: 0}

    def body(a_ref, b_ref, *rest):
        tiles = rest[:n_t]
        rows = rest[n_t:n_t + n_r]
        rest = rest[n_t + n_r + len(extra):]
        o_ref, acc_ref = rest[0], rest[1]
        kk = pl.program_id(2)

        @pl.when(kk == 0)
        def _():
            acc_ref[...] = jnp.zeros_like(acc_ref)

        av = a_ref[...]
        if pro is not None:
            av = pro(av)
        acc_ref[...] += _dot(av, b_ref[...], dims)

        @pl.when(kk == nk - 1)
        def _():
            out = acc_ref[...]
            if epi is not None:
                out = epi(out, *[t[...] for t in tiles], *[r[...] for r in rows])
            o_ref[...] = out.astype(out_dtype)

    return pl.pallas_call(
        body, name=name, grid=(m // tm, n // tn, nk),
        in_specs=in_specs, out_specs=out_spec, out_shape=out_shape,
        scratch_shapes=[pltpu.VMEM((tm, tn), F32)], input_output_aliases=aliases,
        compiler_params=_cparams(("parallel", "parallel", "arbitrary")),
    )(a, b, *epi_tiles, *epi_rows, *extra)


def row_call(name, body, n_rows, tm, ins, outs, accs=()):
    tm = _pick_tile(n_rows, tm)
    in_specs = []
    for arr, kind in ins:
        if kind == "tile":
            if arr.ndim == 2:
                in_specs.append(pl.BlockSpec((tm, arr.shape[1]), lambda i: (i, 0)))
            else:
                in_specs.append(pl.BlockSpec((arr.shape[0], tm, arr.shape[2]), lambda i: (0, i, 0)))
        elif kind == "full":
            in_specs.append(pl.BlockSpec(arr.shape, functools.partial(lambda i, nd: (0,) * nd, nd=arr.ndim)))
        else:
            where, h = kind
            per = tm // h
            if where == "prev":
                in_specs.append(pl.BlockSpec((h, arr.shape[1]), functools.partial(
                    lambda i, per: (jnp.maximum(i * per - 1, 0), 0), per=per)))
            else:
                last = n_rows // h - 1
                in_specs.append(pl.BlockSpec((h, arr.shape[1]), functools.partial(
                    lambda i, per, last: (jnp.minimum((i + 1) * per, last), 0), per=per, last=last)))
    out_shape, out_specs = [], []
    for shape, dtype in outs:
        out_shape.append(jax.ShapeDtypeStruct(shape, dtype))
        if len(shape) == 2:
            out_specs.append(pl.BlockSpec((tm, shape[1]), lambda i: (i, 0)))
        else:
            out_specs.append(pl.BlockSpec((shape[0], tm, shape[2]), lambda i: (0, i, 0)))
    for shape in accs:
        out_shape.append(jax.ShapeDtypeStruct(shape, F32))
        out_specs.append(pl.BlockSpec(shape, lambda i: (0, 0)))
    n_in, n_out, n_acc = len(ins), len(outs), len(accs)

    def kern(*refs):
        i = pl.program_id(0)
        in_refs = refs[:n_in]
        out_refs = refs[n_in:n_in + n_out]
        acc_refs = refs[n_in + n_out:n_in + n_out + n_acc]
        if n_acc:
            @pl.when(i == 0)
            def _():
                for r in acc_refs:
                    r[...] = jnp.zeros_like(r)
        body(i, in_refs, out_refs, acc_refs)

    res = pl.pallas_call(
        kern, name=name, grid=(n_rows // tm,), in_specs=in_specs, out_specs=out_specs, out_shape=out_shape,
        compiler_params=_cparams(("arbitrary",) if n_acc else ("parallel",)),
    )(*[a for a, _ in ins])
    return list(res)


def _rms_stats(h):
    r = lax.rsqrt(jnp.mean(h * h, axis=-1, keepdims=True) + RMS_EPS)
    return h * r, r


def rms_fwd(name, h, g):
    def body(i, ins, outs, accs):
        xhat, _ = _rms_stats(ins[0][...])
        outs[0][...] = (xhat * ins[1][...]).astype(BF16)

    return row_call(name, body, h.shape[0], ROW_TILE, [(h, "tile"), (g, "full")], [(h.shape, BF16)])[0]


def _rms_bwd_tile(dn, h, g):
    xhat, r = _rms_stats(h)
    dxhat = dn * g
    dh = r * (dxhat - xhat * jnp.mean(dxhat * xhat, axis=-1, keepdims=True))
    dg = jnp.sum(dn * xhat, axis=0, keepdims=True)
    return dh, dg


def rms_bwd(name, dn, h, g, dres):
    def body(i, ins, outs, accs):
        dh, dg = _rms_bwd_tile(ins[0][...].astype(F32), ins[1][...], ins[2][...])
        outs[0][...] = ins[3][...] + dh
        accs[0][...] += dg

    d = h.shape[1]
    out, dg = row_call(name, body, h.shape[0], ROW_TILE,
                       [(dn, "tile"), (h, "tile"), (g, "full"), (dres, "tile")], [(h.shape, F32)], [(1, d)])
    return out, dg


def mem_norm_bwd(name, dn, mem, g):
    def body(i, ins, outs, accs):
        _, dg = _rms_bwd_tile(ins[0][...].astype(F32), ins[1][...], ins[2][...])
        accs[0][...] += dg

    return row_call(name, body, mem.shape[0], ROW_TILE, [(dn, "tile"), (mem, "tile"), (g, "full")], [],
                    [(1, mem.shape[1])])[0]


def loss_head(name, h, g, target):
    d = h.shape[1]

    def body(i, ins, outs, accs):
        hv, gv = ins[0][...], ins[1][...]
        xhat, _ = _rms_stats(hv)
        err = xhat * gv - ins[2][...]
        dy = err * (1.0 / d)
        dh, dg = _rms_bwd_tile(dy, hv, gv)
        outs[0][...] = dh
        accs[0][...] += jnp.full((8, LANES), 0.5 / d, F32) * jnp.sum(err * err)
        accs[1][...] += dg

    dh, loss, dg = row_call(name, body, h.shape[0], ROW_TILE, [(h, "tile"), (g, "full"), (target, "tile")],
                            [(h.shape, F32)], [(8, LANES), (1, d)])
    return dh, loss, dg


def col_sum(name, x):
    def body(i, ins, outs, accs):
        accs[0][...] += jnp.sum(ins[0][...].astype(F32), axis=0, keepdims=True)

    return row_call(name, body, x.shape[0], ROW_TILE, [(x, "tile")], [], [(1, x.shape[1])])[0]


def _conv_taps(xcat, w_ref, cols, width, halo, tm):
    rows = halo + tm
    acc = None
    for j in range(width):
        s = width - 1 - j
        xs = xcat if s == 0 else pltpu.roll(xcat, s, 0)
        term = xs[halo:rows] * w_ref[j:j + 1, cols]
        acc = term if acc is None else acc + term
    return acc


def _conv_taps_bwd_x(dcat, w_ref, cols, width, halo, tm):
    rows = halo + tm
    acc = None
    for j in range(width):
        s = width - 1 - j
        ds = dcat if s == 0 else pltpu.roll(dcat, rows - s, 0)
        term = ds[0:tm] * w_ref[j:j + 1, cols]
        acc = term if acc is None else acc + term
    return acc


def _conv_taps_bwd_w(dy, xcat, width, halo, tm, wrows):
    rows = halo + tm
    rid = _iota((wrows, dy.shape[1]), 0)
    out = jnp.zeros((wrows, dy.shape[1]), F32)
    for j in range(width):
        s = width - 1 - j
        xs = xcat if s == 0 else pltpu.roll(xcat, s, 0)
        v = jnp.sum(dy * xs[halo:rows], axis=0, keepdims=True)
        out = out + jnp.where(rid == j, v, 0.0)
    return out


def dn_pre(qkv_raw, ba, w_conv, gate):
    s_len = qkv_raw.shape[0]
    tm = min(DN_ROW_TILE, s_len)
    n_blk = qkv_raw.shape[1] // LANES

    def body(i, ins, outs, accs):
        x_ref, xp_ref, ba_ref, w_ref, gate_ref = ins
        qkv_ref, hs_ref = outs

        def blk(cb, carry):
            cols = pl.ds(pl.multiple_of(cb * LANES, LANES), LANES)
            prev = jnp.where(i > 0, xp_ref[:, cols], 0.0)
            xcat = jnp.concatenate([prev, x_ref[:, cols]], axis=0)
            c = _conv_taps(xcat, w_ref, cols, DN_CONV, DN_HALO, tm)
            y = _silu(c)
            rs = lax.rsqrt(jnp.sum(y * y, axis=-1, keepdims=True) + L2_EPS)
            fac = jnp.where(cb < DN_HEADS, DN_HEAD_DIM ** -0.5, 1.0)
            qkv_ref[:, cols] = jnp.where(cb < 2 * DN_HEADS, y * (rs * fac), y)
            return carry

        lax.fori_loop(0, n_blk, blk, 0)

        bav = ba_ref[...]
        beta = _sigmoid(bav)
        g = -jnp.exp(gate_ref[0:1, :]) * _softplus(bav + gate_ref[1:2, :])
        lane = _iota((tm, LANES), 1)
        g = jnp.where((lane >= DN_HEADS) & (lane < 2 * DN_HEADS), g, 0.0)
        r = _iota((tm, tm), 0)
        c = _iota((tm, tm), 1)
        tri = jnp.where((r >= c) & ((r >> CHUNK_SHIFT) == (c >> CHUNK_SHIFT)), 1.0, 0.0)
        gc = _dot_hi(tri, g)
        for h in range(DN_HEADS):
            hs_ref[h] = jnp.where(lane == 0, _lane_col(beta, lane, h),
                                  jnp.where(lane == 1, _lane_col(g, lane, DN_HEADS + h),
                                            jnp.where(lane == 2, _lane_col(gc, lane, DN_HEADS + h), 0.0)))

    return row_call("dn_pre", body, s_len, tm,
                    [(qkv_raw, "tile"), (qkv_raw, ("prev", DN_HALO)), (ba, "tile"), (w_conv, "full"), (gate, "full")],
                    [(qkv_raw.shape, F32), ((DN_HEADS, s_len, LANES), F32)])


def _chunk_masks():
    r = _iota((DN_CHUNK, DN_CHUNK), 0)
    c = _iota((DN_CHUNK, DN_CHUNK), 1)
    return r, c


def _decay_matrix(gc, r, c):
    lane = _iota((DN_CHUNK, LANES), 1)
    a = jnp.where(lane == 0, gc, jnp.where(lane == 1, 1.0, 0.0))
    b = jnp.where(lane == 0, 1.0, jnp.where(lane == 1, -gc, 0.0))
    diff = _dot_hi(a, b, (((1,), (1,)), ((), ())))
    causal = r >= c
    return jnp.where(causal, jnp.exp(jnp.where(causal, diff, 0.0)), 0.0)


def _tri_inverse(lows, r, c):
    eye = jnp.where(r == c, 1.0, 0.0)
    ts = [eye for _ in lows]
    b = 1
    while b < DN_CHUNK:
        shift = b.bit_length()
        sel = ((r >> shift) == (c >> shift)) & ((r & b) != 0) & ((c & b) == 0)
        lms = [jnp.where(sel, low, 0.0) for low in lows]
        if b == 1:
            ts = [t - lm for t, lm in zip(ts, lms)]
        else:
            t_lm = [_dot_hi(t, lm) for t, lm in zip(ts, lms)]
            t_lm_t = [_dot_hi(x, t) for x, t in zip(t_lm, ts)]
            ts = [t - x for t, x in zip(ts, t_lm_t)]
        b *= 2
    return ts


def dn_solve(qkv, hs):
    s_len = qkv.shape[0]
    rb = min(ROW_TILE, s_len)
    n_chunk = rb // DN_CHUNK
    interleave = min(SOLVE_INTERLEAVE, n_chunk)

    def body(k_ref, v_ref, hs_ref, u_ref, w_ref, t_ref):
        r, c = _chunk_masks()

        def group(gi, carry):
            rows = [pl.ds(pl.multiple_of((gi * interleave + j) * DN_CHUNK, DN_CHUNK), DN_CHUNK)
                    for j in range(interleave)]
            k = [k_ref[rw, :] for rw in rows]
            beta = [hs_ref[rw, 0:1] for rw in rows]
            gc = [hs_ref[rw, 2:3] for rw in rows]
            kb = [a * b for a, b in zip(k, beta)]
            decay = [_decay_matrix(g, r, c) for g in gc]
            lows = [jnp.where(r > c, _dot_nt(a, b) * d, 0.0) for a, b, d in zip(kb, k, decay)]
            ts = _tri_inverse(lows, r, c)
            us = [_dot_hi(t, v_ref[rw, :] * b) for t, rw, b in zip(ts, rows, beta)]
            ws = [_dot_hi(t, a * jnp.exp(g)) for t, a, g in zip(ts, kb, gc)]
            for j, rw in enumerate(rows):
                u_ref[rw, :] = us[j]
                w_ref[rw, :] = ws[j].astype(BF16)
                t_ref[rw, :] = ts[j]
            return carry

        lax.fori_loop(0, n_chunk // interleave, group, 0)

    return pl.pallas_call(
        body, name="dn_solve", grid=(DN_HEADS, s_len // rb),
        in_specs=[pl.BlockSpec((rb, LANES), lambda h, i: (i, DN_HEADS + h)),
                  pl.BlockSpec((rb, LANES), lambda h, i: (i, 2 * DN_HEADS + h)),
                  pl.BlockSpec((None, rb, LANES), lambda h, i: (h, i, 0))],
        out_specs=[pl.BlockSpec((rb, LANES), lambda h, i: (i, h)),
                   pl.BlockSpec((rb, LANES), lambda h, i: (i, h)),
                   pl.BlockSpec((None, rb, DN_CHUNK), lambda h, i: (h, i, 0))],
        out_shape=[jax.ShapeDtypeStruct((s_len, DN_HEADS * LANES), F32),
                   jax.ShapeDtypeStruct((s_len, DN_HEADS * LANES), BF16),
                   jax.ShapeDtypeStruct((DN_HEADS, s_len, DN_CHUNK), F32)],
        compiler_params=_cparams(("parallel", "parallel")),
    )(qkv, qkv, hs)


def dn_scan_fwd(qkv, u, w, hs):
    s_len = qkv.shape[0]
    rb = min(ROW_TILE, s_len)
    n_chunk = rb // DN_CHUNK
    total_chunks = s_len // DN_CHUNK

    hps = FWD_HEADS_PER_STEP
    groups = DN_HEADS // hps

    def body(q_ref, k_ref, u_ref, w_ref, hs_ref, o_ref, st_ref, state):
        @pl.when(pl.program_id(1) == 0)
        def _():
            state[...] = jnp.zeros_like(state)

        r, c = _chunk_masks()

        def chunk(n, carry):
            rows = pl.ds(pl.multiple_of(n * DN_CHUNK, DN_CHUNK), DN_CHUNK)
            heads = range(hps)
            cols = [slice(h * LANES, (h + 1) * LANES) for h in heads]
            each = lambda f, *xs: [f(*a) for a in zip(*xs)]
            q = [q_ref[rows, cl] for cl in cols]
            k = [k_ref[rows, cl] for cl in cols]
            gc = [hs_ref[h, rows, 2:3] for h in heads]
            st = [state[h] for h in heads]
            for h in heads:
                st_ref[h, n] = st[h]
            gl = each(lambda g: jnp.min(g, axis=0, keepdims=True), gc)
            decay = each(lambda g: _decay_matrix(g, r, c), gc)
            w_st = [_dot(w_ref[rows, cols[h]], st[h]) for h in heads]
            qk = each(_dot_nt, q, k)
            q_st = each(lambda a, g, s: _dot(a * jnp.exp(g), s), q, gc, st)
            vn = [u_ref[rows, cols[h]] - w_st[h] for h in heads]
            ai_vn = each(lambda a, d, b: _dot(a * d, b), qk, decay, vn)
            kd_vn = each(lambda a, g0, g, b: _dot_tn(a * jnp.exp(g0 - g), b), k, gl, gc, vn)
            for h in heads:
                o_ref[rows, cols[h]] = q_st[h] + ai_vn[h]
                state[h] = st[h] * jnp.exp(gl[h]) + kd_vn[h]
            return carry

        lax.fori_loop(0, n_chunk, chunk, 0)

    wide = hps * LANES
    blk = lambda off: pl.BlockSpec((rb, wide), lambda h, i: (i, off + h))
    return pl.pallas_call(
        body, name="dn_scan_fwd", grid=(groups, s_len // rb),
        in_specs=[blk(0), blk(groups), blk(0), blk(0),
                  pl.BlockSpec((hps, rb, LANES), lambda h, i: (h, i, 0))],
        out_specs=[blk(0),
                   pl.BlockSpec((hps, n_chunk, LANES, LANES), lambda h, i: (h, i, 0, 0))],
        out_shape=[jax.ShapeDtypeStruct((s_len, DN_HEADS * LANES), F32),
                   jax.ShapeDtypeStruct((DN_HEADS, total_chunks, LANES, LANES), F32)],
        scratch_shapes=[pltpu.VMEM((hps, LANES, LANES), F32)],
        compiler_params=_cparams(("parallel", "arbitrary")),
    )(qkv, qkv, u, w, hs)


def dn_scan_bwd(qkv, u, w, t_inv, hs, states, d_o):
    s_len = qkv.shape[0]
    rb = min(ROW_TILE, s_len)
    n_chunk = rb // DN_CHUNK
    n_blk = s_len // rb
    hps = BWD_HEADS_PER_STEP
    groups = DN_HEADS // hps

    def body(q_ref, k_ref, v_ref, u_ref, w_ref, t_ref, hs_ref, st_ref, do_ref,
             dq_ref, dk_ref, dv_ref, dhs_ref, dstate):
        @pl.when(pl.program_id(1) == 0)
        def _():
            dstate[...] = jnp.zeros_like(dstate)

        r, c = _chunk_masks()
        causal = r >= c
        strict = r > c
        lane = _iota((DN_CHUNK, LANES), 1)
        upper = jnp.where(r <= c, 1.0, 0.0)
        last_row = _iota((DN_CHUNK, 1), 0) == DN_CHUNK - 1

        def chunk(m, carry):
            n = n_chunk - 1 - m
            rows = pl.ds(pl.multiple_of(n * DN_CHUNK, DN_CHUNK), DN_CHUNK)
            heads = range(hps)
            cols = [slice(h * LANES, (h + 1) * LANES) for h in heads]
            each = lambda f, *xs: [f(*a) for a in zip(*xs)]
            rsum = lambda x: jnp.sum(x, axis=-1, keepdims=True)
            dims_tn = (((0,), (0,)), ((), ()))
            ones = jnp.ones((DN_CHUNK, LANES), F32)
            q = [q_ref[rows, cl] for cl in cols]
            k = [k_ref[rows, cl] for cl in cols]
            v = [v_ref[rows, cl] for cl in cols]
            uu = [u_ref[rows, cl] for cl in cols]
            ww = [w_ref[rows, cl] for cl in cols]
            do = [do_ref[rows, cl] for cl in cols]
            tt = [t_ref[h, rows, :] for h in heads]
            beta = [hs_ref[h, rows, 0:1] for h in heads]
            gc = [hs_ref[h, rows, 2:3] for h in heads]
            st = [st_ref[h, n] for h in heads]
            dst = [dstate[h] for h in heads]
            gl = each(lambda g: jnp.min(g, axis=0, keepdims=True), gc)
            egc = each(jnp.exp, gc)
            egl = each(jnp.exp, gl)
            ekd = each(lambda a, b: jnp.exp(a - b), gl, gc)
            decay = each(lambda g: _decay_matrix(g, r, c), gc)
            qd = each(jnp.multiply, q, egc)
            kd = each(jnp.multiply, k, ekd)
            kb = each(jnp.multiply, k, beta)
            w_st = each(_dot, ww, st)
            qk = each(_dot_nt, q, k)
            dqd = each(_dot_nt, do, st)
            kd_dst = each(_dot, kd, dst)
            qd_do = each(_dot_tn, qd, do)
            kbk = each(_dot_nt, kb, k)
            vn = each(jnp.subtract, uu, w_st)
            ai = each(jnp.multiply, qk, decay)
            low = each(lambda a, d: jnp.where(strict, a * d, 0.0), kbk, decay)
            dai = each(lambda a, b: jnp.where(causal, _dot_nt(a, b), 0.0), do, vn)
            ai_do = each(_dot_tn, ai, do)
            dkd = each(_dot_nt, vn, dst)
            dvn = each(jnp.add, ai_do, kd_dst)
            dp = each(jnp.multiply, dai, decay)
            dw = each(lambda a, b: -_dot_nt(a, b), dvn, st)
            w_dvn = each(_dot_tn, ww, dvn)
            dp_k = each(_dot, dp, k)
            dp_q = each(_dot_tn, dp, q)
            drhs_u = each(lambda a, b: _dot_hi(a, b, dims_tn), tt, dvn)
            dgl = each(lambda a, b, e: jnp.sum(a * b) * e, dst, st, egl)
            for h in heads:
                dstate[h] = dst[h] * egl[h] + qd_do[h] - w_dvn[h]
            dq = each(lambda a, e, b: a * e + b, dqd, egc, dp_k)
            dk_a = each(lambda a, e, b: a * e + b, dkd, ekd, dp_q)
            rkd = each(lambda a, b: rsum(a * b), dkd, kd)
            drhs_w = each(lambda a, b: _dot_hi(a, b, dims_tn), tt, dw)
            dl_u = each(_dot_nt, drhs_u, uu)
            dl_w = each(_dot_nt, drhs_w, ww)
            dlow = each(lambda a, b: jnp.where(strict, -(a + b), 0.0), dl_u, dl_w)
            dqm = each(jnp.multiply, dlow, decay)
            m_tot = each(lambda a, b, d, e: a * b + d * e, dai, ai, dlow, low)
            dqm_k = each(_dot, dqm, k)
            dk_l = each(_dot_tn, dqm, kb)
            col_sums = each(lambda m: _dot_hi(m, ones, dims_tn), m_tot)
            dkb_w = each(jnp.multiply, drhs_w, egc)
            dkb = each(jnp.add, dkb_w, dqm_k)
            dgc = [rsum(dqd[h] * qd[h]) - rkd[h] + jnp.where(last_row, jnp.sum(rkd[h]) + dgl[h], 0.0)
                   + rsum(m_tot[h]) + rsum(dkb_w[h] * kb[h]) for h in heads]
            dg = each(lambda a, b: _dot_hi(upper, jnp.where(lane == 1, a - b, 0.0)), dgc, col_sums)
            for h in heads:
                dq_ref[rows, cols[h]] = dq[h]
                dk_ref[rows, cols[h]] = dk_a[h] + dk_l[h] + dkb[h] * beta[h]
                dv_ref[rows, cols[h]] = drhs_u[h] * beta[h]
                dbeta = rsum(drhs_u[h] * v[h]) + rsum(dkb[h] * k[h])
                dhs_ref[h, rows, :] = jnp.where(lane == 0, dbeta, dg[h])
            return carry

        lax.fori_loop(0, n_chunk, chunk, 0)

    wide = hps * LANES
    blk = lambda off: pl.BlockSpec((rb, wide), lambda h, i: (n_blk - 1 - i, off + h))
    head = blk(0)
    hs_spec = pl.BlockSpec((hps, rb, LANES), lambda h, i: (h, n_blk - 1 - i, 0))
    full = jax.ShapeDtypeStruct((s_len, DN_HEADS * LANES), F32)
    return pl.pallas_call(
        body, name="dn_scan_bwd", grid=(groups, n_blk),
        in_specs=[blk(0), blk(groups), blk(2 * groups), head, head,
                  pl.BlockSpec((hps, rb, DN_CHUNK), lambda h, i: (h, n_blk - 1 - i, 0)), hs_spec,
                  pl.BlockSpec((hps, n_chunk, LANES, LANES), lambda h, i: (h, n_blk - 1 - i, 0, 0)), head],
        out_specs=[head, head, head, hs_spec],
        out_shape=[full, full, full, jax.ShapeDtypeStruct((DN_HEADS, s_len, LANES), F32)],
        scratch_shapes=[pltpu.VMEM((hps, LANES, LANES), F32)],
        compiler_params=_cparams(("parallel", "arbitrary")),
    )(qkv, qkv, qkv, u, w, t_inv, hs, states, d_o)


def dn_post(o, z, out_norm):
    def body(i, ins, outs, accs):
        gn = ins[2][...]
        for h in range(DN_HEADS):
            cols = slice(h * LANES, (h + 1) * LANES)
            xhat, _ = _rms_stats(ins[0][:, cols])
            outs[0][:, cols] = (xhat * gn * _silu(ins[1][:, cols])).astype(BF16)

    return row_call("dn_post", body, o.shape[0], ROW_TILE, [(o, "tile"), (z, "tile"), (out_norm, "full")],
                    [(o.shape, BF16)])[0]


def dn_post_bwd(d_og, o, z, out_norm):
    def body(i, ins, outs, accs):
        gn = ins[3][...]
        dgn = jnp.zeros((1, LANES), F32)
        for h in range(DN_HEADS):
            cols = slice(h * LANES, (h + 1) * LANES)
            dy, zh = ins[0][:, cols].astype(F32), ins[2][:, cols]
            xhat, r = _rms_stats(ins[1][:, cols])
            sz = _silu(zh)
            dgn = dgn + jnp.sum(dy * xhat * sz, axis=0, keepdims=True)
            outs[1][:, cols] = (dy * xhat * gn * _silu_grad(zh)).astype(BF16)
            dxhat = dy * gn * sz
            outs[0][:, cols] = r * (dxhat - xhat * jnp.mean(dxhat * xhat, axis=-1, keepdims=True))
        accs[0][...] += dgn

    return row_call("dn_post_bwd", body, o.shape[0], ROW_TILE,
                    [(d_og, "tile"), (o, "tile"), (z, "tile"), (out_norm, "full")],
                    [(o.shape, F32), (o.shape, BF16)], [(1, LANES)])


def dn_pre_bwd(dq, dk, dv, dhs, qkv_raw, ba, w_conv, gate):
    s_len = qkv_raw.shape[0]
    tm = min(DN_ROW_TILE, s_len)

    def body(i, ins, outs, accs):
        dq_ref, dk_ref, dv_ref, dhs_ref, x_ref, xp_ref, ba_ref, w_ref, gate_ref = ins
        dc_ref, dba_ref = outs

        def blk(cb, carry):
            cols = pl.ds(pl.multiple_of(cb * LANES, LANES), LANES)
            hcols = pl.ds(pl.multiple_of((cb & (DN_HEADS - 1)) * LANES, LANES), LANES)
            prev = jnp.where(i > 0, xp_ref[:, cols], 0.0)
            xcat = jnp.concatenate([prev, x_ref[:, cols]], axis=0)
            c = _conv_taps(xcat, w_ref, cols, DN_CONV, DN_HALO, tm)
            y = _silu(c)
            dy = jnp.where(cb < DN_HEADS, dq_ref[:, hcols],
                           jnp.where(cb < 2 * DN_HEADS, dk_ref[:, hcols], dv_ref[:, hcols]))
            rs = lax.rsqrt(jnp.sum(y * y, axis=-1, keepdims=True) + L2_EPS)
            fac = jnp.where(cb < DN_HEADS, DN_HEAD_DIM ** -0.5, 1.0)
            nrm = y * rs
            dn = dy * fac
            dy_norm = rs * (dn - nrm * jnp.sum(dn * nrm, axis=-1, keepdims=True))
            dc_ref[:, cols] = jnp.where(cb < 2 * DN_HEADS, dy_norm, dy) * _silu_grad(c)
            return carry

        lax.fori_loop(0, qkv_raw.shape[1] // LANES, blk, 0)

        lane = _iota((tm, LANES), 1)
        dbeta = jnp.zeros((tm, LANES), F32)
        dg = jnp.zeros((tm, LANES), F32)
        for h in range(DN_HEADS):
            dbeta = dbeta + jnp.where(lane == h, dhs_ref[h, :, 0:1], 0.0)
            dg = dg + jnp.where(lane == DN_HEADS + h, dhs_ref[h, :, 1:2], 0.0)
        bav = ba_ref[...]
        beta = _sigmoid(bav)
        ea = jnp.exp(gate_ref[0:1, :])
        pre = bav + gate_ref[1:2, :]
        g = -ea * _softplus(pre)
        da = dg * (-ea) * _sigmoid(pre)
        dba_ref[...] = (dbeta * beta * (1.0 - beta) + da).astype(BF16)
        rid = _iota((8, LANES), 0)
        accs[0][...] += (jnp.where(rid == 0, jnp.sum(dg * g, axis=0, keepdims=True), 0.0)
                         + jnp.where(rid == 1, jnp.sum(da, axis=0, keepdims=True), 0.0))

    return row_call("dn_pre_bwd", body, s_len, tm,
                    [(dq, "tile"), (dk, "tile"), (dv, "tile"), (dhs, "tile"), (qkv_raw, "tile"),
                     (qkv_raw, ("prev", DN_HALO)), (ba, "tile"), (w_conv, "full"), (gate, "full")],
                    [(qkv_raw.shape, F32), (ba.shape, BF16)], [(8, LANES)])


def dn_conv_bwd(dc, qkv_raw, w_conv):
    s_len = dc.shape[0]
    tm = min(DN_ROW_TILE, s_len)
    nt = s_len // tm

    def body(i, ins, outs, accs):
        dc_ref, dn_ref, x_ref, xp_ref, w_ref = ins

        def blk(cb, carry):
            cols = pl.ds(pl.multiple_of(cb * LANES, LANES), LANES)
            dy = dc_ref[:, cols]
            nxt = jnp.where(i < nt - 1, dn_ref[:, cols], 0.0)
            dcat = jnp.concatenate([dy, nxt], axis=0)
            outs[0][:, cols] = _conv_taps_bwd_x(dcat, w_ref, cols, DN_CONV, DN_HALO, tm).astype(BF16)
            prev = jnp.where(i > 0, xp_ref[:, cols], 0.0)
            xcat = jnp.concatenate([prev, x_ref[:, cols]], axis=0)
            accs[0][:, cols] += _conv_taps_bwd_w(dy, xcat, DN_CONV, DN_HALO, tm, 8)
            return carry

        lax.fori_loop(0, dc.shape[1] // LANES, blk, 0)

    return row_call("dn_conv_bwd", body, s_len, tm,
                    [(dc, "tile"), (dc, ("next", DN_HALO)), (qkv_raw, "tile"), (qkv_raw, ("prev", DN_HALO)),
                     (w_conv, "full")],
                    [(dc.shape, BF16)], [(8, dc.shape[1])])


def _glu(u_ref, cols, d):
    return u_ref[:, cols] * _sigmoid(u_ref[:, pl.ds(pl.multiple_of(d + cols.start, LANES), cols.size)])


def cv_core_fwd(u, w_dw, b_dw, ln_g, ln_b):
    s_len, d = u.shape[0], u.shape[1] // 2
    tm = min(CONV_ROW_TILE, s_len)

    def body(i, ins, outs, accs):
        u_ref, up_ref, w_ref, bdw_ref, g_ref, b_ref = ins
        s_ref, c_ref = outs

        def blk(cb, carry):
            cols = pl.ds(pl.multiple_of(cb * LANES, LANES), LANES)
            prev = jnp.where(i > 0, _glu(up_ref, cols, d), 0.0)
            xcat = jnp.concatenate([prev, _glu(u_ref, cols, d)], axis=0)
            c_ref[:, cols] = _conv_taps(xcat, w_ref, cols, CV_WIDTH, CV_HALO, tm) + bdw_ref[:, cols]
            return carry

        lax.fori_loop(0, d // LANES, blk, 0)
        c = c_ref[...]
        mu = jnp.mean(c, axis=-1, keepdims=True)
        xc = c - mu
        rstd = lax.rsqrt(jnp.mean(xc * xc, axis=-1, keepdims=True) + LN_EPS)
        s_ref[...] = _silu(xc * rstd * g_ref[...] + b_ref[...]).astype(BF16)

    return row_call("cv_core_fwd", body, s_len, tm,
                    [(u, "tile"), (u, ("prev", CV_HALO)), (w_dw, "full"), (b_dw, "full"), (ln_g, "full"),
                     (ln_b, "full")],
                    [((s_len, d), BF16), ((s_len, d), F32)])


def cv_ln_bwd(ds, c, ln_g, ln_b):
    def body(i, ins, outs, accs):
        cv, g = ins[1][...], ins[2][...]
        mu = jnp.mean(cv, axis=-1, keepdims=True)
        xc = cv - mu
        rstd = lax.rsqrt(jnp.mean(xc * xc, axis=-1, keepdims=True) + LN_EPS)
        xhat = xc * rstd
        dl = ins[0][...].astype(F32) * _silu_grad(xhat * g + ins[3][...])
        dxhat = dl * g
        dc = rstd * (dxhat - jnp.mean(dxhat, axis=-1, keepdims=True)
                     - xhat * jnp.mean(dxhat * xhat, axis=-1, keepdims=True))
        outs[0][...] = dc
        rid = _iota((8, cv.shape[1]), 0)
        accs[0][...] += (jnp.where(rid == 0, jnp.sum(dl * xhat, axis=0, keepdims=True), 0.0)
                         + jnp.where(rid == 1, jnp.sum(dl, axis=0, keepdims=True), 0.0)
                         + jnp.where(rid == 2, jnp.sum(dc, axis=0, keepdims=True), 0.0))

    return row_call("cv_ln_bwd", body, c.shape[0], ROW_TILE,
                    [(ds, "tile"), (c, "tile"), (ln_g, "full"), (ln_b, "full")], [(c.shape, F32)], [(8, c.shape[1])])


def cv_conv_bwd(dc, u, w_dw):
    s_len, d = dc.shape
    tm = min(CONV_ROW_TILE, s_len)
    nt = s_len // tm

    def body(i, ins, outs, accs):
        dc_ref, dn_ref, u_ref, up_ref, w_ref = ins

        def blk(cb, carry):
            cols = pl.ds(pl.multiple_of(cb * LANES, LANES), LANES)
            gcols = pl.ds(pl.multiple_of(d + cb * LANES, LANES), LANES)
            dy = dc_ref[:, cols]
            nxt = jnp.where(i < nt - 1, dn_ref[:, cols], 0.0)
            dgl = _conv_taps_bwd_x(jnp.concatenate([dy, nxt], axis=0), w_ref, cols, CV_WIDTH, CV_HALO, tm)
            u1, sg = u_ref[:, cols], _sigmoid(u_ref[:, gcols])
            du1 = dgl * sg
            du2 = dgl * u1 * sg * (1.0 - sg)
            outs[0][:, cols] = du1.astype(BF16)
            outs[0][:, gcols] = du2.astype(BF16)
            accs[1][:, cols] += jnp.sum(du1, axis=0, keepdims=True)
            accs[1][:, gcols] += jnp.sum(du2, axis=0, keepdims=True)
            prev = jnp.where(i > 0, _glu(up_ref, cols, d), 0.0)
            xcat = jnp.concatenate([prev, u1 * sg], axis=0)
            accs[0][:, cols] += _conv_taps_bwd_w(dy, xcat, CV_WIDTH, CV_HALO, tm, CV_HALO)
            return carry

        lax.fori_loop(0, d // LANES, blk, 0)

    return row_call("cv_conv_bwd", body, s_len, tm,
                    [(dc, "tile"), (dc, ("next", CV_HALO)), (u, "tile"), (u, ("prev", CV_HALO)), (w_dw, "full")],
                    [(u.shape, BF16)], [(CV_HALO, d), (1, 2 * d)])


def xa_core_fwd(name, q, kv):
    d = q.shape[1]

    def body(i, ins, outs, accs):
        for h in range(XA_HEADS):
            cols = slice(h * XA_HEAD_DIM, (h + 1) * XA_HEAD_DIM)
            vcols = slice(d + h * XA_HEAD_DIM, d + (h + 1) * XA_HEAD_DIM)
            s = _dot_nt(ins[0][:, cols], ins[1][:, cols]) * (XA_HEAD_DIM ** -0.5)
            e = jnp.exp(s - jnp.max(s, axis=-1, keepdims=True))
            p = e / jnp.sum(e, axis=-1, keepdims=True)
            outs[0][:, cols] = _dot(p, ins[1][:, vcols]).astype(BF16)

    return row_call(name, body, q.shape[0], ROW_TILE, [(q, "tile"), (kv, "full")], [(q.shape, BF16)])[0]


def xa_core_bwd(name, d_o, q, kv):
    d = q.shape[1]

    def body(i, ins, outs, accs):
        for h in range(XA_HEADS):
            cols = slice(h * XA_HEAD_DIM, (h + 1) * XA_HEAD_DIM)
            vcols = slice(d + h * XA_HEAD_DIM, d + (h + 1) * XA_HEAD_DIM)
            qh, kh, vh, doh = ins[1][:, cols], ins[2][:, cols], ins[2][:, vcols], ins[0][:, cols]
            s = _dot_nt(qh, kh) * (XA_HEAD_DIM ** -0.5)
            e = jnp.exp(s - jnp.max(s, axis=-1, keepdims=True))
            p = e / jnp.sum(e, axis=-1, keepdims=True)
            dp = _dot_nt(doh, vh)
            ds = p * (dp - jnp.sum(dp * p, axis=-1, keepdims=True)) * (XA_HEAD_DIM ** -0.5)
            outs[0][:, cols] = _dot(ds, kh).astype(BF16)
            accs[0][:, cols] += _dot_tn(ds, qh)
            accs[0][:, vcols] += _dot_tn(p, doh)

    return row_call(name, body, q.shape[0], ROW_TILE, [(d_o, "tile"), (q, "tile"), (kv, "full")],
                    [(q.shape, BF16)], [kv.shape])


def adamw(name, w, g, m, v):
    def body(i, ins, outs, accs):
        wv, gv = ins[0][...], ins[1][...]
        mn = ADAM_B1 * ins[2][...] + (1.0 - ADAM_B1) * gv
        vn = ADAM_B2 * ins[3][...] + (1.0 - ADAM_B2) * jnp.square(gv)
        m_hat = mn / (1.0 - ADAM_B1 ** ADAM_STEP)
        v_hat = vn / (1.0 - ADAM_B2 ** ADAM_STEP)
        outs[0][...] = -ADAM_LR * (m_hat / (jnp.sqrt(v_hat) + ADAM_EPS) + ADAM_WD * wv)
        outs[1][...] = mn
        outs[2][...] = vn

    return row_call(name, body, w.shape[0], ROW_TILE, [(w, "tile"), (g, "tile"), (m, "tile"), (v, "tile")],
                    [(w.shape, F32)] * 3)


def adamw_halves(name, w, g_mine, g_sibling, m, v, core):
    rows, cols = w.shape
    tm = _pick_tile(rows // 2, ADAMW_ROW_TILE)
    per_half = rows // 2 // tm

    def body(core_ref, w_ref, gm_ref, gs_ref, m_ref, v_ref, g_out, d_out, m_out, v_out):
        mine = (pl.program_id(0) // per_half) == core_ref[0]
        gv = jnp.where(mine, gm_ref[...], gs_ref[...])
        mn = ADAM_B1 * m_ref[...] + (1.0 - ADAM_B1) * gv
        vn = ADAM_B2 * v_ref[...] + (1.0 - ADAM_B2) * jnp.square(gv)
        m_hat = mn / (1.0 - ADAM_B1 ** ADAM_STEP)
        v_hat = vn / (1.0 - ADAM_B2 ** ADAM_STEP)
        g_out[...] = gv
        d_out[...] = -ADAM_LR * (m_hat / (jnp.sqrt(v_hat) + ADAM_EPS) + ADAM_WD * w_ref[...])
        m_out[...] = mn
        v_out[...] = vn

    whole = pl.BlockSpec((tm, cols), lambda i, core_ref: (i, 0))
    half = pl.BlockSpec((tm, cols), lambda i, core_ref: (i % per_half, 0))
    return pl.pallas_call(
        body, name=name,
        grid_spec=pltpu.PrefetchScalarGridSpec(
            num_scalar_prefetch=1, grid=(2 * per_half,),
            in_specs=[whole, half, half, whole, whole], out_specs=[whole] * 4),
        out_shape=[jax.ShapeDtypeStruct(w.shape, F32)] * 4,
        compiler_params=_cparams(("parallel",)),
    )(core, w, g_mine, g_sibling, m, v)


HBM_SPEC = pl.BlockSpec(memory_space=pltpu.HBM)


def _position():
    return lax.axis_index("x"), lax.axis_index("y"), lax.axis_index("c")


def _other_chips(x, y):
    return [(1 - x, y), (x, 1 - y), (1 - x, 1 - y)]


def _row_chunks(rows):
    return rows // D2D_CHUNK_ROWS if rows % D2D_CHUNK_ROWS == 0 else 1


def _start_chunked(make, rows):
    k = _row_chunks(rows)
    for i in range(k):
        make(i * (rows // k), rows // k).start()


def gather_shards(packs):
    n = len(packs)

    def body(*refs):
        srcs, outs = refs[:n], refs[n:2 * n]
        send_sems, recv_sems = refs[2 * n:]
        x, y, c = _position()
        me = 2 * x + y
        chips = _other_chips(x, y)
        sibling = (x, y, 1 - c)

        def over_ici(a, j):
            px, py = chips[j]
            rows = srcs[a].shape[0] // 2
            return pltpu.make_async_remote_copy(
                src_ref=srcs[a].at[pl.ds(c * rows, rows), :], dst_ref=outs[a].at[me, pl.ds(c * rows, rows), :],
                send_sem=send_sems.at[a, j], recv_sem=recv_sems.at[a, j], device_id=(px, py, c), device_id_type=MESH)

        def landed(a, j):
            px, py = chips[j]
            rows = srcs[a].shape[0] // 2
            part = outs[a].at[2 * px + py, pl.ds(c * rows, rows), :]
            return pltpu.make_async_remote_copy(
                src_ref=part, dst_ref=part, send_sem=send_sems.at[a, j], recv_sem=recv_sems.at[a, j],
                device_id=(px, py, c), device_id_type=MESH)

        def over_d2d(a, j, cc, off, size):
            px, py = chips[j]
            rows = srcs[a].shape[0] // 2
            part = outs[a].at[2 * px + py, pl.ds(cc * rows + off, size), :]
            return pltpu.make_async_remote_copy(
                src_ref=part, dst_ref=part, send_sem=send_sems.at[a, 3 + j], recv_sem=recv_sems.at[a, 3 + j],
                device_id=sibling, device_id_type=MESH)

        for a in range(n):
            for j in range(3):
                over_ici(a, j).start()
        for a in range(n):
            for j in range(3):
                landed(a, j).wait_recv()
                _start_chunked(functools.partial(over_d2d, a, j, c), srcs[a].shape[0] // 2)
        for a in range(n):
            rows = srcs[a].shape[0] // 2
            for j in range(3):
                over_d2d(a, j, 1 - c, 0, rows).wait_recv()
                over_d2d(a, j, c, 0, rows).wait_send()
                over_ici(a, j).wait_send()

    return pl.pallas_call(
        body, name="gather_shards",
        in_specs=[HBM_SPEC] * n, out_specs=[HBM_SPEC] * n,
        out_shape=[jax.ShapeDtypeStruct((N_CHIPS,) + p.shape, p.dtype) for p in packs],
        scratch_shapes=[pltpu.SemaphoreType.DMA((n, 6)), pltpu.SemaphoreType.DMA((n, 6))],
    )(*packs)


def pair_split(packs):
    n = len(packs)

    def body(*refs):
        srcs, outs = refs[:n], refs[n:2 * n]
        send_sems, recv_sems = refs[2 * n:]
        x, y, c = _position()

        def remote(a, off, size):
            rows = srcs[a].shape[1] // 2
            return pltpu.make_async_remote_copy(
                src_ref=srcs[a].at[:, pl.ds((1 - c) * rows + off, size), :],
                dst_ref=outs[a].at[:, pl.ds(off, size), :],
                send_sem=send_sems.at[a], recv_sem=recv_sems.at[a], device_id=(x, y, 1 - c), device_id_type=MESH)

        for a in range(n):
            _start_chunked(functools.partial(remote, a), srcs[a].shape[1] // 2)
        for a in range(n):
            remote(a, 0, srcs[a].shape[1] // 2).wait()

    return pl.pallas_call(
        body, name="pair_split", in_specs=[HBM_SPEC] * n, out_specs=[HBM_SPEC] * n,
        out_shape=[jax.ShapeDtypeStruct((p.shape[0], p.shape[1] // 2, p.shape[2]), p.dtype) for p in packs],
        scratch_shapes=[pltpu.SemaphoreType.DMA((n,)), pltpu.SemaphoreType.DMA((n,))],
    )(*packs)


def chip_scatter(packs):
    n = len(packs)

    def body(*refs):
        srcs, outs = refs[:n], refs[n:2 * n]
        send_sems, recv_sems = refs[2 * n:]
        x, y, c = _position()
        copies = []
        for a in range(n):
            for j, (px, py) in enumerate(_other_chips(x, y)):
                cp = pltpu.make_async_remote_copy(
                    src_ref=srcs[a].at[2 * px + py], dst_ref=outs[a].at[j],
                    send_sem=send_sems.at[a, j], recv_sem=recv_sems.at[a, j],
                    device_id=(px, py, c), device_id_type=MESH)
                cp.start()
                copies.append(cp)
        for cp in copies:
            cp.wait()

    return pl.pallas_call(
        body, name="chip_scatter", in_specs=[HBM_SPEC] * n, out_specs=[HBM_SPEC] * n,
        out_shape=[jax.ShapeDtypeStruct((N_CHIPS - 1,) + p.shape[1:], p.dtype) for p in packs],
        scratch_shapes=[pltpu.SemaphoreType.DMA((n, 3)), pltpu.SemaphoreType.DMA((n, 3))],
    )(*packs)


def pair_join(halves):
    n = len(halves)

    def body(*refs):
        srcs, outs = refs[:n], refs[n:2 * n]
        send_sems, recv_sems = refs[2 * n:]
        x, y, c = _position()

        def remote(a, off, size):
            return pltpu.make_async_remote_copy(
                src_ref=srcs[a].at[pl.ds(off, size), :], dst_ref=outs[a].at[pl.ds(off, size), :],
                send_sem=send_sems.at[a], recv_sem=recv_sems.at[a], device_id=(x, y, 1 - c), device_id_type=MESH)

        for a in range(n):
            _start_chunked(functools.partial(remote, a), srcs[a].shape[0])
        for a in range(n):
            remote(a, 0, srcs[a].shape[0]).wait()

    return pl.pallas_call(
        body, name="pair_join", in_specs=[HBM_SPEC] * n, out_specs=[HBM_SPEC] * n,
        out_shape=[jax.ShapeDtypeStruct(p.shape, p.dtype) for p in halves],
        scratch_shapes=[pltpu.SemaphoreType.DMA((n,)), pltpu.SemaphoreType.DMA((n,))],
    )(*halves)


def all_sum_small(part):
    n_dev = 8
    rows = part.shape[0]

    def body(src, out, buf, send_sems, recv_sems):
        x, y, c = _position()
        me = 4 * x + 2 * y + c
        buf[me] = src[...]
        copies = []
        for k in range(1, n_dev):
            px, py, pc = x ^ ((k >> 2) & 1), y ^ ((k >> 1) & 1), c ^ (k & 1)
            cp = pltpu.make_async_remote_copy(
                src_ref=src, dst_ref=buf.at[me], send_sem=send_sems.at[k - 1], recv_sem=recv_sems.at[k - 1],
                device_id=(px, py, pc), device_id_type=MESH)
            cp.start()
            copies.append(cp)
        for cp in copies:
            cp.wait()
        acc = buf[0]
        for k in range(1, n_dev):
            acc = acc + buf[k]
        out[...] = acc

    return pl.pallas_call(
        body, name="all_sum_small",
        in_specs=[pl.BlockSpec(memory_space=pltpu.VMEM)], out_specs=pl.BlockSpec(memory_space=pltpu.VMEM),
        out_shape=jax.ShapeDtypeStruct(part.shape, F32),
        scratch_shapes=[pltpu.VMEM((n_dev, rows, part.shape[1]), F32),
                        pltpu.SemaphoreType.DMA((n_dev - 1,)), pltpu.SemaphoreType.DMA((n_dev - 1,))],
    )(part)


def add_pairs(name, src, theirs, core, out_dtype):
    slabs, rows, cols = theirs.shape
    tm = _pick_tile(rows, ROW_TILE)
    nb = rows // tm

    def body(core_ref, a_ref, b_ref, o_ref):
        o_ref[...] = (a_ref[...].astype(F32) + b_ref[...].astype(F32)).astype(out_dtype)

    return pl.pallas_call(
        body, name=name,
        grid_spec=pltpu.PrefetchScalarGridSpec(
            num_scalar_prefetch=1, grid=(slabs, nb),
            in_specs=[pl.BlockSpec((None, tm, cols), lambda s, i, core_ref: (s, core_ref[0] * nb + i, 0)),
                      pl.BlockSpec((None, tm, cols), lambda s, i, core_ref: (s, i, 0))],
            out_specs=pl.BlockSpec((None, tm, cols), lambda s, i, core_ref: (s, i, 0))),
        out_shape=jax.ShapeDtypeStruct(theirs.shape, out_dtype),
        compiler_params=_cparams(("parallel", "parallel")),
    )(core, src, theirs)


def add_four(name, src, theirs, chip):
    _, rows, cols = theirs.shape
    tm = _pick_tile(rows, ROW_TILE)

    def body(chip_ref, a_ref, b_ref, o_ref):
        acc = a_ref[...].astype(F32)
        for j in range(N_CHIPS - 1):
            acc = acc + b_ref[j].astype(F32)
        o_ref[...] = acc

    return pl.pallas_call(
        body, name=name,
        grid_spec=pltpu.PrefetchScalarGridSpec(
            num_scalar_prefetch=1, grid=(rows // tm,),
            in_specs=[pl.BlockSpec((None, tm, cols), lambda i, chip_ref: (chip_ref[0], i, 0)),
                      pl.BlockSpec((N_CHIPS - 1, tm, cols), lambda i, chip_ref: (0, i, 0))],
            out_specs=pl.BlockSpec((tm, cols), lambda i, chip_ref: (i, 0))),
        out_shape=jax.ShapeDtypeStruct((rows, cols), F32),
        compiler_params=_cparams(("parallel",)),
    )(chip, src, theirs)


PACK_COLS = 1024
BIG_ROW_MULTIPLE = 512
SMALL_ROW_MULTIPLE = 32
BIG = ["dn_w_in", "dn_w_out", "cv_w_pw1", "cv_w_pw2", "xa_w_q", "xa_w_kv", "xa_w_o", "mlp_w_up", "mlp_w_down"]
SMALL = ["dn_w_conv", "cv_norm", "cv_b_pw1", "cv_w_dw", "cv_b_dw", "cv_ln_g", "cv_ln_b", "cv_b_pw2"]
SHARD_AXIS = {"dn_w_in": 2, "dn_w_conv": 2, "dn_w_out": 1, "cv_norm": 1, "cv_w_pw1": 2, "cv_b_pw1": 1,
              "cv_w_dw": 2, "cv_b_dw": 1, "cv_ln_g": 1, "cv_ln_b": 1, "cv_w_pw2": 1, "cv_b_pw2": 1,
              "xa_w_q": 1, "xa_w_kv": 2, "xa_w_o": 1, "mlp_w_up": 2, "mlp_w_down": 1}
REPLICATED = ["dn_norm", "dn_a_log", "dn_dt_bias", "dn_out_norm", "xa_norm", "xa_mem_norm", "mlp_norm", "final_norm"]


def _pack_rows(size):
    return -(-size // PACK_COLS)


SHARD_SHAPES = {
    "dn_w_in": (1, 1024, 1028), "dn_w_conv": (1, 4, 768), "dn_w_out": (1, 256, 1024), "cv_norm": (1, 256),
    "cv_w_pw1": (1, 1024, 512), "cv_b_pw1": (1, 512), "cv_w_dw": (1, 31, 256), "cv_b_dw": (1, 256),
    "cv_ln_g": (1, 256), "cv_ln_b": (1, 256), "cv_w_pw2": (1, 256, 1024), "cv_b_pw2": (1, 256),
    "xa_w_q": (2, 256, 1024), "xa_w_kv": (2, 1024, 512), "xa_w_o": (2, 256, 1024),
    "mlp_w_up": (2, 1024, 1024), "mlp_w_down": (2, 1024, 1024)}


def _shard_shape(nm):
    return SHARD_SHAPES[nm]


def _pack(tensors, names, dtype, row_multiple):
    pieces = []
    for nm in names:
        t = tensors[nm]
        flat = t.reshape(t.shape[0], -1) if t.ndim > len(_shard_shape(nm)) else t.reshape(1, -1)
        pad = _pack_rows(flat.shape[1]) * PACK_COLS - flat.shape[1]
        pieces.append(jnp.pad(flat.astype(dtype), ((0, 0), (0, pad))))
    cat = jnp.concatenate(pieces, axis=1)
    rows = cat.shape[1] // PACK_COLS
    total = -(-rows // row_multiple) * row_multiple
    cat = jnp.pad(cat, ((0, 0), (0, (total - rows) * PACK_COLS)))
    return cat.reshape(cat.shape[0], total, PACK_COLS)


def _unpack(pack, names):
    lead = pack.shape[:-2]
    flat = pack.reshape(lead + (-1,))
    out, off = {}, 0
    for nm in names:
        shp = _shard_shape(nm)
        size = 1
        for s in shp:
            size *= s
        out[nm] = flat[..., off:off + size].reshape(lead + shp)
        off += _pack_rows(size) * PACK_COLS
    return out


def _to_full(nm, stacked):
    ax = SHARD_AXIS[nm]
    moved = jnp.moveaxis(stacked, 0, ax)
    shp = list(_shard_shape(nm))
    shp[ax] *= N_CHIPS
    return moved.reshape(shp)


def _to_shards(nm, full):
    ax = SHARD_AXIS[nm]
    shp = list(_shard_shape(nm))
    split = full.reshape(shp[:ax] + [N_CHIPS, shp[ax]] + shp[ax + 1:])
    return jnp.moveaxis(split, ax, 0)


def _row(v):
    return v.reshape(1, -1)


class Stacked:
    def __init__(self, arr, split, layer):
        self.arr, self.kw = arr, dict(b_split=split, b_layer=layer)


def _grad_out(split, layer, n_layers, into):
    return dict(out_dtype=BF16, out_split=split, out_layer=(layer, n_layers), out_into=into)


def mlp_fwd(tag, h, g, w_up, w_down):
    n = rms_fwd(tag + "_norm", h, g)
    up = mm(tag + "_up", n, w_up.arr, **w_up.kw)
    out = mm(tag + "_down", up, w_down.arr, pro=lambda t: jnp.square(jnp.maximum(t, 0.0)),
             epi=lambda acc, res: acc + res, epi_tiles=(h,), **w_down.kw)
    return out, (n, up)


def mlp_bwd(tag, dh, h, g, w_up, w_down, saved, layer, into):
    n, up = saved
    dup = mm(tag + "_d_act", dh, w_down.arr, tb=True, out_dtype=BF16,
             epi=lambda acc, t: acc * (2.0 * jnp.maximum(t, 0.0)), epi_tiles=(up,), **w_down.kw)
    dw_down = mm(tag + "_dw_down", up, dh, ta=True, pro=lambda t: jnp.square(jnp.maximum(t, 0.0)), tk=512,
                 **_grad_out("rows", layer, 2, into[1]))
    dn = mm(tag + "_dn", dup, w_up.arr, tb=True, **w_up.kw)
    dw_up = mm(tag + "_dw_up", n, dup, ta=True, tk=512, **_grad_out("cols", layer, 2, into[0]))
    dh_in, dg = rms_bwd(tag + "_norm_bwd", dn, h, g, dh)
    return dh_in, dg, (dw_up, dw_down)


def xa_fwd(tag, h, mem, g, g_mem, w_q, w_kv, w_o):
    n = rms_fwd(tag + "_norm", h, g)
    mem_n = rms_fwd(tag + "_mem_norm", mem, g_mem)
    q = mm(tag + "_q", n, w_q.arr, out_dtype=BF16, **w_q.kw)
    kv = mm(tag + "_kv", mem_n, w_kv.arr, out_dtype=BF16, **w_kv.kw)
    o = xa_core_fwd(tag + "_core", q, kv)
    out = mm(tag + "_o", o, w_o.arr, epi=lambda acc, res: acc + res, epi_tiles=(h,), **w_o.kw)
    return out, (n, mem_n, q, kv, o)


def xa_bwd(tag, dh, h, mem, g, g_mem, w_q, w_kv, w_o, saved, layer, into):
    n, mem_n, q, kv, o = saved
    d_o = mm(tag + "_d_o", dh, w_o.arr, tb=True, out_dtype=BF16, **w_o.kw)
    dw_o = mm(tag + "_dw_o", o, dh, ta=True, tk=512, **_grad_out("rows", layer, 2, into[2]))
    dq, dkv = xa_core_bwd(tag + "_core_bwd", d_o, q, kv)
    dn = mm(tag + "_dn", dq, w_q.arr, tb=True, **w_q.kw)
    dw_q = mm(tag + "_dw_q", n, dq, ta=True, tk=512, **_grad_out("rows", layer, 2, into[0]))
    dh_in, dg = rms_bwd(tag + "_norm_bwd", dn, h, g, dh)
    dw_kv = mm(tag + "_dw_kv", mem_n, dkv, ta=True, **_grad_out("cols", layer, 2, into[1]))
    dmem_n = mm(tag + "_dmem", dkv, w_kv.arr, tb=True, **w_kv.kw)
    dg_mem = mem_norm_bwd(tag + "_mem_norm_bwd", dmem_n, mem, g_mem)
    return dh_in, dg, dg_mem, (dw_q, dw_kv, dw_o)


def _gate_tile(a_log, dt_bias):
    t = jnp.zeros((8, LANES), F32)
    t = t.at[0, DN_HEADS:2 * DN_HEADS].set(a_log.reshape(-1))
    return t.at[1, DN_HEADS:2 * DN_HEADS].set(dt_bias.reshape(-1))


def dn_fwd(h, g, w_qkv, w_z, w_ba, w_conv, gate, out_norm, w_out):
    n = rms_fwd("dn_norm", h, g)
    qkv_raw = mm("dn_proj_qkv", n, w_qkv)
    z = mm("dn_proj_z", n, w_z)
    ba = mm("dn_proj_ba", n, w_ba)
    qkv, hs = dn_pre(qkv_raw, ba, w_conv, gate)
    u, w, t_inv = dn_solve(qkv, hs)
    o, states = dn_scan_fwd(qkv, u, w, hs)
    og = dn_post(o, z, out_norm)
    out = mm("dn_out", og, w_out.arr, epi=lambda acc, res: acc + res, epi_tiles=(h,), **w_out.kw)
    return out, (n, qkv_raw, z, ba, qkv, hs, u, w, t_inv, o, states, og)


def dn_bwd(dh, h, g, w_qkv, w_z, w_ba, w_conv, gate, out_norm, w_out, saved):
    n, qkv_raw, z, ba, qkv, hs, u, w, t_inv, o, states, og = saved
    d_og = mm("dn_d_og", dh, w_out.arr, tb=True, out_dtype=BF16, **w_out.kw)
    dw_out = mm("dn_dw_out", og, dh, ta=True, tk=512, **_grad_out("rows", 0, 1, None))
    d_o, dz, d_out_norm = dn_post_bwd(d_og, o, z, out_norm)
    dq, dk, dv, dhs = dn_scan_bwd(qkv, u, w, t_inv, hs, states, d_o)
    dc, dba, d_gate = dn_pre_bwd(dq, dk, dv, dhs, qkv_raw, ba, w_conv, gate)
    dqkv_raw, dw_conv = dn_conv_bwd(dc, qkv_raw, w_conv)
    dn = mm("dn_dn_qkv", dqkv_raw, w_qkv, tb=True)
    dn = mm("dn_dn_z", dz, w_z, tb=True, epi=lambda acc, t: acc + t, epi_tiles=(dn,))
    dn = mm("dn_dn_ba", dba, w_ba, tb=True, epi=lambda acc, t: acc + t, epi_tiles=(dn,))
    dw_qkv = mm("dn_dw_qkv", n, dqkv_raw, ta=True, tk=512)
    dw_z = mm("dn_dw_z", n, dz, ta=True, tk=512)
    dw_ba = mm("dn_dw_ba", n, dba, ta=True, tk=512)
    dh_in, dg = rms_bwd("dn_norm_bwd", dn, h, g, dh)
    return dh_in, dg, dw_qkv, dw_z, dw_ba, dw_conv, d_gate, d_out_norm, dw_out


def cv_fwd(h, g, w_pw1, b_pw1, w_dw, b_dw, ln_g, ln_b, w_pw2, b_pw2):
    n = rms_fwd("cv_norm", h, g)
    u = mm("cv_pw1", n, w_pw1.arr, epi=lambda acc, b: acc + b, epi_rows=(b_pw1,), **w_pw1.kw)
    s, c = cv_core_fwd(u, w_dw, b_dw, ln_g, ln_b)
    out = mm("cv_pw2", s, w_pw2.arr, epi=lambda acc, res, b: acc + res + b, epi_tiles=(h,), epi_rows=(b_pw2,),
             **w_pw2.kw)
    return out, (n, u, s, c)


def cv_bwd(dh, h, g, w_pw1, w_dw, ln_g, ln_b, w_pw2, saved):
    n, u, s, c = saved
    ds = mm("cv_d_s", dh, w_pw2.arr, tb=True, out_dtype=BF16, **w_pw2.kw)
    dw_pw2 = mm("cv_dw_pw2", s, dh, ta=True, tk=512, **_grad_out("rows", 0, 1, None))
    db_pw2 = col_sum("cv_db_pw2", dh)
    dc, ln_acc = cv_ln_bwd(ds, c, ln_g, ln_b)
    du, dw_dw, db_pw1 = cv_conv_bwd(dc, u, w_dw)
    dn = mm("cv_dn", du, w_pw1.arr, tb=True, **w_pw1.kw)
    dw_pw1 = mm("cv_dw_pw1", n, du, ta=True, tk=512, **_grad_out("cols", 0, 1, None))
    dh_in, dg = rms_bwd("cv_norm_bwd", dn, h, g, dh)
    return dh_in, dg, dw_pw1, db_pw1, dw_dw, ln_acc, dw_pw2, db_pw2


WEIGHTS = ["dn_norm", "dn_w_in", "dn_w_conv", "dn_a_log", "dn_dt_bias", "dn_out_norm", "dn_w_out", "cv_norm",
           "cv_w_pw1", "cv_b_pw1", "cv_w_dw", "cv_b_dw", "cv_ln_g", "cv_ln_b", "cv_w_pw2", "cv_b_pw2", "xa_norm",
           "xa_mem_norm", "xa_w_q", "xa_w_kv", "xa_w_o", "mlp_norm", "mlp_w_up", "mlp_w_down", "final_norm"]


def _as_2d(t):
    if t.ndim == 1:
        return t.reshape(1, -1)
    return t.reshape(-1, t.shape[-1])


def kernel(x, mem, dn_norm, dn_w_in, dn_w_conv, dn_a_log, dn_dt_bias, dn_out_norm, dn_w_out, cv_norm, cv_w_pw1, cv_b_pw1, cv_w_dw, cv_b_dw, cv_ln_g, cv_ln_b, cv_w_pw2, cv_b_pw2, xa_norm, xa_mem_norm, xa_w_q, xa_w_kv, xa_w_o, mlp_norm, mlp_w_up, mlp_w_down, final_norm, loss_target, m_dn_norm, m_dn_w_in, m_dn_w_conv, m_dn_a_log, m_dn_dt_bias, m_dn_out_norm, m_dn_w_out, m_cv_norm, m_cv_w_pw1, m_cv_b_pw1, m_cv_w_dw, m_cv_b_dw, m_cv_ln_g, m_cv_ln_b, m_cv_w_pw2, m_cv_b_pw2, m_xa_norm, m_xa_mem_norm, m_xa_w_q, m_xa_w_kv, m_xa_w_o, m_mlp_norm, m_mlp_w_up, m_mlp_w_down, m_final_norm, v_dn_norm, v_dn_w_in, v_dn_w_conv, v_dn_a_log, v_dn_dt_bias, v_dn_out_norm, v_dn_w_out, v_cv_norm, v_cv_w_pw1, v_cv_b_pw1, v_cv_w_dw, v_cv_b_dw, v_cv_ln_g, v_cv_ln_b, v_cv_w_pw2, v_cv_b_pw2, v_xa_norm, v_xa_mem_norm, v_xa_w_q, v_xa_w_kv, v_xa_w_o, v_mlp_norm, v_mlp_w_up, v_mlp_w_down, v_final_norm):
    args = dict(locals())
    wts = {nm: args[nm] for nm in WEIGHTS}
    mom = {nm: args["m_" + nm] for nm in WEIGHTS}
    var = {nm: args["v_" + nm] for nm in WEIGHTS}
    core = lax.axis_index("c").astype(jnp.int32).reshape(1)
    chip = (2 * lax.axis_index("x") + lax.axis_index("y")).astype(jnp.int32)
    sources = [wts[nm].astype(BF16).reshape(-1, wts[nm].shape[-1]) for nm in BIG]
    sources.append(_pack(wts, SMALL, F32, SMALL_ROW_MULTIPLE)[0])
    gathered = [lax.dynamic_update_slice(got, src[None], (chip, 0, 0))
                for got, src in zip(gather_shards(sources), sources)]
    stacked = {nm: got.reshape((N_CHIPS,) + SHARD_SHAPES[nm]) for nm, got in zip(BIG, gathered)}
    full = {nm: _to_full(nm, t) for nm, t in _unpack(gathered[-1], SMALL).items()}
    full["dn_w_in"] = _to_full("dn_w_in", stacked.pop("dn_w_in"))
    full.update({nm: wts[nm] for nm in REPLICATED})

    dh, grads, rep = local_step(x[0], mem[0], loss_target[0], stacked, full)

    grads["dn_w_in"] = _to_shards("dn_w_in", grads["dn_w_in"]).astype(BF16)
    parts = [grads[nm].reshape(N_CHIPS, -1, grads[nm].shape[-1]) for nm in BIG]
    parts.append(_pack({nm: _to_shards(nm, grads[nm]) for nm in SMALL}, SMALL, F32, SMALL_ROW_MULTIPLE))
    names = BIG + ["small"]
    wire = [BF16] * len(BIG) + [F32]
    theirs = pair_split(parts)
    pairs = [add_pairs("pair_add_" + nm, p, t, core, dt) for nm, p, t, dt in zip(names, parts, theirs, wire)]
    others = chip_scatter(pairs)
    halves = [add_four("chip_add_" + nm, p, o, chip.reshape(1)) for nm, p, o in zip(names, pairs, others)]
    siblings = pair_join(halves)
    south = core[0] == 0
    small_red = jnp.concatenate([jnp.where(south, halves[-1], siblings[-1]),
                                 jnp.where(south, siblings[-1], halves[-1])], axis=0)
    red = _unpack(small_red, SMALL)

    rep = all_sum_small(rep)
    red["dn_norm"] = rep[0:1]
    red["dn_a_log"] = rep[1:2, DN_HEADS:2 * DN_HEADS]
    red["dn_dt_bias"] = rep[2:3, DN_HEADS:2 * DN_HEADS]
    red["dn_out_norm"] = rep[3:4, :LANES]
    red["xa_norm"], red["xa_mem_norm"], red["mlp_norm"] = rep[4:6], rep[6:8], rep[8:10]
    red["final_norm"] = rep[10]
    loss = rep[11, 0]

    delta, new_m, new_v = {}, {}, {}
    for nm in WEIGHTS:
        shp = wts[nm].shape
        if nm in BIG:
            i = BIG.index(nm)
            res = adamw_halves("adamw_" + nm, _as_2d(wts[nm]), halves[i], siblings[i], _as_2d(mom[nm]),
                               _as_2d(var[nm]), core)
            red[nm] = res[0]
            res = res[1:]
        else:
            res = adamw("adamw_" + nm, _as_2d(wts[nm]), _as_2d(red[nm].reshape(shp)), _as_2d(mom[nm]),
                        _as_2d(var[nm]))
        delta[nm], new_m[nm], new_v[nm] = (r.reshape(shp) for r in res)
        red[nm] = red[nm].reshape(shp)

    grad_x = dh[None]
    return (loss, grad_x, *[red[nm] for nm in WEIGHTS], *[delta[nm] for nm in WEIGHTS],
            *[new_m[nm] for nm in WEIGHTS], *[new_v[nm] for nm in WEIGHTS])


def local_step(h0, mem0, target, stacked, full):
    d = h0.shape[1]
    dn_norm, dn_a_log, dn_dt_bias, dn_out_norm = (full[nm] for nm in REPLICATED[:4])
    xa_norm, xa_mem_norm, mlp_norm, final_norm = (full[nm] for nm in REPLICATED[4:])
    inner = DN_HEADS * DN_HEAD_DIM
    w_in = full["dn_w_in"][0]
    w_qkv, w_z = w_in[:, :3 * inner], w_in[:, 3 * inner:4 * inner]
    w_ba = jnp.pad(w_in[:, 4 * inner:], ((0, 0), (0, LANES - 2 * DN_HEADS)))
    w_conv = jnp.pad(full["dn_w_conv"][0], ((0, 8 - DN_CONV), (0, 0)))
    gate = _gate_tile(dn_a_log, dn_dt_bias)
    w_dw = jnp.pad(full["cv_w_dw"][0], ((0, CV_HALO - CV_WIDTH), (0, 0)))

    def sw(nm, layer):
        return Stacked(stacked[nm], "rows" if SHARD_AXIS[nm] == 1 else "cols", layer)

    dn_args = (_row(dn_norm), w_qkv, w_z, w_ba, w_conv, gate, _row(dn_out_norm), sw("dn_w_out", 0))
    h1, dn_saved = dn_fwd(h0, *dn_args)
    xa_args = [(_row(xa_norm[l]), _row(xa_mem_norm[l]), sw("xa_w_q", l), sw("xa_w_kv", l), sw("xa_w_o", l))
               for l in range(2)]
    mlp_args = [(_row(mlp_norm[l]), sw("mlp_w_up", l), sw("mlp_w_down", l)) for l in range(2)]
    h2, xa0_saved = xa_fwd("xa0", h1, mem0, *xa_args[0])
    h3, mlp0_saved = mlp_fwd("mlp0", h2, *mlp_args[0])
    cv_args = (_row(full["cv_norm"][0]), sw("cv_w_pw1", 0), full["cv_b_pw1"], w_dw, full["cv_b_dw"],
               full["cv_ln_g"], full["cv_ln_b"], sw("cv_w_pw2", 0), full["cv_b_pw2"])
    h4, cv_saved = cv_fwd(h3, *cv_args)
    h5, xa1_saved = xa_fwd("xa1", h4, mem0, *xa_args[1])
    h6, mlp1_saved = mlp_fwd("mlp1", h5, *mlp_args[1])

    dh, loss_tile, d_final = loss_head("loss_head", h6, _row(final_norm), target)
    grads = {}
    dg_mlp, dg_xa, dg_xa_mem = [None, None], [None, None], [None, None]
    dh, dg_mlp[1], dw_mlp = mlp_bwd("mlp1", dh, h5, *mlp_args[1], mlp1_saved, 1, (None, None))
    dh, dg_xa[1], dg_xa_mem[1], dw_xa = xa_bwd("xa1", dh, h4, mem0, *xa_args[1], xa1_saved, 1, (None, None, None))
    (dh, grads["cv_norm"], grads["cv_w_pw1"], grads["cv_b_pw1"], dw_dw, ln_acc, grads["cv_w_pw2"],
     grads["cv_b_pw2"]) = cv_bwd(dh, h3, cv_args[0], cv_args[1], w_dw, cv_args[5], cv_args[6], cv_args[7], cv_saved)
    dh, dg_mlp[0], dw_mlp = mlp_bwd("mlp0", dh, h2, *mlp_args[0], mlp0_saved, 0, dw_mlp)
    dh, dg_xa[0], dg_xa_mem[0], dw_xa = xa_bwd("xa0", dh, h1, mem0, *xa_args[0], xa0_saved, 0, dw_xa)
    (dh, dg_dn, dw_qkv, dw_z, dw_ba, dw_conv, d_gate, d_out_norm,
     grads["dn_w_out"]) = dn_bwd(dh, h0, *dn_args, dn_saved)

    grads["dn_w_in"] = jnp.concatenate([dw_qkv, dw_z, dw_ba[:, :2 * DN_HEADS]], axis=1)[None]
    grads["dn_w_conv"] = dw_conv[None, :DN_CONV]
    grads["cv_w_dw"] = dw_dw[None, :CV_WIDTH]
    grads["cv_ln_g"], grads["cv_ln_b"], grads["cv_b_dw"] = ln_acc[0:1], ln_acc[1:2], ln_acc[2:3]
    grads["xa_w_q"], grads["xa_w_kv"], grads["xa_w_o"] = dw_xa
    grads["mlp_w_up"], grads["mlp_w_down"] = dw_mlp

    rep = jnp.zeros((16, d), F32)
    rep = rep.at[0].set(dg_dn[0])
    rep = rep.at[1, :LANES].set(d_gate[0])
    rep = rep.at[2, :LANES].set(d_gate[1])
    rep = rep.at[3, :LANES].set(d_out_norm[0])
    rep = rep.at[4].set(dg_xa[0][0]).at[5].set(dg_xa[1][0])
    rep = rep.at[6].set(dg_xa_mem[0][0]).at[7].set(dg_xa_mem[1][0])
    rep = rep.at[8].set(dg_mlp[0][0]).at[9].set(dg_mlp[1][0])
    rep = rep.at[10].set(d_final[0])
    rep = rep.at[11, :LANES].set(loss_tile[0])
    return dh, grads, rep
```

```python
import functools

import jax
import jax.numpy as jnp
from jax import lax
from jax.experimental import pallas as pl
from jax.experimental.pallas import tpu as pltpu

F32 = jnp.float32
BF16 = jnp.bfloat16
HIGHEST = lax.Precision.HIGHEST
MESH = pl.DeviceIdType.MESH

D_MODEL = 1024
DN_HEADS = 8
DN_HEAD_DIM = 128
DN_CONV = 4
DN_CHUNK = 64
CV_WIDTH = 31
XA_HEADS = 4
XA_HEAD_DIM = 256
RMS_EPS = 1e-6
LN_EPS = 1e-5
L2_EPS = 1e-6

ADAM_LR = 0.001
ADAM_B1 = 0.9
ADAM_B2 = 0.999
ADAM_EPS = 1e-08
ADAM_WD = 0.01
ADAM_STEP = 10

LANES = 128
ROW_TILE = 512
CONV_ROW_TILE = 256
MM_TILE = 1024
ADAMW_ROW_TILE = 256
DN_ROW_TILE = 256
CHUNK_SHIFT = 6
SOLVE_INTERLEAVE = 8
FWD_HEADS_PER_STEP = 4
BWD_HEADS_PER_STEP = 4
DN_HALO = 8
CV_HALO = 32
VMEM_LIMIT = 48 * 1024 * 1024
N_CHIPS = 4
D2D_CHUNK_ROWS = 256


def _cparams(sem):
    return pltpu.CompilerParams(dimension_semantics=sem, vmem_limit_bytes=VMEM_LIMIT)


def _dot(a, b, dims=(((1,), (0,)), ((), ()))):
    return lax.dot_general(a.astype(BF16), b.astype(BF16), dims, preferred_element_type=F32)


def _dot_nt(a, b):
    return _dot(a, b, (((1,), (1,)), ((), ())))


def _dot_tn(a, b):
    return _dot(a, b, (((0,), (0,)), ((), ())))


def _dot_hi(a, b, dims=(((1,), (0,)), ((), ()))):
    return lax.dot_general(a.astype(F32), b.astype(F32), dims, precision=HIGHEST, preferred_element_type=F32)


def _sigmoid(x):
    return 1.0 / (1.0 + jnp.exp(-x))


def _silu(x):
    return x * _sigmoid(x)


def _silu_grad(x):
    s = _sigmoid(x)
    return s * (1.0 + x * (1.0 - s))


def _softplus(x):
    return jnp.maximum(x, 0.0) + jnp.log(1.0 + jnp.exp(-jnp.abs(x)))


def _iota(shape, dim):
    return lax.broadcasted_iota(jnp.int32, shape, dim)


def _lane_col(vals, lane, idx):
    return jnp.sum(jnp.where(lane == idx, vals, 0.0), axis=1, keepdims=True)


def _pick_tile(rows, cap):
    best = rows
    for t in range(16, min(rows, cap) + 1, 16):
        if rows % t == 0:
            best = t
    return best


def _stacked_spec(shape, split, layer, rows, cols, block_index):
    per_chip = (shape[-2] // rows) if split == "rows" else (shape[-1] // cols)
    assert (shape[-2] % rows == 0) and (shape[-1] % cols == 0)
    lead = (None,) if layer is None else (None, None)

    def index(i, j, kk):
        bi, bj = block_index(i, j, kk)
        mid = () if layer is None else (layer,)
        if split == "rows":
            return (bi // per_chip,) + mid + (bi % per_chip, bj)
        return (bj // per_chip,) + mid + (bi, bj % per_chip)

    return pl.BlockSpec(lead + (rows, cols), index)


def mm(name, a, b, *, ta=False, tb=False, out_dtype=F32, pro=None, epi=None, epi_tiles=(), epi_rows=(),
       tm=MM_TILE, tn=MM_TILE, tk=MM_TILE, b_split=None, b_layer=None, out_split=None, out_layer=None,
       out_into=None):
    m, k = (a.shape[1], a.shape[0]) if ta else a.shape
    b_rows, b_cols = b.shape[-2], b.shape[-1]
    if b_split == "rows":
        b_rows *= N_CHIPS
    elif b_split == "cols":
        b_cols *= N_CHIPS
    n = b_rows if tb else b_cols
    assert (b_cols if tb else b_rows) == k
    tm, tn, tk = min(tm, m), min(tn, n), min(tk, k)
    if b_split is not None:
        lim_r, lim_c = b.shape[-2], b.shape[-1]
        if tb:
            tn, tk = min(tn, lim_r), min(tk, lim_c)
        else:
            tk, tn = min(tk, lim_r), min(tn, lim_c)
    if out_split == "rows":
        tm = min(tm, m // N_CHIPS)
    elif out_split == "cols":
        tn = min(tn, n // N_CHIPS)
    assert m % tm == 0 and n % tn == 0 and k % tk == 0
    nk = k // tk
    a_spec = pl.BlockSpec((tk, tm), lambda i, j, kk: (kk, i)) if ta else pl.BlockSpec((tm, tk), lambda i, j, kk: (i, kk))
    b_block = (tn, tk) if tb else (tk, tn)
    b_index = (lambda i, j, kk: (j, kk)) if tb else (lambda i, j, kk: (kk, j))
    if b_split is None:
        b_spec = pl.BlockSpec(b_block, b_index)
    else:
        b_spec = _stacked_spec(b.shape, b_split, b_layer, b_block[0], b_block[1], b_index)
    in_specs = [a_spec, b_spec]
    in_specs += [pl.BlockSpec((tm, tn), lambda i, j, kk: (i, j)) for _ in epi_tiles]
    in_specs += [pl.BlockSpec((1, tn), lambda i, j, kk: (0, j)) for _ in epi_rows]
    n_t, n_r = len(epi_tiles), len(epi_rows)
    dims = (((0 if ta else 1,), (1 if tb else 0,)), ((), ()))
    if out_split is None:
        out_shape = jax.ShapeDtypeStruct((m, n), out_dtype)
        out_spec = pl.BlockSpec((tm, tn), lambda i, j, kk: (i, j))
    else:
        shard = (m // N_CHIPS, n) if out_split == "rows" else (m, n // N_CHIPS)
        layers = () if out_layer is None else (out_layer[1],)
        out_shape = jax.ShapeDtypeStruct((N_CHIPS,) + layers + shard, out_dtype)
        out_spec = _stacked_spec(out_shape.shape, out_split, None if out_layer is None else out_layer[0], tm, tn,
                                 lambda i, j, kk: (i, j))
    extra, aliases = [], {}
    if out_into is not None:
        extra = [out_into]
        in_specs.append(pl.BlockSpec(memory_space=pl.ANY))
        aliases = {2 + n_t + n_r: 0}

    def body(a_ref, b_ref, *rest):
        tiles = rest[:n_t]
        rows = rest[n_t:n_t + n_r]
        rest = rest[n_t + n_r + len(extra):]
        o_ref, acc_ref = rest[0], rest[1]
        kk = pl.program_id(2)

        @pl.when(kk == 0)
        def _():
            acc_ref[...] = jnp.zeros_like(acc_ref)

        av = a_ref[...]
        if pro is not None:
            av = pro(av)
        acc_ref[...] += _dot(av, b_ref[...], dims)

        @pl.when(kk == nk - 1)
        def _():
            out = acc_ref[...]
            if epi is not None:
                out = epi(out, *[t[...] for t in tiles], *[r[...] for r in rows])
            o_ref[...] = out.astype(out_dtype)

    return pl.pallas_call(
        body, name=name, grid=(m // tm, n // tn, nk),
        in_specs=in_specs, out_specs=out_spec, out_shape=out_shape,
        scratch_shapes=[pltpu.VMEM((tm, tn), F32)], input_output_aliases=aliases,
        compiler_params=_cparams(("parallel", "parallel", "arbitrary")),
    )(a, b, *epi_tiles, *epi_rows, *extra)


def row_call(name, body, n_rows, tm, ins, outs, accs=()):
    tm = _pick_tile(n_rows, tm)
    in_specs = []
    for arr, kind in ins:
        if kind == "tile":
            if arr.ndim == 2:
                in_specs.append(pl.BlockSpec((tm, arr.shape[1]), lambda i: (i, 0)))
            else:
                in_specs.append(pl.BlockSpec((arr.shape[0], tm, arr.shape[2]), lambda i: (0, i, 0)))
        elif kind == "full":
            in_specs.append(pl.BlockSpec(arr.shape, functools.partial(lambda i, nd: (0,) * nd, nd=arr.ndim)))
        else:
            where, h = kind
            per = tm // h
            if where == "prev":
                in_specs.append(pl.BlockSpec((h, arr.shape[1]), functools.partial(
                    lambda i, per: (jnp.maximum(i * per - 1, 0), 0), per=per)))
            else:
                last = n_rows // h - 1
                in_specs.append(pl.BlockSpec((h, arr.shape[1]), functools.partial(
                    lambda i, per, last: (jnp.minimum((i + 1) * per, last), 0), per=per, last=last)))
    out_shape, out_specs = [], []
    for shape, dtype in outs:
        out_shape.append(jax.ShapeDtypeStruct(shape, dtype))
        if len(shape) == 2:
            out_specs.append(pl.BlockSpec((tm, shape[1]), lambda i: (i, 0)))
        else:
            out_specs.append(pl.BlockSpec((shape[0], tm, shape[2]), lambda i: (0, i, 0)))
    for shape in accs:
        out_shape.append(jax.ShapeDtypeStruct(shape, F32))
        out_specs.append(pl.BlockSpec(shape, lambda i: (0, 0)))
    n_in, n_out, n_acc = len(ins), len(outs), len(accs)

    def kern(*refs):
        i = pl.program_id(0)
        in_refs = refs[:n_in]
        out_refs = refs[n_in:n_in + n_out]
        acc_refs = refs[n_in + n_out:n_in + n_out + n_acc]
        if n_acc:
            @pl.when(i == 0)
            def _():
                for r in acc_refs:
                    r[...] = jnp.zeros_like(r)
        body(i, in_refs, out_refs, acc_refs)

    res = pl.pallas_call(
        kern, name=name, grid=(n_rows // tm,), in_specs=in_specs, out_specs=out_specs, out_shape=out_shape,
        compiler_params=_cparams(("arbitrary",) if n_acc else ("parallel",)),
    )(*[a for a, _ in ins])
    return list(res)


def _rms_stats(h):
    r = lax.rsqrt(jnp.mean(h * h, axis=-1, keepdims=True) + RMS_EPS)
    return h * r, r


def rms_fwd(name, h, g):
    def body(i, ins, outs, accs):
        xhat, _ = _rms_stats(ins[0][...])
        outs[0][...] = (xhat * ins[1][...]).astype(BF16)

    return row_call(name, body, h.shape[0], ROW_TILE, [(h, "tile"), (g, "full")], [(h.shape, BF16)])[0]


def _rms_bwd_tile(dn, h, g):
    xhat, r = _rms_stats(h)
    dxhat = dn * g
    dh = r * (dxhat - xhat * jnp.mean(dxhat * xhat, axis=-1, keepdims=True))
    dg = jnp.sum(dn * xhat, axis=0, keepdims=True)
    return dh, dg


def rms_bwd(name, dn, h, g, dres):
    def body(i, ins, outs, accs):
        dh, dg = _rms_bwd_tile(ins[0][...].astype(F32), ins[1][...], ins[2][...])
        outs[0][...] = ins[3][...] + dh
        accs[0][...] += dg

    d = h.shape[1]
    out, dg = row_call(name, body, h.shape[0], ROW_TILE,
                       [(dn, "tile"), (h, "tile"), (g, "full"), (dres, "tile")], [(h.shape, F32)], [(1, d)])
    return out, dg


def mem_norm_bwd(name, dn, mem, g):
    def body(i, ins, outs, accs):
        _, dg = _rms_bwd_tile(ins[0][...].astype(F32), ins[1][...], ins[2][...])
        accs[0][...] += dg

    return row_call(name, body, mem.shape[0], ROW_TILE, [(dn, "tile"), (mem, "tile"), (g, "full")], [],
                    [(1, mem.shape[1])])[0]


def loss_head(name, h, g, target):
    d = h.shape[1]

    def body(i, ins, outs, accs):
        hv, gv = ins[0][...], ins[1][...]
        xhat, _ = _rms_stats(hv)
        err = xhat * gv - ins[2][...]
        dy = err * (1.0 / d)
        dh, dg = _rms_bwd_tile(dy, hv, gv)
        outs[0][...] = dh
        accs[0][...] += jnp.full((8, LANES), 0.5 / d, F32) * jnp.sum(err * err)
        accs[1][...] += dg

    dh, loss, dg = row_call(name, body, h.shape[0], ROW_TILE, [(h, "tile"), (g, "full"), (target, "tile")],
                            [(h.shape, F32)], [(8, LANES), (1, d)])
    return dh, loss, dg


def col_sum(name, x):
    def body(i, ins, outs, accs):
        accs[0][...] += jnp.sum(ins[0][...].astype(F32), axis=0, keepdims=True)

    return row_call(name, body, x.shape[0], ROW_TILE, [(x, "tile")], [], [(1, x.shape[1])])[0]


def _conv_taps(xcat, w_ref, cols, width, halo, tm):
    rows = halo + tm
    acc = None
    for j in range(width):
        s = width - 1 - j
        xs = xcat if s == 0 else pltpu.roll(xcat, s, 0)
        term = xs[halo:rows] * w_ref[j:j + 1, cols]
        acc = term if acc is None else acc + term
    return acc


def _conv_taps_bwd_x(dcat, w_ref, cols, width, halo, tm):
    rows = halo + tm
    acc = None
    for j in range(width):
        s = width - 1 - j
        ds = dcat if s == 0 else pltpu.roll(dcat, rows - s, 0)
        term = ds[0:tm] * w_ref[j:j + 1, cols]
        acc = term if acc is None else acc + term
    return acc


def _conv_taps_bwd_w(dy, xcat, width, halo, tm, wrows):
    rows = halo + tm
    rid = _iota((wrows, dy.shape[1]), 0)
    out = jnp.zeros((wrows, dy.shape[1]), F32)
    for j in range(width):
        s = width - 1 - j
        xs = xcat if s == 0 else pltpu.roll(xcat, s, 0)
        v = jnp.sum(dy * xs[halo:rows], axis=0, keepdims=True)
        out = out + jnp.where(rid == j, v, 0.0)
    return out


def dn_pre(qkv_raw, ba, w_conv, gate):
    s_len = qkv_raw.shape[0]
    tm = min(DN_ROW_TILE, s_len)
    n_blk = qkv_raw.shape[1] // LANES

    def body(i, ins, outs, accs):
        x_ref, xp_ref, ba_ref, w_ref, gate_ref = ins
        qkv_ref, hs_ref = outs

        def blk(cb, carry):
            cols = pl.ds(pl.multiple_of(cb * LANES, LANES), LANES)
            prev = jnp.where(i > 0, xp_ref[:, cols], 0.0)
            xcat = jnp.concatenate([prev, x_ref[:, cols]], axis=0)
            c = _conv_taps(xcat, w_ref, cols, DN_CONV, DN_HALO, tm)
            y = _silu(c)
            rs = lax.rsqrt(jnp.sum(y * y, axis=-1, keepdims=True) + L2_EPS)
            fac = jnp.where(cb < DN_HEADS, DN_HEAD_DIM ** -0.5, 1.0)
            qkv_ref[:, cols] = jnp.where(cb < 2 * DN_HEADS, y * (rs * fac), y)
            return carry

        lax.fori_loop(0, n_blk, blk, 0)

        bav = ba_ref[...]
        beta = _sigmoid(bav)
        g = -jnp.exp(gate_ref[0:1, :]) * _softplus(bav + gate_ref[1:2, :])
        lane = _iota((tm, LANES), 1)
        g = jnp.where((lane >= DN_HEADS) & (lane < 2 * DN_HEADS), g, 0.0)
        r = _iota((tm, tm), 0)
        c = _iota((tm, tm), 1)
        tri = jnp.where((r >= c) & ((r >> CHUNK_SHIFT) == (c >> CHUNK_SHIFT)), 1.0, 0.0)
        gc = _dot_hi(tri, g)
        for h in range(DN_HEADS):
            hs_ref[h] = jnp.where(lane == 0, _lane_col(beta, lane, h),
                                  jnp.where(lane == 1, _lane_col(g, lane, DN_HEADS + h),
                                            jnp.where(lane == 2, _lane_col(gc, lane, DN_HEADS + h), 0.0)))

    return row_call("dn_pre", body, s_len, tm,
                    [(qkv_raw, "tile"), (qkv_raw, ("prev", DN_HALO)), (ba, "tile"), (w_conv, "full"), (gate, "full")],
                    [(qkv_raw.shape, F32), ((DN_HEADS, s_len, LANES), F32)])


def _chunk_masks():
    r = _iota((DN_CHUNK, DN_CHUNK), 0)
    c = _iota((DN_CHUNK, DN_CHUNK), 1)
    return r, c


def _decay_matrix(gc, r, c):
    lane = _iota((DN_CHUNK, LANES), 1)
    a = jnp.where(lane == 0, gc, jnp.where(lane == 1, 1.0, 0.0))
    b = jnp.where(lane == 0, 1.0, jnp.where(lane == 1, -gc, 0.0))
    diff = _dot_hi(a, b, (((1,), (1,)), ((), ())))
    causal = r >= c
    return jnp.where(causal, jnp.exp(jnp.where(causal, diff, 0.0)), 0.0)


def _tri_inverse(lows, r, c):
    eye = jnp.where(r == c, 1.0, 0.0)
    ts = [eye for _ in lows]
    b = 1
    while b < DN_CHUNK:
        shift = b.bit_length()
        sel = ((r >> shift) == (c >> shift)) & ((r & b) != 0) & ((c & b) == 0)
        lms = [jnp.where(sel, low, 0.0) for low in lows]
        if b == 1:
            ts = [t - lm for t, lm in zip(ts, lms)]
        else:
            t_lm = [_dot_hi(t, lm) for t, lm in zip(ts, lms)]
            t_lm_t = [_dot_hi(x, t) for x, t in zip(t_lm, ts)]
            ts = [t - x for t, x in zip(ts, t_lm_t)]
        b *= 2
    return ts


def dn_solve(qkv, hs):
    s_len = qkv.shape[0]
    rb = min(ROW_TILE, s_len)
    n_chunk = rb // DN_CHUNK
    interleave = min(SOLVE_INTERLEAVE, n_chunk)

    def body(k_ref, v_ref, hs_ref, u_ref, w_ref, t_ref):
        r, c = _chunk_masks()

        def group(gi, carry):
            rows = [pl.ds(pl.multiple_of((gi * interleave + j) * DN_CHUNK, DN_CHUNK), DN_CHUNK)
                    for j in range(interleave)]
            k = [k_ref[rw, :] for rw in rows]
            beta = [hs_ref[rw, 0:1] for rw in rows]
            gc = [hs_ref[rw, 2:3] for rw in rows]
            kb = [a * b for a, b in zip(k, beta)]
            decay = [_decay_matrix(g, r, c) for g in gc]
            lows = [jnp.where(r > c, _dot_nt(a, b) * d, 0.0) for a, b, d in zip(kb, k, decay)]
            ts = _tri_inverse(lows, r, c)
            us = [_dot_hi(t, v_ref[rw, :] * b) for t, rw, b in zip(ts, rows, beta)]
            ws = [_dot_hi(t, a * jnp.exp(g)) for t, a, g in zip(ts, kb, gc)]
            for j, rw in enumerate(rows):
                u_ref[rw, :] = us[j]
                w_ref[rw, :] = ws[j].astype(BF16)
                t_ref[rw, :] = ts[j]
            return carry

        lax.fori_loop(0, n_chunk // interleave, group, 0)

    return pl.pallas_call(
        body, name="dn_solve", grid=(DN_HEADS, s_len // rb),
        in_specs=[pl.BlockSpec((rb, LANES), lambda h, i: (i, DN_HEADS + h)),
                  pl.BlockSpec((rb, LANES), lambda h, i: (i, 2 * DN_HEADS + h)),
                  pl.BlockSpec((None, rb, LANES), lambda h, i: (h, i, 0))],
        out_specs=[pl.BlockSpec((rb, LANES), lambda h, i: (i, h)),
                   pl.BlockSpec((rb, LANES), lambda h, i: (i, h)),
                   pl.BlockSpec((None, rb, DN_CHUNK), lambda h, i: (h, i, 0))],
        out_shape=[jax.ShapeDtypeStruct((s_len, DN_HEADS * LANES), F32),
                   jax.ShapeDtypeStruct((s_len, DN_HEADS * LANES), BF16),
                   jax.ShapeDtypeStruct((DN_HEADS, s_len, DN_CHUNK), F32)],
        compiler_params=_cparams(("parallel", "parallel")),
    )(qkv, qkv, hs)


def dn_scan_fwd(qkv, u, w, hs):
    s_len = qkv.shape[0]
    rb = min(ROW_TILE, s_len)
    n_chunk = rb // DN_CHUNK
    total_chunks = s_len // DN_CHUNK

    hps = FWD_HEADS_PER_STEP
    groups = DN_HEADS // hps

    def body(q_ref, k_ref, u_ref, w_ref, hs_ref, o_ref, st_ref, state):
        @pl.when(pl.program_id(1) == 0)
        def _():
            state[...] = jnp.zeros_like(state)

        r, c = _chunk_masks()

        def chunk(n, carry):
            rows = pl.ds(pl.multiple_of(n * DN_CHUNK, DN_CHUNK), DN_CHUNK)
            heads = range(hps)
            cols = [slice(h * LANES, (h + 1) * LANES) for h in heads]
            each = lambda f, *xs: [f(*a) for a in zip(*xs)]
            q = [q_ref[rows, cl] for cl in cols]
            k = [k_ref[rows, cl] for cl in cols]
            gc = [hs_ref[h, rows, 2:3] for h in heads]
            st = [state[h] for h in heads]
            for h in heads:
                st_ref[h, n] = st[h]
            gl = each(lambda g: jnp.min(g, axis=0, keepdims=True), gc)
            decay = each(lambda g: _decay_matrix(g, r, c), gc)
            w_st = [_dot(w_ref[rows, cols[h]], st[h]) for h in heads]
            qk = each(_dot_nt, q, k)
            q_st = each(lambda a, g, s: _dot(a * jnp.exp(g), s), q, gc, st)
            vn = [u_ref[rows, cols[h]] - w_st[h] for h in heads]
            ai_vn = each(lambda a, d, b: _dot(a * d, b), qk, decay, vn)
            kd_vn = each(lambda a, g0, g, b: _dot_tn(a * jnp.exp(g0 - g), b), k, gl, gc, vn)
            for h in heads:
                o_ref[rows, cols[h]] = q_st[h] + ai_vn[h]
                state[h] = st[h] * jnp.exp(gl[h]) + kd_vn[h]
            return carry

        lax.fori_loop(0, n_chunk, chunk, 0)

    wide = hps * LANES
    blk = lambda off: pl.BlockSpec((rb, wide), lambda h, i: (i, off + h))
    return pl.pallas_call(
        body, name="dn_scan_fwd", grid=(groups, s_len // rb),
        in_specs=[blk(0), blk(groups), blk(0), blk(0),
                  pl.BlockSpec((hps, rb, LANES), lambda h, i: (h, i, 0))],
        out_specs=[blk(0),
                   pl.BlockSpec((hps, n_chunk, LANES, LANES), lambda h, i: (h, i, 0, 0))],
        out_shape=[jax.ShapeDtypeStruct((s_len, DN_HEADS * LANES), F32),
                   jax.ShapeDtypeStruct((DN_HEADS, total_chunks, LANES, LANES), F32)],
        scratch_shapes=[pltpu.VMEM((hps, LANES, LANES), F32)],
        compiler_params=_cparams(("parallel", "arbitrary")),
    )(qkv, qkv, u, w, hs)


def dn_scan_bwd(qkv, u, w, t_inv, hs, states, d_o):
    s_len = qkv.shape[0]
    rb = min(ROW_TILE, s_len)
    n_chunk = rb // DN_CHUNK
    n_blk = s_len // rb
    hps = BWD_HEADS_PER_STEP
    groups = DN_HEADS // hps

    def body(q_ref, k_ref, v_ref, u_ref, w_ref, t_ref, hs_ref, st_ref, do_ref,
             dq_ref, dk_ref, dv_ref, dhs_ref, dstate):
        @pl.when(pl.program_id(1) == 0)
        def _():
            dstate[...] = jnp.zeros_like(dstate)

        r, c = _chunk_masks()
        causal = r >= c
        strict = r > c
        lane = _iota((DN_CHUNK, LANES), 1)
        upper = jnp.where(r <= c, 1.0, 0.0)
        last_row = _iota((DN_CHUNK, 1), 0) == DN_CHUNK - 1

        def chunk(m, carry):
            n = n_chunk - 1 - m
            rows = pl.ds(pl.multiple_of(n * DN_CHUNK, DN_CHUNK), DN_CHUNK)
            heads = range(hps)
            cols = [slice(h * LANES, (h + 1) * LANES) for h in heads]
            each = lambda f, *xs: [f(*a) for a in zip(*xs)]
            rsum = lambda x: jnp.sum(x, axis=-1, keepdims=True)
            dims_tn = (((0,), (0,)), ((), ()))
            ones = jnp.ones((DN_CHUNK, LANES), F32)
            q = [q_ref[rows, cl] for cl in cols]
            k = [k_ref[rows, cl] for cl in cols]
            v = [v_ref[rows, cl] for cl in cols]
            uu = [u_ref[rows, cl] for cl in cols]
            ww = [w_ref[rows, cl] for cl in cols]
            do = [do_ref[rows, cl] for cl in cols]
            tt = [t_ref[h, rows, :] for h in heads]
            beta = [hs_ref[h, rows, 0:1] for h in heads]
            gc = [hs_ref[h, rows, 2:3] for h in heads]
            st = [st_ref[h, n] for h in heads]
            dst = [dstate[h] for h in heads]
            gl = each(lambda g: jnp.min(g, axis=0, keepdims=True), gc)
            egc = each(jnp.exp, gc)
            egl = each(jnp.exp, gl)
            ekd = each(lambda a, b: jnp.exp(a - b), gl, gc)
            decay = each(lambda g: _decay_matrix(g, r, c), gc)
            qd = each(jnp.multiply, q, egc)
            kd = each(jnp.multiply, k, ekd)
            kb = each(jnp.multiply, k, beta)
            w_st = each(_dot, ww, st)
            qk = each(_dot_nt, q, k)
            dqd = each(_dot_nt, do, st)
            kd_dst = each(_dot, kd, dst)
            qd_do = each(_dot_tn, qd, do)
            kbk = each(_dot_nt, kb, k)
            vn = each(jnp.subtract, uu, w_st)
            ai = each(jnp.multiply, qk, decay)
            low = each(lambda a, d: jnp.where(strict, a * d, 0.0), kbk, decay)
            dai = each(lambda a, b: jnp.where(causal, _dot_nt(a, b), 0.0), do, vn)
            ai_do = each(_dot_tn, ai, do)
            dkd = each(_dot_nt, vn, dst)
            dvn = each(jnp.add, ai_do, kd_dst)
            dp = each(jnp.multiply, dai, decay)
            dw = each(lambda a, b: -_dot_nt(a, b), dvn, st)
            w_dvn = each(_dot_tn, ww, dvn)
            dp_k = each(_dot, dp, k)
            dp_q = each(_dot_tn, dp, q)
            drhs_u = each(lambda a, b: _dot_hi(a, b, dims_tn), tt, dvn)
            dgl = each(lambda a, b, e: jnp.sum(a * b) * e, dst, st, egl)
            for h in heads:
                dstate[h] = dst[h] * egl[h] + qd_do[h] - w_dvn[h]
            dq = each(lambda a, e, b: a * e + b, dqd, egc, dp_k)
            dk_a = each(lambda a, e, b: a * e + b, dkd, ekd, dp_q)
            rkd = each(lambda a, b: rsum(a * b), dkd, kd)
            drhs_w = each(lambda a, b: _dot_hi(a, b, dims_tn), tt, dw)
            dl_u = each(_dot_nt, drhs_u, uu)
            dl_w = each(_dot_nt, drhs_w, ww)
            dlow = each(lambda a, b: jnp.where(strict, -(a + b), 0.0), dl_u, dl_w)
            dqm = each(jnp.multiply, dlow, decay)
            m_tot = each(lambda a, b, d, e: a * b + d * e, dai, ai, dlow, low)
            dqm_k = each(_dot, dqm, k)
            dk_l = each(_dot_tn, dqm, kb)
            col_sums = each(lambda m: _dot_hi(m, ones, dims_tn), m_tot)
            dkb_w = each(jnp.multiply, drhs_w, egc)
            dkb = each(jnp.add, dkb_w, dqm_k)
            dgc = [rsum(dqd[h] * qd[h]) - rkd[h] + jnp.where(last_row, jnp.sum(rkd[h]) + dgl[h], 0.0)
                   + rsum(m_tot[h]) + rsum(dkb_w[h] * kb[h]) for h in heads]
            dg = each(lambda a, b: _dot_hi(upper, jnp.where(lane == 1, a - b, 0.0)), dgc, col_sums)
            for h in heads:
                dq_ref[rows, cols[h]] = dq[h]
                dk_ref[rows, cols[h]] = dk_a[h] + dk_l[h] + dkb[h] * beta[h]
                dv_ref[rows, cols[h]] = drhs_u[h] * beta[h]
                dbeta = rsum(drhs_u[h] * v[h]) + rsum(dkb[h] * k[h])
                dhs_ref[h, rows, :] = jnp.where(lane == 0, dbeta, dg[h])
            return carry

        lax.fori_loop(0, n_chunk, chunk, 0)

    wide = hps * LANES
    blk = lambda off: pl.BlockSpec((rb, wide), lambda h, i: (n_blk - 1 - i, off + h))
    head = blk(0)
    hs_spec = pl.BlockSpec((hps, rb, LANES), lambda h, i: (h, n_blk - 1 - i, 0))
    full = jax.ShapeDtypeStruct((s_len, DN_HEADS * LANES), F32)
    return pl.pallas_call(
        body, name="dn_scan_bwd", grid=(groups, n_blk),
        in_specs=[blk(0), blk(groups), blk(2 * groups), head, head,
                  pl.BlockSpec((hps, rb, DN_CHUNK), lambda h, i: (h, n_blk - 1 - i, 0)), hs_spec,
                  pl.BlockSpec((hps, n_chunk, LANES, LANES), lambda h, i: (h, n_blk - 1 - i, 0, 0)), head],
        out_specs=[head, head, head, hs_spec],
        out_shape=[full, full, full, jax.ShapeDtypeStruct((DN_HEADS, s_len, LANES), F32)],
        scratch_shapes=[pltpu.VMEM((hps, LANES, LANES), F32)],
        compiler_params=_cparams(("parallel", "arbitrary")),
    )(qkv, qkv, qkv, u, w, t_inv, hs, states, d_o)


def dn_post(o, z, out_norm):
    def body(i, ins, outs, accs):
        gn = ins[2][...]
        for h in range(DN_HEADS):
            cols = slice(h * LANES, (h + 1) * LANES)
            xhat, _ = _rms_stats(ins[0][:, cols])
            outs[0][:, cols] = (xhat * gn * _silu(ins[1][:, cols])).astype(BF16)

    return row_call("dn_post", body, o.shape[0], ROW_TILE, [(o, "tile"), (z, "tile"), (out_norm, "full")],
                    [(o.shape, BF16)])[0]


def dn_post_bwd(d_og, o, z, out_norm):
    def body(i, ins, outs, accs):
        gn = ins[3][...]
        dgn = jnp.zeros((1, LANES), F32)
        for h in range(DN_HEADS):
            cols = slice(h * LANES, (h + 1) * LANES)
            dy, zh = ins[0][:, cols].astype(F32), ins[2][:, cols]
            xhat, r = _rms_stats(ins[1][:, cols])
            sz = _silu(zh)
            dgn = dgn + jnp.sum(dy * xhat * sz, axis=0, keepdims=True)
            outs[1][:, cols] = (dy * xhat * gn * _silu_grad(zh)).astype(BF16)
            dxhat = dy * gn * sz
            outs[0][:, cols] = r * (dxhat - xhat * jnp.mean(dxhat * xhat, axis=-1, keepdims=True))
        accs[0][...] += dgn

    return row_call("dn_post_bwd", body, o.shape[0], ROW_TILE,
                    [(d_og, "tile"), (o, "tile"), (z, "tile"), (out_norm, "full")],
                    [(o.shape, F32), (o.shape, BF16)], [(1, LANES)])


def dn_pre_bwd(dq, dk, dv, dhs, qkv_raw, ba, w_conv, gate):
    s_len = qkv_raw.shape[0]
    tm = min(DN_ROW_TILE, s_len)

    def body(i, ins, outs, accs):
        dq_ref, dk_ref, dv_ref, dhs_ref, x_ref, xp_ref, ba_ref, w_ref, gate_ref = ins
        dc_ref, dba_ref = outs

        def blk(cb, carry):
            cols = pl.ds(pl.multiple_of(cb * LANES, LANES), LANES)
            hcols = pl.ds(pl.multiple_of((cb & (DN_HEADS - 1)) * LANES, LANES), LANES)
            prev = jnp.where(i > 0, xp_ref[:, cols], 0.0)
            xcat = jnp.concatenate([prev, x_ref[:, cols]], axis=0)
            c = _conv_taps(xcat, w_ref, cols, DN_CONV, DN_HALO, tm)
            y = _silu(c)
            dy = jnp.where(cb < DN_HEADS, dq_ref[:, hcols],
                           jnp.where(cb < 2 * DN_HEADS, dk_ref[:, hcols], dv_ref[:, hcols]))
            rs = lax.rsqrt(jnp.sum(y * y, axis=-1, keepdims=True) + L2_EPS)
            fac = jnp.where(cb < DN_HEADS, DN_HEAD_DIM ** -0.5, 1.0)
            nrm = y * rs
            dn = dy * fac
            dy_norm = rs * (dn - nrm * jnp.sum(dn * nrm, axis=-1, keepdims=True))
            dc_ref[:, cols] = jnp.where(cb < 2 * DN_HEADS, dy_norm, dy) * _silu_grad(c)
            return carry

        lax.fori_loop(0, qkv_raw.shape[1] // LANES, blk, 0)

        lane = _iota((tm, LANES), 1)
        dbeta = jnp.zeros((tm, LANES), F32)
        dg = jnp.zeros((tm, LANES), F32)
        for h in range(DN_HEADS):
            dbeta = dbeta + jnp.where(lane == h, dhs_ref[h, :, 0:1], 0.0)
            dg = dg + jnp.where(lane == DN_HEADS + h, dhs_ref[h, :, 1:2], 0.0)
        bav = ba_ref[...]
        beta = _sigmoid(bav)
        ea = jnp.exp(gate_ref[0:1, :])
        pre = bav + gate_ref[1:2, :]
        g = -ea * _softplus(pre)
        da = dg * (-ea) * _sigmoid(pre)
        dba_ref[...] = (dbeta * beta * (1.0 - beta) + da).astype(BF16)
        rid = _iota((8, LANES), 0)
        accs[0][...] += (jnp.where(rid == 0, jnp.sum(dg * g, axis=0, keepdims=True), 0.0)
                         + jnp.where(rid == 1, jnp.sum(da, axis=0, keepdims=True), 0.0))

    return row_call("dn_pre_bwd", body, s_len, tm,
                    [(dq, "tile"), (dk, "tile"), (dv, "tile"), (dhs, "tile"), (qkv_raw, "tile"),
                     (qkv_raw, ("prev", DN_HALO)), (ba, "tile"), (w_conv, "full"), (gate, "full")],
                    [(qkv_raw.shape, F32), (ba.shape, BF16)], [(8, LANES)])


def dn_conv_bwd(dc, qkv_raw, w_conv):
    s_len = dc.shape[0]
    tm = min(DN_ROW_TILE, s_len)
    nt = s_len // tm

    def body(i, ins, outs, accs):
        dc_ref, dn_ref, x_ref, xp_ref, w_ref = ins

        def blk(cb, carry):
            cols = pl.ds(pl.multiple_of(cb * LANES, LANES), LANES)
            dy = dc_ref[:, cols]
            nxt = jnp.where(i < nt - 1, dn_ref[:, cols], 0.0)
            dcat = jnp.concatenate([dy, nxt], axis=0)
            outs[0][:, cols] = _conv_taps_bwd_x(dcat, w_ref, cols, DN_CONV, DN_HALO, tm).astype(BF16)
            prev = jnp.where(i > 0, xp_ref[:, cols], 0.0)
            xcat = jnp.concatenate([prev, x_ref[:, cols]], axis=0)
            accs[0][:, cols] += _conv_taps_bwd_w(dy, xcat, DN_CONV, DN_HALO, tm, 8)
            return carry

        lax.fori_loop(0, dc.shape[1] // LANES, blk, 0)

    return row_call("dn_conv_bwd", body, s_len, tm,
                    [(dc, "tile"), (dc, ("next", DN_HALO)), (qkv_raw, "tile"), (qkv_raw, ("prev", DN_HALO)),
                     (w_conv, "full")],
                    [(dc.shape, BF16)], [(8, dc.shape[1])])


def _glu(u_ref, cols, d):
    return u_ref[:, cols] * _sigmoid(u_ref[:, pl.ds(pl.multiple_of(d + cols.start, LANES), cols.size)])


def cv_core_fwd(u, w_dw, b_dw, ln_g, ln_b):
    s_len, d = u.shape[0], u.shape[1] // 2
    tm = min(CONV_ROW_TILE, s_len)

    def body(i, ins, outs, accs):
        u_ref, up_ref, w_ref, bdw_ref, g_ref, b_ref = ins
        s_ref, c_ref = outs

        def blk(cb, carry):
            cols = pl.ds(pl.multiple_of(cb * LANES, LANES), LANES)
            prev = jnp.where(i > 0, _glu(up_ref, cols, d), 0.0)
            xcat = jnp.concatenate([prev, _glu(u_ref, cols, d)], axis=0)
            c_ref[:, cols] = _conv_taps(xcat, w_ref, cols, CV_WIDTH, CV_HALO, tm) + bdw_ref[:, cols]
            return carry

        lax.fori_loop(0, d // LANES, blk, 0)
        c = c_ref[...]
        mu = jnp.mean(c, axis=-1, keepdims=True)
        xc = c - mu
        rstd = lax.rsqrt(jnp.mean(xc * xc, axis=-1, keepdims=True) + LN_EPS)
        s_ref[...] = _silu(xc * rstd * g_ref[...] + b_ref[...]).astype(BF16)

    return row_call("cv_core_fwd", body, s_len, tm,
                    [(u, "tile"), (u, ("prev", CV_HALO)), (w_dw, "full"), (b_dw, "full"), (ln_g, "full"),
                     (ln_b, "full")],
                    [((s_len, d), BF16), ((s_len, d), F32)])


def cv_ln_bwd(ds, c, ln_g, ln_b):
    def body(i, ins, outs, accs):
        cv, g = ins[1][...], ins[2][...]
        mu = jnp.mean(cv, axis=-1, keepdims=True)
        xc = cv - mu
        rstd = lax.rsqrt(jnp.mean(xc * xc, axis=-1, keepdims=True) + LN_EPS)
        xhat = xc * rstd
        dl = ins[0][...].astype(F32) * _silu_grad(xhat * g + ins[3][...])
        dxhat = dl * g
        dc = rstd * (dxhat - jnp.mean(dxhat, axis=-1, keepdims=True)
                     - xhat * jnp.mean(dxhat * xhat, axis=-1, keepdims=True))
        outs[0][...] = dc
        rid = _iota((8, cv.shape[1]), 0)
        accs[0][...] += (jnp.where(rid == 0, jnp.sum(dl * xhat, axis=0, keepdims=True), 0.0)
                         + jnp.where(rid == 1, jnp.sum(dl, axis=0, keepdims=True), 0.0)
                         + jnp.where(rid == 2, jnp.sum(dc, axis=0, keepdims=True), 0.0))

    return row_call("cv_ln_bwd", body, c.shape[0], ROW_TILE,
                    [(ds, "tile"), (c, "tile"), (ln_g, "full"), (ln_b, "full")], [(c.shape, F32)], [(8, c.shape[1])])


def cv_conv_bwd(dc, u, w_dw):
    s_len, d = dc.shape
    tm = min(CONV_ROW_TILE, s_len)
    nt = s_len // tm

    def body(i, ins, outs, accs):
        dc_ref, dn_ref, u_ref, up_ref, w_ref = ins

        def blk(cb, carry):
            cols = pl.ds(pl.multiple_of(cb * LANES, LANES), LANES)
            gcols = pl.ds(pl.multiple_of(d + cb * LANES, LANES), LANES)
            dy = dc_ref[:, cols]
            nxt = jnp.where(i < nt - 1, dn_ref[:, cols], 0.0)
            dgl = _conv_taps_bwd_x(jnp.concatenate([dy, nxt], axis=0), w_ref, cols, CV_WIDTH, CV_HALO, tm)
            u1, sg = u_ref[:, cols], _sigmoid(u_ref[:, gcols])
            du1 = dgl * sg
            du2 = dgl * u1 * sg * (1.0 - sg)
            outs[0][:, cols] = du1.astype(BF16)
            outs[0][:, gcols] = du2.astype(BF16)
            accs[1][:, cols] += jnp.sum(du1, axis=0, keepdims=True)
            accs[1][:, gcols] += jnp.sum(du2, axis=0, keepdims=True)
            prev = jnp.where(i > 0, _glu(up_ref, cols, d), 0.0)
            xcat = jnp.concatenate([prev, u1 * sg], axis=0)
            accs[0][:, cols] += _conv_taps_bwd_w(dy, xcat, CV_WIDTH, CV_HALO, tm, CV_HALO)
            return carry

        lax.fori_loop(0, d // LANES, blk, 0)

    return row_call("cv_conv_bwd", body, s_len, tm,
                    [(dc, "tile"), (dc, ("next", CV_HALO)), (u, "tile"), (u, ("prev", CV_HALO)), (w_dw, "full")],
                    [(u.shape, BF16)], [(CV_HALO, d), (1, 2 * d)])


def xa_core_fwd(name, q, kv):
    d = q.shape[1]

    def body(i, ins, outs, accs):
        for h in range(XA_HEADS):
            cols = slice(h * XA_HEAD_DIM, (h + 1) * XA_HEAD_DIM)
            vcols = slice(d + h * XA_HEAD_DIM, d + (h + 1) * XA_HEAD_DIM)
            s = _dot_nt(ins[0][:, cols], ins[1][:, cols]) * (XA_HEAD_DIM ** -0.5)
            e = jnp.exp(s - jnp.max(s, axis=-1, keepdims=True))
            p = e / jnp.sum(e, axis=-1, keepdims=True)
            outs[0][:, cols] = _dot(p, ins[1][:, vcols]).astype(BF16)

    return row_call(name, body, q.shape[0], ROW_TILE, [(q, "tile"), (kv, "full")], [(q.shape, BF16)])[0]


def xa_core_bwd(name, d_o, q, kv):
    d = q.shape[1]

    def body(i, ins, outs, accs):
        for h in range(XA_HEADS):
            cols = slice(h * XA_HEAD_DIM, (h + 1) * XA_HEAD_DIM)
            vcols = slice(d + h * XA_HEAD_DIM, d + (h + 1) * XA_HEAD_DIM)
            qh, kh, vh, doh = ins[1][:, cols], ins[2][:, cols], ins[2][:, vcols], ins[0][:, cols]
            s = _dot_nt(qh, kh) * (XA_HEAD_DIM ** -0.5)
            e = jnp.exp(s - jnp.max(s, axis=-1, keepdims=True))
            p = e / jnp.sum(e, axis=-1, keepdims=True)
            dp = _dot_nt(doh, vh)
            ds = p * (dp - jnp.sum(dp * p, axis=-1, keepdims=True)) * (XA_HEAD_DIM ** -0.5)
            outs[0][:, cols] = _dot(ds, kh).astype(BF16)
            accs[0][:, cols] += _dot_tn(ds, qh)
            accs[0][:, vcols] += _dot_tn(p, doh)

    return row_call(name, body, q.shape[0], ROW_TILE, [(d_o, "tile"), (q, "tile"), (kv, "full")],
                    [(q.shape, BF16)], [kv.shape])


def adamw(name, w, g, m, v):
    def body(i, ins, outs, accs):
        wv, gv = ins[0][...], ins[1][...]
        mn = ADAM_B1 * ins[2][...] + (1.0 - ADAM_B1) * gv
        vn = ADAM_B2 * ins[3][...] + (1.0 - ADAM_B2) * jnp.square(gv)
        m_hat = mn / (1.0 - ADAM_B1 ** ADAM_STEP)
        v_hat = vn / (1.0 - ADAM_B2 ** ADAM_STEP)
        outs[0][...] = -ADAM_LR * (m_hat / (jnp.sqrt(v_hat) + ADAM_EPS) + ADAM_WD * wv)
        outs[1][...] = mn
        outs[2][...] = vn

    return row_call(name, body, w.shape[0], ROW_TILE, [(w, "tile"), (g, "tile"), (m, "tile"), (v, "tile")],
                    [(w.shape, F32)] * 3)


def adamw_halves(name, w, g_mine, g_sibling, m, v, core):
    rows, cols = w.shape
    tm = _pick_tile(rows // 2, ADAMW_ROW_TILE)
    per_half = rows // 2 // tm

    def body(core_ref, w_ref, gm_ref, gs_ref, m_ref, v_ref, g_out, d_out, m_out, v_out):
        mine = (pl.program_id(0) // per_half) == core_ref[0]
        gv = jnp.where(mine, gm_ref[...], gs_ref[...])
        mn = ADAM_B1 * m_ref[...] + (1.0 - ADAM_B1) * gv
        vn = ADAM_B2 * v_ref[...] + (1.0 - ADAM_B2) * jnp.square(gv)
        m_hat = mn / (1.0 - ADAM_B1 ** ADAM_STEP)
        v_hat = vn / (1.0 - ADAM_B2 ** ADAM_STEP)
        g_out[...] = gv
        d_out[...] = -ADAM_LR * (m_hat / (jnp.sqrt(v_hat) + ADAM_EPS) + ADAM_WD * w_ref[...])
        m_out[...] = mn
        v_out[...] = vn

    whole = pl.BlockSpec((tm, cols), lambda i, core_ref: (i, 0))
    half = pl.BlockSpec((tm, cols), lambda i, core_ref: (i % per_half, 0))
    return pl.pallas_call(
        body, name=name,
        grid_spec=pltpu.PrefetchScalarGridSpec(
            num_scalar_prefetch=1, grid=(2 * per_half,),
            in_specs=[whole, half, half, whole, whole], out_specs=[whole] * 4),
        out_shape=[jax.ShapeDtypeStruct(w.shape, F32)] * 4,
        compiler_params=_cparams(("parallel",)),
    )(core, w, g_mine, g_sibling, m, v)


HBM_SPEC = pl.BlockSpec(memory_space=pltpu.HBM)


def _position():
    return lax.axis_index("x"), lax.axis_index("y"), lax.axis_index("c")


def _other_chips(x, y):
    return [(1 - x, y), (x, 1 - y), (1 - x, 1 - y)]


def _row_chunks(rows):
    return rows // D2D_CHUNK_ROWS if rows % D2D_CHUNK_ROWS == 0 else 1


def _start_chunked(make, rows):
    k = _row_chunks(rows)
    for i in range(k):
        make(i * (rows // k), rows // k).start()


def gather_shards(packs):
    n = len(packs)

    def body(*refs):
        srcs, outs = refs[:n], refs[n:2 * n]
        send_sems, recv_sems = refs[2 * n:]
        x, y, c = _position()
        me = 2 * x + y
        chips = _other_chips(x, y)
        sibling = (x, y, 1 - c)

        def over_ici(a, j):
            px, py = chips[j]
            rows = srcs[a].shape[0] // 2
            return pltpu.make_async_remote_copy(
                src_ref=srcs[a].at[pl.ds(c * rows, rows), :], dst_ref=outs[a].at[me, pl.ds(c * rows, rows), :],
                send_sem=send_sems.at[a, j], recv_sem=recv_sems.at[a, j], device_id=(px, py, c), device_id_type=MESH)

        def landed(a, j):
            px, py = chips[j]
            rows = srcs[a].shape[0] // 2
            part = outs[a].at[2 * px + py, pl.ds(c * rows, rows), :]
            return pltpu.make_async_remote_copy(
                src_ref=part, dst_ref=part, send_sem=send_sems.at[a, j], recv_sem=recv_sems.at[a, j],
                device_id=(px, py, c), device_id_type=MESH)

        def over_d2d(a, j, cc, off, size):
            px, py = chips[j]
            rows = srcs[a].shape[0] // 2
            part = outs[a].at[2 * px + py, pl.ds(cc * rows + off, size), :]
            return pltpu.make_async_remote_copy(
                src_ref=part, dst_ref=part, send_sem=send_sems.at[a, 3 + j], recv_sem=recv_sems.at[a, 3 + j],
                device_id=sibling, device_id_type=MESH)

        for a in range(n):
            for j in range(3):
                over_ici(a, j).start()
        for a in range(n):
            for j in range(3):
                landed(a, j).wait_recv()
                _start_chunked(functools.partial(over_d2d, a, j, c), srcs[a].shape[0] // 2)
        for a in range(n):
            rows = srcs[a].shape[0] // 2
            for j in range(3):
                over_d2d(a, j, 1 - c, 0, rows).wait_recv()
                over_d2d(a, j, c, 0, rows).wait_send()
                over_ici(a, j).wait_send()

    return pl.pallas_call(
        body, name="gather_shards",
        in_specs=[HBM_SPEC] * n, out_specs=[HBM_SPEC] * n,
        out_shape=[jax.ShapeDtypeStruct((N_CHIPS,) + p.shape, p.dtype) for p in packs],
        scratch_shapes=[pltpu.SemaphoreType.DMA((n, 6)), pltpu.SemaphoreType.DMA((n, 6))],
    )(*packs)


def pair_split(packs):
    n = len(packs)

    def body(*refs):
        srcs, outs = refs[:n], refs[n:2 * n]
        send_sems, recv_sems = refs[2 * n:]
        x, y, c = _position()

        def remote(a, off, size):
            rows = srcs[a].shape[1] // 2
            return pltpu.make_async_remote_copy(
                src_ref=srcs[a].at[:, pl.ds((1 - c) * rows + off, size), :],
                dst_ref=outs[a].at[:, pl.ds(off, size), :],
                send_sem=send_sems.at[a], recv_sem=recv_sems.at[a], device_id=(x, y, 1 - c), device_id_type=MESH)

        for a in range(n):
            _start_chunked(functools.partial(remote, a), srcs[a].shape[1] // 2)
        for a in range(n):
            remote(a, 0, srcs[a].shape[1] // 2).wait()

    return pl.pallas_call(
        body, name="pair_split", in_specs=[HBM_SPEC] * n, out_specs=[HBM_SPEC] * n,
        out_shape=[jax.ShapeDtypeStruct((p.shape[0], p.shape[1] // 2, p.shape[2]), p.dtype) for p in packs],
        scratch_shapes=[pltpu.SemaphoreType.DMA((n,)), pltpu.SemaphoreType.DMA((n,))],
    )(*packs)


def chip_scatter(packs):
    n = len(packs)

    def body(*refs):
        srcs, outs = refs[:n], refs[n:2 * n]
        send_sems, recv_sems = refs[2 * n:]
        x, y, c = _position()
        copies = []
        for a in range(n):
            for j, (px, py) in enumerate(_other_chips(x, y)):
                cp = pltpu.make_async_remote_copy(
                    src_ref=srcs[a].at[2 * px + py], dst_ref=outs[a].at[j],
                    send_sem=send_sems.at[a, j], recv_sem=recv_sems.at[a, j],
                    device_id=(px, py, c), device_id_type=MESH)
                cp.start()
                copies.append(cp)
        for cp in copies:
            cp.wait()

    return pl.pallas_call(
        body, name="chip_scatter", in_specs=[HBM_SPEC] * n, out_specs=[HBM_SPEC] * n,
        out_shape=[jax.ShapeDtypeStruct((N_CHIPS - 1,) + p.shape[1:], p.dtype) for p in packs],
        scratch_shapes=[pltpu.SemaphoreType.DMA((n, 3)), pltpu.SemaphoreType.DMA((n, 3))],
    )(*packs)


def pair_join(halves):
    n = len(halves)

    def body(*refs):
        srcs, outs = refs[:n], refs[n:2 * n]
        send_sems, recv_sems = refs[2 * n:]
        x, y, c = _position()

        def remote(a, off, size):
            return pltpu.make_async_remote_copy(
                src_ref=srcs[a].at[pl.ds(off, size), :], dst_ref=outs[a].at[pl.ds(off, size), :],
                send_sem=send_sems.at[a], recv_sem=recv_sems.at[a], device_id=(x, y, 1 - c), device_id_type=MESH)

        for a in range(n):
            _start_chunked(functools.partial(remote, a), srcs[a].shape[0])
        for a in range(n):
            remote(a, 0, srcs[a].shape[0]).wait()

    return pl.pallas_call(
        body, name="pair_join", in_specs=[HBM_SPEC] * n, out_specs=[HBM_SPEC] * n,
        out_shape=[jax.ShapeDtypeStruct(p.shape, p.dtype) for p in halves],
        scratch_shapes=[pltpu.SemaphoreType.DMA((n,)), pltpu.SemaphoreType.DMA((n,))],
    )(*halves)


def all_sum_small(part):
    n_dev = 8
    rows = part.shape[0]

    def body(src, out, buf, send_sems, recv_sems):
        x, y, c = _position()
        me = 4 * x + 2 * y + c
        buf[me] = src[...]
        copies = []
        for k in range(1, n_dev):
            px, py, pc = x ^ ((k >> 2) & 1), y ^ ((k >> 1) & 1), c ^ (k & 1)
            cp = pltpu.make_async_remote_copy(
                src_ref=src, dst_ref=buf.at[me], send_sem=send_sems.at[k - 1], recv_sem=recv_sems.at[k - 1],
                device_id=(px, py, pc), device_id_type=MESH)
            cp.start()
            copies.append(cp)
        for cp in copies:
            cp.wait()
        acc = buf[0]
        for k in range(1, n_dev):
            acc = acc + buf[k]
        out[...] = acc

    return pl.pallas_call(
        body, name="all_sum_small",
        in_specs=[pl.BlockSpec(memory_space=pltpu.VMEM)], out_specs=pl.BlockSpec(memory_space=pltpu.VMEM),
        out_shape=jax.ShapeDtypeStruct(part.shape, F32),
        scratch_shapes=[pltpu.VMEM((n_dev, rows, part.shape[1]), F32),
                        pltpu.SemaphoreType.DMA((n_dev - 1,)), pltpu.SemaphoreType.DMA((n_dev - 1,))],
    )(part)


def add_pairs(name, src, theirs, core, out_dtype):
    slabs, rows, cols = theirs.shape
    tm = _pick_tile(rows, ROW_TILE)
    nb = rows // tm

    def body(core_ref, a_ref, b_ref, o_ref):
        o_ref[...] = (a_ref[...].astype(F32) + b_ref[...].astype(F32)).astype(out_dtype)

    return pl.pallas_call(
        body, name=name,
        grid_spec=pltpu.PrefetchScalarGridSpec(
            num_scalar_prefetch=1, grid=(slabs, nb),
            in_specs=[pl.BlockSpec((None, tm, cols), lambda s, i, core_ref: (s, core_ref[0] * nb + i, 0)),
                      pl.BlockSpec((None, tm, cols), lambda s, i, core_ref: (s, i, 0))],
            out_specs=pl.BlockSpec((None, tm, cols), lambda s, i, core_ref: (s, i, 0))),
        out_shape=jax.ShapeDtypeStruct(theirs.shape, out_dtype),
        compiler_params=_cparams(("parallel", "parallel")),
    )(core, src, theirs)


def add_four(name, src, theirs, chip):
    _, rows, cols = theirs.shape
    tm = _pick_tile(rows, ROW_TILE)

    def body(chip_ref, a_ref, b_ref, o_ref):
        acc = a_ref[...].astype(F32)
        for j in range(N_CHIPS - 1):
            acc = acc + b_ref[j].astype(F32)
        o_ref[...] = acc

    return pl.pallas_call(
        body, name=name,
        grid_spec=pltpu.PrefetchScalarGridSpec(
            num_scalar_prefetch=1, grid=(rows // tm,),
            in_specs=[pl.BlockSpec((None, tm, cols), lambda i, chip_ref: (chip_ref[0], i, 0)),
                      pl.BlockSpec((N_CHIPS - 1, tm, cols), lambda i, chip_ref: (0, i, 0))],
            out_specs=pl.BlockSpec((tm, cols), lambda i, chip_ref: (i, 0))),
        out_shape=jax.ShapeDtypeStruct((rows, cols), F32),
        compiler_params=_cparams(("parallel",)),
    )(chip, src, theirs)


PACK_COLS = 1024
BIG_ROW_MULTIPLE = 512
SMALL_ROW_MULTIPLE = 32
BIG = ["dn_w_in", "dn_w_out", "cv_w_pw1", "cv_w_pw2", "xa_w_q", "xa_w_kv", "xa_w_o", "mlp_w_up", "mlp_w_down"]
SMALL = ["dn_w_conv", "cv_norm", "cv_b_pw1", "cv_w_dw", "cv_b_dw", "cv_ln_g", "cv_ln_b", "cv_b_pw2"]
SHARD_AXIS = {"dn_w_in": 2, "dn_w_conv": 2, "dn_w_out": 1, "cv_norm": 1, "cv_w_pw1": 2, "cv_b_pw1": 1,
              "cv_w_dw": 2, "cv_b_dw": 1, "cv_ln_g": 1, "cv_ln_b": 1, "cv_w_pw2": 1, "cv_b_pw2": 1,
              "xa_w_q": 1, "xa_w_kv": 2, "xa_w_o": 1, "mlp_w_up": 2, "mlp_w_down": 1}
REPLICATED = ["dn_norm", "dn_a_log", "dn_dt_bias", "dn_out_norm", "xa_norm", "xa_mem_norm", "mlp_norm", "final_norm"]


def _pack_rows(size):
    return -(-size // PACK_COLS)


SHARD_SHAPES = {
    "dn_w_in": (1, 1024, 1028), "dn_w_conv": (1, 4, 768), "dn_w_out": (1, 256, 1024), "cv_norm": (1, 256),
    "cv_w_pw1": (1, 1024, 512), "cv_b_pw1": (1, 512), "cv_w_dw": (1, 31, 256), "cv_b_dw": (1, 256),
    "cv_ln_g": (1, 256), "cv_ln_b": (1, 256), "cv_w_pw2": (1, 256, 1024), "cv_b_pw2": (1, 256),
    "xa_w_q": (2, 256, 1024), "xa_w_kv": (2, 1024, 512), "xa_w_o": (2, 256, 1024),
    "mlp_w_up": (2, 1024, 1024), "mlp_w_down": (2, 1024, 1024)}


def _shard_shape(nm):
    return SHARD_SHAPES[nm]


def _pack(tensors, names, dtype, row_multiple):
    pieces = []
    for nm in names:
        t = tensors[nm]
        flat = t.reshape(t.shape[0], -1) if t.ndim > len(_shard_shape(nm)) else t.reshape(1, -1)
        pad = _pack_rows(flat.shape[1]) * PACK_COLS - flat.shape[1]
        pieces.append(jnp.pad(flat.astype(dtype), ((0, 0), (0, pad))))
    cat = jnp.concatenate(pieces, axis=1)
    rows = cat.shape[1] // PACK_COLS
    total = -(-rows // row_multiple) * row_multiple
    cat = jnp.pad(cat, ((0, 0), (0, (total - rows) * PACK_COLS)))
    return cat.reshape(cat.shape[0], total, PACK_COLS)


def _unpack(pack, names):
    lead = pack.shape[:-2]
    flat = pack.reshape(lead + (-1,))
    out, off = {}, 0
    for nm in names:
        shp = _shard_shape(nm)
        size = 1
        for s in shp:
            size *= s
        out[nm] = flat[..., off:off + size].reshape(lead + shp)
        off += _pack_rows(size) * PACK_COLS
    return out


def _to_full(nm, stacked):
    ax = SHARD_AXIS[nm]
    moved = jnp.moveaxis(stacked, 0, ax)
    shp = list(_shard_shape(nm))
    shp[ax] *= N_CHIPS
    return moved.reshape(shp)


def _to_shards(nm, full):
    ax = SHARD_AXIS[nm]
    shp = list(_shard_shape(nm))
    split = full.reshape(shp[:ax] + [N_CHIPS, shp[ax]] + shp[ax + 1:])
    return jnp.moveaxis(split, ax, 0)


def _row(v):
    return v.reshape(1, -1)


class Stacked:
    def __init__(self, arr, split, layer):
        self.arr, self.kw = arr, dict(b_split=split, b_layer=layer)


def _grad_out(split, layer, n_layers, into):
    return dict(out_dtype=BF16, out_split=split, out_layer=(layer, n_layers), out_into=into)


def mlp_fwd(tag, h, g, w_up, w_down):
    n = rms_fwd(tag + "_norm", h, g)
    act = mm(tag + "_up", n, w_up.arr, out_dtype=BF16, epi=lambda acc: jnp.square(jnp.maximum(acc, 0.0)), **w_up.kw)
    out = mm(tag + "_down", act, w_down.arr, epi=lambda acc, res: acc + res, epi_tiles=(h,), **w_down.kw)
    return out, (n, act)


def mlp_bwd(tag, dh, h, g, w_up, w_down, saved, layer, into):
    n, act = saved
    dup = mm(tag + "_d_act", dh, w_down.arr, tb=True, out_dtype=BF16,
             epi=lambda acc, t: acc * (2.0 * jnp.sqrt(t.astype(F32))), epi_tiles=(act,), **w_down.kw)
    dw_down = mm(tag + "_dw_down", act, dh, ta=True, tk=512, **_grad_out("rows", layer, 2, into[1]))
    dn = mm(tag + "_dn", dup, w_up.arr, tb=True, **w_up.kw)
    dw_up = mm(tag + "_dw_up", n, dup, ta=True, tk=512, **_grad_out("cols", layer, 2, into[0]))
    dh_in, dg = rms_bwd(tag + "_norm_bwd", dn, h, g, dh)
    return dh_in, dg, (dw_up, dw_down)


def xa_fwd(tag, h, mem, g, g_mem, w_q, w_kv, w_o):
    n = rms_fwd(tag + "_norm", h, g)
    mem_n = rms_fwd(tag + "_mem_norm", mem, g_mem)
    q = mm(tag + "_q", n, w_q.arr, out_dtype=BF16, **w_q.kw)
    kv = mm(tag + "_kv", mem_n, w_kv.arr, out_dtype=BF16, **w_kv.kw)
    o = xa_core_fwd(tag + "_core", q, kv)
    out = mm(tag + "_o", o, w_o.arr, epi=lambda acc, res: acc + res, epi_tiles=(h,), **w_o.kw)
    return out, (n, mem_n, q, kv, o)


def xa_bwd(tag, dh, h, mem, g, g_mem, w_q, w_kv, w_o, saved, layer, into):
    n, mem_n, q, kv, o = saved
    d_o = mm(tag + "_d_o", dh, w_o.arr, tb=True, out_dtype=BF16, **w_o.kw)
    dw_o = mm(tag + "_dw_o", o, dh, ta=True, tk=512, **_grad_out("rows", layer, 2, into[2]))
    dq, dkv = xa_core_bwd(tag + "_core_bwd", d_o, q, kv)
    dn = mm(tag + "_dn", dq, w_q.arr, tb=True, **w_q.kw)
    dw_q = mm(tag + "_dw_q", n, dq, ta=True, tk=512, **_grad_out("rows", layer, 2, into[0]))
    dh_in, dg = rms_bwd(tag + "_norm_bwd", dn, h, g, dh)
    dw_kv = mm(tag + "_dw_kv", mem_n, dkv, ta=True, **_grad_out("cols", layer, 2, into[1]))
    dmem_n = mm(tag + "_dmem", dkv, w_kv.arr, tb=True, **w_kv.kw)
    dg_mem = mem_norm_bwd(tag + "_mem_norm_bwd", dmem_n, mem, g_mem)
    return dh_in, dg, dg_mem, (dw_q, dw_kv, dw_o)


def _gate_tile(a_log, dt_bias):
    t = jnp.zeros((8, LANES), F32)
    t = t.at[0, DN_HEADS:2 * DN_HEADS].set(a_log.reshape(-1))
    return t.at[1, DN_HEADS:2 * DN_HEADS].set(dt_bias.reshape(-1))


def dn_fwd(h, g, w_qkv, w_z, w_ba, w_conv, gate, out_norm, w_out):
    n = rms_fwd("dn_norm", h, g)
    qkv_raw = mm("dn_proj_qkv", n, w_qkv)
    z = mm("dn_proj_z", n, w_z)
    ba = mm("dn_proj_ba", n, w_ba)
    qkv, hs = dn_pre(qkv_raw, ba, w_conv, gate)
    u, w, t_inv = dn_solve(qkv, hs)
    o, states = dn_scan_fwd(qkv, u, w, hs)
    og = dn_post(o, z, out_norm)
    out = mm("dn_out", og, w_out.arr, epi=lambda acc, res: acc + res, epi_tiles=(h,), **w_out.kw)
    return out, (n, qkv_raw, z, ba, qkv, hs, u, w, t_inv, o, states, og)


def dn_bwd(dh, h, g, w_qkv, w_z, w_ba, w_conv, gate, out_norm, w_out, saved):
    n, qkv_raw, z, ba, qkv, hs, u, w, t_inv, o, states, og = saved
    d_og = mm("dn_d_og", dh, w_out.arr, tb=True, out_dtype=BF16, **w_out.kw)
    dw_out = mm("dn_dw_out", og, dh, ta=True, tk=512, **_grad_out("rows", 0, 1, None))
    d_o, dz, d_out_norm = dn_post_bwd(d_og, o, z, out_norm)
    dq, dk, dv, dhs = dn_scan_bwd(qkv, u, w, t_inv, hs, states, d_o)
    dc, dba, d_gate = dn_pre_bwd(dq, dk, dv, dhs, qkv_raw, ba, w_conv, gate)
    dqkv_raw, dw_conv = dn_conv_bwd(dc, qkv_raw, w_conv)
    dn = mm("dn_dn_qkv", dqkv_raw, w_qkv, tb=True)
    dn = mm("dn_dn_z", dz, w_z, tb=True, epi=lambda acc, t: acc + t, epi_tiles=(dn,))
    dn = mm("dn_dn_ba", dba, w_ba, tb=True, epi=lambda acc, t: acc + t, epi_tiles=(dn,))
    dw_qkv = mm("dn_dw_qkv", n, dqkv_raw, ta=True, tk=512)
    dw_z = mm("dn_dw_z", n, dz, ta=True, tk=512)
    dw_ba = mm("dn_dw_ba", n, dba, ta=True, tk=512)
    dh_in, dg = rms_bwd("dn_norm_bwd", dn, h, g, dh)
    return dh_in, dg, dw_qkv, dw_z, dw_ba, dw_conv, d_gate, d_out_norm, dw_out


def cv_fwd(h, g, w_pw1, b_pw1, w_dw, b_dw, ln_g, ln_b, w_pw2, b_pw2):
    n = rms_fwd("cv_norm", h, g)
    u = mm("cv_pw1", n, w_pw1.arr, epi=lambda acc, b: acc + b, epi_rows=(b_pw1,), **w_pw1.kw)
    s, c = cv_core_fwd(u, w_dw, b_dw, ln_g, ln_b)
    out = mm("cv_pw2", s, w_pw2.arr, epi=lambda acc, res, b: acc + res + b, epi_tiles=(h,), epi_rows=(b_pw2,),
             **w_pw2.kw)
    return out, (n, u, s, c)


def cv_bwd(dh, h, g, w_pw1, w_dw, ln_g, ln_b, w_pw2, saved):
    n, u, s, c = saved
    ds = mm("cv_d_s", dh, w_pw2.arr, tb=True, out_dtype=BF16, **w_pw2.kw)
    dw_pw2 = mm("cv_dw_pw2", s, dh, ta=True, tk=512, **_grad_out("rows", 0, 1, None))
    db_pw2 = col_sum("cv_db_pw2", dh)
    dc, ln_acc = cv_ln_bwd(ds, c, ln_g, ln_b)
    du, dw_dw, db_pw1 = cv_conv_bwd(dc, u, w_dw)
    dn = mm("cv_dn", du, w_pw1.arr, tb=True, **w_pw1.kw)
    dw_pw1 = mm("cv_dw_pw1", n, du, ta=True, tk=512, **_grad_out("cols", 0, 1, None))
    dh_in, dg = rms_bwd("cv_norm_bwd", dn, h, g, dh)
    return dh_in, dg, dw_pw1, db_pw1, dw_dw, ln_acc, dw_pw2, db_pw2


WEIGHTS = ["dn_norm", "dn_w_in", "dn_w_conv", "dn_a_log", "dn_dt_bias", "dn_out_norm", "dn_w_out", "cv_norm",
           "cv_w_pw1", "cv_b_pw1", "cv_w_dw", "cv_b_dw", "cv_ln_g", "cv_ln_b", "cv_w_pw2", "cv_b_pw2", "xa_norm",
           "xa_mem_norm", "xa_w_q", "xa_w_kv", "xa_w_o", "mlp_norm", "mlp_w_up", "mlp_w_down", "final_norm"]


def _as_2d(t):
    if t.ndim == 1:
        return t.reshape(1, -1)
    return t.reshape(-1, t.shape[-1])


def kernel(x, mem, dn_norm, dn_w_in, dn_w_conv, dn_a_log, dn_dt_bias, dn_out_norm, dn_w_out, cv_norm, cv_w_pw1, cv_b_pw1, cv_w_dw, cv_b_dw, cv_ln_g, cv_ln_b, cv_w_pw2, cv_b_pw2, xa_norm, xa_mem_norm, xa_w_q, xa_w_kv, xa_w_o, mlp_norm, mlp_w_up, mlp_w_down, final_norm, loss_target, m_dn_norm, m_dn_w_in, m_dn_w_conv, m_dn_a_log, m_dn_dt_bias, m_dn_out_norm, m_dn_w_out, m_cv_norm, m_cv_w_pw1, m_cv_b_pw1, m_cv_w_dw, m_cv_b_dw, m_cv_ln_g, m_cv_ln_b, m_cv_w_pw2, m_cv_b_pw2, m_xa_norm, m_xa_mem_norm, m_xa_w_q, m_xa_w_kv, m_xa_w_o, m_mlp_norm, m_mlp_w_up, m_mlp_w_down, m_final_norm, v_dn_norm, v_dn_w_in, v_dn_w_conv, v_dn_a_log, v_dn_dt_bias, v_dn_out_norm, v_dn_w_out, v_cv_norm, v_cv_w_pw1, v_cv_b_pw1, v_cv_w_dw, v_cv_b_dw, v_cv_ln_g, v_cv_ln_b, v_cv_w_pw2, v_cv_b_pw2, v_xa_norm, v_xa_mem_norm, v_xa_w_q, v_xa_w_kv, v_xa_w_o, v_mlp_norm, v_mlp_w_up, v_mlp_w_down, v_final_norm):
    args = dict(locals())
    wts = {nm: args[nm] for nm in WEIGHTS}
    mom = {nm: args["m_" + nm] for nm in WEIGHTS}
    var = {nm: args["v_" + nm] for nm in WEIGHTS}
    core = lax.axis_index("c").astype(jnp.int32).reshape(1)
    chip = (2 * lax.axis_index("x") + lax.axis_index("y")).astype(jnp.int32)
    sources = [wts[nm].astype(BF16).reshape(-1, wts[nm].shape[-1]) for nm in BIG]
    sources.append(_pack(wts, SMALL, F32, SMALL_ROW_MULTIPLE)[0])
    gathered = [lax.dynamic_update_slice(got, src[None], (chip, 0, 0))
                for got, src in zip(gather_shards(sources), sources)]
    stacked = {nm: got.reshape((N_CHIPS,) + SHARD_SHAPES[nm]) for nm, got in zip(BIG, gathered)}
    full = {nm: _to_full(nm, t) for nm, t in _unpack(gathered[-1], SMALL).items()}
    full["dn_w_in"] = _to_full("dn_w_in", stacked.pop("dn_w_in"))
    full.update({nm: wts[nm] for nm in REPLICATED})

    dh, grads, rep = local_step(x[0], mem[0], loss_target[0], stacked, full)

    grads["dn_w_in"] = _to_shards("dn_w_in", grads["dn_w_in"]).astype(BF16)
    parts = [grads[nm].reshape(N_CHIPS, -1, grads[nm].shape[-1]) for nm in BIG]
    parts.append(_pack({nm: _to_shards(nm, grads[nm]) for nm in SMALL}, SMALL, F32, SMALL_ROW_MULTIPLE))
    names = BIG + ["small"]
    wire = [BF16] * len(BIG) + [F32]
    theirs = pair_split(parts)
    pairs = [add_pairs("pair_add_" + nm, p, t, core, dt) for nm, p, t, dt in zip(names, parts, theirs, wire)]
    others = chip_scatter(pairs)
    halves = [add_four("chip_add_" + nm, p, o, chip.reshape(1)) for nm, p, o in zip(names, pairs, others)]
    siblings = pair_join(halves)
    south = core[0] == 0
    small_red = jnp.concatenate([jnp.where(south, halves[-1], siblings[-1]),
                                 jnp.where(south, siblings[-1], halves[-1])], axis=0)
    red = _unpack(small_red, SMALL)

    rep = all_sum_small(rep)
    red["dn_norm"] = rep[0:1]
    red["dn_a_log"] = rep[1:2, DN_HEADS:2 * DN_HEADS]
    red["dn_dt_bias"] = rep[2:3, DN_HEADS:2 * DN_HEADS]
    red["dn_out_norm"] = rep[3:4, :LANES]
    red["xa_norm"], red["xa_mem_norm"], red["mlp_norm"] = rep[4:6], rep[6:8], rep[8:10]
    red["final_norm"] = rep[10]
    loss = rep[11, 0]

    delta, new_m, new_v = {}, {}, {}
    for nm in WEIGHTS:
        shp = wts[nm].shape
        if nm in BIG:
            i = BIG.index(nm)
            res = adamw_halves("adamw_" + nm, _as_2d(wts[nm]), halves[i], siblings[i], _as_2d(mom[nm]),
                               _as_2d(var[nm]), core)
            red[nm] = res[0]
            res = res[1:]
        else:
            res = adamw("adamw_" + nm, _as_2d(wts[nm]), _as_2d(red[nm].reshape(shp)), _as_2d(mom[nm]),
                        _as_2d(var[nm]))
        delta[nm], new_m[nm], new_v[nm] = (r.reshape(shp) for r in res)
        red[nm] = red[nm].reshape(shp)

    grad_x = dh[None]
    return (loss, grad_x, *[red[nm] for nm in WEIGHTS], *[delta[nm] for nm in WEIGHTS],
            *[new_m[nm] for nm in WEIGHTS], *[new_v[nm] for nm in WEIGHTS])


def local_step(h0, mem0, target, stacked, full):
    d = h0.shape[1]
    dn_norm, dn_a_log, dn_dt_bias, dn_out_norm = (full[nm] for nm in REPLICATED[:4])
    xa_norm, xa_mem_norm, mlp_norm, final_norm = (full[nm] for nm in REPLICATED[4:])
    inner = DN_HEADS * DN_HEAD_DIM
    w_in = full["dn_w_in"][0]
    w_qkv, w_z = w_in[:, :3 * inner], w_in[:, 3 * inner:4 * inner]
    w_ba = jnp.pad(w_in[:, 4 * inner:], ((0, 0), (0, LANES - 2 * DN_HEADS)))
    w_conv = jnp.pad(full["dn_w_conv"][0], ((0, 8 - DN_CONV), (0, 0)))
    gate = _gate_tile(dn_a_log, dn_dt_bias)
    w_dw = jnp.pad(full["cv_w_dw"][0], ((0, CV_HALO - CV_WIDTH), (0, 0)))

    def sw(nm, layer):
        return Stacked(stacked[nm], "rows" if SHARD_AXIS[nm] == 1 else "cols", layer)

    dn_args = (_row(dn_norm), w_qkv, w_z, w_ba, w_conv, gate, _row(dn_out_norm), sw("dn_w_out", 0))
    h1, dn_saved = dn_fwd(h0, *dn_args)
    xa_args = [(_row(xa_norm[l]), _row(xa_mem_norm[l]), sw("xa_w_q", l), sw("xa_w_kv", l), sw("xa_w_o", l))
               for l in range(2)]
    mlp_args = [(_row(mlp_norm[l]), sw("mlp_w_up", l), sw("mlp_w_down", l)) for l in range(2)]
    h2, xa0_saved = xa_fwd("xa0", h1, mem0, *xa_args[0])
    h3, mlp0_saved = mlp_fwd("mlp0", h2, *mlp_args[0])
    cv_args = (_row(full["cv_norm"][0]), sw("cv_w_pw1", 0), full["cv_b_pw1"], w_dw, full["cv_b_dw"],
               full["cv_ln_g"], full["cv_ln_b"], sw("cv_w_pw2", 0), full["cv_b_pw2"])
    h4, cv_saved = cv_fwd(h3, *cv_args)
    h5, xa1_saved = xa_fwd("xa1", h4, mem0, *xa_args[1])
    h6, mlp1_saved = mlp_fwd("mlp1", h5, *mlp_args[1])

    dh, loss_tile, d_final = loss_head("loss_head", h6, _row(final_norm), target)
    grads = {}
    dg_mlp, dg_xa, dg_xa_mem = [None, None], [None, None], [None, None]
    dh, dg_mlp[1], dw_mlp = mlp_bwd("mlp1", dh, h5, *mlp_args[1], mlp1_saved, 1, (None, None))
    dh, dg_xa[1], dg_xa_mem[1], dw_xa = xa_bwd("xa1", dh, h4, mem0, *xa_args[1], xa1_saved, 1, (None, None, None))
    (dh, grads["cv_norm"], grads["cv_w_pw1"], grads["cv_b_pw1"], dw_dw, ln_acc, grads["cv_w_pw2"],
     grads["cv_b_pw2"]) = cv_bwd(dh, h3, cv_args[0], cv_args[1], w_dw, cv_args[5], cv_args[6], cv_args[7], cv_saved)
    dh, dg_mlp[0], dw_mlp = mlp_bwd("mlp0", dh, h2, *mlp_args[0], mlp0_saved, 0, dw_mlp)
    dh, dg_xa[0], dg_xa_mem[0], dw_xa = xa_bwd("xa0", dh, h1, mem0, *xa_args[0], xa0_saved, 0, dw_xa)
    (dh, dg_dn, dw_qkv, dw_z, dw_ba, dw_conv, d_gate, d_out_norm,
     grads["dn_w_out"]) = dn_bwd(dh, h0, *dn_args, dn_saved)

    grads["dn_w_in"] = jnp.concatenate([dw_qkv, dw_z, dw_ba[:, :2 * DN_HEADS]], axis=1)[None]
    grads["dn_w_conv"] = dw_conv[None, :DN_CONV]
    grads["cv_w_dw"] = dw_dw[None, :CV_WIDTH]
    grads["cv_ln_g"], grads["cv_ln_b"], grads["cv_b_dw"] = ln_acc[0:1], ln_acc[1:2], ln_acc[2:3]
    grads["xa_w_q"], grads["xa_w_kv"], grads["xa_w_o"] = dw_xa
    grads["mlp_w_up"], grads["mlp_w_down"] = dw_mlp

    rep = jnp.zeros((16, d), F32)
    rep = rep.at[0].set(dg_dn[0])
    rep = rep.at[1, :LANES].set(d_gate[0])
    rep = rep.at[2, :LANES].set(d_gate[1])
    rep = rep.at[3, :LANES].set(d_out_norm[0])
    rep = rep.at[4].set(dg_xa[0][0]).at[5].set(dg_xa[1][0])
    rep = rep.at[6].set(dg_xa_mem[0][0]).at[7].set(dg_xa_mem[1][0])
    rep = rep.at[8].set(dg_mlp[0][0]).at[9].set(dg_mlp[1][0])
    rep = rep.at[10].set(d_final[0])
    rep = rep.at[11, :LANES].set(loss_tile[0])
    return dh, grads, rep
```

```python
import functools

import jax
import jax.numpy as jnp
from jax import lax
from jax.experimental import pallas as pl
from jax.experimental.pallas import tpu as pltpu

F32 = jnp.float32
BF16 = jnp.bfloat16
HIGHEST = lax.Precision.HIGHEST
MESH = pl.DeviceIdType.MESH

D_MODEL = 1024
DN_HEADS = 8
DN_HEAD_DIM = 128
DN_CONV = 4
DN_CHUNK = 64
CV_WIDTH = 31
XA_HEADS = 4
XA_HEAD_DIM = 256
RMS_EPS = 1e-6
LN_EPS = 1e-5
L2_EPS = 1e-6

ADAM_LR = 0.001
ADAM_B1 = 0.9
ADAM_B2 = 0.999
ADAM_EPS = 1e-08
ADAM_WD = 0.01
ADAM_STEP = 10

LANES = 128
ROW_TILE = 512
CONV_ROW_TILE = 256
MM_TILE = 1024
ADAMW_ROW_TILE = 256
DN_ROW_TILE = 256
CHUNK_SHIFT = 6
SOLVE_INTERLEAVE = 8
FWD_HEADS_PER_STEP = 4
BWD_HEADS_PER_STEP = 4
DN_HALO = 8
CV_HALO = 32
VMEM_LIMIT = 48 * 1024 * 1024
N_CHIPS = 4
D2D_CHUNK_ROWS = 256


def _cparams(sem):
    return pltpu.CompilerParams(dimension_semantics=sem, vmem_limit_bytes=VMEM_LIMIT)


def _dot(a, b, dims=(((1,), (0,)), ((), ()))):
    return lax.dot_general(a.astype(BF16), b.astype(BF16), dims, preferred_element_type=F32)


def _dot_nt(a, b):
    return _dot(a, b, (((1,), (1,)), ((), ())))


def _dot_tn(a, b):
    return _dot(a, b, (((0,), (0,)), ((), ())))


def _dot_hi(a, b, dims=(((1,), (0,)), ((), ()))):
    return lax.dot_general(a.astype(F32), b.astype(F32), dims, precision=HIGHEST, preferred_element_type=F32)


def _sigmoid(x):
    return 1.0 / (1.0 + jnp.exp(-x))


def _silu(x):
    return x * _sigmoid(x)


def _silu_grad(x):
    s = _sigmoid(x)
    return s * (1.0 + x * (1.0 - s))


def _softplus(x):
    return jnp.maximum(x, 0.0) + jnp.log(1.0 + jnp.exp(-jnp.abs(x)))


def _iota(shape, dim):
    return lax.broadcasted_iota(jnp.int32, shape, dim)


def _lane_col(vals, lane, idx):
    return jnp.sum(jnp.where(lane == idx, vals, 0.0), axis=1, keepdims=True)


def _pick_tile(rows, cap):
    best = rows
    for t in range(16, min(rows, cap) + 1, 16):
        if rows % t == 0:
            best = t
    return best


def _stacked_spec(shape, split, layer, rows, cols, block_index):
    per_chip = (shape[-2] // rows) if split == "rows" else (shape[-1] // cols)
    assert (shape[-2] % rows == 0) and (shape[-1] % cols == 0)
    lead = (None,) if layer is None else (None, None)

    def index(i, j, kk):
        bi, bj = block_index(i, j, kk)
        mid = () if layer is None else (layer,)
        if split == "rows":
            return (bi // per_chip,) + mid + (bi % per_chip, bj)
        return (bj // per_chip,) + mid + (bi, bj % per_chip)

    return pl.BlockSpec(lead + (rows, cols), index)


def mm(name, a, b, *, ta=False, tb=False, out_dtype=F32, pro=None, epi=None, epi_tiles=(), epi_rows=(),
       tm=MM_TILE, tn=MM_TILE, tk=MM_TILE, b_split=None, b_layer=None, out_split=None, out_layer=None,
       after=None):
    m, k = (a.shape[1], a.shape[0]) if ta else a.shape
    b_rows, b_cols = b.shape[-2], b.shape[-1]
    if b_split == "rows":
        b_rows *= N_CHIPS
    elif b_split == "cols":
        b_cols *= N_CHIPS
    n = b_rows if tb else b_cols
    assert (b_cols if tb else b_rows) == k
    tm, tn, tk = min(tm, m), min(tn, n), min(tk, k)
    if b_split is not None:
        lim_r, lim_c = b.shape[-2], b.shape[-1]
        if tb:
            tn, tk = min(tn, lim_r), min(tk, lim_c)
        else:
            tk, tn = min(tk, lim_r), min(tn, lim_c)
    if out_split == "rows":
        tm = min(tm, m // N_CHIPS)
    elif out_split == "cols":
        tn = min(tn, n // N_CHIPS)
    assert m % tm == 0 and n % tn == 0 and k % tk == 0
    nk = k // tk
    a_spec = pl.BlockSpec((tk, tm), lambda i, j, kk: (kk, i)) if ta else pl.BlockSpec((tm, tk), lambda i, j, kk: (i, kk))
    b_block = (tn, tk) if tb else (tk, tn)
    b_index = (lambda i, j, kk: (j, kk)) if tb else (lambda i, j, kk: (kk, j))
    if b_split is None:
        b_spec = pl.BlockSpec(b_block, b_index)
    else:
        b_spec = _stacked_spec(b.shape, b_split, b_layer, b_block[0], b_block[1], b_index)
    in_specs = [a_spec, b_spec]
    in_specs += [pl.BlockSpec((tm, tn), lambda i, j, kk: (i, j)) for _ in epi_tiles]
    in_specs += [pl.BlockSpec((1, tn), lambda i, j, kk: (0, j)) for _ in epi_rows]
    n_t, n_r = len(epi_tiles), len(epi_rows)
    dims = (((0 if ta else 1,), (1 if tb else 0,)), ((), ()))
    if out_split is None:
        out_shape = jax.ShapeDtypeStruct((m, n), out_dtype)
        out_spec = pl.BlockSpec((tm, tn), lambda i, j, kk: (i, j))
    else:
        shard = (m // N_CHIPS, n) if out_split == "rows" else (m, n // N_CHIPS)
        layers = () if out_layer is None else (out_layer[1],)
        out_shape = jax.ShapeDtypeStruct((N_CHIPS,) + layers + shard, out_dtype)
        out_spec = _stacked_spec(out_shape.shape, out_split, None if out_layer is None else out_layer[0], tm, tn,
                                 lambda i, j, kk: (i, j))
    extra = []
    if after is not None:
        extra = [after]
        in_specs.append(pl.BlockSpec(memory_space=pl.ANY))

    def body(a_ref, b_ref, *rest):
        tiles = rest[:n_t]
        rows = rest[n_t:n_t + n_r]
        rest = rest[n_t + n_r + len(extra):]
        o_ref, acc_ref = rest[0], rest[1]
        kk = pl.program_id(2)

        @pl.when(kk == 0)
        def _():
            acc_ref[...] = jnp.zeros_like(acc_ref)

        av = a_ref[...]
        if pro is not None:
            av = pro(av)
        acc_ref[...] += _dot(av, b_ref[...], dims)

        @pl.when(kk == nk - 1)
        def _():
            out = acc_ref[...]
            if epi is not None:
                out = epi(out, *[t[...] for t in tiles], *[r[...] for r in rows])
            o_ref[...] = out.astype(out_dtype)

    return pl.pallas_call(
        body, name=name, grid=(m // tm, n // tn, nk),
        in_specs=in_specs, out_specs=out_spec, out_shape=out_shape,
        scratch_shapes=[pltpu.VMEM((tm, tn), F32)],
        compiler_params=_cparams(("parallel", "parallel", "arbitrary")),
    )(a, b, *epi_tiles, *epi_rows, *extra)


def row_call(name, body, n_rows, tm, ins, outs, accs=()):
    tm = _pick_tile(n_rows, tm)
    in_specs = []
    for arr, kind in ins:
        if kind == "tile":
            if arr.ndim == 2:
                in_specs.append(pl.BlockSpec((tm, arr.shape[1]), lambda i: (i, 0)))
            else:
                in_specs.append(pl.BlockSpec((arr.shape[0], tm, arr.shape[2]), lambda i: (0, i, 0)))
        elif kind == "full":
            in_specs.append(pl.BlockSpec(arr.shape, functools.partial(lambda i, nd: (0,) * nd, nd=arr.ndim)))
        else:
            where, h = kind
            per = tm // h
            if where == "prev":
                in_specs.append(pl.BlockSpec((h, arr.shape[1]), functools.partial(
                    lambda i, per: (jnp.maximum(i * per - 1, 0), 0), per=per)))
            else:
                last = n_rows // h - 1
                in_specs.append(pl.BlockSpec((h, arr.shape[1]), functools.partial(
                    lambda i, per, last: (jnp.minimum((i + 1) * per, last), 0), per=per, last=last)))
    out_shape, out_specs = [], []
    for shape, dtype in outs:
        out_shape.append(jax.ShapeDtypeStruct(shape, dtype))
        if len(shape) == 2:
            out_specs.append(pl.BlockSpec((tm, shape[1]), lambda i: (i, 0)))
        else:
            out_specs.append(pl.BlockSpec((shape[0], tm, shape[2]), lambda i: (0, i, 0)))
    for shape in accs:
        out_shape.append(jax.ShapeDtypeStruct(shape, F32))
        out_specs.append(pl.BlockSpec(shape, lambda i: (0, 0)))
    n_in, n_out, n_acc = len(ins), len(outs), len(accs)

    def kern(*refs):
        i = pl.program_id(0)
        in_refs = refs[:n_in]
        out_refs = refs[n_in:n_in + n_out]
        acc_refs = refs[n_in + n_out:n_in + n_out + n_acc]
        if n_acc:
            @pl.when(i == 0)
            def _():
                for r in acc_refs:
                    r[...] = jnp.zeros_like(r)
        body(i, in_refs, out_refs, acc_refs)

    res = pl.pallas_call(
        kern, name=name, grid=(n_rows // tm,), in_specs=in_specs, out_specs=out_specs, out_shape=out_shape,
        compiler_params=_cparams(("arbitrary",) if n_acc else ("parallel",)),
    )(*[a for a, _ in ins])
    return list(res)


def _rms_stats(h):
    r = lax.rsqrt(jnp.mean(h * h, axis=-1, keepdims=True) + RMS_EPS)
    return h * r, r


def rms_fwd(name, h, g):
    def body(i, ins, outs, accs):
        xhat, _ = _rms_stats(ins[0][...])
        outs[0][...] = (xhat * ins[1][...]).astype(BF16)

    return row_call(name, body, h.shape[0], ROW_TILE, [(h, "tile"), (g, "full")], [(h.shape, BF16)])[0]


def _rms_bwd_tile(dn, h, g):
    xhat, r = _rms_stats(h)
    dxhat = dn * g
    dh = r * (dxhat - xhat * jnp.mean(dxhat * xhat, axis=-1, keepdims=True))
    dg = jnp.sum(dn * xhat, axis=0, keepdims=True)
    return dh, dg


def rms_bwd(name, dn, h, g, dres):
    def body(i, ins, outs, accs):
        dh, dg = _rms_bwd_tile(ins[0][...].astype(F32), ins[1][...], ins[2][...])
        outs[0][...] = ins[3][...] + dh
        accs[0][...] += dg

    d = h.shape[1]
    out, dg = row_call(name, body, h.shape[0], ROW_TILE,
                       [(dn, "tile"), (h, "tile"), (g, "full"), (dres, "tile")], [(h.shape, F32)], [(1, d)])
    return out, dg


def mem_norm_bwd(name, dn, mem, g):
    def body(i, ins, outs, accs):
        _, dg = _rms_bwd_tile(ins[0][...].astype(F32), ins[1][...], ins[2][...])
        accs[0][...] += dg

    return row_call(name, body, mem.shape[0], ROW_TILE, [(dn, "tile"), (mem, "tile"), (g, "full")], [],
                    [(1, mem.shape[1])])[0]


def loss_head(name, h, g, target):
    d = h.shape[1]

    def body(i, ins, outs, accs):
        hv, gv = ins[0][...], ins[1][...]
        xhat, _ = _rms_stats(hv)
        err = xhat * gv - ins[2][...]
        dy = err * (1.0 / d)
        dh, dg = _rms_bwd_tile(dy, hv, gv)
        outs[0][...] = dh
        accs[0][...] += jnp.full((8, LANES), 0.5 / d, F32) * jnp.sum(err * err)
        accs[1][...] += dg

    dh, loss, dg = row_call(name, body, h.shape[0], ROW_TILE, [(h, "tile"), (g, "full"), (target, "tile")],
                            [(h.shape, F32)], [(8, LANES), (1, d)])
    return dh, loss, dg


def col_sum(name, x):
    def body(i, ins, outs, accs):
        accs[0][...] += jnp.sum(ins[0][...].astype(F32), axis=0, keepdims=True)

    return row_call(name, body, x.shape[0], ROW_TILE, [(x, "tile")], [], [(1, x.shape[1])])[0]


def _conv_taps(xcat, w_ref, cols, width, halo, tm):
    rows = halo + tm
    acc = None
    for j in range(width):
        s = width - 1 - j
        xs = xcat if s == 0 else pltpu.roll(xcat, s, 0)
        term = xs[halo:rows] * w_ref[j:j + 1, cols]
        acc = term if acc is None else acc + term
    return acc


def _conv_taps_bwd_x(dcat, w_ref, cols, width, halo, tm):
    rows = halo + tm
    acc = None
    for j in range(width):
        s = width - 1 - j
        ds = dcat if s == 0 else pltpu.roll(dcat, rows - s, 0)
        term = ds[0:tm] * w_ref[j:j + 1, cols]
        acc = term if acc is None else acc + term
    return acc


def _conv_taps_bwd_w(dy, xcat, width, halo, tm, wrows):
    rows = halo + tm
    rid = _iota((wrows, dy.shape[1]), 0)
    out = jnp.zeros((wrows, dy.shape[1]), F32)
    for j in range(width):
        s = width - 1 - j
        xs = xcat if s == 0 else pltpu.roll(xcat, s, 0)
        v = jnp.sum(dy * xs[halo:rows], axis=0, keepdims=True)
        out = out + jnp.where(rid == j, v, 0.0)
    return out


def dn_pre(qkv_raw, ba, w_conv, gate):
    s_len = qkv_raw.shape[0]
    tm = min(DN_ROW_TILE, s_len)
    n_blk = qkv_raw.shape[1] // LANES

    def body(i, ins, outs, accs):
        x_ref, xp_ref, ba_ref, w_ref, gate_ref = ins
        qkv_ref, hs_ref = outs

        def blk(cb, carry):
            cols = pl.ds(pl.multiple_of(cb * LANES, LANES), LANES)
            prev = jnp.where(i > 0, xp_ref[:, cols], 0.0)
            xcat = jnp.concatenate([prev, x_ref[:, cols]], axis=0)
            c = _conv_taps(xcat, w_ref, cols, DN_CONV, DN_HALO, tm)
            y = _silu(c)
            rs = lax.rsqrt(jnp.sum(y * y, axis=-1, keepdims=True) + L2_EPS)
            fac = jnp.where(cb < DN_HEADS, DN_HEAD_DIM ** -0.5, 1.0)
            qkv_ref[:, cols] = jnp.where(cb < 2 * DN_HEADS, y * (rs * fac), y)
            return carry

        lax.fori_loop(0, n_blk, blk, 0)

        bav = ba_ref[...]
        beta = _sigmoid(bav)
        g = -jnp.exp(gate_ref[0:1, :]) * _softplus(bav + gate_ref[1:2, :])
        lane = _iota((tm, LANES), 1)
        g = jnp.where((lane >= DN_HEADS) & (lane < 2 * DN_HEADS), g, 0.0)
        r = _iota((tm, tm), 0)
        c = _iota((tm, tm), 1)
        tri = jnp.where((r >= c) & ((r >> CHUNK_SHIFT) == (c >> CHUNK_SHIFT)), 1.0, 0.0)
        gc = _dot_hi(tri, g)
        for h in range(DN_HEADS):
            hs_ref[h] = jnp.where(lane == 0, _lane_col(beta, lane, h),
                                  jnp.where(lane == 1, _lane_col(g, lane, DN_HEADS + h),
                                            jnp.where(lane == 2, _lane_col(gc, lane, DN_HEADS + h), 0.0)))

    return row_call("dn_pre", body, s_len, tm,
                    [(qkv_raw, "tile"), (qkv_raw, ("prev", DN_HALO)), (ba, "tile"), (w_conv, "full"), (gate, "full")],
                    [(qkv_raw.shape, F32), ((DN_HEADS, s_len, LANES), F32)])


def _chunk_masks():
    r = _iota((DN_CHUNK, DN_CHUNK), 0)
    c = _iota((DN_CHUNK, DN_CHUNK), 1)
    return r, c


def _decay_matrix(gc, r, c):
    lane = _iota((DN_CHUNK, LANES), 1)
    a = jnp.where(lane == 0, gc, jnp.where(lane == 1, 1.0, 0.0))
    b = jnp.where(lane == 0, 1.0, jnp.where(lane == 1, -gc, 0.0))
    diff = _dot_hi(a, b, (((1,), (1,)), ((), ())))
    causal = r >= c
    return jnp.where(causal, jnp.exp(jnp.where(causal, diff, 0.0)), 0.0)


def _tri_inverse(lows, r, c):
    eye = jnp.where(r == c, 1.0, 0.0)
    ts = [eye for _ in lows]
    b = 1
    while b < DN_CHUNK:
        shift = b.bit_length()
        sel = ((r >> shift) == (c >> shift)) & ((r & b) != 0) & ((c & b) == 0)
        lms = [jnp.where(sel, low, 0.0) for low in lows]
        if b == 1:
            ts = [t - lm for t, lm in zip(ts, lms)]
        else:
            t_lm = [_dot_hi(t, lm) for t, lm in zip(ts, lms)]
            t_lm_t = [_dot_hi(x, t) for x, t in zip(t_lm, ts)]
            ts = [t - x for t, x in zip(ts, t_lm_t)]
        b *= 2
    return ts


def dn_solve(qkv, hs):
    s_len = qkv.shape[0]
    rb = min(ROW_TILE, s_len)
    n_chunk = rb // DN_CHUNK
    interleave = min(SOLVE_INTERLEAVE, n_chunk)

    def body(k_ref, v_ref, hs_ref, u_ref, w_ref, t_ref):
        r, c = _chunk_masks()

        def group(gi, carry):
            rows = [pl.ds(pl.multiple_of((gi * interleave + j) * DN_CHUNK, DN_CHUNK), DN_CHUNK)
                    for j in range(interleave)]
            k = [k_ref[rw, :] for rw in rows]
            beta = [hs_ref[rw, 0:1] for rw in rows]
            gc = [hs_ref[rw, 2:3] for rw in rows]
            kb = [a * b for a, b in zip(k, beta)]
            decay = [_decay_matrix(g, r, c) for g in gc]
            lows = [jnp.where(r > c, _dot_nt(a, b) * d, 0.0) for a, b, d in zip(kb, k, decay)]
            ts = _tri_inverse(lows, r, c)
            us = [_dot_hi(t, v_ref[rw, :] * b) for t, rw, b in zip(ts, rows, beta)]
            ws = [_dot_hi(t, a * jnp.exp(g)) for t, a, g in zip(ts, kb, gc)]
            for j, rw in enumerate(rows):
                u_ref[rw, :] = us[j]
                w_ref[rw, :] = ws[j].astype(BF16)
                t_ref[rw, :] = ts[j]
            return carry

        lax.fori_loop(0, n_chunk // interleave, group, 0)

    return pl.pallas_call(
        body, name="dn_solve", grid=(DN_HEADS, s_len // rb),
        in_specs=[pl.BlockSpec((rb, LANES), lambda h, i: (i, DN_HEADS + h)),
                  pl.BlockSpec((rb, LANES), lambda h, i: (i, 2 * DN_HEADS + h)),
                  pl.BlockSpec((None, rb, LANES), lambda h, i: (h, i, 0))],
        out_specs=[pl.BlockSpec((rb, LANES), lambda h, i: (i, h)),
                   pl.BlockSpec((rb, LANES), lambda h, i: (i, h)),
                   pl.BlockSpec((None, rb, DN_CHUNK), lambda h, i: (h, i, 0))],
        out_shape=[jax.ShapeDtypeStruct((s_len, DN_HEADS * LANES), F32),
                   jax.ShapeDtypeStruct((s_len, DN_HEADS * LANES), BF16),
                   jax.ShapeDtypeStruct((DN_HEADS, s_len, DN_CHUNK), F32)],
        compiler_params=_cparams(("parallel", "parallel")),
    )(qkv, qkv, hs)


def dn_scan_fwd(qkv, u, w, hs):
    s_len = qkv.shape[0]
    rb = min(ROW_TILE, s_len)
    n_chunk = rb // DN_CHUNK
    total_chunks = s_len // DN_CHUNK

    hps = FWD_HEADS_PER_STEP
    groups = DN_HEADS // hps

    def body(q_ref, k_ref, u_ref, w_ref, hs_ref, o_ref, st_ref, state):
        @pl.when(pl.program_id(1) == 0)
        def _():
            state[...] = jnp.zeros_like(state)

        r, c = _chunk_masks()

        def chunk(n, carry):
            rows = pl.ds(pl.multiple_of(n * DN_CHUNK, DN_CHUNK), DN_CHUNK)
            heads = range(hps)
            cols = [slice(h * LANES, (h + 1) * LANES) for h in heads]
            each = lambda f, *xs: [f(*a) for a in zip(*xs)]
            q = [q_ref[rows, cl] for cl in cols]
            k = [k_ref[rows, cl] for cl in cols]
            gc = [hs_ref[h, rows, 2:3] for h in heads]
            st = [state[h] for h in heads]
            for h in heads:
                st_ref[h, n] = st[h]
            gl = each(lambda g: jnp.min(g, axis=0, keepdims=True), gc)
            decay = each(lambda g: _decay_matrix(g, r, c), gc)
            w_st = [_dot(w_ref[rows, cols[h]], st[h]) for h in heads]
            qk = each(_dot_nt, q, k)
            q_st = each(lambda a, g, s: _dot(a * jnp.exp(g), s), q, gc, st)
            vn = [u_ref[rows, cols[h]] - w_st[h] for h in heads]
            ai_vn = each(lambda a, d, b: _dot(a * d, b), qk, decay, vn)
            kd_vn = each(lambda a, g0, g, b: _dot_tn(a * jnp.exp(g0 - g), b), k, gl, gc, vn)
            for h in heads:
                o_ref[rows, cols[h]] = q_st[h] + ai_vn[h]
                state[h] = st[h] * jnp.exp(gl[h]) + kd_vn[h]
            return carry

        lax.fori_loop(0, n_chunk, chunk, 0)

    wide = hps * LANES
    blk = lambda off: pl.BlockSpec((rb, wide), lambda h, i: (i, off + h))
    return pl.pallas_call(
        body, name="dn_scan_fwd", grid=(groups, s_len // rb),
        in_specs=[blk(0), blk(groups), blk(0), blk(0),
                  pl.BlockSpec((hps, rb, LANES), lambda h, i: (h, i, 0))],
        out_specs=[blk(0),
                   pl.BlockSpec((hps, n_chunk, LANES, LANES), lambda h, i: (h, i, 0, 0))],
        out_shape=[jax.ShapeDtypeStruct((s_len, DN_HEADS * LANES), F32),
                   jax.ShapeDtypeStruct((DN_HEADS, total_chunks, LANES, LANES), F32)],
        scratch_shapes=[pltpu.VMEM((hps, LANES, LANES), F32)],
        compiler_params=_cparams(("parallel", "arbitrary")),
    )(qkv, qkv, u, w, hs)


def dn_scan_bwd(qkv, u, w, t_inv, hs, states, d_o):
    s_len = qkv.shape[0]
    rb = min(ROW_TILE, s_len)
    n_chunk = rb // DN_CHUNK
    n_blk = s_len // rb
    hps = BWD_HEADS_PER_STEP
    groups = DN_HEADS // hps

    def body(q_ref, k_ref, v_ref, u_ref, w_ref, t_ref, hs_ref, st_ref, do_ref,
             dq_ref, dk_ref, dv_ref, dhs_ref, dstate):
        @pl.when(pl.program_id(1) == 0)
        def _():
            dstate[...] = jnp.zeros_like(dstate)

        r, c = _chunk_masks()
        causal = r >= c
        strict = r > c
        lane = _iota((DN_CHUNK, LANES), 1)
        upper = jnp.where(r <= c, 1.0, 0.0)
        last_row = _iota((DN_CHUNK, 1), 0) == DN_CHUNK - 1

        def chunk(m, carry):
            n = n_chunk - 1 - m
            rows = pl.ds(pl.multiple_of(n * DN_CHUNK, DN_CHUNK), DN_CHUNK)
            heads = range(hps)
            cols = [slice(h * LANES, (h + 1) * LANES) for h in heads]
            each = lambda f, *xs: [f(*a) for a in zip(*xs)]
            rsum = lambda x: jnp.sum(x, axis=-1, keepdims=True)
            dims_tn = (((0,), (0,)), ((), ()))
            ones = jnp.ones((DN_CHUNK, LANES), F32)
            q = [q_ref[rows, cl] for cl in cols]
            k = [k_ref[rows, cl] for cl in cols]
            v = [v_ref[rows, cl] for cl in cols]
            uu = [u_ref[rows, cl] for cl in cols]
            ww = [w_ref[rows, cl] for cl in cols]
            do = [do_ref[rows, cl] for cl in cols]
            tt = [t_ref[h, rows, :] for h in heads]
            beta = [hs_ref[h, rows, 0:1] for h in heads]
            gc = [hs_ref[h, rows, 2:3] for h in heads]
            st = [st_ref[h, n] for h in heads]
            dst = [dstate[h] for h in heads]
            gl = each(lambda g: jnp.min(g, axis=0, keepdims=True), gc)
            egc = each(jnp.exp, gc)
            egl = each(jnp.exp, gl)
            ekd = each(lambda a, b: jnp.exp(a - b), gl, gc)
            decay = each(lambda g: _decay_matrix(g, r, c), gc)
            qd = each(jnp.multiply, q, egc)
            kd = each(jnp.multiply, k, ekd)
            kb = each(jnp.multiply, k, beta)
            w_st = each(_dot, ww, st)
            qk = each(_dot_nt, q, k)
            dqd = each(_dot_nt, do, st)
            kd_dst = each(_dot, kd, dst)
            qd_do = each(_dot_tn, qd, do)
            kbk = each(_dot_nt, kb, k)
            vn = each(jnp.subtract, uu, w_st)
            ai = each(jnp.multiply, qk, decay)
            low = each(lambda a, d: jnp.where(strict, a * d, 0.0), kbk, decay)
            dai = each(lambda a, b: jnp.where(causal, _dot_nt(a, b), 0.0), do, vn)
            ai_do = each(_dot_tn, ai, do)
            dkd = each(_dot_nt, vn, dst)
            dvn = each(jnp.add, ai_do, kd_dst)
            dp = each(jnp.multiply, dai, decay)
            dw = each(lambda a, b: -_dot_nt(a, b), dvn, st)
            w_dvn = each(_dot_tn, ww, dvn)
            dp_k = each(_dot, dp, k)
            dp_q = each(_dot_tn, dp, q)
            drhs_u = each(lambda a, b: _dot_hi(a, b, dims_tn), tt, dvn)
            dgl = each(lambda a, b, e: jnp.sum(a * b) * e, dst, st, egl)
            for h in heads:
                dstate[h] = dst[h] * egl[h] + qd_do[h] - w_dvn[h]
            dq = each(lambda a, e, b: a * e + b, dqd, egc, dp_k)
            dk_a = each(lambda a, e, b: a * e + b, dkd, ekd, dp_q)
            rkd = each(lambda a, b: rsum(a * b), dkd, kd)
            drhs_w = each(lambda a, b: _dot_hi(a, b, dims_tn), tt, dw)
            dl_u = each(_dot_nt, drhs_u, uu)
            dl_w = each(_dot_nt, drhs_w, ww)
            dlow = each(lambda a, b: jnp.where(strict, -(a + b), 0.0), dl_u, dl_w)
            dqm = each(jnp.multiply, dlow, decay)
            m_tot = each(lambda a, b, d, e: a * b + d * e, dai, ai, dlow, low)
            dqm_k = each(_dot, dqm, k)
            dk_l = each(_dot_tn, dqm, kb)
            col_sums = each(lambda m: _dot_hi(m, ones, dims_tn), m_tot)
            dkb_w = each(jnp.multiply, drhs_w, egc)
            dkb = each(jnp.add, dkb_w, dqm_k)
            dgc = [rsum(dqd[h] * qd[h]) - rkd[h] + jnp.where(last_row, jnp.sum(rkd[h]) + dgl[h], 0.0)
                   + rsum(m_tot[h]) + rsum(dkb_w[h] * kb[h]) for h in heads]
            dg = each(lambda a, b: _dot_hi(upper, jnp.where(lane == 1, a - b, 0.0)), dgc, col_sums)
            for h in heads:
                dq_ref[rows, cols[h]] = dq[h]
                dk_ref[rows, cols[h]] = dk_a[h] + dk_l[h] + dkb[h] * beta[h]
                dv_ref[rows, cols[h]] = drhs_u[h] * beta[h]
                dbeta = rsum(drhs_u[h] * v[h]) + rsum(dkb[h] * k[h])
                dhs_ref[h, rows, :] = jnp.where(lane == 0, dbeta, dg[h])
            return carry

        lax.fori_loop(0, n_chunk, chunk, 0)

    wide = hps * LANES
    blk = lambda off: pl.BlockSpec((rb, wide), lambda h, i: (n_blk - 1 - i, off + h))
    head = blk(0)
    hs_spec = pl.BlockSpec((hps, rb, LANES), lambda h, i: (h, n_blk - 1 - i, 0))
    full = jax.ShapeDtypeStruct((s_len, DN_HEADS * LANES), F32)
    return pl.pallas_call(
        body, name="dn_scan_bwd", grid=(groups, n_blk),
        in_specs=[blk(0), blk(groups), blk(2 * groups), head, head,
                  pl.BlockSpec((hps, rb, DN_CHUNK), lambda h, i: (h, n_blk - 1 - i, 0)), hs_spec,
                  pl.BlockSpec((hps, n_chunk, LANES, LANES), lambda h, i: (h, n_blk - 1 - i, 0, 0)), head],
        out_specs=[head, head, head, hs_spec],
        out_shape=[full, full, full, jax.ShapeDtypeStruct((DN_HEADS, s_len, LANES), F32)],
        scratch_shapes=[pltpu.VMEM((hps, LANES, LANES), F32)],
        compiler_params=_cparams(("parallel", "arbitrary")),
    )(qkv, qkv, qkv, u, w, t_inv, hs, states, d_o)


def dn_post(o, z, out_norm):
    def body(i, ins, outs, accs):
        gn = ins[2][...]
        for h in range(DN_HEADS):
            cols = slice(h * LANES, (h + 1) * LANES)
            xhat, _ = _rms_stats(ins[0][:, cols])
            outs[0][:, cols] = (xhat * gn * _silu(ins[1][:, cols])).astype(BF16)

    return row_call("dn_post", body, o.shape[0], ROW_TILE, [(o, "tile"), (z, "tile"), (out_norm, "full")],
                    [(o.shape, BF16)])[0]


def dn_post_bwd(d_og, o, z, out_norm):
    def body(i, ins, outs, accs):
        gn = ins[3][...]
        dgn = jnp.zeros((1, LANES), F32)
        for h in range(DN_HEADS):
            cols = slice(h * LANES, (h + 1) * LANES)
            dy, zh = ins[0][:, cols].astype(F32), ins[2][:, cols]
            xhat, r = _rms_stats(ins[1][:, cols])
            sz = _silu(zh)
            dgn = dgn + jnp.sum(dy * xhat * sz, axis=0, keepdims=True)
            outs[1][:, cols] = (dy * xhat * gn * _silu_grad(zh)).astype(BF16)
            dxhat = dy * gn * sz
            outs[0][:, cols] = r * (dxhat - xhat * jnp.mean(dxhat * xhat, axis=-1, keepdims=True))
        accs[0][...] += dgn

    return row_call("dn_post_bwd", body, o.shape[0], ROW_TILE,
                    [(d_og, "tile"), (o, "tile"), (z, "tile"), (out_norm, "full")],
                    [(o.shape, F32), (o.shape, BF16)], [(1, LANES)])


def dn_pre_bwd(dq, dk, dv, dhs, qkv_raw, ba, w_conv, gate):
    s_len = qkv_raw.shape[0]
    tm = min(DN_ROW_TILE, s_len)

    def body(i, ins, outs, accs):
        dq_ref, dk_ref, dv_ref, dhs_ref, x_ref, xp_ref, ba_ref, w_ref, gate_ref = ins
        dc_ref, dba_ref = outs

        def blk(cb, carry):
            cols = pl.ds(pl.multiple_of(cb * LANES, LANES), LANES)
            hcols = pl.ds(pl.multiple_of((cb & (DN_HEADS - 1)) * LANES, LANES), LANES)
            prev = jnp.where(i > 0, xp_ref[:, cols], 0.0)
            xcat = jnp.concatenate([prev, x_ref[:, cols]], axis=0)
            c = _conv_taps(xcat, w_ref, cols, DN_CONV, DN_HALO, tm)
            y = _silu(c)
            dy = jnp.where(cb < DN_HEADS, dq_ref[:, hcols],
                           jnp.where(cb < 2 * DN_HEADS, dk_ref[:, hcols], dv_ref[:, hcols]))
            rs = lax.rsqrt(jnp.sum(y * y, axis=-1, keepdims=True) + L2_EPS)
            fac = jnp.where(cb < DN_HEADS, DN_HEAD_DIM ** -0.5, 1.0)
            nrm = y * rs
            dn = dy * fac
            dy_norm = rs * (dn - nrm * jnp.sum(dn * nrm, axis=-1, keepdims=True))
            dc_ref[:, cols] = jnp.where(cb < 2 * DN_HEADS, dy_norm, dy) * _silu_grad(c)
            return carry

        lax.fori_loop(0, qkv_raw.shape[1] // LANES, blk, 0)

        lane = _iota((tm, LANES), 1)
        dbeta = jnp.zeros((tm, LANES), F32)
        dg = jnp.zeros((tm, LANES), F32)
        for h in range(DN_HEADS):
            dbeta = dbeta + jnp.where(lane == h, dhs_ref[h, :, 0:1], 0.0)
            dg = dg + jnp.where(lane == DN_HEADS + h, dhs_ref[h, :, 1:2], 0.0)
        bav = ba_ref[...]
        beta = _sigmoid(bav)
        ea = jnp.exp(gate_ref[0:1, :])
        pre = bav + gate_ref[1:2, :]
        g = -ea * _softplus(pre)
        da = dg * (-ea) * _sigmoid(pre)
        dba_ref[...] = (dbeta * beta * (1.0 - beta) + da).astype(BF16)
        rid = _iota((8, LANES), 0)
        accs[0][...] += (jnp.where(rid == 0, jnp.sum(dg * g, axis=0, keepdims=True), 0.0)
                         + jnp.where(rid == 1, jnp.sum(da, axis=0, keepdims=True), 0.0))

    return row_call("dn_pre_bwd", body, s_len, tm,
                    [(dq, "tile"), (dk, "tile"), (dv, "tile"), (dhs, "tile"), (qkv_raw, "tile"),
                     (qkv_raw, ("prev", DN_HALO)), (ba, "tile"), (w_conv, "full"), (gate, "full")],
                    [(qkv_raw.shape, F32), (ba.shape, BF16)], [(8, LANES)])


def dn_conv_bwd(dc, qkv_raw, w_conv):
    s_len = dc.shape[0]
    tm = min(DN_ROW_TILE, s_len)
    nt = s_len // tm

    def body(i, ins, outs, accs):
        dc_ref, dn_ref, x_ref, xp_ref, w_ref = ins

        def blk(cb, carry):
            cols = pl.ds(pl.multiple_of(cb * LANES, LANES), LANES)
            dy = dc_ref[:, cols]
            nxt = jnp.where(i < nt - 1, dn_ref[:, cols], 0.0)
            dcat = jnp.concatenate([dy, nxt], axis=0)
            outs[0][:, cols] = _conv_taps_bwd_x(dcat, w_ref, cols, DN_CONV, DN_HALO, tm).astype(BF16)
            prev = jnp.where(i > 0, xp_ref[:, cols], 0.0)
            xcat = jnp.concatenate([prev, x_ref[:, cols]], axis=0)
            accs[0][:, cols] += _conv_taps_bwd_w(dy, xcat, DN_CONV, DN_HALO, tm, 8)
            return carry

        lax.fori_loop(0, dc.shape[1] // LANES, blk, 0)

    return row_call("dn_conv_bwd", body, s_len, tm,
                    [(dc, "tile"), (dc, ("next", DN_HALO)), (qkv_raw, "tile"), (qkv_raw, ("prev", DN_HALO)),
                     (w_conv, "full")],
                    [(dc.shape, BF16)], [(8, dc.shape[1])])


def _glu(u_ref, cols, d):
    return u_ref[:, cols] * _sigmoid(u_ref[:, pl.ds(pl.multiple_of(d + cols.start, LANES), cols.size)])


def cv_core_fwd(u, w_dw, b_dw, ln_g, ln_b):
    s_len, d = u.shape[0], u.shape[1] // 2
    tm = min(CONV_ROW_TILE, s_len)

    def body(i, ins, outs, accs):
        u_ref, up_ref, w_ref, bdw_ref, g_ref, b_ref = ins
        s_ref, c_ref = outs

        def blk(cb, carry):
            cols = pl.ds(pl.multiple_of(cb * LANES, LANES), LANES)
            prev = jnp.where(i > 0, _glu(up_ref, cols, d), 0.0)
            xcat = jnp.concatenate([prev, _glu(u_ref, cols, d)], axis=0)
            c_ref[:, cols] = _conv_taps(xcat, w_ref, cols, CV_WIDTH, CV_HALO, tm) + bdw_ref[:, cols]
            return carry

        lax.fori_loop(0, d // LANES, blk, 0)
        c = c_ref[...]
        mu = jnp.mean(c, axis=-1, keepdims=True)
        xc = c - mu
        rstd = lax.rsqrt(jnp.mean(xc * xc, axis=-1, keepdims=True) + LN_EPS)
        s_ref[...] = _silu(xc * rstd * g_ref[...] + b_ref[...]).astype(BF16)

    return row_call("cv_core_fwd", body, s_len, tm,
                    [(u, "tile"), (u, ("prev", CV_HALO)), (w_dw, "full"), (b_dw, "full"), (ln_g, "full"),
                     (ln_b, "full")],
                    [((s_len, d), BF16), ((s_len, d), F32)])


def cv_ln_bwd(ds, c, ln_g, ln_b):
    def body(i, ins, outs, accs):
        cv, g = ins[1][...], ins[2][...]
        mu = jnp.mean(cv, axis=-1, keepdims=True)
        xc = cv - mu
        rstd = lax.rsqrt(jnp.mean(xc * xc, axis=-1, keepdims=True) + LN_EPS)
        xhat = xc * rstd
        dl = ins[0][...].astype(F32) * _silu_grad(xhat * g + ins[3][...])
        dxhat = dl * g
        dc = rstd * (dxhat - jnp.mean(dxhat, axis=-1, keepdims=True)
                     - xhat * jnp.mean(dxhat * xhat, axis=-1, keepdims=True))
        outs[0][...] = dc
        rid = _iota((8, cv.shape[1]), 0)
        accs[0][...] += (jnp.where(rid == 0, jnp.sum(dl * xhat, axis=0, keepdims=True), 0.0)
                         + jnp.where(rid == 1, jnp.sum(dl, axis=0, keepdims=True), 0.0)
                         + jnp.where(rid == 2, jnp.sum(dc, axis=0, keepdims=True), 0.0))

    return row_call("cv_ln_bwd", body, c.shape[0], ROW_TILE,
                    [(ds, "tile"), (c, "tile"), (ln_g, "full"), (ln_b, "full")], [(c.shape, F32)], [(8, c.shape[1])])


def cv_conv_bwd(dc, u, w_dw):
    s_len, d = dc.shape
    tm = min(CONV_ROW_TILE, s_len)
    nt = s_len // tm

    def body(i, ins, outs, accs):
        dc_ref, dn_ref, u_ref, up_ref, w_ref = ins

        def blk(cb, carry):
            cols = pl.ds(pl.multiple_of(cb * LANES, LANES), LANES)
            gcols = pl.ds(pl.multiple_of(d + cb * LANES, LANES), LANES)
            dy = dc_ref[:, cols]
            nxt = jnp.where(i < nt - 1, dn_ref[:, cols], 0.0)
            dgl = _conv_taps_bwd_x(jnp.concatenate([dy, nxt], axis=0), w_ref, cols, CV_WIDTH, CV_HALO, tm)
            u1, sg = u_ref[:, cols], _sigmoid(u_ref[:, gcols])
            du1 = dgl * sg
            du2 = dgl * u1 * sg * (1.0 - sg)
            outs[0][:, cols] = du1.astype(BF16)
            outs[0][:, gcols] = du2.astype(BF16)
            accs[1][:, cols] += jnp.sum(du1, axis=0, keepdims=True)
            accs[1][:, gcols] += jnp.sum(du2, axis=0, keepdims=True)
            prev = jnp.where(i > 0, _glu(up_ref, cols, d), 0.0)
            xcat = jnp.concatenate([prev, u1 * sg], axis=0)
            accs[0][:, cols] += _conv_taps_bwd_w(dy, xcat, CV_WIDTH, CV_HALO, tm, CV_HALO)
            return carry

        lax.fori_loop(0, d // LANES, blk, 0)

    return row_call("cv_conv_bwd", body, s_len, tm,
                    [(dc, "tile"), (dc, ("next", CV_HALO)), (u, "tile"), (u, ("prev", CV_HALO)), (w_dw, "full")],
                    [(u.shape, BF16)], [(CV_HALO, d), (1, 2 * d)])


def xa_core_fwd(name, q, kv):
    d = q.shape[1]

    def body(i, ins, outs, accs):
        for h in range(XA_HEADS):
            cols = slice(h * XA_HEAD_DIM, (h + 1) * XA_HEAD_DIM)
            vcols = slice(d + h * XA_HEAD_DIM, d + (h + 1) * XA_HEAD_DIM)
            s = _dot_nt(ins[0][:, cols], ins[1][:, cols]) * (XA_HEAD_DIM ** -0.5)
            e = jnp.exp(s - jnp.max(s, axis=-1, keepdims=True))
            p = e / jnp.sum(e, axis=-1, keepdims=True)
            outs[0][:, cols] = _dot(p, ins[1][:, vcols]).astype(BF16)

    return row_call(name, body, q.shape[0], ROW_TILE, [(q, "tile"), (kv, "full")], [(q.shape, BF16)])[0]


def xa_core_bwd(name, d_o, q, kv):
    d = q.shape[1]

    def body(i, ins, outs, accs):
        for h in range(XA_HEADS):
            cols = slice(h * XA_HEAD_DIM, (h + 1) * XA_HEAD_DIM)
            vcols = slice(d + h * XA_HEAD_DIM, d + (h + 1) * XA_HEAD_DIM)
            qh, kh, vh, doh = ins[1][:, cols], ins[2][:, cols], ins[2][:, vcols], ins[0][:, cols]
            s = _dot_nt(qh, kh) * (XA_HEAD_DIM ** -0.5)
            e = jnp.exp(s - jnp.max(s, axis=-1, keepdims=True))
            p = e / jnp.sum(e, axis=-1, keepdims=True)
            dp = _dot_nt(doh, vh)
            ds = p * (dp - jnp.sum(dp * p, axis=-1, keepdims=True)) * (XA_HEAD_DIM ** -0.5)
            outs[0][:, cols] = _dot(ds, kh).astype(BF16)
            accs[0][:, cols] += _dot_tn(ds, qh)
            accs[0][:, vcols] += _dot_tn(p, doh)

    return row_call(name, body, q.shape[0], ROW_TILE, [(d_o, "tile"), (q, "tile"), (kv, "full")],
                    [(q.shape, BF16)], [kv.shape])


def adamw(name, w, g, m, v):
    def body(i, ins, outs, accs):
        wv, gv = ins[0][...], ins[1][...]
        mn = ADAM_B1 * ins[2][...] + (1.0 - ADAM_B1) * gv
        vn = ADAM_B2 * ins[3][...] + (1.0 - ADAM_B2) * jnp.square(gv)
        m_hat = mn / (1.0 - ADAM_B1 ** ADAM_STEP)
        v_hat = vn / (1.0 - ADAM_B2 ** ADAM_STEP)
        outs[0][...] = -ADAM_LR * (m_hat / (jnp.sqrt(v_hat) + ADAM_EPS) + ADAM_WD * wv)
        outs[1][...] = mn
        outs[2][...] = vn

    return row_call(name, body, w.shape[0], ROW_TILE, [(w, "tile"), (g, "tile"), (m, "tile"), (v, "tile")],
                    [(w.shape, F32)] * 3)


def adamw_halves(name, w, g_mine, g_sibling, m, v, core):
    n_layers = len(g_mine)
    rows, cols = w.shape
    half_rows = rows // n_layers // 2
    tm = _pick_tile(half_rows, ADAMW_ROW_TILE)
    per_half = half_rows // tm

    def body(core_ref, w_ref, *rest):
        g_refs = rest[:2 * n_layers]
        m_ref, v_ref, g_out, d_out, m_out, v_out = rest[2 * n_layers:]
        i = pl.program_id(0)
        mine = ((i // per_half) % 2) == core_ref[0]
        layer = i // (2 * per_half)
        gv = jnp.where(mine, g_refs[0][...], g_refs[n_layers][...])
        for l in range(1, n_layers):
            gv = jnp.where(layer == l, jnp.where(mine, g_refs[l][...], g_refs[n_layers + l][...]), gv)
        mn = ADAM_B1 * m_ref[...] + (1.0 - ADAM_B1) * gv
        vn = ADAM_B2 * v_ref[...] + (1.0 - ADAM_B2) * jnp.square(gv)
        m_hat = mn / (1.0 - ADAM_B1 ** ADAM_STEP)
        v_hat = vn / (1.0 - ADAM_B2 ** ADAM_STEP)
        g_out[...] = gv
        d_out[...] = -ADAM_LR * (m_hat / (jnp.sqrt(v_hat) + ADAM_EPS) + ADAM_WD * w_ref[...])
        m_out[...] = mn
        v_out[...] = vn

    whole = pl.BlockSpec((tm, cols), lambda i, core_ref: (i, 0))
    half = pl.BlockSpec((tm, cols), lambda i, core_ref: (i % per_half, 0))
    return pl.pallas_call(
        body, name=name,
        grid_spec=pltpu.PrefetchScalarGridSpec(
            num_scalar_prefetch=1, grid=(2 * per_half * n_layers,),
            in_specs=[whole] + [half] * (2 * n_layers) + [whole, whole], out_specs=[whole] * 4),
        out_shape=[jax.ShapeDtypeStruct(w.shape, F32)] * 4,
        compiler_params=_cparams(("parallel",)),
    )(core, w, *g_mine, *g_sibling, m, v)


HBM_SPEC = pl.BlockSpec(memory_space=pltpu.HBM)


def _position():
    return lax.axis_index("x"), lax.axis_index("y"), lax.axis_index("c")


def _other_chips(x, y):
    return [(1 - x, y), (x, 1 - y), (1 - x, 1 - y)]


def _row_chunks(rows):
    return rows // D2D_CHUNK_ROWS if rows % D2D_CHUNK_ROWS == 0 else 1


def _start_chunked(make, rows):
    k = _row_chunks(rows)
    for i in range(k):
        make(i * (rows // k), rows // k).start()


def gather_shards(packs):
    n = len(packs)

    def body(*refs):
        srcs, outs = refs[:n], refs[n:2 * n]
        send_sems, recv_sems = refs[2 * n:]
        x, y, c = _position()
        me = 2 * x + y
        chips = _other_chips(x, y)
        sibling = (x, y, 1 - c)

        def over_ici(a, j):
            px, py = chips[j]
            rows = srcs[a].shape[0] // 2
            return pltpu.make_async_remote_copy(
                src_ref=srcs[a].at[pl.ds(c * rows, rows), :], dst_ref=outs[a].at[me, pl.ds(c * rows, rows), :],
                send_sem=send_sems.at[a, j], recv_sem=recv_sems.at[a, j], device_id=(px, py, c), device_id_type=MESH)

        def landed(a, j):
            px, py = chips[j]
            rows = srcs[a].shape[0] // 2
            part = outs[a].at[2 * px + py, pl.ds(c * rows, rows), :]
            return pltpu.make_async_remote_copy(
                src_ref=part, dst_ref=part, send_sem=send_sems.at[a, j], recv_sem=recv_sems.at[a, j],
                device_id=(px, py, c), device_id_type=MESH)

        def over_d2d(a, j, cc, off, size):
            px, py = chips[j]
            rows = srcs[a].shape[0] // 2
            part = outs[a].at[2 * px + py, pl.ds(cc * rows + off, size), :]
            return pltpu.make_async_remote_copy(
                src_ref=part, dst_ref=part, send_sem=send_sems.at[a, 3 + j], recv_sem=recv_sems.at[a, 3 + j],
                device_id=sibling, device_id_type=MESH)

        for a in range(n):
            for j in range(3):
                over_ici(a, j).start()
        for a in range(n):
            for j in range(3):
                landed(a, j).wait_recv()
                _start_chunked(functools.partial(over_d2d, a, j, c), srcs[a].shape[0] // 2)
        for a in range(n):
            rows = srcs[a].shape[0] // 2
            for j in range(3):
                over_d2d(a, j, 1 - c, 0, rows).wait_recv()
                over_d2d(a, j, c, 0, rows).wait_send()
                over_ici(a, j).wait_send()

    return pl.pallas_call(
        body, name="gather_shards",
        in_specs=[HBM_SPEC] * n, out_specs=[HBM_SPEC] * n,
        out_shape=[jax.ShapeDtypeStruct((N_CHIPS,) + p.shape, p.dtype) for p in packs],
        scratch_shapes=[pltpu.SemaphoreType.DMA((n, 6)), pltpu.SemaphoreType.DMA((n, 6))],
    )(*packs)


def pair_split(name, packs):
    n = len(packs)

    def body(*refs):
        srcs, outs = refs[:n], refs[n:2 * n]
        send_sems, recv_sems = refs[2 * n:]
        x, y, c = _position()

        def remote(a, off, size):
            rows = srcs[a].shape[1] // 2
            return pltpu.make_async_remote_copy(
                src_ref=srcs[a].at[:, pl.ds((1 - c) * rows + off, size), :],
                dst_ref=outs[a].at[:, pl.ds(off, size), :],
                send_sem=send_sems.at[a], recv_sem=recv_sems.at[a], device_id=(x, y, 1 - c), device_id_type=MESH)

        for a in range(n):
            _start_chunked(functools.partial(remote, a), srcs[a].shape[1] // 2)
        for a in range(n):
            remote(a, 0, srcs[a].shape[1] // 2).wait()

    return pl.pallas_call(
        body, name=name, in_specs=[HBM_SPEC] * n, out_specs=[HBM_SPEC] * n,
        out_shape=[jax.ShapeDtypeStruct((p.shape[0], p.shape[1] // 2, p.shape[2]), p.dtype) for p in packs],
        scratch_shapes=[pltpu.SemaphoreType.DMA((n,)), pltpu.SemaphoreType.DMA((n,))],
    )(*packs)


def chip_scatter(packs):
    n = len(packs)

    def body(*refs):
        srcs, outs = refs[:n], refs[n:2 * n]
        send_sems, recv_sems = refs[2 * n:]
        x, y, c = _position()
        copies = []
        for a in range(n):
            for j, (px, py) in enumerate(_other_chips(x, y)):
                cp = pltpu.make_async_remote_copy(
                    src_ref=srcs[a].at[2 * px + py], dst_ref=outs[a].at[j],
                    send_sem=send_sems.at[a, j], recv_sem=recv_sems.at[a, j],
                    device_id=(px, py, c), device_id_type=MESH)
                cp.start()
                copies.append(cp)
        for cp in copies:
            cp.wait()

    return pl.pallas_call(
        body, name="chip_scatter", in_specs=[HBM_SPEC] * n, out_specs=[HBM_SPEC] * n,
        out_shape=[jax.ShapeDtypeStruct((N_CHIPS - 1,) + p.shape[1:], p.dtype) for p in packs],
        scratch_shapes=[pltpu.SemaphoreType.DMA((n, 3)), pltpu.SemaphoreType.DMA((n, 3))],
    )(*packs)


def pair_join(halves):
    n = len(halves)

    def body(*refs):
        srcs, outs = refs[:n], refs[n:2 * n]
        send_sems, recv_sems = refs[2 * n:]
        x, y, c = _position()

        def remote(a, off, size):
            return pltpu.make_async_remote_copy(
                src_ref=srcs[a].at[pl.ds(off, size), :], dst_ref=outs[a].at[pl.ds(off, size), :],
                send_sem=send_sems.at[a], recv_sem=recv_sems.at[a], device_id=(x, y, 1 - c), device_id_type=MESH)

        for a in range(n):
            _start_chunked(functools.partial(remote, a), srcs[a].shape[0])
        for a in range(n):
            remote(a, 0, srcs[a].shape[0]).wait()

    return pl.pallas_call(
        body, name="pair_join", in_specs=[HBM_SPEC] * n, out_specs=[HBM_SPEC] * n,
        out_shape=[jax.ShapeDtypeStruct(p.shape, p.dtype) for p in halves],
        scratch_shapes=[pltpu.SemaphoreType.DMA((n,)), pltpu.SemaphoreType.DMA((n,))],
    )(*halves)


SEM_SPEC = pl.BlockSpec(memory_space=pltpu.SEMAPHORE)
DATAFLOW = pltpu.SideEffectType.DATAFLOW_SIDE_EFFECTING


def _ici_copy(kind, srcs, lands, send_sems, recv_sems, a, j):
    x, y, c = _position()
    px, py = _other_chips(x, y)[j]
    if kind == "gather":
        rows = srcs[a].shape[0] // 2
        src = srcs[a].at[pl.ds(c * rows, rows), :]
        dst = lands[a].at[2 * x + y, pl.ds(c * rows, rows), :]
    else:
        src = srcs[a].at[2 * px + py]
        dst = lands[a].at[j]
    return pltpu.make_async_remote_copy(src_ref=src, dst_ref=dst, send_sem=send_sems, recv_sem=recv_sems,
                                        device_id=(px, py, c), device_id_type=MESH)


def ici_start(name, kind, srcs, land_shapes):
    n = len(srcs)
    lands = [pltpu.with_memory_space_constraint(lax.empty(shp, s.dtype), pltpu.HBM) for shp, s in zip(land_shapes, srcs)]

    def body(*refs):
        src_refs, land_refs = refs[:n], refs[n:2 * n]
        send_sems, recv_sems = refs[2 * n], refs[2 * n + 1]
        token = refs[-1]
        for a in range(n):
            for j in range(N_CHIPS - 1):
                _ici_copy(kind, src_refs, land_refs, send_sems, recv_sems, a, j).start()
        token[...] = jnp.zeros_like(token)

    sems = pltpu.SemaphoreType.DMA(())
    res = pl.pallas_call(
        body, name=name,
        out_shape=[sems, sems] + [pltpu.HBM(s.shape, s.dtype) for s in srcs]
        + [pltpu.HBM(l.shape, l.dtype) for l in lands] + [jax.ShapeDtypeStruct((8, LANES), F32)],
        in_specs=[HBM_SPEC] * (2 * n),
        out_specs=[SEM_SPEC, SEM_SPEC] + [HBM_SPEC] * (2 * n) + [pl.BlockSpec(memory_space=pltpu.VMEM)],
        input_output_aliases={i: 2 + i for i in range(2 * n)},
        compiler_params=pltpu.CompilerParams(has_side_effects=DATAFLOW),
    )(*[pltpu.with_memory_space_constraint(s, pltpu.HBM) for s in srcs], *lands)
    return res[0], res[1], list(res[2:2 + n]), list(res[2 + n:2 + 2 * n]), res[-1]


def ici_wait(name, kind, send_sems, recv_sems, srcs, lands, after):
    n = len(srcs)

    def body(*refs):
        src_refs, land_refs = refs[:n], refs[n:2 * n]
        send, recv = refs[2 * n], refs[2 * n + 1]
        for a in range(n):
            for j in range(N_CHIPS - 1):
                cp = _ici_copy(kind, src_refs, land_refs, send, recv, a, j)
                cp.wait_send()
                cp.wait_recv()

    res = pl.pallas_call(
        body, name=name,
        out_shape=[pltpu.HBM(s.shape, s.dtype) for s in srcs] + [pltpu.HBM(l.shape, l.dtype) for l in lands],
        in_specs=[HBM_SPEC] * (2 * n) + [SEM_SPEC, SEM_SPEC, pl.BlockSpec(memory_space=pl.ANY)],
        out_specs=[HBM_SPEC] * (2 * n),
        input_output_aliases={i: i for i in range(2 * n)},
        compiler_params=pltpu.CompilerParams(has_side_effects=DATAFLOW),
    )(*srcs, *lands, send_sems, recv_sems, after)
    return list(res[:n]), list(res[n:])


def pair_forward(gathered):
    n = len(gathered)

    def body(*refs):
        outs = refs[n:2 * n]
        send_sems, recv_sems = refs[2 * n:]
        x, y, c = _position()
        chips = _other_chips(x, y)

        def part(a, j, cc, off, size):
            px, py = chips[j]
            rows = outs[a].shape[1] // 2
            ref = outs[a].at[2 * px + py, pl.ds(cc * rows + off, size), :]
            return pltpu.make_async_remote_copy(
                src_ref=ref, dst_ref=ref, send_sem=send_sems.at[a, j], recv_sem=recv_sems.at[a, j],
                device_id=(x, y, 1 - c), device_id_type=MESH)

        for a in range(n):
            for j in range(N_CHIPS - 1):
                _start_chunked(functools.partial(part, a, j, c), outs[a].shape[1] // 2)
        for a in range(n):
            rows = outs[a].shape[1] // 2
            for j in range(N_CHIPS - 1):
                part(a, j, 1 - c, 0, rows).wait_recv()
                part(a, j, c, 0, rows).wait_send()

    return pl.pallas_call(
        body, name="pair_forward", in_specs=[HBM_SPEC] * n, out_specs=[HBM_SPEC] * n,
        out_shape=[jax.ShapeDtypeStruct(g.shape, g.dtype) for g in gathered],
        input_output_aliases={i: i for i in range(n)},
        scratch_shapes=[pltpu.SemaphoreType.DMA((n, N_CHIPS - 1)), pltpu.SemaphoreType.DMA((n, N_CHIPS - 1))],
    )(*gathered)


def all_sum_small(part):
    n_dev = 8
    rows = part.shape[0]

    def body(src, out, buf, send_sems, recv_sems):
        x, y, c = _position()
        me = 4 * x + 2 * y + c
        buf[me] = src[...]
        copies = []
        for k in range(1, n_dev):
            px, py, pc = x ^ ((k >> 2) & 1), y ^ ((k >> 1) & 1), c ^ (k & 1)
            cp = pltpu.make_async_remote_copy(
                src_ref=src, dst_ref=buf.at[me], send_sem=send_sems.at[k - 1], recv_sem=recv_sems.at[k - 1],
                device_id=(px, py, pc), device_id_type=MESH)
            cp.start()
            copies.append(cp)
        for cp in copies:
            cp.wait()
        acc = buf[0]
        for k in range(1, n_dev):
            acc = acc + buf[k]
        out[...] = acc

    return pl.pallas_call(
        body, name="all_sum_small",
        in_specs=[pl.BlockSpec(memory_space=pltpu.VMEM)], out_specs=pl.BlockSpec(memory_space=pltpu.VMEM),
        out_shape=jax.ShapeDtypeStruct(part.shape, F32),
        scratch_shapes=[pltpu.VMEM((n_dev, rows, part.shape[1]), F32),
                        pltpu.SemaphoreType.DMA((n_dev - 1,)), pltpu.SemaphoreType.DMA((n_dev - 1,))],
    )(part)


def add_pairs(name, src, theirs, core, out_dtype):
    slabs, rows, cols = theirs.shape
    tm = _pick_tile(rows, ROW_TILE)
    nb = rows // tm

    def body(core_ref, a_ref, b_ref, o_ref):
        o_ref[...] = (a_ref[...].astype(F32) + b_ref[...].astype(F32)).astype(out_dtype)

    return pl.pallas_call(
        body, name=name,
        grid_spec=pltpu.PrefetchScalarGridSpec(
            num_scalar_prefetch=1, grid=(slabs, nb),
            in_specs=[pl.BlockSpec((None, tm, cols), lambda s, i, core_ref: (s, core_ref[0] * nb + i, 0)),
                      pl.BlockSpec((None, tm, cols), lambda s, i, core_ref: (s, i, 0))],
            out_specs=pl.BlockSpec((None, tm, cols), lambda s, i, core_ref: (s, i, 0))),
        out_shape=jax.ShapeDtypeStruct(theirs.shape, out_dtype),
        compiler_params=_cparams(("parallel", "parallel")),
    )(core, src, theirs)


def add_four(name, src, theirs, chip):
    _, rows, cols = theirs.shape
    tm = _pick_tile(rows, ROW_TILE)

    def body(chip_ref, a_ref, b_ref, o_ref):
        acc = a_ref[...].astype(F32)
        for j in range(N_CHIPS - 1):
            acc = acc + b_ref[j].astype(F32)
        o_ref[...] = acc

    return pl.pallas_call(
        body, name=name,
        grid_spec=pltpu.PrefetchScalarGridSpec(
            num_scalar_prefetch=1, grid=(rows // tm,),
            in_specs=[pl.BlockSpec((None, tm, cols), lambda i, chip_ref: (chip_ref[0], i, 0)),
                      pl.BlockSpec((N_CHIPS - 1, tm, cols), lambda i, chip_ref: (0, i, 0))],
            out_specs=pl.BlockSpec((tm, cols), lambda i, chip_ref: (i, 0))),
        out_shape=jax.ShapeDtypeStruct((rows, cols), F32),
        compiler_params=_cparams(("parallel",)),
    )(chip, src, theirs)


PACK_COLS = 1024
BIG_ROW_MULTIPLE = 512
SMALL_ROW_MULTIPLE = 32
BIG = ["dn_w_in", "dn_w_out", "cv_w_pw1", "cv_w_pw2", "xa_w_q", "xa_w_kv", "xa_w_o", "mlp_w_up", "mlp_w_down"]
SMALL = ["dn_w_conv", "cv_norm", "cv_b_pw1", "cv_w_dw", "cv_b_dw", "cv_ln_g", "cv_ln_b", "cv_b_pw2"]
SHARD_AXIS = {"dn_w_in": 2, "dn_w_conv": 2, "dn_w_out": 1, "cv_norm": 1, "cv_w_pw1": 2, "cv_b_pw1": 1,
              "cv_w_dw": 2, "cv_b_dw": 1, "cv_ln_g": 1, "cv_ln_b": 1, "cv_w_pw2": 1, "cv_b_pw2": 1,
              "xa_w_q": 1, "xa_w_kv": 2, "xa_w_o": 1, "mlp_w_up": 2, "mlp_w_down": 1}
REPLICATED = ["dn_norm", "dn_a_log", "dn_dt_bias", "dn_out_norm", "xa_norm", "xa_mem_norm", "mlp_norm", "final_norm"]


def _pack_rows(size):
    return -(-size // PACK_COLS)


SHARD_SHAPES = {
    "dn_w_in": (1, 1024, 1028), "dn_w_conv": (1, 4, 768), "dn_w_out": (1, 256, 1024), "cv_norm": (1, 256),
    "cv_w_pw1": (1, 1024, 512), "cv_b_pw1": (1, 512), "cv_w_dw": (1, 31, 256), "cv_b_dw": (1, 256),
    "cv_ln_g": (1, 256), "cv_ln_b": (1, 256), "cv_w_pw2": (1, 256, 1024), "cv_b_pw2": (1, 256),
    "xa_w_q": (2, 256, 1024), "xa_w_kv": (2, 1024, 512), "xa_w_o": (2, 256, 1024),
    "mlp_w_up": (2, 1024, 1024), "mlp_w_down": (2, 1024, 1024)}


def _shard_shape(nm):
    return SHARD_SHAPES[nm]


def _pack(tensors, names, dtype, row_multiple):
    pieces = []
    for nm in names:
        t = tensors[nm]
        flat = t.reshape(t.shape[0], -1) if t.ndim > len(_shard_shape(nm)) else t.reshape(1, -1)
        pad = _pack_rows(flat.shape[1]) * PACK_COLS - flat.shape[1]
        pieces.append(jnp.pad(flat.astype(dtype), ((0, 0), (0, pad))))
    cat = jnp.concatenate(pieces, axis=1)
    rows = cat.shape[1] // PACK_COLS
    total = -(-rows // row_multiple) * row_multiple
    cat = jnp.pad(cat, ((0, 0), (0, (total - rows) * PACK_COLS)))
    return cat.reshape(cat.shape[0], total, PACK_COLS)


def _unpack(pack, names):
    lead = pack.shape[:-2]
    flat = pack.reshape(lead + (-1,))
    out, off = {}, 0
    for nm in names:
        shp = _shard_shape(nm)
        size = 1
        for s in shp:
            size *= s
        out[nm] = flat[..., off:off + size].reshape(lead + shp)
        off += _pack_rows(size) * PACK_COLS
    return out


def _to_full(nm, stacked):
    ax = SHARD_AXIS[nm]
    moved = jnp.moveaxis(stacked, 0, ax)
    shp = list(_shard_shape(nm))
    shp[ax] *= N_CHIPS
    return moved.reshape(shp)


def _to_shards(nm, full):
    ax = SHARD_AXIS[nm]
    shp = list(_shard_shape(nm))
    split = full.reshape(shp[:ax] + [N_CHIPS, shp[ax]] + shp[ax + 1:])
    return jnp.moveaxis(split, ax, 0)


def _row(v):
    return v.reshape(1, -1)


class Stacked:
    def __init__(self, arr, split, layer):
        self.arr, self.kw = arr, dict(b_split=split, b_layer=layer)


def _grad_out(split):
    return dict(out_dtype=BF16, out_split=split, out_layer=(0, 1))


def mlp_fwd(tag, h, g, w_up, w_down):
    n = rms_fwd(tag + "_norm", h, g)
    act = mm(tag + "_up", n, w_up.arr, out_dtype=BF16, epi=lambda acc: jnp.square(jnp.maximum(acc, 0.0)), **w_up.kw)
    out = mm(tag + "_down", act, w_down.arr, epi=lambda acc, res: acc + res, epi_tiles=(h,), **w_down.kw)
    return out, (n, act)


def mlp_bwd(tag, dh, h, g, w_up, w_down, saved, after=None):
    n, act = saved
    dup = mm(tag + "_d_act", dh, w_down.arr, tb=True, out_dtype=BF16, after=after,
             epi=lambda acc, t: acc * (2.0 * jnp.sqrt(t.astype(F32))), epi_tiles=(act,), **w_down.kw)
    dw_down = mm(tag + "_dw_down", act, dh, ta=True, tk=512, **_grad_out("rows"))
    dn = mm(tag + "_dn", dup, w_up.arr, tb=True, **w_up.kw)
    dw_up = mm(tag + "_dw_up", n, dup, ta=True, tk=512, **_grad_out("cols"))
    dh_in, dg = rms_bwd(tag + "_norm_bwd", dn, h, g, dh)
    return dh_in, dg, (dw_up, dw_down)


def xa_fwd(tag, h, mem, g, g_mem, w_q, w_kv, w_o):
    n = rms_fwd(tag + "_norm", h, g)
    mem_n = rms_fwd(tag + "_mem_norm", mem, g_mem)
    q = mm(tag + "_q", n, w_q.arr, out_dtype=BF16, **w_q.kw)
    kv = mm(tag + "_kv", mem_n, w_kv.arr, out_dtype=BF16, **w_kv.kw)
    o = xa_core_fwd(tag + "_core", q, kv)
    out = mm(tag + "_o", o, w_o.arr, epi=lambda acc, res: acc + res, epi_tiles=(h,), **w_o.kw)
    return out, (n, mem_n, q, kv, o)


def xa_bwd(tag, dh, h, mem, g, g_mem, w_q, w_kv, w_o, saved):
    n, mem_n, q, kv, o = saved
    d_o = mm(tag + "_d_o", dh, w_o.arr, tb=True, out_dtype=BF16, **w_o.kw)
    dw_o = mm(tag + "_dw_o", o, dh, ta=True, tk=512, **_grad_out("rows"))
    dq, dkv = xa_core_bwd(tag + "_core_bwd", d_o, q, kv)
    dn = mm(tag + "_dn", dq, w_q.arr, tb=True, **w_q.kw)
    dw_q = mm(tag + "_dw_q", n, dq, ta=True, tk=512, **_grad_out("rows"))
    dh_in, dg = rms_bwd(tag + "_norm_bwd", dn, h, g, dh)
    dw_kv = mm(tag + "_dw_kv", mem_n, dkv, ta=True, **_grad_out("cols"))
    dmem_n = mm(tag + "_dmem", dkv, w_kv.arr, tb=True, **w_kv.kw)
    dg_mem = mem_norm_bwd(tag + "_mem_norm_bwd", dmem_n, mem, g_mem)
    return dh_in, dg, dg_mem, (dw_q, dw_kv, dw_o)


def _gate_tile(a_log, dt_bias):
    t = jnp.zeros((8, LANES), F32)
    t = t.at[0, DN_HEADS:2 * DN_HEADS].set(a_log.reshape(-1))
    return t.at[1, DN_HEADS:2 * DN_HEADS].set(dt_bias.reshape(-1))


def dn_fwd(h, g, w_qkv, w_z, w_ba, w_conv, gate, out_norm, w_out):
    n = rms_fwd("dn_norm", h, g)
    qkv_raw = mm("dn_proj_qkv", n, w_qkv)
    z = mm("dn_proj_z", n, w_z)
    ba = mm("dn_proj_ba", n, w_ba)
    qkv, hs = dn_pre(qkv_raw, ba, w_conv, gate)
    u, w, t_inv = dn_solve(qkv, hs)
    o, states = dn_scan_fwd(qkv, u, w, hs)
    og = dn_post(o, z, out_norm)
    out = mm("dn_out", og, w_out.arr, epi=lambda acc, res: acc + res, epi_tiles=(h,), **w_out.kw)
    return out, (n, qkv_raw, z, ba, qkv, hs, u, w, t_inv, o, states, og)


def dn_bwd(dh, h, g, w_qkv, w_z, w_ba, w_conv, gate, out_norm, w_out, saved, after=None):
    n, qkv_raw, z, ba, qkv, hs, u, w, t_inv, o, states, og = saved
    d_og = mm("dn_d_og", dh, w_out.arr, tb=True, out_dtype=BF16, after=after, **w_out.kw)
    dw_out = mm("dn_dw_out", og, dh, ta=True, tk=512, **_grad_out("rows"))
    d_o, dz, d_out_norm = dn_post_bwd(d_og, o, z, out_norm)
    dq, dk, dv, dhs = dn_scan_bwd(qkv, u, w, t_inv, hs, states, d_o)
    dc, dba, d_gate = dn_pre_bwd(dq, dk, dv, dhs, qkv_raw, ba, w_conv, gate)
    dqkv_raw, dw_conv = dn_conv_bwd(dc, qkv_raw, w_conv)
    dn = mm("dn_dn_qkv", dqkv_raw, w_qkv, tb=True)
    dn = mm("dn_dn_z", dz, w_z, tb=True, epi=lambda acc, t: acc + t, epi_tiles=(dn,))
    dn = mm("dn_dn_ba", dba, w_ba, tb=True, epi=lambda acc, t: acc + t, epi_tiles=(dn,))
    dw_qkv = mm("dn_dw_qkv", n, dqkv_raw, ta=True, tk=512)
    dw_z = mm("dn_dw_z", n, dz, ta=True, tk=512)
    dw_ba = mm("dn_dw_ba", n, dba, ta=True, tk=512)
    dh_in, dg = rms_bwd("dn_norm_bwd", dn, h, g, dh)
    return dh_in, dg, dw_qkv, dw_z, dw_ba, dw_conv, d_gate, d_out_norm, dw_out


def cv_fwd(h, g, w_pw1, b_pw1, w_dw, b_dw, ln_g, ln_b, w_pw2, b_pw2):
    n = rms_fwd("cv_norm", h, g)
    u = mm("cv_pw1", n, w_pw1.arr, epi=lambda acc, b: acc + b, epi_rows=(b_pw1,), **w_pw1.kw)
    s, c = cv_core_fwd(u, w_dw, b_dw, ln_g, ln_b)
    out = mm("cv_pw2", s, w_pw2.arr, epi=lambda acc, res, b: acc + res + b, epi_tiles=(h,), epi_rows=(b_pw2,),
             **w_pw2.kw)
    return out, (n, u, s, c)


def cv_bwd(dh, h, g, w_pw1, w_dw, ln_g, ln_b, w_pw2, saved):
    n, u, s, c = saved
    ds = mm("cv_d_s", dh, w_pw2.arr, tb=True, out_dtype=BF16, **w_pw2.kw)
    dw_pw2 = mm("cv_dw_pw2", s, dh, ta=True, tk=512, **_grad_out("rows"))
    db_pw2 = col_sum("cv_db_pw2", dh)
    dc, ln_acc = cv_ln_bwd(ds, c, ln_g, ln_b)
    du, dw_dw, db_pw1 = cv_conv_bwd(dc, u, w_dw)
    dn = mm("cv_dn", du, w_pw1.arr, tb=True, **w_pw1.kw)
    dw_pw1 = mm("cv_dw_pw1", n, du, ta=True, tk=512, **_grad_out("cols"))
    dh_in, dg = rms_bwd("cv_norm_bwd", dn, h, g, dh)
    return dh_in, dg, dw_pw1, db_pw1, dw_dw, ln_acc, dw_pw2, db_pw2


WEIGHTS = ["dn_norm", "dn_w_in", "dn_w_conv", "dn_a_log", "dn_dt_bias", "dn_out_norm", "dn_w_out", "cv_norm",
           "cv_w_pw1", "cv_b_pw1", "cv_w_dw", "cv_b_dw", "cv_ln_g", "cv_ln_b", "cv_w_pw2", "cv_b_pw2", "xa_norm",
           "xa_mem_norm", "xa_w_q", "xa_w_kv", "xa_w_o", "mlp_norm", "mlp_w_up", "mlp_w_down", "final_norm"]


def _as_2d(t):
    if t.ndim == 1:
        return t.reshape(1, -1)
    return t.reshape(-1, t.shape[-1])


def kernel(x, mem, dn_norm, dn_w_in, dn_w_conv, dn_a_log, dn_dt_bias, dn_out_norm, dn_w_out, cv_norm, cv_w_pw1, cv_b_pw1, cv_w_dw, cv_b_dw, cv_ln_g, cv_ln_b, cv_w_pw2, cv_b_pw2, xa_norm, xa_mem_norm, xa_w_q, xa_w_kv, xa_w_o, mlp_norm, mlp_w_up, mlp_w_down, final_norm, loss_target, m_dn_norm, m_dn_w_in, m_dn_w_conv, m_dn_a_log, m_dn_dt_bias, m_dn_out_norm, m_dn_w_out, m_cv_norm, m_cv_w_pw1, m_cv_b_pw1, m_cv_w_dw, m_cv_b_dw, m_cv_ln_g, m_cv_ln_b, m_cv_w_pw2, m_cv_b_pw2, m_xa_norm, m_xa_mem_norm, m_xa_w_q, m_xa_w_kv, m_xa_w_o, m_mlp_norm, m_mlp_w_up, m_mlp_w_down, m_final_norm, v_dn_norm, v_dn_w_in, v_dn_w_conv, v_dn_a_log, v_dn_dt_bias, v_dn_out_norm, v_dn_w_out, v_cv_norm, v_cv_w_pw1, v_cv_b_pw1, v_cv_w_dw, v_cv_b_dw, v_cv_ln_g, v_cv_ln_b, v_cv_w_pw2, v_cv_b_pw2, v_xa_norm, v_xa_mem_norm, v_xa_w_q, v_xa_w_kv, v_xa_w_o, v_mlp_norm, v_mlp_w_up, v_mlp_w_down, v_final_norm):
    args = dict(locals())
    wts = {nm: args[nm] for nm in WEIGHTS}
    mom = {nm: args["m_" + nm] for nm in WEIGHTS}
    var = {nm: args["v_" + nm] for nm in WEIGHTS}
    core = lax.axis_index("c").astype(jnp.int32).reshape(1)
    chip = (2 * lax.axis_index("x") + lax.axis_index("y")).astype(jnp.int32)
    def own_slab(got, src):
        return lax.dynamic_update_slice(got, src[None], (chip, 0, 0))

    shard2d = {nm: wts[nm].astype(BF16).reshape(-1, wts[nm].shape[-1]) for nm in BIG}
    first = ["dn_w_in", "dn_w_out"]
    later = [nm for nm in BIG if nm not in first]
    sources = [shard2d[nm] for nm in first] + [_pack(wts, SMALL, F32, SMALL_ROW_MULTIPLE)[0]]
    gathered = [own_slab(got, src) for got, src in zip(gather_shards(sources), sources)]
    stacked = {"dn_w_out": gathered[1].reshape((N_CHIPS,) + SHARD_SHAPES["dn_w_out"])}
    full = {nm: _to_full(nm, t) for nm, t in _unpack(gathered[2], SMALL).items()}
    full["dn_w_in"] = _to_full("dn_w_in", gathered[0].reshape((N_CHIPS,) + SHARD_SHAPES["dn_w_in"]))
    full.update({nm: wts[nm] for nm in REPLICATED})
    later_src = [shard2d[nm] for nm in later]
    g_send, g_recv, later_src, g_lands, started = ici_start(
        "gather_start", "gather", later_src, [(N_CHIPS,) + s.shape for s in later_src])
    full["dn_norm"] = full["dn_norm"] + started[0, 0]

    def rest_weights(after):
        srcs, lands = ici_wait("gather_wait", "gather", g_send, g_recv, later_src, g_lands, after)
        return {nm: own_slab(land, src).reshape((N_CHIPS,) + SHARD_SHAPES[nm])
                for nm, land, src in zip(later, pair_forward(lands), srcs)}

    pending = []

    def on_grads(items):
        tag = "_".join(sorted({str(layer) for _, layer, _ in items}))
        parts = [g.reshape(N_CHIPS, -1, g.shape[-1]) for _, _, g in items]
        theirs = pair_split("pair_split_" + tag, parts)
        pairs = [add_pairs("pair_add_%s%d" % (nm, layer), p, t, core, BF16)
                 for (nm, layer, _), p, t in zip(items, parts, theirs)]
        send, recv, pairs, lands, token = ici_start(
            "scatter_start_" + tag, "scatter", pairs, [(N_CHIPS - 1,) + p.shape[1:] for p in pairs])
        pending.append((tag, items, send, recv, pairs, lands))
        return token

    dh, grads, rep = local_step(x[0], mem[0], loss_target[0], stacked, full, rest_weights, on_grads)

    halves = {}
    last = [("dn_w_in", 0, _to_shards("dn_w_in", grads["dn_w_in"]).astype(BF16)), ("dn_w_out", 0, grads["dn_w_out"][0]),
            ("small", 0, _pack({nm: _to_shards(nm, grads[nm]) for nm in SMALL}, SMALL, F32, SMALL_ROW_MULTIPLE))]
    parts = [g.reshape(N_CHIPS, -1, g.shape[-1]) for _, _, g in last]
    theirs = pair_split("pair_split_last", parts)
    pairs = [add_pairs("pair_add_" + nm, p, t, core, p.dtype) for (nm, _, _), p, t in zip(last, parts, theirs)]
    for (nm, layer, _), p, o in zip(last, pairs, chip_scatter(pairs)):
        halves[nm, layer] = add_four("chip_add_" + nm, p, o, chip.reshape(1))
    for tag, items, send, recv, pairs, lands in pending:
        pairs, lands = ici_wait("scatter_wait_" + tag, "scatter", send, recv, pairs, lands, dh)
        for (nm, layer, _), p, o in zip(items, pairs, lands):
            halves[nm, layer] = add_four("chip_add_%s%d" % (nm, layer), p, o, chip.reshape(1))
    keys = sorted(halves)
    siblings = dict(zip(keys, pair_join([halves[k] for k in keys])))
    south = core[0] == 0
    mine, theirs = halves["small", 0], siblings["small", 0]
    red = _unpack(jnp.concatenate([jnp.where(south, mine, theirs), jnp.where(south, theirs, mine)], axis=0), SMALL)

    rep = all_sum_small(rep)
    red["dn_norm"] = rep[0:1]
    red["dn_a_log"] = rep[1:2, DN_HEADS:2 * DN_HEADS]
    red["dn_dt_bias"] = rep[2:3, DN_HEADS:2 * DN_HEADS]
    red["dn_out_norm"] = rep[3:4, :LANES]
    red["xa_norm"], red["xa_mem_norm"], red["mlp_norm"] = rep[4:6], rep[6:8], rep[8:10]
    red["final_norm"] = rep[10]
    loss = rep[11, 0]

    delta, new_m, new_v = {}, {}, {}
    for nm in WEIGHTS:
        shp = wts[nm].shape
        if nm in BIG:
            layers = range(shp[0])
            res = adamw_halves("adamw_" + nm, _as_2d(wts[nm]), [halves[nm, l] for l in layers],
                               [siblings[nm, l] for l in layers], _as_2d(mom[nm]), _as_2d(var[nm]), core)
            red[nm] = res[0]
            res = res[1:]
        else:
            res = adamw("adamw_" + nm, _as_2d(wts[nm]), _as_2d(red[nm].reshape(shp)), _as_2d(mom[nm]),
                        _as_2d(var[nm]))
        delta[nm], new_m[nm], new_v[nm] = (r.reshape(shp) for r in res)
        red[nm] = red[nm].reshape(shp)

    grad_x = dh[None]
    return (loss, grad_x, *[red[nm] for nm in WEIGHTS], *[delta[nm] for nm in WEIGHTS],
            *[new_m[nm] for nm in WEIGHTS], *[new_v[nm] for nm in WEIGHTS])


def local_step(h0, mem0, target, stacked, full, rest_weights=None, on_grads=None):
    d = h0.shape[1]
    dn_norm, dn_a_log, dn_dt_bias, dn_out_norm = (full[nm] for nm in REPLICATED[:4])
    xa_norm, xa_mem_norm, mlp_norm, final_norm = (full[nm] for nm in REPLICATED[4:])
    inner = DN_HEADS * DN_HEAD_DIM
    w_in = full["dn_w_in"][0]
    w_qkv, w_z = w_in[:, :3 * inner], w_in[:, 3 * inner:4 * inner]
    w_ba = jnp.pad(w_in[:, 4 * inner:], ((0, 0), (0, LANES - 2 * DN_HEADS)))
    w_conv = jnp.pad(full["dn_w_conv"][0], ((0, 8 - DN_CONV), (0, 0)))
    gate = _gate_tile(dn_a_log, dn_dt_bias)
    w_dw = jnp.pad(full["cv_w_dw"][0], ((0, CV_HALO - CV_WIDTH), (0, 0)))

    def sw(nm, layer):
        return Stacked(stacked[nm], "rows" if SHARD_AXIS[nm] == 1 else "cols", layer)

    dn_args = (_row(dn_norm), w_qkv, w_z, w_ba, w_conv, gate, _row(dn_out_norm), sw("dn_w_out", 0))
    h1, dn_saved = dn_fwd(h0, *dn_args)
    if rest_weights is not None:
        stacked = {**stacked, **rest_weights(h1)}
    xa_args = [(_row(xa_norm[l]), _row(xa_mem_norm[l]), sw("xa_w_q", l), sw("xa_w_kv", l), sw("xa_w_o", l))
               for l in range(2)]
    mlp_args = [(_row(mlp_norm[l]), sw("mlp_w_up", l), sw("mlp_w_down", l)) for l in range(2)]
    h2, xa0_saved = xa_fwd("xa0", h1, mem0, *xa_args[0])
    h3, mlp0_saved = mlp_fwd("mlp0", h2, *mlp_args[0])
    cv_args = (_row(full["cv_norm"][0]), sw("cv_w_pw1", 0), full["cv_b_pw1"], w_dw, full["cv_b_dw"],
               full["cv_ln_g"], full["cv_ln_b"], sw("cv_w_pw2", 0), full["cv_b_pw2"])
    h4, cv_saved = cv_fwd(h3, *cv_args)
    h5, xa1_saved = xa_fwd("xa1", h4, mem0, *xa_args[1])
    h6, mlp1_saved = mlp_fwd("mlp1", h5, *mlp_args[1])

    dh, loss_tile, d_final = loss_head("loss_head", h6, _row(final_norm), target)
    grads = {}
    dg_mlp, dg_xa, dg_xa_mem = [None, None], [None, None], [None, None]
    dw_mlp, dw_xa = [None, None], [None, None]
    mlp_names, xa_names = ("mlp_w_up", "mlp_w_down"), ("xa_w_q", "xa_w_kv", "xa_w_o")

    def announce(items):
        return None if on_grads is None else on_grads(items)

    dh, dg_mlp[1], dw_mlp[1] = mlp_bwd("mlp1", dh, h5, *mlp_args[1], mlp1_saved)
    dh, dg_xa[1], dg_xa_mem[1], dw_xa[1] = xa_bwd("xa1", dh, h4, mem0, *xa_args[1], xa1_saved)
    (dh, grads["cv_norm"], dw_pw1, grads["cv_b_pw1"], dw_dw, ln_acc, dw_pw2,
     grads["cv_b_pw2"]) = cv_bwd(dh, h3, cv_args[0], cv_args[1], w_dw, cv_args[5], cv_args[6], cv_args[7], cv_saved)
    after = announce([(nm, 1, g) for nm, g in zip(mlp_names + xa_names, dw_mlp[1] + dw_xa[1])]
                     + [("cv_w_pw1", 0, dw_pw1), ("cv_w_pw2", 0, dw_pw2)])
    dh, dg_mlp[0], dw_mlp[0] = mlp_bwd("mlp0", dh, h2, *mlp_args[0], mlp0_saved, after=after)
    dh, dg_xa[0], dg_xa_mem[0], dw_xa[0] = xa_bwd("xa0", dh, h1, mem0, *xa_args[0], xa0_saved)
    after = announce([(nm, 0, g) for nm, g in zip(mlp_names + xa_names, dw_mlp[0] + dw_xa[0])])
    dh, dg_dn, dw_qkv, dw_z, dw_ba, dw_conv, d_gate, d_out_norm, dw_out = dn_bwd(dh, h0, *dn_args, dn_saved,
                                                                                 after=after)

    grads["dn_w_in"] = jnp.concatenate([dw_qkv, dw_z, dw_ba[:, :2 * DN_HEADS]], axis=1)[None]
    grads["dn_w_conv"] = dw_conv[None, :DN_CONV]
    grads["dn_w_out"], grads["cv_w_pw1"], grads["cv_w_pw2"] = [dw_out], [dw_pw1], [dw_pw2]
    grads["cv_w_dw"] = dw_dw[None, :CV_WIDTH]
    grads["cv_ln_g"], grads["cv_ln_b"], grads["cv_b_dw"] = ln_acc[0:1], ln_acc[1:2], ln_acc[2:3]
    for i, nm in enumerate(mlp_names):
        grads[nm] = [dw_mlp[0][i], dw_mlp[1][i]]
    for i, nm in enumerate(xa_names):
        grads[nm] = [dw_xa[0][i], dw_xa[1][i]]

    rep = jnp.zeros((16, d), F32)
    rep = rep.at[0].set(dg_dn[0])
    rep = rep.at[1, :LANES].set(d_gate[0])
    rep = rep.at[2, :LANES].set(d_gate[1])
    rep = rep.at[3, :LANES].set(d_out_norm[0])
    rep = rep.at[4].set(dg_xa[0][0]).at[5].set(dg_xa[1][0])
    rep = rep.at[6].set(dg_xa_mem[0][0]).at[7].set(dg_xa_mem[1][0])
    rep = rep.at[8].set(dg_mlp[0][0]).at[9].set(dg_mlp[1][0])
    rep = rep.at[10].set(d_final[0])
    rep = rep.at[11, :LANES].set(loss_tile[0])
    return dh, grads, rep
```

```python
import functools

import jax
import jax.numpy as jnp
from jax import lax
from jax.experimental import pallas as pl
from jax.experimental.pallas import tpu as pltpu

F32 = jnp.float32
BF16 = jnp.bfloat16
HIGHEST = lax.Precision.HIGHEST
MESH = pl.DeviceIdType.MESH

D_MODEL = 1024
DN_HEADS = 8
DN_HEAD_DIM = 128
DN_CONV = 4
DN_CHUNK = 64
CV_WIDTH = 31
XA_HEADS = 4
XA_HEAD_DIM = 256
RMS_EPS = 1e-6
LN_EPS = 1e-5
L2_EPS = 1e-6

ADAM_LR = 0.001
ADAM_B1 = 0.9
ADAM_B2 = 0.999
ADAM_EPS = 1e-08
ADAM_WD = 0.01
ADAM_STEP = 10

LANES = 128
ROW_TILE = 512
CONV_ROW_TILE = 256
MM_TILE = 1024
ADAMW_ROW_TILE = 256
DN_ROW_TILE = 256
CHUNK_SHIFT = 6
SOLVE_INTERLEAVE = 8
FWD_HEADS_PER_STEP = 8
BWD_HEADS_PER_STEP = 8
BWD_SCAN_ROWS = 256
DN_HALO = 8
CV_HALO = 32
VMEM_LIMIT = 48 * 1024 * 1024
N_CHIPS = 4
D2D_CHUNK_ROWS = 256


def _cparams(sem):
    return pltpu.CompilerParams(dimension_semantics=sem, vmem_limit_bytes=VMEM_LIMIT)


def _dot(a, b, dims=(((1,), (0,)), ((), ()))):
    return lax.dot_general(a.astype(BF16), b.astype(BF16), dims, preferred_element_type=F32)


def _dot_nt(a, b):
    return _dot(a, b, (((1,), (1,)), ((), ())))


def _dot_tn(a, b):
    return _dot(a, b, (((0,), (0,)), ((), ())))


def _dot_hi(a, b, dims=(((1,), (0,)), ((), ()))):
    return lax.dot_general(a.astype(F32), b.astype(F32), dims, precision=HIGHEST, preferred_element_type=F32)


def _sigmoid(x):
    return 1.0 / (1.0 + jnp.exp(-x))


def _silu(x):
    return x * _sigmoid(x)


def _silu_grad(x):
    s = _sigmoid(x)
    return s * (1.0 + x * (1.0 - s))


def _softplus(x):
    return jnp.maximum(x, 0.0) + jnp.log(1.0 + jnp.exp(-jnp.abs(x)))


def _iota(shape, dim):
    return lax.broadcasted_iota(jnp.int32, shape, dim)


def _lane_col(vals, lane, idx):
    return jnp.sum(jnp.where(lane == idx, vals, 0.0), axis=1, keepdims=True)


def _pick_tile(rows, cap):
    best = rows
    for t in range(16, min(rows, cap) + 1, 16):
        if rows % t == 0:
            best = t
    return best


def _stacked_spec(shape, split, layer, rows, cols, block_index):
    r_shard, c_shard = shape[-2], shape[-1]
    if split == "rows" and rows > r_shard:
        assert rows % r_shard == 0 and c_shard % cols == 0
        chips = rows // r_shard

        def slabs(i, j, kk):
            bi, bj = block_index(i, j, kk)
            return (bi, layer, 0, bj)

        return pl.BlockSpec((chips, None, r_shard, cols), slabs), chips
    assert r_shard % rows == 0 and c_shard % cols == 0
    per_chip = (r_shard // rows) if split == "rows" else (c_shard // cols)

    def index(i, j, kk):
        bi, bj = block_index(i, j, kk)
        if split == "rows":
            return (bi // per_chip, layer, bi % per_chip, bj)
        return (bj // per_chip, layer, bi, bj % per_chip)

    return pl.BlockSpec((None, None, rows, cols), index), 1


def mm(name, a, b, *, ta=False, tb=False, out_dtype=F32, pro=None, epi=None, epi_tiles=(), epi_rows=(),
       tm=MM_TILE, tn=MM_TILE, tk=MM_TILE, b_split=None, b_layer=None, out_split=None, out_layer=None,
       after=None):
    m, k = (a.shape[1], a.shape[0]) if ta else a.shape
    b_rows, b_cols = b.shape[-2], b.shape[-1]
    if b_split == "rows":
        b_rows *= N_CHIPS
    elif b_split == "cols":
        b_cols *= N_CHIPS
    n = b_rows if tb else b_cols
    assert (b_cols if tb else b_rows) == k
    tm, tn, tk = min(tm, m), min(tn, n), min(tk, k)
    if b_split == "cols":
        if tb:
            tk = min(tk, b.shape[-1])
        else:
            tn = min(tn, b.shape[-1])
    if out_split == "cols":
        tn = min(tn, n // N_CHIPS)
    assert m % tm == 0 and n % tn == 0 and k % tk == 0
    nk = k // tk
    a_spec = pl.BlockSpec((tk, tm), lambda i, j, kk: (kk, i)) if ta else pl.BlockSpec((tm, tk), lambda i, j, kk: (i, kk))
    b_block = (tn, tk) if tb else (tk, tn)
    b_index = (lambda i, j, kk: (j, kk)) if tb else (lambda i, j, kk: (kk, j))
    b_chips = o_chips = 1
    if b_split is None:
        b_spec = pl.BlockSpec(b_block, b_index)
    else:
        b_spec, b_chips = _stacked_spec(b.shape, b_split, b_layer, b_block[0], b_block[1], b_index)
    in_specs = [a_spec, b_spec]
    in_specs += [pl.BlockSpec((tm, tn), lambda i, j, kk: (i, j)) for _ in epi_tiles]
    in_specs += [pl.BlockSpec((1, tn), lambda i, j, kk: (0, j)) for _ in epi_rows]
    n_t, n_r = len(epi_tiles), len(epi_rows)
    dims = (((0 if ta else 1,), (1 if tb else 0,)), ((), ()))
    if out_split is None:
        out_shape = jax.ShapeDtypeStruct((m, n), out_dtype)
        out_spec = pl.BlockSpec((tm, tn), lambda i, j, kk: (i, j))
    else:
        shard = (m // N_CHIPS, n) if out_split == "rows" else (m, n // N_CHIPS)
        out_shape = jax.ShapeDtypeStruct((N_CHIPS, out_layer[1]) + shard, out_dtype)
        out_spec, o_chips = _stacked_spec(out_shape.shape, out_split, out_layer[0], tm, tn, lambda i, j, kk: (i, j))
    extra = []
    if after is not None:
        extra = [after]
        in_specs.append(pl.BlockSpec(memory_space=pl.ANY))

    def body(a_ref, b_ref, *rest):
        tiles = rest[:n_t]
        rows = rest[n_t:n_t + n_r]
        rest = rest[n_t + n_r + len(extra):]
        o_ref, acc_ref = rest[0], rest[1]
        kk = pl.program_id(2)

        @pl.when(kk == 0)
        def _():
            acc_ref[...] = jnp.zeros_like(acc_ref)

        av = a_ref[...]
        if pro is not None:
            av = pro(av)
        bv = b_ref[...]
        if b_chips > 1:
            bv = bv.reshape(b_block)
        acc_ref[...] += _dot(av, bv, dims)

        @pl.when(kk == nk - 1)
        def _():
            out = acc_ref[...]
            if epi is not None:
                out = epi(out, *[t[...] for t in tiles], *[r[...] for r in rows])
            out = out.astype(out_dtype)
            o_ref[...] = out.reshape(o_chips, tm // o_chips, tn) if o_chips > 1 else out

    return pl.pallas_call(
        body, name=name, grid=(m // tm, n // tn, nk),
        in_specs=in_specs, out_specs=out_spec, out_shape=out_shape,
        scratch_shapes=[pltpu.VMEM((tm, tn), F32)],
        compiler_params=_cparams(("parallel", "parallel", "arbitrary")),
    )(a, b, *epi_tiles, *epi_rows, *extra)


def row_call(name, body, n_rows, tm, ins, outs, accs=()):
    tm = _pick_tile(n_rows, tm)
    in_specs = []
    for arr, kind in ins:
        if kind == "tile":
            if arr.ndim == 2:
                in_specs.append(pl.BlockSpec((tm, arr.shape[1]), lambda i: (i, 0)))
            else:
                in_specs.append(pl.BlockSpec((arr.shape[0], tm, arr.shape[2]), lambda i: (0, i, 0)))
        elif kind == "full":
            in_specs.append(pl.BlockSpec(arr.shape, functools.partial(lambda i, nd: (0,) * nd, nd=arr.ndim)))
        else:
            where, h = kind
            per = tm // h
            if where == "prev":
                in_specs.append(pl.BlockSpec((h, arr.shape[1]), functools.partial(
                    lambda i, per: (jnp.maximum(i * per - 1, 0), 0), per=per)))
            else:
                last = n_rows // h - 1
                in_specs.append(pl.BlockSpec((h, arr.shape[1]), functools.partial(
                    lambda i, per, last: (jnp.minimum((i + 1) * per, last), 0), per=per, last=last)))
    out_shape, out_specs = [], []
    for shape, dtype in outs:
        out_shape.append(jax.ShapeDtypeStruct(shape, dtype))
        if len(shape) == 2:
            out_specs.append(pl.BlockSpec((tm, shape[1]), lambda i: (i, 0)))
        else:
            out_specs.append(pl.BlockSpec((shape[0], tm, shape[2]), lambda i: (0, i, 0)))
    for shape in accs:
        out_shape.append(jax.ShapeDtypeStruct(shape, F32))
        out_specs.append(pl.BlockSpec(shape, lambda i: (0, 0)))
    n_in, n_out, n_acc = len(ins), len(outs), len(accs)

    def kern(*refs):
        i = pl.program_id(0)
        in_refs = refs[:n_in]
        out_refs = refs[n_in:n_in + n_out]
        acc_refs = refs[n_in + n_out:n_in + n_out + n_acc]
        if n_acc:
            @pl.when(i == 0)
            def _():
                for r in acc_refs:
                    r[...] = jnp.zeros_like(r)
        body(i, in_refs, out_refs, acc_refs)

    res = pl.pallas_call(
        kern, name=name, grid=(n_rows // tm,), in_specs=in_specs, out_specs=out_specs, out_shape=out_shape,
        compiler_params=_cparams(("arbitrary",) if n_acc else ("parallel",)),
    )(*[a for a, _ in ins])
    return list(res)


def _rms_stats(h):
    r = lax.rsqrt(jnp.mean(h * h, axis=-1, keepdims=True) + RMS_EPS)
    return h * r, r


def rms_fwd(name, h, g):
    def body(i, ins, outs, accs):
        xhat, _ = _rms_stats(ins[0][...])
        outs[0][...] = (xhat * ins[1][...]).astype(BF16)

    return row_call(name, body, h.shape[0], ROW_TILE, [(h, "tile"), (g, "full")], [(h.shape, BF16)])[0]


def _rms_bwd_tile(dn, h, g):
    xhat, r = _rms_stats(h)
    dxhat = dn * g
    dh = r * (dxhat - xhat * jnp.mean(dxhat * xhat, axis=-1, keepdims=True))
    dg = jnp.sum(dn * xhat, axis=0, keepdims=True)
    return dh, dg


def rms_bwd(name, dn, h, g, dres):
    def body(i, ins, outs, accs):
        dh, dg = _rms_bwd_tile(ins[0][...].astype(F32), ins[1][...], ins[2][...])
        total = ins[3][...] + dh
        outs[0][...] = total
        outs[1][...] = total.astype(BF16)
        accs[0][...] += dg

    d = h.shape[1]
    out, out16, dg = row_call(name, body, h.shape[0], ROW_TILE,
                              [(dn, "tile"), (h, "tile"), (g, "full"), (dres, "tile")],
                              [(h.shape, F32), (h.shape, BF16)], [(1, d)])
    return out, out16, dg


def mem_norm_bwd(name, dn, mem, g):
    def body(i, ins, outs, accs):
        _, dg = _rms_bwd_tile(ins[0][...].astype(F32), ins[1][...], ins[2][...])
        accs[0][...] += dg

    return row_call(name, body, mem.shape[0], ROW_TILE, [(dn, "tile"), (mem, "tile"), (g, "full")], [],
                    [(1, mem.shape[1])])[0]


def loss_head(name, h, g, target):
    d = h.shape[1]

    def body(i, ins, outs, accs):
        hv, gv = ins[0][...], ins[1][...]
        xhat, _ = _rms_stats(hv)
        err = xhat * gv - ins[2][...]
        dy = err * (1.0 / d)
        dh, dg = _rms_bwd_tile(dy, hv, gv)
        outs[0][...] = dh
        outs[1][...] = dh.astype(BF16)
        accs[0][...] += jnp.full((8, LANES), 0.5 / d, F32) * jnp.sum(err * err)
        accs[1][...] += dg

    dh, dh16, loss, dg = row_call(name, body, h.shape[0], ROW_TILE, [(h, "tile"), (g, "full"), (target, "tile")],
                                  [(h.shape, F32), (h.shape, BF16)], [(8, LANES), (1, d)])
    return dh, dh16, loss, dg


def col_sum(name, x):
    def body(i, ins, outs, accs):
        accs[0][...] += jnp.sum(ins[0][...].astype(F32), axis=0, keepdims=True)

    return row_call(name, body, x.shape[0], ROW_TILE, [(x, "tile")], [], [(1, x.shape[1])])[0]


def _conv_taps(xcat, w_ref, cols, width, halo, tm):
    rows = halo + tm
    acc = None
    for j in range(width):
        s = width - 1 - j
        xs = xcat if s == 0 else pltpu.roll(xcat, s, 0)
        term = xs[halo:rows] * w_ref[j:j + 1, cols]
        acc = term if acc is None else acc + term
    return acc


def _conv_taps_bwd_x(dcat, w_ref, cols, width, halo, tm):
    rows = halo + tm
    acc = None
    for j in range(width):
        s = width - 1 - j
        ds = dcat if s == 0 else pltpu.roll(dcat, rows - s, 0)
        term = ds[0:tm] * w_ref[j:j + 1, cols]
        acc = term if acc is None else acc + term
    return acc


def _conv_taps_bwd_w(dy, xcat, width, halo, tm, wrows):
    rows = halo + tm
    rid = _iota((wrows, dy.shape[1]), 0)
    out = jnp.zeros((wrows, dy.shape[1]), F32)
    for j in range(width):
        s = width - 1 - j
        xs = xcat if s == 0 else pltpu.roll(xcat, s, 0)
        v = jnp.sum(dy * xs[halo:rows], axis=0, keepdims=True)
        out = out + jnp.where(rid == j, v, 0.0)
    return out


def dn_pre(qkv_raw, ba, w_conv, gate):
    s_len = qkv_raw.shape[0]
    tm = min(DN_ROW_TILE, s_len)
    n_blk = qkv_raw.shape[1] // LANES

    def body(i, ins, outs, accs):
        x_ref, xp_ref, ba_ref, w_ref, gate_ref = ins
        qkv_ref, hs_ref = outs

        def blk(cb, carry):
            cols = pl.ds(pl.multiple_of(cb * LANES, LANES), LANES)
            prev = jnp.where(i > 0, xp_ref[:, cols], 0.0)
            xcat = jnp.concatenate([prev, x_ref[:, cols]], axis=0)
            c = _conv_taps(xcat, w_ref, cols, DN_CONV, DN_HALO, tm)
            y = _silu(c)
            rs = lax.rsqrt(jnp.sum(y * y, axis=-1, keepdims=True) + L2_EPS)
            fac = jnp.where(cb < DN_HEADS, DN_HEAD_DIM ** -0.5, 1.0)
            qkv_ref[:, cols] = jnp.where(cb < 2 * DN_HEADS, y * (rs * fac), y)
            return carry

        lax.fori_loop(0, n_blk, blk, 0)

        bav = ba_ref[...]
        beta = _sigmoid(bav)
        g = -jnp.exp(gate_ref[0:1, :]) * _softplus(bav + gate_ref[1:2, :])
        lane = _iota((tm, LANES), 1)
        g = jnp.where((lane >= DN_HEADS) & (lane < 2 * DN_HEADS), g, 0.0)
        r = _iota((tm, tm), 0)
        c = _iota((tm, tm), 1)
        tri = jnp.where((r >= c) & ((r >> CHUNK_SHIFT) == (c >> CHUNK_SHIFT)), 1.0, 0.0)
        gc = _dot_hi(tri, g)
        for h in range(DN_HEADS):
            hs_ref[h] = jnp.where(lane == 0, _lane_col(beta, lane, h),
                                  jnp.where(lane == 1, _lane_col(g, lane, DN_HEADS + h),
                                            jnp.where(lane == 2, _lane_col(gc, lane, DN_HEADS + h), 0.0)))

    return row_call("dn_pre", body, s_len, tm,
                    [(qkv_raw, "tile"), (qkv_raw, ("prev", DN_HALO)), (ba, "tile"), (w_conv, "full"), (gate, "full")],
                    [(qkv_raw.shape, F32), ((DN_HEADS, s_len, LANES), F32)])


def _chunk_masks():
    r = _iota((DN_CHUNK, DN_CHUNK), 0)
    c = _iota((DN_CHUNK, DN_CHUNK), 1)
    return r, c


def _decay_matrix(gc, r, c):
    lane = _iota((DN_CHUNK, LANES), 1)
    a = jnp.where(lane == 0, gc, jnp.where(lane == 1, 1.0, 0.0))
    b = jnp.where(lane == 0, 1.0, jnp.where(lane == 1, -gc, 0.0))
    diff = _dot_hi(a, b, (((1,), (1,)), ((), ())))
    causal = r >= c
    return jnp.where(causal, jnp.exp(jnp.where(causal, diff, 0.0)), 0.0)


def _tri_inverse(lows, r, c):
    eye = jnp.where(r == c, 1.0, 0.0)
    ts = [eye for _ in lows]
    b = 1
    while b < DN_CHUNK:
        shift = b.bit_length()
        sel = ((r >> shift) == (c >> shift)) & ((r & b) != 0) & ((c & b) == 0)
        lms = [jnp.where(sel, low, 0.0) for low in lows]
        if b == 1:
            ts = [t - lm for t, lm in zip(ts, lms)]
        else:
            t_lm = [_dot_hi(t, lm) for t, lm in zip(ts, lms)]
            t_lm_t = [_dot_hi(x, t) for x, t in zip(t_lm, ts)]
            ts = [t - x for t, x in zip(ts, t_lm_t)]
        b *= 2
    return ts


def dn_solve(qkv, hs):
    s_len = qkv.shape[0]
    rb = min(ROW_TILE, s_len)
    n_chunk = rb // DN_CHUNK
    interleave = min(SOLVE_INTERLEAVE, n_chunk)

    def body(k_ref, v_ref, hs_ref, u_ref, w_ref, t_ref):
        r, c = _chunk_masks()

        def group(gi, carry):
            rows = [pl.ds(pl.multiple_of((gi * interleave + j) * DN_CHUNK, DN_CHUNK), DN_CHUNK)
                    for j in range(interleave)]
            k = [k_ref[rw, :] for rw in rows]
            beta = [hs_ref[rw, 0:1] for rw in rows]
            gc = [hs_ref[rw, 2:3] for rw in rows]
            kb = [a * b for a, b in zip(k, beta)]
            decay = [_decay_matrix(g, r, c) for g in gc]
            lows = [jnp.where(r > c, _dot_nt(a, b) * d, 0.0) for a, b, d in zip(kb, k, decay)]
            ts = _tri_inverse(lows, r, c)
            us = [_dot_hi(t, v_ref[rw, :] * b) for t, rw, b in zip(ts, rows, beta)]
            ws = [_dot_hi(t, a * jnp.exp(g)) for t, a, g in zip(ts, kb, gc)]
            for j, rw in enumerate(rows):
                u_ref[rw, :] = us[j]
                w_ref[rw, :] = ws[j].astype(BF16)
                t_ref[rw, :] = ts[j]
            return carry

        lax.fori_loop(0, n_chunk // interleave, group, 0)

    return pl.pallas_call(
        body, name="dn_solve", grid=(DN_HEADS, s_len // rb),
        in_specs=[pl.BlockSpec((rb, LANES), lambda h, i: (i, DN_HEADS + h)),
                  pl.BlockSpec((rb, LANES), lambda h, i: (i, 2 * DN_HEADS + h)),
                  pl.BlockSpec((None, rb, LANES), lambda h, i: (h, i, 0))],
        out_specs=[pl.BlockSpec((rb, LANES), lambda h, i: (i, h)),
                   pl.BlockSpec((rb, LANES), lambda h, i: (i, h)),
                   pl.BlockSpec((None, rb, DN_CHUNK), lambda h, i: (h, i, 0))],
        out_shape=[jax.ShapeDtypeStruct((s_len, DN_HEADS * LANES), F32),
                   jax.ShapeDtypeStruct((s_len, DN_HEADS * LANES), BF16),
                   jax.ShapeDtypeStruct((DN_HEADS, s_len, DN_CHUNK), F32)],
        compiler_params=_cparams(("parallel", "parallel")),
    )(qkv, qkv, hs)


def dn_scan_fwd(qkv, u, w, hs):
    s_len = qkv.shape[0]
    rb = min(ROW_TILE, s_len)
    n_chunk = rb // DN_CHUNK
    total_chunks = s_len // DN_CHUNK

    hps = FWD_HEADS_PER_STEP
    groups = DN_HEADS // hps

    def body(q_ref, k_ref, u_ref, w_ref, hs_ref, o_ref, st_ref, state):
        @pl.when(pl.program_id(1) == 0)
        def _():
            state[...] = jnp.zeros_like(state)

        r, c = _chunk_masks()

        def chunk(n, carry):
            rows = pl.ds(pl.multiple_of(n * DN_CHUNK, DN_CHUNK), DN_CHUNK)
            heads = range(hps)
            cols = [slice(h * LANES, (h + 1) * LANES) for h in heads]
            each = lambda f, *xs: [f(*a) for a in zip(*xs)]
            q = [q_ref[rows, cl] for cl in cols]
            k = [k_ref[rows, cl] for cl in cols]
            gc = [hs_ref[h, rows, 2:3] for h in heads]
            st = [state[h] for h in heads]
            for h in heads:
                st_ref[h, n] = st[h]
            gl = each(lambda g: jnp.min(g, axis=0, keepdims=True), gc)
            decay = each(lambda g: _decay_matrix(g, r, c), gc)
            w_st = [_dot(w_ref[rows, cols[h]], st[h]) for h in heads]
            qk = each(_dot_nt, q, k)
            q_st = each(lambda a, g, s: _dot(a * jnp.exp(g), s), q, gc, st)
            vn = [u_ref[rows, cols[h]] - w_st[h] for h in heads]
            ai_vn = each(lambda a, d, b: _dot(a * d, b), qk, decay, vn)
            kd_vn = each(lambda a, g0, g, b: _dot_tn(a * jnp.exp(g0 - g), b), k, gl, gc, vn)
            for h in heads:
                o_ref[rows, cols[h]] = q_st[h] + ai_vn[h]
                state[h] = st[h] * jnp.exp(gl[h]) + kd_vn[h]
            return carry

        lax.fori_loop(0, n_chunk, chunk, 0)

    wide = hps * LANES
    blk = lambda off: pl.BlockSpec((rb, wide), lambda h, i: (i, off + h))
    return pl.pallas_call(
        body, name="dn_scan_fwd", grid=(groups, s_len // rb),
        in_specs=[blk(0), blk(groups), blk(0), blk(0),
                  pl.BlockSpec((hps, rb, LANES), lambda h, i: (h, i, 0))],
        out_specs=[blk(0),
                   pl.BlockSpec((hps, n_chunk, LANES, LANES), lambda h, i: (h, i, 0, 0))],
        out_shape=[jax.ShapeDtypeStruct((s_len, DN_HEADS * LANES), F32),
                   jax.ShapeDtypeStruct((DN_HEADS, total_chunks, LANES, LANES), F32)],
        scratch_shapes=[pltpu.VMEM((hps, LANES, LANES), F32)],
        compiler_params=_cparams(("parallel", "arbitrary")),
    )(qkv, qkv, u, w, hs)


def dn_scan_bwd(qkv, u, w, t_inv, hs, states, d_o):
    s_len = qkv.shape[0]
    rb = min(BWD_SCAN_ROWS, s_len)
    n_chunk = rb // DN_CHUNK
    n_blk = s_len // rb
    hps = BWD_HEADS_PER_STEP
    groups = DN_HEADS // hps

    def body(q_ref, k_ref, v_ref, u_ref, w_ref, t_ref, hs_ref, st_ref, do_ref,
             dq_ref, dk_ref, dv_ref, dhs_ref, dstate):
        @pl.when(pl.program_id(1) == 0)
        def _():
            dstate[...] = jnp.zeros_like(dstate)

        r, c = _chunk_masks()
        causal = r >= c
        strict = r > c
        lane = _iota((DN_CHUNK, LANES), 1)
        upper = jnp.where(r <= c, 1.0, 0.0)
        last_row = _iota((DN_CHUNK, 1), 0) == DN_CHUNK - 1

        def chunk(m, carry):
            n = n_chunk - 1 - m
            rows = pl.ds(pl.multiple_of(n * DN_CHUNK, DN_CHUNK), DN_CHUNK)
            heads = range(hps)
            cols = [slice(h * LANES, (h + 1) * LANES) for h in heads]
            each = lambda f, *xs: [f(*a) for a in zip(*xs)]
            rsum = lambda x: jnp.sum(x, axis=-1, keepdims=True)
            dims_tn = (((0,), (0,)), ((), ()))
            ones = jnp.ones((DN_CHUNK, LANES), F32)
            q = [q_ref[rows, cl] for cl in cols]
            k = [k_ref[rows, cl] for cl in cols]
            v = [v_ref[rows, cl] for cl in cols]
            uu = [u_ref[rows, cl] for cl in cols]
            ww = [w_ref[rows, cl] for cl in cols]
            do = [do_ref[rows, cl] for cl in cols]
            tt = [t_ref[h, rows, :] for h in heads]
            beta = [hs_ref[h, rows, 0:1] for h in heads]
            gc = [hs_ref[h, rows, 2:3] for h in heads]
            st = [st_ref[h, n] for h in heads]
            dst = [dstate[h] for h in heads]
            gl = each(lambda g: jnp.min(g, axis=0, keepdims=True), gc)
            egc = each(jnp.exp, gc)
            egl = each(jnp.exp, gl)
            ekd = each(lambda a, b: jnp.exp(a - b), gl, gc)
            decay = each(lambda g: _decay_matrix(g, r, c), gc)
            qd = each(jnp.multiply, q, egc)
            kd = each(jnp.multiply, k, ekd)
            kb = each(jnp.multiply, k, beta)
            w_st = each(_dot, ww, st)
            qk = each(_dot_nt, q, k)
            dqd = each(_dot_nt, do, st)
            kd_dst = each(_dot, kd, dst)
            qd_do = each(_dot_tn, qd, do)
            kbk = each(_dot_nt, kb, k)
            vn = each(jnp.subtract, uu, w_st)
            ai = each(jnp.multiply, qk, decay)
            low = each(lambda a, d: jnp.where(strict, a * d, 0.0), kbk, decay)
            dai = each(lambda a, b: jnp.where(causal, _dot_nt(a, b), 0.0), do, vn)
            ai_do = each(_dot_tn, ai, do)
            dkd = each(_dot_nt, vn, dst)
            dvn = each(jnp.add, ai_do, kd_dst)
            dp = each(jnp.multiply, dai, decay)
            dw = each(lambda a, b: -_dot_nt(a, b), dvn, st)
            w_dvn = each(_dot_tn, ww, dvn)
            dp_k = each(_dot, dp, k)
            dp_q = each(_dot_tn, dp, q)
            drhs_u = each(lambda a, b: _dot_hi(a, b, dims_tn), tt, dvn)
            dgl = each(lambda a, b, e: jnp.sum(a * b) * e, dst, st, egl)
            for h in heads:
                dstate[h] = dst[h] * egl[h] + qd_do[h] - w_dvn[h]
            dq = each(lambda a, e, b: a * e + b, dqd, egc, dp_k)
            dk_a = each(lambda a, e, b: a * e + b, dkd, ekd, dp_q)
            rkd = each(lambda a, b: rsum(a * b), dkd, kd)
            drhs_w = each(lambda a, b: _dot_hi(a, b, dims_tn), tt, dw)
            dl_u = each(_dot_nt, drhs_u, uu)
            dl_w = each(_dot_nt, drhs_w, ww)
            dlow = each(lambda a, b: jnp.where(strict, -(a + b), 0.0), dl_u, dl_w)
            dqm = each(jnp.multiply, dlow, decay)
            m_tot = each(lambda a, b, d, e: a * b + d * e, dai, ai, dlow, low)
            dqm_k = each(_dot, dqm, k)
            dk_l = each(_dot_tn, dqm, kb)
            col_sums = each(lambda m: _dot_hi(m, ones, dims_tn), m_tot)
            dkb_w = each(jnp.multiply, drhs_w, egc)
            dkb = each(jnp.add, dkb_w, dqm_k)
            dgc = [rsum(dqd[h] * qd[h]) - rkd[h] + jnp.where(last_row, jnp.sum(rkd[h]) + dgl[h], 0.0)
                   + rsum(m_tot[h]) + rsum(dkb_w[h] * kb[h]) for h in heads]
            dg = each(lambda a, b: _dot_hi(upper, jnp.where(lane == 1, a - b, 0.0)), dgc, col_sums)
            for h in heads:
                dq_ref[rows, cols[h]] = dq[h]
                dk_ref[rows, cols[h]] = dk_a[h] + dk_l[h] + dkb[h] * beta[h]
                dv_ref[rows, cols[h]] = drhs_u[h] * beta[h]
                dbeta = rsum(drhs_u[h] * v[h]) + rsum(dkb[h] * k[h])
                dhs_ref[h, rows, :] = jnp.where(lane == 0, dbeta, dg[h])
            return carry

        lax.fori_loop(0, n_chunk, chunk, 0)

    wide = hps * LANES
    blk = lambda off: pl.BlockSpec((rb, wide), lambda h, i: (n_blk - 1 - i, off + h))
    head = blk(0)
    hs_spec = pl.BlockSpec((hps, rb, LANES), lambda h, i: (h, n_blk - 1 - i, 0))
    full = jax.ShapeDtypeStruct((s_len, DN_HEADS * LANES), F32)
    return pl.pallas_call(
        body, name="dn_scan_bwd", grid=(groups, n_blk),
        in_specs=[blk(0), blk(groups), blk(2 * groups), head, head,
                  pl.BlockSpec((hps, rb, DN_CHUNK), lambda h, i: (h, n_blk - 1 - i, 0)), hs_spec,
                  pl.BlockSpec((hps, n_chunk, LANES, LANES), lambda h, i: (h, n_blk - 1 - i, 0, 0)), head],
        out_specs=[head, head, head, hs_spec],
        out_shape=[full, full, full, jax.ShapeDtypeStruct((DN_HEADS, s_len, LANES), F32)],
        scratch_shapes=[pltpu.VMEM((hps, LANES, LANES), F32)],
        compiler_params=_cparams(("parallel", "arbitrary")),
    )(qkv, qkv, qkv, u, w, t_inv, hs, states, d_o)


def dn_post(o, z, out_norm):
    def body(i, ins, outs, accs):
        gn = ins[2][...]
        for h in range(DN_HEADS):
            cols = slice(h * LANES, (h + 1) * LANES)
            xhat, _ = _rms_stats(ins[0][:, cols])
            outs[0][:, cols] = (xhat * gn * _silu(ins[1][:, cols])).astype(BF16)

    return row_call("dn_post", body, o.shape[0], ROW_TILE, [(o, "tile"), (z, "tile"), (out_norm, "full")],
                    [(o.shape, BF16)])[0]


def dn_post_bwd(d_og, o, z, out_norm):
    def body(i, ins, outs, accs):
        gn = ins[3][...]
        dgn = jnp.zeros((1, LANES), F32)
        for h in range(DN_HEADS):
            cols = slice(h * LANES, (h + 1) * LANES)
            dy, zh = ins[0][:, cols].astype(F32), ins[2][:, cols]
            xhat, r = _rms_stats(ins[1][:, cols])
            sz = _silu(zh)
            dgn = dgn + jnp.sum(dy * xhat * sz, axis=0, keepdims=True)
            outs[1][:, cols] = (dy * xhat * gn * _silu_grad(zh)).astype(BF16)
            dxhat = dy * gn * sz
            outs[0][:, cols] = r * (dxhat - xhat * jnp.mean(dxhat * xhat, axis=-1, keepdims=True))
        accs[0][...] += dgn

    return row_call("dn_post_bwd", body, o.shape[0], ROW_TILE,
                    [(d_og, "tile"), (o, "tile"), (z, "tile"), (out_norm, "full")],
                    [(o.shape, F32), (o.shape, BF16)], [(1, LANES)])


def dn_pre_bwd(dq, dk, dv, dhs, qkv_raw, ba, w_conv, gate):
    s_len = qkv_raw.shape[0]
    tm = min(DN_ROW_TILE, s_len)

    def body(i, ins, outs, accs):
        dq_ref, dk_ref, dv_ref, dhs_ref, x_ref, xp_ref, ba_ref, w_ref, gate_ref = ins
        dc_ref, dba_ref = outs

        def blk(cb, carry):
            cols = pl.ds(pl.multiple_of(cb * LANES, LANES), LANES)
            hcols = pl.ds(pl.multiple_of((cb & (DN_HEADS - 1)) * LANES, LANES), LANES)
            prev = jnp.where(i > 0, xp_ref[:, cols], 0.0)
            xcat = jnp.concatenate([prev, x_ref[:, cols]], axis=0)
            c = _conv_taps(xcat, w_ref, cols, DN_CONV, DN_HALO, tm)
            y = _silu(c)
            dy = jnp.where(cb < DN_HEADS, dq_ref[:, hcols],
                           jnp.where(cb < 2 * DN_HEADS, dk_ref[:, hcols], dv_ref[:, hcols]))
            rs = lax.rsqrt(jnp.sum(y * y, axis=-1, keepdims=True) + L2_EPS)
            fac = jnp.where(cb < DN_HEADS, DN_HEAD_DIM ** -0.5, 1.0)
            nrm = y * rs
            dn = dy * fac
            dy_norm = rs * (dn - nrm * jnp.sum(dn * nrm, axis=-1, keepdims=True))
            dc_ref[:, cols] = jnp.where(cb < 2 * DN_HEADS, dy_norm, dy) * _silu_grad(c)
            return carry

        lax.fori_loop(0, qkv_raw.shape[1] // LANES, blk, 0)

        lane = _iota((tm, LANES), 1)
        dbeta = jnp.zeros((tm, LANES), F32)
        dg = jnp.zeros((tm, LANES), F32)
        for h in range(DN_HEADS):
            dbeta = dbeta + jnp.where(lane == h, dhs_ref[h, :, 0:1], 0.0)
            dg = dg + jnp.where(lane == DN_HEADS + h, dhs_ref[h, :, 1:2], 0.0)
        bav = ba_ref[...]
        beta = _sigmoid(bav)
        ea = jnp.exp(gate_ref[0:1, :])
        pre = bav + gate_ref[1:2, :]
        g = -ea * _softplus(pre)
        da = dg * (-ea) * _sigmoid(pre)
        dba_ref[...] = (dbeta * beta * (1.0 - beta) + da).astype(BF16)
        rid = _iota((8, LANES), 0)
        accs[0][...] += (jnp.where(rid == 0, jnp.sum(dg * g, axis=0, keepdims=True), 0.0)
                         + jnp.where(rid == 1, jnp.sum(da, axis=0, keepdims=True), 0.0))

    return row_call("dn_pre_bwd", body, s_len, tm,
                    [(dq, "tile"), (dk, "tile"), (dv, "tile"), (dhs, "tile"), (qkv_raw, "tile"),
                     (qkv_raw, ("prev", DN_HALO)), (ba, "tile"), (w_conv, "full"), (gate, "full")],
                    [(qkv_raw.shape, F32), (ba.shape, BF16)], [(8, LANES)])


def dn_conv_bwd(dc, qkv_raw, w_conv):
    s_len = dc.shape[0]
    tm = min(DN_ROW_TILE, s_len)
    nt = s_len // tm

    def body(i, ins, outs, accs):
        dc_ref, dn_ref, x_ref, xp_ref, w_ref = ins

        def blk(cb, carry):
            cols = pl.ds(pl.multiple_of(cb * LANES, LANES), LANES)
            dy = dc_ref[:, cols]
            nxt = jnp.where(i < nt - 1, dn_ref[:, cols], 0.0)
            dcat = jnp.concatenate([dy, nxt], axis=0)
            outs[0][:, cols] = _conv_taps_bwd_x(dcat, w_ref, cols, DN_CONV, DN_HALO, tm).astype(BF16)
            prev = jnp.where(i > 0, xp_ref[:, cols], 0.0)
            xcat = jnp.concatenate([prev, x_ref[:, cols]], axis=0)
            accs[0][:, cols] += _conv_taps_bwd_w(dy, xcat, DN_CONV, DN_HALO, tm, 8)
            return carry

        lax.fori_loop(0, dc.shape[1] // LANES, blk, 0)

    return row_call("dn_conv_bwd", body, s_len, tm,
                    [(dc, "tile"), (dc, ("next", DN_HALO)), (qkv_raw, "tile"), (qkv_raw, ("prev", DN_HALO)),
                     (w_conv, "full")],
                    [(dc.shape, BF16)], [(8, dc.shape[1])])


def _glu(u_ref, cols, d):
    return u_ref[:, cols] * _sigmoid(u_ref[:, pl.ds(pl.multiple_of(d + cols.start, LANES), cols.size)])


def cv_core_fwd(u, w_dw, b_dw, ln_g, ln_b):
    s_len, d = u.shape[0], u.shape[1] // 2
    tm = min(CONV_ROW_TILE, s_len)

    def body(i, ins, outs, accs):
        u_ref, up_ref, w_ref, bdw_ref, g_ref, b_ref = ins
        s_ref, c_ref = outs

        def blk(cb, carry):
            cols = pl.ds(pl.multiple_of(cb * LANES, LANES), LANES)
            prev = jnp.where(i > 0, _glu(up_ref, cols, d), 0.0)
            xcat = jnp.concatenate([prev, _glu(u_ref, cols, d)], axis=0)
            c_ref[:, cols] = _conv_taps(xcat, w_ref, cols, CV_WIDTH, CV_HALO, tm) + bdw_ref[:, cols]
            return carry

        lax.fori_loop(0, d // LANES, blk, 0)
        c = c_ref[...]
        mu = jnp.mean(c, axis=-1, keepdims=True)
        xc = c - mu
        rstd = lax.rsqrt(jnp.mean(xc * xc, axis=-1, keepdims=True) + LN_EPS)
        s_ref[...] = _silu(xc * rstd * g_ref[...] + b_ref[...]).astype(BF16)

    return row_call("cv_core_fwd", body, s_len, tm,
                    [(u, "tile"), (u, ("prev", CV_HALO)), (w_dw, "full"), (b_dw, "full"), (ln_g, "full"),
                     (ln_b, "full")],
                    [((s_len, d), BF16), ((s_len, d), F32)])


def cv_ln_bwd(ds, c, ln_g, ln_b):
    def body(i, ins, outs, accs):
        cv, g = ins[1][...], ins[2][...]
        mu = jnp.mean(cv, axis=-1, keepdims=True)
        xc = cv - mu
        rstd = lax.rsqrt(jnp.mean(xc * xc, axis=-1, keepdims=True) + LN_EPS)
        xhat = xc * rstd
        dl = ins[0][...].astype(F32) * _silu_grad(xhat * g + ins[3][...])
        dxhat = dl * g
        dc = rstd * (dxhat - jnp.mean(dxhat, axis=-1, keepdims=True)
                     - xhat * jnp.mean(dxhat * xhat, axis=-1, keepdims=True))
        outs[0][...] = dc
        rid = _iota((8, cv.shape[1]), 0)
        accs[0][...] += (jnp.where(rid == 0, jnp.sum(dl * xhat, axis=0, keepdims=True), 0.0)
                         + jnp.where(rid == 1, jnp.sum(dl, axis=0, keepdims=True), 0.0)
                         + jnp.where(rid == 2, jnp.sum(dc, axis=0, keepdims=True), 0.0))

    return row_call("cv_ln_bwd", body, c.shape[0], ROW_TILE,
                    [(ds, "tile"), (c, "tile"), (ln_g, "full"), (ln_b, "full")], [(c.shape, F32)], [(8, c.shape[1])])


def cv_conv_bwd(dc, u, w_dw):
    s_len, d = dc.shape
    tm = min(CONV_ROW_TILE, s_len)
    nt = s_len // tm

    def body(i, ins, outs, accs):
        dc_ref, dn_ref, u_ref, up_ref, w_ref = ins

        def blk(cb, carry):
            cols = pl.ds(pl.multiple_of(cb * LANES, LANES), LANES)
            gcols = pl.ds(pl.multiple_of(d + cb * LANES, LANES), LANES)
            dy = dc_ref[:, cols]
            nxt = jnp.where(i < nt - 1, dn_ref[:, cols], 0.0)
            dgl = _conv_taps_bwd_x(jnp.concatenate([dy, nxt], axis=0), w_ref, cols, CV_WIDTH, CV_HALO, tm)
            u1, sg = u_ref[:, cols], _sigmoid(u_ref[:, gcols])
            du1 = dgl * sg
            du2 = dgl * u1 * sg * (1.0 - sg)
            outs[0][:, cols] = du1.astype(BF16)
            outs[0][:, gcols] = du2.astype(BF16)
            accs[1][:, cols] += jnp.sum(du1, axis=0, keepdims=True)
            accs[1][:, gcols] += jnp.sum(du2, axis=0, keepdims=True)
            prev = jnp.where(i > 0, _glu(up_ref, cols, d), 0.0)
            xcat = jnp.concatenate([prev, u1 * sg], axis=0)
            accs[0][:, cols] += _conv_taps_bwd_w(dy, xcat, CV_WIDTH, CV_HALO, tm, CV_HALO)
            return carry

        lax.fori_loop(0, d // LANES, blk, 0)

    return row_call("cv_conv_bwd", body, s_len, tm,
                    [(dc, "tile"), (dc, ("next", CV_HALO)), (u, "tile"), (u, ("prev", CV_HALO)), (w_dw, "full")],
                    [(u.shape, BF16)], [(CV_HALO, d), (1, 2 * d)])


def xa_core_fwd(name, q, kv):
    d = q.shape[1]

    def body(i, ins, outs, accs):
        for h in range(XA_HEADS):
            cols = slice(h * XA_HEAD_DIM, (h + 1) * XA_HEAD_DIM)
            vcols = slice(d + h * XA_HEAD_DIM, d + (h + 1) * XA_HEAD_DIM)
            s = _dot_nt(ins[0][:, cols], ins[1][:, cols]) * (XA_HEAD_DIM ** -0.5)
            e = jnp.exp(s - jnp.max(s, axis=-1, keepdims=True))
            p = e / jnp.sum(e, axis=-1, keepdims=True)
            outs[0][:, cols] = _dot(p, ins[1][:, vcols]).astype(BF16)

    return row_call(name, body, q.shape[0], ROW_TILE, [(q, "tile"), (kv, "full")], [(q.shape, BF16)])[0]


def xa_core_bwd(name, d_o, q, kv):
    d = q.shape[1]

    def body(i, ins, outs, accs):
        for h in range(XA_HEADS):
            cols = slice(h * XA_HEAD_DIM, (h + 1) * XA_HEAD_DIM)
            vcols = slice(d + h * XA_HEAD_DIM, d + (h + 1) * XA_HEAD_DIM)
            qh, kh, vh, doh = ins[1][:, cols], ins[2][:, cols], ins[2][:, vcols], ins[0][:, cols]
            s = _dot_nt(qh, kh) * (XA_HEAD_DIM ** -0.5)
            e = jnp.exp(s - jnp.max(s, axis=-1, keepdims=True))
            p = e / jnp.sum(e, axis=-1, keepdims=True)
            dp = _dot_nt(doh, vh)
            ds = p * (dp - jnp.sum(dp * p, axis=-1, keepdims=True)) * (XA_HEAD_DIM ** -0.5)
            outs[0][:, cols] = _dot(ds, kh).astype(BF16)
            accs[0][:, cols] += _dot_tn(ds, qh)
            accs[0][:, vcols] += _dot_tn(p, doh)

    return row_call(name, body, q.shape[0], ROW_TILE, [(d_o, "tile"), (q, "tile"), (kv, "full")],
                    [(q.shape, BF16)], [kv.shape])


def adamw(name, w, g, m, v):
    def body(i, ins, outs, accs):
        wv, gv = ins[0][...], ins[1][...]
        mn = ADAM_B1 * ins[2][...] + (1.0 - ADAM_B1) * gv
        vn = ADAM_B2 * ins[3][...] + (1.0 - ADAM_B2) * jnp.square(gv)
        m_hat = mn / (1.0 - ADAM_B1 ** ADAM_STEP)
        v_hat = vn / (1.0 - ADAM_B2 ** ADAM_STEP)
        outs[0][...] = -ADAM_LR * (m_hat / (jnp.sqrt(v_hat) + ADAM_EPS) + ADAM_WD * wv)
        outs[1][...] = mn
        outs[2][...] = vn

    return row_call(name, body, w.shape[0], ROW_TILE, [(w, "tile"), (g, "tile"), (m, "tile"), (v, "tile")],
                    [(w.shape, F32)] * 3)


def adamw_halves(name, w, g_mine, g_sibling, m, v, core):
    n_layers = len(g_mine)
    rows, cols = w.shape
    half_rows = rows // n_layers // 2
    tm = _pick_tile(half_rows, ADAMW_ROW_TILE)
    per_half = half_rows // tm

    def body(core_ref, w_ref, *rest):
        g_refs = rest[:2 * n_layers]
        m_ref, v_ref, g_out, d_out, m_out, v_out = rest[2 * n_layers:]
        i = pl.program_id(0)
        mine = ((i // per_half) % 2) == core_ref[0]
        layer = i // (2 * per_half)
        gv = jnp.where(mine, g_refs[0][...], g_refs[n_layers][...])
        for l in range(1, n_layers):
            gv = jnp.where(layer == l, jnp.where(mine, g_refs[l][...], g_refs[n_layers + l][...]), gv)
        mn = ADAM_B1 * m_ref[...] + (1.0 - ADAM_B1) * gv
        vn = ADAM_B2 * v_ref[...] + (1.0 - ADAM_B2) * jnp.square(gv)
        m_hat = mn / (1.0 - ADAM_B1 ** ADAM_STEP)
        v_hat = vn / (1.0 - ADAM_B2 ** ADAM_STEP)
        g_out[...] = gv
        d_out[...] = -ADAM_LR * (m_hat / (jnp.sqrt(v_hat) + ADAM_EPS) + ADAM_WD * w_ref[...])
        m_out[...] = mn
        v_out[...] = vn

    whole = pl.BlockSpec((tm, cols), lambda i, core_ref: (i, 0))
    half = pl.BlockSpec((tm, cols), lambda i, core_ref: (i % per_half, 0))
    return pl.pallas_call(
        body, name=name,
        grid_spec=pltpu.PrefetchScalarGridSpec(
            num_scalar_prefetch=1, grid=(2 * per_half * n_layers,),
            in_specs=[whole] + [half] * (2 * n_layers) + [whole, whole], out_specs=[whole] * 4),
        out_shape=[jax.ShapeDtypeStruct(w.shape, F32)] * 4,
        compiler_params=_cparams(("parallel",)),
    )(core, w, *g_mine, *g_sibling, m, v)


HBM_SPEC = pl.BlockSpec(memory_space=pltpu.HBM)


def _position():
    return lax.axis_index("x"), lax.axis_index("y"), lax.axis_index("c")


def _other_chips(x, y):
    return [(1 - x, y), (x, 1 - y), (1 - x, 1 - y)]


def _row_chunks(rows):
    return rows // D2D_CHUNK_ROWS if rows % D2D_CHUNK_ROWS == 0 else 1


def _start_chunked(make, rows):
    k = _row_chunks(rows)
    for i in range(k):
        make(i * (rows // k), rows // k).start()


def gather_shards(packs):
    n = len(packs)

    def body(*refs):
        srcs, outs = refs[:n], refs[n:2 * n]
        send_sems, recv_sems = refs[2 * n:]
        x, y, c = _position()
        me = 2 * x + y
        chips = _other_chips(x, y)
        sibling = (x, y, 1 - c)

        def over_ici(a, j):
            px, py = chips[j]
            rows = srcs[a].shape[0] // 2
            return pltpu.make_async_remote_copy(
                src_ref=srcs[a].at[pl.ds(c * rows, rows), :], dst_ref=outs[a].at[me, pl.ds(c * rows, rows), :],
                send_sem=send_sems.at[a, j], recv_sem=recv_sems.at[a, j], device_id=(px, py, c), device_id_type=MESH)

        def landed(a, j):
            px, py = chips[j]
            rows = srcs[a].shape[0] // 2
            part = outs[a].at[2 * px + py, pl.ds(c * rows, rows), :]
            return pltpu.make_async_remote_copy(
                src_ref=part, dst_ref=part, send_sem=send_sems.at[a, j], recv_sem=recv_sems.at[a, j],
                device_id=(px, py, c), device_id_type=MESH)

        def over_d2d(a, j, cc, off, size):
            px, py = chips[j]
            rows = srcs[a].shape[0] // 2
            part = outs[a].at[2 * px + py, pl.ds(cc * rows + off, size), :]
            return pltpu.make_async_remote_copy(
                src_ref=part, dst_ref=part, send_sem=send_sems.at[a, 3 + j], recv_sem=recv_sems.at[a, 3 + j],
                device_id=sibling, device_id_type=MESH)

        for a in range(n):
            for j in range(3):
                over_ici(a, j).start()
        for a in range(n):
            for j in range(3):
                landed(a, j).wait_recv()
                _start_chunked(functools.partial(over_d2d, a, j, c), srcs[a].shape[0] // 2)
        for a in range(n):
            rows = srcs[a].shape[0] // 2
            for j in range(3):
                over_d2d(a, j, 1 - c, 0, rows).wait_recv()
                over_d2d(a, j, c, 0, rows).wait_send()
                over_ici(a, j).wait_send()

    return pl.pallas_call(
        body, name="gather_shards",
        in_specs=[HBM_SPEC] * n, out_specs=[HBM_SPEC] * n,
        out_shape=[jax.ShapeDtypeStruct((N_CHIPS,) + p.shape, p.dtype) for p in packs],
        scratch_shapes=[pltpu.SemaphoreType.DMA((n, 6)), pltpu.SemaphoreType.DMA((n, 6))],
    )(*packs)


def pair_split(name, packs):
    n = len(packs)

    def body(*refs):
        srcs, outs = refs[:n], refs[n:2 * n]
        send_sems, recv_sems = refs[2 * n:]
        x, y, c = _position()

        def remote(a, off, size):
            rows = srcs[a].shape[1] // 2
            return pltpu.make_async_remote_copy(
                src_ref=srcs[a].at[:, pl.ds((1 - c) * rows + off, size), :],
                dst_ref=outs[a].at[:, pl.ds(off, size), :],
                send_sem=send_sems.at[a], recv_sem=recv_sems.at[a], device_id=(x, y, 1 - c), device_id_type=MESH)

        for a in range(n):
            _start_chunked(functools.partial(remote, a), srcs[a].shape[1] // 2)
        for a in range(n):
            remote(a, 0, srcs[a].shape[1] // 2).wait()

    return pl.pallas_call(
        body, name=name, in_specs=[HBM_SPEC] * n, out_specs=[HBM_SPEC] * n,
        out_shape=[jax.ShapeDtypeStruct((p.shape[0], p.shape[1] // 2, p.shape[2]), p.dtype) for p in packs],
        scratch_shapes=[pltpu.SemaphoreType.DMA((n,)), pltpu.SemaphoreType.DMA((n,))],
    )(*packs)


def chip_scatter(packs):
    n = len(packs)

    def body(*refs):
        srcs, outs = refs[:n], refs[n:2 * n]
        send_sems, recv_sems = refs[2 * n:]
        x, y, c = _position()
        copies = []
        for a in range(n):
            for j, (px, py) in enumerate(_other_chips(x, y)):
                cp = pltpu.make_async_remote_copy(
                    src_ref=srcs[a].at[2 * px + py], dst_ref=outs[a].at[j],
                    send_sem=send_sems.at[a, j], recv_sem=recv_sems.at[a, j],
                    device_id=(px, py, c), device_id_type=MESH)
                cp.start()
                copies.append(cp)
        for cp in copies:
            cp.wait()

    return pl.pallas_call(
        body, name="chip_scatter", in_specs=[HBM_SPEC] * n, out_specs=[HBM_SPEC] * n,
        out_shape=[jax.ShapeDtypeStruct((N_CHIPS - 1,) + p.shape[1:], p.dtype) for p in packs],
        scratch_shapes=[pltpu.SemaphoreType.DMA((n, 3)), pltpu.SemaphoreType.DMA((n, 3))],
    )(*packs)


def pair_join(halves):
    n = len(halves)

    def body(*refs):
        srcs, outs = refs[:n], refs[n:2 * n]
        send_sems, recv_sems = refs[2 * n:]
        x, y, c = _position()

        def remote(a, off, size):
            return pltpu.make_async_remote_copy(
                src_ref=srcs[a].at[pl.ds(off, size), :], dst_ref=outs[a].at[pl.ds(off, size), :],
                send_sem=send_sems.at[a], recv_sem=recv_sems.at[a], device_id=(x, y, 1 - c), device_id_type=MESH)

        for a in range(n):
            _start_chunked(functools.partial(remote, a), srcs[a].shape[0])
        for a in range(n):
            remote(a, 0, srcs[a].shape[0]).wait()

    return pl.pallas_call(
        body, name="pair_join", in_specs=[HBM_SPEC] * n, out_specs=[HBM_SPEC] * n,
        out_shape=[jax.ShapeDtypeStruct(p.shape, p.dtype) for p in halves],
        scratch_shapes=[pltpu.SemaphoreType.DMA((n,)), pltpu.SemaphoreType.DMA((n,))],
    )(*halves)


SEM_SPEC = pl.BlockSpec(memory_space=pltpu.SEMAPHORE)
DATAFLOW = pltpu.SideEffectType.DATAFLOW_SIDE_EFFECTING


def _ici_copy(kind, srcs, lands, send_sems, recv_sems, a, j):
    x, y, c = _position()
    px, py = _other_chips(x, y)[j]
    if kind == "gather":
        rows = srcs[a].shape[0] // 2
        src = srcs[a].at[pl.ds(c * rows, rows), :]
        dst = lands[a].at[2 * x + y, pl.ds(c * rows, rows), :]
    else:
        src = srcs[a].at[2 * px + py]
        dst = lands[a].at[j]
    return pltpu.make_async_remote_copy(src_ref=src, dst_ref=dst, send_sem=send_sems, recv_sem=recv_sems,
                                        device_id=(px, py, c), device_id_type=MESH)


def ici_start(name, kind, srcs, land_shapes):
    n = len(srcs)
    lands = [pltpu.with_memory_space_constraint(lax.empty(shp, s.dtype), pltpu.HBM) for shp, s in zip(land_shapes, srcs)]

    def body(*refs):
        src_refs, land_refs = refs[:n], refs[n:2 * n]
        send_sems, recv_sems = refs[2 * n], refs[2 * n + 1]
        token = refs[-1]
        for a in range(n):
            for j in range(N_CHIPS - 1):
                _ici_copy(kind, src_refs, land_refs, send_sems, recv_sems, a, j).start()
        token[...] = jnp.zeros_like(token)

    sems = pltpu.SemaphoreType.DMA(())
    res = pl.pallas_call(
        body, name=name,
        out_shape=[sems, sems] + [pltpu.HBM(s.shape, s.dtype) for s in srcs]
        + [pltpu.HBM(l.shape, l.dtype) for l in lands] + [jax.ShapeDtypeStruct((8, LANES), F32)],
        in_specs=[HBM_SPEC] * (2 * n),
        out_specs=[SEM_SPEC, SEM_SPEC] + [HBM_SPEC] * (2 * n) + [pl.BlockSpec(memory_space=pltpu.VMEM)],
        input_output_aliases={i: 2 + i for i in range(2 * n)},
        compiler_params=pltpu.CompilerParams(has_side_effects=DATAFLOW),
    )(*[pltpu.with_memory_space_constraint(s, pltpu.HBM) for s in srcs], *lands)
    return res[0], res[1], list(res[2:2 + n]), list(res[2 + n:2 + 2 * n]), res[-1]


def ici_wait(name, kind, send_sems, recv_sems, srcs, lands, after):
    n = len(srcs)

    def body(*refs):
        src_refs, land_refs = refs[:n], refs[n:2 * n]
        send, recv = refs[2 * n], refs[2 * n + 1]
        for a in range(n):
            for j in range(N_CHIPS - 1):
                cp = _ici_copy(kind, src_refs, land_refs, send, recv, a, j)
                cp.wait_send()
                cp.wait_recv()

    res = pl.pallas_call(
        body, name=name,
        out_shape=[pltpu.HBM(s.shape, s.dtype) for s in srcs] + [pltpu.HBM(l.shape, l.dtype) for l in lands],
        in_specs=[HBM_SPEC] * (2 * n) + [SEM_SPEC, SEM_SPEC, pl.BlockSpec(memory_space=pl.ANY)],
        out_specs=[HBM_SPEC] * (2 * n),
        input_output_aliases={i: i for i in range(2 * n)},
        compiler_params=pltpu.CompilerParams(has_side_effects=DATAFLOW),
    )(*srcs, *lands, send_sems, recv_sems, after)
    return list(res[:n]), list(res[n:])


def pair_forward(gathered):
    n = len(gathered)

    def body(*refs):
        outs = refs[n:2 * n]
        send_sems, recv_sems = refs[2 * n:]
        x, y, c = _position()
        chips = _other_chips(x, y)

        def part(a, j, cc, off, size):
            px, py = chips[j]
            rows = outs[a].shape[1] // 2
            ref = outs[a].at[2 * px + py, pl.ds(cc * rows + off, size), :]
            return pltpu.make_async_remote_copy(
                src_ref=ref, dst_ref=ref, send_sem=send_sems.at[a, j], recv_sem=recv_sems.at[a, j],
                device_id=(x, y, 1 - c), device_id_type=MESH)

        for a in range(n):
            for j in range(N_CHIPS - 1):
                _start_chunked(functools.partial(part, a, j, c), outs[a].shape[1] // 2)
        for a in range(n):
            rows = outs[a].shape[1] // 2
            for j in range(N_CHIPS - 1):
                part(a, j, 1 - c, 0, rows).wait_recv()
                part(a, j, c, 0, rows).wait_send()

    return pl.pallas_call(
        body, name="pair_forward", in_specs=[HBM_SPEC] * n, out_specs=[HBM_SPEC] * n,
        out_shape=[jax.ShapeDtypeStruct(g.shape, g.dtype) for g in gathered],
        input_output_aliases={i: i for i in range(n)},
        scratch_shapes=[pltpu.SemaphoreType.DMA((n, N_CHIPS - 1)), pltpu.SemaphoreType.DMA((n, N_CHIPS - 1))],
    )(*gathered)


def all_sum_small(part):
    n_dev = 8
    rows = part.shape[0]

    def body(src, out, buf, send_sems, recv_sems):
        x, y, c = _position()
        me = 4 * x + 2 * y + c
        buf[me] = src[...]
        copies = []
        for k in range(1, n_dev):
            px, py, pc = x ^ ((k >> 2) & 1), y ^ ((k >> 1) & 1), c ^ (k & 1)
            cp = pltpu.make_async_remote_copy(
                src_ref=src, dst_ref=buf.at[me], send_sem=send_sems.at[k - 1], recv_sem=recv_sems.at[k - 1],
                device_id=(px, py, pc), device_id_type=MESH)
            cp.start()
            copies.append(cp)
        for cp in copies:
            cp.wait()
        acc = buf[0]
        for k in range(1, n_dev):
            acc = acc + buf[k]
        out[...] = acc

    return pl.pallas_call(
        body, name="all_sum_small",
        in_specs=[pl.BlockSpec(memory_space=pltpu.VMEM)], out_specs=pl.BlockSpec(memory_space=pltpu.VMEM),
        out_shape=jax.ShapeDtypeStruct(part.shape, F32),
        scratch_shapes=[pltpu.VMEM((n_dev, rows, part.shape[1]), F32),
                        pltpu.SemaphoreType.DMA((n_dev - 1,)), pltpu.SemaphoreType.DMA((n_dev - 1,))],
    )(part)


def add_pairs(name, src, theirs, core, out_dtype):
    slabs, rows, cols = theirs.shape
    tm = _pick_tile(rows, ROW_TILE)
    nb = rows // tm

    def body(core_ref, a_ref, b_ref, o_ref):
        o_ref[...] = (a_ref[...].astype(F32) + b_ref[...].astype(F32)).astype(out_dtype)

    return pl.pallas_call(
        body, name=name,
        grid_spec=pltpu.PrefetchScalarGridSpec(
            num_scalar_prefetch=1, grid=(slabs, nb),
            in_specs=[pl.BlockSpec((None, tm, cols), lambda s, i, core_ref: (s, core_ref[0] * nb + i, 0)),
                      pl.BlockSpec((None, tm, cols), lambda s, i, core_ref: (s, i, 0))],
            out_specs=pl.BlockSpec((None, tm, cols), lambda s, i, core_ref: (s, i, 0))),
        out_shape=jax.ShapeDtypeStruct(theirs.shape, out_dtype),
        compiler_params=_cparams(("parallel", "parallel")),
    )(core, src, theirs)


def add_four(name, src, theirs, chip):
    _, rows, cols = theirs.shape
    tm = _pick_tile(rows, ROW_TILE)

    def body(chip_ref, a_ref, b_ref, o_ref):
        acc = a_ref[...].astype(F32)
        for j in range(N_CHIPS - 1):
            acc = acc + b_ref[j].astype(F32)
        o_ref[...] = acc

    return pl.pallas_call(
        body, name=name,
        grid_spec=pltpu.PrefetchScalarGridSpec(
            num_scalar_prefetch=1, grid=(rows // tm,),
            in_specs=[pl.BlockSpec((None, tm, cols), lambda i, chip_ref: (chip_ref[0], i, 0)),
                      pl.BlockSpec((N_CHIPS - 1, tm, cols), lambda i, chip_ref: (0, i, 0))],
            out_specs=pl.BlockSpec((tm, cols), lambda i, chip_ref: (i, 0))),
        out_shape=jax.ShapeDtypeStruct((rows, cols), F32),
        compiler_params=_cparams(("parallel",)),
    )(chip, src, theirs)


PACK_COLS = 1024
BIG_ROW_MULTIPLE = 512
SMALL_ROW_MULTIPLE = 32
BIG = ["dn_w_in", "dn_w_out", "cv_w_pw1", "cv_w_pw2", "xa_w_q", "xa_w_kv", "xa_w_o", "mlp_w_up", "mlp_w_down"]
SMALL = ["dn_w_conv", "cv_norm", "cv_b_pw1", "cv_w_dw", "cv_b_dw", "cv_ln_g", "cv_ln_b", "cv_b_pw2"]
SHARD_AXIS = {"dn_w_in": 2, "dn_w_conv": 2, "dn_w_out": 1, "cv_norm": 1, "cv_w_pw1": 2, "cv_b_pw1": 1,
              "cv_w_dw": 2, "cv_b_dw": 1, "cv_ln_g": 1, "cv_ln_b": 1, "cv_w_pw2": 1, "cv_b_pw2": 1,
              "xa_w_q": 1, "xa_w_kv": 2, "xa_w_o": 1, "mlp_w_up": 2, "mlp_w_down": 1}
REPLICATED = ["dn_norm", "dn_a_log", "dn_dt_bias", "dn_out_norm", "xa_norm", "xa_mem_norm", "mlp_norm", "final_norm"]


def _pack_rows(size):
    return -(-size // PACK_COLS)


SHARD_SHAPES = {
    "dn_w_in": (1, 1024, 1028), "dn_w_conv": (1, 4, 768), "dn_w_out": (1, 256, 1024), "cv_norm": (1, 256),
    "cv_w_pw1": (1, 1024, 512), "cv_b_pw1": (1, 512), "cv_w_dw": (1, 31, 256), "cv_b_dw": (1, 256),
    "cv_ln_g": (1, 256), "cv_ln_b": (1, 256), "cv_w_pw2": (1, 256, 1024), "cv_b_pw2": (1, 256),
    "xa_w_q": (2, 256, 1024), "xa_w_kv": (2, 1024, 512), "xa_w_o": (2, 256, 1024),
    "mlp_w_up": (2, 1024, 1024), "mlp_w_down": (2, 1024, 1024)}


def _shard_shape(nm):
    return SHARD_SHAPES[nm]


def _pack(tensors, names, dtype, row_multiple):
    pieces = []
    for nm in names:
        t = tensors[nm]
        flat = t.reshape(t.shape[0], -1) if t.ndim > len(_shard_shape(nm)) else t.reshape(1, -1)
        pad = _pack_rows(flat.shape[1]) * PACK_COLS - flat.shape[1]
        pieces.append(jnp.pad(flat.astype(dtype), ((0, 0), (0, pad))))
    cat = jnp.concatenate(pieces, axis=1)
    rows = cat.shape[1] // PACK_COLS
    total = -(-rows // row_multiple) * row_multiple
    cat = jnp.pad(cat, ((0, 0), (0, (total - rows) * PACK_COLS)))
    return cat.reshape(cat.shape[0], total, PACK_COLS)


def _unpack(pack, names):
    lead = pack.shape[:-2]
    flat = pack.reshape(lead + (-1,))
    out, off = {}, 0
    for nm in names:
        shp = _shard_shape(nm)
        size = 1
        for s in shp:
            size *= s
        out[nm] = flat[..., off:off + size].reshape(lead + shp)
        off += _pack_rows(size) * PACK_COLS
    return out


def _to_full(nm, stacked):
    ax = SHARD_AXIS[nm]
    moved = jnp.moveaxis(stacked, 0, ax)
    shp = list(_shard_shape(nm))
    shp[ax] *= N_CHIPS
    return moved.reshape(shp)


def _to_shards(nm, full):
    ax = SHARD_AXIS[nm]
    shp = list(_shard_shape(nm))
    split = full.reshape(shp[:ax] + [N_CHIPS, shp[ax]] + shp[ax + 1:])
    return jnp.moveaxis(split, ax, 0)


def _row(v):
    return v.reshape(1, -1)


class Stacked:
    def __init__(self, arr, split, layer):
        self.arr, self.kw = arr, dict(b_split=split, b_layer=layer)


def _grad_out(split):
    return dict(out_dtype=BF16, out_split=split, out_layer=(0, 1))


def mlp_fwd(tag, h, g, w_up, w_down):
    n = rms_fwd(tag + "_norm", h, g)
    act = mm(tag + "_up", n, w_up.arr, out_dtype=BF16, epi=lambda acc: jnp.square(jnp.maximum(acc, 0.0)), **w_up.kw)
    out = mm(tag + "_down", act, w_down.arr, epi=lambda acc, res: acc + res, epi_tiles=(h,), **w_down.kw)
    return out, (n, act)


def mlp_bwd(tag, dh, h, g, w_up, w_down, saved, after=None):
    n, act = saved
    dh, dh16 = dh
    dup = mm(tag + "_d_act", dh16, w_down.arr, tb=True, out_dtype=BF16, after=after,
             epi=lambda acc, t: acc * (2.0 * jnp.sqrt(t.astype(F32))), epi_tiles=(act,), **w_down.kw)
    dw_down = mm(tag + "_dw_down", act, dh16, ta=True, tk=512, **_grad_out("rows"))
    dn = mm(tag + "_dn", dup, w_up.arr, tb=True, **w_up.kw)
    dw_up = mm(tag + "_dw_up", n, dup, ta=True, tk=512, **_grad_out("cols"))
    dh_in, dh16_in, dg = rms_bwd(tag + "_norm_bwd", dn, h, g, dh)
    return (dh_in, dh16_in), dg, (dw_up, dw_down)


def xa_fwd(tag, h, mem, g, g_mem, w_q, w_kv, w_o):
    n = rms_fwd(tag + "_norm", h, g)
    mem_n = rms_fwd(tag + "_mem_norm", mem, g_mem)
    q = mm(tag + "_q", n, w_q.arr, out_dtype=BF16, **w_q.kw)
    kv = mm(tag + "_kv", mem_n, w_kv.arr, out_dtype=BF16, **w_kv.kw)
    o = xa_core_fwd(tag + "_core", q, kv)
    out = mm(tag + "_o", o, w_o.arr, epi=lambda acc, res: acc + res, epi_tiles=(h,), **w_o.kw)
    return out, (n, mem_n, q, kv, o)


def xa_bwd(tag, dh, h, mem, g, g_mem, w_q, w_kv, w_o, saved):
    n, mem_n, q, kv, o = saved
    dh, dh16 = dh
    d_o = mm(tag + "_d_o", dh16, w_o.arr, tb=True, out_dtype=BF16, **w_o.kw)
    dw_o = mm(tag + "_dw_o", o, dh16, ta=True, tk=512, **_grad_out("rows"))
    dq, dkv = xa_core_bwd(tag + "_core_bwd", d_o, q, kv)
    dn = mm(tag + "_dn", dq, w_q.arr, tb=True, **w_q.kw)
    dw_q = mm(tag + "_dw_q", n, dq, ta=True, tk=512, **_grad_out("rows"))
    dh_in, dh16_in, dg = rms_bwd(tag + "_norm_bwd", dn, h, g, dh)
    dw_kv = mm(tag + "_dw_kv", mem_n, dkv, ta=True, **_grad_out("cols"))
    dmem_n = mm(tag + "_dmem", dkv, w_kv.arr, tb=True, **w_kv.kw)
    dg_mem = mem_norm_bwd(tag + "_mem_norm_bwd", dmem_n, mem, g_mem)
    return (dh_in, dh16_in), dg, dg_mem, (dw_q, dw_kv, dw_o)


def _gate_tile(a_log, dt_bias):
    t = jnp.zeros((8, LANES), F32)
    t = t.at[0, DN_HEADS:2 * DN_HEADS].set(a_log.reshape(-1))
    return t.at[1, DN_HEADS:2 * DN_HEADS].set(dt_bias.reshape(-1))


def dn_fwd(h, g, w_qkv, w_z, w_ba, w_conv, gate, out_norm, w_out):
    n = rms_fwd("dn_norm", h, g)
    qkv_raw = mm("dn_proj_qkv", n, w_qkv)
    z = mm("dn_proj_z", n, w_z)
    ba = mm("dn_proj_ba", n, w_ba)
    qkv, hs = dn_pre(qkv_raw, ba, w_conv, gate)
    u, w, t_inv = dn_solve(qkv, hs)
    o, states = dn_scan_fwd(qkv, u, w, hs)
    og = dn_post(o, z, out_norm)
    out = mm("dn_out", og, w_out.arr, epi=lambda acc, res: acc + res, epi_tiles=(h,), **w_out.kw)
    return out, (n, qkv_raw, z, ba, qkv, hs, u, w, t_inv, o, states, og)


def dn_bwd(dh, h, g, w_qkv, w_z, w_ba, w_conv, gate, out_norm, w_out, saved, after=None):
    n, qkv_raw, z, ba, qkv, hs, u, w, t_inv, o, states, og = saved
    dh, dh16 = dh
    d_og = mm("dn_d_og", dh16, w_out.arr, tb=True, out_dtype=BF16, after=after, **w_out.kw)
    dw_out = mm("dn_dw_out", og, dh16, ta=True, tk=512, **_grad_out("rows"))
    d_o, dz, d_out_norm = dn_post_bwd(d_og, o, z, out_norm)
    dq, dk, dv, dhs = dn_scan_bwd(qkv, u, w, t_inv, hs, states, d_o)
    dc, dba, d_gate = dn_pre_bwd(dq, dk, dv, dhs, qkv_raw, ba, w_conv, gate)
    dqkv_raw, dw_conv = dn_conv_bwd(dc, qkv_raw, w_conv)
    dn = mm("dn_dn_qkv", dqkv_raw, w_qkv, tb=True)
    dn = mm("dn_dn_z", dz, w_z, tb=True, epi=lambda acc, t: acc + t, epi_tiles=(dn,))
    dn = mm("dn_dn_ba", dba, w_ba, tb=True, epi=lambda acc, t: acc + t, epi_tiles=(dn,))
    dw_qkv = mm("dn_dw_qkv", n, dqkv_raw, ta=True, tk=512)
    dw_z = mm("dn_dw_z", n, dz, ta=True, tk=512)
    dw_ba = mm("dn_dw_ba", n, dba, ta=True, tk=512)
    dh_in, _, dg = rms_bwd("dn_norm_bwd", dn, h, g, dh)
    return dh_in, dg, dw_qkv, dw_z, dw_ba, dw_conv, d_gate, d_out_norm, dw_out


def cv_fwd(h, g, w_pw1, b_pw1, w_dw, b_dw, ln_g, ln_b, w_pw2, b_pw2):
    n = rms_fwd("cv_norm", h, g)
    u = mm("cv_pw1", n, w_pw1.arr, epi=lambda acc, b: acc + b, epi_rows=(b_pw1,), **w_pw1.kw)
    s, c = cv_core_fwd(u, w_dw, b_dw, ln_g, ln_b)
    out = mm("cv_pw2", s, w_pw2.arr, epi=lambda acc, res, b: acc + res + b, epi_tiles=(h,), epi_rows=(b_pw2,),
             **w_pw2.kw)
    return out, (n, u, s, c)


def cv_bwd(dh, h, g, w_pw1, w_dw, ln_g, ln_b, w_pw2, saved):
    n, u, s, c = saved
    dh, dh16 = dh
    ds = mm("cv_d_s", dh16, w_pw2.arr, tb=True, out_dtype=BF16, **w_pw2.kw)
    dw_pw2 = mm("cv_dw_pw2", s, dh16, ta=True, tk=512, **_grad_out("rows"))
    db_pw2 = col_sum("cv_db_pw2", dh)
    dc, ln_acc = cv_ln_bwd(ds, c, ln_g, ln_b)
    du, dw_dw, db_pw1 = cv_conv_bwd(dc, u, w_dw)
    dn = mm("cv_dn", du, w_pw1.arr, tb=True, **w_pw1.kw)
    dw_pw1 = mm("cv_dw_pw1", n, du, ta=True, tk=512, **_grad_out("cols"))
    dh_in, dh16_in, dg = rms_bwd("cv_norm_bwd", dn, h, g, dh)
    return (dh_in, dh16_in), dg, dw_pw1, db_pw1, dw_dw, ln_acc, dw_pw2, db_pw2


WEIGHTS = ["dn_norm", "dn_w_in", "dn_w_conv", "dn_a_log", "dn_dt_bias", "dn_out_norm", "dn_w_out", "cv_norm",
           "cv_w_pw1", "cv_b_pw1", "cv_w_dw", "cv_b_dw", "cv_ln_g", "cv_ln_b", "cv_w_pw2", "cv_b_pw2", "xa_norm",
           "xa_mem_norm", "xa_w_q", "xa_w_kv", "xa_w_o", "mlp_norm", "mlp_w_up", "mlp_w_down", "final_norm"]


def _as_2d(t):
    if t.ndim == 1:
        return t.reshape(1, -1)
    return t.reshape(-1, t.shape[-1])


def kernel(x, mem, dn_norm, dn_w_in, dn_w_conv, dn_a_log, dn_dt_bias, dn_out_norm, dn_w_out, cv_norm, cv_w_pw1, cv_b_pw1, cv_w_dw, cv_b_dw, cv_ln_g, cv_ln_b, cv_w_pw2, cv_b_pw2, xa_norm, xa_mem_norm, xa_w_q, xa_w_kv, xa_w_o, mlp_norm, mlp_w_up, mlp_w_down, final_norm, loss_target, m_dn_norm, m_dn_w_in, m_dn_w_conv, m_dn_a_log, m_dn_dt_bias, m_dn_out_norm, m_dn_w_out, m_cv_norm, m_cv_w_pw1, m_cv_b_pw1, m_cv_w_dw, m_cv_b_dw, m_cv_ln_g, m_cv_ln_b, m_cv_w_pw2, m_cv_b_pw2, m_xa_norm, m_xa_mem_norm, m_xa_w_q, m_xa_w_kv, m_xa_w_o, m_mlp_norm, m_mlp_w_up, m_mlp_w_down, m_final_norm, v_dn_norm, v_dn_w_in, v_dn_w_conv, v_dn_a_log, v_dn_dt_bias, v_dn_out_norm, v_dn_w_out, v_cv_norm, v_cv_w_pw1, v_cv_b_pw1, v_cv_w_dw, v_cv_b_dw, v_cv_ln_g, v_cv_ln_b, v_cv_w_pw2, v_cv_b_pw2, v_xa_norm, v_xa_mem_norm, v_xa_w_q, v_xa_w_kv, v_xa_w_o, v_mlp_norm, v_mlp_w_up, v_mlp_w_down, v_final_norm):
    args = dict(locals())
    wts = {nm: args[nm] for nm in WEIGHTS}
    mom = {nm: args["m_" + nm] for nm in WEIGHTS}
    var = {nm: args["v_" + nm] for nm in WEIGHTS}
    core = lax.axis_index("c").astype(jnp.int32).reshape(1)
    chip = (2 * lax.axis_index("x") + lax.axis_index("y")).astype(jnp.int32)
    def own_slab(got, src):
        return lax.dynamic_update_slice(got, src[None], (chip, 0, 0))

    shard2d = {nm: wts[nm].astype(BF16).reshape(-1, wts[nm].shape[-1]) for nm in BIG}
    first = ["dn_w_in", "dn_w_out"]
    later = [nm for nm in BIG if nm not in first]
    sources = [shard2d[nm] for nm in first] + [_pack(wts, SMALL, F32, SMALL_ROW_MULTIPLE)[0]]
    gathered = [own_slab(got, src) for got, src in zip(gather_shards(sources), sources)]
    stacked = {"dn_w_out": gathered[1].reshape((N_CHIPS,) + SHARD_SHAPES["dn_w_out"])}
    full = {nm: _to_full(nm, t) for nm, t in _unpack(gathered[2], SMALL).items()}
    full["dn_w_in"] = _to_full("dn_w_in", gathered[0].reshape((N_CHIPS,) + SHARD_SHAPES["dn_w_in"]))
    full.update({nm: wts[nm] for nm in REPLICATED})
    later_src = [shard2d[nm] for nm in later]
    g_send, g_recv, later_src, g_lands, started = ici_start(
        "gather_start", "gather", later_src, [(N_CHIPS,) + s.shape for s in later_src])
    full["dn_norm"] = full["dn_norm"] + started[0, 0]

    def rest_weights(after):
        srcs, lands = ici_wait("gather_wait", "gather", g_send, g_recv, later_src, g_lands, after)
        return {nm: own_slab(land, src).reshape((N_CHIPS,) + SHARD_SHAPES[nm])
                for nm, land, src in zip(later, pair_forward(lands), srcs)}

    pending = []

    def on_grads(items):
        tag = "_".join(sorted({str(layer) for _, layer, _ in items}))
        parts = [g.reshape(N_CHIPS, -1, g.shape[-1]) for _, _, g in items]
        theirs = pair_split("pair_split_" + tag, parts)
        pairs = [add_pairs("pair_add_%s%d" % (nm, layer), p, t, core, BF16)
                 for (nm, layer, _), p, t in zip(items, parts, theirs)]
        send, recv, pairs, lands, token = ici_start(
            "scatter_start_" + tag, "scatter", pairs, [(N_CHIPS - 1,) + p.shape[1:] for p in pairs])
        pending.append((tag, items, send, recv, pairs, lands))
        return token

    dh, grads, rep = local_step(x[0], mem[0], loss_target[0], stacked, full, rest_weights, on_grads)

    halves = {}
    last = [("dn_w_in", 0, _to_shards("dn_w_in", grads["dn_w_in"]).astype(BF16)), ("dn_w_out", 0, grads["dn_w_out"][0]),
            ("small", 0, _pack({nm: _to_shards(nm, grads[nm]) for nm in SMALL}, SMALL, F32, SMALL_ROW_MULTIPLE))]
    parts = [g.reshape(N_CHIPS, -1, g.shape[-1]) for _, _, g in last]
    theirs = pair_split("pair_split_last", parts)
    pairs = [add_pairs("pair_add_" + nm, p, t, core, p.dtype) for (nm, _, _), p, t in zip(last, parts, theirs)]
    for (nm, layer, _), p, o in zip(last, pairs, chip_scatter(pairs)):
        halves[nm, layer] = add_four("chip_add_" + nm, p, o, chip.reshape(1))
    for tag, items, send, recv, pairs, lands in pending:
        pairs, lands = ici_wait("scatter_wait_" + tag, "scatter", send, recv, pairs, lands, dh)
        for (nm, layer, _), p, o in zip(items, pairs, lands):
            halves[nm, layer] = add_four("chip_add_%s%d" % (nm, layer), p, o, chip.reshape(1))
    keys = sorted(halves)
    siblings = dict(zip(keys, pair_join([halves[k] for k in keys])))
    south = core[0] == 0
    mine, theirs = halves["small", 0], siblings["small", 0]
    red = _unpack(jnp.concatenate([jnp.where(south, mine, theirs), jnp.where(south, theirs, mine)], axis=0), SMALL)

    rep = all_sum_small(rep)
    red["dn_norm"] = rep[0:1]
    red["dn_a_log"] = rep[1:2, DN_HEADS:2 * DN_HEADS]
    red["dn_dt_bias"] = rep[2:3, DN_HEADS:2 * DN_HEADS]
    red["dn_out_norm"] = rep[3:4, :LANES]
    red["xa_norm"], red["xa_mem_norm"], red["mlp_norm"] = rep[4:6], rep[6:8], rep[8:10]
    red["final_norm"] = rep[10]
    loss = rep[11, 0]

    delta, new_m, new_v = {}, {}, {}
    for nm in WEIGHTS:
        shp = wts[nm].shape
        if nm in BIG:
            layers = range(shp[0])
            res = adamw_halves("adamw_" + nm, _as_2d(wts[nm]), [halves[nm, l] for l in layers],
                               [siblings[nm, l] for l in layers], _as_2d(mom[nm]), _as_2d(var[nm]), core)
            red[nm] = res[0]
            res = res[1:]
        else:
            res = adamw("adamw_" + nm, _as_2d(wts[nm]), _as_2d(red[nm].reshape(shp)), _as_2d(mom[nm]),
                        _as_2d(var[nm]))
        delta[nm], new_m[nm], new_v[nm] = (r.reshape(shp) for r in res)
        red[nm] = red[nm].reshape(shp)

    grad_x = dh[None]
    return (loss, grad_x, *[red[nm] for nm in WEIGHTS], *[delta[nm] for nm in WEIGHTS],
            *[new_m[nm] for nm in WEIGHTS], *[new_v[nm] for nm in WEIGHTS])


def local_step(h0, mem0, target, stacked, full, rest_weights=None, on_grads=None):
    d = h0.shape[1]
    dn_norm, dn_a_log, dn_dt_bias, dn_out_norm = (full[nm] for nm in REPLICATED[:4])
    xa_norm, xa_mem_norm, mlp_norm, final_norm = (full[nm] for nm in REPLICATED[4:])
    inner = DN_HEADS * DN_HEAD_DIM
    w_in = full["dn_w_in"][0]
    w_qkv, w_z = w_in[:, :3 * inner], w_in[:, 3 * inner:4 * inner]
    w_ba = jnp.pad(w_in[:, 4 * inner:], ((0, 0), (0, LANES - 2 * DN_HEADS)))
    w_conv = jnp.pad(full["dn_w_conv"][0], ((0, 8 - DN_CONV), (0, 0)))
    gate = _gate_tile(dn_a_log, dn_dt_bias)
    w_dw = jnp.pad(full["cv_w_dw"][0], ((0, CV_HALO - CV_WIDTH), (0, 0)))

    def sw(nm, layer):
        return Stacked(stacked[nm], "rows" if SHARD_AXIS[nm] == 1 else "cols", layer)

    dn_args = (_row(dn_norm), w_qkv, w_z, w_ba, w_conv, gate, _row(dn_out_norm), sw("dn_w_out", 0))
    h1, dn_saved = dn_fwd(h0, *dn_args)
    if rest_weights is not None:
        stacked = {**stacked, **rest_weights(h1)}
    xa_args = [(_row(xa_norm[l]), _row(xa_mem_norm[l]), sw("xa_w_q", l), sw("xa_w_kv", l), sw("xa_w_o", l))
               for l in range(2)]
    mlp_args = [(_row(mlp_norm[l]), sw("mlp_w_up", l), sw("mlp_w_down", l)) for l in range(2)]
    h2, xa0_saved = xa_fwd("xa0", h1, mem0, *xa_args[0])
    h3, mlp0_saved = mlp_fwd("mlp0", h2, *mlp_args[0])
    cv_args = (_row(full["cv_norm"][0]), sw("cv_w_pw1", 0), full["cv_b_pw1"], w_dw, full["cv_b_dw"],
               full["cv_ln_g"], full["cv_ln_b"], sw("cv_w_pw2", 0), full["cv_b_pw2"])
    h4, cv_saved = cv_fwd(h3, *cv_args)
    h5, xa1_saved = xa_fwd("xa1", h4, mem0, *xa_args[1])
    h6, mlp1_saved = mlp_fwd("mlp1", h5, *mlp_args[1])

    dh32, dh16, loss_tile, d_final = loss_head("loss_head", h6, _row(final_norm), target)
    dh = (dh32, dh16)
    grads = {}
    dg_mlp, dg_xa, dg_xa_mem = [None, None], [None, None], [None, None]
    dw_mlp, dw_xa = [None, None], [None, None]
    mlp_names, xa_names = ("mlp_w_up", "mlp_w_down"), ("xa_w_q", "xa_w_kv", "xa_w_o")

    def announce(items):
        return None if on_grads is None else on_grads(items)

    dh, dg_mlp[1], dw_mlp[1] = mlp_bwd("mlp1", dh, h5, *mlp_args[1], mlp1_saved)
    dh, dg_xa[1], dg_xa_mem[1], dw_xa[1] = xa_bwd("xa1", dh, h4, mem0, *xa_args[1], xa1_saved)
    (dh, grads["cv_norm"], dw_pw1, grads["cv_b_pw1"], dw_dw, ln_acc, dw_pw2,
     grads["cv_b_pw2"]) = cv_bwd(dh, h3, cv_args[0], cv_args[1], w_dw, cv_args[5], cv_args[6], cv_args[7], cv_saved)
    after = announce([(nm, 1, g) for nm, g in zip(mlp_names + xa_names, dw_mlp[1] + dw_xa[1])]
                     + [("cv_w_pw1", 0, dw_pw1), ("cv_w_pw2", 0, dw_pw2)])
    dh, dg_mlp[0], dw_mlp[0] = mlp_bwd("mlp0", dh, h2, *mlp_args[0], mlp0_saved, after=after)
    dh, dg_xa[0], dg_xa_mem[0], dw_xa[0] = xa_bwd("xa0", dh, h1, mem0, *xa_args[0], xa0_saved)
    after = announce([(nm, 0, g) for nm, g in zip(mlp_names + xa_names, dw_mlp[0] + dw_xa[0])])
    dh, dg_dn, dw_qkv, dw_z, dw_ba, dw_conv, d_gate, d_out_norm, dw_out = dn_bwd(dh, h0, *dn_args, dn_saved,
                                                                                 after=after)

    grads["dn_w_in"] = jnp.concatenate([dw_qkv, dw_z, dw_ba[:, :2 * DN_HEADS]], axis=1)[None]
    grads["dn_w_conv"] = dw_conv[None, :DN_CONV]
    grads["dn_w_out"], grads["cv_w_pw1"], grads["cv_w_pw2"] = [dw_out], [dw_pw1], [dw_pw2]
    grads["cv_w_dw"] = dw_dw[None, :CV_WIDTH]
    grads["cv_ln_g"], grads["cv_ln_b"], grads["cv_b_dw"] = ln_acc[0:1], ln_acc[1:2], ln_acc[2:3]
    for i, nm in enumerate(mlp_names):
        grads[nm] = [dw_mlp[0][i], dw_mlp[1][i]]
    for i, nm in enumerate(xa_names):
        grads[nm] = [dw_xa[0][i], dw_xa[1][i]]

    rep = jnp.zeros((16, d), F32)
    rep = rep.at[0].set(dg_dn[0])
    rep = rep.at[1, :LANES].set(d_gate[0])
    rep = rep.at[2, :LANES].set(d_gate[1])
    rep = rep.at[3, :LANES].set(d_out_norm[0])
    rep = rep.at[4].set(dg_xa[0][0]).at[5].set(dg_xa[1][0])
    rep = rep.at[6].set(dg_xa_mem[0][0]).at[7].set(dg_xa_mem[1][0])
    rep = rep.at[8].set(dg_mlp[0][0]).at[9].set(dg_mlp[1][0])
    rep = rep.at[10].set(d_final[0])
    rep = rep.at[11, :LANES].set(loss_tile[0])
    return dh, grads, rep
```

```python
import functools

import jax
import jax.numpy as jnp
from jax import lax
from jax.experimental import pallas as pl
from jax.experimental.pallas import tpu as pltpu

F32 = jnp.float32
BF16 = jnp.bfloat16
HIGHEST = lax.Precision.HIGHEST
MESH = pl.DeviceIdType.MESH

D_MODEL = 1024
DN_HEADS = 8
DN_HEAD_DIM = 128
DN_CONV = 4
DN_CHUNK = 64
CV_WIDTH = 31
XA_HEADS = 4
XA_HEAD_DIM = 256
RMS_EPS = 1e-6
LN_EPS = 1e-5
L2_EPS = 1e-6

ADAM_LR = 0.001
ADAM_B1 = 0.9
ADAM_B2 = 0.999
ADAM_EPS = 1e-08
ADAM_WD = 0.01
ADAM_STEP = 10

LANES = 128
ROW_TILE = 512
CONV_ROW_TILE = 256
MM_TILE = 1024
LONG_TILE_K = 2048
ADAMW_ROW_TILE = 256
DN_ROW_TILE = 256
CHUNK_SHIFT = 6
SOLVE_INTERLEAVE = 8
FWD_HEADS_PER_STEP = 8
BWD_HEADS_PER_STEP = 8
BWD_SCAN_ROWS = 256
DN_HALO = 8
CV_HALO = 32
VMEM_LIMIT = 48 * 1024 * 1024
N_CHIPS = 4
D2D_CHUNK_ROWS = 256


def _cparams(sem):
    return pltpu.CompilerParams(dimension_semantics=sem, vmem_limit_bytes=VMEM_LIMIT)


def _dot(a, b, dims=(((1,), (0,)), ((), ()))):
    return lax.dot_general(a.astype(BF16), b.astype(BF16), dims, preferred_element_type=F32)


def _dot_nt(a, b):
    return _dot(a, b, (((1,), (1,)), ((), ())))


def _dot_tn(a, b):
    return _dot(a, b, (((0,), (0,)), ((), ())))


def _dot_hi(a, b, dims=(((1,), (0,)), ((), ()))):
    return lax.dot_general(a.astype(F32), b.astype(F32), dims, precision=HIGHEST, preferred_element_type=F32)


def _dot_x3(a, b, dims=(((1,), (0,)), ((), ()))):
    a_hi, b_hi = a.astype(BF16), b.astype(BF16)
    a_lo = (a - a_hi.astype(F32)).astype(BF16)
    b_lo = (b - b_hi.astype(F32)).astype(BF16)

    def dot(p, q):
        return lax.dot_general(p, q, dims, preferred_element_type=F32)

    return dot(a_hi, b_hi) + (dot(a_hi, b_lo) + dot(a_lo, b_hi))


def _sigmoid(x):
    return 1.0 / (1.0 + jnp.exp(-x))


def _silu(x):
    return x * _sigmoid(x)


def _silu_grad(x):
    s = _sigmoid(x)
    return s * (1.0 + x * (1.0 - s))


def _softplus(x):
    return jnp.maximum(x, 0.0) + jnp.log(1.0 + jnp.exp(-jnp.abs(x)))


def _iota(shape, dim):
    return lax.broadcasted_iota(jnp.int32, shape, dim)


def _lane_col(vals, lane, idx):
    return jnp.sum(jnp.where(lane == idx, vals, 0.0), axis=1, keepdims=True)


def _pick_tile(rows, cap):
    best = rows
    for t in range(16, min(rows, cap) + 1, 16):
        if rows % t == 0:
            best = t
    return best


def _stacked_spec(shape, split, layer, rows, cols, block_index):
    r_shard, c_shard = shape[-2], shape[-1]
    if split == "rows" and rows > r_shard:
        assert rows % r_shard == 0 and c_shard % cols == 0
        chips = rows // r_shard

        def slabs(i, j, kk):
            bi, bj = block_index(i, j, kk)
            return (bi, layer, 0, bj)

        return pl.BlockSpec((chips, None, r_shard, cols), slabs), chips
    assert r_shard % rows == 0 and c_shard % cols == 0
    per_chip = (r_shard // rows) if split == "rows" else (c_shard // cols)

    def index(i, j, kk):
        bi, bj = block_index(i, j, kk)
        if split == "rows":
            return (bi // per_chip, layer, bi % per_chip, bj)
        return (bj // per_chip, layer, bi, bj % per_chip)

    return pl.BlockSpec((None, None, rows, cols), index), 1


def mm(name, a, b, *, ta=False, tb=False, out_dtype=F32, pro=None, epi=None, epi_tiles=(), epi_rows=(),
       tm=MM_TILE, tn=MM_TILE, tk=MM_TILE, b_split=None, b_layer=None, out_split=None, out_layer=None,
       after=None):
    m, k = (a.shape[1], a.shape[0]) if ta else a.shape
    b_rows, b_cols = b.shape[-2], b.shape[-1]
    if b_split == "rows":
        b_rows *= N_CHIPS
    elif b_split == "cols":
        b_cols *= N_CHIPS
    n = b_rows if tb else b_cols
    assert (b_cols if tb else b_rows) == k
    tm, tn, tk = min(tm, m), min(tn, n), min(tk, k)
    if b_split == "cols":
        if tb:
            tk = min(tk, b.shape[-1])
        else:
            tn = min(tn, b.shape[-1])
    if out_split == "cols":
        tn = min(tn, n // N_CHIPS)
    assert m % tm == 0 and n % tn == 0 and k % tk == 0
    nk = k // tk
    a_spec = pl.BlockSpec((tk, tm), lambda i, j, kk: (kk, i)) if ta else pl.BlockSpec((tm, tk), lambda i, j, kk: (i, kk))
    b_block = (tn, tk) if tb else (tk, tn)
    b_index = (lambda i, j, kk: (j, kk)) if tb else (lambda i, j, kk: (kk, j))
    b_chips = o_chips = 1
    if b_split is None:
        b_spec = pl.BlockSpec(b_block, b_index)
    else:
        b_spec, b_chips = _stacked_spec(b.shape, b_split, b_layer, b_block[0], b_block[1], b_index)
    in_specs = [a_spec, b_spec]
    in_specs += [pl.BlockSpec((tm, tn), lambda i, j, kk: (i, j)) for _ in epi_tiles]
    in_specs += [pl.BlockSpec((1, tn), lambda i, j, kk: (0, j)) for _ in epi_rows]
    n_t, n_r = len(epi_tiles), len(epi_rows)
    dims = (((0 if ta else 1,), (1 if tb else 0,)), ((), ()))
    if out_split is None:
        out_shape = jax.ShapeDtypeStruct((m, n), out_dtype)
        out_spec = pl.BlockSpec((tm, tn), lambda i, j, kk: (i, j))
    else:
        shard = (m // N_CHIPS, n) if out_split == "rows" else (m, n // N_CHIPS)
        out_shape = jax.ShapeDtypeStruct((N_CHIPS, out_layer[1]) + shard, out_dtype)
        out_spec, o_chips = _stacked_spec(out_shape.shape, out_split, out_layer[0], tm, tn, lambda i, j, kk: (i, j))
    extra = []
    if after is not None:
        extra = [after]
        in_specs.append(pl.BlockSpec(memory_space=pl.ANY))

    def body(a_ref, b_ref, *rest):
        tiles = rest[:n_t]
        rows = rest[n_t:n_t + n_r]
        rest = rest[n_t + n_r + len(extra):]
        o_ref, acc_ref = rest[0], rest[1]
        kk = pl.program_id(2)

        @pl.when(kk == 0)
        def _():
            acc_ref[...] = jnp.zeros_like(acc_ref)

        av = a_ref[...]
        if pro is not None:
            av = pro(av)
        bv = b_ref[...]
        if b_chips > 1:
            bv = bv.reshape(b_block)
        acc_ref[...] += _dot(av, bv, dims)

        @pl.when(kk == nk - 1)
        def _():
            out = acc_ref[...]
            if epi is not None:
                out = epi(out, *[t[...] for t in tiles], *[r[...] for r in rows])
            out = out.astype(out_dtype)
            o_ref[...] = out.reshape(o_chips, tm // o_chips, tn) if o_chips > 1 else out

    return pl.pallas_call(
        body, name=name, grid=(m // tm, n // tn, nk),
        in_specs=in_specs, out_specs=out_spec, out_shape=out_shape,
        scratch_shapes=[pltpu.VMEM((tm, tn), F32)],
        compiler_params=_cparams(("parallel", "parallel", "arbitrary")),
    )(a, b, *epi_tiles, *epi_rows, *extra)


def row_call(name, body, n_rows, tm, ins, outs, accs=()):
    tm = _pick_tile(n_rows, tm)
    in_specs = []
    for arr, kind in ins:
        if kind == "tile":
            if arr.ndim == 2:
                in_specs.append(pl.BlockSpec((tm, arr.shape[1]), lambda i: (i, 0)))
            else:
                in_specs.append(pl.BlockSpec((arr.shape[0], tm, arr.shape[2]), lambda i: (0, i, 0)))
        elif kind == "full":
            in_specs.append(pl.BlockSpec(arr.shape, functools.partial(lambda i, nd: (0,) * nd, nd=arr.ndim)))
        else:
            where, h = kind
            per = tm // h
            if where == "prev":
                in_specs.append(pl.BlockSpec((h, arr.shape[1]), functools.partial(
                    lambda i, per: (jnp.maximum(i * per - 1, 0), 0), per=per)))
            else:
                last = n_rows // h - 1
                in_specs.append(pl.BlockSpec((h, arr.shape[1]), functools.partial(
                    lambda i, per, last: (jnp.minimum((i + 1) * per, last), 0), per=per, last=last)))
    out_shape, out_specs = [], []
    for shape, dtype in outs:
        out_shape.append(jax.ShapeDtypeStruct(shape, dtype))
        if len(shape) == 2:
            out_specs.append(pl.BlockSpec((tm, shape[1]), lambda i: (i, 0)))
        else:
            out_specs.append(pl.BlockSpec((shape[0], tm, shape[2]), lambda i: (0, i, 0)))
    for shape in accs:
        out_shape.append(jax.ShapeDtypeStruct(shape, F32))
        out_specs.append(pl.BlockSpec(shape, lambda i: (0, 0)))
    n_in, n_out, n_acc = len(ins), len(outs), len(accs)

    def kern(*refs):
        i = pl.program_id(0)
        in_refs = refs[:n_in]
        out_refs = refs[n_in:n_in + n_out]
        acc_refs = refs[n_in + n_out:n_in + n_out + n_acc]
        if n_acc:
            @pl.when(i == 0)
            def _():
                for r in acc_refs:
                    r[...] = jnp.zeros_like(r)
        body(i, in_refs, out_refs, acc_refs)

    res = pl.pallas_call(
        kern, name=name, grid=(n_rows // tm,), in_specs=in_specs, out_specs=out_specs, out_shape=out_shape,
        compiler_params=_cparams(("arbitrary",) if n_acc else ("parallel",)),
    )(*[a for a, _ in ins])
    return list(res)


def _rms_stats(h):
    r = lax.rsqrt(jnp.mean(h * h, axis=-1, keepdims=True) + RMS_EPS)
    return h * r, r


def rms_fwd(name, h, g):
    def body(i, ins, outs, accs):
        xhat, _ = _rms_stats(ins[0][...])
        outs[0][...] = (xhat * ins[1][...]).astype(BF16)

    return row_call(name, body, h.shape[0], ROW_TILE, [(h, "tile"), (g, "full")], [(h.shape, BF16)])[0]


def _rms_bwd_tile(dn, h, g):
    xhat, r = _rms_stats(h)
    dxhat = dn * g
    dh = r * (dxhat - xhat * jnp.mean(dxhat * xhat, axis=-1, keepdims=True))
    dg = jnp.sum(dn * xhat, axis=0, keepdims=True)
    return dh, dg


def rms_bwd(name, dn, h, g, dres):
    def body(i, ins, outs, accs):
        dh, dg = _rms_bwd_tile(ins[0][...].astype(F32), ins[1][...], ins[2][...])
        total = ins[3][...] + dh
        outs[0][...] = total
        outs[1][...] = total.astype(BF16)
        accs[0][...] += dg

    d = h.shape[1]
    out, out16, dg = row_call(name, body, h.shape[0], ROW_TILE,
                              [(dn, "tile"), (h, "tile"), (g, "full"), (dres, "tile")],
                              [(h.shape, F32), (h.shape, BF16)], [(1, d)])
    return out, out16, dg


def mem_norm_bwd(name, dn, mem, g):
    def body(i, ins, outs, accs):
        _, dg = _rms_bwd_tile(ins[0][...].astype(F32), ins[1][...], ins[2][...])
        accs[0][...] += dg

    return row_call(name, body, mem.shape[0], ROW_TILE, [(dn, "tile"), (mem, "tile"), (g, "full")], [],
                    [(1, mem.shape[1])])[0]


def loss_head(name, h, g, target):
    d = h.shape[1]

    def body(i, ins, outs, accs):
        hv, gv = ins[0][...], ins[1][...]
        xhat, _ = _rms_stats(hv)
        err = xhat * gv - ins[2][...]
        dy = err * (1.0 / d)
        dh, dg = _rms_bwd_tile(dy, hv, gv)
        outs[0][...] = dh
        outs[1][...] = dh.astype(BF16)
        accs[0][...] += jnp.full((8, LANES), 0.5 / d, F32) * jnp.sum(err * err)
        accs[1][...] += dg

    dh, dh16, loss, dg = row_call(name, body, h.shape[0], ROW_TILE, [(h, "tile"), (g, "full"), (target, "tile")],
                                  [(h.shape, F32), (h.shape, BF16)], [(8, LANES), (1, d)])
    return dh, dh16, loss, dg


def col_sum(name, x):
    def body(i, ins, outs, accs):
        accs[0][...] += jnp.sum(ins[0][...].astype(F32), axis=0, keepdims=True)

    return row_call(name, body, x.shape[0], ROW_TILE, [(x, "tile")], [], [(1, x.shape[1])])[0]


def _conv_taps(xcat, w_ref, cols, width, halo, tm):
    rows = halo + tm
    acc = None
    for j in range(width):
        s = width - 1 - j
        xs = xcat if s == 0 else pltpu.roll(xcat, s, 0)
        term = xs[halo:rows] * w_ref[j:j + 1, cols]
        acc = term if acc is None else acc + term
    return acc


def _conv_taps_bwd_x(dcat, w_ref, cols, width, halo, tm):
    rows = halo + tm
    acc = None
    for j in range(width):
        s = width - 1 - j
        ds = dcat if s == 0 else pltpu.roll(dcat, rows - s, 0)
        term = ds[0:tm] * w_ref[j:j + 1, cols]
        acc = term if acc is None else acc + term
    return acc


def _conv_taps_bwd_w(dy, xcat, width, halo, tm, wrows):
    rows = halo + tm
    rid = _iota((wrows, dy.shape[1]), 0)
    out = jnp.zeros((wrows, dy.shape[1]), F32)
    for j in range(width):
        s = width - 1 - j
        xs = xcat if s == 0 else pltpu.roll(xcat, s, 0)
        v = jnp.sum(dy * xs[halo:rows], axis=0, keepdims=True)
        out = out + jnp.where(rid == j, v, 0.0)
    return out


def dn_pre(qkv_raw, ba, w_conv, gate):
    s_len = qkv_raw.shape[0]
    tm = min(DN_ROW_TILE, s_len)
    n_blk = qkv_raw.shape[1] // LANES

    def body(i, ins, outs, accs):
        x_ref, xp_ref, ba_ref, w_ref, gate_ref = ins
        qkv_ref, hs_ref = outs

        def blk(cb, carry):
            cols = pl.ds(pl.multiple_of(cb * LANES, LANES), LANES)
            prev = jnp.where(i > 0, xp_ref[:, cols], 0.0)
            xcat = jnp.concatenate([prev, x_ref[:, cols]], axis=0)
            c = _conv_taps(xcat, w_ref, cols, DN_CONV, DN_HALO, tm)
            y = _silu(c)
            rs = lax.rsqrt(jnp.sum(y * y, axis=-1, keepdims=True) + L2_EPS)
            fac = jnp.where(cb < DN_HEADS, DN_HEAD_DIM ** -0.5, 1.0)
            qkv_ref[:, cols] = jnp.where(cb < 2 * DN_HEADS, y * (rs * fac), y)
            return carry

        lax.fori_loop(0, n_blk, blk, 0)

        bav = ba_ref[...]
        beta = _sigmoid(bav)
        g = -jnp.exp(gate_ref[0:1, :]) * _softplus(bav + gate_ref[1:2, :])
        lane = _iota((tm, LANES), 1)
        g = jnp.where((lane >= DN_HEADS) & (lane < 2 * DN_HEADS), g, 0.0)
        r = _iota((tm, tm), 0)
        c = _iota((tm, tm), 1)
        tri = jnp.where((r >= c) & ((r >> CHUNK_SHIFT) == (c >> CHUNK_SHIFT)), 1.0, 0.0)
        gc = _dot_hi(tri, g)
        for h in range(DN_HEADS):
            hs_ref[h] = jnp.where(lane == 0, _lane_col(beta, lane, h),
                                  jnp.where(lane == 1, _lane_col(g, lane, DN_HEADS + h),
                                            jnp.where(lane == 2, _lane_col(gc, lane, DN_HEADS + h), 0.0)))

    return row_call("dn_pre", body, s_len, tm,
                    [(qkv_raw, "tile"), (qkv_raw, ("prev", DN_HALO)), (ba, "tile"), (w_conv, "full"), (gate, "full")],
                    [(qkv_raw.shape, F32), ((DN_HEADS, s_len, LANES), F32)])


def _chunk_masks():
    r = _iota((DN_CHUNK, DN_CHUNK), 0)
    c = _iota((DN_CHUNK, DN_CHUNK), 1)
    return r, c


def _decay_matrix(gc, r, c):
    lane = _iota((DN_CHUNK, LANES), 1)
    a = jnp.where(lane == 0, gc, jnp.where(lane == 1, 1.0, 0.0))
    b = jnp.where(lane == 0, 1.0, jnp.where(lane == 1, -gc, 0.0))
    diff = _dot_hi(a, b, (((1,), (1,)), ((), ())))
    causal = r >= c
    return jnp.where(causal, jnp.exp(jnp.where(causal, diff, 0.0)), 0.0)


def _tri_inverse(lows, r, c):
    eye = jnp.where(r == c, 1.0, 0.0)
    ts = [eye for _ in lows]
    b = 1
    while b < DN_CHUNK:
        shift = b.bit_length()
        sel = ((r >> shift) == (c >> shift)) & ((r & b) != 0) & ((c & b) == 0)
        lms = [jnp.where(sel, low, 0.0) for low in lows]
        if b == 1:
            ts = [t - lm for t, lm in zip(ts, lms)]
        else:
            t_lm = [_dot_x3(t, lm) for t, lm in zip(ts, lms)]
            t_lm_t = [_dot_x3(x, t) for x, t in zip(t_lm, ts)]
            ts = [t - x for t, x in zip(ts, t_lm_t)]
        b *= 2
    return ts


def dn_solve(qkv, hs):
    s_len = qkv.shape[0]
    rb = min(ROW_TILE, s_len)
    n_chunk = rb // DN_CHUNK
    interleave = min(SOLVE_INTERLEAVE, n_chunk)

    def body(k_ref, v_ref, hs_ref, u_ref, w_ref, t_ref):
        r, c = _chunk_masks()

        def group(gi, carry):
            rows = [pl.ds(pl.multiple_of((gi * interleave + j) * DN_CHUNK, DN_CHUNK), DN_CHUNK)
                    for j in range(interleave)]
            k = [k_ref[rw, :] for rw in rows]
            beta = [hs_ref[rw, 0:1] for rw in rows]
            gc = [hs_ref[rw, 2:3] for rw in rows]
            kb = [a * b for a, b in zip(k, beta)]
            decay = [_decay_matrix(g, r, c) for g in gc]
            lows = [jnp.where(r > c, _dot_nt(a, b) * d, 0.0) for a, b, d in zip(kb, k, decay)]
            ts = _tri_inverse(lows, r, c)
            us = [_dot_x3(t, v_ref[rw, :] * b) for t, rw, b in zip(ts, rows, beta)]
            ws = [_dot_x3(t, a * jnp.exp(g)) for t, a, g in zip(ts, kb, gc)]
            for j, rw in enumerate(rows):
                u_ref[rw, :] = us[j]
                w_ref[rw, :] = ws[j].astype(BF16)
                t_ref[rw, :] = ts[j]
            return carry

        lax.fori_loop(0, n_chunk // interleave, group, 0)

    return pl.pallas_call(
        body, name="dn_solve", grid=(DN_HEADS, s_len // rb),
        in_specs=[pl.BlockSpec((rb, LANES), lambda h, i: (i, DN_HEADS + h)),
                  pl.BlockSpec((rb, LANES), lambda h, i: (i, 2 * DN_HEADS + h)),
                  pl.BlockSpec((None, rb, LANES), lambda h, i: (h, i, 0))],
        out_specs=[pl.BlockSpec((rb, LANES), lambda h, i: (i, h)),
                   pl.BlockSpec((rb, LANES), lambda h, i: (i, h)),
                   pl.BlockSpec((None, rb, DN_CHUNK), lambda h, i: (h, i, 0))],
        out_shape=[jax.ShapeDtypeStruct((s_len, DN_HEADS * LANES), F32),
                   jax.ShapeDtypeStruct((s_len, DN_HEADS * LANES), BF16),
                   jax.ShapeDtypeStruct((DN_HEADS, s_len, DN_CHUNK), F32)],
        compiler_params=_cparams(("parallel", "parallel")),
    )(qkv, qkv, hs)


def dn_scan_fwd(qkv, u, w, hs):
    s_len = qkv.shape[0]
    rb = min(ROW_TILE, s_len)
    n_chunk = rb // DN_CHUNK
    total_chunks = s_len // DN_CHUNK

    hps = FWD_HEADS_PER_STEP
    groups = DN_HEADS // hps

    def body(q_ref, k_ref, u_ref, w_ref, hs_ref, o_ref, st_ref, state):
        @pl.when(pl.program_id(1) == 0)
        def _():
            state[...] = jnp.zeros_like(state)

        r, c = _chunk_masks()

        def chunk(n, carry):
            rows = pl.ds(pl.multiple_of(n * DN_CHUNK, DN_CHUNK), DN_CHUNK)
            heads = range(hps)
            cols = [slice(h * LANES, (h + 1) * LANES) for h in heads]
            each = lambda f, *xs: [f(*a) for a in zip(*xs)]
            q = [q_ref[rows, cl] for cl in cols]
            k = [k_ref[rows, cl] for cl in cols]
            gc = [hs_ref[h, rows, 2:3] for h in heads]
            st = [state[h] for h in heads]
            for h in heads:
                st_ref[h, n] = st[h]
            gl = each(lambda g: jnp.min(g, axis=0, keepdims=True), gc)
            decay = each(lambda g: _decay_matrix(g, r, c), gc)
            w_st = [_dot(w_ref[rows, cols[h]], st[h]) for h in heads]
            qk = each(_dot_nt, q, k)
            q_st = each(lambda a, g, s: _dot(a * jnp.exp(g), s), q, gc, st)
            vn = [u_ref[rows, cols[h]] - w_st[h] for h in heads]
            ai_vn = each(lambda a, d, b: _dot(a * d, b), qk, decay, vn)
            kd_vn = each(lambda a, g0, g, b: _dot_tn(a * jnp.exp(g0 - g), b), k, gl, gc, vn)
            for h in heads:
                o_ref[rows, cols[h]] = q_st[h] + ai_vn[h]
                state[h] = st[h] * jnp.exp(gl[h]) + kd_vn[h]
            return carry

        lax.fori_loop(0, n_chunk, chunk, 0)

    wide = hps * LANES
    blk = lambda off: pl.BlockSpec((rb, wide), lambda h, i: (i, off + h))
    return pl.pallas_call(
        body, name="dn_scan_fwd", grid=(groups, s_len // rb),
        in_specs=[blk(0), blk(groups), blk(0), blk(0),
                  pl.BlockSpec((hps, rb, LANES), lambda h, i: (h, i, 0))],
        out_specs=[blk(0),
                   pl.BlockSpec((hps, n_chunk, LANES, LANES), lambda h, i: (h, i, 0, 0))],
        out_shape=[jax.ShapeDtypeStruct((s_len, DN_HEADS * LANES), F32),
                   jax.ShapeDtypeStruct((DN_HEADS, total_chunks, LANES, LANES), F32)],
        scratch_shapes=[pltpu.VMEM((hps, LANES, LANES), F32)],
        compiler_params=_cparams(("parallel", "arbitrary")),
    )(qkv, qkv, u, w, hs)


def dn_scan_bwd(qkv, u, w, t_inv, hs, states, d_o):
    s_len = qkv.shape[0]
    rb = min(BWD_SCAN_ROWS, s_len)
    n_chunk = rb // DN_CHUNK
    n_blk = s_len // rb
    hps = BWD_HEADS_PER_STEP
    groups = DN_HEADS // hps

    def body(q_ref, k_ref, v_ref, u_ref, w_ref, t_ref, hs_ref, st_ref, do_ref,
             dq_ref, dk_ref, dv_ref, dhs_ref, dstate):
        @pl.when(pl.program_id(1) == 0)
        def _():
            dstate[...] = jnp.zeros_like(dstate)

        r, c = _chunk_masks()
        causal = r >= c
        strict = r > c
        lane = _iota((DN_CHUNK, LANES), 1)
        upper = jnp.where(r <= c, 1.0, 0.0)
        last_row = _iota((DN_CHUNK, 1), 0) == DN_CHUNK - 1

        def chunk(m, carry):
            n = n_chunk - 1 - m
            rows = pl.ds(pl.multiple_of(n * DN_CHUNK, DN_CHUNK), DN_CHUNK)
            heads = range(hps)
            cols = [slice(h * LANES, (h + 1) * LANES) for h in heads]
            each = lambda f, *xs: [f(*a) for a in zip(*xs)]
            rsum = lambda x: jnp.sum(x, axis=-1, keepdims=True)
            dims_tn = (((0,), (0,)), ((), ()))
            ones = jnp.ones((DN_CHUNK, LANES), F32)
            q = [q_ref[rows, cl] for cl in cols]
            k = [k_ref[rows, cl] for cl in cols]
            v = [v_ref[rows, cl] for cl in cols]
            uu = [u_ref[rows, cl] for cl in cols]
            ww = [w_ref[rows, cl] for cl in cols]
            do = [do_ref[rows, cl] for cl in cols]
            tt = [t_ref[h, rows, :] for h in heads]
            beta = [hs_ref[h, rows, 0:1] for h in heads]
            gc = [hs_ref[h, rows, 2:3] for h in heads]
            st = [st_ref[h, n] for h in heads]
            dst = [dstate[h] for h in heads]
            gl = each(lambda g: jnp.min(g, axis=0, keepdims=True), gc)
            egc = each(jnp.exp, gc)
            egl = each(jnp.exp, gl)
            ekd = each(lambda a, b: jnp.exp(a - b), gl, gc)
            decay = each(lambda g: _decay_matrix(g, r, c), gc)
            qd = each(jnp.multiply, q, egc)
            kd = each(jnp.multiply, k, ekd)
            kb = each(jnp.multiply, k, beta)
            w_st = each(_dot, ww, st)
            qk = each(_dot_nt, q, k)
            dqd = each(_dot_nt, do, st)
            kd_dst = each(_dot, kd, dst)
            qd_do = each(_dot_tn, qd, do)
            kbk = each(_dot_nt, kb, k)
            vn = each(jnp.subtract, uu, w_st)
            ai = each(jnp.multiply, qk, decay)
            low = each(lambda a, d: jnp.where(strict, a * d, 0.0), kbk, decay)
            dai = each(lambda a, b: jnp.where(causal, _dot_nt(a, b), 0.0), do, vn)
            ai_do = each(_dot_tn, ai, do)
            dkd = each(_dot_nt, vn, dst)
            dvn = each(jnp.add, ai_do, kd_dst)
            dp = each(jnp.multiply, dai, decay)
            dw = each(lambda a, b: -_dot_nt(a, b), dvn, st)
            w_dvn = each(_dot_tn, ww, dvn)
            dp_k = each(_dot, dp, k)
            dp_q = each(_dot_tn, dp, q)
            drhs_u = each(lambda a, b: _dot_x3(a, b, dims_tn), tt, dvn)
            dgl = each(lambda a, b, e: jnp.sum(a * b) * e, dst, st, egl)
            for h in heads:
                dstate[h] = dst[h] * egl[h] + qd_do[h] - w_dvn[h]
            dq = each(lambda a, e, b: a * e + b, dqd, egc, dp_k)
            dk_a = each(lambda a, e, b: a * e + b, dkd, ekd, dp_q)
            rkd = each(lambda a, b: rsum(a * b), dkd, kd)
            drhs_w = each(lambda a, b: _dot_x3(a, b, dims_tn), tt, dw)
            dl_u = each(_dot_nt, drhs_u, uu)
            dl_w = each(_dot_nt, drhs_w, ww)
            dlow = each(lambda a, b: jnp.where(strict, -(a + b), 0.0), dl_u, dl_w)
            dqm = each(jnp.multiply, dlow, decay)
            m_tot = each(lambda a, b, d, e: a * b + d * e, dai, ai, dlow, low)
            dqm_k = each(_dot, dqm, k)
            dk_l = each(_dot_tn, dqm, kb)
            col_sums = each(lambda m: _dot_hi(m, ones, dims_tn), m_tot)
            dkb_w = each(jnp.multiply, drhs_w, egc)
            dkb = each(jnp.add, dkb_w, dqm_k)
            dgc = [rsum(dqd[h] * qd[h]) - rkd[h] + jnp.where(last_row, jnp.sum(rkd[h]) + dgl[h], 0.0)
                   + rsum(m_tot[h]) + rsum(dkb_w[h] * kb[h]) for h in heads]
            dg = each(lambda a, b: _dot_hi(upper, jnp.where(lane == 1, a - b, 0.0)), dgc, col_sums)
            for h in heads:
                dq_ref[rows, cols[h]] = dq[h]
                dk_ref[rows, cols[h]] = dk_a[h] + dk_l[h] + dkb[h] * beta[h]
                dv_ref[rows, cols[h]] = drhs_u[h] * beta[h]
                dbeta = rsum(drhs_u[h] * v[h]) + rsum(dkb[h] * k[h])
                dhs_ref[h, rows, :] = jnp.where(lane == 0, dbeta, dg[h])
            return carry

        lax.fori_loop(0, n_chunk, chunk, 0)

    wide = hps * LANES
    blk = lambda off: pl.BlockSpec((rb, wide), lambda h, i: (n_blk - 1 - i, off + h))
    head = blk(0)
    hs_spec = pl.BlockSpec((hps, rb, LANES), lambda h, i: (h, n_blk - 1 - i, 0))
    full = jax.ShapeDtypeStruct((s_len, DN_HEADS * LANES), F32)
    return pl.pallas_call(
        body, name="dn_scan_bwd", grid=(groups, n_blk),
        in_specs=[blk(0), blk(groups), blk(2 * groups), head, head,
                  pl.BlockSpec((hps, rb, DN_CHUNK), lambda h, i: (h, n_blk - 1 - i, 0)), hs_spec,
                  pl.BlockSpec((hps, n_chunk, LANES, LANES), lambda h, i: (h, n_blk - 1 - i, 0, 0)), head],
        out_specs=[head, head, head, hs_spec],
        out_shape=[full, full, full, jax.ShapeDtypeStruct((DN_HEADS, s_len, LANES), F32)],
        scratch_shapes=[pltpu.VMEM((hps, LANES, LANES), F32)],
        compiler_params=_cparams(("parallel", "arbitrary")),
    )(qkv, qkv, qkv, u, w, t_inv, hs, states, d_o)


def dn_post(o, z, out_norm):
    def body(i, ins, outs, accs):
        gn = ins[2][...]
        for h in range(DN_HEADS):
            cols = slice(h * LANES, (h + 1) * LANES)
            xhat, _ = _rms_stats(ins[0][:, cols])
            outs[0][:, cols] = (xhat * gn * _silu(ins[1][:, cols])).astype(BF16)

    return row_call("dn_post", body, o.shape[0], ROW_TILE, [(o, "tile"), (z, "tile"), (out_norm, "full")],
                    [(o.shape, BF16)])[0]


def dn_post_bwd(d_og, o, z, out_norm):
    def body(i, ins, outs, accs):
        gn = ins[3][...]
        dgn = jnp.zeros((1, LANES), F32)
        for h in range(DN_HEADS):
            cols = slice(h * LANES, (h + 1) * LANES)
            dy, zh = ins[0][:, cols].astype(F32), ins[2][:, cols]
            xhat, r = _rms_stats(ins[1][:, cols])
            sz = _silu(zh)
            dgn = dgn + jnp.sum(dy * xhat * sz, axis=0, keepdims=True)
            outs[1][:, cols] = (dy * xhat * gn * _silu_grad(zh)).astype(BF16)
            dxhat = dy * gn * sz
            outs[0][:, cols] = r * (dxhat - xhat * jnp.mean(dxhat * xhat, axis=-1, keepdims=True))
        accs[0][...] += dgn

    return row_call("dn_post_bwd", body, o.shape[0], ROW_TILE,
                    [(d_og, "tile"), (o, "tile"), (z, "tile"), (out_norm, "full")],
                    [(o.shape, F32), (o.shape, BF16)], [(1, LANES)])


def dn_pre_bwd(dq, dk, dv, dhs, qkv_raw, ba, w_conv, gate):
    s_len = qkv_raw.shape[0]
    tm = min(DN_ROW_TILE, s_len)

    def body(i, ins, outs, accs):
        dq_ref, dk_ref, dv_ref, dhs_ref, x_ref, xp_ref, ba_ref, w_ref, gate_ref = ins
        dc_ref, dba_ref = outs

        def blk(cb, carry):
            cols = pl.ds(pl.multiple_of(cb * LANES, LANES), LANES)
            hcols = pl.ds(pl.multiple_of((cb & (DN_HEADS - 1)) * LANES, LANES), LANES)
            prev = jnp.where(i > 0, xp_ref[:, cols], 0.0)
            xcat = jnp.concatenate([prev, x_ref[:, cols]], axis=0)
            c = _conv_taps(xcat, w_ref, cols, DN_CONV, DN_HALO, tm)
            y = _silu(c)
            dy = jnp.where(cb < DN_HEADS, dq_ref[:, hcols],
                           jnp.where(cb < 2 * DN_HEADS, dk_ref[:, hcols], dv_ref[:, hcols]))
            rs = lax.rsqrt(jnp.sum(y * y, axis=-1, keepdims=True) + L2_EPS)
            fac = jnp.where(cb < DN_HEADS, DN_HEAD_DIM ** -0.5, 1.0)
            nrm = y * rs
            dn = dy * fac
            dy_norm = rs * (dn - nrm * jnp.sum(dn * nrm, axis=-1, keepdims=True))
            dc_ref[:, cols] = jnp.where(cb < 2 * DN_HEADS, dy_norm, dy) * _silu_grad(c)
            return carry

        lax.fori_loop(0, qkv_raw.shape[1] // LANES, blk, 0)

        lane = _iota((tm, LANES), 1)
        dbeta = jnp.zeros((tm, LANES), F32)
        dg = jnp.zeros((tm, LANES), F32)
        for h in range(DN_HEADS):
            dbeta = dbeta + jnp.where(lane == h, dhs_ref[h, :, 0:1], 0.0)
            dg = dg + jnp.where(lane == DN_HEADS + h, dhs_ref[h, :, 1:2], 0.0)
        bav = ba_ref[...]
        beta = _sigmoid(bav)
        ea = jnp.exp(gate_ref[0:1, :])
        pre = bav + gate_ref[1:2, :]
        g = -ea * _softplus(pre)
        da = dg * (-ea) * _sigmoid(pre)
        dba_ref[...] = (dbeta * beta * (1.0 - beta) + da).astype(BF16)
        rid = _iota((8, LANES), 0)
        accs[0][...] += (jnp.where(rid == 0, jnp.sum(dg * g, axis=0, keepdims=True), 0.0)
                         + jnp.where(rid == 1, jnp.sum(da, axis=0, keepdims=True), 0.0))

    return row_call("dn_pre_bwd", body, s_len, tm,
                    [(dq, "tile"), (dk, "tile"), (dv, "tile"), (dhs, "tile"), (qkv_raw, "tile"),
                     (qkv_raw, ("prev", DN_HALO)), (ba, "tile"), (w_conv, "full"), (gate, "full")],
                    [(qkv_raw.shape, F32), (ba.shape, BF16)], [(8, LANES)])


def dn_conv_bwd(dc, qkv_raw, w_conv):
    s_len = dc.shape[0]
    tm = min(DN_ROW_TILE, s_len)
    nt = s_len // tm

    def body(i, ins, outs, accs):
        dc_ref, dn_ref, x_ref, xp_ref, w_ref = ins

        def blk(cb, carry):
            cols = pl.ds(pl.multiple_of(cb * LANES, LANES), LANES)
            dy = dc_ref[:, cols]
            nxt = jnp.where(i < nt - 1, dn_ref[:, cols], 0.0)
            dcat = jnp.concatenate([dy, nxt], axis=0)
            outs[0][:, cols] = _conv_taps_bwd_x(dcat, w_ref, cols, DN_CONV, DN_HALO, tm).astype(BF16)
            prev = jnp.where(i > 0, xp_ref[:, cols], 0.0)
            xcat = jnp.concatenate([prev, x_ref[:, cols]], axis=0)
            accs[0][:, cols] += _conv_taps_bwd_w(dy, xcat, DN_CONV, DN_HALO, tm, 8)
            return carry

        lax.fori_loop(0, dc.shape[1] // LANES, blk, 0)

    return row_call("dn_conv_bwd", body, s_len, tm,
                    [(dc, "tile"), (dc, ("next", DN_HALO)), (qkv_raw, "tile"), (qkv_raw, ("prev", DN_HALO)),
                     (w_conv, "full")],
                    [(dc.shape, BF16)], [(8, dc.shape[1])])


def _glu(u_ref, cols, d):
    return u_ref[:, cols] * _sigmoid(u_ref[:, pl.ds(pl.multiple_of(d + cols.start, LANES), cols.size)])


def cv_core_fwd(u, w_dw, b_dw, ln_g, ln_b):
    s_len, d = u.shape[0], u.shape[1] // 2
    tm = min(CONV_ROW_TILE, s_len)

    def body(i, ins, outs, accs):
        u_ref, up_ref, w_ref, bdw_ref, g_ref, b_ref = ins
        s_ref, c_ref = outs

        def blk(cb, carry):
            cols = pl.ds(pl.multiple_of(cb * LANES, LANES), LANES)
            prev = jnp.where(i > 0, _glu(up_ref, cols, d), 0.0)
            xcat = jnp.concatenate([prev, _glu(u_ref, cols, d)], axis=0)
            c_ref[:, cols] = _conv_taps(xcat, w_ref, cols, CV_WIDTH, CV_HALO, tm) + bdw_ref[:, cols]
            return carry

        lax.fori_loop(0, d // LANES, blk, 0)
        c = c_ref[...]
        mu = jnp.mean(c, axis=-1, keepdims=True)
        xc = c - mu
        rstd = lax.rsqrt(jnp.mean(xc * xc, axis=-1, keepdims=True) + LN_EPS)
        s_ref[...] = _silu(xc * rstd * g_ref[...] + b_ref[...]).astype(BF16)

    return row_call("cv_core_fwd", body, s_len, tm,
                    [(u, "tile"), (u, ("prev", CV_HALO)), (w_dw, "full"), (b_dw, "full"), (ln_g, "full"),
                     (ln_b, "full")],
                    [((s_len, d), BF16), ((s_len, d), F32)])


def cv_ln_bwd(ds, c, ln_g, ln_b):
    def body(i, ins, outs, accs):
        cv, g = ins[1][...], ins[2][...]
        mu = jnp.mean(cv, axis=-1, keepdims=True)
        xc = cv - mu
        rstd = lax.rsqrt(jnp.mean(xc * xc, axis=-1, keepdims=True) + LN_EPS)
        xhat = xc * rstd
        dl = ins[0][...].astype(F32) * _silu_grad(xhat * g + ins[3][...])
        dxhat = dl * g
        dc = rstd * (dxhat - jnp.mean(dxhat, axis=-1, keepdims=True)
                     - xhat * jnp.mean(dxhat * xhat, axis=-1, keepdims=True))
        outs[0][...] = dc
        rid = _iota((8, cv.shape[1]), 0)
        accs[0][...] += (jnp.where(rid == 0, jnp.sum(dl * xhat, axis=0, keepdims=True), 0.0)
                         + jnp.where(rid == 1, jnp.sum(dl, axis=0, keepdims=True), 0.0)
                         + jnp.where(rid == 2, jnp.sum(dc, axis=0, keepdims=True), 0.0))

    return row_call("cv_ln_bwd", body, c.shape[0], ROW_TILE,
                    [(ds, "tile"), (c, "tile"), (ln_g, "full"), (ln_b, "full")], [(c.shape, F32)], [(8, c.shape[1])])


def cv_conv_bwd(dc, u, w_dw):
    s_len, d = dc.shape
    tm = min(CONV_ROW_TILE, s_len)
    nt = s_len // tm

    def body(i, ins, outs, accs):
        dc_ref, dn_ref, u_ref, up_ref, w_ref = ins

        def blk(cb, carry):
            cols = pl.ds(pl.multiple_of(cb * LANES, LANES), LANES)
            gcols = pl.ds(pl.multiple_of(d + cb * LANES, LANES), LANES)
            dy = dc_ref[:, cols]
            nxt = jnp.where(i < nt - 1, dn_ref[:, cols], 0.0)
            dgl = _conv_taps_bwd_x(jnp.concatenate([dy, nxt], axis=0), w_ref, cols, CV_WIDTH, CV_HALO, tm)
            u1, sg = u_ref[:, cols], _sigmoid(u_ref[:, gcols])
            du1 = dgl * sg
            du2 = dgl * u1 * sg * (1.0 - sg)
            outs[0][:, cols] = du1.astype(BF16)
            outs[0][:, gcols] = du2.astype(BF16)
            accs[1][:, cols] += jnp.sum(du1, axis=0, keepdims=True)
            accs[1][:, gcols] += jnp.sum(du2, axis=0, keepdims=True)
            prev = jnp.where(i > 0, _glu(up_ref, cols, d), 0.0)
            xcat = jnp.concatenate([prev, u1 * sg], axis=0)
            accs[0][:, cols] += _conv_taps_bwd_w(dy, xcat, CV_WIDTH, CV_HALO, tm, CV_HALO)
            return carry

        lax.fori_loop(0, d // LANES, blk, 0)

    return row_call("cv_conv_bwd", body, s_len, tm,
                    [(dc, "tile"), (dc, ("next", CV_HALO)), (u, "tile"), (u, ("prev", CV_HALO)), (w_dw, "full")],
                    [(u.shape, BF16)], [(CV_HALO, d), (1, 2 * d)])


def xa_core_fwd(name, q, kv):
    d = q.shape[1]

    def body(i, ins, outs, accs):
        for h in range(XA_HEADS):
            cols = slice(h * XA_HEAD_DIM, (h + 1) * XA_HEAD_DIM)
            vcols = slice(d + h * XA_HEAD_DIM, d + (h + 1) * XA_HEAD_DIM)
            s = _dot_nt(ins[0][:, cols], ins[1][:, cols]) * (XA_HEAD_DIM ** -0.5)
            e = jnp.exp(s - jnp.max(s, axis=-1, keepdims=True))
            p = e / jnp.sum(e, axis=-1, keepdims=True)
            outs[0][:, cols] = _dot(p, ins[1][:, vcols]).astype(BF16)

    return row_call(name, body, q.shape[0], ROW_TILE, [(q, "tile"), (kv, "full")], [(q.shape, BF16)])[0]


def xa_core_bwd(name, d_o, q, kv):
    d = q.shape[1]

    def body(i, ins, outs, accs):
        for h in range(XA_HEADS):
            cols = slice(h * XA_HEAD_DIM, (h + 1) * XA_HEAD_DIM)
            vcols = slice(d + h * XA_HEAD_DIM, d + (h + 1) * XA_HEAD_DIM)
            qh, kh, vh, doh = ins[1][:, cols], ins[2][:, cols], ins[2][:, vcols], ins[0][:, cols]
            s = _dot_nt(qh, kh) * (XA_HEAD_DIM ** -0.5)
            e = jnp.exp(s - jnp.max(s, axis=-1, keepdims=True))
            p = e / jnp.sum(e, axis=-1, keepdims=True)
            dp = _dot_nt(doh, vh)
            ds = p * (dp - jnp.sum(dp * p, axis=-1, keepdims=True)) * (XA_HEAD_DIM ** -0.5)
            outs[0][:, cols] = _dot(ds, kh).astype(BF16)
            accs[0][:, cols] += _dot_tn(ds, qh)
            accs[0][:, vcols] += _dot_tn(p, doh)

    return row_call(name, body, q.shape[0], ROW_TILE, [(d_o, "tile"), (q, "tile"), (kv, "full")],
                    [(q.shape, BF16)], [kv.shape])


def adamw(name, w, g, m, v):
    def body(i, ins, outs, accs):
        wv, gv = ins[0][...], ins[1][...]
        mn = ADAM_B1 * ins[2][...] + (1.0 - ADAM_B1) * gv
        vn = ADAM_B2 * ins[3][...] + (1.0 - ADAM_B2) * jnp.square(gv)
        m_hat = mn / (1.0 - ADAM_B1 ** ADAM_STEP)
        v_hat = vn / (1.0 - ADAM_B2 ** ADAM_STEP)
        outs[0][...] = -ADAM_LR * (m_hat / (jnp.sqrt(v_hat) + ADAM_EPS) + ADAM_WD * wv)
        outs[1][...] = mn
        outs[2][...] = vn

    return row_call(name, body, w.shape[0], ROW_TILE, [(w, "tile"), (g, "tile"), (m, "tile"), (v, "tile")],
                    [(w.shape, F32)] * 3)


def adamw_halves(name, w, g_mine, g_sibling, m, v, core):
    n_layers = len(g_mine)
    rows, cols = w.shape
    half_rows = rows // n_layers // 2
    tm = _pick_tile(half_rows, ADAMW_ROW_TILE)
    per_half = half_rows // tm

    def body(core_ref, w_ref, *rest):
        g_refs = rest[:2 * n_layers]
        m_ref, v_ref, g_out, d_out, m_out, v_out = rest[2 * n_layers:]
        i = pl.program_id(0)
        mine = ((i // per_half) % 2) == core_ref[0]
        layer = i // (2 * per_half)
        gv = jnp.where(mine, g_refs[0][...], g_refs[n_layers][...])
        for l in range(1, n_layers):
            gv = jnp.where(layer == l, jnp.where(mine, g_refs[l][...], g_refs[n_layers + l][...]), gv)
        mn = ADAM_B1 * m_ref[...] + (1.0 - ADAM_B1) * gv
        vn = ADAM_B2 * v_ref[...] + (1.0 - ADAM_B2) * jnp.square(gv)
        m_hat = mn / (1.0 - ADAM_B1 ** ADAM_STEP)
        v_hat = vn / (1.0 - ADAM_B2 ** ADAM_STEP)
        g_out[...] = gv
        d_out[...] = -ADAM_LR * (m_hat / (jnp.sqrt(v_hat) + ADAM_EPS) + ADAM_WD * w_ref[...])
        m_out[...] = mn
        v_out[...] = vn

    whole = pl.BlockSpec((tm, cols), lambda i, core_ref: (i, 0))
    half = pl.BlockSpec((tm, cols), lambda i, core_ref: (i % per_half, 0))
    return pl.pallas_call(
        body, name=name,
        grid_spec=pltpu.PrefetchScalarGridSpec(
            num_scalar_prefetch=1, grid=(2 * per_half * n_layers,),
            in_specs=[whole] + [half] * (2 * n_layers) + [whole, whole], out_specs=[whole] * 4),
        out_shape=[jax.ShapeDtypeStruct(w.shape, F32)] * 4,
        compiler_params=_cparams(("parallel",)),
    )(core, w, *g_mine, *g_sibling, m, v)


HBM_SPEC = pl.BlockSpec(memory_space=pltpu.HBM)


def _position():
    return lax.axis_index("x"), lax.axis_index("y"), lax.axis_index("c")


def _other_chips(x, y):
    return [(1 - x, y), (x, 1 - y), (1 - x, 1 - y)]


def _row_chunks(rows):
    return rows // D2D_CHUNK_ROWS if rows % D2D_CHUNK_ROWS == 0 else 1


def _start_chunked(make, rows):
    k = _row_chunks(rows)
    for i in range(k):
        make(i * (rows // k), rows // k).start()


def gather_shards(packs):
    n = len(packs)

    def body(*refs):
        srcs, outs = refs[:n], refs[n:2 * n]
        send_sems, recv_sems = refs[2 * n:]
        x, y, c = _position()
        me = 2 * x + y
        chips = _other_chips(x, y)
        sibling = (x, y, 1 - c)

        def over_ici(a, j):
            px, py = chips[j]
            rows = srcs[a].shape[0] // 2
            return pltpu.make_async_remote_copy(
                src_ref=srcs[a].at[pl.ds(c * rows, rows), :], dst_ref=outs[a].at[me, pl.ds(c * rows, rows), :],
                send_sem=send_sems.at[a, j], recv_sem=recv_sems.at[a, j], device_id=(px, py, c), device_id_type=MESH)

        def landed(a, j):
            px, py = chips[j]
            rows = srcs[a].shape[0] // 2
            part = outs[a].at[2 * px + py, pl.ds(c * rows, rows), :]
            return pltpu.make_async_remote_copy(
                src_ref=part, dst_ref=part, send_sem=send_sems.at[a, j], recv_sem=recv_sems.at[a, j],
                device_id=(px, py, c), device_id_type=MESH)

        def over_d2d(a, j, cc, off, size):
            px, py = chips[j]
            rows = srcs[a].shape[0] // 2
            part = outs[a].at[2 * px + py, pl.ds(cc * rows + off, size), :]
            return pltpu.make_async_remote_copy(
                src_ref=part, dst_ref=part, send_sem=send_sems.at[a, 3 + j], recv_sem=recv_sems.at[a, 3 + j],
                device_id=sibling, device_id_type=MESH)

        for a in range(n):
            for j in range(3):
                over_ici(a, j).start()
        for a in range(n):
            for j in range(3):
                landed(a, j).wait_recv()
                _start_chunked(functools.partial(over_d2d, a, j, c), srcs[a].shape[0] // 2)
        for a in range(n):
            rows = srcs[a].shape[0] // 2
            for j in range(3):
                over_d2d(a, j, 1 - c, 0, rows).wait_recv()
                over_d2d(a, j, c, 0, rows).wait_send()
                over_ici(a, j).wait_send()

    return pl.pallas_call(
        body, name="gather_shards",
        in_specs=[HBM_SPEC] * n, out_specs=[HBM_SPEC] * n,
        out_shape=[jax.ShapeDtypeStruct((N_CHIPS,) + p.shape, p.dtype) for p in packs],
        scratch_shapes=[pltpu.SemaphoreType.DMA((n, 6)), pltpu.SemaphoreType.DMA((n, 6))],
    )(*packs)


def pair_split(name, packs):
    n = len(packs)

    def body(*refs):
        srcs, outs = refs[:n], refs[n:2 * n]
        send_sems, recv_sems = refs[2 * n:]
        x, y, c = _position()

        def remote(a, off, size):
            rows = srcs[a].shape[1] // 2
            return pltpu.make_async_remote_copy(
                src_ref=srcs[a].at[:, pl.ds((1 - c) * rows + off, size), :],
                dst_ref=outs[a].at[:, pl.ds(off, size), :],
                send_sem=send_sems.at[a], recv_sem=recv_sems.at[a], device_id=(x, y, 1 - c), device_id_type=MESH)

        for a in range(n):
            _start_chunked(functools.partial(remote, a), srcs[a].shape[1] // 2)
        for a in range(n):
            remote(a, 0, srcs[a].shape[1] // 2).wait()

    return pl.pallas_call(
        body, name=name, in_specs=[HBM_SPEC] * n, out_specs=[HBM_SPEC] * n,
        out_shape=[jax.ShapeDtypeStruct((p.shape[0], p.shape[1] // 2, p.shape[2]), p.dtype) for p in packs],
        scratch_shapes=[pltpu.SemaphoreType.DMA((n,)), pltpu.SemaphoreType.DMA((n,))],
    )(*packs)


def chip_scatter(packs):
    n = len(packs)

    def body(*refs):
        srcs, outs = refs[:n], refs[n:2 * n]
        send_sems, recv_sems = refs[2 * n:]
        x, y, c = _position()
        copies = []
        for a in range(n):
            for j, (px, py) in enumerate(_other_chips(x, y)):
                cp = pltpu.make_async_remote_copy(
                    src_ref=srcs[a].at[2 * px + py], dst_ref=outs[a].at[j],
                    send_sem=send_sems.at[a, j], recv_sem=recv_sems.at[a, j],
                    device_id=(px, py, c), device_id_type=MESH)
                cp.start()
                copies.append(cp)
        for cp in copies:
            cp.wait()

    return pl.pallas_call(
        body, name="chip_scatter", in_specs=[HBM_SPEC] * n, out_specs=[HBM_SPEC] * n,
        out_shape=[jax.ShapeDtypeStruct((N_CHIPS - 1,) + p.shape[1:], p.dtype) for p in packs],
        scratch_shapes=[pltpu.SemaphoreType.DMA((n, 3)), pltpu.SemaphoreType.DMA((n, 3))],
    )(*packs)


def pair_join(name, halves):
    n = len(halves)

    def body(*refs):
        srcs, outs = refs[:n], refs[n:2 * n]
        send_sems, recv_sems = refs[2 * n:]
        x, y, c = _position()

        def remote(a, off, size):
            return pltpu.make_async_remote_copy(
                src_ref=srcs[a].at[pl.ds(off, size), :], dst_ref=outs[a].at[pl.ds(off, size), :],
                send_sem=send_sems.at[a], recv_sem=recv_sems.at[a], device_id=(x, y, 1 - c), device_id_type=MESH)

        for a in range(n):
            _start_chunked(functools.partial(remote, a), srcs[a].shape[0])
        for a in range(n):
            remote(a, 0, srcs[a].shape[0]).wait()

    return pl.pallas_call(
        body, name=name, in_specs=[HBM_SPEC] * n, out_specs=[HBM_SPEC] * n,
        out_shape=[jax.ShapeDtypeStruct(p.shape, p.dtype) for p in halves],
        scratch_shapes=[pltpu.SemaphoreType.DMA((n,)), pltpu.SemaphoreType.DMA((n,))],
    )(*halves)


SEM_SPEC = pl.BlockSpec(memory_space=pltpu.SEMAPHORE)
DATAFLOW = pltpu.SideEffectType.DATAFLOW_SIDE_EFFECTING


def _ici_copy(kind, srcs, lands, send_sems, recv_sems, a, j):
    x, y, c = _position()
    px, py = _other_chips(x, y)[j]
    if kind == "gather":
        rows = srcs[a].shape[0] // 2
        src = srcs[a].at[pl.ds(c * rows, rows), :]
        dst = lands[a].at[2 * x + y, pl.ds(c * rows, rows), :]
    else:
        src = srcs[a].at[2 * px + py]
        dst = lands[a].at[j]
    return pltpu.make_async_remote_copy(src_ref=src, dst_ref=dst, send_sem=send_sems, recv_sem=recv_sems,
                                        device_id=(px, py, c), device_id_type=MESH)


def ici_start(name, kind, srcs, land_shapes):
    n = len(srcs)
    lands = [pltpu.with_memory_space_constraint(lax.empty(shp, s.dtype), pltpu.HBM) for shp, s in zip(land_shapes, srcs)]

    def body(*refs):
        src_refs, land_refs = refs[:n], refs[n:2 * n]
        send_sems, recv_sems = refs[2 * n], refs[2 * n + 1]
        token = refs[-1]
        for a in range(n):
            for j in range(N_CHIPS - 1):
                _ici_copy(kind, src_refs, land_refs, send_sems, recv_sems, a, j).start()
        token[...] = jnp.zeros_like(token)

    sems = pltpu.SemaphoreType.DMA(())
    res = pl.pallas_call(
        body, name=name,
        out_shape=[sems, sems] + [pltpu.HBM(s.shape, s.dtype) for s in srcs]
        + [pltpu.HBM(l.shape, l.dtype) for l in lands] + [jax.ShapeDtypeStruct((8, LANES), F32)],
        in_specs=[HBM_SPEC] * (2 * n),
        out_specs=[SEM_SPEC, SEM_SPEC] + [HBM_SPEC] * (2 * n) + [pl.BlockSpec(memory_space=pltpu.VMEM)],
        input_output_aliases={i: 2 + i for i in range(2 * n)},
        compiler_params=pltpu.CompilerParams(has_side_effects=DATAFLOW),
    )(*[pltpu.with_memory_space_constraint(s, pltpu.HBM) for s in srcs], *lands)
    return res[0], res[1], list(res[2:2 + n]), list(res[2 + n:2 + 2 * n]), res[-1]


def ici_wait(name, kind, send_sems, recv_sems, srcs, lands, after):
    n = len(srcs)

    def body(*refs):
        src_refs, land_refs = refs[:n], refs[n:2 * n]
        send, recv = refs[2 * n], refs[2 * n + 1]
        for a in range(n):
            for j in range(N_CHIPS - 1):
                cp = _ici_copy(kind, src_refs, land_refs, send, recv, a, j)
                cp.wait_send()
                cp.wait_recv()

    res = pl.pallas_call(
        body, name=name,
        out_shape=[pltpu.HBM(s.shape, s.dtype) for s in srcs] + [pltpu.HBM(l.shape, l.dtype) for l in lands],
        in_specs=[HBM_SPEC] * (2 * n) + [SEM_SPEC, SEM_SPEC, pl.BlockSpec(memory_space=pl.ANY)],
        out_specs=[HBM_SPEC] * (2 * n),
        input_output_aliases={i: i for i in range(2 * n)},
        compiler_params=pltpu.CompilerParams(has_side_effects=DATAFLOW),
    )(*srcs, *lands, send_sems, recv_sems, after)
    return list(res[:n]), list(res[n:])


def pair_forward(gathered):
    n = len(gathered)

    def body(*refs):
        outs = refs[n:2 * n]
        send_sems, recv_sems = refs[2 * n:]
        x, y, c = _position()
        chips = _other_chips(x, y)

        def part(a, j, cc, off, size):
            px, py = chips[j]
            rows = outs[a].shape[1] // 2
            ref = outs[a].at[2 * px + py, pl.ds(cc * rows + off, size), :]
            return pltpu.make_async_remote_copy(
                src_ref=ref, dst_ref=ref, send_sem=send_sems.at[a, j], recv_sem=recv_sems.at[a, j],
                device_id=(x, y, 1 - c), device_id_type=MESH)

        for a in range(n):
            for j in range(N_CHIPS - 1):
                _start_chunked(functools.partial(part, a, j, c), outs[a].shape[1] // 2)
        for a in range(n):
            rows = outs[a].shape[1] // 2
            for j in range(N_CHIPS - 1):
                part(a, j, 1 - c, 0, rows).wait_recv()
                part(a, j, c, 0, rows).wait_send()

    return pl.pallas_call(
        body, name="pair_forward", in_specs=[HBM_SPEC] * n, out_specs=[HBM_SPEC] * n,
        out_shape=[jax.ShapeDtypeStruct(g.shape, g.dtype) for g in gathered],
        input_output_aliases={i: i for i in range(n)},
        scratch_shapes=[pltpu.SemaphoreType.DMA((n, N_CHIPS - 1)), pltpu.SemaphoreType.DMA((n, N_CHIPS - 1))],
    )(*gathered)


def all_sum_small(part):
    n_dev = 8
    rows = part.shape[0]

    def body(src, out, buf, send_sems, recv_sems):
        x, y, c = _position()
        me = 4 * x + 2 * y + c
        buf[me] = src[...]
        copies = []
        for k in range(1, n_dev):
            px, py, pc = x ^ ((k >> 2) & 1), y ^ ((k >> 1) & 1), c ^ (k & 1)
            cp = pltpu.make_async_remote_copy(
                src_ref=src, dst_ref=buf.at[me], send_sem=send_sems.at[k - 1], recv_sem=recv_sems.at[k - 1],
                device_id=(px, py, pc), device_id_type=MESH)
            cp.start()
            copies.append(cp)
        for cp in copies:
            cp.wait()
        acc = buf[0]
        for k in range(1, n_dev):
            acc = acc + buf[k]
        out[...] = acc

    return pl.pallas_call(
        body, name="all_sum_small",
        in_specs=[pl.BlockSpec(memory_space=pltpu.VMEM)], out_specs=pl.BlockSpec(memory_space=pltpu.VMEM),
        out_shape=jax.ShapeDtypeStruct(part.shape, F32),
        scratch_shapes=[pltpu.VMEM((n_dev, rows, part.shape[1]), F32),
                        pltpu.SemaphoreType.DMA((n_dev - 1,)), pltpu.SemaphoreType.DMA((n_dev - 1,))],
    )(part)


def add_pairs(name, src, theirs, core, out_dtype):
    slabs, rows, cols = theirs.shape
    tm = _pick_tile(rows, ROW_TILE)
    nb = rows // tm

    def body(core_ref, a_ref, b_ref, o_ref):
        o_ref[...] = (a_ref[...].astype(F32) + b_ref[...].astype(F32)).astype(out_dtype)

    return pl.pallas_call(
        body, name=name,
        grid_spec=pltpu.PrefetchScalarGridSpec(
            num_scalar_prefetch=1, grid=(slabs, nb),
            in_specs=[pl.BlockSpec((None, tm, cols), lambda s, i, core_ref: (s, core_ref[0] * nb + i, 0)),
                      pl.BlockSpec((None, tm, cols), lambda s, i, core_ref: (s, i, 0))],
            out_specs=pl.BlockSpec((None, tm, cols), lambda s, i, core_ref: (s, i, 0))),
        out_shape=jax.ShapeDtypeStruct(theirs.shape, out_dtype),
        compiler_params=_cparams(("parallel", "parallel")),
    )(core, src, theirs)


def add_four(name, src, theirs, chip):
    _, rows, cols = theirs.shape
    tm = _pick_tile(rows, ROW_TILE)

    def body(chip_ref, a_ref, b_ref, o_ref):
        acc = a_ref[...].astype(F32)
        for j in range(N_CHIPS - 1):
            acc = acc + b_ref[j].astype(F32)
        o_ref[...] = acc

    return pl.pallas_call(
        body, name=name,
        grid_spec=pltpu.PrefetchScalarGridSpec(
            num_scalar_prefetch=1, grid=(rows // tm,),
            in_specs=[pl.BlockSpec((None, tm, cols), lambda i, chip_ref: (chip_ref[0], i, 0)),
                      pl.BlockSpec((N_CHIPS - 1, tm, cols), lambda i, chip_ref: (0, i, 0))],
            out_specs=pl.BlockSpec((tm, cols), lambda i, chip_ref: (i, 0))),
        out_shape=jax.ShapeDtypeStruct((rows, cols), F32),
        compiler_params=_cparams(("parallel",)),
    )(chip, src, theirs)


PACK_COLS = 1024
BIG_ROW_MULTIPLE = 512
SMALL_ROW_MULTIPLE = 32
BIG = ["dn_w_in", "dn_w_out", "cv_w_pw1", "cv_w_pw2", "xa_w_q", "xa_w_kv", "xa_w_o", "mlp_w_up", "mlp_w_down"]
SMALL = ["dn_w_conv", "cv_norm", "cv_b_pw1", "cv_w_dw", "cv_b_dw", "cv_ln_g", "cv_ln_b", "cv_b_pw2"]
SHARD_AXIS = {"dn_w_in": 2, "dn_w_conv": 2, "dn_w_out": 1, "cv_norm": 1, "cv_w_pw1": 2, "cv_b_pw1": 1,
              "cv_w_dw": 2, "cv_b_dw": 1, "cv_ln_g": 1, "cv_ln_b": 1, "cv_w_pw2": 1, "cv_b_pw2": 1,
              "xa_w_q": 1, "xa_w_kv": 2, "xa_w_o": 1, "mlp_w_up": 2, "mlp_w_down": 1}
REPLICATED = ["dn_norm", "dn_a_log", "dn_dt_bias", "dn_out_norm", "xa_norm", "xa_mem_norm", "mlp_norm", "final_norm"]


def _pack_rows(size):
    return -(-size // PACK_COLS)


SHARD_SHAPES = {
    "dn_w_in": (1, 1024, 1028), "dn_w_conv": (1, 4, 768), "dn_w_out": (1, 256, 1024), "cv_norm": (1, 256),
    "cv_w_pw1": (1, 1024, 512), "cv_b_pw1": (1, 512), "cv_w_dw": (1, 31, 256), "cv_b_dw": (1, 256),
    "cv_ln_g": (1, 256), "cv_ln_b": (1, 256), "cv_w_pw2": (1, 256, 1024), "cv_b_pw2": (1, 256),
    "xa_w_q": (2, 256, 1024), "xa_w_kv": (2, 1024, 512), "xa_w_o": (2, 256, 1024),
    "mlp_w_up": (2, 1024, 1024), "mlp_w_down": (2, 1024, 1024)}


def _shard_shape(nm):
    return SHARD_SHAPES[nm]


def _pack(tensors, names, dtype, row_multiple):
    pieces = []
    for nm in names:
        t = tensors[nm]
        flat = t.reshape(t.shape[0], -1) if t.ndim > len(_shard_shape(nm)) else t.reshape(1, -1)
        pad = _pack_rows(flat.shape[1]) * PACK_COLS - flat.shape[1]
        pieces.append(jnp.pad(flat.astype(dtype), ((0, 0), (0, pad))))
    cat = jnp.concatenate(pieces, axis=1)
    rows = cat.shape[1] // PACK_COLS
    total = -(-rows // row_multiple) * row_multiple
    cat = jnp.pad(cat, ((0, 0), (0, (total - rows) * PACK_COLS)))
    return cat.reshape(cat.shape[0], total, PACK_COLS)


def _unpack(pack, names):
    lead = pack.shape[:-2]
    flat = pack.reshape(lead + (-1,))
    out, off = {}, 0
    for nm in names:
        shp = _shard_shape(nm)
        size = 1
        for s in shp:
            size *= s
        out[nm] = flat[..., off:off + size].reshape(lead + shp)
        off += _pack_rows(size) * PACK_COLS
    return out


def _to_full(nm, stacked):
    ax = SHARD_AXIS[nm]
    moved = jnp.moveaxis(stacked, 0, ax)
    shp = list(_shard_shape(nm))
    shp[ax] *= N_CHIPS
    return moved.reshape(shp)


def _to_shards(nm, full):
    ax = SHARD_AXIS[nm]
    shp = list(_shard_shape(nm))
    split = full.reshape(shp[:ax] + [N_CHIPS, shp[ax]] + shp[ax + 1:])
    return jnp.moveaxis(split, ax, 0)


def _row(v):
    return v.reshape(1, -1)


class Stacked:
    def __init__(self, arr, split, layer):
        self.arr, self.kw = arr, dict(b_split=split, b_layer=layer)


def _grad_out(split):
    return dict(out_dtype=BF16, out_split=split, out_layer=(0, 1))


def mlp_fwd(tag, h, g, w_up, w_down):
    n = rms_fwd(tag + "_norm", h, g)
    act = mm(tag + "_up", n, w_up.arr, out_dtype=BF16, epi=lambda acc: jnp.square(jnp.maximum(acc, 0.0)), **w_up.kw)
    out = mm(tag + "_down", act, w_down.arr, tk=LONG_TILE_K, epi=lambda acc, res: acc + res, epi_tiles=(h,),
             **w_down.kw)
    return out, (n, act)


def mlp_bwd(tag, dh, h, g, w_up, w_down, saved, after=None):
    n, act = saved
    dh, dh16 = dh
    dup = mm(tag + "_d_act", dh16, w_down.arr, tb=True, out_dtype=BF16, after=after,
             epi=lambda acc, t: acc * (2.0 * jnp.sqrt(t.astype(F32))), epi_tiles=(act,), **w_down.kw)
    dw_down = mm(tag + "_dw_down", act, dh16, ta=True, tk=LONG_TILE_K, **_grad_out("rows"))
    dn = mm(tag + "_dn", dup, w_up.arr, tb=True, **w_up.kw)
    dw_up = mm(tag + "_dw_up", n, dup, ta=True, tk=LONG_TILE_K, **_grad_out("cols"))
    dh_in, dh16_in, dg = rms_bwd(tag + "_norm_bwd", dn, h, g, dh)
    return (dh_in, dh16_in), dg, (dw_up, dw_down)


def xa_fwd(tag, h, mem, g, g_mem, w_q, w_kv, w_o):
    n = rms_fwd(tag + "_norm", h, g)
    mem_n = rms_fwd(tag + "_mem_norm", mem, g_mem)
    q = mm(tag + "_q", n, w_q.arr, out_dtype=BF16, **w_q.kw)
    kv = mm(tag + "_kv", mem_n, w_kv.arr, out_dtype=BF16, **w_kv.kw)
    o = xa_core_fwd(tag + "_core", q, kv)
    out = mm(tag + "_o", o, w_o.arr, epi=lambda acc, res: acc + res, epi_tiles=(h,), **w_o.kw)
    return out, (n, mem_n, q, kv, o)


def xa_bwd(tag, dh, h, mem, g, g_mem, w_q, w_kv, w_o, saved):
    n, mem_n, q, kv, o = saved
    dh, dh16 = dh
    d_o = mm(tag + "_d_o", dh16, w_o.arr, tb=True, out_dtype=BF16, **w_o.kw)
    dw_o = mm(tag + "_dw_o", o, dh16, ta=True, tk=LONG_TILE_K, **_grad_out("rows"))
    dq, dkv = xa_core_bwd(tag + "_core_bwd", d_o, q, kv)
    dn = mm(tag + "_dn", dq, w_q.arr, tb=True, **w_q.kw)
    dw_q = mm(tag + "_dw_q", n, dq, ta=True, tk=LONG_TILE_K, **_grad_out("rows"))
    dh_in, dh16_in, dg = rms_bwd(tag + "_norm_bwd", dn, h, g, dh)
    dw_kv = mm(tag + "_dw_kv", mem_n, dkv, ta=True, **_grad_out("cols"))
    dmem_n = mm(tag + "_dmem", dkv, w_kv.arr, tb=True, **w_kv.kw)
    dg_mem = mem_norm_bwd(tag + "_mem_norm_bwd", dmem_n, mem, g_mem)
    return (dh_in, dh16_in), dg, dg_mem, (dw_q, dw_kv, dw_o)


def _gate_tile(a_log, dt_bias):
    t = jnp.zeros((8, LANES), F32)
    t = t.at[0, DN_HEADS:2 * DN_HEADS].set(a_log.reshape(-1))
    return t.at[1, DN_HEADS:2 * DN_HEADS].set(dt_bias.reshape(-1))


def dn_fwd(h, g, w_qkv, w_z, w_ba, w_conv, gate, out_norm, w_out):
    n = rms_fwd("dn_norm", h, g)
    qkv_raw = mm("dn_proj_qkv", n, w_qkv)
    z = mm("dn_proj_z", n, w_z)
    ba = mm("dn_proj_ba", n, w_ba)
    qkv, hs = dn_pre(qkv_raw, ba, w_conv, gate)
    u, w, t_inv = dn_solve(qkv, hs)
    o, states = dn_scan_fwd(qkv, u, w, hs)
    og = dn_post(o, z, out_norm)
    out = mm("dn_out", og, w_out.arr, epi=lambda acc, res: acc + res, epi_tiles=(h,), **w_out.kw)
    return out, (n, qkv_raw, z, ba, qkv, hs, u, w, t_inv, o, states, og)


def dn_bwd(dh, h, g, w_qkv, w_z, w_ba, w_conv, gate, out_norm, w_out, saved, after=None):
    n, qkv_raw, z, ba, qkv, hs, u, w, t_inv, o, states, og = saved
    dh, dh16 = dh
    d_og = mm("dn_d_og", dh16, w_out.arr, tb=True, out_dtype=BF16, after=after, **w_out.kw)
    dw_out = mm("dn_dw_out", og, dh16, ta=True, tk=LONG_TILE_K, **_grad_out("rows"))
    d_o, dz, d_out_norm = dn_post_bwd(d_og, o, z, out_norm)
    dq, dk, dv, dhs = dn_scan_bwd(qkv, u, w, t_inv, hs, states, d_o)
    dc, dba, d_gate = dn_pre_bwd(dq, dk, dv, dhs, qkv_raw, ba, w_conv, gate)
    dqkv_raw, dw_conv = dn_conv_bwd(dc, qkv_raw, w_conv)
    dn = mm("dn_dn_qkv", dqkv_raw, w_qkv, tb=True)
    dn = mm("dn_dn_z", dz, w_z, tb=True, epi=lambda acc, t: acc + t, epi_tiles=(dn,))
    dn = mm("dn_dn_ba", dba, w_ba, tb=True, epi=lambda acc, t: acc + t, epi_tiles=(dn,))
    dw_qkv = mm("dn_dw_qkv", n, dqkv_raw, ta=True, tk=LONG_TILE_K)
    dw_z = mm("dn_dw_z", n, dz, ta=True, tk=LONG_TILE_K)
    dw_ba = mm("dn_dw_ba", n, dba, ta=True, tk=LONG_TILE_K)
    dh_in, _, dg = rms_bwd("dn_norm_bwd", dn, h, g, dh)
    return dh_in, dg, dw_qkv, dw_z, dw_ba, dw_conv, d_gate, d_out_norm, dw_out


def cv_fwd(h, g, w_pw1, b_pw1, w_dw, b_dw, ln_g, ln_b, w_pw2, b_pw2):
    n = rms_fwd("cv_norm", h, g)
    u = mm("cv_pw1", n, w_pw1.arr, epi=lambda acc, b: acc + b, epi_rows=(b_pw1,), **w_pw1.kw)
    s, c = cv_core_fwd(u, w_dw, b_dw, ln_g, ln_b)
    out = mm("cv_pw2", s, w_pw2.arr, epi=lambda acc, res, b: acc + res + b, epi_tiles=(h,), epi_rows=(b_pw2,),
             **w_pw2.kw)
    return out, (n, u, s, c)


def cv_bwd(dh, h, g, w_pw1, w_dw, ln_g, ln_b, w_pw2, saved):
    n, u, s, c = saved
    dh, dh16 = dh
    ds = mm("cv_d_s", dh16, w_pw2.arr, tb=True, out_dtype=BF16, **w_pw2.kw)
    dw_pw2 = mm("cv_dw_pw2", s, dh16, ta=True, tk=LONG_TILE_K, **_grad_out("rows"))
    db_pw2 = col_sum("cv_db_pw2", dh)
    dc, ln_acc = cv_ln_bwd(ds, c, ln_g, ln_b)
    du, dw_dw, db_pw1 = cv_conv_bwd(dc, u, w_dw)
    dn = mm("cv_dn", du, w_pw1.arr, tb=True, **w_pw1.kw)
    dw_pw1 = mm("cv_dw_pw1", n, du, ta=True, tk=LONG_TILE_K, **_grad_out("cols"))
    dh_in, dh16_in, dg = rms_bwd("cv_norm_bwd", dn, h, g, dh)
    return (dh_in, dh16_in), dg, dw_pw1, db_pw1, dw_dw, ln_acc, dw_pw2, db_pw2


WEIGHTS = ["dn_norm", "dn_w_in", "dn_w_conv", "dn_a_log", "dn_dt_bias", "dn_out_norm", "dn_w_out", "cv_norm",
           "cv_w_pw1", "cv_b_pw1", "cv_w_dw", "cv_b_dw", "cv_ln_g", "cv_ln_b", "cv_w_pw2", "cv_b_pw2", "xa_norm",
           "xa_mem_norm", "xa_w_q", "xa_w_kv", "xa_w_o", "mlp_norm", "mlp_w_up", "mlp_w_down", "final_norm"]


def _as_2d(t):
    if t.ndim == 1:
        return t.reshape(1, -1)
    return t.reshape(-1, t.shape[-1])


def kernel(x, mem, dn_norm, dn_w_in, dn_w_conv, dn_a_log, dn_dt_bias, dn_out_norm, dn_w_out, cv_norm, cv_w_pw1, cv_b_pw1, cv_w_dw, cv_b_dw, cv_ln_g, cv_ln_b, cv_w_pw2, cv_b_pw2, xa_norm, xa_mem_norm, xa_w_q, xa_w_kv, xa_w_o, mlp_norm, mlp_w_up, mlp_w_down, final_norm, loss_target, m_dn_norm, m_dn_w_in, m_dn_w_conv, m_dn_a_log, m_dn_dt_bias, m_dn_out_norm, m_dn_w_out, m_cv_norm, m_cv_w_pw1, m_cv_b_pw1, m_cv_w_dw, m_cv_b_dw, m_cv_ln_g, m_cv_ln_b, m_cv_w_pw2, m_cv_b_pw2, m_xa_norm, m_xa_mem_norm, m_xa_w_q, m_xa_w_kv, m_xa_w_o, m_mlp_norm, m_mlp_w_up, m_mlp_w_down, m_final_norm, v_dn_norm, v_dn_w_in, v_dn_w_conv, v_dn_a_log, v_dn_dt_bias, v_dn_out_norm, v_dn_w_out, v_cv_norm, v_cv_w_pw1, v_cv_b_pw1, v_cv_w_dw, v_cv_b_dw, v_cv_ln_g, v_cv_ln_b, v_cv_w_pw2, v_cv_b_pw2, v_xa_norm, v_xa_mem_norm, v_xa_w_q, v_xa_w_kv, v_xa_w_o, v_mlp_norm, v_mlp_w_up, v_mlp_w_down, v_final_norm):
    args = dict(locals())
    wts = {nm: args[nm] for nm in WEIGHTS}
    mom = {nm: args["m_" + nm] for nm in WEIGHTS}
    var = {nm: args["v_" + nm] for nm in WEIGHTS}
    core = lax.axis_index("c").astype(jnp.int32).reshape(1)
    chip = (2 * lax.axis_index("x") + lax.axis_index("y")).astype(jnp.int32)
    def own_slab(got, src):
        return lax.dynamic_update_slice(got, src[None], (chip, 0, 0))

    shard2d = {nm: wts[nm].astype(BF16).reshape(-1, wts[nm].shape[-1]) for nm in BIG}
    first = ["dn_w_in", "dn_w_out"]
    later = [nm for nm in BIG if nm not in first]
    sources = [shard2d[nm] for nm in first] + [_pack(wts, SMALL, F32, SMALL_ROW_MULTIPLE)[0]]
    gathered = [own_slab(got, src) for got, src in zip(gather_shards(sources), sources)]
    stacked = {"dn_w_out": gathered[1].reshape((N_CHIPS,) + SHARD_SHAPES["dn_w_out"])}
    full = {nm: _to_full(nm, t) for nm, t in _unpack(gathered[2], SMALL).items()}
    full["dn_w_in"] = _to_full("dn_w_in", gathered[0].reshape((N_CHIPS,) + SHARD_SHAPES["dn_w_in"]))
    full.update({nm: wts[nm] for nm in REPLICATED})
    later_src = [shard2d[nm] for nm in later]
    g_send, g_recv, later_src, g_lands, started = ici_start(
        "gather_start", "gather", later_src, [(N_CHIPS,) + s.shape for s in later_src])
    full["dn_norm"] = full["dn_norm"] + started[0, 0]

    def rest_weights(after):
        srcs, lands = ici_wait("gather_wait", "gather", g_send, g_recv, later_src, g_lands, after)
        return {nm: own_slab(land, src).reshape((N_CHIPS,) + SHARD_SHAPES[nm])
                for nm, land, src in zip(later, pair_forward(lands), srcs)}

    pending = []

    def on_grads(items):
        tag = "_".join(sorted({str(layer) for _, layer, _ in items}))
        parts = [g.reshape(N_CHIPS, -1, g.shape[-1]) for _, _, g in items]
        theirs = pair_split("pair_split_" + tag, parts)
        pairs = [add_pairs("pair_add_%s%d" % (nm, layer), p, t, core, BF16)
                 for (nm, layer, _), p, t in zip(items, parts, theirs)]
        send, recv, pairs, lands, token = ici_start(
            "scatter_start_" + tag, "scatter", pairs, [(N_CHIPS - 1,) + p.shape[1:] for p in pairs])
        pending.append((tag, items, send, recv, pairs, lands))
        return token

    dh, grads, rep = local_step(x[0], mem[0], loss_target[0], stacked, full, rest_weights, on_grads)

    halves = {}
    last = [("dn_w_in", 0, _to_shards("dn_w_in", grads["dn_w_in"]).astype(BF16)), ("dn_w_out", 0, grads["dn_w_out"][0]),
            ("small", 0, _pack({nm: _to_shards(nm, grads[nm]) for nm in SMALL}, SMALL, F32, SMALL_ROW_MULTIPLE))]
    parts = [g.reshape(N_CHIPS, -1, g.shape[-1]) for _, _, g in last]
    theirs = pair_split("pair_split_last", parts)
    pairs = [add_pairs("pair_add_" + nm, p, t, core, p.dtype) for (nm, _, _), p, t in zip(last, parts, theirs)]
    l_send, l_recv, l_pairs, l_lands, _ = ici_start(
        "scatter_start_last", "scatter", pairs, [(N_CHIPS - 1,) + p.shape[1:] for p in pairs])
    for tag, items, send, recv, pairs, lands in pending:
        pairs, lands = ici_wait("scatter_wait_" + tag, "scatter", send, recv, pairs, lands, dh)
        for (nm, layer, _), p, o in zip(items, pairs, lands):
            halves[nm, layer] = add_four("chip_add_%s%d" % (nm, layer), p, o, chip.reshape(1))
    keys = sorted(halves)
    siblings = dict(zip(keys, pair_join("pair_join_early", [halves[k] for k in keys])))

    delta, new_m, new_v, red = {}, {}, {}, {}

    def big_adamw(nm):
        layers = range(wts[nm].shape[0])
        res = adamw_halves("adamw_" + nm, _as_2d(wts[nm]), [halves[nm, l] for l in layers],
                           [siblings[nm, l] for l in layers], _as_2d(mom[nm]), _as_2d(var[nm]), core)
        red[nm], delta[nm], new_m[nm], new_v[nm] = (r.reshape(wts[nm].shape) for r in res)

    early = [nm for nm in BIG if (nm, 0) in halves]
    for nm in early:
        big_adamw(nm)
    l_pairs, l_lands = ici_wait("scatter_wait_last", "scatter", l_send, l_recv, l_pairs, l_lands, new_v[early[-1]])
    for (nm, layer, _), p, o in zip(last, l_pairs, l_lands):
        halves[nm, layer] = add_four("chip_add_" + nm, p, o, chip.reshape(1))
    keys = [(nm, layer) for nm, layer, _ in last]
    siblings.update(zip(keys, pair_join("pair_join_last", [halves[k] for k in keys])))
    south = core[0] == 0
    mine, theirs = halves["small", 0], siblings["small", 0]
    red.update(_unpack(jnp.concatenate([jnp.where(south, mine, theirs), jnp.where(south, theirs, mine)], axis=0),
                       SMALL))

    rep = all_sum_small(rep)
    red["dn_norm"] = rep[0:1]
    red["dn_a_log"] = rep[1:2, DN_HEADS:2 * DN_HEADS]
    red["dn_dt_bias"] = rep[2:3, DN_HEADS:2 * DN_HEADS]
    red["dn_out_norm"] = rep[3:4, :LANES]
    red["xa_norm"], red["xa_mem_norm"], red["mlp_norm"] = rep[4:6], rep[6:8], rep[8:10]
    red["final_norm"] = rep[10]
    loss = rep[11, 0]

    for nm in WEIGHTS:
        shp = wts[nm].shape
        if nm in early:
            continue
        if nm in BIG:
            big_adamw(nm)
            continue
        res = adamw("adamw_" + nm, _as_2d(wts[nm]), _as_2d(red[nm].reshape(shp)), _as_2d(mom[nm]), _as_2d(var[nm]))
        delta[nm], new_m[nm], new_v[nm] = (r.reshape(shp) for r in res)
        red[nm] = red[nm].reshape(shp)

    grad_x = dh[None]
    return (loss, grad_x, *[red[nm] for nm in WEIGHTS], *[delta[nm] for nm in WEIGHTS],
            *[new_m[nm] for nm in WEIGHTS], *[new_v[nm] for nm in WEIGHTS])


def local_step(h0, mem0, target, stacked, full, rest_weights=None, on_grads=None):
    d = h0.shape[1]
    dn_norm, dn_a_log, dn_dt_bias, dn_out_norm = (full[nm] for nm in REPLICATED[:4])
    xa_norm, xa_mem_norm, mlp_norm, final_norm = (full[nm] for nm in REPLICATED[4:])
    inner = DN_HEADS * DN_HEAD_DIM
    w_in = full["dn_w_in"][0]
    w_qkv, w_z = w_in[:, :3 * inner], w_in[:, 3 * inner:4 * inner]
    w_ba = jnp.pad(w_in[:, 4 * inner:], ((0, 0), (0, LANES - 2 * DN_HEADS)))
    w_conv = jnp.pad(full["dn_w_conv"][0], ((0, 8 - DN_CONV), (0, 0)))
    gate = _gate_tile(dn_a_log, dn_dt_bias)
    w_dw = jnp.pad(full["cv_w_dw"][0], ((0, CV_HALO - CV_WIDTH), (0, 0)))

    def sw(nm, layer):
        return Stacked(stacked[nm], "rows" if SHARD_AXIS[nm] == 1 else "cols", layer)

    dn_args = (_row(dn_norm), w_qkv, w_z, w_ba, w_conv, gate, _row(dn_out_norm), sw("dn_w_out", 0))
    h1, dn_saved = dn_fwd(h0, *dn_args)
    if rest_weights is not None:
        stacked = {**stacked, **rest_weights(h1)}
    xa_args = [(_row(xa_norm[l]), _row(xa_mem_norm[l]), sw("xa_w_q", l), sw("xa_w_kv", l), sw("xa_w_o", l))
               for l in range(2)]
    mlp_args = [(_row(mlp_norm[l]), sw("mlp_w_up", l), sw("mlp_w_down", l)) for l in range(2)]
    h2, xa0_saved = xa_fwd("xa0", h1, mem0, *xa_args[0])
    h3, mlp0_saved = mlp_fwd("mlp0", h2, *mlp_args[0])
    cv_args = (_row(full["cv_norm"][0]), sw("cv_w_pw1", 0), full["cv_b_pw1"], w_dw, full["cv_b_dw"],
               full["cv_ln_g"], full["cv_ln_b"], sw("cv_w_pw2", 0), full["cv_b_pw2"])
    h4, cv_saved = cv_fwd(h3, *cv_args)
    h5, xa1_saved = xa_fwd("xa1", h4, mem0, *xa_args[1])
    h6, mlp1_saved = mlp_fwd("mlp1", h5, *mlp_args[1])

    dh32, dh16, loss_tile, d_final = loss_head("loss_head", h6, _row(final_norm), target)
    dh = (dh32, dh16)
    grads = {}
    dg_mlp, dg_xa, dg_xa_mem = [None, None], [None, None], [None, None]
    dw_mlp, dw_xa = [None, None], [None, None]
    mlp_names, xa_names = ("mlp_w_up", "mlp_w_down"), ("xa_w_q", "xa_w_kv", "xa_w_o")

    def announce(items):
        return None if on_grads is None else on_grads(items)

    dh, dg_mlp[1], dw_mlp[1] = mlp_bwd("mlp1", dh, h5, *mlp_args[1], mlp1_saved)
    dh, dg_xa[1], dg_xa_mem[1], dw_xa[1] = xa_bwd("xa1", dh, h4, mem0, *xa_args[1], xa1_saved)
    (dh, grads["cv_norm"], dw_pw1, grads["cv_b_pw1"], dw_dw, ln_acc, dw_pw2,
     grads["cv_b_pw2"]) = cv_bwd(dh, h3, cv_args[0], cv_args[1], w_dw, cv_args[5], cv_args[6], cv_args[7], cv_saved)
    after = announce([(nm, 1, g) for nm, g in zip(mlp_names + xa_names, dw_mlp[1] + dw_xa[1])]
                     + [("cv_w_pw1", 0, dw_pw1), ("cv_w_pw2", 0, dw_pw2)])
    dh, dg_mlp[0], dw_mlp[0] = mlp_bwd("mlp0", dh, h2, *mlp_args[0], mlp0_saved, after=after)
    dh, dg_xa[0], dg_xa_mem[0], dw_xa[0] = xa_bwd("xa0", dh, h1, mem0, *xa_args[0], xa0_saved)
    after = announce([(nm, 0, g) for nm, g in zip(mlp_names + xa_names, dw_mlp[0] + dw_xa[0])])
    dh, dg_dn, dw_qkv, dw_z, dw_ba, dw_conv, d_gate, d_out_norm, dw_out = dn_bwd(dh, h0, *dn_args, dn_saved,
                                                                                 after=after)

    grads["dn_w_in"] = jnp.concatenate([dw_qkv, dw_z, dw_ba[:, :2 * DN_HEADS]], axis=1)[None]
    grads["dn_w_conv"] = dw_conv[None, :DN_CONV]
    grads["dn_w_out"], grads["cv_w_pw1"], grads["cv_w_pw2"] = [dw_out], [dw_pw1], [dw_pw2]
    grads["cv_w_dw"] = dw_dw[None, :CV_WIDTH]
    grads["cv_ln_g"], grads["cv_ln_b"], grads["cv_b_dw"] = ln_acc[0:1], ln_acc[1:2], ln_acc[2:3]
    for i, nm in enumerate(mlp_names):
        grads[nm] = [dw_mlp[0][i], dw_mlp[1][i]]
    for i, nm in enumerate(xa_names):
        grads[nm] = [dw_xa[0][i], dw_xa[1][i]]

    rep = jnp.zeros((16, d), F32)
    rep = rep.at[0].set(dg_dn[0])
    rep = rep.at[1, :LANES].set(d_gate[0])
    rep = rep.at[2, :LANES].set(d_gate[1])
    rep = rep.at[3, :LANES].set(d_out_norm[0])
    rep = rep.at[4].set(dg_xa[0][0]).at[5].set(dg_xa[1][0])
    rep = rep.at[6].set(dg_xa_mem[0][0]).at[7].set(dg_xa_mem[1][0])
    rep = rep.at[8].set(dg_mlp[0][0]).at[9].set(dg_mlp[1][0])
    rep = rep.at[10].set(d_final[0])
    rep = rep.at[11, :LANES].set(loss_tile[0])
    return dh, grads, rep
```

```python
import functools

import jax
import jax.numpy as jnp
from jax import lax
from jax.experimental import pallas as pl
from jax.experimental.pallas import tpu as pltpu

F32 = jnp.float32
BF16 = jnp.bfloat16
HIGHEST = lax.Precision.HIGHEST
MESH = pl.DeviceIdType.MESH

D_MODEL = 1024
DN_HEADS = 8
DN_HEAD_DIM = 128
DN_CONV = 4
DN_CHUNK = 64
CV_WIDTH = 31
XA_HEADS = 4
XA_HEAD_DIM = 256
RMS_EPS = 1e-6
LN_EPS = 1e-5
L2_EPS = 1e-6

ADAM_LR = 0.001
ADAM_B1 = 0.9
ADAM_B2 = 0.999
ADAM_EPS = 1e-08
ADAM_WD = 0.01
ADAM_STEP = 10

LANES = 128
ROW_TILE = 512
CONV_ROW_TILE = 256
MM_TILE = 1024
LONG_TILE_K = 2048
ADAMW_ROW_TILE = 256
DN_ROW_TILE = 256
CHUNK_SHIFT = 6
SOLVE_INTERLEAVE = 8
FWD_HEADS_PER_STEP = 8
BWD_HEADS_PER_STEP = 8
BWD_SCAN_ROWS = 256
DN_HALO = 8
CV_HALO = 32
VMEM_LIMIT = 48 * 1024 * 1024
N_CHIPS = 4
D2D_CHUNK_ROWS = 256


def _cparams(sem):
    return pltpu.CompilerParams(dimension_semantics=sem, vmem_limit_bytes=VMEM_LIMIT)


def _dot(a, b, dims=(((1,), (0,)), ((), ()))):
    return lax.dot_general(a.astype(BF16), b.astype(BF16), dims, preferred_element_type=F32)


def _dot_nt(a, b):
    return _dot(a, b, (((1,), (1,)), ((), ())))


def _dot_tn(a, b):
    return _dot(a, b, (((0,), (0,)), ((), ())))


def _dot_hi(a, b, dims=(((1,), (0,)), ((), ()))):
    return lax.dot_general(a.astype(F32), b.astype(F32), dims, precision=HIGHEST, preferred_element_type=F32)


def _dot_x3(a, b, dims=(((1,), (0,)), ((), ()))):
    a_hi, b_hi = a.astype(BF16), b.astype(BF16)
    a_lo = (a - a_hi.astype(F32)).astype(BF16)
    b_lo = (b - b_hi.astype(F32)).astype(BF16)

    def dot(p, q):
        return lax.dot_general(p, q, dims, preferred_element_type=F32)

    return dot(a_hi, b_hi) + (dot(a_hi, b_lo) + dot(a_lo, b_hi))


def _sigmoid(x):
    return 1.0 / (1.0 + jnp.exp(-x))


def _silu(x):
    return x * _sigmoid(x)


def _silu_grad(x):
    s = _sigmoid(x)
    return s * (1.0 + x * (1.0 - s))


def _softplus(x):
    return jnp.maximum(x, 0.0) + jnp.log(1.0 + jnp.exp(-jnp.abs(x)))


def _iota(shape, dim):
    return lax.broadcasted_iota(jnp.int32, shape, dim)


def _lane_col(vals, lane, idx):
    return jnp.sum(jnp.where(lane == idx, vals, 0.0), axis=1, keepdims=True)


def _pick_tile(rows, cap):
    best = rows
    for t in range(16, min(rows, cap) + 1, 16):
        if rows % t == 0:
            best = t
    return best


def _stacked_spec(shape, split, layer, rows, cols, block_index):
    r_shard, c_shard = shape[-2], shape[-1]
    if split == "rows" and rows > r_shard:
        assert rows % r_shard == 0 and c_shard % cols == 0
        chips = rows // r_shard

        def slabs(i, j, kk):
            bi, bj = block_index(i, j, kk)
            return (bi, layer, 0, bj)

        return pl.BlockSpec((chips, None, r_shard, cols), slabs), chips
    assert r_shard % rows == 0 and c_shard % cols == 0
    per_chip = (r_shard // rows) if split == "rows" else (c_shard // cols)

    def index(i, j, kk):
        bi, bj = block_index(i, j, kk)
        if split == "rows":
            return (bi // per_chip, layer, bi % per_chip, bj)
        return (bj // per_chip, layer, bi, bj % per_chip)

    return pl.BlockSpec((None, None, rows, cols), index), 1


def mm(name, a, b, *, ta=False, tb=False, out_dtype=F32, pro=None, epi=None, epi_tiles=(), epi_rows=(),
       tm=MM_TILE, tn=MM_TILE, tk=MM_TILE, b_split=None, b_layer=None, out_split=None, out_layer=None,
       after=None, norm_gain=None):
    m, k = (a.shape[1], a.shape[0]) if ta else a.shape
    b_rows, b_cols = b.shape[-2], b.shape[-1]
    if b_split == "rows":
        b_rows *= N_CHIPS
    elif b_split == "cols":
        b_cols *= N_CHIPS
    n = b_rows if tb else b_cols
    assert (b_cols if tb else b_rows) == k
    tm, tn, tk = min(tm, m), min(tn, n), min(tk, k)
    if b_split == "cols":
        if tb:
            tk = min(tk, b.shape[-1])
        else:
            tn = min(tn, b.shape[-1])
    if out_split == "cols":
        tn = min(tn, n // N_CHIPS)
    assert m % tm == 0 and n % tn == 0 and k % tk == 0
    nk = k // tk
    a_spec = pl.BlockSpec((tk, tm), lambda i, j, kk: (kk, i)) if ta else pl.BlockSpec((tm, tk), lambda i, j, kk: (i, kk))
    b_block = (tn, tk) if tb else (tk, tn)
    b_index = (lambda i, j, kk: (j, kk)) if tb else (lambda i, j, kk: (kk, j))
    b_chips = o_chips = 1
    if b_split is None:
        b_spec = pl.BlockSpec(b_block, b_index)
    else:
        b_spec, b_chips = _stacked_spec(b.shape, b_split, b_layer, b_block[0], b_block[1], b_index)
    in_specs = [a_spec, b_spec]
    in_specs += [pl.BlockSpec((tm, tn), lambda i, j, kk: (i, j)) for _ in epi_tiles]
    in_specs += [pl.BlockSpec((1, tn), lambda i, j, kk: (0, j)) for _ in epi_rows]
    n_t, n_r = len(epi_tiles), len(epi_rows)
    dims = (((0 if ta else 1,), (1 if tb else 0,)), ((), ()))
    if out_split is None:
        out_shape = jax.ShapeDtypeStruct((m, n), out_dtype)
        out_spec = pl.BlockSpec((tm, tn), lambda i, j, kk: (i, j))
    else:
        shard = (m // N_CHIPS, n) if out_split == "rows" else (m, n // N_CHIPS)
        out_shape = jax.ShapeDtypeStruct((N_CHIPS, out_layer[1]) + shard, out_dtype)
        out_spec, o_chips = _stacked_spec(out_shape.shape, out_split, out_layer[0], tm, tn, lambda i, j, kk: (i, j))
    extra = []
    if norm_gain is not None:
        assert tn == n and out_split is None
        extra.append(norm_gain)
        in_specs.append(pl.BlockSpec((1, n), lambda i, j, kk: (0, 0)))
        out_shape = [out_shape, jax.ShapeDtypeStruct((m, n), BF16)]
        out_spec = [out_spec, pl.BlockSpec((tm, tn), lambda i, j, kk: (i, j))]
    if after is not None:
        extra.append(after)
        in_specs.append(pl.BlockSpec(memory_space=pl.ANY))

    def body(a_ref, b_ref, *rest):
        tiles = rest[:n_t]
        rows = rest[n_t:n_t + n_r]
        gain_ref = rest[n_t + n_r] if norm_gain is not None else None
        rest = rest[n_t + n_r + len(extra):]
        o_ref, acc_ref = rest[0], rest[-1]
        kk = pl.program_id(2)

        @pl.when(kk == 0)
        def _():
            acc_ref[...] = jnp.zeros_like(acc_ref)

        av = a_ref[...]
        if pro is not None:
            av = pro(av)
        bv = b_ref[...]
        if b_chips > 1:
            bv = bv.reshape(b_block)
        acc_ref[...] += _dot(av, bv, dims)

        @pl.when(kk == nk - 1)
        def _():
            out = acc_ref[...]
            if epi is not None:
                out = epi(out, *[t[...] for t in tiles], *[r[...] for r in rows])
            if gain_ref is not None:
                rest[1][...] = (_rms_stats(out)[0] * gain_ref[...]).astype(BF16)
            out = out.astype(out_dtype)
            o_ref[...] = out.reshape(o_chips, tm // o_chips, tn) if o_chips > 1 else out

    return pl.pallas_call(
        body, name=name, grid=(m // tm, n // tn, nk),
        in_specs=in_specs, out_specs=out_spec, out_shape=out_shape,
        scratch_shapes=[pltpu.VMEM((tm, tn), F32)],
        compiler_params=_cparams(("parallel", "parallel", "arbitrary")),
    )(a, b, *epi_tiles, *epi_rows, *extra)


def row_call(name, body, n_rows, tm, ins, outs, accs=()):
    tm = _pick_tile(n_rows, tm)
    in_specs = []
    for arr, kind in ins:
        if kind == "tile":
            if arr.ndim == 2:
                in_specs.append(pl.BlockSpec((tm, arr.shape[1]), lambda i: (i, 0)))
            else:
                in_specs.append(pl.BlockSpec((arr.shape[0], tm, arr.shape[2]), lambda i: (0, i, 0)))
        elif kind == "full":
            in_specs.append(pl.BlockSpec(arr.shape, functools.partial(lambda i, nd: (0,) * nd, nd=arr.ndim)))
        else:
            where, h = kind
            per = tm // h
            if where == "prev":
                in_specs.append(pl.BlockSpec((h, arr.shape[1]), functools.partial(
                    lambda i, per: (jnp.maximum(i * per - 1, 0), 0), per=per)))
            else:
                last = n_rows // h - 1
                in_specs.append(pl.BlockSpec((h, arr.shape[1]), functools.partial(
                    lambda i, per, last: (jnp.minimum((i + 1) * per, last), 0), per=per, last=last)))
    out_shape, out_specs = [], []
    for shape, dtype in outs:
        out_shape.append(jax.ShapeDtypeStruct(shape, dtype))
        if len(shape) == 2:
            out_specs.append(pl.BlockSpec((tm, shape[1]), lambda i: (i, 0)))
        else:
            out_specs.append(pl.BlockSpec((shape[0], tm, shape[2]), lambda i: (0, i, 0)))
    for shape in accs:
        out_shape.append(jax.ShapeDtypeStruct(shape, F32))
        out_specs.append(pl.BlockSpec(shape, lambda i: (0, 0)))
    n_in, n_out, n_acc = len(ins), len(outs), len(accs)

    def kern(*refs):
        i = pl.program_id(0)
        in_refs = refs[:n_in]
        out_refs = refs[n_in:n_in + n_out]
        acc_refs = refs[n_in + n_out:n_in + n_out + n_acc]
        if n_acc:
            @pl.when(i == 0)
            def _():
                for r in acc_refs:
                    r[...] = jnp.zeros_like(r)
        body(i, in_refs, out_refs, acc_refs)

    res = pl.pallas_call(
        kern, name=name, grid=(n_rows // tm,), in_specs=in_specs, out_specs=out_specs, out_shape=out_shape,
        compiler_params=_cparams(("arbitrary",) if n_acc else ("parallel",)),
    )(*[a for a, _ in ins])
    return list(res)


def _rms_stats(h):
    r = lax.rsqrt(jnp.mean(h * h, axis=-1, keepdims=True) + RMS_EPS)
    return h * r, r


def rms_fwd(name, h, g):
    def body(i, ins, outs, accs):
        xhat, _ = _rms_stats(ins[0][...])
        outs[0][...] = (xhat * ins[1][...]).astype(BF16)

    return row_call(name, body, h.shape[0], ROW_TILE, [(h, "tile"), (g, "full")], [(h.shape, BF16)])[0]


def _rms_bwd_tile(dn, h, g):
    xhat, r = _rms_stats(h)
    dxhat = dn * g
    dh = r * (dxhat - xhat * jnp.mean(dxhat * xhat, axis=-1, keepdims=True))
    dg = jnp.sum(dn * xhat, axis=0, keepdims=True)
    return dh, dg


def rms_bwd(name, dn, h, g, dres):
    def body(i, ins, outs, accs):
        dh, dg = _rms_bwd_tile(ins[0][...].astype(F32), ins[1][...], ins[2][...])
        total = ins[3][...] + dh
        outs[0][...] = total
        outs[1][...] = total.astype(BF16)
        accs[0][...] += dg

    d = h.shape[1]
    out, out16, dg = row_call(name, body, h.shape[0], ROW_TILE,
                              [(dn, "tile"), (h, "tile"), (g, "full"), (dres, "tile")],
                              [(h.shape, F32), (h.shape, BF16)], [(1, d)])
    return out, out16, dg


def mem_norm_bwd(name, dn, mem, g):
    def body(i, ins, outs, accs):
        _, dg = _rms_bwd_tile(ins[0][...].astype(F32), ins[1][...], ins[2][...])
        accs[0][...] += dg

    return row_call(name, body, mem.shape[0], ROW_TILE, [(dn, "tile"), (mem, "tile"), (g, "full")], [],
                    [(1, mem.shape[1])])[0]


def loss_head(name, h, g, target):
    d = h.shape[1]

    def body(i, ins, outs, accs):
        hv, gv = ins[0][...], ins[1][...]
        xhat, _ = _rms_stats(hv)
        err = xhat * gv - ins[2][...]
        dy = err * (1.0 / d)
        dh, dg = _rms_bwd_tile(dy, hv, gv)
        outs[0][...] = dh
        outs[1][...] = dh.astype(BF16)
        accs[0][...] += jnp.full((8, LANES), 0.5 / d, F32) * jnp.sum(err * err)
        accs[1][...] += dg

    dh, dh16, loss, dg = row_call(name, body, h.shape[0], ROW_TILE, [(h, "tile"), (g, "full"), (target, "tile")],
                                  [(h.shape, F32), (h.shape, BF16)], [(8, LANES), (1, d)])
    return dh, dh16, loss, dg


def col_sum(name, x):
    def body(i, ins, outs, accs):
        accs[0][...] += jnp.sum(ins[0][...].astype(F32), axis=0, keepdims=True)

    return row_call(name, body, x.shape[0], ROW_TILE, [(x, "tile")], [], [(1, x.shape[1])])[0]


def _conv_taps(xcat, w_ref, cols, width, halo, tm):
    rows = halo + tm
    acc = None
    for j in range(width):
        s = width - 1 - j
        xs = xcat if s == 0 else pltpu.roll(xcat, s, 0)
        term = xs[halo:rows] * w_ref[j:j + 1, cols]
        acc = term if acc is None else acc + term
    return acc


def _conv_taps_bwd_x(dcat, w_ref, cols, width, halo, tm):
    rows = halo + tm
    acc = None
    for j in range(width):
        s = width - 1 - j
        ds = dcat if s == 0 else pltpu.roll(dcat, rows - s, 0)
        term = ds[0:tm] * w_ref[j:j + 1, cols]
        acc = term if acc is None else acc + term
    return acc


def _conv_taps_bwd_w(dy, xcat, width, halo, tm, wrows):
    rows = halo + tm
    rid = _iota((wrows, dy.shape[1]), 0)
    out = jnp.zeros((wrows, dy.shape[1]), F32)
    for j in range(width):
        s = width - 1 - j
        xs = xcat if s == 0 else pltpu.roll(xcat, s, 0)
        v = jnp.sum(dy * xs[halo:rows], axis=0, keepdims=True)
        out = out + jnp.where(rid == j, v, 0.0)
    return out


def dn_pre(qkv_raw, ba, w_conv, gate):
    s_len = qkv_raw.shape[0]
    tm = min(DN_ROW_TILE, s_len)
    n_blk = qkv_raw.shape[1] // LANES

    def body(i, ins, outs, accs):
        x_ref, xp_ref, ba_ref, w_ref, gate_ref = ins
        qkv_ref, hs_ref = outs

        def blk(cb, carry):
            cols = pl.ds(pl.multiple_of(cb * LANES, LANES), LANES)
            prev = jnp.where(i > 0, xp_ref[:, cols], 0.0)
            xcat = jnp.concatenate([prev, x_ref[:, cols]], axis=0)
            c = _conv_taps(xcat, w_ref, cols, DN_CONV, DN_HALO, tm)
            y = _silu(c)
            rs = lax.rsqrt(jnp.sum(y * y, axis=-1, keepdims=True) + L2_EPS)
            fac = jnp.where(cb < DN_HEADS, DN_HEAD_DIM ** -0.5, 1.0)
            qkv_ref[:, cols] = jnp.where(cb < 2 * DN_HEADS, y * (rs * fac), y)
            return carry

        lax.fori_loop(0, n_blk, blk, 0)

        bav = ba_ref[...]
        beta = _sigmoid(bav)
        g = -jnp.exp(gate_ref[0:1, :]) * _softplus(bav + gate_ref[1:2, :])
        lane = _iota((tm, LANES), 1)
        g = jnp.where((lane >= DN_HEADS) & (lane < 2 * DN_HEADS), g, 0.0)
        r = _iota((tm, tm), 0)
        c = _iota((tm, tm), 1)
        tri = jnp.where((r >= c) & ((r >> CHUNK_SHIFT) == (c >> CHUNK_SHIFT)), 1.0, 0.0)
        gc = _dot_hi(tri, g)
        for h in range(DN_HEADS):
            hs_ref[h] = jnp.where(lane == 0, _lane_col(beta, lane, h),
                                  jnp.where(lane == 1, _lane_col(g, lane, DN_HEADS + h),
                                            jnp.where(lane == 2, _lane_col(gc, lane, DN_HEADS + h), 0.0)))

    return row_call("dn_pre", body, s_len, tm,
                    [(qkv_raw, "tile"), (qkv_raw, ("prev", DN_HALO)), (ba, "tile"), (w_conv, "full"), (gate, "full")],
                    [(qkv_raw.shape, F32), ((DN_HEADS, s_len, LANES), F32)])


def _chunk_masks():
    r = _iota((DN_CHUNK, DN_CHUNK), 0)
    c = _iota((DN_CHUNK, DN_CHUNK), 1)
    return r, c


def _decay_matrix(gc, r, c):
    lane = _iota((DN_CHUNK, LANES), 1)
    a = jnp.where(lane == 0, gc, jnp.where(lane == 1, 1.0, 0.0))
    b = jnp.where(lane == 0, 1.0, jnp.where(lane == 1, -gc, 0.0))
    diff = _dot_hi(a, b, (((1,), (1,)), ((), ())))
    causal = r >= c
    return jnp.where(causal, jnp.exp(jnp.where(causal, diff, 0.0)), 0.0)


def _tri_inverse(lows, r, c):
    eye = jnp.where(r == c, 1.0, 0.0)
    ts = [eye for _ in lows]
    b = 1
    while b < DN_CHUNK:
        shift = b.bit_length()
        sel = ((r >> shift) == (c >> shift)) & ((r & b) != 0) & ((c & b) == 0)
        lms = [jnp.where(sel, low, 0.0) for low in lows]
        if b == 1:
            ts = [t - lm for t, lm in zip(ts, lms)]
        else:
            t_lm = [_dot_x3(t, lm) for t, lm in zip(ts, lms)]
            t_lm_t = [_dot_x3(x, t) for x, t in zip(t_lm, ts)]
            ts = [t - x for t, x in zip(ts, t_lm_t)]
        b *= 2
    return ts


def dn_solve(qkv, hs):
    s_len = qkv.shape[0]
    rb = min(ROW_TILE, s_len)
    n_chunk = rb // DN_CHUNK
    interleave = min(SOLVE_INTERLEAVE, n_chunk)

    def body(k_ref, v_ref, hs_ref, u_ref, w_ref, t_ref):
        r, c = _chunk_masks()

        def group(gi, carry):
            rows = [pl.ds(pl.multiple_of((gi * interleave + j) * DN_CHUNK, DN_CHUNK), DN_CHUNK)
                    for j in range(interleave)]
            k = [k_ref[rw, :] for rw in rows]
            beta = [hs_ref[rw, 0:1] for rw in rows]
            gc = [hs_ref[rw, 2:3] for rw in rows]
            kb = [a * b for a, b in zip(k, beta)]
            decay = [_decay_matrix(g, r, c) for g in gc]
            lows = [jnp.where(r > c, _dot_nt(a, b) * d, 0.0) for a, b, d in zip(kb, k, decay)]
            ts = _tri_inverse(lows, r, c)
            us = [_dot_x3(t, v_ref[rw, :] * b) for t, rw, b in zip(ts, rows, beta)]
            ws = [_dot_x3(t, a * jnp.exp(g)) for t, a, g in zip(ts, kb, gc)]
            for j, rw in enumerate(rows):
                u_ref[rw, :] = us[j]
                w_ref[rw, :] = ws[j].astype(BF16)
                t_ref[rw, :] = ts[j]
            return carry

        lax.fori_loop(0, n_chunk // interleave, group, 0)

    return pl.pallas_call(
        body, name="dn_solve", grid=(DN_HEADS, s_len // rb),
        in_specs=[pl.BlockSpec((rb, LANES), lambda h, i: (i, DN_HEADS + h)),
                  pl.BlockSpec((rb, LANES), lambda h, i: (i, 2 * DN_HEADS + h)),
                  pl.BlockSpec((None, rb, LANES), lambda h, i: (h, i, 0))],
        out_specs=[pl.BlockSpec((rb, LANES), lambda h, i: (i, h)),
                   pl.BlockSpec((rb, LANES), lambda h, i: (i, h)),
                   pl.BlockSpec((None, rb, DN_CHUNK), lambda h, i: (h, i, 0))],
        out_shape=[jax.ShapeDtypeStruct((s_len, DN_HEADS * LANES), F32),
                   jax.ShapeDtypeStruct((s_len, DN_HEADS * LANES), BF16),
                   jax.ShapeDtypeStruct((DN_HEADS, s_len, DN_CHUNK), F32)],
        compiler_params=_cparams(("parallel", "parallel")),
    )(qkv, qkv, hs)


def dn_scan_fwd(qkv, u, w, hs):
    s_len = qkv.shape[0]
    rb = min(ROW_TILE, s_len)
    n_chunk = rb // DN_CHUNK
    total_chunks = s_len // DN_CHUNK

    hps = FWD_HEADS_PER_STEP
    groups = DN_HEADS // hps

    def body(q_ref, k_ref, u_ref, w_ref, hs_ref, o_ref, st_ref, state):
        @pl.when(pl.program_id(1) == 0)
        def _():
            state[...] = jnp.zeros_like(state)

        r, c = _chunk_masks()

        def chunk(n, carry):
            rows = pl.ds(pl.multiple_of(n * DN_CHUNK, DN_CHUNK), DN_CHUNK)
            heads = range(hps)
            cols = [slice(h * LANES, (h + 1) * LANES) for h in heads]
            each = lambda f, *xs: [f(*a) for a in zip(*xs)]
            q = [q_ref[rows, cl] for cl in cols]
            k = [k_ref[rows, cl] for cl in cols]
            gc = [hs_ref[h, rows, 2:3] for h in heads]
            st = [state[h] for h in heads]
            for h in heads:
                st_ref[h, n] = st[h]
            gl = each(lambda g: jnp.min(g, axis=0, keepdims=True), gc)
            decay = each(lambda g: _decay_matrix(g, r, c), gc)
            w_st = [_dot(w_ref[rows, cols[h]], st[h]) for h in heads]
            qk = each(_dot_nt, q, k)
            q_st = each(lambda a, g, s: _dot(a * jnp.exp(g), s), q, gc, st)
            vn = [u_ref[rows, cols[h]] - w_st[h] for h in heads]
            ai_vn = each(lambda a, d, b: _dot(a * d, b), qk, decay, vn)
            kd_vn = each(lambda a, g0, g, b: _dot_tn(a * jnp.exp(g0 - g), b), k, gl, gc, vn)
            for h in heads:
                o_ref[rows, cols[h]] = q_st[h] + ai_vn[h]
                state[h] = st[h] * jnp.exp(gl[h]) + kd_vn[h]
            return carry

        lax.fori_loop(0, n_chunk, chunk, 0)

    wide = hps * LANES
    blk = lambda off: pl.BlockSpec((rb, wide), lambda h, i: (i, off + h))
    return pl.pallas_call(
        body, name="dn_scan_fwd", grid=(groups, s_len // rb),
        in_specs=[blk(0), blk(groups), blk(0), blk(0),
                  pl.BlockSpec((hps, rb, LANES), lambda h, i: (h, i, 0))],
        out_specs=[blk(0),
                   pl.BlockSpec((hps, n_chunk, LANES, LANES), lambda h, i: (h, i, 0, 0))],
        out_shape=[jax.ShapeDtypeStruct((s_len, DN_HEADS * LANES), F32),
                   jax.ShapeDtypeStruct((DN_HEADS, total_chunks, LANES, LANES), F32)],
        scratch_shapes=[pltpu.VMEM((hps, LANES, LANES), F32)],
        compiler_params=_cparams(("parallel", "arbitrary")),
    )(qkv, qkv, u, w, hs)


def dn_scan_bwd(qkv, u, w, t_inv, hs, states, d_o):
    s_len = qkv.shape[0]
    rb = min(BWD_SCAN_ROWS, s_len)
    n_chunk = rb // DN_CHUNK
    n_blk = s_len // rb
    hps = BWD_HEADS_PER_STEP
    groups = DN_HEADS // hps

    def body(q_ref, k_ref, v_ref, u_ref, w_ref, t_ref, hs_ref, st_ref, do_ref,
             dq_ref, dk_ref, dv_ref, dhs_ref, dstate):
        @pl.when(pl.program_id(1) == 0)
        def _():
            dstate[...] = jnp.zeros_like(dstate)

        r, c = _chunk_masks()
        causal = r >= c
        strict = r > c
        lane = _iota((DN_CHUNK, LANES), 1)
        upper = jnp.where(r <= c, 1.0, 0.0)
        last_row = _iota((DN_CHUNK, 1), 0) == DN_CHUNK - 1

        def chunk(m, carry):
            n = n_chunk - 1 - m
            rows = pl.ds(pl.multiple_of(n * DN_CHUNK, DN_CHUNK), DN_CHUNK)
            heads = range(hps)
            cols = [slice(h * LANES, (h + 1) * LANES) for h in heads]
            each = lambda f, *xs: [f(*a) for a in zip(*xs)]
            rsum = lambda x: jnp.sum(x, axis=-1, keepdims=True)
            dims_tn = (((0,), (0,)), ((), ()))
            ones = jnp.ones((DN_CHUNK, LANES), F32)
            q = [q_ref[rows, cl] for cl in cols]
            k = [k_ref[rows, cl] for cl in cols]
            v = [v_ref[rows, cl] for cl in cols]
            uu = [u_ref[rows, cl] for cl in cols]
            ww = [w_ref[rows, cl] for cl in cols]
            do = [do_ref[rows, cl] for cl in cols]
            tt = [t_ref[h, rows, :] for h in heads]
            beta = [hs_ref[h, rows, 0:1] for h in heads]
            gc = [hs_ref[h, rows, 2:3] for h in heads]
            st = [st_ref[h, n] for h in heads]
            dst = [dstate[h] for h in heads]
            gl = each(lambda g: jnp.min(g, axis=0, keepdims=True), gc)
            egc = each(jnp.exp, gc)
            egl = each(jnp.exp, gl)
            ekd = each(lambda a, b: jnp.exp(a - b), gl, gc)
            decay = each(lambda g: _decay_matrix(g, r, c), gc)
            qd = each(jnp.multiply, q, egc)
            kd = each(jnp.multiply, k, ekd)
            kb = each(jnp.multiply, k, beta)
            w_st = each(_dot, ww, st)
            qk = each(_dot_nt, q, k)
            dqd = each(_dot_nt, do, st)
            kd_dst = each(_dot, kd, dst)
            qd_do = each(_dot_tn, qd, do)
            kbk = each(_dot_nt, kb, k)
            vn = each(jnp.subtract, uu, w_st)
            ai = each(jnp.multiply, qk, decay)
            low = each(lambda a, d: jnp.where(strict, a * d, 0.0), kbk, decay)
            dai = each(lambda a, b: jnp.where(causal, _dot_nt(a, b), 0.0), do, vn)
            ai_do = each(_dot_tn, ai, do)
            dkd = each(_dot_nt, vn, dst)
            dvn = each(jnp.add, ai_do, kd_dst)
            dp = each(jnp.multiply, dai, decay)
            dw = each(lambda a, b: -_dot_nt(a, b), dvn, st)
            w_dvn = each(_dot_tn, ww, dvn)
            dp_k = each(_dot, dp, k)
            dp_q = each(_dot_tn, dp, q)
            drhs_u = each(lambda a, b: _dot_x3(a, b, dims_tn), tt, dvn)
            dgl = each(lambda a, b, e: jnp.sum(a * b) * e, dst, st, egl)
            for h in heads:
                dstate[h] = dst[h] * egl[h] + qd_do[h] - w_dvn[h]
            dq = each(lambda a, e, b: a * e + b, dqd, egc, dp_k)
            dk_a = each(lambda a, e, b: a * e + b, dkd, ekd, dp_q)
            rkd = each(lambda a, b: rsum(a * b), dkd, kd)
            drhs_w = each(lambda a, b: _dot_x3(a, b, dims_tn), tt, dw)
            dl_u = each(_dot_nt, drhs_u, uu)
            dl_w = each(_dot_nt, drhs_w, ww)
            dlow = each(lambda a, b: jnp.where(strict, -(a + b), 0.0), dl_u, dl_w)
            dqm = each(jnp.multiply, dlow, decay)
            m_tot = each(lambda a, b, d, e: a * b + d * e, dai, ai, dlow, low)
            dqm_k = each(_dot, dqm, k)
            dk_l = each(_dot_tn, dqm, kb)
            col_sums = each(lambda m: _dot_hi(m, ones, dims_tn), m_tot)
            dkb_w = each(jnp.multiply, drhs_w, egc)
            dkb = each(jnp.add, dkb_w, dqm_k)
            dgc = [rsum(dqd[h] * qd[h]) - rkd[h] + jnp.where(last_row, jnp.sum(rkd[h]) + dgl[h], 0.0)
                   + rsum(m_tot[h]) + rsum(dkb_w[h] * kb[h]) for h in heads]
            dg = each(lambda a, b: _dot_hi(upper, jnp.where(lane == 1, a - b, 0.0)), dgc, col_sums)
            for h in heads:
                dq_ref[rows, cols[h]] = dq[h]
                dk_ref[rows, cols[h]] = dk_a[h] + dk_l[h] + dkb[h] * beta[h]
                dv_ref[rows, cols[h]] = drhs_u[h] * beta[h]
                dbeta = rsum(drhs_u[h] * v[h]) + rsum(dkb[h] * k[h])
                dhs_ref[h, rows, :] = jnp.where(lane == 0, dbeta, dg[h])
            return carry

        lax.fori_loop(0, n_chunk, chunk, 0)

    wide = hps * LANES
    blk = lambda off: pl.BlockSpec((rb, wide), lambda h, i: (n_blk - 1 - i, off + h))
    head = blk(0)
    hs_spec = pl.BlockSpec((hps, rb, LANES), lambda h, i: (h, n_blk - 1 - i, 0))
    full = jax.ShapeDtypeStruct((s_len, DN_HEADS * LANES), F32)
    return pl.pallas_call(
        body, name="dn_scan_bwd", grid=(groups, n_blk),
        in_specs=[blk(0), blk(groups), blk(2 * groups), head, head,
                  pl.BlockSpec((hps, rb, DN_CHUNK), lambda h, i: (h, n_blk - 1 - i, 0)), hs_spec,
                  pl.BlockSpec((hps, n_chunk, LANES, LANES), lambda h, i: (h, n_blk - 1 - i, 0, 0)), head],
        out_specs=[head, head, head, hs_spec],
        out_shape=[full, full, full, jax.ShapeDtypeStruct((DN_HEADS, s_len, LANES), F32)],
        scratch_shapes=[pltpu.VMEM((hps, LANES, LANES), F32)],
        compiler_params=_cparams(("parallel", "arbitrary")),
    )(qkv, qkv, qkv, u, w, t_inv, hs, states, d_o)


def dn_post(o, z, out_norm):
    def body(i, ins, outs, accs):
        gn = ins[2][...]
        for h in range(DN_HEADS):
            cols = slice(h * LANES, (h + 1) * LANES)
            xhat, _ = _rms_stats(ins[0][:, cols])
            outs[0][:, cols] = (xhat * gn * _silu(ins[1][:, cols])).astype(BF16)

    return row_call("dn_post", body, o.shape[0], ROW_TILE, [(o, "tile"), (z, "tile"), (out_norm, "full")],
                    [(o.shape, BF16)])[0]


def dn_post_bwd(d_og, o, z, out_norm):
    def body(i, ins, outs, accs):
        gn = ins[3][...]
        dgn = jnp.zeros((1, LANES), F32)
        for h in range(DN_HEADS):
            cols = slice(h * LANES, (h + 1) * LANES)
            dy, zh = ins[0][:, cols].astype(F32), ins[2][:, cols]
            xhat, r = _rms_stats(ins[1][:, cols])
            sz = _silu(zh)
            dgn = dgn + jnp.sum(dy * xhat * sz, axis=0, keepdims=True)
            outs[1][:, cols] = (dy * xhat * gn * _silu_grad(zh)).astype(BF16)
            dxhat = dy * gn * sz
            outs[0][:, cols] = r * (dxhat - xhat * jnp.mean(dxhat * xhat, axis=-1, keepdims=True))
        accs[0][...] += dgn

    return row_call("dn_post_bwd", body, o.shape[0], ROW_TILE,
                    [(d_og, "tile"), (o, "tile"), (z, "tile"), (out_norm, "full")],
                    [(o.shape, F32), (o.shape, BF16)], [(1, LANES)])


def dn_pre_bwd(dq, dk, dv, dhs, qkv_raw, ba, w_conv, gate):
    s_len = qkv_raw.shape[0]
    tm = min(DN_ROW_TILE, s_len)

    def body(i, ins, outs, accs):
        dq_ref, dk_ref, dv_ref, dhs_ref, x_ref, xp_ref, ba_ref, w_ref, gate_ref = ins
        dc_ref, dba_ref = outs

        def blk(cb, carry):
            cols = pl.ds(pl.multiple_of(cb * LANES, LANES), LANES)
            hcols = pl.ds(pl.multiple_of((cb & (DN_HEADS - 1)) * LANES, LANES), LANES)
            prev = jnp.where(i > 0, xp_ref[:, cols], 0.0)
            xcat = jnp.concatenate([prev, x_ref[:, cols]], axis=0)
            c = _conv_taps(xcat, w_ref, cols, DN_CONV, DN_HALO, tm)
            y = _silu(c)
            dy = jnp.where(cb < DN_HEADS, dq_ref[:, hcols],
                           jnp.where(cb < 2 * DN_HEADS, dk_ref[:, hcols], dv_ref[:, hcols]))
            rs = lax.rsqrt(jnp.sum(y * y, axis=-1, keepdims=True) + L2_EPS)
            fac = jnp.where(cb < DN_HEADS, DN_HEAD_DIM ** -0.5, 1.0)
            nrm = y * rs
            dn = dy * fac
            dy_norm = rs * (dn - nrm * jnp.sum(dn * nrm, axis=-1, keepdims=True))
            dc_ref[:, cols] = jnp.where(cb < 2 * DN_HEADS, dy_norm, dy) * _silu_grad(c)
            return carry

        lax.fori_loop(0, qkv_raw.shape[1] // LANES, blk, 0)

        lane = _iota((tm, LANES), 1)
        dbeta = jnp.zeros((tm, LANES), F32)
        dg = jnp.zeros((tm, LANES), F32)
        for h in range(DN_HEADS):
            dbeta = dbeta + jnp.where(lane == h, dhs_ref[h, :, 0:1], 0.0)
            dg = dg + jnp.where(lane == DN_HEADS + h, dhs_ref[h, :, 1:2], 0.0)
        bav = ba_ref[...]
        beta = _sigmoid(bav)
        ea = jnp.exp(gate_ref[0:1, :])
        pre = bav + gate_ref[1:2, :]
        g = -ea * _softplus(pre)
        da = dg * (-ea) * _sigmoid(pre)
        dba_ref[...] = (dbeta * beta * (1.0 - beta) + da).astype(BF16)
        rid = _iota((8, LANES), 0)
        accs[0][...] += (jnp.where(rid == 0, jnp.sum(dg * g, axis=0, keepdims=True), 0.0)
                         + jnp.where(rid == 1, jnp.sum(da, axis=0, keepdims=True), 0.0))

    return row_call("dn_pre_bwd", body, s_len, tm,
                    [(dq, "tile"), (dk, "tile"), (dv, "tile"), (dhs, "tile"), (qkv_raw, "tile"),
                     (qkv_raw, ("prev", DN_HALO)), (ba, "tile"), (w_conv, "full"), (gate, "full")],
                    [(qkv_raw.shape, F32), (ba.shape, BF16)], [(8, LANES)])


def dn_conv_bwd(dc, qkv_raw, w_conv):
    s_len = dc.shape[0]
    tm = min(DN_ROW_TILE, s_len)
    nt = s_len // tm

    def body(i, ins, outs, accs):
        dc_ref, dn_ref, x_ref, xp_ref, w_ref = ins

        def blk(cb, carry):
            cols = pl.ds(pl.multiple_of(cb * LANES, LANES), LANES)
            dy = dc_ref[:, cols]
            nxt = jnp.where(i < nt - 1, dn_ref[:, cols], 0.0)
            dcat = jnp.concatenate([dy, nxt], axis=0)
            outs[0][:, cols] = _conv_taps_bwd_x(dcat, w_ref, cols, DN_CONV, DN_HALO, tm).astype(BF16)
            prev = jnp.where(i > 0, xp_ref[:, cols], 0.0)
            xcat = jnp.concatenate([prev, x_ref[:, cols]], axis=0)
            accs[0][:, cols] += _conv_taps_bwd_w(dy, xcat, DN_CONV, DN_HALO, tm, 8)
            return carry

        lax.fori_loop(0, dc.shape[1] // LANES, blk, 0)

    return row_call("dn_conv_bwd", body, s_len, tm,
                    [(dc, "tile"), (dc, ("next", DN_HALO)), (qkv_raw, "tile"), (qkv_raw, ("prev", DN_HALO)),
                     (w_conv, "full")],
                    [(dc.shape, BF16)], [(8, dc.shape[1])])


def _glu(u_ref, cols, d):
    return u_ref[:, cols] * _sigmoid(u_ref[:, pl.ds(pl.multiple_of(d + cols.start, LANES), cols.size)])


def cv_core_fwd(u, w_dw, b_dw, ln_g, ln_b):
    s_len, d = u.shape[0], u.shape[1] // 2
    tm = min(CONV_ROW_TILE, s_len)

    def body(i, ins, outs, accs):
        u_ref, up_ref, w_ref, bdw_ref, g_ref, b_ref = ins
        s_ref, c_ref = outs

        def blk(cb, carry):
            cols = pl.ds(pl.multiple_of(cb * LANES, LANES), LANES)
            prev = jnp.where(i > 0, _glu(up_ref, cols, d), 0.0)
            xcat = jnp.concatenate([prev, _glu(u_ref, cols, d)], axis=0)
            c_ref[:, cols] = _conv_taps(xcat, w_ref, cols, CV_WIDTH, CV_HALO, tm) + bdw_ref[:, cols]
            return carry

        lax.fori_loop(0, d // LANES, blk, 0)
        c = c_ref[...]
        mu = jnp.mean(c, axis=-1, keepdims=True)
        xc = c - mu
        rstd = lax.rsqrt(jnp.mean(xc * xc, axis=-1, keepdims=True) + LN_EPS)
        s_ref[...] = _silu(xc * rstd * g_ref[...] + b_ref[...]).astype(BF16)

    return row_call("cv_core_fwd", body, s_len, tm,
                    [(u, "tile"), (u, ("prev", CV_HALO)), (w_dw, "full"), (b_dw, "full"), (ln_g, "full"),
                     (ln_b, "full")],
                    [((s_len, d), BF16), ((s_len, d), F32)])


def cv_ln_bwd(ds, c, ln_g, ln_b):
    def body(i, ins, outs, accs):
        cv, g = ins[1][...], ins[2][...]
        mu = jnp.mean(cv, axis=-1, keepdims=True)
        xc = cv - mu
        rstd = lax.rsqrt(jnp.mean(xc * xc, axis=-1, keepdims=True) + LN_EPS)
        xhat = xc * rstd
        dl = ins[0][...].astype(F32) * _silu_grad(xhat * g + ins[3][...])
        dxhat = dl * g
        dc = rstd * (dxhat - jnp.mean(dxhat, axis=-1, keepdims=True)
                     - xhat * jnp.mean(dxhat * xhat, axis=-1, keepdims=True))
        outs[0][...] = dc
        rid = _iota((8, cv.shape[1]), 0)
        accs[0][...] += (jnp.where(rid == 0, jnp.sum(dl * xhat, axis=0, keepdims=True), 0.0)
                         + jnp.where(rid == 1, jnp.sum(dl, axis=0, keepdims=True), 0.0)
                         + jnp.where(rid == 2, jnp.sum(dc, axis=0, keepdims=True), 0.0))

    return row_call("cv_ln_bwd", body, c.shape[0], ROW_TILE,
                    [(ds, "tile"), (c, "tile"), (ln_g, "full"), (ln_b, "full")], [(c.shape, F32)], [(8, c.shape[1])])


def cv_conv_bwd(dc, u, w_dw):
    s_len, d = dc.shape
    tm = min(CONV_ROW_TILE, s_len)
    nt = s_len // tm

    def body(i, ins, outs, accs):
        dc_ref, dn_ref, u_ref, up_ref, w_ref = ins

        def blk(cb, carry):
            cols = pl.ds(pl.multiple_of(cb * LANES, LANES), LANES)
            gcols = pl.ds(pl.multiple_of(d + cb * LANES, LANES), LANES)
            dy = dc_ref[:, cols]
            nxt = jnp.where(i < nt - 1, dn_ref[:, cols], 0.0)
            dgl = _conv_taps_bwd_x(jnp.concatenate([dy, nxt], axis=0), w_ref, cols, CV_WIDTH, CV_HALO, tm)
            u1, sg = u_ref[:, cols], _sigmoid(u_ref[:, gcols])
            du1 = dgl * sg
            du2 = dgl * u1 * sg * (1.0 - sg)
            outs[0][:, cols] = du1.astype(BF16)
            outs[0][:, gcols] = du2.astype(BF16)
            accs[1][:, cols] += jnp.sum(du1, axis=0, keepdims=True)
            accs[1][:, gcols] += jnp.sum(du2, axis=0, keepdims=True)
            prev = jnp.where(i > 0, _glu(up_ref, cols, d), 0.0)
            xcat = jnp.concatenate([prev, u1 * sg], axis=0)
            accs[0][:, cols] += _conv_taps_bwd_w(dy, xcat, CV_WIDTH, CV_HALO, tm, CV_HALO)
            return carry

        lax.fori_loop(0, d // LANES, blk, 0)

    return row_call("cv_conv_bwd", body, s_len, tm,
                    [(dc, "tile"), (dc, ("next", CV_HALO)), (u, "tile"), (u, ("prev", CV_HALO)), (w_dw, "full")],
                    [(u.shape, BF16)], [(CV_HALO, d), (1, 2 * d)])


def xa_core_fwd(name, q, kv):
    d = q.shape[1]

    def body(i, ins, outs, accs):
        for h in range(XA_HEADS):
            cols = slice(h * XA_HEAD_DIM, (h + 1) * XA_HEAD_DIM)
            vcols = slice(d + h * XA_HEAD_DIM, d + (h + 1) * XA_HEAD_DIM)
            s = _dot_nt(ins[0][:, cols], ins[1][:, cols]) * (XA_HEAD_DIM ** -0.5)
            e = jnp.exp(s - jnp.max(s, axis=-1, keepdims=True))
            p = e / jnp.sum(e, axis=-1, keepdims=True)
            outs[0][:, cols] = _dot(p, ins[1][:, vcols]).astype(BF16)

    return row_call(name, body, q.shape[0], ROW_TILE, [(q, "tile"), (kv, "full")], [(q.shape, BF16)])[0]


def xa_core_bwd(name, d_o, q, kv):
    d = q.shape[1]

    def body(i, ins, outs, accs):
        for h in range(XA_HEADS):
            cols = slice(h * XA_HEAD_DIM, (h + 1) * XA_HEAD_DIM)
            vcols = slice(d + h * XA_HEAD_DIM, d + (h + 1) * XA_HEAD_DIM)
            qh, kh, vh, doh = ins[1][:, cols], ins[2][:, cols], ins[2][:, vcols], ins[0][:, cols]
            s = _dot_nt(qh, kh) * (XA_HEAD_DIM ** -0.5)
            e = jnp.exp(s - jnp.max(s, axis=-1, keepdims=True))
            p = e / jnp.sum(e, axis=-1, keepdims=True)
            dp = _dot_nt(doh, vh)
            ds = p * (dp - jnp.sum(dp * p, axis=-1, keepdims=True)) * (XA_HEAD_DIM ** -0.5)
            outs[0][:, cols] = _dot(ds, kh).astype(BF16)
            accs[0][:, cols] += _dot_tn(ds, qh)
            accs[0][:, vcols] += _dot_tn(p, doh)

    return row_call(name, body, q.shape[0], ROW_TILE, [(d_o, "tile"), (q, "tile"), (kv, "full")],
                    [(q.shape, BF16)], [kv.shape])


def adamw(name, w, g, m, v):
    def body(i, ins, outs, accs):
        wv, gv = ins[0][...], ins[1][...]
        mn = ADAM_B1 * ins[2][...] + (1.0 - ADAM_B1) * gv
        vn = ADAM_B2 * ins[3][...] + (1.0 - ADAM_B2) * jnp.square(gv)
        m_hat = mn / (1.0 - ADAM_B1 ** ADAM_STEP)
        v_hat = vn / (1.0 - ADAM_B2 ** ADAM_STEP)
        outs[0][...] = -ADAM_LR * (m_hat / (jnp.sqrt(v_hat) + ADAM_EPS) + ADAM_WD * wv)
        outs[1][...] = mn
        outs[2][...] = vn

    return row_call(name, body, w.shape[0], ROW_TILE, [(w, "tile"), (g, "tile"), (m, "tile"), (v, "tile")],
                    [(w.shape, F32)] * 3)


def adamw_halves(name, w, g_mine, g_sibling, m, v, core):
    n_layers = len(g_mine)
    rows, cols = w.shape
    half_rows = rows // n_layers // 2
    tm = _pick_tile(half_rows, ADAMW_ROW_TILE)
    per_half = half_rows // tm

    def body(core_ref, w_ref, *rest):
        g_refs = rest[:2 * n_layers]
        m_ref, v_ref, g_out, d_out, m_out, v_out = rest[2 * n_layers:]
        i = pl.program_id(0)
        mine = ((i // per_half) % 2) == core_ref[0]
        layer = i // (2 * per_half)
        gv = jnp.where(mine, g_refs[0][...], g_refs[n_layers][...])
        for l in range(1, n_layers):
            gv = jnp.where(layer == l, jnp.where(mine, g_refs[l][...], g_refs[n_layers + l][...]), gv)
        mn = ADAM_B1 * m_ref[...] + (1.0 - ADAM_B1) * gv
        vn = ADAM_B2 * v_ref[...] + (1.0 - ADAM_B2) * jnp.square(gv)
        m_hat = mn / (1.0 - ADAM_B1 ** ADAM_STEP)
        v_hat = vn / (1.0 - ADAM_B2 ** ADAM_STEP)
        g_out[...] = gv
        d_out[...] = -ADAM_LR * (m_hat / (jnp.sqrt(v_hat) + ADAM_EPS) + ADAM_WD * w_ref[...])
        m_out[...] = mn
        v_out[...] = vn

    whole = pl.BlockSpec((tm, cols), lambda i, core_ref: (i, 0))
    half = pl.BlockSpec((tm, cols), lambda i, core_ref: (i % per_half, 0))
    return pl.pallas_call(
        body, name=name,
        grid_spec=pltpu.PrefetchScalarGridSpec(
            num_scalar_prefetch=1, grid=(2 * per_half * n_layers,),
            in_specs=[whole] + [half] * (2 * n_layers) + [whole, whole], out_specs=[whole] * 4),
        out_shape=[jax.ShapeDtypeStruct(w.shape, F32)] * 4,
        compiler_params=_cparams(("parallel",)),
    )(core, w, *g_mine, *g_sibling, m, v)


HBM_SPEC = pl.BlockSpec(memory_space=pltpu.HBM)


def _position():
    return lax.axis_index("x"), lax.axis_index("y"), lax.axis_index("c")


def _other_chips(x, y):
    return [(1 - x, y), (x, 1 - y), (1 - x, 1 - y)]


def _row_chunks(rows):
    return rows // D2D_CHUNK_ROWS if rows % D2D_CHUNK_ROWS == 0 else 1


def _start_chunked(make, rows):
    k = _row_chunks(rows)
    for i in range(k):
        make(i * (rows // k), rows // k).start()


def gather_shards(packs):
    n = len(packs)

    def body(*refs):
        srcs, outs = refs[:n], refs[n:2 * n]
        send_sems, recv_sems = refs[2 * n:]
        x, y, c = _position()
        me = 2 * x + y
        chips = _other_chips(x, y)
        sibling = (x, y, 1 - c)

        def over_ici(a, j):
            px, py = chips[j]
            rows = srcs[a].shape[0] // 2
            return pltpu.make_async_remote_copy(
                src_ref=srcs[a].at[pl.ds(c * rows, rows), :], dst_ref=outs[a].at[me, pl.ds(c * rows, rows), :],
                send_sem=send_sems.at[a, j], recv_sem=recv_sems.at[a, j], device_id=(px, py, c), device_id_type=MESH)

        def landed(a, j):
            px, py = chips[j]
            rows = srcs[a].shape[0] // 2
            part = outs[a].at[2 * px + py, pl.ds(c * rows, rows), :]
            return pltpu.make_async_remote_copy(
                src_ref=part, dst_ref=part, send_sem=send_sems.at[a, j], recv_sem=recv_sems.at[a, j],
                device_id=(px, py, c), device_id_type=MESH)

        def over_d2d(a, j, cc, off, size):
            px, py = chips[j]
            rows = srcs[a].shape[0] // 2
            part = outs[a].at[2 * px + py, pl.ds(cc * rows + off, size), :]
            return pltpu.make_async_remote_copy(
                src_ref=part, dst_ref=part, send_sem=send_sems.at[a, 3 + j], recv_sem=recv_sems.at[a, 3 + j],
                device_id=sibling, device_id_type=MESH)

        for a in range(n):
            for j in range(3):
                over_ici(a, j).start()
        for a in range(n):
            for j in range(3):
                landed(a, j).wait_recv()
                _start_chunked(functools.partial(over_d2d, a, j, c), srcs[a].shape[0] // 2)
        for a in range(n):
            rows = srcs[a].shape[0] // 2
            for j in range(3):
                over_d2d(a, j, 1 - c, 0, rows).wait_recv()
                over_d2d(a, j, c, 0, rows).wait_send()
                over_ici(a, j).wait_send()

    return pl.pallas_call(
        body, name="gather_shards",
        in_specs=[HBM_SPEC] * n, out_specs=[HBM_SPEC] * n,
        out_shape=[jax.ShapeDtypeStruct((N_CHIPS,) + p.shape, p.dtype) for p in packs],
        scratch_shapes=[pltpu.SemaphoreType.DMA((n, 6)), pltpu.SemaphoreType.DMA((n, 6))],
    )(*packs)


def pair_split(name, packs):
    n = len(packs)

    def body(*refs):
        srcs, outs = refs[:n], refs[n:2 * n]
        send_sems, recv_sems = refs[2 * n:]
        x, y, c = _position()

        def remote(a, off, size):
            rows = srcs[a].shape[1] // 2
            return pltpu.make_async_remote_copy(
                src_ref=srcs[a].at[:, pl.ds((1 - c) * rows + off, size), :],
                dst_ref=outs[a].at[:, pl.ds(off, size), :],
                send_sem=send_sems.at[a], recv_sem=recv_sems.at[a], device_id=(x, y, 1 - c), device_id_type=MESH)

        for a in range(n):
            _start_chunked(functools.partial(remote, a), srcs[a].shape[1] // 2)
        for a in range(n):
            remote(a, 0, srcs[a].shape[1] // 2).wait()

    return pl.pallas_call(
        body, name=name, in_specs=[HBM_SPEC] * n, out_specs=[HBM_SPEC] * n,
        out_shape=[jax.ShapeDtypeStruct((p.shape[0], p.shape[1] // 2, p.shape[2]), p.dtype) for p in packs],
        scratch_shapes=[pltpu.SemaphoreType.DMA((n,)), pltpu.SemaphoreType.DMA((n,))],
    )(*packs)


def pair_join(name, halves):
    n = len(halves)

    def body(*refs):
        srcs, outs = refs[:n], refs[n:2 * n]
        send_sems, recv_sems = refs[2 * n:]
        x, y, c = _position()

        def remote(a, off, size):
            return pltpu.make_async_remote_copy(
                src_ref=srcs[a].at[pl.ds(off, size), :], dst_ref=outs[a].at[pl.ds(off, size), :],
                send_sem=send_sems.at[a], recv_sem=recv_sems.at[a], device_id=(x, y, 1 - c), device_id_type=MESH)

        for a in range(n):
            _start_chunked(functools.partial(remote, a), srcs[a].shape[0])
        for a in range(n):
            remote(a, 0, srcs[a].shape[0]).wait()

    return pl.pallas_call(
        body, name=name, in_specs=[HBM_SPEC] * n, out_specs=[HBM_SPEC] * n,
        out_shape=[jax.ShapeDtypeStruct(p.shape, p.dtype) for p in halves],
        scratch_shapes=[pltpu.SemaphoreType.DMA((n,)), pltpu.SemaphoreType.DMA((n,))],
    )(*halves)


SEM_SPEC = pl.BlockSpec(memory_space=pltpu.SEMAPHORE)
DATAFLOW = pltpu.SideEffectType.DATAFLOW_SIDE_EFFECTING


def _ici_copy(kind, srcs, lands, send_sems, recv_sems, a, j):
    x, y, c = _position()
    px, py = _other_chips(x, y)[j]
    if kind == "gather":
        rows = srcs[a].shape[0] // 2
        src = srcs[a].at[pl.ds(c * rows, rows), :]
        dst = lands[a].at[2 * x + y, pl.ds(c * rows, rows), :]
    else:
        src = srcs[a].at[2 * px + py]
        dst = lands[a].at[j]
    return pltpu.make_async_remote_copy(src_ref=src, dst_ref=dst, send_sem=send_sems, recv_sem=recv_sems,
                                        device_id=(px, py, c), device_id_type=MESH)


def ici_start(name, kind, srcs, land_shapes):
    n = len(srcs)
    lands = [pltpu.with_memory_space_constraint(lax.empty(shp, s.dtype), pltpu.HBM) for shp, s in zip(land_shapes, srcs)]

    def body(*refs):
        src_refs, land_refs = refs[:n], refs[n:2 * n]
        send_sems, recv_sems = refs[2 * n], refs[2 * n + 1]
        token = refs[-1]
        for a in range(n):
            for j in range(N_CHIPS - 1):
                _ici_copy(kind, src_refs, land_refs, send_sems, recv_sems, a, j).start()
        token[...] = jnp.zeros_like(token)

    sems = pltpu.SemaphoreType.DMA(())
    res = pl.pallas_call(
        body, name=name,
        out_shape=[sems, sems] + [pltpu.HBM(s.shape, s.dtype) for s in srcs]
        + [pltpu.HBM(l.shape, l.dtype) for l in lands] + [jax.ShapeDtypeStruct((8, LANES), F32)],
        in_specs=[HBM_SPEC] * (2 * n),
        out_specs=[SEM_SPEC, SEM_SPEC] + [HBM_SPEC] * (2 * n) + [pl.BlockSpec(memory_space=pltpu.VMEM)],
        input_output_aliases={i: 2 + i for i in range(2 * n)},
        compiler_params=pltpu.CompilerParams(has_side_effects=DATAFLOW),
    )(*[pltpu.with_memory_space_constraint(s, pltpu.HBM) for s in srcs], *lands)
    return res[0], res[1], list(res[2:2 + n]), list(res[2 + n:2 + 2 * n]), res[-1]


def ici_wait(name, kind, send_sems, recv_sems, srcs, lands, after):
    n = len(srcs)

    def body(*refs):
        src_refs, land_refs = refs[:n], refs[n:2 * n]
        send, recv = refs[2 * n], refs[2 * n + 1]
        for a in range(n):
            for j in range(N_CHIPS - 1):
                cp = _ici_copy(kind, src_refs, land_refs, send, recv, a, j)
                cp.wait_send()
                cp.wait_recv()

    res = pl.pallas_call(
        body, name=name,
        out_shape=[pltpu.HBM(s.shape, s.dtype) for s in srcs] + [pltpu.HBM(l.shape, l.dtype) for l in lands],
        in_specs=[HBM_SPEC] * (2 * n) + [SEM_SPEC, SEM_SPEC, pl.BlockSpec(memory_space=pl.ANY)],
        out_specs=[HBM_SPEC] * (2 * n),
        input_output_aliases={i: i for i in range(2 * n)},
        compiler_params=pltpu.CompilerParams(has_side_effects=DATAFLOW),
    )(*srcs, *lands, send_sems, recv_sems, after)
    return list(res[:n]), list(res[n:])


def pair_forward(gathered):
    n = len(gathered)

    def body(*refs):
        outs = refs[n:2 * n]
        send_sems, recv_sems = refs[2 * n:]
        x, y, c = _position()
        chips = _other_chips(x, y)

        def part(a, j, cc, off, size):
            px, py = chips[j]
            rows = outs[a].shape[1] // 2
            ref = outs[a].at[2 * px + py, pl.ds(cc * rows + off, size), :]
            return pltpu.make_async_remote_copy(
                src_ref=ref, dst_ref=ref, send_sem=send_sems.at[a, j], recv_sem=recv_sems.at[a, j],
                device_id=(x, y, 1 - c), device_id_type=MESH)

        for a in range(n):
            for j in range(N_CHIPS - 1):
                _start_chunked(functools.partial(part, a, j, c), outs[a].shape[1] // 2)
        for a in range(n):
            rows = outs[a].shape[1] // 2
            for j in range(N_CHIPS - 1):
                part(a, j, 1 - c, 0, rows).wait_recv()
                part(a, j, c, 0, rows).wait_send()

    return pl.pallas_call(
        body, name="pair_forward", in_specs=[HBM_SPEC] * n, out_specs=[HBM_SPEC] * n,
        out_shape=[jax.ShapeDtypeStruct(g.shape, g.dtype) for g in gathered],
        input_output_aliases={i: i for i in range(n)},
        scratch_shapes=[pltpu.SemaphoreType.DMA((n, N_CHIPS - 1)), pltpu.SemaphoreType.DMA((n, N_CHIPS - 1))],
    )(*gathered)


def all_sum_small(part):
    n_dev = 8
    rows = part.shape[0]

    def body(src, out, buf, send_sems, recv_sems):
        x, y, c = _position()
        me = 4 * x + 2 * y + c
        buf[me] = src[...]
        copies = []
        for k in range(1, n_dev):
            px, py, pc = x ^ ((k >> 2) & 1), y ^ ((k >> 1) & 1), c ^ (k & 1)
            cp = pltpu.make_async_remote_copy(
                src_ref=src, dst_ref=buf.at[me], send_sem=send_sems.at[k - 1], recv_sem=recv_sems.at[k - 1],
                device_id=(px, py, pc), device_id_type=MESH)
            cp.start()
            copies.append(cp)
        for cp in copies:
            cp.wait()
        acc = buf[0]
        for k in range(1, n_dev):
            acc = acc + buf[k]
        out[...] = acc

    return pl.pallas_call(
        body, name="all_sum_small",
        in_specs=[pl.BlockSpec(memory_space=pltpu.VMEM)], out_specs=pl.BlockSpec(memory_space=pltpu.VMEM),
        out_shape=jax.ShapeDtypeStruct(part.shape, F32),
        scratch_shapes=[pltpu.VMEM((n_dev, rows, part.shape[1]), F32),
                        pltpu.SemaphoreType.DMA((n_dev - 1,)), pltpu.SemaphoreType.DMA((n_dev - 1,))],
    )(part)


def add_pairs(name, src, theirs, core, out_dtype):
    slabs, rows, cols = theirs.shape
    tm = _pick_tile(rows, ROW_TILE)
    nb = rows // tm

    def body(core_ref, a_ref, b_ref, o_ref):
        o_ref[...] = (a_ref[...].astype(F32) + b_ref[...].astype(F32)).astype(out_dtype)

    return pl.pallas_call(
        body, name=name,
        grid_spec=pltpu.PrefetchScalarGridSpec(
            num_scalar_prefetch=1, grid=(slabs, nb),
            in_specs=[pl.BlockSpec((None, tm, cols), lambda s, i, core_ref: (s, core_ref[0] * nb + i, 0)),
                      pl.BlockSpec((None, tm, cols), lambda s, i, core_ref: (s, i, 0))],
            out_specs=pl.BlockSpec((None, tm, cols), lambda s, i, core_ref: (s, i, 0))),
        out_shape=jax.ShapeDtypeStruct(theirs.shape, out_dtype),
        compiler_params=_cparams(("parallel", "parallel")),
    )(core, src, theirs)


def add_four(name, src, theirs, chip):
    _, rows, cols = theirs.shape
    tm = _pick_tile(rows, ROW_TILE)

    def body(chip_ref, a_ref, b_ref, o_ref):
        acc = a_ref[...].astype(F32)
        for j in range(N_CHIPS - 1):
            acc = acc + b_ref[j].astype(F32)
        o_ref[...] = acc

    return pl.pallas_call(
        body, name=name,
        grid_spec=pltpu.PrefetchScalarGridSpec(
            num_scalar_prefetch=1, grid=(rows // tm,),
            in_specs=[pl.BlockSpec((None, tm, cols), lambda i, chip_ref: (chip_ref[0], i, 0)),
                      pl.BlockSpec((N_CHIPS - 1, tm, cols), lambda i, chip_ref: (0, i, 0))],
            out_specs=pl.BlockSpec((tm, cols), lambda i, chip_ref: (i, 0))),
        out_shape=jax.ShapeDtypeStruct((rows, cols), F32),
        compiler_params=_cparams(("parallel",)),
    )(chip, src, theirs)


PACK_COLS = 1024
BIG_ROW_MULTIPLE = 512
SMALL_ROW_MULTIPLE = 32
BIG = ["dn_w_in", "dn_w_out", "cv_w_pw1", "cv_w_pw2", "xa_w_q", "xa_w_kv", "xa_w_o", "mlp_w_up", "mlp_w_down"]
SMALL = ["dn_w_conv", "cv_norm", "cv_b_pw1", "cv_w_dw", "cv_b_dw", "cv_ln_g", "cv_ln_b", "cv_b_pw2"]
SHARD_AXIS = {"dn_w_in": 2, "dn_w_conv": 2, "dn_w_out": 1, "cv_norm": 1, "cv_w_pw1": 2, "cv_b_pw1": 1,
              "cv_w_dw": 2, "cv_b_dw": 1, "cv_ln_g": 1, "cv_ln_b": 1, "cv_w_pw2": 1, "cv_b_pw2": 1,
              "xa_w_q": 1, "xa_w_kv": 2, "xa_w_o": 1, "mlp_w_up": 2, "mlp_w_down": 1}
REPLICATED = ["dn_norm", "dn_a_log", "dn_dt_bias", "dn_out_norm", "xa_norm", "xa_mem_norm", "mlp_norm", "final_norm"]


def _pack_rows(size):
    return -(-size // PACK_COLS)


SHARD_SHAPES = {
    "dn_w_in": (1, 1024, 1028), "dn_w_conv": (1, 4, 768), "dn_w_out": (1, 256, 1024), "cv_norm": (1, 256),
    "cv_w_pw1": (1, 1024, 512), "cv_b_pw1": (1, 512), "cv_w_dw": (1, 31, 256), "cv_b_dw": (1, 256),
    "cv_ln_g": (1, 256), "cv_ln_b": (1, 256), "cv_w_pw2": (1, 256, 1024), "cv_b_pw2": (1, 256),
    "xa_w_q": (2, 256, 1024), "xa_w_kv": (2, 1024, 512), "xa_w_o": (2, 256, 1024),
    "mlp_w_up": (2, 1024, 1024), "mlp_w_down": (2, 1024, 1024)}


def _shard_shape(nm):
    return SHARD_SHAPES[nm]


def _pack(tensors, names, dtype, row_multiple):
    pieces = []
    for nm in names:
        t = tensors[nm]
        flat = t.reshape(t.shape[0], -1) if t.ndim > len(_shard_shape(nm)) else t.reshape(1, -1)
        pad = _pack_rows(flat.shape[1]) * PACK_COLS - flat.shape[1]
        pieces.append(jnp.pad(flat.astype(dtype), ((0, 0), (0, pad))))
    cat = jnp.concatenate(pieces, axis=1)
    rows = cat.shape[1] // PACK_COLS
    total = -(-rows // row_multiple) * row_multiple
    cat = jnp.pad(cat, ((0, 0), (0, (total - rows) * PACK_COLS)))
    return cat.reshape(cat.shape[0], total, PACK_COLS)


def _unpack(pack, names):
    lead = pack.shape[:-2]
    flat = pack.reshape(lead + (-1,))
    out, off = {}, 0
    for nm in names:
        shp = _shard_shape(nm)
        size = 1
        for s in shp:
            size *= s
        out[nm] = flat[..., off:off + size].reshape(lead + shp)
        off += _pack_rows(size) * PACK_COLS
    return out


def _to_full(nm, stacked):
    ax = SHARD_AXIS[nm]
    moved = jnp.moveaxis(stacked, 0, ax)
    shp = list(_shard_shape(nm))
    shp[ax] *= N_CHIPS
    return moved.reshape(shp)


def _to_shards(nm, full):
    ax = SHARD_AXIS[nm]
    shp = list(_shard_shape(nm))
    split = full.reshape(shp[:ax] + [N_CHIPS, shp[ax]] + shp[ax + 1:])
    return jnp.moveaxis(split, ax, 0)


def _row(v):
    return v.reshape(1, -1)


class Stacked:
    def __init__(self, arr, split, layer):
        self.arr, self.kw = arr, dict(b_split=split, b_layer=layer)


def _grad_out(split):
    return dict(out_dtype=BF16, out_split=split, out_layer=(0, 1))


def _with_next(res, next_gain):
    return (res[0], res[1]) if next_gain is not None else (res, None)


def mlp_fwd(tag, h, g, w_up, w_down, n=None, next_gain=None):
    if n is None:
        n = rms_fwd(tag + "_norm", h, g)
    act = mm(tag + "_up", n, w_up.arr, out_dtype=BF16, epi=lambda acc: jnp.square(jnp.maximum(acc, 0.0)), **w_up.kw)
    out, n_next = _with_next(mm(tag + "_down", act, w_down.arr, tk=LONG_TILE_K, epi=lambda acc, res: acc + res,
                                epi_tiles=(h,), norm_gain=next_gain, **w_down.kw), next_gain)
    return out, n_next, (n, act)


def mlp_bwd(tag, dh, h, g, w_up, w_down, saved, after=None):
    n, act = saved
    dh, dh16 = dh
    dup = mm(tag + "_d_act", dh16, w_down.arr, tb=True, out_dtype=BF16, after=after,
             epi=lambda acc, t: acc * (2.0 * jnp.sqrt(t.astype(F32))), epi_tiles=(act,), **w_down.kw)
    dw_down = mm(tag + "_dw_down", act, dh16, ta=True, tk=LONG_TILE_K, **_grad_out("rows"))
    dn = mm(tag + "_dn", dup, w_up.arr, tb=True, **w_up.kw)
    dw_up = mm(tag + "_dw_up", n, dup, ta=True, tk=LONG_TILE_K, **_grad_out("cols"))
    dh_in, dh16_in, dg = rms_bwd(tag + "_norm_bwd", dn, h, g, dh)
    return (dh_in, dh16_in), dg, (dw_up, dw_down)


def xa_fwd(tag, h, mem, g, g_mem, w_q, w_kv, w_o, n=None, next_gain=None):
    if n is None:
        n = rms_fwd(tag + "_norm", h, g)
    mem_n = rms_fwd(tag + "_mem_norm", mem, g_mem)
    q = mm(tag + "_q", n, w_q.arr, out_dtype=BF16, **w_q.kw)
    kv = mm(tag + "_kv", mem_n, w_kv.arr, out_dtype=BF16, **w_kv.kw)
    o = xa_core_fwd(tag + "_core", q, kv)
    out, n_next = _with_next(mm(tag + "_o", o, w_o.arr, epi=lambda acc, res: acc + res, epi_tiles=(h,),
                                norm_gain=next_gain, **w_o.kw), next_gain)
    return out, n_next, (n, mem_n, q, kv, o)


def xa_bwd(tag, dh, h, mem, g, g_mem, w_q, w_kv, w_o, saved):
    n, mem_n, q, kv, o = saved
    dh, dh16 = dh
    d_o = mm(tag + "_d_o", dh16, w_o.arr, tb=True, out_dtype=BF16, **w_o.kw)
    dw_o = mm(tag + "_dw_o", o, dh16, ta=True, tk=LONG_TILE_K, **_grad_out("rows"))
    dq, dkv = xa_core_bwd(tag + "_core_bwd", d_o, q, kv)
    dn = mm(tag + "_dn", dq, w_q.arr, tb=True, **w_q.kw)
    dw_q = mm(tag + "_dw_q", n, dq, ta=True, tk=LONG_TILE_K, **_grad_out("rows"))
    dh_in, dh16_in, dg = rms_bwd(tag + "_norm_bwd", dn, h, g, dh)
    dw_kv = mm(tag + "_dw_kv", mem_n, dkv, ta=True, **_grad_out("cols"))
    dmem_n = mm(tag + "_dmem", dkv, w_kv.arr, tb=True, **w_kv.kw)
    dg_mem = mem_norm_bwd(tag + "_mem_norm_bwd", dmem_n, mem, g_mem)
    return (dh_in, dh16_in), dg, dg_mem, (dw_q, dw_kv, dw_o)


def _gate_tile(a_log, dt_bias):
    t = jnp.zeros((8, LANES), F32)
    t = t.at[0, DN_HEADS:2 * DN_HEADS].set(a_log.reshape(-1))
    return t.at[1, DN_HEADS:2 * DN_HEADS].set(dt_bias.reshape(-1))


def dn_fwd(h, g, w_qkv, w_z, w_ba, w_conv, gate, out_norm, w_out, next_gain=None):
    n = rms_fwd("dn_norm", h, g)
    qkv_raw = mm("dn_proj_qkv", n, w_qkv)
    z = mm("dn_proj_z", n, w_z)
    ba = mm("dn_proj_ba", n, w_ba)
    qkv, hs = dn_pre(qkv_raw, ba, w_conv, gate)
    u, w, t_inv = dn_solve(qkv, hs)
    o, states = dn_scan_fwd(qkv, u, w, hs)
    og = dn_post(o, z, out_norm)
    out, n_next = _with_next(mm("dn_out", og, w_out.arr, epi=lambda acc, res: acc + res, epi_tiles=(h,),
                                norm_gain=next_gain, **w_out.kw), next_gain)
    return out, n_next, (n, qkv_raw, z, ba, qkv, hs, u, w, t_inv, o, states, og)


def dn_bwd(dh, h, g, w_qkv, w_z, w_ba, w_conv, gate, out_norm, w_out, saved, after=None):
    n, qkv_raw, z, ba, qkv, hs, u, w, t_inv, o, states, og = saved
    dh, dh16 = dh
    d_og = mm("dn_d_og", dh16, w_out.arr, tb=True, out_dtype=BF16, after=after, **w_out.kw)
    dw_out = mm("dn_dw_out", og, dh16, ta=True, tk=LONG_TILE_K, **_grad_out("rows"))
    d_o, dz, d_out_norm = dn_post_bwd(d_og, o, z, out_norm)
    dq, dk, dv, dhs = dn_scan_bwd(qkv, u, w, t_inv, hs, states, d_o)
    dc, dba, d_gate = dn_pre_bwd(dq, dk, dv, dhs, qkv_raw, ba, w_conv, gate)
    dqkv_raw, dw_conv = dn_conv_bwd(dc, qkv_raw, w_conv)
    dn = mm("dn_dn_qkv", dqkv_raw, w_qkv, tb=True)
    dn = mm("dn_dn_z", dz, w_z, tb=True, epi=lambda acc, t: acc + t, epi_tiles=(dn,))
    dn = mm("dn_dn_ba", dba, w_ba, tb=True, epi=lambda acc, t: acc + t, epi_tiles=(dn,))
    dw_qkv = mm("dn_dw_qkv", n, dqkv_raw, ta=True, tk=LONG_TILE_K)
    dw_z = mm("dn_dw_z", n, dz, ta=True, tk=LONG_TILE_K)
    dw_ba = mm("dn_dw_ba", n, dba, ta=True, tk=LONG_TILE_K)
    dh_in, _, dg = rms_bwd("dn_norm_bwd", dn, h, g, dh)
    return dh_in, dg, dw_qkv, dw_z, dw_ba, dw_conv, d_gate, d_out_norm, dw_out


def cv_fwd(h, g, w_pw1, b_pw1, w_dw, b_dw, ln_g, ln_b, w_pw2, b_pw2, n=None, next_gain=None):
    if n is None:
        n = rms_fwd("cv_norm", h, g)
    u = mm("cv_pw1", n, w_pw1.arr, epi=lambda acc, b: acc + b, epi_rows=(b_pw1,), **w_pw1.kw)
    s, c = cv_core_fwd(u, w_dw, b_dw, ln_g, ln_b)
    out, n_next = _with_next(mm("cv_pw2", s, w_pw2.arr, epi=lambda acc, res, b: acc + res + b, epi_tiles=(h,),
                                epi_rows=(b_pw2,), norm_gain=next_gain, **w_pw2.kw), next_gain)
    return out, n_next, (n, u, s, c)


def cv_bwd(dh, h, g, w_pw1, w_dw, ln_g, ln_b, w_pw2, saved):
    n, u, s, c = saved
    dh, dh16 = dh
    ds = mm("cv_d_s", dh16, w_pw2.arr, tb=True, out_dtype=BF16, **w_pw2.kw)
    dw_pw2 = mm("cv_dw_pw2", s, dh16, ta=True, tk=LONG_TILE_K, **_grad_out("rows"))
    db_pw2 = col_sum("cv_db_pw2", dh)
    dc, ln_acc = cv_ln_bwd(ds, c, ln_g, ln_b)
    du, dw_dw, db_pw1 = cv_conv_bwd(dc, u, w_dw)
    dn = mm("cv_dn", du, w_pw1.arr, tb=True, **w_pw1.kw)
    dw_pw1 = mm("cv_dw_pw1", n, du, ta=True, tk=LONG_TILE_K, **_grad_out("cols"))
    dh_in, dh16_in, dg = rms_bwd("cv_norm_bwd", dn, h, g, dh)
    return (dh_in, dh16_in), dg, dw_pw1, db_pw1, dw_dw, ln_acc, dw_pw2, db_pw2


WEIGHTS = ["dn_norm", "dn_w_in", "dn_w_conv", "dn_a_log", "dn_dt_bias", "dn_out_norm", "dn_w_out", "cv_norm",
           "cv_w_pw1", "cv_b_pw1", "cv_w_dw", "cv_b_dw", "cv_ln_g", "cv_ln_b", "cv_w_pw2", "cv_b_pw2", "xa_norm",
           "xa_mem_norm", "xa_w_q", "xa_w_kv", "xa_w_o", "mlp_norm", "mlp_w_up", "mlp_w_down", "final_norm"]


def _as_2d(t):
    if t.ndim == 1:
        return t.reshape(1, -1)
    return t.reshape(-1, t.shape[-1])


def kernel(x, mem, dn_norm, dn_w_in, dn_w_conv, dn_a_log, dn_dt_bias, dn_out_norm, dn_w_out, cv_norm, cv_w_pw1, cv_b_pw1, cv_w_dw, cv_b_dw, cv_ln_g, cv_ln_b, cv_w_pw2, cv_b_pw2, xa_norm, xa_mem_norm, xa_w_q, xa_w_kv, xa_w_o, mlp_norm, mlp_w_up, mlp_w_down, final_norm, loss_target, m_dn_norm, m_dn_w_in, m_dn_w_conv, m_dn_a_log, m_dn_dt_bias, m_dn_out_norm, m_dn_w_out, m_cv_norm, m_cv_w_pw1, m_cv_b_pw1, m_cv_w_dw, m_cv_b_dw, m_cv_ln_g, m_cv_ln_b, m_cv_w_pw2, m_cv_b_pw2, m_xa_norm, m_xa_mem_norm, m_xa_w_q, m_xa_w_kv, m_xa_w_o, m_mlp_norm, m_mlp_w_up, m_mlp_w_down, m_final_norm, v_dn_norm, v_dn_w_in, v_dn_w_conv, v_dn_a_log, v_dn_dt_bias, v_dn_out_norm, v_dn_w_out, v_cv_norm, v_cv_w_pw1, v_cv_b_pw1, v_cv_w_dw, v_cv_b_dw, v_cv_ln_g, v_cv_ln_b, v_cv_w_pw2, v_cv_b_pw2, v_xa_norm, v_xa_mem_norm, v_xa_w_q, v_xa_w_kv, v_xa_w_o, v_mlp_norm, v_mlp_w_up, v_mlp_w_down, v_final_norm):
    args = dict(locals())
    wts = {nm: args[nm] for nm in WEIGHTS}
    mom = {nm: args["m_" + nm] for nm in WEIGHTS}
    var = {nm: args["v_" + nm] for nm in WEIGHTS}
    core = lax.axis_index("c").astype(jnp.int32).reshape(1)
    chip = (2 * lax.axis_index("x") + lax.axis_index("y")).astype(jnp.int32)
    def own_slab(got, src):
        return lax.dynamic_update_slice(got, src[None], (chip, 0, 0))

    shard2d = {nm: wts[nm].astype(BF16).reshape(-1, wts[nm].shape[-1]) for nm in BIG}
    first = ["dn_w_in", "dn_w_out"]
    later = [nm for nm in BIG if nm not in first]
    sources = [shard2d[nm] for nm in first] + [_pack(wts, SMALL, F32, SMALL_ROW_MULTIPLE)[0]]
    gathered = [own_slab(got, src) for got, src in zip(gather_shards(sources), sources)]
    stacked = {"dn_w_out": gathered[1].reshape((N_CHIPS,) + SHARD_SHAPES["dn_w_out"])}
    full = {nm: _to_full(nm, t) for nm, t in _unpack(gathered[2], SMALL).items()}
    full["dn_w_in"] = _to_full("dn_w_in", gathered[0].reshape((N_CHIPS,) + SHARD_SHAPES["dn_w_in"]))
    full.update({nm: wts[nm] for nm in REPLICATED})
    later_src = [shard2d[nm] for nm in later]
    g_send, g_recv, later_src, g_lands, started = ici_start(
        "gather_start", "gather", later_src, [(N_CHIPS,) + s.shape for s in later_src])
    full["dn_norm"] = full["dn_norm"] + started[0, 0]

    def rest_weights(after):
        srcs, lands = ici_wait("gather_wait", "gather", g_send, g_recv, later_src, g_lands, after)
        return {nm: own_slab(land, src).reshape((N_CHIPS,) + SHARD_SHAPES[nm])
                for nm, land, src in zip(later, pair_forward(lands), srcs)}

    pending = []

    def on_grads(items):
        tag = "_".join(sorted({str(layer) for _, layer, _ in items}))
        parts = [g.reshape(N_CHIPS, -1, g.shape[-1]) for _, _, g in items]
        theirs = pair_split("pair_split_" + tag, parts)
        pairs = [add_pairs("pair_add_%s%d" % (nm, layer), p, t, core, BF16)
                 for (nm, layer, _), p, t in zip(items, parts, theirs)]
        send, recv, pairs, lands, token = ici_start(
            "scatter_start_" + tag, "scatter", pairs, [(N_CHIPS - 1,) + p.shape[1:] for p in pairs])
        pending.append((tag, items, send, recv, pairs, lands))
        return token

    dh, grads, rep = local_step(x[0], mem[0], loss_target[0], stacked, full, rest_weights, on_grads)

    halves = {}
    last = [("dn_w_in", 0, _to_shards("dn_w_in", grads["dn_w_in"]).astype(BF16)), ("dn_w_out", 0, grads["dn_w_out"][0]),
            ("small", 0, _pack({nm: _to_shards(nm, grads[nm]) for nm in SMALL}, SMALL, F32, SMALL_ROW_MULTIPLE))]
    parts = [g.reshape(N_CHIPS, -1, g.shape[-1]) for _, _, g in last]
    theirs = pair_split("pair_split_last", parts)
    pairs = [add_pairs("pair_add_" + nm, p, t, core, p.dtype) for (nm, _, _), p, t in zip(last, parts, theirs)]
    l_send, l_recv, l_pairs, l_lands, _ = ici_start(
        "scatter_start_last", "scatter", pairs, [(N_CHIPS - 1,) + p.shape[1:] for p in pairs])
    for tag, items, send, recv, pairs, lands in pending:
        pairs, lands = ici_wait("scatter_wait_" + tag, "scatter", send, recv, pairs, lands, dh)
        for (nm, layer, _), p, o in zip(items, pairs, lands):
            halves[nm, layer] = add_four("chip_add_%s%d" % (nm, layer), p, o, chip.reshape(1))
    keys = sorted(halves)
    siblings = dict(zip(keys, pair_join("pair_join_early", [halves[k] for k in keys])))

    delta, new_m, new_v, red = {}, {}, {}, {}

    def big_adamw(nm):
        layers = range(wts[nm].shape[0])
        res = adamw_halves("adamw_" + nm, _as_2d(wts[nm]), [halves[nm, l] for l in layers],
                           [siblings[nm, l] for l in layers], _as_2d(mom[nm]), _as_2d(var[nm]), core)
        red[nm], delta[nm], new_m[nm], new_v[nm] = (r.reshape(wts[nm].shape) for r in res)

    early = [nm for nm in BIG if (nm, 0) in halves]
    for nm in early:
        big_adamw(nm)
    done = jnp.concatenate([new_v[nm].reshape(-1)[:1] for nm in early])
    l_pairs, l_lands = ici_wait("scatter_wait_last", "scatter", l_send, l_recv, l_pairs, l_lands, done)
    for (nm, layer, _), p, o in zip(last, l_pairs, l_lands):
        halves[nm, layer] = add_four("chip_add_" + nm, p, o, chip.reshape(1))
    keys = [(nm, layer) for nm, layer, _ in last]
    siblings.update(zip(keys, pair_join("pair_join_last", [halves[k] for k in keys])))
    south = core[0] == 0
    mine, theirs = halves["small", 0], siblings["small", 0]
    red.update(_unpack(jnp.concatenate([jnp.where(south, mine, theirs), jnp.where(south, theirs, mine)], axis=0),
                       SMALL))

    rep = all_sum_small(rep)
    red["dn_norm"] = rep[0:1]
    red["dn_a_log"] = rep[1:2, DN_HEADS:2 * DN_HEADS]
    red["dn_dt_bias"] = rep[2:3, DN_HEADS:2 * DN_HEADS]
    red["dn_out_norm"] = rep[3:4, :LANES]
    red["xa_norm"], red["xa_mem_norm"], red["mlp_norm"] = rep[4:6], rep[6:8], rep[8:10]
    red["final_norm"] = rep[10]
    loss = rep[11, 0]

    for nm in WEIGHTS:
        shp = wts[nm].shape
        if nm in early:
            continue
        if nm in BIG:
            big_adamw(nm)
            continue
        res = adamw("adamw_" + nm, _as_2d(wts[nm]), _as_2d(red[nm].reshape(shp)), _as_2d(mom[nm]), _as_2d(var[nm]))
        delta[nm], new_m[nm], new_v[nm] = (r.reshape(shp) for r in res)
        red[nm] = red[nm].reshape(shp)

    grad_x = dh[None]
    return (loss, grad_x, *[red[nm] for nm in WEIGHTS], *[delta[nm] for nm in WEIGHTS],
            *[new_m[nm] for nm in WEIGHTS], *[new_v[nm] for nm in WEIGHTS])


def local_step(h0, mem0, target, stacked, full, rest_weights=None, on_grads=None):
    d = h0.shape[1]
    dn_norm, dn_a_log, dn_dt_bias, dn_out_norm = (full[nm] for nm in REPLICATED[:4])
    xa_norm, xa_mem_norm, mlp_norm, final_norm = (full[nm] for nm in REPLICATED[4:])
    inner = DN_HEADS * DN_HEAD_DIM
    w_in = full["dn_w_in"][0]
    w_qkv, w_z = w_in[:, :3 * inner], w_in[:, 3 * inner:4 * inner]
    w_ba = jnp.pad(w_in[:, 4 * inner:], ((0, 0), (0, LANES - 2 * DN_HEADS)))
    w_conv = jnp.pad(full["dn_w_conv"][0], ((0, 8 - DN_CONV), (0, 0)))
    gate = _gate_tile(dn_a_log, dn_dt_bias)
    w_dw = jnp.pad(full["cv_w_dw"][0], ((0, CV_HALO - CV_WIDTH), (0, 0)))

    def sw(nm, layer):
        return Stacked(stacked[nm], "rows" if SHARD_AXIS[nm] == 1 else "cols", layer)

    dn_args = (_row(dn_norm), w_qkv, w_z, w_ba, w_conv, gate, _row(dn_out_norm), sw("dn_w_out", 0))
    h1, n, dn_saved = dn_fwd(h0, *dn_args, next_gain=_row(xa_norm[0]))
    if rest_weights is not None:
        stacked = {**stacked, **rest_weights(h1)}
    xa_args = [(_row(xa_norm[l]), _row(xa_mem_norm[l]), sw("xa_w_q", l), sw("xa_w_kv", l), sw("xa_w_o", l))
               for l in range(2)]
    mlp_args = [(_row(mlp_norm[l]), sw("mlp_w_up", l), sw("mlp_w_down", l)) for l in range(2)]
    cv_args = (_row(full["cv_norm"][0]), sw("cv_w_pw1", 0), full["cv_b_pw1"], w_dw, full["cv_b_dw"],
               full["cv_ln_g"], full["cv_ln_b"], sw("cv_w_pw2", 0), full["cv_b_pw2"])
    h2, n, xa0_saved = xa_fwd("xa0", h1, mem0, *xa_args[0], n=n, next_gain=mlp_args[0][0])
    h3, n, mlp0_saved = mlp_fwd("mlp0", h2, *mlp_args[0], n=n, next_gain=cv_args[0])
    h4, n, cv_saved = cv_fwd(h3, *cv_args, n=n, next_gain=xa_args[1][0])
    h5, n, xa1_saved = xa_fwd("xa1", h4, mem0, *xa_args[1], n=n, next_gain=mlp_args[1][0])
    h6, _, mlp1_saved = mlp_fwd("mlp1", h5, *mlp_args[1], n=n)

    dh32, dh16, loss_tile, d_final = loss_head("loss_head", h6, _row(final_norm), target)
    dh = (dh32, dh16)
    grads = {}
    dg_mlp, dg_xa, dg_xa_mem = [None, None], [None, None], [None, None]
    dw_mlp, dw_xa = [None, None], [None, None]
    mlp_names, xa_names = ("mlp_w_up", "mlp_w_down"), ("xa_w_q", "xa_w_kv", "xa_w_o")

    def announce(items):
        return None if on_grads is None else on_grads(items)

    dh, dg_mlp[1], dw_mlp[1] = mlp_bwd("mlp1", dh, h5, *mlp_args[1], mlp1_saved)
    dh, dg_xa[1], dg_xa_mem[1], dw_xa[1] = xa_bwd("xa1", dh, h4, mem0, *xa_args[1], xa1_saved)
    (dh, grads["cv_norm"], dw_pw1, grads["cv_b_pw1"], dw_dw, ln_acc, dw_pw2,
     grads["cv_b_pw2"]) = cv_bwd(dh, h3, cv_args[0], cv_args[1], w_dw, cv_args[5], cv_args[6], cv_args[7], cv_saved)
    after = announce([(nm, 1, g) for nm, g in zip(mlp_names + xa_names, dw_mlp[1] + dw_xa[1])]
                     + [("cv_w_pw1", 0, dw_pw1), ("cv_w_pw2", 0, dw_pw2)])
    dh, dg_mlp[0], dw_mlp[0] = mlp_bwd("mlp0", dh, h2, *mlp_args[0], mlp0_saved, after=after)
    dh, dg_xa[0], dg_xa_mem[0], dw_xa[0] = xa_bwd("xa0", dh, h1, mem0, *xa_args[0], xa0_saved)
    after = announce([(nm, 0, g) for nm, g in zip(mlp_names + xa_names, dw_mlp[0] + dw_xa[0])])
    dh, dg_dn, dw_qkv, dw_z, dw_ba, dw_conv, d_gate, d_out_norm, dw_out = dn_bwd(dh, h0, *dn_args, dn_saved,
                                                                                 after=after)

    grads["dn_w_in"] = jnp.concatenate([dw_qkv, dw_z, dw_ba[:, :2 * DN_HEADS]], axis=1)[None]
    grads["dn_w_conv"] = dw_conv[None, :DN_CONV]
    grads["dn_w_out"], grads["cv_w_pw1"], grads["cv_w_pw2"] = [dw_out], [dw_pw1], [dw_pw2]
    grads["cv_w_dw"] = dw_dw[None, :CV_WIDTH]
    grads["cv_ln_g"], grads["cv_ln_b"], grads["cv_b_dw"] = ln_acc[0:1], ln_acc[1:2], ln_acc[2:3]
    for i, nm in enumerate(mlp_names):
        grads[nm] = [dw_mlp[0][i], dw_mlp[1][i]]
    for i, nm in enumerate(xa_names):
        grads[nm] = [dw_xa[0][i], dw_xa[1][i]]

    rep = jnp.zeros((16, d), F32)
    rep = rep.at[0].set(dg_dn[0])
    rep = rep.at[1, :LANES].set(d_gate[0])
    rep = rep.at[2, :LANES].set(d_gate[1])
    rep = rep.at[3, :LANES].set(d_out_norm[0])
    rep = rep.at[4].set(dg_xa[0][0]).at[5].set(dg_xa[1][0])
    rep = rep.at[6].set(dg_xa_mem[0][0]).at[7].set(dg_xa_mem[1][0])
    rep = rep.at[8].set(dg_mlp[0][0]).at[9].set(dg_mlp[1][0])
    rep = rep.at[10].set(d_final[0])
    rep = rep.at[11, :LANES].set(loss_tile[0])
    return dh, grads, rep
```

```python
import functools

import jax
import jax.numpy as jnp
from jax import lax
from jax.experimental import pallas as pl
from jax.experimental.pallas import tpu as pltpu

F32 = jnp.float32
BF16 = jnp.bfloat16
HIGHEST = lax.Precision.HIGHEST
MESH = pl.DeviceIdType.MESH

D_MODEL = 1024
DN_HEADS = 8
DN_HEAD_DIM = 128
DN_CONV = 4
DN_CHUNK = 64
CV_WIDTH = 31
XA_HEADS = 4
XA_HEAD_DIM = 256
RMS_EPS = 1e-6
LN_EPS = 1e-5
L2_EPS = 1e-6

ADAM_LR = 0.001
ADAM_B1 = 0.9
ADAM_B2 = 0.999
ADAM_EPS = 1e-08
ADAM_WD = 0.01
ADAM_STEP = 10

LANES = 128
ROW_TILE = 512
CONV_ROW_TILE = 256
MM_TILE = 1024
N_STRIPS = 4
LONG_TILE_K = 2048
ADAMW_ROW_TILE = 256
DN_ROW_TILE = 256
CHUNK_SHIFT = 6
SOLVE_INTERLEAVE = 8
FWD_HEADS_PER_STEP = 8
BWD_HEADS_PER_STEP = 8
BWD_SCAN_ROWS = 256
DN_HALO = 8
CV_HALO = 32
VMEM_LIMIT = 48 * 1024 * 1024
N_CHIPS = 4
D2D_CHUNK_ROWS = 256


def _cparams(sem):
    return pltpu.CompilerParams(dimension_semantics=sem, vmem_limit_bytes=VMEM_LIMIT)


def _dot(a, b, dims=(((1,), (0,)), ((), ()))):
    return lax.dot_general(a.astype(BF16), b.astype(BF16), dims, preferred_element_type=F32)


def _dot_nt(a, b):
    return _dot(a, b, (((1,), (1,)), ((), ())))


def _dot_tn(a, b):
    return _dot(a, b, (((0,), (0,)), ((), ())))


def _dot_hi(a, b, dims=(((1,), (0,)), ((), ()))):
    return lax.dot_general(a.astype(F32), b.astype(F32), dims, precision=HIGHEST, preferred_element_type=F32)


def _dot_x3(a, b, dims=(((1,), (0,)), ((), ()))):
    a_hi, b_hi = a.astype(BF16), b.astype(BF16)
    a_lo = (a - a_hi.astype(F32)).astype(BF16)
    b_lo = (b - b_hi.astype(F32)).astype(BF16)

    def dot(p, q):
        return lax.dot_general(p, q, dims, preferred_element_type=F32)

    return dot(a_hi, b_hi) + (dot(a_hi, b_lo) + dot(a_lo, b_hi))


def _sigmoid(x):
    return 1.0 / (1.0 + jnp.exp(-x))


def _silu(x):
    return x * _sigmoid(x)


def _silu_grad(x):
    s = _sigmoid(x)
    return s * (1.0 + x * (1.0 - s))


def _softplus(x):
    return jnp.maximum(x, 0.0) + jnp.log(1.0 + jnp.exp(-jnp.abs(x)))


def _iota(shape, dim):
    return lax.broadcasted_iota(jnp.int32, shape, dim)


def _lane_col(vals, lane, idx):
    return jnp.sum(jnp.where(lane == idx, vals, 0.0), axis=1, keepdims=True)


def _pick_tile(rows, cap):
    best = rows
    for t in range(16, min(rows, cap) + 1, 16):
        if rows % t == 0:
            best = t
    return best


def _stacked_spec(shape, split, layer, rows, cols, block_index):
    r_shard, c_shard = shape[-2], shape[-1]
    if split == "rows" and rows > r_shard:
        assert rows % r_shard == 0 and c_shard % cols == 0
        chips = rows // r_shard

        def slabs(i, j, kk):
            bi, bj = block_index(i, j, kk)
            return (bi, layer, 0, bj)

        return pl.BlockSpec((chips, None, r_shard, cols), slabs), chips
    assert r_shard % rows == 0 and c_shard % cols == 0
    per_chip = (r_shard // rows) if split == "rows" else (c_shard // cols)

    def index(i, j, kk):
        bi, bj = block_index(i, j, kk)
        if split == "rows":
            return (bi // per_chip, layer, bi % per_chip, bj)
        return (bj // per_chip, layer, bi, bj % per_chip)

    return pl.BlockSpec((None, None, rows, cols), index), 1


def mm(name, a, b, *, ta=False, tb=False, out_dtype=F32, pro=None, epi=None, epi_tiles=(), epi_rows=(),
       tm=MM_TILE, tn=MM_TILE, tk=MM_TILE, b_split=None, b_layer=None, out_split=None, out_layer=None,
       after=None, norm_gain=None):
    m, k = (a.shape[1], a.shape[0]) if ta else a.shape
    b_rows, b_cols = b.shape[-2], b.shape[-1]
    if b_split == "rows":
        b_rows *= N_CHIPS
    elif b_split == "cols":
        b_cols *= N_CHIPS
    n = b_rows if tb else b_cols
    assert (b_cols if tb else b_rows) == k
    tm, tn, tk = min(tm, m), min(tn, n), min(tk, k)
    if b_split == "cols":
        if tb:
            tk = min(tk, b.shape[-1])
        else:
            tn = min(tn, b.shape[-1])
    if out_split == "cols":
        tn = min(tn, n // N_CHIPS)
    assert m % tm == 0 and n % tn == 0 and k % tk == 0
    nk = k // tk
    a_spec = pl.BlockSpec((tk, tm), lambda i, j, kk: (kk, i)) if ta else pl.BlockSpec((tm, tk), lambda i, j, kk: (i, kk))
    b_block = (tn, tk) if tb else (tk, tn)
    b_index = (lambda i, j, kk: (j, kk)) if tb else (lambda i, j, kk: (kk, j))
    b_chips = o_chips = 1
    if b_split is None:
        b_spec = pl.BlockSpec(b_block, b_index)
    else:
        b_spec, b_chips = _stacked_spec(b.shape, b_split, b_layer, b_block[0], b_block[1], b_index)
    in_specs = [a_spec, b_spec]
    in_specs += [pl.BlockSpec((tm, tn), lambda i, j, kk: (i, j)) for _ in epi_tiles]
    in_specs += [pl.BlockSpec((1, tn), lambda i, j, kk: (0, j)) for _ in epi_rows]
    n_t, n_r = len(epi_tiles), len(epi_rows)
    dims = (((0 if ta else 1,), (1 if tb else 0,)), ((), ()))
    if out_split is None:
        out_shape = jax.ShapeDtypeStruct((m, n), out_dtype)
        out_spec = pl.BlockSpec((tm, tn), lambda i, j, kk: (i, j))
    else:
        shard = (m // N_CHIPS, n) if out_split == "rows" else (m, n // N_CHIPS)
        out_shape = jax.ShapeDtypeStruct((N_CHIPS, out_layer[1]) + shard, out_dtype)
        out_spec, o_chips = _stacked_spec(out_shape.shape, out_split, out_layer[0], tm, tn, lambda i, j, kk: (i, j))
    single_pass = (nk == 1 and norm_gain is None and b_chips == 1 and o_chips == 1
                   and tn % (N_STRIPS * LANES) == 0)
    extra = []
    if norm_gain is not None:
        assert tn == n and out_split is None
        extra.append(norm_gain)
        in_specs.append(pl.BlockSpec((1, n), lambda i, j, kk: (0, 0)))
        out_shape = [out_shape, jax.ShapeDtypeStruct((m, n), BF16)]
        out_spec = [out_spec, pl.BlockSpec((tm, tn), lambda i, j, kk: (i, j))]
    if after is not None:
        extra.append(after)
        in_specs.append(pl.BlockSpec(memory_space=pl.ANY))

    def body(a_ref, b_ref, *rest):
        tiles = rest[:n_t]
        rows = rest[n_t:n_t + n_r]
        gain_ref = rest[n_t + n_r] if norm_gain is not None else None
        rest = rest[n_t + n_r + len(extra):]
        o_ref, acc_ref = rest[0], rest[-1]
        av = a_ref[...]
        if pro is not None:
            av = pro(av)
        if single_pass:
            strip = tn // N_STRIPS
            for s in range(N_STRIPS):
                cols = slice(s * strip, (s + 1) * strip)
                part = _dot(av, b_ref[cols, :] if tb else b_ref[:, cols], dims)
                if epi is not None:
                    part = epi(part, *[t[:, cols] for t in tiles], *[r[:, cols] for r in rows])
                o_ref[:, cols] = part.astype(out_dtype)
            return
        kk = pl.program_id(2)

        @pl.when(kk == 0)
        def _():
            acc_ref[...] = jnp.zeros_like(acc_ref)

        bv = b_ref[...]
        if b_chips > 1:
            bv = bv.reshape(b_block)
        acc_ref[...] += _dot(av, bv, dims)

        @pl.when(kk == nk - 1)
        def _():
            out = acc_ref[...]
            if epi is not None:
                out = epi(out, *[t[...] for t in tiles], *[r[...] for r in rows])
            if gain_ref is not None:
                rest[1][...] = (_rms_stats(out)[0] * gain_ref[...]).astype(BF16)
            out = out.astype(out_dtype)
            o_ref[...] = out.reshape(o_chips, tm // o_chips, tn) if o_chips > 1 else out

    return pl.pallas_call(
        body, name=name, grid=(m // tm, n // tn, nk),
        in_specs=in_specs, out_specs=out_spec, out_shape=out_shape,
        scratch_shapes=[pltpu.VMEM((tm, tn), F32)],
        compiler_params=_cparams(("parallel", "parallel", "arbitrary")),
    )(a, b, *epi_tiles, *epi_rows, *extra)


def row_call(name, body, n_rows, tm, ins, outs, accs=()):
    tm = _pick_tile(n_rows, tm)
    in_specs = []
    for arr, kind in ins:
        if kind == "tile":
            if arr.ndim == 2:
                in_specs.append(pl.BlockSpec((tm, arr.shape[1]), lambda i: (i, 0)))
            else:
                in_specs.append(pl.BlockSpec((arr.shape[0], tm, arr.shape[2]), lambda i: (0, i, 0)))
        elif kind == "full":
            in_specs.append(pl.BlockSpec(arr.shape, functools.partial(lambda i, nd: (0,) * nd, nd=arr.ndim)))
        else:
            where, h = kind
            per = tm // h
            if where == "prev":
                in_specs.append(pl.BlockSpec((h, arr.shape[1]), functools.partial(
                    lambda i, per: (jnp.maximum(i * per - 1, 0), 0), per=per)))
            else:
                last = n_rows // h - 1
                in_specs.append(pl.BlockSpec((h, arr.shape[1]), functools.partial(
                    lambda i, per, last: (jnp.minimum((i + 1) * per, last), 0), per=per, last=last)))
    out_shape, out_specs = [], []
    for shape, dtype in outs:
        out_shape.append(jax.ShapeDtypeStruct(shape, dtype))
        if len(shape) == 2:
            out_specs.append(pl.BlockSpec((tm, shape[1]), lambda i: (i, 0)))
        else:
            out_specs.append(pl.BlockSpec((shape[0], tm, shape[2]), lambda i: (0, i, 0)))
    for shape in accs:
        out_shape.append(jax.ShapeDtypeStruct(shape, F32))
        out_specs.append(pl.BlockSpec(shape, lambda i: (0, 0)))
    n_in, n_out, n_acc = len(ins), len(outs), len(accs)

    def kern(*refs):
        i = pl.program_id(0)
        in_refs = refs[:n_in]
        out_refs = refs[n_in:n_in + n_out]
        acc_refs = refs[n_in + n_out:n_in + n_out + n_acc]
        if n_acc:
            @pl.when(i == 0)
            def _():
                for r in acc_refs:
                    r[...] = jnp.zeros_like(r)
        body(i, in_refs, out_refs, acc_refs)

    res = pl.pallas_call(
        kern, name=name, grid=(n_rows // tm,), in_specs=in_specs, out_specs=out_specs, out_shape=out_shape,
        compiler_params=_cparams(("arbitrary",) if n_acc else ("parallel",)),
    )(*[a for a, _ in ins])
    return list(res)


def _rms_stats(h):
    r = lax.rsqrt(jnp.mean(h * h, axis=-1, keepdims=True) + RMS_EPS)
    return h * r, r


def rms_fwd(name, h, g):
    def body(i, ins, outs, accs):
        xhat, _ = _rms_stats(ins[0][...])
        outs[0][...] = (xhat * ins[1][...]).astype(BF16)

    return row_call(name, body, h.shape[0], ROW_TILE, [(h, "tile"), (g, "full")], [(h.shape, BF16)])[0]


def _rms_bwd_tile(dn, h, g):
    xhat, r = _rms_stats(h)
    dxhat = dn * g
    dh = r * (dxhat - xhat * jnp.mean(dxhat * xhat, axis=-1, keepdims=True))
    dg = jnp.sum(dn * xhat, axis=0, keepdims=True)
    return dh, dg


def rms_bwd(name, dn, h, g, dres):
    def body(i, ins, outs, accs):
        dh, dg = _rms_bwd_tile(ins[0][...].astype(F32), ins[1][...], ins[2][...])
        total = ins[3][...] + dh
        outs[0][...] = total
        outs[1][...] = total.astype(BF16)
        accs[0][...] += dg

    d = h.shape[1]
    out, out16, dg = row_call(name, body, h.shape[0], ROW_TILE,
                              [(dn, "tile"), (h, "tile"), (g, "full"), (dres, "tile")],
                              [(h.shape, F32), (h.shape, BF16)], [(1, d)])
    return out, out16, dg


def mem_norm_bwd(name, dn, mem, g):
    def body(i, ins, outs, accs):
        _, dg = _rms_bwd_tile(ins[0][...].astype(F32), ins[1][...], ins[2][...])
        accs[0][...] += dg

    return row_call(name, body, mem.shape[0], ROW_TILE, [(dn, "tile"), (mem, "tile"), (g, "full")], [],
                    [(1, mem.shape[1])])[0]


def loss_head(name, h, g, target):
    d = h.shape[1]

    def body(i, ins, outs, accs):
        hv, gv = ins[0][...], ins[1][...]
        xhat, _ = _rms_stats(hv)
        err = xhat * gv - ins[2][...]
        dy = err * (1.0 / d)
        dh, dg = _rms_bwd_tile(dy, hv, gv)
        outs[0][...] = dh
        outs[1][...] = dh.astype(BF16)
        accs[0][...] += jnp.full((8, LANES), 0.5 / d, F32) * jnp.sum(err * err)
        accs[1][...] += dg

    dh, dh16, loss, dg = row_call(name, body, h.shape[0], ROW_TILE, [(h, "tile"), (g, "full"), (target, "tile")],
                                  [(h.shape, F32), (h.shape, BF16)], [(8, LANES), (1, d)])
    return dh, dh16, loss, dg


def col_sum(name, x):
    def body(i, ins, outs, accs):
        accs[0][...] += jnp.sum(ins[0][...].astype(F32), axis=0, keepdims=True)

    return row_call(name, body, x.shape[0], ROW_TILE, [(x, "tile")], [], [(1, x.shape[1])])[0]


def _conv_taps(xcat, w_ref, cols, width, halo, tm):
    rows = halo + tm
    acc = None
    for j in range(width):
        s = width - 1 - j
        xs = xcat if s == 0 else pltpu.roll(xcat, s, 0)
        term = xs[halo:rows] * w_ref[j:j + 1, cols]
        acc = term if acc is None else acc + term
    return acc


def _conv_taps_bwd_x(dcat, w_ref, cols, width, halo, tm):
    rows = halo + tm
    acc = None
    for j in range(width):
        s = width - 1 - j
        ds = dcat if s == 0 else pltpu.roll(dcat, rows - s, 0)
        term = ds[0:tm] * w_ref[j:j + 1, cols]
        acc = term if acc is None else acc + term
    return acc


def _conv_taps_bwd_w(dy, xcat, width, halo, tm, wrows):
    rows = halo + tm
    rid = _iota((wrows, dy.shape[1]), 0)
    out = jnp.zeros((wrows, dy.shape[1]), F32)
    for j in range(width):
        s = width - 1 - j
        xs = xcat if s == 0 else pltpu.roll(xcat, s, 0)
        v = jnp.sum(dy * xs[halo:rows], axis=0, keepdims=True)
        out = out + jnp.where(rid == j, v, 0.0)
    return out


def dn_pre(qkv_raw, ba, w_conv, gate):
    s_len = qkv_raw.shape[0]
    tm = min(DN_ROW_TILE, s_len)
    n_blk = qkv_raw.shape[1] // LANES

    def body(i, ins, outs, accs):
        x_ref, xp_ref, ba_ref, w_ref, gate_ref = ins
        qkv_ref, hs_ref = outs

        def blk(cb, carry):
            cols = pl.ds(pl.multiple_of(cb * LANES, LANES), LANES)
            prev = jnp.where(i > 0, xp_ref[:, cols], 0.0)
            xcat = jnp.concatenate([prev, x_ref[:, cols]], axis=0)
            c = _conv_taps(xcat, w_ref, cols, DN_CONV, DN_HALO, tm)
            y = _silu(c)
            rs = lax.rsqrt(jnp.sum(y * y, axis=-1, keepdims=True) + L2_EPS)
            fac = jnp.where(cb < DN_HEADS, DN_HEAD_DIM ** -0.5, 1.0)
            qkv_ref[:, cols] = jnp.where(cb < 2 * DN_HEADS, y * (rs * fac), y)
            return carry

        lax.fori_loop(0, n_blk, blk, 0)

        bav = ba_ref[...]
        beta = _sigmoid(bav)
        g = -jnp.exp(gate_ref[0:1, :]) * _softplus(bav + gate_ref[1:2, :])
        lane = _iota((tm, LANES), 1)
        g = jnp.where((lane >= DN_HEADS) & (lane < 2 * DN_HEADS), g, 0.0)
        r = _iota((tm, tm), 0)
        c = _iota((tm, tm), 1)
        tri = jnp.where((r >= c) & ((r >> CHUNK_SHIFT) == (c >> CHUNK_SHIFT)), 1.0, 0.0)
        gc = _dot_hi(tri, g)
        for h in range(DN_HEADS):
            hs_ref[h] = jnp.where(lane == 0, _lane_col(beta, lane, h),
                                  jnp.where(lane == 1, _lane_col(g, lane, DN_HEADS + h),
                                            jnp.where(lane == 2, _lane_col(gc, lane, DN_HEADS + h), 0.0)))

    return row_call("dn_pre", body, s_len, tm,
                    [(qkv_raw, "tile"), (qkv_raw, ("prev", DN_HALO)), (ba, "tile"), (w_conv, "full"), (gate, "full")],
                    [(qkv_raw.shape, F32), ((DN_HEADS, s_len, LANES), F32)])


def _chunk_masks():
    r = _iota((DN_CHUNK, DN_CHUNK), 0)
    c = _iota((DN_CHUNK, DN_CHUNK), 1)
    return r, c


def _decay_matrix(gc, r, c):
    gc_row = jnp.sum(jnp.where(r == c, gc, 0.0), axis=0, keepdims=True)
    causal = r >= c
    return jnp.where(causal, jnp.exp(jnp.where(causal, gc - gc_row, 0.0)), 0.0)


def _tri_inverse(lows, r, c):
    eye = jnp.where(r == c, 1.0, 0.0)
    ts = [eye for _ in lows]
    b = 1
    while b < DN_CHUNK:
        shift = b.bit_length()
        sel = ((r >> shift) == (c >> shift)) & ((r & b) != 0) & ((c & b) == 0)
        lms = [jnp.where(sel, low, 0.0) for low in lows]
        if b == 1:
            ts = [t - lm for t, lm in zip(ts, lms)]
        else:
            t_lm = [_dot_x3(t, lm) for t, lm in zip(ts, lms)]
            t_lm_t = [_dot_x3(x, t) for x, t in zip(t_lm, ts)]
            ts = [t - x for t, x in zip(ts, t_lm_t)]
        b *= 2
    return ts


def dn_solve(qkv, hs):
    s_len = qkv.shape[0]
    rb = min(ROW_TILE, s_len)
    n_chunk = rb // DN_CHUNK
    interleave = min(SOLVE_INTERLEAVE, n_chunk)

    def body(k_ref, v_ref, hs_ref, u_ref, w_ref, t_ref):
        r, c = _chunk_masks()

        def group(gi, carry):
            rows = [pl.ds(pl.multiple_of((gi * interleave + j) * DN_CHUNK, DN_CHUNK), DN_CHUNK)
                    for j in range(interleave)]
            k = [k_ref[rw, :] for rw in rows]
            beta = [hs_ref[rw, 0:1] for rw in rows]
            gc = [hs_ref[rw, 2:3] for rw in rows]
            kb = [a * b for a, b in zip(k, beta)]
            decay = [_decay_matrix(g, r, c) for g in gc]
            lows = [jnp.where(r > c, _dot_nt(a, b) * d, 0.0) for a, b, d in zip(kb, k, decay)]
            ts = _tri_inverse(lows, r, c)
            us = [_dot_x3(t, v_ref[rw, :] * b) for t, rw, b in zip(ts, rows, beta)]
            ws = [_dot_x3(t, a * jnp.exp(g)) for t, a, g in zip(ts, kb, gc)]
            for j, rw in enumerate(rows):
                u_ref[rw, :] = us[j]
                w_ref[rw, :] = ws[j].astype(BF16)
                t_ref[rw, :] = ts[j]
            return carry

        lax.fori_loop(0, n_chunk // interleave, group, 0)

    return pl.pallas_call(
        body, name="dn_solve", grid=(DN_HEADS, s_len // rb),
        in_specs=[pl.BlockSpec((rb, LANES), lambda h, i: (i, DN_HEADS + h)),
                  pl.BlockSpec((rb, LANES), lambda h, i: (i, 2 * DN_HEADS + h)),
                  pl.BlockSpec((None, rb, LANES), lambda h, i: (h, i, 0))],
        out_specs=[pl.BlockSpec((rb, LANES), lambda h, i: (i, h)),
                   pl.BlockSpec((rb, LANES), lambda h, i: (i, h)),
                   pl.BlockSpec((None, rb, DN_CHUNK), lambda h, i: (h, i, 0))],
        out_shape=[jax.ShapeDtypeStruct((s_len, DN_HEADS * LANES), F32),
                   jax.ShapeDtypeStruct((s_len, DN_HEADS * LANES), BF16),
                   jax.ShapeDtypeStruct((DN_HEADS, s_len, DN_CHUNK), F32)],
        compiler_params=_cparams(("parallel", "parallel")),
    )(qkv, qkv, hs)


def dn_scan_fwd(qkv, u, w, hs):
    s_len = qkv.shape[0]
    rb = min(ROW_TILE, s_len)
    n_chunk = rb // DN_CHUNK
    total_chunks = s_len // DN_CHUNK

    hps = FWD_HEADS_PER_STEP
    groups = DN_HEADS // hps

    def body(q_ref, k_ref, u_ref, w_ref, hs_ref, o_ref, st_ref, state):
        @pl.when(pl.program_id(1) == 0)
        def _():
            state[...] = jnp.zeros_like(state)

        r, c = _chunk_masks()

        def chunk(n, carry):
            rows = pl.ds(pl.multiple_of(n * DN_CHUNK, DN_CHUNK), DN_CHUNK)
            heads = range(hps)
            cols = [slice(h * LANES, (h + 1) * LANES) for h in heads]
            each = lambda f, *xs: [f(*a) for a in zip(*xs)]
            q = [q_ref[rows, cl] for cl in cols]
            k = [k_ref[rows, cl] for cl in cols]
            gc = [hs_ref[h, rows, 2:3] for h in heads]
            st = [state[h] for h in heads]
            for h in heads:
                st_ref[h, n] = st[h]
            gl = each(lambda g: jnp.min(g, axis=0, keepdims=True), gc)
            decay = each(lambda g: _decay_matrix(g, r, c), gc)
            w_st = [_dot(w_ref[rows, cols[h]], st[h]) for h in heads]
            qk = each(_dot_nt, q, k)
            q_st = each(lambda a, g, s: _dot(a * jnp.exp(g), s), q, gc, st)
            vn = [u_ref[rows, cols[h]] - w_st[h] for h in heads]
            ai_vn = each(lambda a, d, b: _dot(a * d, b), qk, decay, vn)
            kd_vn = each(lambda a, g0, g, b: _dot_tn(a * jnp.exp(g0 - g), b), k, gl, gc, vn)
            for h in heads:
                o_ref[rows, cols[h]] = q_st[h] + ai_vn[h]
                state[h] = st[h] * jnp.exp(gl[h]) + kd_vn[h]
            return carry

        lax.fori_loop(0, n_chunk, chunk, 0)

    wide = hps * LANES
    blk = lambda off: pl.BlockSpec((rb, wide), lambda h, i: (i, off + h))
    return pl.pallas_call(
        body, name="dn_scan_fwd", grid=(groups, s_len // rb),
        in_specs=[blk(0), blk(groups), blk(0), blk(0),
                  pl.BlockSpec((hps, rb, LANES), lambda h, i: (h, i, 0))],
        out_specs=[blk(0),
                   pl.BlockSpec((hps, n_chunk, LANES, LANES), lambda h, i: (h, i, 0, 0))],
        out_shape=[jax.ShapeDtypeStruct((s_len, DN_HEADS * LANES), F32),
                   jax.ShapeDtypeStruct((DN_HEADS, total_chunks, LANES, LANES), F32)],
        scratch_shapes=[pltpu.VMEM((hps, LANES, LANES), F32)],
        compiler_params=_cparams(("parallel", "arbitrary")),
    )(qkv, qkv, u, w, hs)


def dn_scan_bwd(qkv, u, w, t_inv, hs, states, d_o):
    s_len = qkv.shape[0]
    rb = min(BWD_SCAN_ROWS, s_len)
    n_chunk = rb // DN_CHUNK
    n_blk = s_len // rb
    hps = BWD_HEADS_PER_STEP
    groups = DN_HEADS // hps

    def body(q_ref, k_ref, v_ref, u_ref, w_ref, t_ref, hs_ref, st_ref, do_ref,
             dq_ref, dk_ref, dv_ref, dhs_ref, dstate):
        @pl.when(pl.program_id(1) == 0)
        def _():
            dstate[...] = jnp.zeros_like(dstate)

        r, c = _chunk_masks()
        causal = r >= c
        strict = r > c
        lane = _iota((DN_CHUNK, LANES), 1)
        upper = jnp.where(r <= c, 1.0, 0.0)
        last_row = _iota((DN_CHUNK, 1), 0) == DN_CHUNK - 1

        def chunk(m, carry):
            n = n_chunk - 1 - m
            rows = pl.ds(pl.multiple_of(n * DN_CHUNK, DN_CHUNK), DN_CHUNK)
            heads = range(hps)
            cols = [slice(h * LANES, (h + 1) * LANES) for h in heads]
            each = lambda f, *xs: [f(*a) for a in zip(*xs)]
            rsum = lambda x: jnp.sum(x, axis=-1, keepdims=True)
            dims_tn = (((0,), (0,)), ((), ()))
            q = [q_ref[rows, cl] for cl in cols]
            k = [k_ref[rows, cl] for cl in cols]
            v = [v_ref[rows, cl] for cl in cols]
            uu = [u_ref[rows, cl] for cl in cols]
            ww = [w_ref[rows, cl] for cl in cols]
            do = [do_ref[rows, cl] for cl in cols]
            tt = [t_ref[h, rows, :] for h in heads]
            beta = [hs_ref[h, rows, 0:1] for h in heads]
            gc = [hs_ref[h, rows, 2:3] for h in heads]
            st = [st_ref[h, n] for h in heads]
            dst = [dstate[h] for h in heads]
            gl = each(lambda g: jnp.min(g, axis=0, keepdims=True), gc)
            egc = each(jnp.exp, gc)
            egl = each(jnp.exp, gl)
            ekd = each(lambda a, b: jnp.exp(a - b), gl, gc)
            decay = each(lambda g: _decay_matrix(g, r, c), gc)
            qd = each(jnp.multiply, q, egc)
            kd = each(jnp.multiply, k, ekd)
            kb = each(jnp.multiply, k, beta)
            w_st = each(_dot, ww, st)
            qk = each(_dot_nt, q, k)
            dqd = each(_dot_nt, do, st)
            kd_dst = each(_dot, kd, dst)
            qd_do = each(_dot_tn, qd, do)
            kbk = each(_dot_nt, kb, k)
            vn = each(jnp.subtract, uu, w_st)
            ai = each(jnp.multiply, qk, decay)
            low = each(lambda a, d: jnp.where(strict, a * d, 0.0), kbk, decay)
            dai = each(lambda a, b: jnp.where(causal, _dot_nt(a, b), 0.0), do, vn)
            ai_do = each(_dot_tn, ai, do)
            dkd = each(_dot_nt, vn, dst)
            dvn = each(jnp.add, ai_do, kd_dst)
            dp = each(jnp.multiply, dai, decay)
            dw = each(lambda a, b: -_dot_nt(a, b), dvn, st)
            w_dvn = each(_dot_tn, ww, dvn)
            dp_k = each(_dot, dp, k)
            dp_q = each(_dot_tn, dp, q)
            drhs_u = each(lambda a, b: _dot_x3(a, b, dims_tn), tt, dvn)
            dgl = each(lambda a, b, e: jnp.sum(a * b) * e, dst, st, egl)
            for h in heads:
                dstate[h] = dst[h] * egl[h] + qd_do[h] - w_dvn[h]
            dq = each(lambda a, e, b: a * e + b, dqd, egc, dp_k)
            dk_a = each(lambda a, e, b: a * e + b, dkd, ekd, dp_q)
            rkd = each(lambda a, b: rsum(a * b), dkd, kd)
            drhs_w = each(lambda a, b: _dot_x3(a, b, dims_tn), tt, dw)
            dl_u = each(_dot_nt, drhs_u, uu)
            dl_w = each(_dot_nt, drhs_w, ww)
            dlow = each(lambda a, b: jnp.where(strict, -(a + b), 0.0), dl_u, dl_w)
            dqm = each(jnp.multiply, dlow, decay)
            m_tot = each(lambda a, b, d, e: a * b + d * e, dai, ai, dlow, low)
            dqm_k = each(_dot, dqm, k)
            dk_l = each(_dot_tn, dqm, kb)
            col_rows = each(lambda m: jnp.sum(m, axis=0, keepdims=True), m_tot)
            col_sums = each(lambda rw: jnp.sum(jnp.where(r == c, rw, 0.0), axis=1, keepdims=True), col_rows)
            dkb_w = each(jnp.multiply, drhs_w, egc)
            dkb = each(jnp.add, dkb_w, dqm_k)
            dgc = [rsum(dqd[h] * qd[h]) - rkd[h] + jnp.where(last_row, jnp.sum(rkd[h]) + dgl[h], 0.0)
                   + rsum(m_tot[h]) + rsum(dkb_w[h] * kb[h]) for h in heads]
            dg = each(lambda a, b: _dot_hi(upper, jnp.where(lane == 1, a - b, 0.0)), dgc, col_sums)
            for h in heads:
                dq_ref[rows, cols[h]] = dq[h]
                dk_ref[rows, cols[h]] = dk_a[h] + dk_l[h] + dkb[h] * beta[h]
                dv_ref[rows, cols[h]] = drhs_u[h] * beta[h]
                dbeta = rsum(drhs_u[h] * v[h]) + rsum(dkb[h] * k[h])
                dhs_ref[h, rows, :] = jnp.where(lane == 0, dbeta, dg[h])
            return carry

        lax.fori_loop(0, n_chunk, chunk, 0)

    wide = hps * LANES
    blk = lambda off: pl.BlockSpec((rb, wide), lambda h, i: (n_blk - 1 - i, off + h))
    head = blk(0)
    hs_spec = pl.BlockSpec((hps, rb, LANES), lambda h, i: (h, n_blk - 1 - i, 0))
    full = jax.ShapeDtypeStruct((s_len, DN_HEADS * LANES), F32)
    return pl.pallas_call(
        body, name="dn_scan_bwd", grid=(groups, n_blk),
        in_specs=[blk(0), blk(groups), blk(2 * groups), head, head,
                  pl.BlockSpec((hps, rb, DN_CHUNK), lambda h, i: (h, n_blk - 1 - i, 0)), hs_spec,
                  pl.BlockSpec((hps, n_chunk, LANES, LANES), lambda h, i: (h, n_blk - 1 - i, 0, 0)), head],
        out_specs=[head, head, head, hs_spec],
        out_shape=[full, full, full, jax.ShapeDtypeStruct((DN_HEADS, s_len, LANES), F32)],
        scratch_shapes=[pltpu.VMEM((hps, LANES, LANES), F32)],
        compiler_params=_cparams(("parallel", "arbitrary")),
    )(qkv, qkv, qkv, u, w, t_inv, hs, states, d_o)


def dn_post(o, z, out_norm):
    def body(i, ins, outs, accs):
        gn = ins[2][...]
        for h in range(DN_HEADS):
            cols = slice(h * LANES, (h + 1) * LANES)
            xhat, _ = _rms_stats(ins[0][:, cols])
            outs[0][:, cols] = (xhat * gn * _silu(ins[1][:, cols])).astype(BF16)

    return row_call("dn_post", body, o.shape[0], ROW_TILE, [(o, "tile"), (z, "tile"), (out_norm, "full")],
                    [(o.shape, BF16)])[0]


def dn_post_bwd(d_og, o, z, out_norm):
    def body(i, ins, outs, accs):
        gn = ins[3][...]
        dgn = jnp.zeros((1, LANES), F32)
        for h in range(DN_HEADS):
            cols = slice(h * LANES, (h + 1) * LANES)
            dy, zh = ins[0][:, cols].astype(F32), ins[2][:, cols]
            xhat, r = _rms_stats(ins[1][:, cols])
            sz = _silu(zh)
            dgn = dgn + jnp.sum(dy * xhat * sz, axis=0, keepdims=True)
            outs[1][:, cols] = (dy * xhat * gn * _silu_grad(zh)).astype(BF16)
            dxhat = dy * gn * sz
            outs[0][:, cols] = r * (dxhat - xhat * jnp.mean(dxhat * xhat, axis=-1, keepdims=True))
        accs[0][...] += dgn

    return row_call("dn_post_bwd", body, o.shape[0], ROW_TILE,
                    [(d_og, "tile"), (o, "tile"), (z, "tile"), (out_norm, "full")],
                    [(o.shape, F32), (o.shape, BF16)], [(1, LANES)])


def dn_pre_bwd(dq, dk, dv, dhs, qkv_raw, ba, w_conv, gate):
    s_len = qkv_raw.shape[0]
    tm = min(DN_ROW_TILE, s_len)

    def body(i, ins, outs, accs):
        dq_ref, dk_ref, dv_ref, dhs_ref, x_ref, xp_ref, ba_ref, w_ref, gate_ref = ins
        dc_ref, dba_ref = outs

        def blk(cb, carry):
            cols = pl.ds(pl.multiple_of(cb * LANES, LANES), LANES)
            hcols = pl.ds(pl.multiple_of((cb & (DN_HEADS - 1)) * LANES, LANES), LANES)
            prev = jnp.where(i > 0, xp_ref[:, cols], 0.0)
            xcat = jnp.concatenate([prev, x_ref[:, cols]], axis=0)
            c = _conv_taps(xcat, w_ref, cols, DN_CONV, DN_HALO, tm)
            y = _silu(c)
            dy = jnp.where(cb < DN_HEADS, dq_ref[:, hcols],
                           jnp.where(cb < 2 * DN_HEADS, dk_ref[:, hcols], dv_ref[:, hcols]))
            rs = lax.rsqrt(jnp.sum(y * y, axis=-1, keepdims=True) + L2_EPS)
            fac = jnp.where(cb < DN_HEADS, DN_HEAD_DIM ** -0.5, 1.0)
            nrm = y * rs
            dn = dy * fac
            dy_norm = rs * (dn - nrm * jnp.sum(dn * nrm, axis=-1, keepdims=True))
            dc_ref[:, cols] = jnp.where(cb < 2 * DN_HEADS, dy_norm, dy) * _silu_grad(c)
            return carry

        lax.fori_loop(0, qkv_raw.shape[1] // LANES, blk, 0)

        lane = _iota((tm, LANES), 1)
        dbeta = jnp.zeros((tm, LANES), F32)
        dg = jnp.zeros((tm, LANES), F32)
        for h in range(DN_HEADS):
            dbeta = dbeta + jnp.where(lane == h, dhs_ref[h, :, 0:1], 0.0)
            dg = dg + jnp.where(lane == DN_HEADS + h, dhs_ref[h, :, 1:2], 0.0)
        bav = ba_ref[...]
        beta = _sigmoid(bav)
        ea = jnp.exp(gate_ref[0:1, :])
        pre = bav + gate_ref[1:2, :]
        g = -ea * _softplus(pre)
        da = dg * (-ea) * _sigmoid(pre)
        dba_ref[...] = (dbeta * beta * (1.0 - beta) + da).astype(BF16)
        rid = _iota((8, LANES), 0)
        accs[0][...] += (jnp.where(rid == 0, jnp.sum(dg * g, axis=0, keepdims=True), 0.0)
                         + jnp.where(rid == 1, jnp.sum(da, axis=0, keepdims=True), 0.0))

    return row_call("dn_pre_bwd", body, s_len, tm,
                    [(dq, "tile"), (dk, "tile"), (dv, "tile"), (dhs, "tile"), (qkv_raw, "tile"),
                     (qkv_raw, ("prev", DN_HALO)), (ba, "tile"), (w_conv, "full"), (gate, "full")],
                    [(qkv_raw.shape, F32), (ba.shape, BF16)], [(8, LANES)])


def dn_conv_bwd(dc, qkv_raw, w_conv):
    s_len = dc.shape[0]
    tm = min(DN_ROW_TILE, s_len)
    nt = s_len // tm

    def body(i, ins, outs, accs):
        dc_ref, dn_ref, x_ref, xp_ref, w_ref = ins

        def blk(cb, carry):
            cols = pl.ds(pl.multiple_of(cb * LANES, LANES), LANES)
            dy = dc_ref[:, cols]
            nxt = jnp.where(i < nt - 1, dn_ref[:, cols], 0.0)
            dcat = jnp.concatenate([dy, nxt], axis=0)
            outs[0][:, cols] = _conv_taps_bwd_x(dcat, w_ref, cols, DN_CONV, DN_HALO, tm).astype(BF16)
            prev = jnp.where(i > 0, xp_ref[:, cols], 0.0)
            xcat = jnp.concatenate([prev, x_ref[:, cols]], axis=0)
            accs[0][:, cols] += _conv_taps_bwd_w(dy, xcat, DN_CONV, DN_HALO, tm, 8)
            return carry

        lax.fori_loop(0, dc.shape[1] // LANES, blk, 0)

    return row_call("dn_conv_bwd", body, s_len, tm,
                    [(dc, "tile"), (dc, ("next", DN_HALO)), (qkv_raw, "tile"), (qkv_raw, ("prev", DN_HALO)),
                     (w_conv, "full")],
                    [(dc.shape, BF16)], [(8, dc.shape[1])])


def _glu(u_ref, cols, d):
    return u_ref[:, cols] * _sigmoid(u_ref[:, pl.ds(pl.multiple_of(d + cols.start, LANES), cols.size)])


def cv_core_fwd(u, w_dw, b_dw, ln_g, ln_b):
    s_len, d = u.shape[0], u.shape[1] // 2
    tm = min(CONV_ROW_TILE, s_len)

    def body(i, ins, outs, accs):
        u_ref, up_ref, w_ref, bdw_ref, g_ref, b_ref = ins
        s_ref, c_ref = outs

        def blk(cb, carry):
            cols = pl.ds(pl.multiple_of(cb * LANES, LANES), LANES)
            prev = jnp.where(i > 0, _glu(up_ref, cols, d), 0.0)
            xcat = jnp.concatenate([prev, _glu(u_ref, cols, d)], axis=0)
            c_ref[:, cols] = _conv_taps(xcat, w_ref, cols, CV_WIDTH, CV_HALO, tm) + bdw_ref[:, cols]
            return carry

        lax.fori_loop(0, d // LANES, blk, 0)
        c = c_ref[...]
        mu = jnp.mean(c, axis=-1, keepdims=True)
        xc = c - mu
        rstd = lax.rsqrt(jnp.mean(xc * xc, axis=-1, keepdims=True) + LN_EPS)
        s_ref[...] = _silu(xc * rstd * g_ref[...] + b_ref[...]).astype(BF16)

    return row_call("cv_core_fwd", body, s_len, tm,
                    [(u, "tile"), (u, ("prev", CV_HALO)), (w_dw, "full"), (b_dw, "full"), (ln_g, "full"),
                     (ln_b, "full")],
                    [((s_len, d), BF16), ((s_len, d), F32)])


def cv_ln_bwd(ds, c, ln_g, ln_b):
    def body(i, ins, outs, accs):
        cv, g = ins[1][...], ins[2][...]
        mu = jnp.mean(cv, axis=-1, keepdims=True)
        xc = cv - mu
        rstd = lax.rsqrt(jnp.mean(xc * xc, axis=-1, keepdims=True) + LN_EPS)
        xhat = xc * rstd
        dl = ins[0][...].astype(F32) * _silu_grad(xhat * g + ins[3][...])
        dxhat = dl * g
        dc = rstd * (dxhat - jnp.mean(dxhat, axis=-1, keepdims=True)
                     - xhat * jnp.mean(dxhat * xhat, axis=-1, keepdims=True))
        outs[0][...] = dc
        rid = _iota((8, cv.shape[1]), 0)
        accs[0][...] += (jnp.where(rid == 0, jnp.sum(dl * xhat, axis=0, keepdims=True), 0.0)
                         + jnp.where(rid == 1, jnp.sum(dl, axis=0, keepdims=True), 0.0)
                         + jnp.where(rid == 2, jnp.sum(dc, axis=0, keepdims=True), 0.0))

    return row_call("cv_ln_bwd", body, c.shape[0], ROW_TILE,
                    [(ds, "tile"), (c, "tile"), (ln_g, "full"), (ln_b, "full")], [(c.shape, F32)], [(8, c.shape[1])])


def cv_conv_bwd(dc, u, w_dw):
    s_len, d = dc.shape
    tm = min(CONV_ROW_TILE, s_len)
    nt = s_len // tm

    def body(i, ins, outs, accs):
        dc_ref, dn_ref, u_ref, up_ref, w_ref = ins

        def blk(cb, carry):
            cols = pl.ds(pl.multiple_of(cb * LANES, LANES), LANES)
            gcols = pl.ds(pl.multiple_of(d + cb * LANES, LANES), LANES)
            dy = dc_ref[:, cols]
            nxt = jnp.where(i < nt - 1, dn_ref[:, cols], 0.0)
            dgl = _conv_taps_bwd_x(jnp.concatenate([dy, nxt], axis=0), w_ref, cols, CV_WIDTH, CV_HALO, tm)
            u1, sg = u_ref[:, cols], _sigmoid(u_ref[:, gcols])
            du1 = dgl * sg
            du2 = dgl * u1 * sg * (1.0 - sg)
            outs[0][:, cols] = du1.astype(BF16)
            outs[0][:, gcols] = du2.astype(BF16)
            accs[1][:, cols] += jnp.sum(du1, axis=0, keepdims=True)
            accs[1][:, gcols] += jnp.sum(du2, axis=0, keepdims=True)
            prev = jnp.where(i > 0, _glu(up_ref, cols, d), 0.0)
            xcat = jnp.concatenate([prev, u1 * sg], axis=0)
            accs[0][:, cols] += _conv_taps_bwd_w(dy, xcat, CV_WIDTH, CV_HALO, tm, CV_HALO)
            return carry

        lax.fori_loop(0, d // LANES, blk, 0)

    return row_call("cv_conv_bwd", body, s_len, tm,
                    [(dc, "tile"), (dc, ("next", CV_HALO)), (u, "tile"), (u, ("prev", CV_HALO)), (w_dw, "full")],
                    [(u.shape, BF16)], [(CV_HALO, d), (1, 2 * d)])


def xa_core_fwd(name, q, kv):
    d = q.shape[1]

    def body(i, ins, outs, accs):
        for h in range(XA_HEADS):
            cols = slice(h * XA_HEAD_DIM, (h + 1) * XA_HEAD_DIM)
            vcols = slice(d + h * XA_HEAD_DIM, d + (h + 1) * XA_HEAD_DIM)
            s = _dot_nt(ins[0][:, cols], ins[1][:, cols]) * (XA_HEAD_DIM ** -0.5)
            e = jnp.exp(s - jnp.max(s, axis=-1, keepdims=True))
            p = e / jnp.sum(e, axis=-1, keepdims=True)
            outs[0][:, cols] = _dot(p, ins[1][:, vcols]).astype(BF16)

    return row_call(name, body, q.shape[0], ROW_TILE, [(q, "tile"), (kv, "full")], [(q.shape, BF16)])[0]


def xa_core_bwd(name, d_o, q, kv):
    d = q.shape[1]

    def body(i, ins, outs, accs):
        for h in range(XA_HEADS):
            cols = slice(h * XA_HEAD_DIM, (h + 1) * XA_HEAD_DIM)
            vcols = slice(d + h * XA_HEAD_DIM, d + (h + 1) * XA_HEAD_DIM)
            qh, kh, vh, doh = ins[1][:, cols], ins[2][:, cols], ins[2][:, vcols], ins[0][:, cols]
            s = _dot_nt(qh, kh) * (XA_HEAD_DIM ** -0.5)
            e = jnp.exp(s - jnp.max(s, axis=-1, keepdims=True))
            p = e / jnp.sum(e, axis=-1, keepdims=True)
            dp = _dot_nt(doh, vh)
            ds = p * (dp - jnp.sum(dp * p, axis=-1, keepdims=True)) * (XA_HEAD_DIM ** -0.5)
            outs[0][:, cols] = _dot(ds, kh).astype(BF16)
            accs[0][:, cols] += _dot_tn(ds, qh)
            accs[0][:, vcols] += _dot_tn(p, doh)

    return row_call(name, body, q.shape[0], ROW_TILE, [(d_o, "tile"), (q, "tile"), (kv, "full")],
                    [(q.shape, BF16)], [kv.shape])


def adamw(name, w, g, m, v):
    def body(i, ins, outs, accs):
        wv, gv = ins[0][...], ins[1][...]
        mn = ADAM_B1 * ins[2][...] + (1.0 - ADAM_B1) * gv
        vn = ADAM_B2 * ins[3][...] + (1.0 - ADAM_B2) * jnp.square(gv)
        m_hat = mn / (1.0 - ADAM_B1 ** ADAM_STEP)
        v_hat = vn / (1.0 - ADAM_B2 ** ADAM_STEP)
        outs[0][...] = -ADAM_LR * (m_hat / (jnp.sqrt(v_hat) + ADAM_EPS) + ADAM_WD * wv)
        outs[1][...] = mn
        outs[2][...] = vn

    return row_call(name, body, w.shape[0], ROW_TILE, [(w, "tile"), (g, "tile"), (m, "tile"), (v, "tile")],
                    [(w.shape, F32)] * 3)


def adamw_halves(name, w, g_mine, g_sibling, m, v, core):
    n_layers = len(g_mine)
    rows, cols = w.shape
    half_rows = rows // n_layers // 2
    tm = _pick_tile(half_rows, ADAMW_ROW_TILE)
    per_half = half_rows // tm

    def body(core_ref, w_ref, *rest):
        g_refs = rest[:2 * n_layers]
        m_ref, v_ref, g_out, d_out, m_out, v_out = rest[2 * n_layers:]
        i = pl.program_id(0)
        mine = ((i // per_half) % 2) == core_ref[0]
        layer = i // (2 * per_half)
        gv = jnp.where(mine, g_refs[0][...], g_refs[n_layers][...])
        for l in range(1, n_layers):
            gv = jnp.where(layer == l, jnp.where(mine, g_refs[l][...], g_refs[n_layers + l][...]), gv)
        mn = ADAM_B1 * m_ref[...] + (1.0 - ADAM_B1) * gv
        vn = ADAM_B2 * v_ref[...] + (1.0 - ADAM_B2) * jnp.square(gv)
        m_hat = mn / (1.0 - ADAM_B1 ** ADAM_STEP)
        v_hat = vn / (1.0 - ADAM_B2 ** ADAM_STEP)
        g_out[...] = gv
        d_out[...] = -ADAM_LR * (m_hat / (jnp.sqrt(v_hat) + ADAM_EPS) + ADAM_WD * w_ref[...])
        m_out[...] = mn
        v_out[...] = vn

    whole = pl.BlockSpec((tm, cols), lambda i, core_ref: (i, 0))
    half = pl.BlockSpec((tm, cols), lambda i, core_ref: (i % per_half, 0))
    return pl.pallas_call(
        body, name=name,
        grid_spec=pltpu.PrefetchScalarGridSpec(
            num_scalar_prefetch=1, grid=(2 * per_half * n_layers,),
            in_specs=[whole] + [half] * (2 * n_layers) + [whole, whole], out_specs=[whole] * 4),
        out_shape=[jax.ShapeDtypeStruct(w.shape, F32)] * 4,
        compiler_params=_cparams(("parallel",)),
    )(core, w, *g_mine, *g_sibling, m, v)


HBM_SPEC = pl.BlockSpec(memory_space=pltpu.HBM)


def _position():
    return lax.axis_index("x"), lax.axis_index("y"), lax.axis_index("c")


def _other_chips(x, y):
    return [(1 - x, y), (x, 1 - y), (1 - x, 1 - y)]


def _row_chunks(rows):
    return rows // D2D_CHUNK_ROWS if rows % D2D_CHUNK_ROWS == 0 else 1


def _start_chunked(make, rows):
    k = _row_chunks(rows)
    for i in range(k):
        make(i * (rows // k), rows // k).start()


def gather_shards(packs):
    n = len(packs)

    def body(*refs):
        srcs, outs = refs[:n], refs[n:2 * n]
        send_sems, recv_sems = refs[2 * n:]
        x, y, c = _position()
        me = 2 * x + y
        chips = _other_chips(x, y)
        sibling = (x, y, 1 - c)

        def over_ici(a, j):
            px, py = chips[j]
            rows = srcs[a].shape[0] // 2
            return pltpu.make_async_remote_copy(
                src_ref=srcs[a].at[pl.ds(c * rows, rows), :], dst_ref=outs[a].at[me, pl.ds(c * rows, rows), :],
                send_sem=send_sems.at[a, j], recv_sem=recv_sems.at[a, j], device_id=(px, py, c), device_id_type=MESH)

        def landed(a, j):
            px, py = chips[j]
            rows = srcs[a].shape[0] // 2
            part = outs[a].at[2 * px + py, pl.ds(c * rows, rows), :]
            return pltpu.make_async_remote_copy(
                src_ref=part, dst_ref=part, send_sem=send_sems.at[a, j], recv_sem=recv_sems.at[a, j],
                device_id=(px, py, c), device_id_type=MESH)

        def over_d2d(a, j, cc, off, size):
            px, py = chips[j]
            rows = srcs[a].shape[0] // 2
            part = outs[a].at[2 * px + py, pl.ds(cc * rows + off, size), :]
            return pltpu.make_async_remote_copy(
                src_ref=part, dst_ref=part, send_sem=send_sems.at[a, 3 + j], recv_sem=recv_sems.at[a, 3 + j],
                device_id=sibling, device_id_type=MESH)

        for a in range(n):
            for j in range(3):
                over_ici(a, j).start()
        for a in range(n):
            for j in range(3):
                landed(a, j).wait_recv()
                _start_chunked(functools.partial(over_d2d, a, j, c), srcs[a].shape[0] // 2)
        for a in range(n):
            rows = srcs[a].shape[0] // 2
            for j in range(3):
                over_d2d(a, j, 1 - c, 0, rows).wait_recv()
                over_d2d(a, j, c, 0, rows).wait_send()
                over_ici(a, j).wait_send()

    return pl.pallas_call(
        body, name="gather_shards",
        in_specs=[HBM_SPEC] * n, out_specs=[HBM_SPEC] * n,
        out_shape=[jax.ShapeDtypeStruct((N_CHIPS,) + p.shape, p.dtype) for p in packs],
        scratch_shapes=[pltpu.SemaphoreType.DMA((n, 6)), pltpu.SemaphoreType.DMA((n, 6))],
    )(*packs)


def pair_split(name, packs):
    n = len(packs)

    def body(*refs):
        srcs, outs = refs[:n], refs[n:2 * n]
        send_sems, recv_sems = refs[2 * n:]
        x, y, c = _position()

        def remote(a, off, size):
            rows = srcs[a].shape[1] // 2
            return pltpu.make_async_remote_copy(
                src_ref=srcs[a].at[:, pl.ds((1 - c) * rows + off, size), :],
                dst_ref=outs[a].at[:, pl.ds(off, size), :],
                send_sem=send_sems.at[a], recv_sem=recv_sems.at[a], device_id=(x, y, 1 - c), device_id_type=MESH)

        for a in range(n):
            _start_chunked(functools.partial(remote, a), srcs[a].shape[1] // 2)
        for a in range(n):
            remote(a, 0, srcs[a].shape[1] // 2).wait()

    return pl.pallas_call(
        body, name=name, in_specs=[HBM_SPEC] * n, out_specs=[HBM_SPEC] * n,
        out_shape=[jax.ShapeDtypeStruct((p.shape[0], p.shape[1] // 2, p.shape[2]), p.dtype) for p in packs],
        scratch_shapes=[pltpu.SemaphoreType.DMA((n,)), pltpu.SemaphoreType.DMA((n,))],
    )(*packs)


def pair_join(name, halves):
    n = len(halves)

    def body(*refs):
        srcs, outs = refs[:n], refs[n:2 * n]
        send_sems, recv_sems = refs[2 * n:]
        x, y, c = _position()

        def remote(a, off, size):
            return pltpu.make_async_remote_copy(
                src_ref=srcs[a].at[pl.ds(off, size), :], dst_ref=outs[a].at[pl.ds(off, size), :],
                send_sem=send_sems.at[a], recv_sem=recv_sems.at[a], device_id=(x, y, 1 - c), device_id_type=MESH)

        for a in range(n):
            _start_chunked(functools.partial(remote, a), srcs[a].shape[0])
        for a in range(n):
            remote(a, 0, srcs[a].shape[0]).wait()

    return pl.pallas_call(
        body, name=name, in_specs=[HBM_SPEC] * n, out_specs=[HBM_SPEC] * n,
        out_shape=[jax.ShapeDtypeStruct(p.shape, p.dtype) for p in halves],
        scratch_shapes=[pltpu.SemaphoreType.DMA((n,)), pltpu.SemaphoreType.DMA((n,))],
    )(*halves)


SEM_SPEC = pl.BlockSpec(memory_space=pltpu.SEMAPHORE)
DATAFLOW = pltpu.SideEffectType.DATAFLOW_SIDE_EFFECTING


def _ici_copy(kind, srcs, lands, send_sems, recv_sems, a, j):
    x, y, c = _position()
    px, py = _other_chips(x, y)[j]
    if kind == "gather":
        rows = srcs[a].shape[0] // 2
        src = srcs[a].at[pl.ds(c * rows, rows), :]
        dst = lands[a].at[2 * x + y, pl.ds(c * rows, rows), :]
    else:
        src = srcs[a].at[2 * px + py]
        dst = lands[a].at[j]
    return pltpu.make_async_remote_copy(src_ref=src, dst_ref=dst, send_sem=send_sems, recv_sem=recv_sems,
                                        device_id=(px, py, c), device_id_type=MESH)


def ici_start(name, kind, srcs, land_shapes):
    n = len(srcs)
    lands = [pltpu.with_memory_space_constraint(lax.empty(shp, s.dtype), pltpu.HBM) for shp, s in zip(land_shapes, srcs)]

    def body(*refs):
        src_refs, land_refs = refs[:n], refs[n:2 * n]
        send_sems, recv_sems = refs[2 * n], refs[2 * n + 1]
        token = refs[-1]
        for a in range(n):
            for j in range(N_CHIPS - 1):
                _ici_copy(kind, src_refs, land_refs, send_sems, recv_sems, a, j).start()
        token[...] = jnp.zeros_like(token)

    sems = pltpu.SemaphoreType.DMA(())
    res = pl.pallas_call(
        body, name=name,
        out_shape=[sems, sems] + [pltpu.HBM(s.shape, s.dtype) for s in srcs]
        + [pltpu.HBM(l.shape, l.dtype) for l in lands] + [jax.ShapeDtypeStruct((8, LANES), F32)],
        in_specs=[HBM_SPEC] * (2 * n),
        out_specs=[SEM_SPEC, SEM_SPEC] + [HBM_SPEC] * (2 * n) + [pl.BlockSpec(memory_space=pltpu.VMEM)],
        input_output_aliases={i: 2 + i for i in range(2 * n)},
        compiler_params=pltpu.CompilerParams(has_side_effects=DATAFLOW),
    )(*[pltpu.with_memory_space_constraint(s, pltpu.HBM) for s in srcs], *lands)
    return res[0], res[1], list(res[2:2 + n]), list(res[2 + n:2 + 2 * n]), res[-1]


def ici_wait(name, kind, send_sems, recv_sems, srcs, lands, after):
    n = len(srcs)

    def body(*refs):
        src_refs, land_refs = refs[:n], refs[n:2 * n]
        send, recv = refs[2 * n], refs[2 * n + 1]
        for a in range(n):
            for j in range(N_CHIPS - 1):
                cp = _ici_copy(kind, src_refs, land_refs, send, recv, a, j)
                cp.wait_send()
                cp.wait_recv()

    res = pl.pallas_call(
        body, name=name,
        out_shape=[pltpu.HBM(s.shape, s.dtype) for s in srcs] + [pltpu.HBM(l.shape, l.dtype) for l in lands],
        in_specs=[HBM_SPEC] * (2 * n) + [SEM_SPEC, SEM_SPEC, pl.BlockSpec(memory_space=pl.ANY)],
        out_specs=[HBM_SPEC] * (2 * n),
        input_output_aliases={i: i for i in range(2 * n)},
        compiler_params=pltpu.CompilerParams(has_side_effects=DATAFLOW),
    )(*srcs, *lands, send_sems, recv_sems, after)
    return list(res[:n]), list(res[n:])


def pair_forward(gathered):
    n = len(gathered)

    def body(*refs):
        outs = refs[n:2 * n]
        send_sems, recv_sems = refs[2 * n:]
        x, y, c = _position()
        chips = _other_chips(x, y)

        def part(a, j, cc, off, size):
            px, py = chips[j]
            rows = outs[a].shape[1] // 2
            ref = outs[a].at[2 * px + py, pl.ds(cc * rows + off, size), :]
            return pltpu.make_async_remote_copy(
                src_ref=ref, dst_ref=ref, send_sem=send_sems.at[a, j], recv_sem=recv_sems.at[a, j],
                device_id=(x, y, 1 - c), device_id_type=MESH)

        for a in range(n):
            for j in range(N_CHIPS - 1):
                _start_chunked(functools.partial(part, a, j, c), outs[a].shape[1] // 2)
        for a in range(n):
            rows = outs[a].shape[1] // 2
            for j in range(N_CHIPS - 1):
                part(a, j, 1 - c, 0, rows).wait_recv()
                part(a, j, c, 0, rows).wait_send()

    return pl.pallas_call(
        body, name="pair_forward", in_specs=[HBM_SPEC] * n, out_specs=[HBM_SPEC] * n,
        out_shape=[jax.ShapeDtypeStruct(g.shape, g.dtype) for g in gathered],
        input_output_aliases={i: i for i in range(n)},
        scratch_shapes=[pltpu.SemaphoreType.DMA((n, N_CHIPS - 1)), pltpu.SemaphoreType.DMA((n, N_CHIPS - 1))],
    )(*gathered)


def all_sum_small(part):
    n_dev = 8
    rows = part.shape[0]

    def body(src, out, buf, send_sems, recv_sems):
        x, y, c = _position()
        me = 4 * x + 2 * y + c
        buf[me] = src[...]
        copies = []
        for k in range(1, n_dev):
            px, py, pc = x ^ ((k >> 2) & 1), y ^ ((k >> 1) & 1), c ^ (k & 1)
            cp = pltpu.make_async_remote_copy(
                src_ref=src, dst_ref=buf.at[me], send_sem=send_sems.at[k - 1], recv_sem=recv_sems.at[k - 1],
                device_id=(px, py, pc), device_id_type=MESH)
            cp.start()
            copies.append(cp)
        for cp in copies:
            cp.wait()
        acc = buf[0]
        for k in range(1, n_dev):
            acc = acc + buf[k]
        out[...] = acc

    return pl.pallas_call(
        body, name="all_sum_small",
        in_specs=[pl.BlockSpec(memory_space=pltpu.VMEM)], out_specs=pl.BlockSpec(memory_space=pltpu.VMEM),
        out_shape=jax.ShapeDtypeStruct(part.shape, F32),
        scratch_shapes=[pltpu.VMEM((n_dev, rows, part.shape[1]), F32),
                        pltpu.SemaphoreType.DMA((n_dev - 1,)), pltpu.SemaphoreType.DMA((n_dev - 1,))],
    )(part)


def add_pairs(name, src, theirs, core, out_dtype):
    slabs, rows, cols = theirs.shape
    tm = _pick_tile(rows, ROW_TILE)
    nb = rows // tm

    def body(core_ref, a_ref, b_ref, o_ref):
        o_ref[...] = (a_ref[...].astype(F32) + b_ref[...].astype(F32)).astype(out_dtype)

    return pl.pallas_call(
        body, name=name,
        grid_spec=pltpu.PrefetchScalarGridSpec(
            num_scalar_prefetch=1, grid=(slabs, nb),
            in_specs=[pl.BlockSpec((None, tm, cols), lambda s, i, core_ref: (s, core_ref[0] * nb + i, 0)),
                      pl.BlockSpec((None, tm, cols), lambda s, i, core_ref: (s, i, 0))],
            out_specs=pl.BlockSpec((None, tm, cols), lambda s, i, core_ref: (s, i, 0))),
        out_shape=jax.ShapeDtypeStruct(theirs.shape, out_dtype),
        compiler_params=_cparams(("parallel", "parallel")),
    )(core, src, theirs)


def add_four(name, src, theirs, chip):
    _, rows, cols = theirs.shape
    tm = _pick_tile(rows, ROW_TILE)

    def body(chip_ref, a_ref, b_ref, o_ref):
        acc = a_ref[...].astype(F32)
        for j in range(N_CHIPS - 1):
            acc = acc + b_ref[j].astype(F32)
        o_ref[...] = acc

    return pl.pallas_call(
        body, name=name,
        grid_spec=pltpu.PrefetchScalarGridSpec(
            num_scalar_prefetch=1, grid=(rows // tm,),
            in_specs=[pl.BlockSpec((None, tm, cols), lambda i, chip_ref: (chip_ref[0], i, 0)),
                      pl.BlockSpec((N_CHIPS - 1, tm, cols), lambda i, chip_ref: (0, i, 0))],
            out_specs=pl.BlockSpec((tm, cols), lambda i, chip_ref: (i, 0))),
        out_shape=jax.ShapeDtypeStruct((rows, cols), F32),
        compiler_params=_cparams(("parallel",)),
    )(chip, src, theirs)


PACK_COLS = 1024
SMALL_ROW_MULTIPLE = 32
BIG = ["dn_w_in", "dn_w_out", "cv_w_pw1", "cv_w_pw2", "xa_w_q", "xa_w_kv", "xa_w_o", "mlp_w_up", "mlp_w_down"]
SMALL = ["dn_w_conv", "cv_norm", "cv_b_pw1", "cv_w_dw", "cv_b_dw", "cv_ln_g", "cv_ln_b", "cv_b_pw2"]
SHARD_AXIS = {"dn_w_in": 2, "dn_w_conv": 2, "dn_w_out": 1, "cv_norm": 1, "cv_w_pw1": 2, "cv_b_pw1": 1,
              "cv_w_dw": 2, "cv_b_dw": 1, "cv_ln_g": 1, "cv_ln_b": 1, "cv_w_pw2": 1, "cv_b_pw2": 1,
              "xa_w_q": 1, "xa_w_kv": 2, "xa_w_o": 1, "mlp_w_up": 2, "mlp_w_down": 1}
REPLICATED = ["dn_norm", "dn_a_log", "dn_dt_bias", "dn_out_norm", "xa_norm", "xa_mem_norm", "mlp_norm", "final_norm"]


def _pack_rows(size):
    return -(-size // PACK_COLS)


SHARD_SHAPES = {
    "dn_w_in": (1, 1024, 1028), "dn_w_conv": (1, 4, 768), "dn_w_out": (1, 256, 1024), "cv_norm": (1, 256),
    "cv_w_pw1": (1, 1024, 512), "cv_b_pw1": (1, 512), "cv_w_dw": (1, 31, 256), "cv_b_dw": (1, 256),
    "cv_ln_g": (1, 256), "cv_ln_b": (1, 256), "cv_w_pw2": (1, 256, 1024), "cv_b_pw2": (1, 256),
    "xa_w_q": (2, 256, 1024), "xa_w_kv": (2, 1024, 512), "xa_w_o": (2, 256, 1024),
    "mlp_w_up": (2, 1024, 1024), "mlp_w_down": (2, 1024, 1024)}


def _shard_shape(nm):
    return SHARD_SHAPES[nm]


def _pack(tensors, names, dtype, row_multiple):
    pieces = []
    for nm in names:
        t = tensors[nm]
        flat = t.reshape(t.shape[0], -1) if t.ndim > len(_shard_shape(nm)) else t.reshape(1, -1)
        pad = _pack_rows(flat.shape[1]) * PACK_COLS - flat.shape[1]
        pieces.append(jnp.pad(flat.astype(dtype), ((0, 0), (0, pad))))
    cat = jnp.concatenate(pieces, axis=1)
    rows = cat.shape[1] // PACK_COLS
    total = -(-rows // row_multiple) * row_multiple
    cat = jnp.pad(cat, ((0, 0), (0, (total - rows) * PACK_COLS)))
    return cat.reshape(cat.shape[0], total, PACK_COLS)


def _unpack(pack, names):
    lead = pack.shape[:-2]
    flat = pack.reshape(lead + (-1,))
    out, off = {}, 0
    for nm in names:
        shp = _shard_shape(nm)
        size = 1
        for s in shp:
            size *= s
        out[nm] = flat[..., off:off + size].reshape(lead + shp)
        off += _pack_rows(size) * PACK_COLS
    return out


def _to_full(nm, stacked):
    ax = SHARD_AXIS[nm]
    moved = jnp.moveaxis(stacked, 0, ax)
    shp = list(_shard_shape(nm))
    shp[ax] *= N_CHIPS
    return moved.reshape(shp)


def _to_shards(nm, full):
    ax = SHARD_AXIS[nm]
    shp = list(_shard_shape(nm))
    split = full.reshape(shp[:ax] + [N_CHIPS, shp[ax]] + shp[ax + 1:])
    return jnp.moveaxis(split, ax, 0)


def _row(v):
    return v.reshape(1, -1)


class Stacked:
    def __init__(self, arr, split, layer):
        self.arr, self.kw = arr, dict(b_split=split, b_layer=layer)


def _grad_out(split):
    return dict(out_dtype=BF16, out_split=split, out_layer=(0, 1))


def _with_next(res, next_gain):
    return (res[0], res[1]) if next_gain is not None else (res, None)


def mlp_fwd(tag, h, g, w_up, w_down, n=None, next_gain=None):
    if n is None:
        n = rms_fwd(tag + "_norm", h, g)
    act = mm(tag + "_up", n, w_up.arr, out_dtype=BF16, epi=lambda acc: jnp.square(jnp.maximum(acc, 0.0)), **w_up.kw)
    out, n_next = _with_next(mm(tag + "_down", act, w_down.arr, tk=LONG_TILE_K, epi=lambda acc, res: acc + res,
                                epi_tiles=(h,), norm_gain=next_gain, **w_down.kw), next_gain)
    return out, n_next, (n, act)


def mlp_bwd(tag, dh, h, g, w_up, w_down, saved, after=None):
    n, act = saved
    dh, dh16 = dh
    dup = mm(tag + "_d_act", dh16, w_down.arr, tb=True, out_dtype=BF16, after=after,
             epi=lambda acc, t: acc * (2.0 * jnp.sqrt(t.astype(F32))), epi_tiles=(act,), **w_down.kw)
    dw_down = mm(tag + "_dw_down", act, dh16, ta=True, tk=LONG_TILE_K, **_grad_out("rows"))
    dn = mm(tag + "_dn", dup, w_up.arr, tb=True, **w_up.kw)
    dw_up = mm(tag + "_dw_up", n, dup, ta=True, tk=LONG_TILE_K, **_grad_out("cols"))
    dh_in, dh16_in, dg = rms_bwd(tag + "_norm_bwd", dn, h, g, dh)
    return (dh_in, dh16_in), dg, (dw_up, dw_down)


def xa_fwd(tag, h, mem, g, g_mem, w_q, w_kv, w_o, n=None, next_gain=None):
    if n is None:
        n = rms_fwd(tag + "_norm", h, g)
    mem_n = rms_fwd(tag + "_mem_norm", mem, g_mem)
    q = mm(tag + "_q", n, w_q.arr, out_dtype=BF16, **w_q.kw)
    kv = mm(tag + "_kv", mem_n, w_kv.arr, out_dtype=BF16, **w_kv.kw)
    o = xa_core_fwd(tag + "_core", q, kv)
    out, n_next = _with_next(mm(tag + "_o", o, w_o.arr, epi=lambda acc, res: acc + res, epi_tiles=(h,),
                                norm_gain=next_gain, **w_o.kw), next_gain)
    return out, n_next, (n, mem_n, q, kv, o)


def xa_bwd(tag, dh, h, mem, g, g_mem, w_q, w_kv, w_o, saved):
    n, mem_n, q, kv, o = saved
    dh, dh16 = dh
    d_o = mm(tag + "_d_o", dh16, w_o.arr, tb=True, out_dtype=BF16, **w_o.kw)
    dw_o = mm(tag + "_dw_o", o, dh16, ta=True, tk=LONG_TILE_K, **_grad_out("rows"))
    dq, dkv = xa_core_bwd(tag + "_core_bwd", d_o, q, kv)
    dn = mm(tag + "_dn", dq, w_q.arr, tb=True, **w_q.kw)
    dw_q = mm(tag + "_dw_q", n, dq, ta=True, tk=LONG_TILE_K, **_grad_out("rows"))
    dh_in, dh16_in, dg = rms_bwd(tag + "_norm_bwd", dn, h, g, dh)
    dw_kv = mm(tag + "_dw_kv", mem_n, dkv, ta=True, **_grad_out("cols"))
    dmem_n = mm(tag + "_dmem", dkv, w_kv.arr, tb=True, **w_kv.kw)
    dg_mem = mem_norm_bwd(tag + "_mem_norm_bwd", dmem_n, mem, g_mem)
    return (dh_in, dh16_in), dg, dg_mem, (dw_q, dw_kv, dw_o)


def _gate_tile(a_log, dt_bias):
    t = jnp.zeros((8, LANES), F32)
    t = t.at[0, DN_HEADS:2 * DN_HEADS].set(a_log.reshape(-1))
    return t.at[1, DN_HEADS:2 * DN_HEADS].set(dt_bias.reshape(-1))


def dn_fwd(h, g, w_qkv, w_z, w_ba, w_conv, gate, out_norm, w_out, next_gain=None):
    n = rms_fwd("dn_norm", h, g)
    qkv_raw = mm("dn_proj_qkv", n, w_qkv)
    z = mm("dn_proj_z", n, w_z)
    ba = mm("dn_proj_ba", n, w_ba)
    qkv, hs = dn_pre(qkv_raw, ba, w_conv, gate)
    u, w, t_inv = dn_solve(qkv, hs)
    o, states = dn_scan_fwd(qkv, u, w, hs)
    og = dn_post(o, z, out_norm)
    out, n_next = _with_next(mm("dn_out", og, w_out.arr, epi=lambda acc, res: acc + res, epi_tiles=(h,),
                                norm_gain=next_gain, **w_out.kw), next_gain)
    return out, n_next, (n, qkv_raw, z, ba, qkv, hs, u, w, t_inv, o, states, og)


def dn_bwd(dh, h, g, w_qkv, w_z, w_ba, w_conv, gate, out_norm, w_out, saved, after=None):
    n, qkv_raw, z, ba, qkv, hs, u, w, t_inv, o, states, og = saved
    dh, dh16 = dh
    d_og = mm("dn_d_og", dh16, w_out.arr, tb=True, out_dtype=BF16, after=after, **w_out.kw)
    dw_out = mm("dn_dw_out", og, dh16, ta=True, tk=LONG_TILE_K, **_grad_out("rows"))
    d_o, dz, d_out_norm = dn_post_bwd(d_og, o, z, out_norm)
    dq, dk, dv, dhs = dn_scan_bwd(qkv, u, w, t_inv, hs, states, d_o)
    dc, dba, d_gate = dn_pre_bwd(dq, dk, dv, dhs, qkv_raw, ba, w_conv, gate)
    dqkv_raw, dw_conv = dn_conv_bwd(dc, qkv_raw, w_conv)
    dn = mm("dn_dn_qkv", dqkv_raw, w_qkv, tb=True)
    dn = mm("dn_dn_z", dz, w_z, tb=True, epi=lambda acc, t: acc + t, epi_tiles=(dn,))
    dn = mm("dn_dn_ba", dba, w_ba, tb=True, epi=lambda acc, t: acc + t, epi_tiles=(dn,))
    dw_qkv = mm("dn_dw_qkv", n, dqkv_raw, ta=True, tk=LONG_TILE_K)
    dw_z = mm("dn_dw_z", n, dz, ta=True, tk=LONG_TILE_K)
    dw_ba = mm("dn_dw_ba", n, dba, ta=True, tk=LONG_TILE_K)
    dh_in, _, dg = rms_bwd("dn_norm_bwd", dn, h, g, dh)
    return dh_in, dg, dw_qkv, dw_z, dw_ba, dw_conv, d_gate, d_out_norm, dw_out


def cv_fwd(h, g, w_pw1, b_pw1, w_dw, b_dw, ln_g, ln_b, w_pw2, b_pw2, n=None, next_gain=None):
    if n is None:
        n = rms_fwd("cv_norm", h, g)
    u = mm("cv_pw1", n, w_pw1.arr, epi=lambda acc, b: acc + b, epi_rows=(b_pw1,), **w_pw1.kw)
    s, c = cv_core_fwd(u, w_dw, b_dw, ln_g, ln_b)
    out, n_next = _with_next(mm("cv_pw2", s, w_pw2.arr, epi=lambda acc, res, b: acc + res + b, epi_tiles=(h,),
                                epi_rows=(b_pw2,), norm_gain=next_gain, **w_pw2.kw), next_gain)
    return out, n_next, (n, u, s, c)


def cv_bwd(dh, h, g, w_pw1, w_dw, ln_g, ln_b, w_pw2, saved):
    n, u, s, c = saved
    dh, dh16 = dh
    ds = mm("cv_d_s", dh16, w_pw2.arr, tb=True, out_dtype=BF16, **w_pw2.kw)
    dw_pw2 = mm("cv_dw_pw2", s, dh16, ta=True, tk=LONG_TILE_K, **_grad_out("rows"))
    db_pw2 = col_sum("cv_db_pw2", dh)
    dc, ln_acc = cv_ln_bwd(ds, c, ln_g, ln_b)
    du, dw_dw, db_pw1 = cv_conv_bwd(dc, u, w_dw)
    dn = mm("cv_dn", du, w_pw1.arr, tb=True, **w_pw1.kw)
    dw_pw1 = mm("cv_dw_pw1", n, du, ta=True, tk=LONG_TILE_K, **_grad_out("cols"))
    dh_in, dh16_in, dg = rms_bwd("cv_norm_bwd", dn, h, g, dh)
    return (dh_in, dh16_in), dg, dw_pw1, db_pw1, dw_dw, ln_acc, dw_pw2, db_pw2


WEIGHTS = ["dn_norm", "dn_w_in", "dn_w_conv", "dn_a_log", "dn_dt_bias", "dn_out_norm", "dn_w_out", "cv_norm",
           "cv_w_pw1", "cv_b_pw1", "cv_w_dw", "cv_b_dw", "cv_ln_g", "cv_ln_b", "cv_w_pw2", "cv_b_pw2", "xa_norm",
           "xa_mem_norm", "xa_w_q", "xa_w_kv", "xa_w_o", "mlp_norm", "mlp_w_up", "mlp_w_down", "final_norm"]


def _as_2d(t):
    if t.ndim == 1:
        return t.reshape(1, -1)
    return t.reshape(-1, t.shape[-1])


def kernel(x, mem, dn_norm, dn_w_in, dn_w_conv, dn_a_log, dn_dt_bias, dn_out_norm, dn_w_out, cv_norm, cv_w_pw1, cv_b_pw1, cv_w_dw, cv_b_dw, cv_ln_g, cv_ln_b, cv_w_pw2, cv_b_pw2, xa_norm, xa_mem_norm, xa_w_q, xa_w_kv, xa_w_o, mlp_norm, mlp_w_up, mlp_w_down, final_norm, loss_target, m_dn_norm, m_dn_w_in, m_dn_w_conv, m_dn_a_log, m_dn_dt_bias, m_dn_out_norm, m_dn_w_out, m_cv_norm, m_cv_w_pw1, m_cv_b_pw1, m_cv_w_dw, m_cv_b_dw, m_cv_ln_g, m_cv_ln_b, m_cv_w_pw2, m_cv_b_pw2, m_xa_norm, m_xa_mem_norm, m_xa_w_q, m_xa_w_kv, m_xa_w_o, m_mlp_norm, m_mlp_w_up, m_mlp_w_down, m_final_norm, v_dn_norm, v_dn_w_in, v_dn_w_conv, v_dn_a_log, v_dn_dt_bias, v_dn_out_norm, v_dn_w_out, v_cv_norm, v_cv_w_pw1, v_cv_b_pw1, v_cv_w_dw, v_cv_b_dw, v_cv_ln_g, v_cv_ln_b, v_cv_w_pw2, v_cv_b_pw2, v_xa_norm, v_xa_mem_norm, v_xa_w_q, v_xa_w_kv, v_xa_w_o, v_mlp_norm, v_mlp_w_up, v_mlp_w_down, v_final_norm):
    args = dict(locals())
    wts = {nm: args[nm] for nm in WEIGHTS}
    mom = {nm: args["m_" + nm] for nm in WEIGHTS}
    var = {nm: args["v_" + nm] for nm in WEIGHTS}
    core = lax.axis_index("c").astype(jnp.int32).reshape(1)
    chip = (2 * lax.axis_index("x") + lax.axis_index("y")).astype(jnp.int32)
    def own_slab(got, src):
        return lax.dynamic_update_slice(got, src[None], (chip, 0, 0))

    shard2d = {nm: wts[nm].astype(BF16).reshape(-1, wts[nm].shape[-1]) for nm in BIG}
    first = ["dn_w_in", "dn_w_out"]
    later = [nm for nm in BIG if nm not in first]
    sources = [shard2d[nm] for nm in first] + [_pack(wts, SMALL, F32, SMALL_ROW_MULTIPLE)[0]]
    gathered = [own_slab(got, src) for got, src in zip(gather_shards(sources), sources)]
    stacked = {"dn_w_out": gathered[1].reshape((N_CHIPS,) + SHARD_SHAPES["dn_w_out"])}
    full = {nm: _to_full(nm, t) for nm, t in _unpack(gathered[2], SMALL).items()}
    full["dn_w_in"] = _to_full("dn_w_in", gathered[0].reshape((N_CHIPS,) + SHARD_SHAPES["dn_w_in"]))
    full.update({nm: wts[nm] for nm in REPLICATED})
    later_src = [shard2d[nm] for nm in later]
    g_send, g_recv, later_src, g_lands, started = ici_start(
        "gather_start", "gather", later_src, [(N_CHIPS,) + s.shape for s in later_src])
    full["dn_norm"] = full["dn_norm"] + started[0, 0]

    def rest_weights(after):
        srcs, lands = ici_wait("gather_wait", "gather", g_send, g_recv, later_src, g_lands, after)
        return {nm: own_slab(land, src).reshape((N_CHIPS,) + SHARD_SHAPES[nm])
                for nm, land, src in zip(later, pair_forward(lands), srcs)}

    pending = []

    def on_grads(items):
        tag = "_".join(sorted({str(layer) for _, layer, _ in items}))
        parts = [g.reshape(N_CHIPS, -1, g.shape[-1]) for _, _, g in items]
        theirs = pair_split("pair_split_" + tag, parts)
        pairs = [add_pairs("pair_add_%s%d" % (nm, layer), p, t, core, BF16)
                 for (nm, layer, _), p, t in zip(items, parts, theirs)]
        send, recv, pairs, lands, token = ici_start(
            "scatter_start_" + tag, "scatter", pairs, [(N_CHIPS - 1,) + p.shape[1:] for p in pairs])
        pending.append((tag, items, send, recv, pairs, lands))
        return token

    dh, grads, rep = local_step(x[0], mem[0], loss_target[0], stacked, full, rest_weights, on_grads)

    halves = {}
    last = [("dn_w_in", 0, _to_shards("dn_w_in", grads["dn_w_in"]).astype(BF16)), ("dn_w_out", 0, grads["dn_w_out"][0]),
            ("small", 0, _pack({nm: _to_shards(nm, grads[nm]) for nm in SMALL}, SMALL, F32, SMALL_ROW_MULTIPLE))]
    parts = [g.reshape(N_CHIPS, -1, g.shape[-1]) for _, _, g in last]
    theirs = pair_split("pair_split_last", parts)
    pairs = [add_pairs("pair_add_" + nm, p, t, core, p.dtype) for (nm, _, _), p, t in zip(last, parts, theirs)]
    l_send, l_recv, l_pairs, l_lands, l_started = ici_start(
        "scatter_start_last", "scatter", pairs, [(N_CHIPS - 1,) + p.shape[1:] for p in pairs])
    for tag, items, send, recv, pairs, lands in pending:
        pairs, lands = ici_wait("scatter_wait_" + tag, "scatter", send, recv, pairs, lands, l_started)
        for (nm, layer, _), p, o in zip(items, pairs, lands):
            halves[nm, layer] = add_four("chip_add_%s%d" % (nm, layer), p, o, chip.reshape(1))
    keys = sorted(halves)
    siblings = dict(zip(keys, pair_join("pair_join_early", [halves[k] for k in keys])))

    delta, new_m, new_v, red = {}, {}, {}, {}

    def big_adamw(nm):
        layers = range(wts[nm].shape[0])
        res = adamw_halves("adamw_" + nm, _as_2d(wts[nm]), [halves[nm, l] for l in layers],
                           [siblings[nm, l] for l in layers], _as_2d(mom[nm]), _as_2d(var[nm]), core)
        red[nm], delta[nm], new_m[nm], new_v[nm] = (r.reshape(wts[nm].shape) for r in res)

    early = [nm for nm in BIG if (nm, 0) in halves]
    for nm in early:
        big_adamw(nm)
    done = jnp.concatenate([new_v[nm].reshape(-1)[:1] for nm in early])
    l_pairs, l_lands = ici_wait("scatter_wait_last", "scatter", l_send, l_recv, l_pairs, l_lands, done)
    for (nm, layer, _), p, o in zip(last, l_pairs, l_lands):
        halves[nm, layer] = add_four("chip_add_" + nm, p, o, chip.reshape(1))
    keys = [(nm, layer) for nm, layer, _ in last]
    siblings.update(zip(keys, pair_join("pair_join_last", [halves[k] for k in keys])))
    south = core[0] == 0
    mine, theirs = halves["small", 0], siblings["small", 0]
    red.update(_unpack(jnp.concatenate([jnp.where(south, mine, theirs), jnp.where(south, theirs, mine)], axis=0),
                       SMALL))

    rep = all_sum_small(rep)
    red["dn_norm"] = rep[0:1]
    red["dn_a_log"] = rep[1:2, DN_HEADS:2 * DN_HEADS]
    red["dn_dt_bias"] = rep[2:3, DN_HEADS:2 * DN_HEADS]
    red["dn_out_norm"] = rep[3:4, :LANES]
    red["xa_norm"], red["xa_mem_norm"], red["mlp_norm"] = rep[4:6], rep[6:8], rep[8:10]
    red["final_norm"] = rep[10]
    loss = rep[11, 0]

    for nm in WEIGHTS:
        shp = wts[nm].shape
        if nm in early:
            continue
        if nm in BIG:
            big_adamw(nm)
            continue
        res = adamw("adamw_" + nm, _as_2d(wts[nm]), _as_2d(red[nm].reshape(shp)), _as_2d(mom[nm]), _as_2d(var[nm]))
        delta[nm], new_m[nm], new_v[nm] = (r.reshape(shp) for r in res)
        red[nm] = red[nm].reshape(shp)

    grad_x = dh[None]
    return (loss, grad_x, *[red[nm] for nm in WEIGHTS], *[delta[nm] for nm in WEIGHTS],
            *[new_m[nm] for nm in WEIGHTS], *[new_v[nm] for nm in WEIGHTS])


def local_step(h0, mem0, target, stacked, full, rest_weights=None, on_grads=None):
    d = h0.shape[1]
    dn_norm, dn_a_log, dn_dt_bias, dn_out_norm = (full[nm] for nm in REPLICATED[:4])
    xa_norm, xa_mem_norm, mlp_norm, final_norm = (full[nm] for nm in REPLICATED[4:])
    inner = DN_HEADS * DN_HEAD_DIM
    w_in = full["dn_w_in"][0]
    w_qkv, w_z = w_in[:, :3 * inner], w_in[:, 3 * inner:4 * inner]
    w_ba = jnp.pad(w_in[:, 4 * inner:], ((0, 0), (0, LANES - 2 * DN_HEADS)))
    w_conv = jnp.pad(full["dn_w_conv"][0], ((0, 8 - DN_CONV), (0, 0)))
    gate = _gate_tile(dn_a_log, dn_dt_bias)
    w_dw = jnp.pad(full["cv_w_dw"][0], ((0, CV_HALO - CV_WIDTH), (0, 0)))

    def sw(nm, layer):
        return Stacked(stacked[nm], "rows" if SHARD_AXIS[nm] == 1 else "cols", layer)

    dn_args = (_row(dn_norm), w_qkv, w_z, w_ba, w_conv, gate, _row(dn_out_norm), sw("dn_w_out", 0))
    h1, n, dn_saved = dn_fwd(h0, *dn_args, next_gain=_row(xa_norm[0]))
    if rest_weights is not None:
        stacked = {**stacked, **rest_weights(h1)}
    xa_args = [(_row(xa_norm[l]), _row(xa_mem_norm[l]), sw("xa_w_q", l), sw("xa_w_kv", l), sw("xa_w_o", l))
               for l in range(2)]
    mlp_args = [(_row(mlp_norm[l]), sw("mlp_w_up", l), sw("mlp_w_down", l)) for l in range(2)]
    cv_args = (_row(full["cv_norm"][0]), sw("cv_w_pw1", 0), full["cv_b_pw1"], w_dw, full["cv_b_dw"],
               full["cv_ln_g"], full["cv_ln_b"], sw("cv_w_pw2", 0), full["cv_b_pw2"])
    h2, n, xa0_saved = xa_fwd("xa0", h1, mem0, *xa_args[0], n=n, next_gain=mlp_args[0][0])
    h3, n, mlp0_saved = mlp_fwd("mlp0", h2, *mlp_args[0], n=n, next_gain=cv_args[0])
    h4, n, cv_saved = cv_fwd(h3, *cv_args, n=n, next_gain=xa_args[1][0])
    h5, n, xa1_saved = xa_fwd("xa1", h4, mem0, *xa_args[1], n=n, next_gain=mlp_args[1][0])
    h6, _, mlp1_saved = mlp_fwd("mlp1", h5, *mlp_args[1], n=n)

    dh32, dh16, loss_tile, d_final = loss_head("loss_head", h6, _row(final_norm), target)
    dh = (dh32, dh16)
    grads = {}
    dg_mlp, dg_xa, dg_xa_mem = [None, None], [None, None], [None, None]
    dw_mlp, dw_xa = [None, None], [None, None]
    mlp_names, xa_names = ("mlp_w_up", "mlp_w_down"), ("xa_w_q", "xa_w_kv", "xa_w_o")

    def announce(items):
        return None if on_grads is None else on_grads(items)

    dh, dg_mlp[1], dw_mlp[1] = mlp_bwd("mlp1", dh, h5, *mlp_args[1], mlp1_saved)
    dh, dg_xa[1], dg_xa_mem[1], dw_xa[1] = xa_bwd("xa1", dh, h4, mem0, *xa_args[1], xa1_saved)
    (dh, grads["cv_norm"], dw_pw1, grads["cv_b_pw1"], dw_dw, ln_acc, dw_pw2,
     grads["cv_b_pw2"]) = cv_bwd(dh, h3, cv_args[0], cv_args[1], w_dw, cv_args[5], cv_args[6], cv_args[7], cv_saved)
    after = announce([(nm, 1, g) for nm, g in zip(mlp_names + xa_names, dw_mlp[1] + dw_xa[1])]
                     + [("cv_w_pw1", 0, dw_pw1), ("cv_w_pw2", 0, dw_pw2)])
    dh, dg_mlp[0], dw_mlp[0] = mlp_bwd("mlp0", dh, h2, *mlp_args[0], mlp0_saved, after=after)
    dh, dg_xa[0], dg_xa_mem[0], dw_xa[0] = xa_bwd("xa0", dh, h1, mem0, *xa_args[0], xa0_saved)
    after = announce([(nm, 0, g) for nm, g in zip(mlp_names + xa_names, dw_mlp[0] + dw_xa[0])])
    dh, dg_dn, dw_qkv, dw_z, dw_ba, dw_conv, d_gate, d_out_norm, dw_out = dn_bwd(dh, h0, *dn_args, dn_saved,
                                                                                 after=after)

    grads["dn_w_in"] = jnp.concatenate([dw_qkv, dw_z, dw_ba[:, :2 * DN_HEADS]], axis=1)[None]
    grads["dn_w_conv"] = dw_conv[None, :DN_CONV]
    grads["dn_w_out"], grads["cv_w_pw1"], grads["cv_w_pw2"] = [dw_out], [dw_pw1], [dw_pw2]
    grads["cv_w_dw"] = dw_dw[None, :CV_WIDTH]
    grads["cv_ln_g"], grads["cv_ln_b"], grads["cv_b_dw"] = ln_acc[0:1], ln_acc[1:2], ln_acc[2:3]
    for i, nm in enumerate(mlp_names):
        grads[nm] = [dw_mlp[0][i], dw_mlp[1][i]]
    for i, nm in enumerate(xa_names):
        grads[nm] = [dw_xa[0][i], dw_xa[1][i]]

    rep = jnp.zeros((16, d), F32)
    rep = rep.at[0].set(dg_dn[0])
    rep = rep.at[1, :LANES].set(d_gate[0])
    rep = rep.at[2, :LANES].set(d_gate[1])
    rep = rep.at[3, :LANES].set(d_out_norm[0])
    rep = rep.at[4].set(dg_xa[0][0]).at[5].set(dg_xa[1][0])
    rep = rep.at[6].set(dg_xa_mem[0][0]).at[7].set(dg_xa_mem[1][0])
    rep = rep.at[8].set(dg_mlp[0][0]).at[9].set(dg_mlp[1][0])
    rep = rep.at[10].set(d_final[0])
    rep = rep.at[11, :LANES].set(loss_tile[0])
    return dh, grads, rep
```

```python
import functools

import jax
import jax.numpy as jnp
from jax import lax
from jax.experimental import pallas as pl
from jax.experimental.pallas import tpu as pltpu

F32 = jnp.float32
BF16 = jnp.bfloat16
HIGHEST = lax.Precision.HIGHEST
MESH = pl.DeviceIdType.MESH

D_MODEL = 1024
DN_HEADS = 8
DN_HEAD_DIM = 128
DN_CONV = 4
DN_CHUNK = 64
CV_WIDTH = 31
XA_HEADS = 4
XA_HEAD_DIM = 256
RMS_EPS = 1e-6
LN_EPS = 1e-5
L2_EPS = 1e-6

ADAM_LR = 0.001
ADAM_B1 = 0.9
ADAM_B2 = 0.999
ADAM_EPS = 1e-08
ADAM_WD = 0.01
ADAM_STEP = 10

LANES = 128
ROW_TILE = 512
CONV_ROW_TILE = 256
MM_TILE = 1024
N_STRIPS = 1
LONG_TILE_K = 2048
ADAMW_ROW_TILE = 256
DN_ROW_TILE = 256
CHUNK_SHIFT = 6
SOLVE_INTERLEAVE = 8
FWD_HEADS_PER_STEP = 8
BWD_HEADS_PER_STEP = 8
BWD_SCAN_ROWS = 256
DN_HALO = 8
CV_HALO = 32
VMEM_LIMIT = 48 * 1024 * 1024
N_CHIPS = 4
D2D_CHUNK_ROWS = 256


def _cparams(sem):
    return pltpu.CompilerParams(dimension_semantics=sem, vmem_limit_bytes=VMEM_LIMIT)


def _dot(a, b, dims=(((1,), (0,)), ((), ()))):
    return lax.dot_general(a.astype(BF16), b.astype(BF16), dims, preferred_element_type=F32)


def _dot_nt(a, b):
    return _dot(a, b, (((1,), (1,)), ((), ())))


def _dot_tn(a, b):
    return _dot(a, b, (((0,), (0,)), ((), ())))


def _dot_hi(a, b, dims=(((1,), (0,)), ((), ()))):
    return lax.dot_general(a.astype(F32), b.astype(F32), dims, precision=HIGHEST, preferred_element_type=F32)


def _dot_x3(a, b, dims=(((1,), (0,)), ((), ()))):
    a_hi, b_hi = a.astype(BF16), b.astype(BF16)
    a_lo = (a - a_hi.astype(F32)).astype(BF16)
    b_lo = (b - b_hi.astype(F32)).astype(BF16)

    def dot(p, q):
        return lax.dot_general(p, q, dims, preferred_element_type=F32)

    return dot(a_hi, b_hi) + (dot(a_hi, b_lo) + dot(a_lo, b_hi))


def _sigmoid(x):
    return 1.0 / (1.0 + jnp.exp(-x))


def _silu(x):
    return x * _sigmoid(x)


def _silu_grad(x):
    s = _sigmoid(x)
    return s * (1.0 + x * (1.0 - s))


def _softplus(x):
    return jnp.maximum(x, 0.0) + jnp.log(1.0 + jnp.exp(-jnp.abs(x)))


def _iota(shape, dim):
    return lax.broadcasted_iota(jnp.int32, shape, dim)


def _lane_col(vals, lane, idx):
    return jnp.sum(jnp.where(lane == idx, vals, 0.0), axis=1, keepdims=True)


def _pick_tile(rows, cap):
    best = rows
    for t in range(16, min(rows, cap) + 1, 16):
        if rows % t == 0:
            best = t
    return best


def _stacked_spec(shape, split, layer, rows, cols, block_index):
    r_shard, c_shard = shape[-2], shape[-1]
    if split == "rows" and rows > r_shard:
        assert rows % r_shard == 0 and c_shard % cols == 0
        chips = rows // r_shard

        def slabs(i, j, kk):
            bi, bj = block_index(i, j, kk)
            return (bi, layer, 0, bj)

        return pl.BlockSpec((chips, None, r_shard, cols), slabs), chips
    assert r_shard % rows == 0 and c_shard % cols == 0
    per_chip = (r_shard // rows) if split == "rows" else (c_shard // cols)

    def index(i, j, kk):
        bi, bj = block_index(i, j, kk)
        if split == "rows":
            return (bi // per_chip, layer, bi % per_chip, bj)
        return (bj // per_chip, layer, bi, bj % per_chip)

    return pl.BlockSpec((None, None, rows, cols), index), 1


def mm(name, a, b, *, ta=False, tb=False, out_dtype=F32, pro=None, epi=None, epi_tiles=(), epi_rows=(),
       tm=MM_TILE, tn=MM_TILE, tk=MM_TILE, b_split=None, b_layer=None, out_split=None, out_layer=None,
       after=None, norm_gain=None):
    m, k = (a.shape[1], a.shape[0]) if ta else a.shape
    b_rows, b_cols = b.shape[-2], b.shape[-1]
    if b_split == "rows":
        b_rows *= N_CHIPS
    elif b_split == "cols":
        b_cols *= N_CHIPS
    n = b_rows if tb else b_cols
    assert (b_cols if tb else b_rows) == k
    tm, tn, tk = min(tm, m), min(tn, n), min(tk, k)
    if b_split == "cols":
        if tb:
            tk = min(tk, b.shape[-1])
        else:
            tn = min(tn, b.shape[-1])
    if out_split == "cols":
        tn = min(tn, n // N_CHIPS)
    assert m % tm == 0 and n % tn == 0 and k % tk == 0
    nk = k // tk
    a_spec = pl.BlockSpec((tk, tm), lambda i, j, kk: (kk, i)) if ta else pl.BlockSpec((tm, tk), lambda i, j, kk: (i, kk))
    b_block = (tn, tk) if tb else (tk, tn)
    b_index = (lambda i, j, kk: (j, kk)) if tb else (lambda i, j, kk: (kk, j))
    b_chips = o_chips = 1
    if b_split is None:
        b_spec = pl.BlockSpec(b_block, b_index)
    else:
        b_spec, b_chips = _stacked_spec(b.shape, b_split, b_layer, b_block[0], b_block[1], b_index)
    in_specs = [a_spec, b_spec]
    in_specs += [pl.BlockSpec((tm, tn), lambda i, j, kk: (i, j)) for _ in epi_tiles]
    in_specs += [pl.BlockSpec((1, tn), lambda i, j, kk: (0, j)) for _ in epi_rows]
    n_t, n_r = len(epi_tiles), len(epi_rows)
    dims = (((0 if ta else 1,), (1 if tb else 0,)), ((), ()))
    if out_split is None:
        out_shape = jax.ShapeDtypeStruct((m, n), out_dtype)
        out_spec = pl.BlockSpec((tm, tn), lambda i, j, kk: (i, j))
    else:
        shard = (m // N_CHIPS, n) if out_split == "rows" else (m, n // N_CHIPS)
        out_shape = jax.ShapeDtypeStruct((N_CHIPS, out_layer[1]) + shard, out_dtype)
        out_spec, o_chips = _stacked_spec(out_shape.shape, out_split, out_layer[0], tm, tn, lambda i, j, kk: (i, j))
    single_pass = (nk == 1 and norm_gain is None and b_chips == 1 and o_chips == 1
                   and tn % (N_STRIPS * LANES) == 0)
    extra = []
    if norm_gain is not None:
        assert tn == n and out_split is None
        extra.append(norm_gain)
        in_specs.append(pl.BlockSpec((1, n), lambda i, j, kk: (0, 0)))
        out_shape = [out_shape, jax.ShapeDtypeStruct((m, n), BF16)]
        out_spec = [out_spec, pl.BlockSpec((tm, tn), lambda i, j, kk: (i, j))]
    if after is not None:
        extra.append(after)
        in_specs.append(pl.BlockSpec(memory_space=pl.ANY))

    def body(a_ref, b_ref, *rest):
        tiles = rest[:n_t]
        rows = rest[n_t:n_t + n_r]
        gain_ref = rest[n_t + n_r] if norm_gain is not None else None
        rest = rest[n_t + n_r + len(extra):]
        o_ref, acc_ref = rest[0], rest[-1]
        av = a_ref[...]
        if pro is not None:
            av = pro(av)
        if single_pass:
            strip = tn // N_STRIPS
            for s in range(N_STRIPS):
                cols = slice(s * strip, (s + 1) * strip)
                part = _dot(av, b_ref[cols, :] if tb else b_ref[:, cols], dims)
                if epi is not None:
                    part = epi(part, *[t[:, cols] for t in tiles], *[r[:, cols] for r in rows])
                o_ref[:, cols] = part.astype(out_dtype)
            return
        kk = pl.program_id(2)

        @pl.when(kk == 0)
        def _():
            acc_ref[...] = jnp.zeros_like(acc_ref)

        bv = b_ref[...]
        if b_chips > 1:
            bv = bv.reshape(b_block)
        acc_ref[...] += _dot(av, bv, dims)

        @pl.when(kk == nk - 1)
        def _():
            out = acc_ref[...]
            if epi is not None:
                out = epi(out, *[t[...] for t in tiles], *[r[...] for r in rows])
            if gain_ref is not None:
                rest[1][...] = (_rms_stats(out)[0] * gain_ref[...]).astype(BF16)
            out = out.astype(out_dtype)
            o_ref[...] = out.reshape(o_chips, tm // o_chips, tn) if o_chips > 1 else out

    return pl.pallas_call(
        body, name=name, grid=(m // tm, n // tn, nk),
        in_specs=in_specs, out_specs=out_spec, out_shape=out_shape,
        scratch_shapes=[pltpu.VMEM((tm, tn), F32)],
        compiler_params=_cparams(("parallel", "parallel", "arbitrary")),
    )(a, b, *epi_tiles, *epi_rows, *extra)


def row_call(name, body, n_rows, tm, ins, outs, accs=()):
    tm = _pick_tile(n_rows, tm)
    in_specs = []
    for arr, kind in ins:
        if kind == "tile":
            if arr.ndim == 2:
                in_specs.append(pl.BlockSpec((tm, arr.shape[1]), lambda i: (i, 0)))
            else:
                in_specs.append(pl.BlockSpec((arr.shape[0], tm, arr.shape[2]), lambda i: (0, i, 0)))
        elif kind == "full":
            in_specs.append(pl.BlockSpec(arr.shape, functools.partial(lambda i, nd: (0,) * nd, nd=arr.ndim)))
        else:
            where, h = kind
            per = tm // h
            if where == "prev":
                in_specs.append(pl.BlockSpec((h, arr.shape[1]), functools.partial(
                    lambda i, per: (jnp.maximum(i * per - 1, 0), 0), per=per)))
            else:
                last = n_rows // h - 1
                in_specs.append(pl.BlockSpec((h, arr.shape[1]), functools.partial(
                    lambda i, per, last: (jnp.minimum((i + 1) * per, last), 0), per=per, last=last)))
    out_shape, out_specs = [], []
    for shape, dtype in outs:
        out_shape.append(jax.ShapeDtypeStruct(shape, dtype))
        if len(shape) == 2:
            out_specs.append(pl.BlockSpec((tm, shape[1]), lambda i: (i, 0)))
        else:
            out_specs.append(pl.BlockSpec((shape[0], tm, shape[2]), lambda i: (0, i, 0)))
    for shape in accs:
        out_shape.append(jax.ShapeDtypeStruct(shape, F32))
        out_specs.append(pl.BlockSpec(shape, lambda i: (0, 0)))
    n_in, n_out, n_acc = len(ins), len(outs), len(accs)

    def kern(*refs):
        i = pl.program_id(0)
        in_refs = refs[:n_in]
        out_refs = refs[n_in:n_in + n_out]
        acc_refs = refs[n_in + n_out:n_in + n_out + n_acc]
        if n_acc:
            @pl.when(i == 0)
            def _():
                for r in acc_refs:
                    r[...] = jnp.zeros_like(r)
        body(i, in_refs, out_refs, acc_refs)

    res = pl.pallas_call(
        kern, name=name, grid=(n_rows // tm,), in_specs=in_specs, out_specs=out_specs, out_shape=out_shape,
        compiler_params=_cparams(("arbitrary",) if n_acc else ("parallel",)),
    )(*[a for a, _ in ins])
    return list(res)


def _rms_stats(h):
    r = lax.rsqrt(jnp.mean(h * h, axis=-1, keepdims=True) + RMS_EPS)
    return h * r, r


def rms_fwd(name, h, g):
    def body(i, ins, outs, accs):
        xhat, _ = _rms_stats(ins[0][...])
        outs[0][...] = (xhat * ins[1][...]).astype(BF16)

    return row_call(name, body, h.shape[0], ROW_TILE, [(h, "tile"), (g, "full")], [(h.shape, BF16)])[0]


def _rms_bwd_tile(dn, h, g):
    xhat, r = _rms_stats(h)
    dxhat = dn * g
    dh = r * (dxhat - xhat * jnp.mean(dxhat * xhat, axis=-1, keepdims=True))
    dg = jnp.sum(dn * xhat, axis=0, keepdims=True)
    return dh, dg


def rms_bwd(name, dn, h, g, dres):
    def body(i, ins, outs, accs):
        dh, dg = _rms_bwd_tile(ins[0][...].astype(F32), ins[1][...], ins[2][...])
        total = ins[3][...] + dh
        outs[0][...] = total
        outs[1][...] = total.astype(BF16)
        accs[0][...] += dg

    d = h.shape[1]
    out, out16, dg = row_call(name, body, h.shape[0], ROW_TILE,
                              [(dn, "tile"), (h, "tile"), (g, "full"), (dres, "tile")],
                              [(h.shape, F32), (h.shape, BF16)], [(1, d)])
    return out, out16, dg


def mem_norm_bwd(name, dn, mem, g):
    def body(i, ins, outs, accs):
        _, dg = _rms_bwd_tile(ins[0][...].astype(F32), ins[1][...], ins[2][...])
        accs[0][...] += dg

    return row_call(name, body, mem.shape[0], ROW_TILE, [(dn, "tile"), (mem, "tile"), (g, "full")], [],
                    [(1, mem.shape[1])])[0]


def loss_head(name, h, g, target):
    d = h.shape[1]

    def body(i, ins, outs, accs):
        hv, gv = ins[0][...], ins[1][...]
        xhat, _ = _rms_stats(hv)
        err = xhat * gv - ins[2][...]
        dy = err * (1.0 / d)
        dh, dg = _rms_bwd_tile(dy, hv, gv)
        outs[0][...] = dh
        outs[1][...] = dh.astype(BF16)
        accs[0][...] += jnp.full((8, LANES), 0.5 / d, F32) * jnp.sum(err * err)
        accs[1][...] += dg

    dh, dh16, loss, dg = row_call(name, body, h.shape[0], ROW_TILE, [(h, "tile"), (g, "full"), (target, "tile")],
                                  [(h.shape, F32), (h.shape, BF16)], [(8, LANES), (1, d)])
    return dh, dh16, loss, dg


def col_sum(name, x):
    def body(i, ins, outs, accs):
        accs[0][...] += jnp.sum(ins[0][...].astype(F32), axis=0, keepdims=True)

    return row_call(name, body, x.shape[0], ROW_TILE, [(x, "tile")], [], [(1, x.shape[1])])[0]


def _conv_taps(xcat, w_ref, cols, width, halo, tm):
    rows = halo + tm
    acc = None
    for j in range(width):
        s = width - 1 - j
        xs = xcat if s == 0 else pltpu.roll(xcat, s, 0)
        term = xs[halo:rows] * w_ref[j:j + 1, cols]
        acc = term if acc is None else acc + term
    return acc


def _conv_taps_bwd_x(dcat, w_ref, cols, width, halo, tm):
    rows = halo + tm
    acc = None
    for j in range(width):
        s = width - 1 - j
        ds = dcat if s == 0 else pltpu.roll(dcat, rows - s, 0)
        term = ds[0:tm] * w_ref[j:j + 1, cols]
        acc = term if acc is None else acc + term
    return acc


def _conv_taps_bwd_w(dy, xcat, width, halo, tm, wrows):
    rows = halo + tm
    cols = dy.shape[1]
    rid = _iota((wrows, cols), 0)
    diag = _iota((cols, cols), 0) == _iota((cols, cols), 1)
    out = jnp.zeros((wrows, cols), F32)
    for j in range(width):
        s = width - 1 - j
        xs = xcat if s == 0 else pltpu.roll(xcat, s, 0)
        v = jnp.sum(jnp.where(diag, _dot_tn(xs[halo:rows], dy), 0.0), axis=0, keepdims=True)
        out = out + jnp.where(rid == j, v, 0.0)
    return out


def dn_pre(qkv_raw, ba, w_conv, gate):
    s_len = qkv_raw.shape[0]
    tm = min(DN_ROW_TILE, s_len)
    n_blk = qkv_raw.shape[1] // LANES

    def body(i, ins, outs, accs):
        x_ref, xp_ref, ba_ref, w_ref, gate_ref = ins
        qkv_ref, hs_ref = outs

        def blk(cb, carry):
            cols = pl.ds(pl.multiple_of(cb * LANES, LANES), LANES)
            prev = jnp.where(i > 0, xp_ref[:, cols], 0.0)
            xcat = jnp.concatenate([prev, x_ref[:, cols]], axis=0)
            c = _conv_taps(xcat, w_ref, cols, DN_CONV, DN_HALO, tm)
            y = _silu(c)
            rs = lax.rsqrt(jnp.sum(y * y, axis=-1, keepdims=True) + L2_EPS)
            fac = jnp.where(cb < DN_HEADS, DN_HEAD_DIM ** -0.5, 1.0)
            qkv_ref[:, cols] = jnp.where(cb < 2 * DN_HEADS, y * (rs * fac), y)
            return carry

        lax.fori_loop(0, n_blk, blk, 0)

        bav = ba_ref[...]
        beta = _sigmoid(bav)
        g = -jnp.exp(gate_ref[0:1, :]) * _softplus(bav + gate_ref[1:2, :])
        lane = _iota((tm, LANES), 1)
        g = jnp.where((lane >= DN_HEADS) & (lane < 2 * DN_HEADS), g, 0.0)
        r = _iota((tm, tm), 0)
        c = _iota((tm, tm), 1)
        tri = jnp.where((r >= c) & ((r >> CHUNK_SHIFT) == (c >> CHUNK_SHIFT)), 1.0, 0.0)
        gc = _dot_hi(tri, g)
        for h in range(DN_HEADS):
            hs_ref[h] = jnp.where(lane == 0, _lane_col(beta, lane, h),
                                  jnp.where(lane == 1, _lane_col(g, lane, DN_HEADS + h),
                                            jnp.where(lane == 2, _lane_col(gc, lane, DN_HEADS + h), 0.0)))

    return row_call("dn_pre", body, s_len, tm,
                    [(qkv_raw, "tile"), (qkv_raw, ("prev", DN_HALO)), (ba, "tile"), (w_conv, "full"), (gate, "full")],
                    [(qkv_raw.shape, F32), ((DN_HEADS, s_len, LANES), F32)])


def _chunk_masks():
    r = _iota((DN_CHUNK, DN_CHUNK), 0)
    c = _iota((DN_CHUNK, DN_CHUNK), 1)
    return r, c


def _decay_matrix(gc, r, c):
    gc_row = jnp.sum(jnp.where(r == c, gc, 0.0), axis=0, keepdims=True)
    causal = r >= c
    return jnp.where(causal, jnp.exp(jnp.where(causal, gc - gc_row, 0.0)), 0.0)


def _tri_inverse(lows, r, c):
    eye = jnp.where(r == c, 1.0, 0.0)
    ts = [eye for _ in lows]
    b = 1
    while b < DN_CHUNK:
        shift = b.bit_length()
        sel = ((r >> shift) == (c >> shift)) & ((r & b) != 0) & ((c & b) == 0)
        lms = [jnp.where(sel, low, 0.0) for low in lows]
        if b == 1:
            ts = [t - lm for t, lm in zip(ts, lms)]
        else:
            t_lm = [_dot_x3(t, lm) for t, lm in zip(ts, lms)]
            t_lm_t = [_dot_x3(x, t) for x, t in zip(t_lm, ts)]
            ts = [t - x for t, x in zip(ts, t_lm_t)]
        b *= 2
    return ts


def dn_solve(qkv, hs):
    s_len = qkv.shape[0]
    rb = min(ROW_TILE, s_len)
    n_chunk = rb // DN_CHUNK
    interleave = min(SOLVE_INTERLEAVE, n_chunk)

    def body(k_ref, v_ref, hs_ref, u_ref, w_ref, t_ref):
        r, c = _chunk_masks()

        def group(gi, carry):
            rows = [pl.ds(pl.multiple_of((gi * interleave + j) * DN_CHUNK, DN_CHUNK), DN_CHUNK)
                    for j in range(interleave)]
            k = [k_ref[rw, :] for rw in rows]
            beta = [hs_ref[rw, 0:1] for rw in rows]
            gc = [hs_ref[rw, 2:3] for rw in rows]
            kb = [a * b for a, b in zip(k, beta)]
            decay = [_decay_matrix(g, r, c) for g in gc]
            lows = [jnp.where(r > c, _dot_nt(a, b) * d, 0.0) for a, b, d in zip(kb, k, decay)]
            ts = _tri_inverse(lows, r, c)
            us = [_dot_x3(t, v_ref[rw, :] * b) for t, rw, b in zip(ts, rows, beta)]
            ws = [_dot_x3(t, a * jnp.exp(g)) for t, a, g in zip(ts, kb, gc)]
            for j, rw in enumerate(rows):
                u_ref[rw, :] = us[j]
                w_ref[rw, :] = ws[j].astype(BF16)
                t_ref[rw, :] = ts[j]
            return carry

        lax.fori_loop(0, n_chunk // interleave, group, 0)

    return pl.pallas_call(
        body, name="dn_solve", grid=(DN_HEADS, s_len // rb),
        in_specs=[pl.BlockSpec((rb, LANES), lambda h, i: (i, DN_HEADS + h)),
                  pl.BlockSpec((rb, LANES), lambda h, i: (i, 2 * DN_HEADS + h)),
                  pl.BlockSpec((None, rb, LANES), lambda h, i: (h, i, 0))],
        out_specs=[pl.BlockSpec((rb, LANES), lambda h, i: (i, h)),
                   pl.BlockSpec((rb, LANES), lambda h, i: (i, h)),
                   pl.BlockSpec((None, rb, DN_CHUNK), lambda h, i: (h, i, 0))],
        out_shape=[jax.ShapeDtypeStruct((s_len, DN_HEADS * LANES), F32),
                   jax.ShapeDtypeStruct((s_len, DN_HEADS * LANES), BF16),
                   jax.ShapeDtypeStruct((DN_HEADS, s_len, DN_CHUNK), F32)],
        compiler_params=_cparams(("parallel", "parallel")),
    )(qkv, qkv, hs)


def dn_scan_fwd(qkv, u, w, hs):
    s_len = qkv.shape[0]
    rb = min(ROW_TILE, s_len)
    n_chunk = rb // DN_CHUNK
    total_chunks = s_len // DN_CHUNK

    hps = FWD_HEADS_PER_STEP
    groups = DN_HEADS // hps

    def body(q_ref, k_ref, u_ref, w_ref, hs_ref, o_ref, st_ref, state):
        @pl.when(pl.program_id(1) == 0)
        def _():
            state[...] = jnp.zeros_like(state)

        r, c = _chunk_masks()

        def chunk(n, carry):
            rows = pl.ds(pl.multiple_of(n * DN_CHUNK, DN_CHUNK), DN_CHUNK)
            heads = range(hps)
            cols = [slice(h * LANES, (h + 1) * LANES) for h in heads]
            each = lambda f, *xs: [f(*a) for a in zip(*xs)]
            q = [q_ref[rows, cl] for cl in cols]
            k = [k_ref[rows, cl] for cl in cols]
            gc = [hs_ref[h, rows, 2:3] for h in heads]
            st = [state[h] for h in heads]
            for h in heads:
                st_ref[h, n] = st[h]
            gl = each(lambda g: jnp.min(g, axis=0, keepdims=True), gc)
            decay = each(lambda g: _decay_matrix(g, r, c), gc)
            w_st = [_dot(w_ref[rows, cols[h]], st[h]) for h in heads]
            qk = each(_dot_nt, q, k)
            q_st = each(lambda a, g, s: _dot(a * jnp.exp(g), s), q, gc, st)
            vn = [u_ref[rows, cols[h]] - w_st[h] for h in heads]
            ai_vn = each(lambda a, d, b: _dot(a * d, b), qk, decay, vn)
            kd_vn = each(lambda a, g0, g, b: _dot_tn(a * jnp.exp(g0 - g), b), k, gl, gc, vn)
            for h in heads:
                o_ref[rows, cols[h]] = q_st[h] + ai_vn[h]
                state[h] = st[h] * jnp.exp(gl[h]) + kd_vn[h]
            return carry

        lax.fori_loop(0, n_chunk, chunk, 0)

    wide = hps * LANES
    blk = lambda off: pl.BlockSpec((rb, wide), lambda h, i: (i, off + h))
    return pl.pallas_call(
        body, name="dn_scan_fwd", grid=(groups, s_len // rb),
        in_specs=[blk(0), blk(groups), blk(0), blk(0),
                  pl.BlockSpec((hps, rb, LANES), lambda h, i: (h, i, 0))],
        out_specs=[blk(0),
                   pl.BlockSpec((hps, n_chunk, LANES, LANES), lambda h, i: (h, i, 0, 0))],
        out_shape=[jax.ShapeDtypeStruct((s_len, DN_HEADS * LANES), F32),
                   jax.ShapeDtypeStruct((DN_HEADS, total_chunks, LANES, LANES), F32)],
        scratch_shapes=[pltpu.VMEM((hps, LANES, LANES), F32)],
        compiler_params=_cparams(("parallel", "arbitrary")),
    )(qkv, qkv, u, w, hs)


def dn_scan_bwd(qkv, u, w, t_inv, hs, states, d_o):
    s_len = qkv.shape[0]
    rb = min(BWD_SCAN_ROWS, s_len)
    n_chunk = rb // DN_CHUNK
    n_blk = s_len // rb
    hps = BWD_HEADS_PER_STEP
    groups = DN_HEADS // hps

    def body(q_ref, k_ref, v_ref, u_ref, w_ref, t_ref, hs_ref, st_ref, do_ref,
             dq_ref, dk_ref, dv_ref, dhs_ref, dstate):
        @pl.when(pl.program_id(1) == 0)
        def _():
            dstate[...] = jnp.zeros_like(dstate)

        r, c = _chunk_masks()
        causal = r >= c
        strict = r > c
        lane = _iota((DN_CHUNK, LANES), 1)
        upper = jnp.where(r <= c, 1.0, 0.0)
        last_row = _iota((DN_CHUNK, 1), 0) == DN_CHUNK - 1

        def chunk(m, carry):
            n = n_chunk - 1 - m
            rows = pl.ds(pl.multiple_of(n * DN_CHUNK, DN_CHUNK), DN_CHUNK)
            heads = range(hps)
            cols = [slice(h * LANES, (h + 1) * LANES) for h in heads]
            each = lambda f, *xs: [f(*a) for a in zip(*xs)]
            rsum = lambda x: jnp.sum(x, axis=-1, keepdims=True)
            dims_tn = (((0,), (0,)), ((), ()))
            q = [q_ref[rows, cl] for cl in cols]
            k = [k_ref[rows, cl] for cl in cols]
            v = [v_ref[rows, cl] for cl in cols]
            uu = [u_ref[rows, cl] for cl in cols]
            ww = [w_ref[rows, cl] for cl in cols]
            do = [do_ref[rows, cl] for cl in cols]
            tt = [t_ref[h, rows, :] for h in heads]
            beta = [hs_ref[h, rows, 0:1] for h in heads]
            gc = [hs_ref[h, rows, 2:3] for h in heads]
            st = [st_ref[h, n] for h in heads]
            dst = [dstate[h] for h in heads]
            gl = each(lambda g: jnp.min(g, axis=0, keepdims=True), gc)
            egc = each(jnp.exp, gc)
            egl = each(jnp.exp, gl)
            ekd = each(lambda a, b: jnp.exp(a - b), gl, gc)
            decay = each(lambda g: _decay_matrix(g, r, c), gc)
            qd = each(jnp.multiply, q, egc)
            kd = each(jnp.multiply, k, ekd)
            kb = each(jnp.multiply, k, beta)
            w_st = each(_dot, ww, st)
            qk = each(_dot_nt, q, k)
            dqd = each(_dot_nt, do, st)
            kd_dst = each(_dot, kd, dst)
            qd_do = each(_dot_tn, qd, do)
            kbk = each(_dot_nt, kb, k)
            vn = each(jnp.subtract, uu, w_st)
            ai = each(jnp.multiply, qk, decay)
            low = each(lambda a, d: jnp.where(strict, a * d, 0.0), kbk, decay)
            dai = each(lambda a, b: jnp.where(causal, _dot_nt(a, b), 0.0), do, vn)
            ai_do = each(_dot_tn, ai, do)
            dkd = each(_dot_nt, vn, dst)
            dvn = each(jnp.add, ai_do, kd_dst)
            dp = each(jnp.multiply, dai, decay)
            dw = each(lambda a, b: -_dot_nt(a, b), dvn, st)
            w_dvn = each(_dot_tn, ww, dvn)
            dp_k = each(_dot, dp, k)
            dp_q = each(_dot_tn, dp, q)
            drhs_u = each(lambda a, b: _dot_x3(a, b, dims_tn), tt, dvn)
            dgl = each(lambda a, b, e: jnp.sum(a * b) * e, dst, st, egl)
            for h in heads:
                dstate[h] = dst[h] * egl[h] + qd_do[h] - w_dvn[h]
            dq = each(lambda a, e, b: a * e + b, dqd, egc, dp_k)
            dk_a = each(lambda a, e, b: a * e + b, dkd, ekd, dp_q)
            rkd = each(lambda a, b: rsum(a * b), dkd, kd)
            drhs_w = each(lambda a, b: _dot_x3(a, b, dims_tn), tt, dw)
            dl_u = each(_dot_nt, drhs_u, uu)
            dl_w = each(_dot_nt, drhs_w, ww)
            dlow = each(lambda a, b: jnp.where(strict, -(a + b), 0.0), dl_u, dl_w)
            dqm = each(jnp.multiply, dlow, decay)
            m_tot = each(lambda a, b, d, e: a * b + d * e, dai, ai, dlow, low)
            dqm_k = each(_dot, dqm, k)
            dk_l = each(_dot_tn, dqm, kb)
            col_rows = each(lambda m: jnp.sum(m, axis=0, keepdims=True), m_tot)
            col_sums = each(lambda rw: jnp.sum(jnp.where(r == c, rw, 0.0), axis=1, keepdims=True), col_rows)
            dkb_w = each(jnp.multiply, drhs_w, egc)
            dkb = each(jnp.add, dkb_w, dqm_k)
            dgc = [rsum(dqd[h] * qd[h]) - rkd[h] + jnp.where(last_row, jnp.sum(rkd[h]) + dgl[h], 0.0)
                   + rsum(m_tot[h]) + rsum(dkb_w[h] * kb[h]) for h in heads]
            dg = each(lambda a, b: _dot_hi(upper, jnp.where(lane == 1, a - b, 0.0)), dgc, col_sums)
            for h in heads:
                dq_ref[rows, cols[h]] = dq[h]
                dk_ref[rows, cols[h]] = dk_a[h] + dk_l[h] + dkb[h] * beta[h]
                dv_ref[rows, cols[h]] = drhs_u[h] * beta[h]
                dbeta = rsum(drhs_u[h] * v[h]) + rsum(dkb[h] * k[h])
                dhs_ref[h, rows, :] = jnp.where(lane == 0, dbeta, dg[h])
            return carry

        lax.fori_loop(0, n_chunk, chunk, 0)

    wide = hps * LANES
    blk = lambda off: pl.BlockSpec((rb, wide), lambda h, i: (n_blk - 1 - i, off + h))
    head = blk(0)
    hs_spec = pl.BlockSpec((hps, rb, LANES), lambda h, i: (h, n_blk - 1 - i, 0))
    full = jax.ShapeDtypeStruct((s_len, DN_HEADS * LANES), F32)
    return pl.pallas_call(
        body, name="dn_scan_bwd", grid=(groups, n_blk),
        in_specs=[blk(0), blk(groups), blk(2 * groups), head, head,
                  pl.BlockSpec((hps, rb, DN_CHUNK), lambda h, i: (h, n_blk - 1 - i, 0)), hs_spec,
                  pl.BlockSpec((hps, n_chunk, LANES, LANES), lambda h, i: (h, n_blk - 1 - i, 0, 0)), head],
        out_specs=[head, head, head, hs_spec],
        out_shape=[full, full, full, jax.ShapeDtypeStruct((DN_HEADS, s_len, LANES), F32)],
        scratch_shapes=[pltpu.VMEM((hps, LANES, LANES), F32)],
        compiler_params=_cparams(("parallel", "arbitrary")),
    )(qkv, qkv, qkv, u, w, t_inv, hs, states, d_o)


def dn_post(o, z, out_norm):
    def body(i, ins, outs, accs):
        gn = ins[2][...]
        for h in range(DN_HEADS):
            cols = slice(h * LANES, (h + 1) * LANES)
            xhat, _ = _rms_stats(ins[0][:, cols])
            outs[0][:, cols] = (xhat * gn * _silu(ins[1][:, cols])).astype(BF16)

    return row_call("dn_post", body, o.shape[0], ROW_TILE, [(o, "tile"), (z, "tile"), (out_norm, "full")],
                    [(o.shape, BF16)])[0]


def dn_post_bwd(d_og, o, z, out_norm):
    def body(i, ins, outs, accs):
        gn = ins[3][...]
        dgn = jnp.zeros((1, LANES), F32)
        for h in range(DN_HEADS):
            cols = slice(h * LANES, (h + 1) * LANES)
            dy, zh = ins[0][:, cols].astype(F32), ins[2][:, cols]
            xhat, r = _rms_stats(ins[1][:, cols])
            sz = _silu(zh)
            dgn = dgn + jnp.sum(dy * xhat * sz, axis=0, keepdims=True)
            outs[1][:, cols] = (dy * xhat * gn * _silu_grad(zh)).astype(BF16)
            dxhat = dy * gn * sz
            outs[0][:, cols] = r * (dxhat - xhat * jnp.mean(dxhat * xhat, axis=-1, keepdims=True))
        accs[0][...] += dgn

    return row_call("dn_post_bwd", body, o.shape[0], ROW_TILE,
                    [(d_og, "tile"), (o, "tile"), (z, "tile"), (out_norm, "full")],
                    [(o.shape, F32), (o.shape, BF16)], [(1, LANES)])


def dn_pre_bwd(dq, dk, dv, dhs, qkv_raw, ba, w_conv, gate):
    s_len = qkv_raw.shape[0]
    tm = min(DN_ROW_TILE, s_len)

    def body(i, ins, outs, accs):
        dq_ref, dk_ref, dv_ref, dhs_ref, x_ref, xp_ref, ba_ref, w_ref, gate_ref = ins
        dc_ref, dba_ref = outs

        def blk(cb, carry):
            cols = pl.ds(pl.multiple_of(cb * LANES, LANES), LANES)
            hcols = pl.ds(pl.multiple_of((cb & (DN_HEADS - 1)) * LANES, LANES), LANES)
            prev = jnp.where(i > 0, xp_ref[:, cols], 0.0)
            xcat = jnp.concatenate([prev, x_ref[:, cols]], axis=0)
            c = _conv_taps(xcat, w_ref, cols, DN_CONV, DN_HALO, tm)
            y = _silu(c)
            dy = jnp.where(cb < DN_HEADS, dq_ref[:, hcols],
                           jnp.where(cb < 2 * DN_HEADS, dk_ref[:, hcols], dv_ref[:, hcols]))
            rs = lax.rsqrt(jnp.sum(y * y, axis=-1, keepdims=True) + L2_EPS)
            fac = jnp.where(cb < DN_HEADS, DN_HEAD_DIM ** -0.5, 1.0)
            nrm = y * rs
            dn = dy * fac
            dy_norm = rs * (dn - nrm * jnp.sum(dn * nrm, axis=-1, keepdims=True))
            dc_ref[:, cols] = jnp.where(cb < 2 * DN_HEADS, dy_norm, dy) * _silu_grad(c)
            return carry

        lax.fori_loop(0, qkv_raw.shape[1] // LANES, blk, 0)

        lane = _iota((tm, LANES), 1)
        dbeta = jnp.zeros((tm, LANES), F32)
        dg = jnp.zeros((tm, LANES), F32)
        for h in range(DN_HEADS):
            dbeta = dbeta + jnp.where(lane == h, dhs_ref[h, :, 0:1], 0.0)
            dg = dg + jnp.where(lane == DN_HEADS + h, dhs_ref[h, :, 1:2], 0.0)
        bav = ba_ref[...]
        beta = _sigmoid(bav)
        ea = jnp.exp(gate_ref[0:1, :])
        pre = bav + gate_ref[1:2, :]
        g = -ea * _softplus(pre)
        da = dg * (-ea) * _sigmoid(pre)
        dba_ref[...] = (dbeta * beta * (1.0 - beta) + da).astype(BF16)
        rid = _iota((8, LANES), 0)
        accs[0][...] += (jnp.where(rid == 0, jnp.sum(dg * g, axis=0, keepdims=True), 0.0)
                         + jnp.where(rid == 1, jnp.sum(da, axis=0, keepdims=True), 0.0))

    return row_call("dn_pre_bwd", body, s_len, tm,
                    [(dq, "tile"), (dk, "tile"), (dv, "tile"), (dhs, "tile"), (qkv_raw, "tile"),
                     (qkv_raw, ("prev", DN_HALO)), (ba, "tile"), (w_conv, "full"), (gate, "full")],
                    [(qkv_raw.shape, F32), (ba.shape, BF16)], [(8, LANES)])


def dn_conv_bwd(dc, qkv_raw, w_conv):
    s_len = dc.shape[0]
    tm = min(DN_ROW_TILE, s_len)
    nt = s_len // tm

    def body(i, ins, outs, accs):
        dc_ref, dn_ref, x_ref, xp_ref, w_ref = ins

        def blk(cb, carry):
            cols = pl.ds(pl.multiple_of(cb * LANES, LANES), LANES)
            dy = dc_ref[:, cols]
            nxt = jnp.where(i < nt - 1, dn_ref[:, cols], 0.0)
            dcat = jnp.concatenate([dy, nxt], axis=0)
            outs[0][:, cols] = _conv_taps_bwd_x(dcat, w_ref, cols, DN_CONV, DN_HALO, tm).astype(BF16)
            prev = jnp.where(i > 0, xp_ref[:, cols], 0.0)
            xcat = jnp.concatenate([prev, x_ref[:, cols]], axis=0)
            accs[0][:, cols] += _conv_taps_bwd_w(dy, xcat, DN_CONV, DN_HALO, tm, 8)
            return carry

        lax.fori_loop(0, dc.shape[1] // LANES, blk, 0)

    return row_call("dn_conv_bwd", body, s_len, tm,
                    [(dc, "tile"), (dc, ("next", DN_HALO)), (qkv_raw, "tile"), (qkv_raw, ("prev", DN_HALO)),
                     (w_conv, "full")],
                    [(dc.shape, BF16)], [(8, dc.shape[1])])


def _glu(u_ref, cols, d):
    return u_ref[:, cols] * _sigmoid(u_ref[:, pl.ds(pl.multiple_of(d + cols.start, LANES), cols.size)])


def cv_core_fwd(u, w_dw, b_dw, ln_g, ln_b):
    s_len, d = u.shape[0], u.shape[1] // 2
    tm = min(CONV_ROW_TILE, s_len)

    def body(i, ins, outs, accs):
        u_ref, up_ref, w_ref, bdw_ref, g_ref, b_ref = ins
        s_ref, c_ref = outs

        def blk(cb, carry):
            cols = pl.ds(pl.multiple_of(cb * LANES, LANES), LANES)
            prev = jnp.where(i > 0, _glu(up_ref, cols, d), 0.0)
            xcat = jnp.concatenate([prev, _glu(u_ref, cols, d)], axis=0)
            c_ref[:, cols] = _conv_taps(xcat, w_ref, cols, CV_WIDTH, CV_HALO, tm) + bdw_ref[:, cols]
            return carry

        lax.fori_loop(0, d // LANES, blk, 0)
        c = c_ref[...]
        mu = jnp.mean(c, axis=-1, keepdims=True)
        xc = c - mu
        rstd = lax.rsqrt(jnp.mean(xc * xc, axis=-1, keepdims=True) + LN_EPS)
        s_ref[...] = _silu(xc * rstd * g_ref[...] + b_ref[...]).astype(BF16)

    return row_call("cv_core_fwd", body, s_len, tm,
                    [(u, "tile"), (u, ("prev", CV_HALO)), (w_dw, "full"), (b_dw, "full"), (ln_g, "full"),
                     (ln_b, "full")],
                    [((s_len, d), BF16), ((s_len, d), F32)])


def cv_ln_bwd(ds, c, ln_g, ln_b):
    def body(i, ins, outs, accs):
        cv, g = ins[1][...], ins[2][...]
        mu = jnp.mean(cv, axis=-1, keepdims=True)
        xc = cv - mu
        rstd = lax.rsqrt(jnp.mean(xc * xc, axis=-1, keepdims=True) + LN_EPS)
        xhat = xc * rstd
        dl = ins[0][...].astype(F32) * _silu_grad(xhat * g + ins[3][...])
        dxhat = dl * g
        dc = rstd * (dxhat - jnp.mean(dxhat, axis=-1, keepdims=True)
                     - xhat * jnp.mean(dxhat * xhat, axis=-1, keepdims=True))
        outs[0][...] = dc
        rid = _iota((8, cv.shape[1]), 0)
        accs[0][...] += (jnp.where(rid == 0, jnp.sum(dl * xhat, axis=0, keepdims=True), 0.0)
                         + jnp.where(rid == 1, jnp.sum(dl, axis=0, keepdims=True), 0.0)
                         + jnp.where(rid == 2, jnp.sum(dc, axis=0, keepdims=True), 0.0))

    return row_call("cv_ln_bwd", body, c.shape[0], ROW_TILE,
                    [(ds, "tile"), (c, "tile"), (ln_g, "full"), (ln_b, "full")], [(c.shape, F32)], [(8, c.shape[1])])


def cv_conv_bwd(dc, u, w_dw):
    s_len, d = dc.shape
    tm = min(CONV_ROW_TILE, s_len)
    nt = s_len // tm

    def body(i, ins, outs, accs):
        dc_ref, dn_ref, u_ref, up_ref, w_ref = ins

        def blk(cb, carry):
            cols = pl.ds(pl.multiple_of(cb * LANES, LANES), LANES)
            gcols = pl.ds(pl.multiple_of(d + cb * LANES, LANES), LANES)
            dy = dc_ref[:, cols]
            nxt = jnp.where(i < nt - 1, dn_ref[:, cols], 0.0)
            dgl = _conv_taps_bwd_x(jnp.concatenate([dy, nxt], axis=0), w_ref, cols, CV_WIDTH, CV_HALO, tm)
            u1, sg = u_ref[:, cols], _sigmoid(u_ref[:, gcols])
            du1 = dgl * sg
            du2 = dgl * u1 * sg * (1.0 - sg)
            outs[0][:, cols] = du1.astype(BF16)
            outs[0][:, gcols] = du2.astype(BF16)
            accs[1][:, cols] += jnp.sum(du1, axis=0, keepdims=True)
            accs[1][:, gcols] += jnp.sum(du2, axis=0, keepdims=True)
            prev = jnp.where(i > 0, _glu(up_ref, cols, d), 0.0)
            xcat = jnp.concatenate([prev, u1 * sg], axis=0)
            accs[0][:, cols] += _conv_taps_bwd_w(dy, xcat, CV_WIDTH, CV_HALO, tm, CV_HALO)
            return carry

        lax.fori_loop(0, d // LANES, blk, 0)

    return row_call("cv_conv_bwd", body, s_len, tm,
                    [(dc, "tile"), (dc, ("next", CV_HALO)), (u, "tile"), (u, ("prev", CV_HALO)), (w_dw, "full")],
                    [(u.shape, BF16)], [(CV_HALO, d), (1, 2 * d)])


def xa_core_fwd(name, q, kv):
    d = q.shape[1]

    def body(i, ins, outs, accs):
        for h in range(XA_HEADS):
            cols = slice(h * XA_HEAD_DIM, (h + 1) * XA_HEAD_DIM)
            vcols = slice(d + h * XA_HEAD_DIM, d + (h + 1) * XA_HEAD_DIM)
            s = _dot_nt(ins[0][:, cols], ins[1][:, cols]) * (XA_HEAD_DIM ** -0.5)
            e = jnp.exp(s - jnp.max(s, axis=-1, keepdims=True))
            p = e / jnp.sum(e, axis=-1, keepdims=True)
            outs[0][:, cols] = _dot(p, ins[1][:, vcols]).astype(BF16)

    return row_call(name, body, q.shape[0], ROW_TILE, [(q, "tile"), (kv, "full")], [(q.shape, BF16)])[0]


def xa_core_bwd(name, d_o, q, kv):
    d = q.shape[1]

    def body(i, ins, outs, accs):
        for h in range(XA_HEADS):
            cols = slice(h * XA_HEAD_DIM, (h + 1) * XA_HEAD_DIM)
            vcols = slice(d + h * XA_HEAD_DIM, d + (h + 1) * XA_HEAD_DIM)
            qh, kh, vh, doh = ins[1][:, cols], ins[2][:, cols], ins[2][:, vcols], ins[0][:, cols]
            s = _dot_nt(qh, kh) * (XA_HEAD_DIM ** -0.5)
            e = jnp.exp(s - jnp.max(s, axis=-1, keepdims=True))
            p = e / jnp.sum(e, axis=-1, keepdims=True)
            dp = _dot_nt(doh, vh)
            ds = p * (dp - jnp.sum(dp * p, axis=-1, keepdims=True)) * (XA_HEAD_DIM ** -0.5)
            outs[0][:, cols] = _dot(ds, kh).astype(BF16)
            accs[0][:, cols] += _dot_tn(ds, qh)
            accs[0][:, vcols] += _dot_tn(p, doh)

    return row_call(name, body, q.shape[0], ROW_TILE, [(d_o, "tile"), (q, "tile"), (kv, "full")],
                    [(q.shape, BF16)], [kv.shape])


def adamw(name, w, g, m, v):
    def body(i, ins, outs, accs):
        wv, gv = ins[0][...], ins[1][...]
        mn = ADAM_B1 * ins[2][...] + (1.0 - ADAM_B1) * gv
        vn = ADAM_B2 * ins[3][...] + (1.0 - ADAM_B2) * jnp.square(gv)
        m_hat = mn / (1.0 - ADAM_B1 ** ADAM_STEP)
        v_hat = vn / (1.0 - ADAM_B2 ** ADAM_STEP)
        outs[0][...] = -ADAM_LR * (m_hat / (jnp.sqrt(v_hat) + ADAM_EPS) + ADAM_WD * wv)
        outs[1][...] = mn
        outs[2][...] = vn

    return row_call(name, body, w.shape[0], ROW_TILE, [(w, "tile"), (g, "tile"), (m, "tile"), (v, "tile")],
                    [(w.shape, F32)] * 3)


def adamw_halves(name, w, g_mine, g_sibling, m, v, core):
    n_layers = len(g_mine)
    rows, cols = w.shape
    half_rows = rows // n_layers // 2
    tm = _pick_tile(half_rows, ADAMW_ROW_TILE)
    per_half = half_rows // tm

    def body(core_ref, w_ref, *rest):
        g_refs = rest[:2 * n_layers]
        m_ref, v_ref, g_out, d_out, m_out, v_out = rest[2 * n_layers:]
        i = pl.program_id(0)
        mine = ((i // per_half) % 2) == core_ref[0]
        layer = i // (2 * per_half)
        gv = jnp.where(mine, g_refs[0][...], g_refs[n_layers][...])
        for l in range(1, n_layers):
            gv = jnp.where(layer == l, jnp.where(mine, g_refs[l][...], g_refs[n_layers + l][...]), gv)
        mn = ADAM_B1 * m_ref[...] + (1.0 - ADAM_B1) * gv
        vn = ADAM_B2 * v_ref[...] + (1.0 - ADAM_B2) * jnp.square(gv)
        m_hat = mn / (1.0 - ADAM_B1 ** ADAM_STEP)
        v_hat = vn / (1.0 - ADAM_B2 ** ADAM_STEP)
        g_out[...] = gv
        d_out[...] = -ADAM_LR * (m_hat / (jnp.sqrt(v_hat) + ADAM_EPS) + ADAM_WD * w_ref[...])
        m_out[...] = mn
        v_out[...] = vn

    whole = pl.BlockSpec((tm, cols), lambda i, core_ref: (i, 0))
    half = pl.BlockSpec((tm, cols), lambda i, core_ref: (i % per_half, 0))
    return pl.pallas_call(
        body, name=name,
        grid_spec=pltpu.PrefetchScalarGridSpec(
            num_scalar_prefetch=1, grid=(2 * per_half * n_layers,),
            in_specs=[whole] + [half] * (2 * n_layers) + [whole, whole], out_specs=[whole] * 4),
        out_shape=[jax.ShapeDtypeStruct(w.shape, F32)] * 4,
        compiler_params=_cparams(("parallel",)),
    )(core, w, *g_mine, *g_sibling, m, v)


HBM_SPEC = pl.BlockSpec(memory_space=pltpu.HBM)


def _position():
    return lax.axis_index("x"), lax.axis_index("y"), lax.axis_index("c")


def _other_chips(x, y):
    return [(1 - x, y), (x, 1 - y), (1 - x, 1 - y)]


def _row_chunks(rows):
    return rows // D2D_CHUNK_ROWS if rows % D2D_CHUNK_ROWS == 0 else 1


def _start_chunked(make, rows):
    k = _row_chunks(rows)
    for i in range(k):
        make(i * (rows // k), rows // k).start()


def gather_shards(packs):
    n = len(packs)

    def body(*refs):
        srcs, outs = refs[:n], refs[n:2 * n]
        send_sems, recv_sems = refs[2 * n:]
        x, y, c = _position()
        me = 2 * x + y
        chips = _other_chips(x, y)
        sibling = (x, y, 1 - c)

        def over_ici(a, j):
            px, py = chips[j]
            rows = srcs[a].shape[0] // 2
            return pltpu.make_async_remote_copy(
                src_ref=srcs[a].at[pl.ds(c * rows, rows), :], dst_ref=outs[a].at[me, pl.ds(c * rows, rows), :],
                send_sem=send_sems.at[a, j], recv_sem=recv_sems.at[a, j], device_id=(px, py, c), device_id_type=MESH)

        def landed(a, j):
            px, py = chips[j]
            rows = srcs[a].shape[0] // 2
            part = outs[a].at[2 * px + py, pl.ds(c * rows, rows), :]
            return pltpu.make_async_remote_copy(
                src_ref=part, dst_ref=part, send_sem=send_sems.at[a, j], recv_sem=recv_sems.at[a, j],
                device_id=(px, py, c), device_id_type=MESH)

        def over_d2d(a, j, cc, off, size):
            px, py = chips[j]
            rows = srcs[a].shape[0] // 2
            part = outs[a].at[2 * px + py, pl.ds(cc * rows + off, size), :]
            return pltpu.make_async_remote_copy(
                src_ref=part, dst_ref=part, send_sem=send_sems.at[a, 3 + j], recv_sem=recv_sems.at[a, 3 + j],
                device_id=sibling, device_id_type=MESH)

        for a in range(n):
            for j in range(3):
                over_ici(a, j).start()
        for a in range(n):
            for j in range(3):
                landed(a, j).wait_recv()
                _start_chunked(functools.partial(over_d2d, a, j, c), srcs[a].shape[0] // 2)
        for a in range(n):
            rows = srcs[a].shape[0] // 2
            for j in range(3):
                over_d2d(a, j, 1 - c, 0, rows).wait_recv()
                over_d2d(a, j, c, 0, rows).wait_send()
                over_ici(a, j).wait_send()

    return pl.pallas_call(
        body, name="gather_shards",
        in_specs=[HBM_SPEC] * n, out_specs=[HBM_SPEC] * n,
        out_shape=[jax.ShapeDtypeStruct((N_CHIPS,) + p.shape, p.dtype) for p in packs],
        scratch_shapes=[pltpu.SemaphoreType.DMA((n, 6)), pltpu.SemaphoreType.DMA((n, 6))],
    )(*packs)


def pair_split(name, packs):
    n = len(packs)

    def body(*refs):
        srcs, outs = refs[:n], refs[n:2 * n]
        send_sems, recv_sems = refs[2 * n:]
        x, y, c = _position()

        def remote(a, off, size):
            rows = srcs[a].shape[1] // 2
            return pltpu.make_async_remote_copy(
                src_ref=srcs[a].at[:, pl.ds((1 - c) * rows + off, size), :],
                dst_ref=outs[a].at[:, pl.ds(off, size), :],
                send_sem=send_sems.at[a], recv_sem=recv_sems.at[a], device_id=(x, y, 1 - c), device_id_type=MESH)

        for a in range(n):
            _start_chunked(functools.partial(remote, a), srcs[a].shape[1] // 2)
        for a in range(n):
            remote(a, 0, srcs[a].shape[1] // 2).wait()

    return pl.pallas_call(
        body, name=name, in_specs=[HBM_SPEC] * n, out_specs=[HBM_SPEC] * n,
        out_shape=[jax.ShapeDtypeStruct((p.shape[0], p.shape[1] // 2, p.shape[2]), p.dtype) for p in packs],
        scratch_shapes=[pltpu.SemaphoreType.DMA((n,)), pltpu.SemaphoreType.DMA((n,))],
    )(*packs)


def pair_join(name, halves):
    n = len(halves)

    def body(*refs):
        srcs, outs = refs[:n], refs[n:2 * n]
        send_sems, recv_sems = refs[2 * n:]
        x, y, c = _position()

        def remote(a, off, size):
            return pltpu.make_async_remote_copy(
                src_ref=srcs[a].at[pl.ds(off, size), :], dst_ref=outs[a].at[pl.ds(off, size), :],
                send_sem=send_sems.at[a], recv_sem=recv_sems.at[a], device_id=(x, y, 1 - c), device_id_type=MESH)

        for a in range(n):
            _start_chunked(functools.partial(remote, a), srcs[a].shape[0])
        for a in range(n):
            remote(a, 0, srcs[a].shape[0]).wait()

    return pl.pallas_call(
        body, name=name, in_specs=[HBM_SPEC] * n, out_specs=[HBM_SPEC] * n,
        out_shape=[jax.ShapeDtypeStruct(p.shape, p.dtype) for p in halves],
        scratch_shapes=[pltpu.SemaphoreType.DMA((n,)), pltpu.SemaphoreType.DMA((n,))],
    )(*halves)


SEM_SPEC = pl.BlockSpec(memory_space=pltpu.SEMAPHORE)
DATAFLOW = pltpu.SideEffectType.DATAFLOW_SIDE_EFFECTING


def _ici_copy(kind, srcs, lands, send_sems, recv_sems, a, j):
    x, y, c = _position()
    px, py = _other_chips(x, y)[j]
    if kind == "gather":
        rows = srcs[a].shape[0] // 2
        src = srcs[a].at[pl.ds(c * rows, rows), :]
        dst = lands[a].at[2 * x + y, pl.ds(c * rows, rows), :]
    else:
        src = srcs[a].at[2 * px + py]
        dst = lands[a].at[j]
    return pltpu.make_async_remote_copy(src_ref=src, dst_ref=dst, send_sem=send_sems, recv_sem=recv_sems,
                                        device_id=(px, py, c), device_id_type=MESH)


def ici_start(name, kind, srcs, land_shapes):
    n = len(srcs)
    lands = [pltpu.with_memory_space_constraint(lax.empty(shp, s.dtype), pltpu.HBM) for shp, s in zip(land_shapes, srcs)]

    def body(*refs):
        src_refs, land_refs = refs[:n], refs[n:2 * n]
        send_sems, recv_sems = refs[2 * n], refs[2 * n + 1]
        token = refs[-1]
        for a in range(n):
            for j in range(N_CHIPS - 1):
                _ici_copy(kind, src_refs, land_refs, send_sems, recv_sems, a, j).start()
        token[...] = jnp.zeros_like(token)

    sems = pltpu.SemaphoreType.DMA(())
    res = pl.pallas_call(
        body, name=name,
        out_shape=[sems, sems] + [pltpu.HBM(s.shape, s.dtype) for s in srcs]
        + [pltpu.HBM(l.shape, l.dtype) for l in lands] + [jax.ShapeDtypeStruct((8, LANES), F32)],
        in_specs=[HBM_SPEC] * (2 * n),
        out_specs=[SEM_SPEC, SEM_SPEC] + [HBM_SPEC] * (2 * n) + [pl.BlockSpec(memory_space=pltpu.VMEM)],
        input_output_aliases={i: 2 + i for i in range(2 * n)},
        compiler_params=pltpu.CompilerParams(has_side_effects=DATAFLOW),
    )(*[pltpu.with_memory_space_constraint(s, pltpu.HBM) for s in srcs], *lands)
    return res[0], res[1], list(res[2:2 + n]), list(res[2 + n:2 + 2 * n]), res[-1]


def ici_wait(name, kind, send_sems, recv_sems, srcs, lands, after):
    n = len(srcs)

    def body(*refs):
        src_refs, land_refs = refs[:n], refs[n:2 * n]
        send, recv = refs[2 * n], refs[2 * n + 1]
        for a in range(n):
            for j in range(N_CHIPS - 1):
                cp = _ici_copy(kind, src_refs, land_refs, send, recv, a, j)
                cp.wait_send()
                cp.wait_recv()

    res = pl.pallas_call(
        body, name=name,
        out_shape=[pltpu.HBM(s.shape, s.dtype) for s in srcs] + [pltpu.HBM(l.shape, l.dtype) for l in lands],
        in_specs=[HBM_SPEC] * (2 * n) + [SEM_SPEC, SEM_SPEC, pl.BlockSpec(memory_space=pl.ANY)],
        out_specs=[HBM_SPEC] * (2 * n),
        input_output_aliases={i: i for i in range(2 * n)},
        compiler_params=pltpu.CompilerParams(has_side_effects=DATAFLOW),
    )(*srcs, *lands, send_sems, recv_sems, after)
    return list(res[:n]), list(res[n:])


def pair_forward(gathered):
    n = len(gathered)

    def body(*refs):
        outs = refs[n:2 * n]
        send_sems, recv_sems = refs[2 * n:]
        x, y, c = _position()
        chips = _other_chips(x, y)

        def part(a, j, cc, off, size):
            px, py = chips[j]
            rows = outs[a].shape[1] // 2
            ref = outs[a].at[2 * px + py, pl.ds(cc * rows + off, size), :]
            return pltpu.make_async_remote_copy(
                src_ref=ref, dst_ref=ref, send_sem=send_sems.at[a, j], recv_sem=recv_sems.at[a, j],
                device_id=(x, y, 1 - c), device_id_type=MESH)

        for a in range(n):
            for j in range(N_CHIPS - 1):
                _start_chunked(functools.partial(part, a, j, c), outs[a].shape[1] // 2)
        for a in range(n):
            rows = outs[a].shape[1] // 2
            for j in range(N_CHIPS - 1):
                part(a, j, 1 - c, 0, rows).wait_recv()
                part(a, j, c, 0, rows).wait_send()

    return pl.pallas_call(
        body, name="pair_forward", in_specs=[HBM_SPEC] * n, out_specs=[HBM_SPEC] * n,
        out_shape=[jax.ShapeDtypeStruct(g.shape, g.dtype) for g in gathered],
        input_output_aliases={i: i for i in range(n)},
        scratch_shapes=[pltpu.SemaphoreType.DMA((n, N_CHIPS - 1)), pltpu.SemaphoreType.DMA((n, N_CHIPS - 1))],
    )(*gathered)


def all_sum_small(part):
    n_dev = 8
    rows = part.shape[0]

    def body(src, out, buf, send_sems, recv_sems):
        x, y, c = _position()
        me = 4 * x + 2 * y + c
        buf[me] = src[...]
        copies = []
        for k in range(1, n_dev):
            px, py, pc = x ^ ((k >> 2) & 1), y ^ ((k >> 1) & 1), c ^ (k & 1)
            cp = pltpu.make_async_remote_copy(
                src_ref=src, dst_ref=buf.at[me], send_sem=send_sems.at[k - 1], recv_sem=recv_sems.at[k - 1],
                device_id=(px, py, pc), device_id_type=MESH)
            cp.start()
            copies.append(cp)
        for cp in copies:
            cp.wait()
        acc = buf[0]
        for k in range(1, n_dev):
            acc = acc + buf[k]
        out[...] = acc

    return pl.pallas_call(
        body, name="all_sum_small",
        in_specs=[pl.BlockSpec(memory_space=pltpu.VMEM)], out_specs=pl.BlockSpec(memory_space=pltpu.VMEM),
        out_shape=jax.ShapeDtypeStruct(part.shape, F32),
        scratch_shapes=[pltpu.VMEM((n_dev, rows, part.shape[1]), F32),
                        pltpu.SemaphoreType.DMA((n_dev - 1,)), pltpu.SemaphoreType.DMA((n_dev - 1,))],
    )(part)


def add_pairs(name, src, theirs, core, out_dtype):
    slabs, rows, cols = theirs.shape
    tm = _pick_tile(rows, ROW_TILE)
    nb = rows // tm

    def body(core_ref, a_ref, b_ref, o_ref):
        o_ref[...] = (a_ref[...].astype(F32) + b_ref[...].astype(F32)).astype(out_dtype)

    return pl.pallas_call(
        body, name=name,
        grid_spec=pltpu.PrefetchScalarGridSpec(
            num_scalar_prefetch=1, grid=(slabs, nb),
            in_specs=[pl.BlockSpec((None, tm, cols), lambda s, i, core_ref: (s, core_ref[0] * nb + i, 0)),
                      pl.BlockSpec((None, tm, cols), lambda s, i, core_ref: (s, i, 0))],
            out_specs=pl.BlockSpec((None, tm, cols), lambda s, i, core_ref: (s, i, 0))),
        out_shape=jax.ShapeDtypeStruct(theirs.shape, out_dtype),
        compiler_params=_cparams(("parallel", "parallel")),
    )(core, src, theirs)


def add_four(name, src, theirs, chip):
    _, rows, cols = theirs.shape
    tm = _pick_tile(rows, ROW_TILE)

    def body(chip_ref, a_ref, b_ref, o_ref):
        acc = a_ref[...].astype(F32)
        for j in range(N_CHIPS - 1):
            acc = acc + b_ref[j].astype(F32)
        o_ref[...] = acc

    return pl.pallas_call(
        body, name=name,
        grid_spec=pltpu.PrefetchScalarGridSpec(
            num_scalar_prefetch=1, grid=(rows // tm,),
            in_specs=[pl.BlockSpec((None, tm, cols), lambda i, chip_ref: (chip_ref[0], i, 0)),
                      pl.BlockSpec((N_CHIPS - 1, tm, cols), lambda i, chip_ref: (0, i, 0))],
            out_specs=pl.BlockSpec((tm, cols), lambda i, chip_ref: (i, 0))),
        out_shape=jax.ShapeDtypeStruct((rows, cols), F32),
        compiler_params=_cparams(("parallel",)),
    )(chip, src, theirs)


PACK_COLS = 1024
SMALL_ROW_MULTIPLE = 32
BIG = ["dn_w_in", "dn_w_out", "cv_w_pw1", "cv_w_pw2", "xa_w_q", "xa_w_kv", "xa_w_o", "mlp_w_up", "mlp_w_down"]
SMALL = ["dn_w_conv", "cv_norm", "cv_b_pw1", "cv_w_dw", "cv_b_dw", "cv_ln_g", "cv_ln_b", "cv_b_pw2"]
SHARD_AXIS = {"dn_w_in": 2, "dn_w_conv": 2, "dn_w_out": 1, "cv_norm": 1, "cv_w_pw1": 2, "cv_b_pw1": 1,
              "cv_w_dw": 2, "cv_b_dw": 1, "cv_ln_g": 1, "cv_ln_b": 1, "cv_w_pw2": 1, "cv_b_pw2": 1,
              "xa_w_q": 1, "xa_w_kv": 2, "xa_w_o": 1, "mlp_w_up": 2, "mlp_w_down": 1}
REPLICATED = ["dn_norm", "dn_a_log", "dn_dt_bias", "dn_out_norm", "xa_norm", "xa_mem_norm", "mlp_norm", "final_norm"]


def _pack_rows(size):
    return -(-size // PACK_COLS)


SHARD_SHAPES = {
    "dn_w_in": (1, 1024, 1028), "dn_w_conv": (1, 4, 768), "dn_w_out": (1, 256, 1024), "cv_norm": (1, 256),
    "cv_w_pw1": (1, 1024, 512), "cv_b_pw1": (1, 512), "cv_w_dw": (1, 31, 256), "cv_b_dw": (1, 256),
    "cv_ln_g": (1, 256), "cv_ln_b": (1, 256), "cv_w_pw2": (1, 256, 1024), "cv_b_pw2": (1, 256),
    "xa_w_q": (2, 256, 1024), "xa_w_kv": (2, 1024, 512), "xa_w_o": (2, 256, 1024),
    "mlp_w_up": (2, 1024, 1024), "mlp_w_down": (2, 1024, 1024)}


def _shard_shape(nm):
    return SHARD_SHAPES[nm]


def _pack(tensors, names, dtype, row_multiple):
    pieces = []
    for nm in names:
        t = tensors[nm]
        flat = t.reshape(t.shape[0], -1) if t.ndim > len(_shard_shape(nm)) else t.reshape(1, -1)
        pad = _pack_rows(flat.shape[1]) * PACK_COLS - flat.shape[1]
        pieces.append(jnp.pad(flat.astype(dtype), ((0, 0), (0, pad))))
    cat = jnp.concatenate(pieces, axis=1)
    rows = cat.shape[1] // PACK_COLS
    total = -(-rows // row_multiple) * row_multiple
    cat = jnp.pad(cat, ((0, 0), (0, (total - rows) * PACK_COLS)))
    return cat.reshape(cat.shape[0], total, PACK_COLS)


def _unpack(pack, names):
    lead = pack.shape[:-2]
    flat = pack.reshape(lead + (-1,))
    out, off = {}, 0
    for nm in names:
        shp = _shard_shape(nm)
        size = 1
        for s in shp:
            size *= s
        out[nm] = flat[..., off:off + size].reshape(lead + shp)
        off += _pack_rows(size) * PACK_COLS
    return out


def _to_full(nm, stacked):
    ax = SHARD_AXIS[nm]
    moved = jnp.moveaxis(stacked, 0, ax)
    shp = list(_shard_shape(nm))
    shp[ax] *= N_CHIPS
    return moved.reshape(shp)


def _to_shards(nm, full):
    ax = SHARD_AXIS[nm]
    shp = list(_shard_shape(nm))
    split = full.reshape(shp[:ax] + [N_CHIPS, shp[ax]] + shp[ax + 1:])
    return jnp.moveaxis(split, ax, 0)


def _row(v):
    return v.reshape(1, -1)


class Stacked:
    def __init__(self, arr, split, layer):
        self.arr, self.kw = arr, dict(b_split=split, b_layer=layer)


def _grad_out(split):
    return dict(out_dtype=BF16, out_split=split, out_layer=(0, 1))


def _with_next(res, next_gain):
    return (res[0], res[1]) if next_gain is not None else (res, None)


def mlp_fwd(tag, h, g, w_up, w_down, n=None, next_gain=None):
    if n is None:
        n = rms_fwd(tag + "_norm", h, g)
    act = mm(tag + "_up", n, w_up.arr, out_dtype=BF16, epi=lambda acc: jnp.square(jnp.maximum(acc, 0.0)), **w_up.kw)
    out, n_next = _with_next(mm(tag + "_down", act, w_down.arr, tk=LONG_TILE_K, epi=lambda acc, res: acc + res,
                                epi_tiles=(h,), norm_gain=next_gain, **w_down.kw), next_gain)
    return out, n_next, (n, act)


def mlp_bwd(tag, dh, h, g, w_up, w_down, saved, after=None):
    n, act = saved
    dh, dh16 = dh
    dup = mm(tag + "_d_act", dh16, w_down.arr, tb=True, out_dtype=BF16, after=after,
             epi=lambda acc, t: acc * (2.0 * jnp.sqrt(t.astype(F32))), epi_tiles=(act,), **w_down.kw)
    dw_down = mm(tag + "_dw_down", act, dh16, ta=True, tk=LONG_TILE_K, **_grad_out("rows"))
    dn = mm(tag + "_dn", dup, w_up.arr, tb=True, out_dtype=BF16, **w_up.kw)
    dw_up = mm(tag + "_dw_up", n, dup, ta=True, tk=LONG_TILE_K, **_grad_out("cols"))
    dh_in, dh16_in, dg = rms_bwd(tag + "_norm_bwd", dn, h, g, dh)
    return (dh_in, dh16_in), dg, (dw_up, dw_down)


def xa_fwd(tag, h, mem, g, g_mem, w_q, w_kv, w_o, n=None, next_gain=None):
    if n is None:
        n = rms_fwd(tag + "_norm", h, g)
    mem_n = rms_fwd(tag + "_mem_norm", mem, g_mem)
    q = mm(tag + "_q", n, w_q.arr, out_dtype=BF16, **w_q.kw)
    kv = mm(tag + "_kv", mem_n, w_kv.arr, out_dtype=BF16, **w_kv.kw)
    o = xa_core_fwd(tag + "_core", q, kv)
    out, n_next = _with_next(mm(tag + "_o", o, w_o.arr, epi=lambda acc, res: acc + res, epi_tiles=(h,),
                                norm_gain=next_gain, **w_o.kw), next_gain)
    return out, n_next, (n, mem_n, q, kv, o)


def xa_bwd(tag, dh, h, mem, g, g_mem, w_q, w_kv, w_o, saved):
    n, mem_n, q, kv, o = saved
    dh, dh16 = dh
    d_o = mm(tag + "_d_o", dh16, w_o.arr, tb=True, out_dtype=BF16, **w_o.kw)
    dw_o = mm(tag + "_dw_o", o, dh16, ta=True, tk=LONG_TILE_K, **_grad_out("rows"))
    dq, dkv = xa_core_bwd(tag + "_core_bwd", d_o, q, kv)
    dn = mm(tag + "_dn", dq, w_q.arr, tb=True, out_dtype=BF16, **w_q.kw)
    dw_q = mm(tag + "_dw_q", n, dq, ta=True, tk=LONG_TILE_K, **_grad_out("rows"))
    dh_in, dh16_in, dg = rms_bwd(tag + "_norm_bwd", dn, h, g, dh)
    dw_kv = mm(tag + "_dw_kv", mem_n, dkv, ta=True, **_grad_out("cols"))
    dmem_n = mm(tag + "_dmem", dkv, w_kv.arr, tb=True, **w_kv.kw)
    dg_mem = mem_norm_bwd(tag + "_mem_norm_bwd", dmem_n, mem, g_mem)
    return (dh_in, dh16_in), dg, dg_mem, (dw_q, dw_kv, dw_o)


def _gate_tile(a_log, dt_bias):
    t = jnp.zeros((8, LANES), F32)
    t = t.at[0, DN_HEADS:2 * DN_HEADS].set(a_log.reshape(-1))
    return t.at[1, DN_HEADS:2 * DN_HEADS].set(dt_bias.reshape(-1))


def dn_fwd(h, g, w_qkv, w_z, w_ba, w_conv, gate, out_norm, w_out, next_gain=None):
    n = rms_fwd("dn_norm", h, g)
    qkv_raw = mm("dn_proj_qkv", n, w_qkv)
    z = mm("dn_proj_z", n, w_z)
    ba = mm("dn_proj_ba", n, w_ba)
    qkv, hs = dn_pre(qkv_raw, ba, w_conv, gate)
    u, w, t_inv = dn_solve(qkv, hs)
    o, states = dn_scan_fwd(qkv, u, w, hs)
    og = dn_post(o, z, out_norm)
    out, n_next = _with_next(mm("dn_out", og, w_out.arr, epi=lambda acc, res: acc + res, epi_tiles=(h,),
                                norm_gain=next_gain, **w_out.kw), next_gain)
    return out, n_next, (n, qkv_raw, z, ba, qkv, hs, u, w, t_inv, o, states, og)


def dn_bwd(dh, h, g, w_qkv, w_z, w_ba, w_conv, gate, out_norm, w_out, saved, after=None):
    n, qkv_raw, z, ba, qkv, hs, u, w, t_inv, o, states, og = saved
    dh, dh16 = dh
    d_og = mm("dn_d_og", dh16, w_out.arr, tb=True, out_dtype=BF16, after=after, **w_out.kw)
    dw_out = mm("dn_dw_out", og, dh16, ta=True, tk=LONG_TILE_K, **_grad_out("rows"))
    d_o, dz, d_out_norm = dn_post_bwd(d_og, o, z, out_norm)
    dq, dk, dv, dhs = dn_scan_bwd(qkv, u, w, t_inv, hs, states, d_o)
    dc, dba, d_gate = dn_pre_bwd(dq, dk, dv, dhs, qkv_raw, ba, w_conv, gate)
    dqkv_raw, dw_conv = dn_conv_bwd(dc, qkv_raw, w_conv)
    dn = mm("dn_dn_qkv", dqkv_raw, w_qkv, tb=True)
    dn = mm("dn_dn_z", dz, w_z, tb=True, epi=lambda acc, t: acc + t, epi_tiles=(dn,))
    dn = mm("dn_dn_ba", dba, w_ba, tb=True, epi=lambda acc, t: acc + t, epi_tiles=(dn,))
    dw_qkv = mm("dn_dw_qkv", n, dqkv_raw, ta=True, tk=LONG_TILE_K)
    dw_z = mm("dn_dw_z", n, dz, ta=True, tk=LONG_TILE_K)
    dw_ba = mm("dn_dw_ba", n, dba, ta=True, tk=LONG_TILE_K)
    dh_in, _, dg = rms_bwd("dn_norm_bwd", dn, h, g, dh)
    return dh_in, dg, dw_qkv, dw_z, dw_ba, dw_conv, d_gate, d_out_norm, dw_out


def cv_fwd(h, g, w_pw1, b_pw1, w_dw, b_dw, ln_g, ln_b, w_pw2, b_pw2, n=None, next_gain=None):
    if n is None:
        n = rms_fwd("cv_norm", h, g)
    u = mm("cv_pw1", n, w_pw1.arr, epi=lambda acc, b: acc + b, epi_rows=(b_pw1,), **w_pw1.kw)
    s, c = cv_core_fwd(u, w_dw, b_dw, ln_g, ln_b)
    out, n_next = _with_next(mm("cv_pw2", s, w_pw2.arr, epi=lambda acc, res, b: acc + res + b, epi_tiles=(h,),
                                epi_rows=(b_pw2,), norm_gain=next_gain, **w_pw2.kw), next_gain)
    return out, n_next, (n, u, s, c)


def cv_bwd(dh, h, g, w_pw1, w_dw, ln_g, ln_b, w_pw2, saved):
    n, u, s, c = saved
    dh, dh16 = dh
    ds = mm("cv_d_s", dh16, w_pw2.arr, tb=True, out_dtype=BF16, **w_pw2.kw)
    dw_pw2 = mm("cv_dw_pw2", s, dh16, ta=True, tk=LONG_TILE_K, **_grad_out("rows"))
    db_pw2 = col_sum("cv_db_pw2", dh)
    dc, ln_acc = cv_ln_bwd(ds, c, ln_g, ln_b)
    du, dw_dw, db_pw1 = cv_conv_bwd(dc, u, w_dw)
    dn = mm("cv_dn", du, w_pw1.arr, tb=True, out_dtype=BF16, **w_pw1.kw)
    dw_pw1 = mm("cv_dw_pw1", n, du, ta=True, tk=LONG_TILE_K, **_grad_out("cols"))
    dh_in, dh16_in, dg = rms_bwd("cv_norm_bwd", dn, h, g, dh)
    return (dh_in, dh16_in), dg, dw_pw1, db_pw1, dw_dw, ln_acc, dw_pw2, db_pw2


WEIGHTS = ["dn_norm", "dn_w_in", "dn_w_conv", "dn_a_log", "dn_dt_bias", "dn_out_norm", "dn_w_out", "cv_norm",
           "cv_w_pw1", "cv_b_pw1", "cv_w_dw", "cv_b_dw", "cv_ln_g", "cv_ln_b", "cv_w_pw2", "cv_b_pw2", "xa_norm",
           "xa_mem_norm", "xa_w_q", "xa_w_kv", "xa_w_o", "mlp_norm", "mlp_w_up", "mlp_w_down", "final_norm"]


def _as_2d(t):
    if t.ndim == 1:
        return t.reshape(1, -1)
    return t.reshape(-1, t.shape[-1])


def kernel(x, mem, dn_norm, dn_w_in, dn_w_conv, dn_a_log, dn_dt_bias, dn_out_norm, dn_w_out, cv_norm, cv_w_pw1, cv_b_pw1, cv_w_dw, cv_b_dw, cv_ln_g, cv_ln_b, cv_w_pw2, cv_b_pw2, xa_norm, xa_mem_norm, xa_w_q, xa_w_kv, xa_w_o, mlp_norm, mlp_w_up, mlp_w_down, final_norm, loss_target, m_dn_norm, m_dn_w_in, m_dn_w_conv, m_dn_a_log, m_dn_dt_bias, m_dn_out_norm, m_dn_w_out, m_cv_norm, m_cv_w_pw1, m_cv_b_pw1, m_cv_w_dw, m_cv_b_dw, m_cv_ln_g, m_cv_ln_b, m_cv_w_pw2, m_cv_b_pw2, m_xa_norm, m_xa_mem_norm, m_xa_w_q, m_xa_w_kv, m_xa_w_o, m_mlp_norm, m_mlp_w_up, m_mlp_w_down, m_final_norm, v_dn_norm, v_dn_w_in, v_dn_w_conv, v_dn_a_log, v_dn_dt_bias, v_dn_out_norm, v_dn_w_out, v_cv_norm, v_cv_w_pw1, v_cv_b_pw1, v_cv_w_dw, v_cv_b_dw, v_cv_ln_g, v_cv_ln_b, v_cv_w_pw2, v_cv_b_pw2, v_xa_norm, v_xa_mem_norm, v_xa_w_q, v_xa_w_kv, v_xa_w_o, v_mlp_norm, v_mlp_w_up, v_mlp_w_down, v_final_norm):
    args = dict(locals())
    wts = {nm: args[nm] for nm in WEIGHTS}
    mom = {nm: args["m_" + nm] for nm in WEIGHTS}
    var = {nm: args["v_" + nm] for nm in WEIGHTS}
    core = lax.axis_index("c").astype(jnp.int32).reshape(1)
    chip = (2 * lax.axis_index("x") + lax.axis_index("y")).astype(jnp.int32)
    def own_slab(got, src):
        return lax.dynamic_update_slice(got, src[None], (chip, 0, 0))

    shard2d = {nm: wts[nm].astype(BF16).reshape(-1, wts[nm].shape[-1]) for nm in BIG}
    first = ["dn_w_in", "dn_w_out"]
    later = [nm for nm in BIG if nm not in first]
    sources = [shard2d[nm] for nm in first] + [_pack(wts, SMALL, F32, SMALL_ROW_MULTIPLE)[0]]
    gathered = [own_slab(got, src) for got, src in zip(gather_shards(sources), sources)]
    stacked = {"dn_w_out": gathered[1].reshape((N_CHIPS,) + SHARD_SHAPES["dn_w_out"])}
    full = {nm: _to_full(nm, t) for nm, t in _unpack(gathered[2], SMALL).items()}
    full["dn_w_in"] = _to_full("dn_w_in", gathered[0].reshape((N_CHIPS,) + SHARD_SHAPES["dn_w_in"]))
    full.update({nm: wts[nm] for nm in REPLICATED})
    later_src = [shard2d[nm] for nm in later]
    g_send, g_recv, later_src, g_lands, started = ici_start(
        "gather_start", "gather", later_src, [(N_CHIPS,) + s.shape for s in later_src])
    full["dn_norm"] = full["dn_norm"] + started[0, 0]

    def rest_weights(after):
        srcs, lands = ici_wait("gather_wait", "gather", g_send, g_recv, later_src, g_lands, after)
        return {nm: own_slab(land, src).reshape((N_CHIPS,) + SHARD_SHAPES[nm])
                for nm, land, src in zip(later, pair_forward(lands), srcs)}

    pending = []

    def on_grads(items):
        tag = "_".join(sorted({str(layer) for _, layer, _ in items}))
        parts = [g.reshape(N_CHIPS, -1, g.shape[-1]) for _, _, g in items]
        theirs = pair_split("pair_split_" + tag, parts)
        pairs = [add_pairs("pair_add_%s%d" % (nm, layer), p, t, core, BF16)
                 for (nm, layer, _), p, t in zip(items, parts, theirs)]
        send, recv, pairs, lands, token = ici_start(
            "scatter_start_" + tag, "scatter", pairs, [(N_CHIPS - 1,) + p.shape[1:] for p in pairs])
        pending.append((tag, items, send, recv, pairs, lands))
        return token

    dh, grads, rep = local_step(x[0], mem[0], loss_target[0], stacked, full, rest_weights, on_grads)

    halves = {}
    last = [("dn_w_in", 0, _to_shards("dn_w_in", grads["dn_w_in"]).astype(BF16)), ("dn_w_out", 0, grads["dn_w_out"][0]),
            ("small", 0, _pack({nm: _to_shards(nm, grads[nm]) for nm in SMALL}, SMALL, F32, SMALL_ROW_MULTIPLE))]
    parts = [g.reshape(N_CHIPS, -1, g.shape[-1]) for _, _, g in last]
    theirs = pair_split("pair_split_last", parts)
    pairs = [add_pairs("pair_add_" + nm, p, t, core, p.dtype) for (nm, _, _), p, t in zip(last, parts, theirs)]
    l_send, l_recv, l_pairs, l_lands, l_started = ici_start(
        "scatter_start_last", "scatter", pairs, [(N_CHIPS - 1,) + p.shape[1:] for p in pairs])
    for tag, items, send, recv, pairs, lands in pending:
        pairs, lands = ici_wait("scatter_wait_" + tag, "scatter", send, recv, pairs, lands, l_started)
        for (nm, layer, _), p, o in zip(items, pairs, lands):
            halves[nm, layer] = add_four("chip_add_%s%d" % (nm, layer), p, o, chip.reshape(1))
    keys = sorted(halves)
    siblings = dict(zip(keys, pair_join("pair_join_early", [halves[k] for k in keys])))

    delta, new_m, new_v, red = {}, {}, {}, {}

    def big_adamw(nm):
        layers = range(wts[nm].shape[0])
        res = adamw_halves("adamw_" + nm, _as_2d(wts[nm]), [halves[nm, l] for l in layers],
                           [siblings[nm, l] for l in layers], _as_2d(mom[nm]), _as_2d(var[nm]), core)
        red[nm], delta[nm], new_m[nm], new_v[nm] = (r.reshape(wts[nm].shape) for r in res)

    early = [nm for nm in BIG if (nm, 0) in halves]
    for nm in early:
        big_adamw(nm)
    done = jnp.concatenate([new_v[nm].reshape(-1)[:1] for nm in early])
    l_pairs, l_lands = ici_wait("scatter_wait_last", "scatter", l_send, l_recv, l_pairs, l_lands, done)
    for (nm, layer, _), p, o in zip(last, l_pairs, l_lands):
        halves[nm, layer] = add_four("chip_add_" + nm, p, o, chip.reshape(1))
    keys = [(nm, layer) for nm, layer, _ in last]
    siblings.update(zip(keys, pair_join("pair_join_last", [halves[k] for k in keys])))
    south = core[0] == 0
    mine, theirs = halves["small", 0], siblings["small", 0]
    red.update(_unpack(jnp.concatenate([jnp.where(south, mine, theirs), jnp.where(south, theirs, mine)], axis=0),
                       SMALL))

    rep = all_sum_small(rep)
    red["dn_norm"] = rep[0:1]
    red["dn_a_log"] = rep[1:2, DN_HEADS:2 * DN_HEADS]
    red["dn_dt_bias"] = rep[2:3, DN_HEADS:2 * DN_HEADS]
    red["dn_out_norm"] = rep[3:4, :LANES]
    red["xa_norm"], red["xa_mem_norm"], red["mlp_norm"] = rep[4:6], rep[6:8], rep[8:10]
    red["final_norm"] = rep[10]
    loss = rep[11, 0]

    for nm in WEIGHTS:
        shp = wts[nm].shape
        if nm in early:
            continue
        if nm in BIG:
            big_adamw(nm)
            continue
        res = adamw("adamw_" + nm, _as_2d(wts[nm]), _as_2d(red[nm].reshape(shp)), _as_2d(mom[nm]), _as_2d(var[nm]))
        delta[nm], new_m[nm], new_v[nm] = (r.reshape(shp) for r in res)
        red[nm] = red[nm].reshape(shp)

    grad_x = dh[None]
    return (loss, grad_x, *[red[nm] for nm in WEIGHTS], *[delta[nm] for nm in WEIGHTS],
            *[new_m[nm] for nm in WEIGHTS], *[new_v[nm] for nm in WEIGHTS])


def local_step(h0, mem0, target, stacked, full, rest_weights=None, on_grads=None):
    d = h0.shape[1]
    dn_norm, dn_a_log, dn_dt_bias, dn_out_norm = (full[nm] for nm in REPLICATED[:4])
    xa_norm, xa_mem_norm, mlp_norm, final_norm = (full[nm] for nm in REPLICATED[4:])
    inner = DN_HEADS * DN_HEAD_DIM
    w_in = full["dn_w_in"][0]
    w_qkv, w_z = w_in[:, :3 * inner], w_in[:, 3 * inner:4 * inner]
    w_ba = jnp.pad(w_in[:, 4 * inner:], ((0, 0), (0, LANES - 2 * DN_HEADS)))
    w_conv = jnp.pad(full["dn_w_conv"][0], ((0, 8 - DN_CONV), (0, 0)))
    gate = _gate_tile(dn_a_log, dn_dt_bias)
    w_dw = jnp.pad(full["cv_w_dw"][0], ((0, CV_HALO - CV_WIDTH), (0, 0)))

    def sw(nm, layer):
        return Stacked(stacked[nm], "rows" if SHARD_AXIS[nm] == 1 else "cols", layer)

    dn_args = (_row(dn_norm), w_qkv, w_z, w_ba, w_conv, gate, _row(dn_out_norm), sw("dn_w_out", 0))
    h1, n, dn_saved = dn_fwd(h0, *dn_args, next_gain=_row(xa_norm[0]))
    if rest_weights is not None:
        stacked = {**stacked, **rest_weights(h1)}
    xa_args = [(_row(xa_norm[l]), _row(xa_mem_norm[l]), sw("xa_w_q", l), sw("xa_w_kv", l), sw("xa_w_o", l))
               for l in range(2)]
    mlp_args = [(_row(mlp_norm[l]), sw("mlp_w_up", l), sw("mlp_w_down", l)) for l in range(2)]
    cv_args = (_row(full["cv_norm"][0]), sw("cv_w_pw1", 0), full["cv_b_pw1"], w_dw, full["cv_b_dw"],
               full["cv_ln_g"], full["cv_ln_b"], sw("cv_w_pw2", 0), full["cv_b_pw2"])
    h2, n, xa0_saved = xa_fwd("xa0", h1, mem0, *xa_args[0], n=n, next_gain=mlp_args[0][0])
    h3, n, mlp0_saved = mlp_fwd("mlp0", h2, *mlp_args[0], n=n, next_gain=cv_args[0])
    h4, n, cv_saved = cv_fwd(h3, *cv_args, n=n, next_gain=xa_args[1][0])
    h5, n, xa1_saved = xa_fwd("xa1", h4, mem0, *xa_args[1], n=n, next_gain=mlp_args[1][0])
    h6, _, mlp1_saved = mlp_fwd("mlp1", h5, *mlp_args[1], n=n)

    dh32, dh16, loss_tile, d_final = loss_head("loss_head", h6, _row(final_norm), target)
    dh = (dh32, dh16)
    grads = {}
    dg_mlp, dg_xa, dg_xa_mem = [None, None], [None, None], [None, None]
    dw_mlp, dw_xa = [None, None], [None, None]
    mlp_names, xa_names = ("mlp_w_up", "mlp_w_down"), ("xa_w_q", "xa_w_kv", "xa_w_o")

    def announce(items):
        return None if on_grads is None else on_grads(items)

    dh, dg_mlp[1], dw_mlp[1] = mlp_bwd("mlp1", dh, h5, *mlp_args[1], mlp1_saved)
    dh, dg_xa[1], dg_xa_mem[1], dw_xa[1] = xa_bwd("xa1", dh, h4, mem0, *xa_args[1], xa1_saved)
    (dh, grads["cv_norm"], dw_pw1, grads["cv_b_pw1"], dw_dw, ln_acc, dw_pw2,
     grads["cv_b_pw2"]) = cv_bwd(dh, h3, cv_args[0], cv_args[1], w_dw, cv_args[5], cv_args[6], cv_args[7], cv_saved)
    after = announce([(nm, 1, g) for nm, g in zip(mlp_names + xa_names, dw_mlp[1] + dw_xa[1])]
                     + [("cv_w_pw1", 0, dw_pw1), ("cv_w_pw2", 0, dw_pw2)])
    dh, dg_mlp[0], dw_mlp[0] = mlp_bwd("mlp0", dh, h2, *mlp_args[0], mlp0_saved, after=after)
    dh, dg_xa[0], dg_xa_mem[0], dw_xa[0] = xa_bwd("xa0", dh, h1, mem0, *xa_args[0], xa0_saved)
    after = announce([(nm, 0, g) for nm, g in zip(mlp_names + xa_names, dw_mlp[0] + dw_xa[0])])
    dh, dg_dn, dw_qkv, dw_z, dw_ba, dw_conv, d_gate, d_out_norm, dw_out = dn_bwd(dh, h0, *dn_args, dn_saved,
                                                                                 after=after)

    grads["dn_w_in"] = jnp.concatenate([dw_qkv, dw_z, dw_ba[:, :2 * DN_HEADS]], axis=1)[None]
    grads["dn_w_conv"] = dw_conv[None, :DN_CONV]
    grads["dn_w_out"], grads["cv_w_pw1"], grads["cv_w_pw2"] = [dw_out], [dw_pw1], [dw_pw2]
    grads["cv_w_dw"] = dw_dw[None, :CV_WIDTH]
    grads["cv_ln_g"], grads["cv_ln_b"], grads["cv_b_dw"] = ln_acc[0:1], ln_acc[1:2], ln_acc[2:3]
    for i, nm in enumerate(mlp_names):
        grads[nm] = [dw_mlp[0][i], dw_mlp[1][i]]
    for i, nm in enumerate(xa_names):
        grads[nm] = [dw_xa[0][i], dw_xa[1][i]]

    rep = jnp.zeros((16, d), F32)
    rep = rep.at[0].set(dg_dn[0])
    rep = rep.at[1, :LANES].set(d_gate[0])
    rep = rep.at[2, :LANES].set(d_gate[1])
    rep = rep.at[3, :LANES].set(d_out_norm[0])
    rep = rep.at[4].set(dg_xa[0][0]).at[5].set(dg_xa[1][0])
    rep = rep.at[6].set(dg_xa_mem[0][0]).at[7].set(dg_xa_mem[1][0])
    rep = rep.at[8].set(dg_mlp[0][0]).at[9].set(dg_mlp[1][0])
    rep = rep.at[10].set(d_final[0])
    rep = rep.at[11, :LANES].set(loss_tile[0])
    return dh, grads, rep
```

```python
import functools

import jax
import jax.numpy as jnp
from jax import lax
from jax.experimental import pallas as pl
from jax.experimental.pallas import tpu as pltpu

F32 = jnp.float32
BF16 = jnp.bfloat16
HIGHEST = lax.Precision.HIGHEST
MESH = pl.DeviceIdType.MESH

D_MODEL = 1024
DN_HEADS = 8
DN_HEAD_DIM = 128
DN_CONV = 4
DN_CHUNK = 64
CV_WIDTH = 31
XA_HEADS = 4
XA_HEAD_DIM = 256
RMS_EPS = 1e-6
LN_EPS = 1e-5
L2_EPS = 1e-6

ADAM_LR = 0.001
ADAM_B1 = 0.9
ADAM_B2 = 0.999
ADAM_EPS = 1e-08
ADAM_WD = 0.01
ADAM_STEP = 10

LANES = 128
ROW_TILE = 512
CONV_ROW_TILE = 256
MM_TILE = 1024
LONG_TILE_K = 2048
ADAMW_ROW_TILE = 256
DN_ROW_TILE = 256
CHUNK_SHIFT = 6
SOLVE_INTERLEAVE = 8
FWD_HEADS_PER_STEP = 8
BWD_HEADS_PER_STEP = 8
BWD_SCAN_ROWS = 256
DN_HALO = 8
CV_HALO = 32
VMEM_LIMIT = 48 * 1024 * 1024
N_CHIPS = 4
D2D_CHUNK_ROWS = 256


def _cparams(sem):
    return pltpu.CompilerParams(dimension_semantics=sem, vmem_limit_bytes=VMEM_LIMIT)


def _dot(a, b, dims=(((1,), (0,)), ((), ()))):
    return lax.dot_general(a.astype(BF16), b.astype(BF16), dims, preferred_element_type=F32)


def _dot_nt(a, b):
    return _dot(a, b, (((1,), (1,)), ((), ())))


def _dot_tn(a, b):
    return _dot(a, b, (((0,), (0,)), ((), ())))


def _dot_hi(a, b, dims=(((1,), (0,)), ((), ()))):
    return lax.dot_general(a.astype(F32), b.astype(F32), dims, precision=HIGHEST, preferred_element_type=F32)


def _dot_x3(a, b, dims=(((1,), (0,)), ((), ()))):
    a_hi, b_hi = a.astype(BF16), b.astype(BF16)
    a_lo = (a - a_hi.astype(F32)).astype(BF16)
    b_lo = (b - b_hi.astype(F32)).astype(BF16)

    def dot(p, q):
        return lax.dot_general(p, q, dims, preferred_element_type=F32)

    return dot(a_hi, b_hi) + (dot(a_hi, b_lo) + dot(a_lo, b_hi))


def _sigmoid(x):
    return 1.0 / (1.0 + jnp.exp(-x))


def _silu(x):
    return x * _sigmoid(x)


def _silu_grad(x):
    s = _sigmoid(x)
    return s * (1.0 + x * (1.0 - s))


def _softplus(x):
    return jnp.maximum(x, 0.0) + jnp.log(1.0 + jnp.exp(-jnp.abs(x)))


def _iota(shape, dim):
    return lax.broadcasted_iota(jnp.int32, shape, dim)


def _lane_col(vals, lane, idx):
    return jnp.sum(jnp.where(lane == idx, vals, 0.0), axis=1, keepdims=True)


def _pick_tile(rows, cap):
    best = rows
    for t in range(16, min(rows, cap) + 1, 16):
        if rows % t == 0:
            best = t
    return best


def _stacked_spec(shape, split, layer, rows, cols, block_index):
    r_shard, c_shard = shape[-2], shape[-1]
    if split == "rows" and rows > r_shard:
        assert rows % r_shard == 0 and c_shard % cols == 0
        chips = rows // r_shard

        def slabs(i, j, kk):
            bi, bj = block_index(i, j, kk)
            return (bi, layer, 0, bj)

        return pl.BlockSpec((chips, None, r_shard, cols), slabs), chips
    assert r_shard % rows == 0 and c_shard % cols == 0
    per_chip = (r_shard // rows) if split == "rows" else (c_shard // cols)

    def index(i, j, kk):
        bi, bj = block_index(i, j, kk)
        if split == "rows":
            return (bi // per_chip, layer, bi % per_chip, bj)
        return (bj // per_chip, layer, bi, bj % per_chip)

    return pl.BlockSpec((None, None, rows, cols), index), 1


def mm(name, a, b, *, ta=False, tb=False, out_dtype=F32, pro=None, epi=None, epi_tiles=(), epi_rows=(),
       tm=MM_TILE, tn=MM_TILE, tk=MM_TILE, b_split=None, b_layer=None, out_split=None, out_layer=None,
       after=None, norm_gain=None):
    m, k = (a.shape[1], a.shape[0]) if ta else a.shape
    b_rows, b_cols = b.shape[-2], b.shape[-1]
    if b_split == "rows":
        b_rows *= N_CHIPS
    elif b_split == "cols":
        b_cols *= N_CHIPS
    n = b_rows if tb else b_cols
    assert (b_cols if tb else b_rows) == k
    tm, tn, tk = min(tm, m), min(tn, n), min(tk, k)
    if b_split == "cols":
        if tb:
            tk = min(tk, b.shape[-1])
        else:
            tn = min(tn, b.shape[-1])
    if out_split == "cols":
        tn = min(tn, n // N_CHIPS)
    assert m % tm == 0 and n % tn == 0 and k % tk == 0
    nk = k // tk
    a_spec = pl.BlockSpec((tk, tm), lambda i, j, kk: (kk, i)) if ta else pl.BlockSpec((tm, tk), lambda i, j, kk: (i, kk))
    b_block = (tn, tk) if tb else (tk, tn)
    b_index = (lambda i, j, kk: (j, kk)) if tb else (lambda i, j, kk: (kk, j))
    b_chips = o_chips = 1
    if b_split is None:
        b_spec = pl.BlockSpec(b_block, b_index)
    else:
        b_spec, b_chips = _stacked_spec(b.shape, b_split, b_layer, b_block[0], b_block[1], b_index)
    in_specs = [a_spec, b_spec]
    in_specs += [pl.BlockSpec((tm, tn), lambda i, j, kk: (i, j)) for _ in epi_tiles]
    in_specs += [pl.BlockSpec((1, tn), lambda i, j, kk: (0, j)) for _ in epi_rows]
    n_t, n_r = len(epi_tiles), len(epi_rows)
    dims = (((0 if ta else 1,), (1 if tb else 0,)), ((), ()))
    if out_split is None:
        out_shape = jax.ShapeDtypeStruct((m, n), out_dtype)
        out_spec = pl.BlockSpec((tm, tn), lambda i, j, kk: (i, j))
    else:
        shard = (m // N_CHIPS, n) if out_split == "rows" else (m, n // N_CHIPS)
        out_shape = jax.ShapeDtypeStruct((N_CHIPS, out_layer[1]) + shard, out_dtype)
        out_spec, o_chips = _stacked_spec(out_shape.shape, out_split, out_layer[0], tm, tn, lambda i, j, kk: (i, j))
    single_pass = nk == 1 and norm_gain is None and b_chips == 1 and o_chips == 1
    extra = []
    if norm_gain is not None:
        assert tn == n and out_split is None
        extra.append(norm_gain)
        in_specs.append(pl.BlockSpec((1, n), lambda i, j, kk: (0, 0)))
        out_shape = [out_shape, jax.ShapeDtypeStruct((m, n), BF16)]
        out_spec = [out_spec, pl.BlockSpec((tm, tn), lambda i, j, kk: (i, j))]
    if after is not None:
        extra.append(after)
        in_specs.append(pl.BlockSpec(memory_space=pl.ANY))

    def body(a_ref, b_ref, *rest):
        tiles = rest[:n_t]
        rows = rest[n_t:n_t + n_r]
        gain_ref = rest[n_t + n_r] if norm_gain is not None else None
        rest = rest[n_t + n_r + len(extra):]
        o_ref, acc_ref = rest[0], rest[-1]
        av = a_ref[...]
        if pro is not None:
            av = pro(av)
        if single_pass:
            out = _dot(av, b_ref[...], dims)
            if epi is not None:
                out = epi(out, *[t[...] for t in tiles], *[r[...] for r in rows])
            o_ref[...] = out.astype(out_dtype)
            return
        kk = pl.program_id(2)

        @pl.when(kk == 0)
        def _():
            acc_ref[...] = jnp.zeros_like(acc_ref)

        bv = b_ref[...]
        if b_chips > 1:
            bv = bv.reshape(b_block)
        acc_ref[...] += _dot(av, bv, dims)

        @pl.when(kk == nk - 1)
        def _():
            out = acc_ref[...]
            if epi is not None:
                out = epi(out, *[t[...] for t in tiles], *[r[...] for r in rows])
            if gain_ref is not None:
                rest[1][...] = (_rms_stats(out)[0] * gain_ref[...]).astype(BF16)
            out = out.astype(out_dtype)
            o_ref[...] = out.reshape(o_chips, tm // o_chips, tn) if o_chips > 1 else out

    return pl.pallas_call(
        body, name=name, grid=(m // tm, n // tn, nk),
        in_specs=in_specs, out_specs=out_spec, out_shape=out_shape,
        scratch_shapes=[] if single_pass else [pltpu.VMEM((tm, tn), F32)],
        compiler_params=_cparams(("parallel", "parallel", "arbitrary")),
    )(a, b, *epi_tiles, *epi_rows, *extra)


def row_call(name, body, n_rows, tm, ins, outs, accs=()):
    tm = _pick_tile(n_rows, tm)
    in_specs = []
    for arr, kind in ins:
        if kind == "tile":
            if arr.ndim == 2:
                in_specs.append(pl.BlockSpec((tm, arr.shape[1]), lambda i: (i, 0)))
            else:
                in_specs.append(pl.BlockSpec((arr.shape[0], tm, arr.shape[2]), lambda i: (0, i, 0)))
        elif kind == "full":
            in_specs.append(pl.BlockSpec(arr.shape, functools.partial(lambda i, nd: (0,) * nd, nd=arr.ndim)))
        else:
            where, h = kind
            per = tm // h
            if where == "prev":
                in_specs.append(pl.BlockSpec((h, arr.shape[1]), functools.partial(
                    lambda i, per: (jnp.maximum(i * per - 1, 0), 0), per=per)))
            else:
                last = n_rows // h - 1
                in_specs.append(pl.BlockSpec((h, arr.shape[1]), functools.partial(
                    lambda i, per, last: (jnp.minimum((i + 1) * per, last), 0), per=per, last=last)))
    out_shape, out_specs = [], []
    for shape, dtype in outs:
        out_shape.append(jax.ShapeDtypeStruct(shape, dtype))
        if len(shape) == 2:
            out_specs.append(pl.BlockSpec((tm, shape[1]), lambda i: (i, 0)))
        else:
            out_specs.append(pl.BlockSpec((shape[0], tm, shape[2]), lambda i: (0, i, 0)))
    for shape in accs:
        out_shape.append(jax.ShapeDtypeStruct(shape, F32))
        out_specs.append(pl.BlockSpec(shape, lambda i: (0, 0)))
    n_in, n_out, n_acc = len(ins), len(outs), len(accs)

    def kern(*refs):
        i = pl.program_id(0)
        in_refs = refs[:n_in]
        out_refs = refs[n_in:n_in + n_out]
        acc_refs = refs[n_in + n_out:n_in + n_out + n_acc]
        if n_acc:
            @pl.when(i == 0)
            def _():
                for r in acc_refs:
                    r[...] = jnp.zeros_like(r)
        body(i, in_refs, out_refs, acc_refs)

    res = pl.pallas_call(
        kern, name=name, grid=(n_rows // tm,), in_specs=in_specs, out_specs=out_specs, out_shape=out_shape,
        compiler_params=_cparams(("arbitrary",) if n_acc else ("parallel",)),
    )(*[a for a, _ in ins])
    return list(res)


def _rms_stats(h):
    r = lax.rsqrt(jnp.mean(h * h, axis=-1, keepdims=True) + RMS_EPS)
    return h * r, r


def rms_fwd(name, h, g):
    def body(i, ins, outs, accs):
        xhat, _ = _rms_stats(ins[0][...])
        outs[0][...] = (xhat * ins[1][...]).astype(BF16)

    return row_call(name, body, h.shape[0], ROW_TILE, [(h, "tile"), (g, "full")], [(h.shape, BF16)])[0]


def _rms_bwd_tile(dn, h, g):
    xhat, r = _rms_stats(h)
    dxhat = dn * g
    dh = r * (dxhat - xhat * jnp.mean(dxhat * xhat, axis=-1, keepdims=True))
    dg = jnp.sum(dn * xhat, axis=0, keepdims=True)
    return dh, dg


def rms_bwd(name, dn, h, g, dres):
    def body(i, ins, outs, accs):
        dh, dg = _rms_bwd_tile(ins[0][...].astype(F32), ins[1][...], ins[2][...])
        total = ins[3][...] + dh
        outs[0][...] = total
        outs[1][...] = total.astype(BF16)
        accs[0][...] += dg

    d = h.shape[1]
    out, out16, dg = row_call(name, body, h.shape[0], ROW_TILE,
                              [(dn, "tile"), (h, "tile"), (g, "full"), (dres, "tile")],
                              [(h.shape, F32), (h.shape, BF16)], [(1, d)])
    return out, out16, dg


def mem_norm_bwd(name, dn, mem, g):
    def body(i, ins, outs, accs):
        _, dg = _rms_bwd_tile(ins[0][...].astype(F32), ins[1][...], ins[2][...])
        accs[0][...] += dg

    return row_call(name, body, mem.shape[0], ROW_TILE, [(dn, "tile"), (mem, "tile"), (g, "full")], [],
                    [(1, mem.shape[1])])[0]


def loss_head(name, h, g, target):
    d = h.shape[1]

    def body(i, ins, outs, accs):
        hv, gv = ins[0][...], ins[1][...]
        xhat, _ = _rms_stats(hv)
        err = xhat * gv - ins[2][...]
        dy = err * (1.0 / d)
        dh, dg = _rms_bwd_tile(dy, hv, gv)
        outs[0][...] = dh
        outs[1][...] = dh.astype(BF16)
        accs[0][...] += jnp.full((8, LANES), 0.5 / d, F32) * jnp.sum(err * err)
        accs[1][...] += dg

    dh, dh16, loss, dg = row_call(name, body, h.shape[0], ROW_TILE, [(h, "tile"), (g, "full"), (target, "tile")],
                                  [(h.shape, F32), (h.shape, BF16)], [(8, LANES), (1, d)])
    return dh, dh16, loss, dg


def col_sum(name, x):
    def body(i, ins, outs, accs):
        accs[0][...] += jnp.sum(ins[0][...].astype(F32), axis=0, keepdims=True)

    return row_call(name, body, x.shape[0], ROW_TILE, [(x, "tile")], [], [(1, x.shape[1])])[0]


def _conv_taps(xcat, w_ref, cols, width, halo, tm):
    rows = halo + tm
    acc = None
    for j in range(width):
        s = width - 1 - j
        xs = xcat if s == 0 else pltpu.roll(xcat, s, 0)
        term = xs[halo:rows] * w_ref[j:j + 1, cols]
        acc = term if acc is None else acc + term
    return acc


def _conv_taps_bwd_x(dcat, w_ref, cols, width, halo, tm):
    rows = halo + tm
    acc = None
    for j in range(width):
        s = width - 1 - j
        ds = dcat if s == 0 else pltpu.roll(dcat, rows - s, 0)
        term = ds[0:tm] * w_ref[j:j + 1, cols]
        acc = term if acc is None else acc + term
    return acc


def _conv_taps_bwd_w(dy, xcat, width, halo, tm, wrows):
    rows = halo + tm
    rid = _iota((wrows, dy.shape[1]), 0)
    out = jnp.zeros((wrows, dy.shape[1]), F32)
    for j in range(width):
        s = width - 1 - j
        xs = xcat if s == 0 else pltpu.roll(xcat, s, 0)
        v = jnp.sum(dy * xs[halo:rows], axis=0, keepdims=True)
        out = out + jnp.where(rid == j, v, 0.0)
    return out


def dn_pre(qkv_raw, ba, w_conv, gate):
    s_len = qkv_raw.shape[0]
    tm = min(DN_ROW_TILE, s_len)
    n_blk = qkv_raw.shape[1] // LANES

    def body(i, ins, outs, accs):
        x_ref, xp_ref, ba_ref, w_ref, gate_ref = ins
        qkv_ref, hs_ref = outs

        def blk(cb, carry):
            cols = pl.ds(pl.multiple_of(cb * LANES, LANES), LANES)
            prev = jnp.where(i > 0, xp_ref[:, cols], 0.0)
            xcat = jnp.concatenate([prev, x_ref[:, cols]], axis=0)
            c = _conv_taps(xcat, w_ref, cols, DN_CONV, DN_HALO, tm)
            y = _silu(c)
            rs = lax.rsqrt(jnp.sum(y * y, axis=-1, keepdims=True) + L2_EPS)
            fac = jnp.where(cb < DN_HEADS, DN_HEAD_DIM ** -0.5, 1.0)
            qkv_ref[:, cols] = jnp.where(cb < 2 * DN_HEADS, y * (rs * fac), y)
            return carry

        lax.fori_loop(0, n_blk, blk, 0)

        bav = ba_ref[...]
        beta = _sigmoid(bav)
        g = -jnp.exp(gate_ref[0:1, :]) * _softplus(bav + gate_ref[1:2, :])
        lane = _iota((tm, LANES), 1)
        g = jnp.where((lane >= DN_HEADS) & (lane < 2 * DN_HEADS), g, 0.0)
        r = _iota((tm, tm), 0)
        c = _iota((tm, tm), 1)
        tri = jnp.where((r >= c) & ((r >> CHUNK_SHIFT) == (c >> CHUNK_SHIFT)), 1.0, 0.0)
        gc = _dot_hi(tri, g)
        for h in range(DN_HEADS):
            hs_ref[h] = jnp.where(lane == 0, _lane_col(beta, lane, h),
                                  jnp.where(lane == 1, _lane_col(g, lane, DN_HEADS + h),
                                            jnp.where(lane == 2, _lane_col(gc, lane, DN_HEADS + h), 0.0)))

    return row_call("dn_pre", body, s_len, tm,
                    [(qkv_raw, "tile"), (qkv_raw, ("prev", DN_HALO)), (ba, "tile"), (w_conv, "full"), (gate, "full")],
                    [(qkv_raw.shape, F32), ((DN_HEADS, s_len, LANES), F32)])


def _chunk_masks():
    r = _iota((DN_CHUNK, DN_CHUNK), 0)
    c = _iota((DN_CHUNK, DN_CHUNK), 1)
    return r, c


def _decay_matrix(gc, r, c):
    gc_row = jnp.sum(jnp.where(r == c, gc, 0.0), axis=0, keepdims=True)
    causal = r >= c
    return jnp.where(causal, jnp.exp(jnp.where(causal, gc - gc_row, 0.0)), 0.0)


def _tri_inverse(lows, r, c):
    eye = jnp.where(r == c, 1.0, 0.0)
    ts = [eye for _ in lows]
    b = 1
    while b < DN_CHUNK:
        shift = b.bit_length()
        sel = ((r >> shift) == (c >> shift)) & ((r & b) != 0) & ((c & b) == 0)
        lms = [jnp.where(sel, low, 0.0) for low in lows]
        if b == 1:
            ts = [t - lm for t, lm in zip(ts, lms)]
        else:
            t_lm = [_dot_x3(t, lm) for t, lm in zip(ts, lms)]
            t_lm_t = [_dot_x3(x, t) for x, t in zip(t_lm, ts)]
            ts = [t - x for t, x in zip(ts, t_lm_t)]
        b *= 2
    return ts


def dn_solve(qkv, hs):
    s_len = qkv.shape[0]
    rb = min(ROW_TILE, s_len)
    n_chunk = rb // DN_CHUNK
    interleave = min(SOLVE_INTERLEAVE, n_chunk)

    def body(k_ref, v_ref, hs_ref, u_ref, w_ref, t_ref):
        r, c = _chunk_masks()

        def group(gi, carry):
            rows = [pl.ds(pl.multiple_of((gi * interleave + j) * DN_CHUNK, DN_CHUNK), DN_CHUNK)
                    for j in range(interleave)]
            k = [k_ref[rw, :] for rw in rows]
            beta = [hs_ref[rw, 0:1] for rw in rows]
            gc = [hs_ref[rw, 2:3] for rw in rows]
            kb = [a * b for a, b in zip(k, beta)]
            decay = [_decay_matrix(g, r, c) for g in gc]
            lows = [jnp.where(r > c, _dot_nt(a, b) * d, 0.0) for a, b, d in zip(kb, k, decay)]
            ts = _tri_inverse(lows, r, c)
            us = [_dot_x3(t, v_ref[rw, :] * b) for t, rw, b in zip(ts, rows, beta)]
            ws = [_dot_x3(t, a * jnp.exp(g)) for t, a, g in zip(ts, kb, gc)]
            for j, rw in enumerate(rows):
                u_ref[rw, :] = us[j]
                w_ref[rw, :] = ws[j].astype(BF16)
                t_ref[rw, :] = ts[j]
            return carry

        lax.fori_loop(0, n_chunk // interleave, group, 0)

    return pl.pallas_call(
        body, name="dn_solve", grid=(DN_HEADS, s_len // rb),
        in_specs=[pl.BlockSpec((rb, LANES), lambda h, i: (i, DN_HEADS + h)),
                  pl.BlockSpec((rb, LANES), lambda h, i: (i, 2 * DN_HEADS + h)),
                  pl.BlockSpec((None, rb, LANES), lambda h, i: (h, i, 0))],
        out_specs=[pl.BlockSpec((rb, LANES), lambda h, i: (i, h)),
                   pl.BlockSpec((rb, LANES), lambda h, i: (i, h)),
                   pl.BlockSpec((None, rb, DN_CHUNK), lambda h, i: (h, i, 0))],
        out_shape=[jax.ShapeDtypeStruct((s_len, DN_HEADS * LANES), F32),
                   jax.ShapeDtypeStruct((s_len, DN_HEADS * LANES), BF16),
                   jax.ShapeDtypeStruct((DN_HEADS, s_len, DN_CHUNK), F32)],
        compiler_params=_cparams(("parallel", "parallel")),
    )(qkv, qkv, hs)


def dn_scan_fwd(qkv, u, w, hs):
    s_len = qkv.shape[0]
    rb = min(ROW_TILE, s_len)
    n_chunk = rb // DN_CHUNK
    total_chunks = s_len // DN_CHUNK

    hps = FWD_HEADS_PER_STEP
    groups = DN_HEADS // hps

    def body(q_ref, k_ref, u_ref, w_ref, hs_ref, o_ref, st_ref, state):
        @pl.when(pl.program_id(1) == 0)
        def _():
            state[...] = jnp.zeros_like(state)

        r, c = _chunk_masks()

        def chunk(n, carry):
            rows = pl.ds(pl.multiple_of(n * DN_CHUNK, DN_CHUNK), DN_CHUNK)
            heads = range(hps)
            cols = [slice(h * LANES, (h + 1) * LANES) for h in heads]
            each = lambda f, *xs: [f(*a) for a in zip(*xs)]
            q = [q_ref[rows, cl] for cl in cols]
            k = [k_ref[rows, cl] for cl in cols]
            gc = [hs_ref[h, rows, 2:3] for h in heads]
            st = [state[h] for h in heads]
            for h in heads:
                st_ref[h, n] = st[h]
            gl = each(lambda g: jnp.min(g, axis=0, keepdims=True), gc)
            decay = each(lambda g: _decay_matrix(g, r, c), gc)
            w_st = [_dot(w_ref[rows, cols[h]], st[h]) for h in heads]
            qk = each(_dot_nt, q, k)
            q_st = each(lambda a, g, s: _dot(a * jnp.exp(g), s), q, gc, st)
            vn = [u_ref[rows, cols[h]] - w_st[h] for h in heads]
            ai_vn = each(lambda a, d, b: _dot(a * d, b), qk, decay, vn)
            kd_vn = each(lambda a, g0, g, b: _dot_tn(a * jnp.exp(g0 - g), b), k, gl, gc, vn)
            for h in heads:
                o_ref[rows, cols[h]] = q_st[h] + ai_vn[h]
                state[h] = st[h] * jnp.exp(gl[h]) + kd_vn[h]
            return carry

        lax.fori_loop(0, n_chunk, chunk, 0)

    wide = hps * LANES
    blk = lambda off: pl.BlockSpec((rb, wide), lambda h, i: (i, off + h))
    return pl.pallas_call(
        body, name="dn_scan_fwd", grid=(groups, s_len // rb),
        in_specs=[blk(0), blk(groups), blk(0), blk(0),
                  pl.BlockSpec((hps, rb, LANES), lambda h, i: (h, i, 0))],
        out_specs=[blk(0),
                   pl.BlockSpec((hps, n_chunk, LANES, LANES), lambda h, i: (h, i, 0, 0))],
        out_shape=[jax.ShapeDtypeStruct((s_len, DN_HEADS * LANES), F32),
                   jax.ShapeDtypeStruct((DN_HEADS, total_chunks, LANES, LANES), F32)],
        scratch_shapes=[pltpu.VMEM((hps, LANES, LANES), F32)],
        compiler_params=_cparams(("parallel", "arbitrary")),
    )(qkv, qkv, u, w, hs)


def dn_scan_bwd(qkv, u, w, t_inv, hs, states, d_o):
    s_len = qkv.shape[0]
    rb = min(BWD_SCAN_ROWS, s_len)
    n_chunk = rb // DN_CHUNK
    n_blk = s_len // rb
    hps = BWD_HEADS_PER_STEP
    groups = DN_HEADS // hps

    def body(q_ref, k_ref, v_ref, u_ref, w_ref, t_ref, hs_ref, st_ref, do_ref,
             dq_ref, dk_ref, dv_ref, dhs_ref, dstate):
        @pl.when(pl.program_id(1) == 0)
        def _():
            dstate[...] = jnp.zeros_like(dstate)

        r, c = _chunk_masks()
        causal = r >= c
        strict = r > c
        lane = _iota((DN_CHUNK, LANES), 1)
        upper = jnp.where(r <= c, 1.0, 0.0)
        last_row = _iota((DN_CHUNK, 1), 0) == DN_CHUNK - 1

        def chunk(m, carry):
            n = n_chunk - 1 - m
            rows = pl.ds(pl.multiple_of(n * DN_CHUNK, DN_CHUNK), DN_CHUNK)
            heads = range(hps)
            cols = [slice(h * LANES, (h + 1) * LANES) for h in heads]
            each = lambda f, *xs: [f(*a) for a in zip(*xs)]
            rsum = lambda x: jnp.sum(x, axis=-1, keepdims=True)
            dims_tn = (((0,), (0,)), ((), ()))
            q = [q_ref[rows, cl] for cl in cols]
            k = [k_ref[rows, cl] for cl in cols]
            v = [v_ref[rows, cl] for cl in cols]
            uu = [u_ref[rows, cl] for cl in cols]
            ww = [w_ref[rows, cl] for cl in cols]
            do = [do_ref[rows, cl] for cl in cols]
            tt = [t_ref[h, rows, :] for h in heads]
            beta = [hs_ref[h, rows, 0:1] for h in heads]
            gc = [hs_ref[h, rows, 2:3] for h in heads]
            st = [st_ref[h, n] for h in heads]
            dst = [dstate[h] for h in heads]
            gl = each(lambda g: jnp.min(g, axis=0, keepdims=True), gc)
            egc = each(jnp.exp, gc)
            egl = each(jnp.exp, gl)
            ekd = each(lambda a, b: jnp.exp(a - b), gl, gc)
            decay = each(lambda g: _decay_matrix(g, r, c), gc)
            qd = each(jnp.multiply, q, egc)
            kd = each(jnp.multiply, k, ekd)
            kb = each(jnp.multiply, k, beta)
            w_st = each(_dot, ww, st)
            qk = each(_dot_nt, q, k)
            dqd = each(_dot_nt, do, st)
            kd_dst = each(_dot, kd, dst)
            qd_do = each(_dot_tn, qd, do)
            kbk = each(_dot_nt, kb, k)
            vn = each(jnp.subtract, uu, w_st)
            ai = each(jnp.multiply, qk, decay)
            low = each(lambda a, d: jnp.where(strict, a * d, 0.0), kbk, decay)
            dai = each(lambda a, b: jnp.where(causal, _dot_nt(a, b), 0.0), do, vn)
            ai_do = each(_dot_tn, ai, do)
            dkd = each(_dot_nt, vn, dst)
            dvn = each(jnp.add, ai_do, kd_dst)
            dp = each(jnp.multiply, dai, decay)
            dw = each(lambda a, b: -_dot_nt(a, b), dvn, st)
            w_dvn = each(_dot_tn, ww, dvn)
            dp_k = each(_dot, dp, k)
            dp_q = each(_dot_tn, dp, q)
            drhs_u = each(lambda a, b: _dot_x3(a, b, dims_tn), tt, dvn)
            dgl = each(lambda a, b, e: jnp.sum(a * b) * e, dst, st, egl)
            for h in heads:
                dstate[h] = dst[h] * egl[h] + qd_do[h] - w_dvn[h]
            dq = each(lambda a, e, b: a * e + b, dqd, egc, dp_k)
            dk_a = each(lambda a, e, b: a * e + b, dkd, ekd, dp_q)
            rkd = each(lambda a, b: rsum(a * b), dkd, kd)
            drhs_w = each(lambda a, b: _dot_x3(a, b, dims_tn), tt, dw)
            dl_u = each(_dot_nt, drhs_u, uu)
            dl_w = each(_dot_nt, drhs_w, ww)
            dlow = each(lambda a, b: jnp.where(strict, -(a + b), 0.0), dl_u, dl_w)
            dqm = each(jnp.multiply, dlow, decay)
            m_tot = each(lambda a, b, d, e: a * b + d * e, dai, ai, dlow, low)
            dqm_k = each(_dot, dqm, k)
            dk_l = each(_dot_tn, dqm, kb)
            col_rows = each(lambda m: jnp.sum(m, axis=0, keepdims=True), m_tot)
            col_sums = each(lambda rw: jnp.sum(jnp.where(r == c, rw, 0.0), axis=1, keepdims=True), col_rows)
            dkb_w = each(jnp.multiply, drhs_w, egc)
            dkb = each(jnp.add, dkb_w, dqm_k)
            dgc = [rsum(dqd[h] * qd[h]) - rkd[h] + jnp.where(last_row, jnp.sum(rkd[h]) + dgl[h], 0.0)
                   + rsum(m_tot[h]) + rsum(dkb_w[h] * kb[h]) for h in heads]
            dg = each(lambda a, b: _dot_hi(upper, jnp.where(lane == 1, a - b, 0.0)), dgc, col_sums)
            for h in heads:
                dq_ref[rows, cols[h]] = dq[h]
                dk_ref[rows, cols[h]] = dk_a[h] + dk_l[h] + dkb[h] * beta[h]
                dv_ref[rows, cols[h]] = drhs_u[h] * beta[h]
                dbeta = rsum(drhs_u[h] * v[h]) + rsum(dkb[h] * k[h])
                dhs_ref[h, rows, :] = jnp.where(lane == 0, dbeta, dg[h])
            return carry

        lax.fori_loop(0, n_chunk, chunk, 0)

    wide = hps * LANES
    blk = lambda off: pl.BlockSpec((rb, wide), lambda h, i: (n_blk - 1 - i, off + h))
    head = blk(0)
    hs_spec = pl.BlockSpec((hps, rb, LANES), lambda h, i: (h, n_blk - 1 - i, 0))
    full = jax.ShapeDtypeStruct((s_len, DN_HEADS * LANES), F32)
    return pl.pallas_call(
        body, name="dn_scan_bwd", grid=(groups, n_blk),
        in_specs=[blk(0), blk(groups), blk(2 * groups), head, head,
                  pl.BlockSpec((hps, rb, DN_CHUNK), lambda h, i: (h, n_blk - 1 - i, 0)), hs_spec,
                  pl.BlockSpec((hps, n_chunk, LANES, LANES), lambda h, i: (h, n_blk - 1 - i, 0, 0)), head],
        out_specs=[head, head, head, hs_spec],
        out_shape=[full, full, full, jax.ShapeDtypeStruct((DN_HEADS, s_len, LANES), F32)],
        scratch_shapes=[pltpu.VMEM((hps, LANES, LANES), F32)],
        compiler_params=_cparams(("parallel", "arbitrary")),
    )(qkv, qkv, qkv, u, w, t_inv, hs, states, d_o)


def dn_post(o, z, out_norm):
    def body(i, ins, outs, accs):
        gn = ins[2][...]
        for h in range(DN_HEADS):
            cols = slice(h * LANES, (h + 1) * LANES)
            xhat, _ = _rms_stats(ins[0][:, cols])
            outs[0][:, cols] = (xhat * gn * _silu(ins[1][:, cols])).astype(BF16)

    return row_call("dn_post", body, o.shape[0], ROW_TILE, [(o, "tile"), (z, "tile"), (out_norm, "full")],
                    [(o.shape, BF16)])[0]


def dn_post_bwd(d_og, o, z, out_norm):
    def body(i, ins, outs, accs):
        gn = ins[3][...]
        dgn = jnp.zeros((1, LANES), F32)
        for h in range(DN_HEADS):
            cols = slice(h * LANES, (h + 1) * LANES)
            dy, zh = ins[0][:, cols].astype(F32), ins[2][:, cols]
            xhat, r = _rms_stats(ins[1][:, cols])
            sz = _silu(zh)
            dgn = dgn + jnp.sum(dy * xhat * sz, axis=0, keepdims=True)
            outs[1][:, cols] = (dy * xhat * gn * _silu_grad(zh)).astype(BF16)
            dxhat = dy * gn * sz
            outs[0][:, cols] = r * (dxhat - xhat * jnp.mean(dxhat * xhat, axis=-1, keepdims=True))
        accs[0][...] += dgn

    return row_call("dn_post_bwd", body, o.shape[0], ROW_TILE,
                    [(d_og, "tile"), (o, "tile"), (z, "tile"), (out_norm, "full")],
                    [(o.shape, F32), (o.shape, BF16)], [(1, LANES)])


def dn_pre_bwd(dq, dk, dv, dhs, qkv_raw, ba, w_conv, gate):
    s_len = qkv_raw.shape[0]
    tm = min(DN_ROW_TILE, s_len)

    def body(i, ins, outs, accs):
        dq_ref, dk_ref, dv_ref, dhs_ref, x_ref, xp_ref, ba_ref, w_ref, gate_ref = ins
        dc_ref, dba_ref = outs

        def blk(cb, carry):
            cols = pl.ds(pl.multiple_of(cb * LANES, LANES), LANES)
            hcols = pl.ds(pl.multiple_of((cb & (DN_HEADS - 1)) * LANES, LANES), LANES)
            prev = jnp.where(i > 0, xp_ref[:, cols], 0.0)
            xcat = jnp.concatenate([prev, x_ref[:, cols]], axis=0)
            c = _conv_taps(xcat, w_ref, cols, DN_CONV, DN_HALO, tm)
            y = _silu(c)
            dy = jnp.where(cb < DN_HEADS, dq_ref[:, hcols],
                           jnp.where(cb < 2 * DN_HEADS, dk_ref[:, hcols], dv_ref[:, hcols]))
            rs = lax.rsqrt(jnp.sum(y * y, axis=-1, keepdims=True) + L2_EPS)
            fac = jnp.where(cb < DN_HEADS, DN_HEAD_DIM ** -0.5, 1.0)
            nrm = y * rs
            dn = dy * fac
            dy_norm = rs * (dn - nrm * jnp.sum(dn * nrm, axis=-1, keepdims=True))
            dc_ref[:, cols] = jnp.where(cb < 2 * DN_HEADS, dy_norm, dy) * _silu_grad(c)
            return carry

        lax.fori_loop(0, qkv_raw.shape[1] // LANES, blk, 0)

        lane = _iota((tm, LANES), 1)
        dbeta = jnp.zeros((tm, LANES), F32)
        dg = jnp.zeros((tm, LANES), F32)
        for h in range(DN_HEADS):
            dbeta = dbeta + jnp.where(lane == h, dhs_ref[h, :, 0:1], 0.0)
            dg = dg + jnp.where(lane == DN_HEADS + h, dhs_ref[h, :, 1:2], 0.0)
        bav = ba_ref[...]
        beta = _sigmoid(bav)
        ea = jnp.exp(gate_ref[0:1, :])
        pre = bav + gate_ref[1:2, :]
        g = -ea * _softplus(pre)
        da = dg * (-ea) * _sigmoid(pre)
        dba_ref[...] = (dbeta * beta * (1.0 - beta) + da).astype(BF16)
        rid = _iota((8, LANES), 0)
        accs[0][...] += (jnp.where(rid == 0, jnp.sum(dg * g, axis=0, keepdims=True), 0.0)
                         + jnp.where(rid == 1, jnp.sum(da, axis=0, keepdims=True), 0.0))

    return row_call("dn_pre_bwd", body, s_len, tm,
                    [(dq, "tile"), (dk, "tile"), (dv, "tile"), (dhs, "tile"), (qkv_raw, "tile"),
                     (qkv_raw, ("prev", DN_HALO)), (ba, "tile"), (w_conv, "full"), (gate, "full")],
                    [(qkv_raw.shape, F32), (ba.shape, BF16)], [(8, LANES)])


def dn_conv_bwd(dc, qkv_raw, w_conv):
    s_len = dc.shape[0]
    tm = min(DN_ROW_TILE, s_len)
    nt = s_len // tm

    def body(i, ins, outs, accs):
        dc_ref, dn_ref, x_ref, xp_ref, w_ref = ins

        def blk(cb, carry):
            cols = pl.ds(pl.multiple_of(cb * LANES, LANES), LANES)
            dy = dc_ref[:, cols]
            nxt = jnp.where(i < nt - 1, dn_ref[:, cols], 0.0)
            dcat = jnp.concatenate([dy, nxt], axis=0)
            outs[0][:, cols] = _conv_taps_bwd_x(dcat, w_ref, cols, DN_CONV, DN_HALO, tm).astype(BF16)
            prev = jnp.where(i > 0, xp_ref[:, cols], 0.0)
            xcat = jnp.concatenate([prev, x_ref[:, cols]], axis=0)
            accs[0][:, cols] += _conv_taps_bwd_w(dy, xcat, DN_CONV, DN_HALO, tm, 8)
            return carry

        lax.fori_loop(0, dc.shape[1] // LANES, blk, 0)

    return row_call("dn_conv_bwd", body, s_len, tm,
                    [(dc, "tile"), (dc, ("next", DN_HALO)), (qkv_raw, "tile"), (qkv_raw, ("prev", DN_HALO)),
                     (w_conv, "full")],
                    [(dc.shape, BF16)], [(8, dc.shape[1])])


def _glu(u_ref, cols, d):
    return u_ref[:, cols] * _sigmoid(u_ref[:, pl.ds(pl.multiple_of(d + cols.start, LANES), cols.size)])


def cv_core_fwd(u, w_dw, b_dw, ln_g, ln_b):
    s_len, d = u.shape[0], u.shape[1] // 2
    tm = min(CONV_ROW_TILE, s_len)

    def body(i, ins, outs, accs):
        u_ref, up_ref, w_ref, bdw_ref, g_ref, b_ref = ins
        s_ref, c_ref = outs

        def blk(cb, carry):
            cols = pl.ds(pl.multiple_of(cb * LANES, LANES), LANES)
            prev = jnp.where(i > 0, _glu(up_ref, cols, d), 0.0)
            xcat = jnp.concatenate([prev, _glu(u_ref, cols, d)], axis=0)
            c_ref[:, cols] = _conv_taps(xcat, w_ref, cols, CV_WIDTH, CV_HALO, tm) + bdw_ref[:, cols]
            return carry

        lax.fori_loop(0, d // LANES, blk, 0)
        c = c_ref[...]
        mu = jnp.mean(c, axis=-1, keepdims=True)
        xc = c - mu
        rstd = lax.rsqrt(jnp.mean(xc * xc, axis=-1, keepdims=True) + LN_EPS)
        s_ref[...] = _silu(xc * rstd * g_ref[...] + b_ref[...]).astype(BF16)

    return row_call("cv_core_fwd", body, s_len, tm,
                    [(u, "tile"), (u, ("prev", CV_HALO)), (w_dw, "full"), (b_dw, "full"), (ln_g, "full"),
                     (ln_b, "full")],
                    [((s_len, d), BF16), ((s_len, d), F32)])


def cv_ln_bwd(ds, c, ln_g, ln_b):
    def body(i, ins, outs, accs):
        cv, g = ins[1][...], ins[2][...]
        mu = jnp.mean(cv, axis=-1, keepdims=True)
        xc = cv - mu
        rstd = lax.rsqrt(jnp.mean(xc * xc, axis=-1, keepdims=True) + LN_EPS)
        xhat = xc * rstd
        dl = ins[0][...].astype(F32) * _silu_grad(xhat * g + ins[3][...])
        dxhat = dl * g
        dc = rstd * (dxhat - jnp.mean(dxhat, axis=-1, keepdims=True)
                     - xhat * jnp.mean(dxhat * xhat, axis=-1, keepdims=True))
        outs[0][...] = dc
        rid = _iota((8, cv.shape[1]), 0)
        accs[0][...] += (jnp.where(rid == 0, jnp.sum(dl * xhat, axis=0, keepdims=True), 0.0)
                         + jnp.where(rid == 1, jnp.sum(dl, axis=0, keepdims=True), 0.0)
                         + jnp.where(rid == 2, jnp.sum(dc, axis=0, keepdims=True), 0.0))

    return row_call("cv_ln_bwd", body, c.shape[0], ROW_TILE,
                    [(ds, "tile"), (c, "tile"), (ln_g, "full"), (ln_b, "full")], [(c.shape, F32)], [(8, c.shape[1])])


def cv_conv_bwd(dc, u, w_dw):
    s_len, d = dc.shape
    tm = min(CONV_ROW_TILE, s_len)
    nt = s_len // tm

    def body(i, ins, outs, accs):
        dc_ref, dn_ref, u_ref, up_ref, w_ref = ins

        def blk(cb, carry):
            cols = pl.ds(pl.multiple_of(cb * LANES, LANES), LANES)
            gcols = pl.ds(pl.multiple_of(d + cb * LANES, LANES), LANES)
            dy = dc_ref[:, cols]
            nxt = jnp.where(i < nt - 1, dn_ref[:, cols], 0.0)
            dgl = _conv_taps_bwd_x(jnp.concatenate([dy, nxt], axis=0), w_ref, cols, CV_WIDTH, CV_HALO, tm)
            u1, sg = u_ref[:, cols], _sigmoid(u_ref[:, gcols])
            du1 = dgl * sg
            du2 = dgl * u1 * sg * (1.0 - sg)
            outs[0][:, cols] = du1.astype(BF16)
            outs[0][:, gcols] = du2.astype(BF16)
            accs[1][:, cols] += jnp.sum(du1, axis=0, keepdims=True)
            accs[1][:, gcols] += jnp.sum(du2, axis=0, keepdims=True)
            prev = jnp.where(i > 0, _glu(up_ref, cols, d), 0.0)
            xcat = jnp.concatenate([prev, u1 * sg], axis=0)
            accs[0][:, cols] += _conv_taps_bwd_w(dy, xcat, CV_WIDTH, CV_HALO, tm, CV_HALO)
            return carry

        lax.fori_loop(0, d // LANES, blk, 0)

    return row_call("cv_conv_bwd", body, s_len, tm,
                    [(dc, "tile"), (dc, ("next", CV_HALO)), (u, "tile"), (u, ("prev", CV_HALO)), (w_dw, "full")],
                    [(u.shape, BF16)], [(CV_HALO, d), (1, 2 * d)])


def xa_core_fwd(name, q, kv):
    d = q.shape[1]

    def body(i, ins, outs, accs):
        for h in range(XA_HEADS):
            cols = slice(h * XA_HEAD_DIM, (h + 1) * XA_HEAD_DIM)
            vcols = slice(d + h * XA_HEAD_DIM, d + (h + 1) * XA_HEAD_DIM)
            s = _dot_nt(ins[0][:, cols], ins[1][:, cols]) * (XA_HEAD_DIM ** -0.5)
            e = jnp.exp(s - jnp.max(s, axis=-1, keepdims=True))
            p = e / jnp.sum(e, axis=-1, keepdims=True)
            outs[0][:, cols] = _dot(p, ins[1][:, vcols]).astype(BF16)

    return row_call(name, body, q.shape[0], ROW_TILE, [(q, "tile"), (kv, "full")], [(q.shape, BF16)])[0]


def xa_core_bwd(name, d_o, q, kv):
    d = q.shape[1]

    def body(i, ins, outs, accs):
        for h in range(XA_HEADS):
            cols = slice(h * XA_HEAD_DIM, (h + 1) * XA_HEAD_DIM)
            vcols = slice(d + h * XA_HEAD_DIM, d + (h + 1) * XA_HEAD_DIM)
            qh, kh, vh, doh = ins[1][:, cols], ins[2][:, cols], ins[2][:, vcols], ins[0][:, cols]
            s = _dot_nt(qh, kh) * (XA_HEAD_DIM ** -0.5)
            e = jnp.exp(s - jnp.max(s, axis=-1, keepdims=True))
            p = e / jnp.sum(e, axis=-1, keepdims=True)
            dp = _dot_nt(doh, vh)
            ds = p * (dp - jnp.sum(dp * p, axis=-1, keepdims=True)) * (XA_HEAD_DIM ** -0.5)
            outs[0][:, cols] = _dot(ds, kh).astype(BF16)
            accs[0][:, cols] += _dot_tn(ds, qh)
            accs[0][:, vcols] += _dot_tn(p, doh)

    return row_call(name, body, q.shape[0], ROW_TILE, [(d_o, "tile"), (q, "tile"), (kv, "full")],
                    [(q.shape, BF16)], [kv.shape])


def adamw(name, w, g, m, v):
    def body(i, ins, outs, accs):
        wv, gv = ins[0][...], ins[1][...]
        mn = ADAM_B1 * ins[2][...] + (1.0 - ADAM_B1) * gv
        vn = ADAM_B2 * ins[3][...] + (1.0 - ADAM_B2) * jnp.square(gv)
        m_hat = mn / (1.0 - ADAM_B1 ** ADAM_STEP)
        v_hat = vn / (1.0 - ADAM_B2 ** ADAM_STEP)
        outs[0][...] = -ADAM_LR * (m_hat / (jnp.sqrt(v_hat) + ADAM_EPS) + ADAM_WD * wv)
        outs[1][...] = mn
        outs[2][...] = vn

    return row_call(name, body, w.shape[0], ROW_TILE, [(w, "tile"), (g, "tile"), (m, "tile"), (v, "tile")],
                    [(w.shape, F32)] * 3)


def adamw_halves(name, w, g_mine, g_sibling, m, v, core):
    n_layers = len(g_mine)
    rows, cols = w.shape
    half_rows = rows // n_layers // 2
    tm = _pick_tile(half_rows, ADAMW_ROW_TILE)
    per_half = half_rows // tm

    def body(core_ref, w_ref, *rest):
        g_refs = rest[:2 * n_layers]
        m_ref, v_ref, g_out, d_out, m_out, v_out = rest[2 * n_layers:]
        i = pl.program_id(0)
        mine = ((i // per_half) % 2) == core_ref[0]
        layer = i // (2 * per_half)
        gv = jnp.where(mine, g_refs[0][...], g_refs[n_layers][...])
        for l in range(1, n_layers):
            gv = jnp.where(layer == l, jnp.where(mine, g_refs[l][...], g_refs[n_layers + l][...]), gv)
        mn = ADAM_B1 * m_ref[...] + (1.0 - ADAM_B1) * gv
        vn = ADAM_B2 * v_ref[...] + (1.0 - ADAM_B2) * jnp.square(gv)
        m_hat = mn / (1.0 - ADAM_B1 ** ADAM_STEP)
        v_hat = vn / (1.0 - ADAM_B2 ** ADAM_STEP)
        g_out[...] = gv
        d_out[...] = -ADAM_LR * (m_hat / (jnp.sqrt(v_hat) + ADAM_EPS) + ADAM_WD * w_ref[...])
        m_out[...] = mn
        v_out[...] = vn

    whole = pl.BlockSpec((tm, cols), lambda i, core_ref: (i, 0))
    half = pl.BlockSpec((tm, cols), lambda i, core_ref: (i % per_half, 0))
    return pl.pallas_call(
        body, name=name,
        grid_spec=pltpu.PrefetchScalarGridSpec(
            num_scalar_prefetch=1, grid=(2 * per_half * n_layers,),
            in_specs=[whole] + [half] * (2 * n_layers) + [whole, whole], out_specs=[whole] * 4),
        out_shape=[jax.ShapeDtypeStruct(w.shape, F32)] * 4,
        compiler_params=_cparams(("parallel",)),
    )(core, w, *g_mine, *g_sibling, m, v)


HBM_SPEC = pl.BlockSpec(memory_space=pltpu.HBM)


def _position():
    return lax.axis_index("x"), lax.axis_index("y"), lax.axis_index("c")


def _other_chips(x, y):
    return [(1 - x, y), (x, 1 - y), (1 - x, 1 - y)]


def _row_chunks(rows):
    return rows // D2D_CHUNK_ROWS if rows % D2D_CHUNK_ROWS == 0 else 1


def _start_chunked(make, rows):
    k = _row_chunks(rows)
    for i in range(k):
        make(i * (rows // k), rows // k).start()


def gather_shards(packs):
    n = len(packs)

    def body(*refs):
        srcs, outs = refs[:n], refs[n:2 * n]
        send_sems, recv_sems = refs[2 * n:]
        x, y, c = _position()
        me = 2 * x + y
        chips = _other_chips(x, y)
        sibling = (x, y, 1 - c)

        def over_ici(a, j):
            px, py = chips[j]
            rows = srcs[a].shape[0] // 2
            return pltpu.make_async_remote_copy(
                src_ref=srcs[a].at[pl.ds(c * rows, rows), :], dst_ref=outs[a].at[me, pl.ds(c * rows, rows), :],
                send_sem=send_sems.at[a, j], recv_sem=recv_sems.at[a, j], device_id=(px, py, c), device_id_type=MESH)

        def landed(a, j):
            px, py = chips[j]
            rows = srcs[a].shape[0] // 2
            part = outs[a].at[2 * px + py, pl.ds(c * rows, rows), :]
            return pltpu.make_async_remote_copy(
                src_ref=part, dst_ref=part, send_sem=send_sems.at[a, j], recv_sem=recv_sems.at[a, j],
                device_id=(px, py, c), device_id_type=MESH)

        def over_d2d(a, j, cc, off, size):
            px, py = chips[j]
            rows = srcs[a].shape[0] // 2
            part = outs[a].at[2 * px + py, pl.ds(cc * rows + off, size), :]
            return pltpu.make_async_remote_copy(
                src_ref=part, dst_ref=part, send_sem=send_sems.at[a, 3 + j], recv_sem=recv_sems.at[a, 3 + j],
                device_id=sibling, device_id_type=MESH)

        for a in range(n):
            for j in range(3):
                over_ici(a, j).start()
        for a in range(n):
            for j in range(3):
                landed(a, j).wait_recv()
                _start_chunked(functools.partial(over_d2d, a, j, c), srcs[a].shape[0] // 2)
        for a in range(n):
            rows = srcs[a].shape[0] // 2
            for j in range(3):
                over_d2d(a, j, 1 - c, 0, rows).wait_recv()
                over_d2d(a, j, c, 0, rows).wait_send()
                over_ici(a, j).wait_send()

    return pl.pallas_call(
        body, name="gather_shards",
        in_specs=[HBM_SPEC] * n, out_specs=[HBM_SPEC] * n,
        out_shape=[jax.ShapeDtypeStruct((N_CHIPS,) + p.shape, p.dtype) for p in packs],
        scratch_shapes=[pltpu.SemaphoreType.DMA((n, 6)), pltpu.SemaphoreType.DMA((n, 6))],
    )(*packs)


def pair_split(name, packs):
    n = len(packs)

    def body(*refs):
        srcs, outs = refs[:n], refs[n:2 * n]
        send_sems, recv_sems = refs[2 * n:]
        x, y, c = _position()

        def remote(a, off, size):
            rows = srcs[a].shape[1] // 2
            return pltpu.make_async_remote_copy(
                src_ref=srcs[a].at[:, pl.ds((1 - c) * rows + off, size), :],
                dst_ref=outs[a].at[:, pl.ds(off, size), :],
                send_sem=send_sems.at[a], recv_sem=recv_sems.at[a], device_id=(x, y, 1 - c), device_id_type=MESH)

        for a in range(n):
            _start_chunked(functools.partial(remote, a), srcs[a].shape[1] // 2)
        for a in range(n):
            remote(a, 0, srcs[a].shape[1] // 2).wait()

    return pl.pallas_call(
        body, name=name, in_specs=[HBM_SPEC] * n, out_specs=[HBM_SPEC] * n,
        out_shape=[jax.ShapeDtypeStruct((p.shape[0], p.shape[1] // 2, p.shape[2]), p.dtype) for p in packs],
        scratch_shapes=[pltpu.SemaphoreType.DMA((n,)), pltpu.SemaphoreType.DMA((n,))],
    )(*packs)


def pair_join(name, halves):
    n = len(halves)

    def body(*refs):
        srcs, outs = refs[:n], refs[n:2 * n]
        send_sems, recv_sems = refs[2 * n:]
        x, y, c = _position()

        def remote(a, off, size):
            return pltpu.make_async_remote_copy(
                src_ref=srcs[a].at[pl.ds(off, size), :], dst_ref=outs[a].at[pl.ds(off, size), :],
                send_sem=send_sems.at[a], recv_sem=recv_sems.at[a], device_id=(x, y, 1 - c), device_id_type=MESH)

        for a in range(n):
            _start_chunked(functools.partial(remote, a), srcs[a].shape[0])
        for a in range(n):
            remote(a, 0, srcs[a].shape[0]).wait()

    return pl.pallas_call(
        body, name=name, in_specs=[HBM_SPEC] * n, out_specs=[HBM_SPEC] * n,
        out_shape=[jax.ShapeDtypeStruct(p.shape, p.dtype) for p in halves],
        scratch_shapes=[pltpu.SemaphoreType.DMA((n,)), pltpu.SemaphoreType.DMA((n,))],
    )(*halves)


SEM_SPEC = pl.BlockSpec(memory_space=pltpu.SEMAPHORE)
DATAFLOW = pltpu.SideEffectType.DATAFLOW_SIDE_EFFECTING


def _ici_copy(kind, srcs, lands, send_sems, recv_sems, a, j):
    x, y, c = _position()
    px, py = _other_chips(x, y)[j]
    if kind == "gather":
        rows = srcs[a].shape[0] // 2
        src = srcs[a].at[pl.ds(c * rows, rows), :]
        dst = lands[a].at[2 * x + y, pl.ds(c * rows, rows), :]
    else:
        src = srcs[a].at[2 * px + py]
        dst = lands[a].at[j]
    return pltpu.make_async_remote_copy(src_ref=src, dst_ref=dst, send_sem=send_sems, recv_sem=recv_sems,
                                        device_id=(px, py, c), device_id_type=MESH)


def ici_start(name, kind, srcs, land_shapes):
    n = len(srcs)
    lands = [pltpu.with_memory_space_constraint(lax.empty(shp, s.dtype), pltpu.HBM) for shp, s in zip(land_shapes, srcs)]

    def body(*refs):
        src_refs, land_refs = refs[:n], refs[n:2 * n]
        send_sems, recv_sems = refs[2 * n], refs[2 * n + 1]
        token = refs[-1]
        for a in range(n):
            for j in range(N_CHIPS - 1):
                _ici_copy(kind, src_refs, land_refs, send_sems, recv_sems, a, j).start()
        token[...] = jnp.zeros_like(token)

    sems = pltpu.SemaphoreType.DMA(())
    res = pl.pallas_call(
        body, name=name,
        out_shape=[sems, sems] + [pltpu.HBM(s.shape, s.dtype) for s in srcs]
        + [pltpu.HBM(l.shape, l.dtype) for l in lands] + [jax.ShapeDtypeStruct((8, LANES), F32)],
        in_specs=[HBM_SPEC] * (2 * n),
        out_specs=[SEM_SPEC, SEM_SPEC] + [HBM_SPEC] * (2 * n) + [pl.BlockSpec(memory_space=pltpu.VMEM)],
        input_output_aliases={i: 2 + i for i in range(2 * n)},
        compiler_params=pltpu.CompilerParams(has_side_effects=DATAFLOW),
    )(*[pltpu.with_memory_space_constraint(s, pltpu.HBM) for s in srcs], *lands)
    return res[0], res[1], list(res[2:2 + n]), list(res[2 + n:2 + 2 * n]), res[-1]


def ici_wait(name, kind, send_sems, recv_sems, srcs, lands, after):
    n = len(srcs)

    def body(*refs):
        src_refs, land_refs = refs[:n], refs[n:2 * n]
        send, recv = refs[2 * n], refs[2 * n + 1]
        for a in range(n):
            for j in range(N_CHIPS - 1):
                cp = _ici_copy(kind, src_refs, land_refs, send, recv, a, j)
                cp.wait_send()
                cp.wait_recv()

    res = pl.pallas_call(
        body, name=name,
        out_shape=[pltpu.HBM(s.shape, s.dtype) for s in srcs] + [pltpu.HBM(l.shape, l.dtype) for l in lands],
        in_specs=[HBM_SPEC] * (2 * n) + [SEM_SPEC, SEM_SPEC, pl.BlockSpec(memory_space=pl.ANY)],
        out_specs=[HBM_SPEC] * (2 * n),
        input_output_aliases={i: i for i in range(2 * n)},
        compiler_params=pltpu.CompilerParams(has_side_effects=DATAFLOW),
    )(*srcs, *lands, send_sems, recv_sems, after)
    return list(res[:n]), list(res[n:])


def pair_forward(gathered):
    n = len(gathered)

    def body(*refs):
        outs = refs[n:2 * n]
        send_sems, recv_sems = refs[2 * n:]
        x, y, c = _position()
        chips = _other_chips(x, y)

        def part(a, j, cc, off, size):
            px, py = chips[j]
            rows = outs[a].shape[1] // 2
            ref = outs[a].at[2 * px + py, pl.ds(cc * rows + off, size), :]
            return pltpu.make_async_remote_copy(
                src_ref=ref, dst_ref=ref, send_sem=send_sems.at[a, j], recv_sem=recv_sems.at[a, j],
                device_id=(x, y, 1 - c), device_id_type=MESH)

        for a in range(n):
            for j in range(N_CHIPS - 1):
                _start_chunked(functools.partial(part, a, j, c), outs[a].shape[1] // 2)
        for a in range(n):
            rows = outs[a].shape[1] // 2
            for j in range(N_CHIPS - 1):
                part(a, j, 1 - c, 0, rows).wait_recv()
                part(a, j, c, 0, rows).wait_send()

    return pl.pallas_call(
        body, name="pair_forward", in_specs=[HBM_SPEC] * n, out_specs=[HBM_SPEC] * n,
        out_shape=[jax.ShapeDtypeStruct(g.shape, g.dtype) for g in gathered],
        input_output_aliases={i: i for i in range(n)},
        scratch_shapes=[pltpu.SemaphoreType.DMA((n, N_CHIPS - 1)), pltpu.SemaphoreType.DMA((n, N_CHIPS - 1))],
    )(*gathered)


def all_sum_small(part):
    n_dev = 8
    rows = part.shape[0]

    def body(src, out, buf, send_sems, recv_sems):
        x, y, c = _position()
        me = 4 * x + 2 * y + c
        buf[me] = src[...]
        copies = []
        for k in range(1, n_dev):
            px, py, pc = x ^ ((k >> 2) & 1), y ^ ((k >> 1) & 1), c ^ (k & 1)
            cp = pltpu.make_async_remote_copy(
                src_ref=src, dst_ref=buf.at[me], send_sem=send_sems.at[k - 1], recv_sem=recv_sems.at[k - 1],
                device_id=(px, py, pc), device_id_type=MESH)
            cp.start()
            copies.append(cp)
        for cp in copies:
            cp.wait()
        acc = buf[0]
        for k in range(1, n_dev):
            acc = acc + buf[k]
        out[...] = acc

    return pl.pallas_call(
        body, name="all_sum_small",
        in_specs=[pl.BlockSpec(memory_space=pltpu.VMEM)], out_specs=pl.BlockSpec(memory_space=pltpu.VMEM),
        out_shape=jax.ShapeDtypeStruct(part.shape, F32),
        scratch_shapes=[pltpu.VMEM((n_dev, rows, part.shape[1]), F32),
                        pltpu.SemaphoreType.DMA((n_dev - 1,)), pltpu.SemaphoreType.DMA((n_dev - 1,))],
    )(part)


def add_pairs(name, src, theirs, core, out_dtype):
    slabs, rows, cols = theirs.shape
    tm = _pick_tile(rows, ROW_TILE)
    nb = rows // tm

    def body(core_ref, a_ref, b_ref, o_ref):
        o_ref[...] = (a_ref[...].astype(F32) + b_ref[...].astype(F32)).astype(out_dtype)

    return pl.pallas_call(
        body, name=name,
        grid_spec=pltpu.PrefetchScalarGridSpec(
            num_scalar_prefetch=1, grid=(slabs, nb),
            in_specs=[pl.BlockSpec((None, tm, cols), lambda s, i, core_ref: (s, core_ref[0] * nb + i, 0)),
                      pl.BlockSpec((None, tm, cols), lambda s, i, core_ref: (s, i, 0))],
            out_specs=pl.BlockSpec((None, tm, cols), lambda s, i, core_ref: (s, i, 0))),
        out_shape=jax.ShapeDtypeStruct(theirs.shape, out_dtype),
        compiler_params=_cparams(("parallel", "parallel")),
    )(core, src, theirs)


def add_four(name, src, theirs, chip):
    _, rows, cols = theirs.shape
    tm = _pick_tile(rows, ROW_TILE)

    def body(chip_ref, a_ref, b_ref, o_ref):
        acc = a_ref[...].astype(F32)
        for j in range(N_CHIPS - 1):
            acc = acc + b_ref[j].astype(F32)
        o_ref[...] = acc

    return pl.pallas_call(
        body, name=name,
        grid_spec=pltpu.PrefetchScalarGridSpec(
            num_scalar_prefetch=1, grid=(rows // tm,),
            in_specs=[pl.BlockSpec((None, tm, cols), lambda i, chip_ref: (chip_ref[0], i, 0)),
                      pl.BlockSpec((N_CHIPS - 1, tm, cols), lambda i, chip_ref: (0, i, 0))],
            out_specs=pl.BlockSpec((tm, cols), lambda i, chip_ref: (i, 0))),
        out_shape=jax.ShapeDtypeStruct((rows, cols), F32),
        compiler_params=_cparams(("parallel",)),
    )(chip, src, theirs)


PACK_COLS = 1024
SMALL_ROW_MULTIPLE = 32
BIG = ["dn_w_in", "dn_w_out", "cv_w_pw1", "cv_w_pw2", "xa_w_q", "xa_w_kv", "xa_w_o", "mlp_w_up", "mlp_w_down"]
SMALL = ["dn_w_conv", "cv_norm", "cv_b_pw1", "cv_w_dw", "cv_b_dw", "cv_ln_g", "cv_ln_b", "cv_b_pw2"]
SHARD_AXIS = {"dn_w_in": 2, "dn_w_conv": 2, "dn_w_out": 1, "cv_norm": 1, "cv_w_pw1": 2, "cv_b_pw1": 1,
              "cv_w_dw": 2, "cv_b_dw": 1, "cv_ln_g": 1, "cv_ln_b": 1, "cv_w_pw2": 1, "cv_b_pw2": 1,
              "xa_w_q": 1, "xa_w_kv": 2, "xa_w_o": 1, "mlp_w_up": 2, "mlp_w_down": 1}
REPLICATED = ["dn_norm", "dn_a_log", "dn_dt_bias", "dn_out_norm", "xa_norm", "xa_mem_norm", "mlp_norm", "final_norm"]


def _pack_rows(size):
    return -(-size // PACK_COLS)


SHARD_SHAPES = {
    "dn_w_in": (1, 1024, 1028), "dn_w_conv": (1, 4, 768), "dn_w_out": (1, 256, 1024), "cv_norm": (1, 256),
    "cv_w_pw1": (1, 1024, 512), "cv_b_pw1": (1, 512), "cv_w_dw": (1, 31, 256), "cv_b_dw": (1, 256),
    "cv_ln_g": (1, 256), "cv_ln_b": (1, 256), "cv_w_pw2": (1, 256, 1024), "cv_b_pw2": (1, 256),
    "xa_w_q": (2, 256, 1024), "xa_w_kv": (2, 1024, 512), "xa_w_o": (2, 256, 1024),
    "mlp_w_up": (2, 1024, 1024), "mlp_w_down": (2, 1024, 1024)}


def _shard_shape(nm):
    return SHARD_SHAPES[nm]


def _pack(tensors, names, dtype, row_multiple):
    pieces = []
    for nm in names:
        t = tensors[nm]
        flat = t.reshape(t.shape[0], -1) if t.ndim > len(_shard_shape(nm)) else t.reshape(1, -1)
        pad = _pack_rows(flat.shape[1]) * PACK_COLS - flat.shape[1]
        pieces.append(jnp.pad(flat.astype(dtype), ((0, 0), (0, pad))))
    cat = jnp.concatenate(pieces, axis=1)
    rows = cat.shape[1] // PACK_COLS
    total = -(-rows // row_multiple) * row_multiple
    cat = jnp.pad(cat, ((0, 0), (0, (total - rows) * PACK_COLS)))
    return cat.reshape(cat.shape[0], total, PACK_COLS)


def _unpack(pack, names):
    lead = pack.shape[:-2]
    flat = pack.reshape(lead + (-1,))
    out, off = {}, 0
    for nm in names:
        shp = _shard_shape(nm)
        size = 1
        for s in shp:
            size *= s
        out[nm] = flat[..., off:off + size].reshape(lead + shp)
        off += _pack_rows(size) * PACK_COLS
    return out


def _to_full(nm, stacked):
    ax = SHARD_AXIS[nm]
    moved = jnp.moveaxis(stacked, 0, ax)
    shp = list(_shard_shape(nm))
    shp[ax] *= N_CHIPS
    return moved.reshape(shp)


def _to_shards(nm, full):
    ax = SHARD_AXIS[nm]
    shp = list(_shard_shape(nm))
    split = full.reshape(shp[:ax] + [N_CHIPS, shp[ax]] + shp[ax + 1:])
    return jnp.moveaxis(split, ax, 0)


def _row(v):
    return v.reshape(1, -1)


class Stacked:
    def __init__(self, arr, split, layer):
        self.arr, self.kw = arr, dict(b_split=split, b_layer=layer)


def _grad_out(split):
    return dict(out_dtype=BF16, out_split=split, out_layer=(0, 1))


def _with_next(res, next_gain):
    return (res[0], res[1]) if next_gain is not None else (res, None)


def mlp_fwd(tag, h, g, w_up, w_down, n=None, next_gain=None):
    if n is None:
        n = rms_fwd(tag + "_norm", h, g)
    act = mm(tag + "_up", n, w_up.arr, out_dtype=BF16, epi=lambda acc: jnp.square(jnp.maximum(acc, 0.0)), **w_up.kw)
    out, n_next = _with_next(mm(tag + "_down", act, w_down.arr, tk=LONG_TILE_K, epi=lambda acc, res: acc + res,
                                epi_tiles=(h,), norm_gain=next_gain, **w_down.kw), next_gain)
    return out, n_next, (n, act)


def mlp_bwd(tag, dh, h, g, w_up, w_down, saved, after=None):
    n, act = saved
    dh, dh16 = dh
    dup = mm(tag + "_d_act", dh16, w_down.arr, tb=True, out_dtype=BF16, after=after,
             epi=lambda acc, t: acc * (2.0 * jnp.sqrt(t.astype(F32))), epi_tiles=(act,), **w_down.kw)
    dw_down = mm(tag + "_dw_down", act, dh16, ta=True, tk=LONG_TILE_K, **_grad_out("rows"))
    dn = mm(tag + "_dn", dup, w_up.arr, tb=True, out_dtype=BF16, **w_up.kw)
    dw_up = mm(tag + "_dw_up", n, dup, ta=True, tk=LONG_TILE_K, **_grad_out("cols"))
    dh_in, dh16_in, dg = rms_bwd(tag + "_norm_bwd", dn, h, g, dh)
    return (dh_in, dh16_in), dg, (dw_up, dw_down)


def xa_fwd(tag, h, mem, g, g_mem, w_q, w_kv, w_o, n=None, next_gain=None):
    if n is None:
        n = rms_fwd(tag + "_norm", h, g)
    mem_n = rms_fwd(tag + "_mem_norm", mem, g_mem)
    q = mm(tag + "_q", n, w_q.arr, out_dtype=BF16, **w_q.kw)
    kv = mm(tag + "_kv", mem_n, w_kv.arr, out_dtype=BF16, **w_kv.kw)
    o = xa_core_fwd(tag + "_core", q, kv)
    out, n_next = _with_next(mm(tag + "_o", o, w_o.arr, epi=lambda acc, res: acc + res, epi_tiles=(h,),
                                norm_gain=next_gain, **w_o.kw), next_gain)
    return out, n_next, (n, mem_n, q, kv, o)


def xa_bwd(tag, dh, h, mem, g, g_mem, w_q, w_kv, w_o, saved):
    n, mem_n, q, kv, o = saved
    dh, dh16 = dh
    d_o = mm(tag + "_d_o", dh16, w_o.arr, tb=True, out_dtype=BF16, **w_o.kw)
    dw_o = mm(tag + "_dw_o", o, dh16, ta=True, tk=LONG_TILE_K, **_grad_out("rows"))
    dq, dkv = xa_core_bwd(tag + "_core_bwd", d_o, q, kv)
    dn = mm(tag + "_dn", dq, w_q.arr, tb=True, out_dtype=BF16, **w_q.kw)
    dw_q = mm(tag + "_dw_q", n, dq, ta=True, tk=LONG_TILE_K, **_grad_out("rows"))
    dh_in, dh16_in, dg = rms_bwd(tag + "_norm_bwd", dn, h, g, dh)
    dw_kv = mm(tag + "_dw_kv", mem_n, dkv, ta=True, **_grad_out("cols"))
    dmem_n = mm(tag + "_dmem", dkv, w_kv.arr, tb=True, **w_kv.kw)
    dg_mem = mem_norm_bwd(tag + "_mem_norm_bwd", dmem_n, mem, g_mem)
    return (dh_in, dh16_in), dg, dg_mem, (dw_q, dw_kv, dw_o)


def _gate_tile(a_log, dt_bias):
    t = jnp.zeros((8, LANES), F32)
    t = t.at[0, DN_HEADS:2 * DN_HEADS].set(a_log.reshape(-1))
    return t.at[1, DN_HEADS:2 * DN_HEADS].set(dt_bias.reshape(-1))


def dn_fwd(h, g, w_qkv, w_z, w_ba, w_conv, gate, out_norm, w_out, next_gain=None):
    n = rms_fwd("dn_norm", h, g)
    qkv_raw = mm("dn_proj_qkv", n, w_qkv)
    z = mm("dn_proj_z", n, w_z)
    ba = mm("dn_proj_ba", n, w_ba)
    qkv, hs = dn_pre(qkv_raw, ba, w_conv, gate)
    u, w, t_inv = dn_solve(qkv, hs)
    o, states = dn_scan_fwd(qkv, u, w, hs)
    og = dn_post(o, z, out_norm)
    out, n_next = _with_next(mm("dn_out", og, w_out.arr, epi=lambda acc, res: acc + res, epi_tiles=(h,),
                                norm_gain=next_gain, **w_out.kw), next_gain)
    return out, n_next, (n, qkv_raw, z, ba, qkv, hs, u, w, t_inv, o, states, og)


def dn_bwd(dh, h, g, w_qkv, w_z, w_ba, w_conv, gate, out_norm, w_out, saved, after=None):
    n, qkv_raw, z, ba, qkv, hs, u, w, t_inv, o, states, og = saved
    dh, dh16 = dh
    d_og = mm("dn_d_og", dh16, w_out.arr, tb=True, out_dtype=BF16, after=after, **w_out.kw)
    dw_out = mm("dn_dw_out", og, dh16, ta=True, tk=LONG_TILE_K, **_grad_out("rows"))
    d_o, dz, d_out_norm = dn_post_bwd(d_og, o, z, out_norm)
    dq, dk, dv, dhs = dn_scan_bwd(qkv, u, w, t_inv, hs, states, d_o)
    dc, dba, d_gate = dn_pre_bwd(dq, dk, dv, dhs, qkv_raw, ba, w_conv, gate)
    dqkv_raw, dw_conv = dn_conv_bwd(dc, qkv_raw, w_conv)
    dn = mm("dn_dn_qkv", dqkv_raw, w_qkv, tb=True)
    dn = mm("dn_dn_z", dz, w_z, tb=True, epi=lambda acc, t: acc + t, epi_tiles=(dn,))
    dn = mm("dn_dn_ba", dba, w_ba, tb=True, epi=lambda acc, t: acc + t, epi_tiles=(dn,))
    dw_qkv = mm("dn_dw_qkv", n, dqkv_raw, ta=True, tk=LONG_TILE_K)
    dw_z = mm("dn_dw_z", n, dz, ta=True, tk=LONG_TILE_K)
    dw_ba = mm("dn_dw_ba", n, dba, ta=True, tk=LONG_TILE_K)
    dh_in, _, dg = rms_bwd("dn_norm_bwd", dn, h, g, dh)
    return dh_in, dg, dw_qkv, dw_z, dw_ba, dw_conv, d_gate, d_out_norm, dw_out


def cv_fwd(h, g, w_pw1, b_pw1, w_dw, b_dw, ln_g, ln_b, w_pw2, b_pw2, n=None, next_gain=None):
    if n is None:
        n = rms_fwd("cv_norm", h, g)
    u = mm("cv_pw1", n, w_pw1.arr, epi=lambda acc, b: acc + b, epi_rows=(b_pw1,), **w_pw1.kw)
    s, c = cv_core_fwd(u, w_dw, b_dw, ln_g, ln_b)
    out, n_next = _with_next(mm("cv_pw2", s, w_pw2.arr, epi=lambda acc, res, b: acc + res + b, epi_tiles=(h,),
                                epi_rows=(b_pw2,), norm_gain=next_gain, **w_pw2.kw), next_gain)
    return out, n_next, (n, u, s, c)


def cv_bwd(dh, h, g, w_pw1, w_dw, ln_g, ln_b, w_pw2, saved):
    n, u, s, c = saved
    dh, dh16 = dh
    ds = mm("cv_d_s", dh16, w_pw2.arr, tb=True, out_dtype=BF16, **w_pw2.kw)
    dw_pw2 = mm("cv_dw_pw2", s, dh16, ta=True, tk=LONG_TILE_K, **_grad_out("rows"))
    db_pw2 = col_sum("cv_db_pw2", dh)
    dc, ln_acc = cv_ln_bwd(ds, c, ln_g, ln_b)
    du, dw_dw, db_pw1 = cv_conv_bwd(dc, u, w_dw)
    dn = mm("cv_dn", du, w_pw1.arr, tb=True, out_dtype=BF16, **w_pw1.kw)
    dw_pw1 = mm("cv_dw_pw1", n, du, ta=True, tk=LONG_TILE_K, **_grad_out("cols"))
    dh_in, dh16_in, dg = rms_bwd("cv_norm_bwd", dn, h, g, dh)
    return (dh_in, dh16_in), dg, dw_pw1, db_pw1, dw_dw, ln_acc, dw_pw2, db_pw2


WEIGHTS = ["dn_norm", "dn_w_in", "dn_w_conv", "dn_a_log", "dn_dt_bias", "dn_out_norm", "dn_w_out", "cv_norm",
           "cv_w_pw1", "cv_b_pw1", "cv_w_dw", "cv_b_dw", "cv_ln_g", "cv_ln_b", "cv_w_pw2", "cv_b_pw2", "xa_norm",
           "xa_mem_norm", "xa_w_q", "xa_w_kv", "xa_w_o", "mlp_norm", "mlp_w_up", "mlp_w_down", "final_norm"]


def _as_2d(t):
    if t.ndim == 1:
        return t.reshape(1, -1)
    return t.reshape(-1, t.shape[-1])


def kernel(x, mem, dn_norm, dn_w_in, dn_w_conv, dn_a_log, dn_dt_bias, dn_out_norm, dn_w_out, cv_norm, cv_w_pw1, cv_b_pw1, cv_w_dw, cv_b_dw, cv_ln_g, cv_ln_b, cv_w_pw2, cv_b_pw2, xa_norm, xa_mem_norm, xa_w_q, xa_w_kv, xa_w_o, mlp_norm, mlp_w_up, mlp_w_down, final_norm, loss_target, m_dn_norm, m_dn_w_in, m_dn_w_conv, m_dn_a_log, m_dn_dt_bias, m_dn_out_norm, m_dn_w_out, m_cv_norm, m_cv_w_pw1, m_cv_b_pw1, m_cv_w_dw, m_cv_b_dw, m_cv_ln_g, m_cv_ln_b, m_cv_w_pw2, m_cv_b_pw2, m_xa_norm, m_xa_mem_norm, m_xa_w_q, m_xa_w_kv, m_xa_w_o, m_mlp_norm, m_mlp_w_up, m_mlp_w_down, m_final_norm, v_dn_norm, v_dn_w_in, v_dn_w_conv, v_dn_a_log, v_dn_dt_bias, v_dn_out_norm, v_dn_w_out, v_cv_norm, v_cv_w_pw1, v_cv_b_pw1, v_cv_w_dw, v_cv_b_dw, v_cv_ln_g, v_cv_ln_b, v_cv_w_pw2, v_cv_b_pw2, v_xa_norm, v_xa_mem_norm, v_xa_w_q, v_xa_w_kv, v_xa_w_o, v_mlp_norm, v_mlp_w_up, v_mlp_w_down, v_final_norm):
    args = dict(locals())
    wts = {nm: args[nm] for nm in WEIGHTS}
    mom = {nm: args["m_" + nm] for nm in WEIGHTS}
    var = {nm: args["v_" + nm] for nm in WEIGHTS}
    core = lax.axis_index("c").astype(jnp.int32).reshape(1)
    chip = (2 * lax.axis_index("x") + lax.axis_index("y")).astype(jnp.int32)
    def own_slab(got, src):
        return lax.dynamic_update_slice(got, src[None], (chip, 0, 0))

    shard2d = {nm: wts[nm].astype(BF16).reshape(-1, wts[nm].shape[-1]) for nm in BIG}
    first = ["dn_w_in", "dn_w_out"]
    later = [nm for nm in BIG if nm not in first]
    sources = [shard2d[nm] for nm in first] + [_pack(wts, SMALL, F32, SMALL_ROW_MULTIPLE)[0]]
    gathered = [own_slab(got, src) for got, src in zip(gather_shards(sources), sources)]
    stacked = {"dn_w_out": gathered[1].reshape((N_CHIPS,) + SHARD_SHAPES["dn_w_out"])}
    full = {nm: _to_full(nm, t) for nm, t in _unpack(gathered[2], SMALL).items()}
    full["dn_w_in"] = _to_full("dn_w_in", gathered[0].reshape((N_CHIPS,) + SHARD_SHAPES["dn_w_in"]))
    full.update({nm: wts[nm] for nm in REPLICATED})
    later_src = [shard2d[nm] for nm in later]
    g_send, g_recv, later_src, g_lands, started = ici_start(
        "gather_start", "gather", later_src, [(N_CHIPS,) + s.shape for s in later_src])
    full["dn_norm"] = full["dn_norm"] + started[0, 0]

    def rest_weights(after):
        srcs, lands = ici_wait("gather_wait", "gather", g_send, g_recv, later_src, g_lands, after)
        return {nm: own_slab(land, src).reshape((N_CHIPS,) + SHARD_SHAPES[nm])
                for nm, land, src in zip(later, pair_forward(lands), srcs)}

    pending = []

    def on_grads(items):
        tag = "_".join(sorted({str(layer) for _, layer, _ in items}))
        parts = [g.reshape(N_CHIPS, -1, g.shape[-1]) for _, _, g in items]
        theirs = pair_split("pair_split_" + tag, parts)
        pairs = [add_pairs("pair_add_%s%d" % (nm, layer), p, t, core, BF16)
                 for (nm, layer, _), p, t in zip(items, parts, theirs)]
        send, recv, pairs, lands, token = ici_start(
            "scatter_start_" + tag, "scatter", pairs, [(N_CHIPS - 1,) + p.shape[1:] for p in pairs])
        pending.append((tag, items, send, recv, pairs, lands))
        return token

    dh, grads, rep = local_step(x[0], mem[0], loss_target[0], stacked, full, rest_weights, on_grads)

    halves = {}
    last = [("dn_w_in", 0, _to_shards("dn_w_in", grads["dn_w_in"]).astype(BF16)), ("dn_w_out", 0, grads["dn_w_out"][0]),
            ("small", 0, _pack({nm: _to_shards(nm, grads[nm]) for nm in SMALL}, SMALL, F32, SMALL_ROW_MULTIPLE))]
    parts = [g.reshape(N_CHIPS, -1, g.shape[-1]) for _, _, g in last]
    theirs = pair_split("pair_split_last", parts)
    pairs = [add_pairs("pair_add_" + nm, p, t, core, p.dtype) for (nm, _, _), p, t in zip(last, parts, theirs)]
    l_send, l_recv, l_pairs, l_lands, l_started = ici_start(
        "scatter_start_last", "scatter", pairs, [(N_CHIPS - 1,) + p.shape[1:] for p in pairs])
    for tag, items, send, recv, pairs, lands in pending:
        pairs, lands = ici_wait("scatter_wait_" + tag, "scatter", send, recv, pairs, lands, l_started)
        for (nm, layer, _), p, o in zip(items, pairs, lands):
            halves[nm, layer] = add_four("chip_add_%s%d" % (nm, layer), p, o, chip.reshape(1))
    keys = sorted(halves)
    siblings = dict(zip(keys, pair_join("pair_join_early", [halves[k] for k in keys])))

    delta, new_m, new_v, red = {}, {}, {}, {}

    def big_adamw(nm):
        layers = range(wts[nm].shape[0])
        res = adamw_halves("adamw_" + nm, _as_2d(wts[nm]), [halves[nm, l] for l in layers],
                           [siblings[nm, l] for l in layers], _as_2d(mom[nm]), _as_2d(var[nm]), core)
        red[nm], delta[nm], new_m[nm], new_v[nm] = (r.reshape(wts[nm].shape) for r in res)

    early = [nm for nm in BIG if (nm, 0) in halves]
    for nm in early:
        big_adamw(nm)
    done = jnp.concatenate([new_v[nm].reshape(-1)[:1] for nm in early])
    l_pairs, l_lands = ici_wait("scatter_wait_last", "scatter", l_send, l_recv, l_pairs, l_lands, done)
    for (nm, layer, _), p, o in zip(last, l_pairs, l_lands):
        halves[nm, layer] = add_four("chip_add_" + nm, p, o, chip.reshape(1))
    keys = [(nm, layer) for nm, layer, _ in last]
    siblings.update(zip(keys, pair_join("pair_join_last", [halves[k] for k in keys])))
    south = core[0] == 0
    mine, theirs = halves["small", 0], siblings["small", 0]
    red.update(_unpack(jnp.concatenate([jnp.where(south, mine, theirs), jnp.where(south, theirs, mine)], axis=0),
                       SMALL))

    rep = all_sum_small(rep)
    red["dn_norm"] = rep[0:1]
    red["dn_a_log"] = rep[1:2, DN_HEADS:2 * DN_HEADS]
    red["dn_dt_bias"] = rep[2:3, DN_HEADS:2 * DN_HEADS]
    red["dn_out_norm"] = rep[3:4, :LANES]
    red["xa_norm"], red["xa_mem_norm"], red["mlp_norm"] = rep[4:6], rep[6:8], rep[8:10]
    red["final_norm"] = rep[10]
    loss = rep[11, 0]

    for nm in WEIGHTS:
        shp = wts[nm].shape
        if nm in early:
            continue
        if nm in BIG:
            big_adamw(nm)
            continue
        res = adamw("adamw_" + nm, _as_2d(wts[nm]), _as_2d(red[nm].reshape(shp)), _as_2d(mom[nm]), _as_2d(var[nm]))
        delta[nm], new_m[nm], new_v[nm] = (r.reshape(shp) for r in res)
        red[nm] = red[nm].reshape(shp)

    grad_x = dh[None]
    return (loss, grad_x, *[red[nm] for nm in WEIGHTS], *[delta[nm] for nm in WEIGHTS],
            *[new_m[nm] for nm in WEIGHTS], *[new_v[nm] for nm in WEIGHTS])


def local_step(h0, mem0, target, stacked, full, rest_weights=None, on_grads=None):
    d = h0.shape[1]
    dn_norm, dn_a_log, dn_dt_bias, dn_out_norm = (full[nm] for nm in REPLICATED[:4])
    xa_norm, xa_mem_norm, mlp_norm, final_norm = (full[nm] for nm in REPLICATED[4:])
    inner = DN_HEADS * DN_HEAD_DIM
    w_in = full["dn_w_in"][0]
    w_qkv, w_z = w_in[:, :3 * inner], w_in[:, 3 * inner:4 * inner]
    w_ba = jnp.pad(w_in[:, 4 * inner:], ((0, 0), (0, LANES - 2 * DN_HEADS)))
    w_conv = jnp.pad(full["dn_w_conv"][0], ((0, 8 - DN_CONV), (0, 0)))
    gate = _gate_tile(dn_a_log, dn_dt_bias)
    w_dw = jnp.pad(full["cv_w_dw"][0], ((0, CV_HALO - CV_WIDTH), (0, 0)))

    def sw(nm, layer):
        return Stacked(stacked[nm], "rows" if SHARD_AXIS[nm] == 1 else "cols", layer)

    dn_args = (_row(dn_norm), w_qkv, w_z, w_ba, w_conv, gate, _row(dn_out_norm), sw("dn_w_out", 0))
    h1, n, dn_saved = dn_fwd(h0, *dn_args, next_gain=_row(xa_norm[0]))
    if rest_weights is not None:
        stacked = {**stacked, **rest_weights(h1)}
    xa_args = [(_row(xa_norm[l]), _row(xa_mem_norm[l]), sw("xa_w_q", l), sw("xa_w_kv", l), sw("xa_w_o", l))
               for l in range(2)]
    mlp_args = [(_row(mlp_norm[l]), sw("mlp_w_up", l), sw("mlp_w_down", l)) for l in range(2)]
    cv_args = (_row(full["cv_norm"][0]), sw("cv_w_pw1", 0), full["cv_b_pw1"], w_dw, full["cv_b_dw"],
               full["cv_ln_g"], full["cv_ln_b"], sw("cv_w_pw2", 0), full["cv_b_pw2"])
    h2, n, xa0_saved = xa_fwd("xa0", h1, mem0, *xa_args[0], n=n, next_gain=mlp_args[0][0])
    h3, n, mlp0_saved = mlp_fwd("mlp0", h2, *mlp_args[0], n=n, next_gain=cv_args[0])
    h4, n, cv_saved = cv_fwd(h3, *cv_args, n=n, next_gain=xa_args[1][0])
    h5, n, xa1_saved = xa_fwd("xa1", h4, mem0, *xa_args[1], n=n, next_gain=mlp_args[1][0])
    h6, _, mlp1_saved = mlp_fwd("mlp1", h5, *mlp_args[1], n=n)

    dh32, dh16, loss_tile, d_final = loss_head("loss_head", h6, _row(final_norm), target)
    dh = (dh32, dh16)
    grads = {}
    dg_mlp, dg_xa, dg_xa_mem = [None, None], [None, None], [None, None]
    dw_mlp, dw_xa = [None, None], [None, None]
    mlp_names, xa_names = ("mlp_w_up", "mlp_w_down"), ("xa_w_q", "xa_w_kv", "xa_w_o")

    def announce(items):
        return None if on_grads is None else on_grads(items)

    dh, dg_mlp[1], dw_mlp[1] = mlp_bwd("mlp1", dh, h5, *mlp_args[1], mlp1_saved)
    dh, dg_xa[1], dg_xa_mem[1], dw_xa[1] = xa_bwd("xa1", dh, h4, mem0, *xa_args[1], xa1_saved)
    (dh, grads["cv_norm"], dw_pw1, grads["cv_b_pw1"], dw_dw, ln_acc, dw_pw2,
     grads["cv_b_pw2"]) = cv_bwd(dh, h3, cv_args[0], cv_args[1], w_dw, cv_args[5], cv_args[6], cv_args[7], cv_saved)
    after = announce([(nm, 1, g) for nm, g in zip(mlp_names + xa_names, dw_mlp[1] + dw_xa[1])]
                     + [("cv_w_pw1", 0, dw_pw1), ("cv_w_pw2", 0, dw_pw2)])
    dh, dg_mlp[0], dw_mlp[0] = mlp_bwd("mlp0", dh, h2, *mlp_args[0], mlp0_saved, after=after)
    dh, dg_xa[0], dg_xa_mem[0], dw_xa[0] = xa_bwd("xa0", dh, h1, mem0, *xa_args[0], xa0_saved)
    after = announce([(nm, 0, g) for nm, g in zip(mlp_names + xa_names, dw_mlp[0] + dw_xa[0])])
    dh, dg_dn, dw_qkv, dw_z, dw_ba, dw_conv, d_gate, d_out_norm, dw_out = dn_bwd(dh, h0, *dn_args, dn_saved,
                                                                                 after=after)

    grads["dn_w_in"] = jnp.concatenate([dw_qkv, dw_z, dw_ba[:, :2 * DN_HEADS]], axis=1)[None]
    grads["dn_w_conv"] = dw_conv[None, :DN_CONV]
    grads["dn_w_out"], grads["cv_w_pw1"], grads["cv_w_pw2"] = [dw_out], [dw_pw1], [dw_pw2]
    grads["cv_w_dw"] = dw_dw[None, :CV_WIDTH]
    grads["cv_ln_g"], grads["cv_ln_b"], grads["cv_b_dw"] = ln_acc[0:1], ln_acc[1:2], ln_acc[2:3]
    for i, nm in enumerate(mlp_names):
        grads[nm] = [dw_mlp[0][i], dw_mlp[1][i]]
    for i, nm in enumerate(xa_names):
        grads[nm] = [dw_xa[0][i], dw_xa[1][i]]

    rep = jnp.zeros((16, d), F32)
    rep = rep.at[0].set(dg_dn[0])
    rep = rep.at[1, :LANES].set(d_gate[0])
    rep = rep.at[2, :LANES].set(d_gate[1])
    rep = rep.at[3, :LANES].set(d_out_norm[0])
    rep = rep.at[4].set(dg_xa[0][0]).at[5].set(dg_xa[1][0])
    rep = rep.at[6].set(dg_xa_mem[0][0]).at[7].set(dg_xa_mem[1][0])
    rep = rep.at[8].set(dg_mlp[0][0]).at[9].set(dg_mlp[1][0])
    rep = rep.at[10].set(d_final[0])
    rep = rep.at[11, :LANES].set(loss_tile[0])
    return dh, grads, rep
```

```python
import functools

import jax
import jax.numpy as jnp
from jax import lax
from jax.experimental import pallas as pl
from jax.experimental.pallas import tpu as pltpu

F32 = jnp.float32
BF16 = jnp.bfloat16
HIGHEST = lax.Precision.HIGHEST
MESH = pl.DeviceIdType.MESH

D_MODEL = 1024
DN_HEADS = 8
DN_HEAD_DIM = 128
DN_CONV = 4
DN_CHUNK = 64
CV_WIDTH = 31
XA_HEADS = 4
XA_HEAD_DIM = 256
RMS_EPS = 1e-6
LN_EPS = 1e-5
L2_EPS = 1e-6

ADAM_LR = 0.001
ADAM_B1 = 0.9
ADAM_B2 = 0.999
ADAM_EPS = 1e-08
ADAM_WD = 0.01
ADAM_STEP = 10

LANES = 128
ROW_TILE = 512
CONV_ROW_TILE = 256
MM_TILE = 1024
GRAD_TILE_K = 4096
LONG_TILE_K = 2048
ADAMW_ROW_TILE = 256
DN_ROW_TILE = 256
CHUNK_SHIFT = 6
SOLVE_INTERLEAVE = 8
FWD_HEADS_PER_STEP = 8
BWD_HEADS_PER_STEP = 8
BWD_SCAN_ROWS = 256
DN_HALO = 8
CV_HALO = 32
VMEM_LIMIT = 48 * 1024 * 1024
N_CHIPS = 4
D2D_CHUNK_ROWS = 256


def _cparams(sem):
    return pltpu.CompilerParams(dimension_semantics=sem, vmem_limit_bytes=VMEM_LIMIT)


def _dot(a, b, dims=(((1,), (0,)), ((), ()))):
    return lax.dot_general(a.astype(BF16), b.astype(BF16), dims, preferred_element_type=F32)


def _dot_nt(a, b):
    return _dot(a, b, (((1,), (1,)), ((), ())))


def _dot_tn(a, b):
    return _dot(a, b, (((0,), (0,)), ((), ())))


def _dot_hi(a, b, dims=(((1,), (0,)), ((), ()))):
    return lax.dot_general(a.astype(F32), b.astype(F32), dims, precision=HIGHEST, preferred_element_type=F32)


def _dot_x3(a, b, dims=(((1,), (0,)), ((), ()))):
    a_hi, b_hi = a.astype(BF16), b.astype(BF16)
    a_lo = (a - a_hi.astype(F32)).astype(BF16)
    b_lo = (b - b_hi.astype(F32)).astype(BF16)

    def dot(p, q):
        return lax.dot_general(p, q, dims, preferred_element_type=F32)

    return dot(a_hi, b_hi) + (dot(a_hi, b_lo) + dot(a_lo, b_hi))


def _sigmoid(x):
    return 1.0 / (1.0 + jnp.exp(-x))


def _silu(x):
    return x * _sigmoid(x)


def _silu_grad(x):
    s = _sigmoid(x)
    return s * (1.0 + x * (1.0 - s))


def _softplus(x):
    return jnp.maximum(x, 0.0) + jnp.log(1.0 + jnp.exp(-jnp.abs(x)))


def _iota(shape, dim):
    return lax.broadcasted_iota(jnp.int32, shape, dim)


def _lane_col(vals, lane, idx):
    return jnp.sum(jnp.where(lane == idx, vals, 0.0), axis=1, keepdims=True)


def _pick_tile(rows, cap):
    best = rows
    for t in range(16, min(rows, cap) + 1, 16):
        if rows % t == 0:
            best = t
    return best


def _stacked_spec(shape, split, layer, rows, cols, block_index):
    r_shard, c_shard = shape[-2], shape[-1]
    if split == "rows" and rows > r_shard:
        assert rows % r_shard == 0 and c_shard % cols == 0
        chips = rows // r_shard

        def slabs(i, j, kk):
            bi, bj = block_index(i, j, kk)
            return (bi, layer, 0, bj)

        return pl.BlockSpec((chips, None, r_shard, cols), slabs), chips
    assert r_shard % rows == 0 and c_shard % cols == 0
    per_chip = (r_shard // rows) if split == "rows" else (c_shard // cols)

    def index(i, j, kk):
        bi, bj = block_index(i, j, kk)
        if split == "rows":
            return (bi // per_chip, layer, bi % per_chip, bj)
        return (bj // per_chip, layer, bi, bj % per_chip)

    return pl.BlockSpec((None, None, rows, cols), index), 1


def mm(name, a, b, *, ta=False, tb=False, out_dtype=F32, pro=None, epi=None, epi_tiles=(), epi_rows=(),
       tm=MM_TILE, tn=MM_TILE, tk=MM_TILE, b_split=None, b_layer=None, out_split=None, out_layer=None,
       after=None, norm_gain=None):
    m, k = (a.shape[1], a.shape[0]) if ta else a.shape
    b_rows, b_cols = b.shape[-2], b.shape[-1]
    if b_split == "rows":
        b_rows *= N_CHIPS
    elif b_split == "cols":
        b_cols *= N_CHIPS
    n = b_rows if tb else b_cols
    assert (b_cols if tb else b_rows) == k
    tm, tn, tk = min(tm, m), min(tn, n), min(tk, k)
    if b_split == "cols":
        if tb:
            tk = min(tk, b.shape[-1])
        else:
            tn = min(tn, b.shape[-1])
    if out_split == "cols":
        tn = min(tn, n // N_CHIPS)
    assert m % tm == 0 and n % tn == 0 and k % tk == 0
    nk = k // tk
    a_spec = pl.BlockSpec((tk, tm), lambda i, j, kk: (kk, i)) if ta else pl.BlockSpec((tm, tk), lambda i, j, kk: (i, kk))
    b_block = (tn, tk) if tb else (tk, tn)
    b_index = (lambda i, j, kk: (j, kk)) if tb else (lambda i, j, kk: (kk, j))
    b_chips = o_chips = 1
    if b_split is None:
        b_spec = pl.BlockSpec(b_block, b_index)
    else:
        b_spec, b_chips = _stacked_spec(b.shape, b_split, b_layer, b_block[0], b_block[1], b_index)
    in_specs = [a_spec, b_spec]
    in_specs += [pl.BlockSpec((tm, tn), lambda i, j, kk: (i, j)) for _ in epi_tiles]
    in_specs += [pl.BlockSpec((1, tn), lambda i, j, kk: (0, j)) for _ in epi_rows]
    n_t, n_r = len(epi_tiles), len(epi_rows)
    dims = (((0 if ta else 1,), (1 if tb else 0,)), ((), ()))
    if out_split is None:
        out_shape = jax.ShapeDtypeStruct((m, n), out_dtype)
        out_spec = pl.BlockSpec((tm, tn), lambda i, j, kk: (i, j))
    else:
        shard = (m // N_CHIPS, n) if out_split == "rows" else (m, n // N_CHIPS)
        out_shape = jax.ShapeDtypeStruct((N_CHIPS, out_layer[1]) + shard, out_dtype)
        out_spec, o_chips = _stacked_spec(out_shape.shape, out_split, out_layer[0], tm, tn, lambda i, j, kk: (i, j))
    single_pass = nk == 1 and norm_gain is None and b_chips == 1 and o_chips == 1
    extra = []
    if norm_gain is not None:
        assert tn == n and out_split is None
        extra.append(norm_gain)
        in_specs.append(pl.BlockSpec((1, n), lambda i, j, kk: (0, 0)))
        out_shape = [out_shape, jax.ShapeDtypeStruct((m, n), BF16)]
        out_spec = [out_spec, pl.BlockSpec((tm, tn), lambda i, j, kk: (i, j))]
    if after is not None:
        extra.append(after)
        in_specs.append(pl.BlockSpec(memory_space=pl.ANY))

    def body(a_ref, b_ref, *rest):
        tiles = rest[:n_t]
        rows = rest[n_t:n_t + n_r]
        gain_ref = rest[n_t + n_r] if norm_gain is not None else None
        rest = rest[n_t + n_r + len(extra):]
        o_ref, acc_ref = rest[0], rest[-1]
        av = a_ref[...]
        if pro is not None:
            av = pro(av)
        if single_pass:
            out = _dot(av, b_ref[...], dims)
            if epi is not None:
                out = epi(out, *[t[...] for t in tiles], *[r[...] for r in rows])
            o_ref[...] = out.astype(out_dtype)
            return
        kk = pl.program_id(2)

        @pl.when(kk == 0)
        def _():
            acc_ref[...] = jnp.zeros_like(acc_ref)

        bv = b_ref[...]
        if b_chips > 1:
            bv = bv.reshape(b_block)
        acc_ref[...] += _dot(av, bv, dims)

        @pl.when(kk == nk - 1)
        def _():
            out = acc_ref[...]
            if epi is not None:
                out = epi(out, *[t[...] for t in tiles], *[r[...] for r in rows])
            if gain_ref is not None:
                rest[1][...] = (_rms_stats(out)[0] * gain_ref[...]).astype(BF16)
            out = out.astype(out_dtype)
            o_ref[...] = out.reshape(o_chips, tm // o_chips, tn) if o_chips > 1 else out

    return pl.pallas_call(
        body, name=name, grid=(m // tm, n // tn, nk),
        in_specs=in_specs, out_specs=out_spec, out_shape=out_shape,
        scratch_shapes=[] if single_pass else [pltpu.VMEM((tm, tn), F32)],
        compiler_params=_cparams(("parallel", "parallel", "arbitrary")),
    )(a, b, *epi_tiles, *epi_rows, *extra)


def row_call(name, body, n_rows, tm, ins, outs, accs=()):
    tm = _pick_tile(n_rows, tm)
    in_specs = []
    for arr, kind in ins:
        if kind == "tile":
            if arr.ndim == 2:
                in_specs.append(pl.BlockSpec((tm, arr.shape[1]), lambda i: (i, 0)))
            else:
                in_specs.append(pl.BlockSpec((arr.shape[0], tm, arr.shape[2]), lambda i: (0, i, 0)))
        elif kind == "full":
            in_specs.append(pl.BlockSpec(arr.shape, functools.partial(lambda i, nd: (0,) * nd, nd=arr.ndim)))
        else:
            where, h = kind
            per = tm // h
            if where == "prev":
                in_specs.append(pl.BlockSpec((h, arr.shape[1]), functools.partial(
                    lambda i, per: (jnp.maximum(i * per - 1, 0), 0), per=per)))
            else:
                last = n_rows // h - 1
                in_specs.append(pl.BlockSpec((h, arr.shape[1]), functools.partial(
                    lambda i, per, last: (jnp.minimum((i + 1) * per, last), 0), per=per, last=last)))
    out_shape, out_specs = [], []
    for shape, dtype in outs:
        out_shape.append(jax.ShapeDtypeStruct(shape, dtype))
        if len(shape) == 2:
            out_specs.append(pl.BlockSpec((tm, shape[1]), lambda i: (i, 0)))
        else:
            out_specs.append(pl.BlockSpec((shape[0], tm, shape[2]), lambda i: (0, i, 0)))
    for shape in accs:
        out_shape.append(jax.ShapeDtypeStruct(shape, F32))
        out_specs.append(pl.BlockSpec(shape, lambda i: (0, 0)))
    n_in, n_out, n_acc = len(ins), len(outs), len(accs)

    def kern(*refs):
        i = pl.program_id(0)
        in_refs = refs[:n_in]
        out_refs = refs[n_in:n_in + n_out]
        acc_refs = refs[n_in + n_out:n_in + n_out + n_acc]
        if n_acc:
            @pl.when(i == 0)
            def _():
                for r in acc_refs:
                    r[...] = jnp.zeros_like(r)
        body(i, in_refs, out_refs, acc_refs)

    res = pl.pallas_call(
        kern, name=name, grid=(n_rows // tm,), in_specs=in_specs, out_specs=out_specs, out_shape=out_shape,
        compiler_params=_cparams(("arbitrary",) if n_acc else ("parallel",)),
    )(*[a for a, _ in ins])
    return list(res)


def _rms_stats(h):
    r = lax.rsqrt(jnp.mean(h * h, axis=-1, keepdims=True) + RMS_EPS)
    return h * r, r


def rms_fwd(name, h, g):
    def body(i, ins, outs, accs):
        xhat, _ = _rms_stats(ins[0][...])
        outs[0][...] = (xhat * ins[1][...]).astype(BF16)

    return row_call(name, body, h.shape[0], ROW_TILE, [(h, "tile"), (g, "full")], [(h.shape, BF16)])[0]


def _rms_bwd_tile(dn, h, g):
    xhat, r = _rms_stats(h)
    dxhat = dn * g
    dh = r * (dxhat - xhat * jnp.mean(dxhat * xhat, axis=-1, keepdims=True))
    dg = jnp.sum(dn * xhat, axis=0, keepdims=True)
    return dh, dg


def rms_bwd(name, dn, h, g, dres):
    def body(i, ins, outs, accs):
        dh, dg = _rms_bwd_tile(ins[0][...].astype(F32), ins[1][...], ins[2][...])
        total = ins[3][...] + dh
        outs[0][...] = total
        outs[1][...] = total.astype(BF16)
        accs[0][...] += dg

    d = h.shape[1]
    out, out16, dg = row_call(name, body, h.shape[0], ROW_TILE,
                              [(dn, "tile"), (h, "tile"), (g, "full"), (dres, "tile")],
                              [(h.shape, F32), (h.shape, BF16)], [(1, d)])
    return out, out16, dg


def mem_norm_bwd(name, dn, mem, g):
    def body(i, ins, outs, accs):
        _, dg = _rms_bwd_tile(ins[0][...].astype(F32), ins[1][...], ins[2][...])
        accs[0][...] += dg

    return row_call(name, body, mem.shape[0], ROW_TILE, [(dn, "tile"), (mem, "tile"), (g, "full")], [],
                    [(1, mem.shape[1])])[0]


def loss_head(name, h, g, target):
    d = h.shape[1]

    def body(i, ins, outs, accs):
        hv, gv = ins[0][...], ins[1][...]
        xhat, _ = _rms_stats(hv)
        err = xhat * gv - ins[2][...]
        dy = err * (1.0 / d)
        dh, dg = _rms_bwd_tile(dy, hv, gv)
        outs[0][...] = dh
        outs[1][...] = dh.astype(BF16)
        accs[0][...] += jnp.full((8, LANES), 0.5 / d, F32) * jnp.sum(err * err)
        accs[1][...] += dg

    dh, dh16, loss, dg = row_call(name, body, h.shape[0], ROW_TILE, [(h, "tile"), (g, "full"), (target, "tile")],
                                  [(h.shape, F32), (h.shape, BF16)], [(8, LANES), (1, d)])
    return dh, dh16, loss, dg


def col_sum(name, x):
    def body(i, ins, outs, accs):
        accs[0][...] += jnp.sum(ins[0][...].astype(F32), axis=0, keepdims=True)

    return row_call(name, body, x.shape[0], ROW_TILE, [(x, "tile")], [], [(1, x.shape[1])])[0]


def _conv_taps(xcat, w_ref, cols, width, halo, tm):
    rows = halo + tm
    acc = None
    for j in range(width):
        s = width - 1 - j
        xs = xcat if s == 0 else pltpu.roll(xcat, s, 0)
        term = xs[halo:rows] * w_ref[j:j + 1, cols]
        acc = term if acc is None else acc + term
    return acc


def _conv_taps_bwd_x(dcat, w_ref, cols, width, halo, tm):
    rows = halo + tm
    acc = None
    for j in range(width):
        s = width - 1 - j
        ds = dcat if s == 0 else pltpu.roll(dcat, rows - s, 0)
        term = ds[0:tm] * w_ref[j:j + 1, cols]
        acc = term if acc is None else acc + term
    return acc


def _conv_taps_bwd_w(dy, xcat, width, halo, tm, wrows):
    rows = halo + tm
    rid = _iota((wrows, dy.shape[1]), 0)
    out = jnp.zeros((wrows, dy.shape[1]), F32)
    for j in range(width):
        s = width - 1 - j
        xs = xcat if s == 0 else pltpu.roll(xcat, s, 0)
        v = jnp.sum(dy * xs[halo:rows], axis=0, keepdims=True)
        out = out + jnp.where(rid == j, v, 0.0)
    return out


def dn_pre(qkv_raw, ba, w_conv, gate):
    s_len = qkv_raw.shape[0]
    tm = min(DN_ROW_TILE, s_len)
    n_blk = qkv_raw.shape[1] // LANES

    def body(i, ins, outs, accs):
        x_ref, xp_ref, ba_ref, w_ref, gate_ref = ins
        qkv_ref, hs_ref = outs

        def blk(cb, carry):
            cols = pl.ds(pl.multiple_of(cb * LANES, LANES), LANES)
            prev = jnp.where(i > 0, xp_ref[:, cols], 0.0)
            xcat = jnp.concatenate([prev, x_ref[:, cols]], axis=0)
            c = _conv_taps(xcat, w_ref, cols, DN_CONV, DN_HALO, tm)
            y = _silu(c)
            rs = lax.rsqrt(jnp.sum(y * y, axis=-1, keepdims=True) + L2_EPS)
            fac = jnp.where(cb < DN_HEADS, DN_HEAD_DIM ** -0.5, 1.0)
            qkv_ref[:, cols] = jnp.where(cb < 2 * DN_HEADS, y * (rs * fac), y)
            return carry

        lax.fori_loop(0, n_blk, blk, 0)

        bav = ba_ref[...]
        beta = _sigmoid(bav)
        g = -jnp.exp(gate_ref[0:1, :]) * _softplus(bav + gate_ref[1:2, :])
        lane = _iota((tm, LANES), 1)
        g = jnp.where((lane >= DN_HEADS) & (lane < 2 * DN_HEADS), g, 0.0)
        r = _iota((tm, tm), 0)
        c = _iota((tm, tm), 1)
        tri = jnp.where((r >= c) & ((r >> CHUNK_SHIFT) == (c >> CHUNK_SHIFT)), 1.0, 0.0)
        gc = _dot_hi(tri, g)
        for h in range(DN_HEADS):
            hs_ref[h] = jnp.where(lane == 0, _lane_col(beta, lane, h),
                                  jnp.where(lane == 1, _lane_col(g, lane, DN_HEADS + h),
                                            jnp.where(lane == 2, _lane_col(gc, lane, DN_HEADS + h), 0.0)))

    return row_call("dn_pre", body, s_len, tm,
                    [(qkv_raw, "tile"), (qkv_raw, ("prev", DN_HALO)), (ba, "tile"), (w_conv, "full"), (gate, "full")],
                    [(qkv_raw.shape, F32), ((DN_HEADS, s_len, LANES), F32)])


def _chunk_masks():
    r = _iota((DN_CHUNK, DN_CHUNK), 0)
    c = _iota((DN_CHUNK, DN_CHUNK), 1)
    return r, c


def _decay_matrix(gc, r, c):
    gc_row = jnp.sum(jnp.where(r == c, gc, 0.0), axis=0, keepdims=True)
    causal = r >= c
    return jnp.where(causal, jnp.exp(jnp.where(causal, gc - gc_row, 0.0)), 0.0)


def _tri_inverse(lows, r, c):
    eye = jnp.where(r == c, 1.0, 0.0)
    ts = [eye for _ in lows]
    b = 1
    while b < DN_CHUNK:
        shift = b.bit_length()
        sel = ((r >> shift) == (c >> shift)) & ((r & b) != 0) & ((c & b) == 0)
        lms = [jnp.where(sel, low, 0.0) for low in lows]
        if b == 1:
            ts = [t - lm for t, lm in zip(ts, lms)]
        else:
            t_lm = [_dot_x3(t, lm) for t, lm in zip(ts, lms)]
            t_lm_t = [_dot_x3(x, t) for x, t in zip(t_lm, ts)]
            ts = [t - x for t, x in zip(ts, t_lm_t)]
        b *= 2
    return ts


def dn_solve(qkv, hs):
    s_len = qkv.shape[0]
    rb = min(ROW_TILE, s_len)
    n_chunk = rb // DN_CHUNK
    interleave = min(SOLVE_INTERLEAVE, n_chunk)

    def body(k_ref, v_ref, hs_ref, u_ref, w_ref, t_ref):
        r, c = _chunk_masks()

        def group(gi, carry):
            rows = [pl.ds(pl.multiple_of((gi * interleave + j) * DN_CHUNK, DN_CHUNK), DN_CHUNK)
                    for j in range(interleave)]
            k = [k_ref[rw, :] for rw in rows]
            beta = [hs_ref[rw, 0:1] for rw in rows]
            gc = [hs_ref[rw, 2:3] for rw in rows]
            kb = [a * b for a, b in zip(k, beta)]
            decay = [_decay_matrix(g, r, c) for g in gc]
            lows = [jnp.where(r > c, _dot_nt(a, b) * d, 0.0) for a, b, d in zip(kb, k, decay)]
            ts = _tri_inverse(lows, r, c)
            us = [_dot_x3(t, v_ref[rw, :] * b) for t, rw, b in zip(ts, rows, beta)]
            ws = [_dot_x3(t, a * jnp.exp(g)) for t, a, g in zip(ts, kb, gc)]
            for j, rw in enumerate(rows):
                u_ref[rw, :] = us[j]
                w_ref[rw, :] = ws[j].astype(BF16)
                t_ref[rw, :] = ts[j]
            return carry

        lax.fori_loop(0, n_chunk // interleave, group, 0)

    return pl.pallas_call(
        body, name="dn_solve", grid=(DN_HEADS, s_len // rb),
        in_specs=[pl.BlockSpec((rb, LANES), lambda h, i: (i, DN_HEADS + h)),
                  pl.BlockSpec((rb, LANES), lambda h, i: (i, 2 * DN_HEADS + h)),
                  pl.BlockSpec((None, rb, LANES), lambda h, i: (h, i, 0))],
        out_specs=[pl.BlockSpec((rb, LANES), lambda h, i: (i, h)),
                   pl.BlockSpec((rb, LANES), lambda h, i: (i, h)),
                   pl.BlockSpec((None, rb, DN_CHUNK), lambda h, i: (h, i, 0))],
        out_shape=[jax.ShapeDtypeStruct((s_len, DN_HEADS * LANES), F32),
                   jax.ShapeDtypeStruct((s_len, DN_HEADS * LANES), BF16),
                   jax.ShapeDtypeStruct((DN_HEADS, s_len, DN_CHUNK), F32)],
        compiler_params=_cparams(("parallel", "parallel")),
    )(qkv, qkv, hs)


def dn_scan_fwd(qkv, u, w, hs):
    s_len = qkv.shape[0]
    rb = min(ROW_TILE, s_len)
    n_chunk = rb // DN_CHUNK
    total_chunks = s_len // DN_CHUNK

    hps = FWD_HEADS_PER_STEP
    groups = DN_HEADS // hps

    def body(q_ref, k_ref, u_ref, w_ref, hs_ref, o_ref, st_ref, state):
        @pl.when(pl.program_id(1) == 0)
        def _():
            state[...] = jnp.zeros_like(state)

        r, c = _chunk_masks()

        def chunk(n, carry):
            rows = pl.ds(pl.multiple_of(n * DN_CHUNK, DN_CHUNK), DN_CHUNK)
            heads = range(hps)
            cols = [slice(h * LANES, (h + 1) * LANES) for h in heads]
            each = lambda f, *xs: [f(*a) for a in zip(*xs)]
            q = [q_ref[rows, cl] for cl in cols]
            k = [k_ref[rows, cl] for cl in cols]
            gc = [hs_ref[h, rows, 2:3] for h in heads]
            st = [state[h] for h in heads]
            for h in heads:
                st_ref[h, n] = st[h]
            gl = each(lambda g: jnp.min(g, axis=0, keepdims=True), gc)
            decay = each(lambda g: _decay_matrix(g, r, c), gc)
            w_st = [_dot(w_ref[rows, cols[h]], st[h]) for h in heads]
            qk = each(_dot_nt, q, k)
            q_st = each(lambda a, g, s: _dot(a * jnp.exp(g), s), q, gc, st)
            vn = [u_ref[rows, cols[h]] - w_st[h] for h in heads]
            ai_vn = each(lambda a, d, b: _dot(a * d, b), qk, decay, vn)
            kd_vn = each(lambda a, g0, g, b: _dot_tn(a * jnp.exp(g0 - g), b), k, gl, gc, vn)
            for h in heads:
                o_ref[rows, cols[h]] = q_st[h] + ai_vn[h]
                state[h] = st[h] * jnp.exp(gl[h]) + kd_vn[h]
            return carry

        lax.fori_loop(0, n_chunk, chunk, 0)

    wide = hps * LANES
    blk = lambda off: pl.BlockSpec((rb, wide), lambda h, i: (i, off + h))
    return pl.pallas_call(
        body, name="dn_scan_fwd", grid=(groups, s_len // rb),
        in_specs=[blk(0), blk(groups), blk(0), blk(0),
                  pl.BlockSpec((hps, rb, LANES), lambda h, i: (h, i, 0))],
        out_specs=[blk(0),
                   pl.BlockSpec((hps, n_chunk, LANES, LANES), lambda h, i: (h, i, 0, 0))],
        out_shape=[jax.ShapeDtypeStruct((s_len, DN_HEADS * LANES), F32),
                   jax.ShapeDtypeStruct((DN_HEADS, total_chunks, LANES, LANES), F32)],
        scratch_shapes=[pltpu.VMEM((hps, LANES, LANES), F32)],
        compiler_params=_cparams(("parallel", "arbitrary")),
    )(qkv, qkv, u, w, hs)


def dn_scan_bwd(qkv, u, w, t_inv, hs, states, d_o):
    s_len = qkv.shape[0]
    rb = min(BWD_SCAN_ROWS, s_len)
    n_chunk = rb // DN_CHUNK
    n_blk = s_len // rb
    hps = BWD_HEADS_PER_STEP
    groups = DN_HEADS // hps

    def body(q_ref, k_ref, v_ref, u_ref, w_ref, t_ref, hs_ref, st_ref, do_ref,
             dq_ref, dk_ref, dv_ref, dhs_ref, dstate):
        @pl.when(pl.program_id(1) == 0)
        def _():
            dstate[...] = jnp.zeros_like(dstate)

        r, c = _chunk_masks()
        causal = r >= c
        strict = r > c
        lane = _iota((DN_CHUNK, LANES), 1)
        upper = jnp.where(r <= c, 1.0, 0.0)
        last_row = _iota((DN_CHUNK, 1), 0) == DN_CHUNK - 1

        def chunk(m, carry):
            n = n_chunk - 1 - m
            rows = pl.ds(pl.multiple_of(n * DN_CHUNK, DN_CHUNK), DN_CHUNK)
            heads = range(hps)
            cols = [slice(h * LANES, (h + 1) * LANES) for h in heads]
            each = lambda f, *xs: [f(*a) for a in zip(*xs)]
            rsum = lambda x: jnp.sum(x, axis=-1, keepdims=True)
            dims_tn = (((0,), (0,)), ((), ()))
            q = [q_ref[rows, cl] for cl in cols]
            k = [k_ref[rows, cl] for cl in cols]
            v = [v_ref[rows, cl] for cl in cols]
            uu = [u_ref[rows, cl] for cl in cols]
            ww = [w_ref[rows, cl] for cl in cols]
            do = [do_ref[rows, cl] for cl in cols]
            tt = [t_ref[h, rows, :] for h in heads]
            beta = [hs_ref[h, rows, 0:1] for h in heads]
            gc = [hs_ref[h, rows, 2:3] for h in heads]
            st = [st_ref[h, n] for h in heads]
            dst = [dstate[h] for h in heads]
            gl = each(lambda g: jnp.min(g, axis=0, keepdims=True), gc)
            egc = each(jnp.exp, gc)
            egl = each(jnp.exp, gl)
            ekd = each(lambda a, b: jnp.exp(a - b), gl, gc)
            decay = each(lambda g: _decay_matrix(g, r, c), gc)
            qd = each(jnp.multiply, q, egc)
            kd = each(jnp.multiply, k, ekd)
            kb = each(jnp.multiply, k, beta)
            w_st = each(_dot, ww, st)
            qk = each(_dot_nt, q, k)
            dqd = each(_dot_nt, do, st)
            kd_dst = each(_dot, kd, dst)
            qd_do = each(_dot_tn, qd, do)
            kbk = each(_dot_nt, kb, k)
            vn = each(jnp.subtract, uu, w_st)
            ai = each(jnp.multiply, qk, decay)
            low = each(lambda a, d: jnp.where(strict, a * d, 0.0), kbk, decay)
            dai = each(lambda a, b: jnp.where(causal, _dot_nt(a, b), 0.0), do, vn)
            ai_do = each(_dot_tn, ai, do)
            dkd = each(_dot_nt, vn, dst)
            dvn = each(jnp.add, ai_do, kd_dst)
            dp = each(jnp.multiply, dai, decay)
            dw = each(lambda a, b: -_dot_nt(a, b), dvn, st)
            w_dvn = each(_dot_tn, ww, dvn)
            dp_k = each(_dot, dp, k)
            dp_q = each(_dot_tn, dp, q)
            drhs_u = each(lambda a, b: _dot_x3(a, b, dims_tn), tt, dvn)
            dgl = each(lambda a, b, e: jnp.sum(a * b) * e, dst, st, egl)
            for h in heads:
                dstate[h] = dst[h] * egl[h] + qd_do[h] - w_dvn[h]
            dq = each(lambda a, e, b: a * e + b, dqd, egc, dp_k)
            dk_a = each(lambda a, e, b: a * e + b, dkd, ekd, dp_q)
            rkd = each(lambda a, b: rsum(a * b), dkd, kd)
            drhs_w = each(lambda a, b: _dot_x3(a, b, dims_tn), tt, dw)
            dl_u = each(_dot_nt, drhs_u, uu)
            dl_w = each(_dot_nt, drhs_w, ww)
            dlow = each(lambda a, b: jnp.where(strict, -(a + b), 0.0), dl_u, dl_w)
            dqm = each(jnp.multiply, dlow, decay)
            m_tot = each(lambda a, b, d, e: a * b + d * e, dai, ai, dlow, low)
            dqm_k = each(_dot, dqm, k)
            dk_l = each(_dot_tn, dqm, kb)
            col_rows = each(lambda m: jnp.sum(m, axis=0, keepdims=True), m_tot)
            col_sums = each(lambda rw: jnp.sum(jnp.where(r == c, rw, 0.0), axis=1, keepdims=True), col_rows)
            dkb_w = each(jnp.multiply, drhs_w, egc)
            dkb = each(jnp.add, dkb_w, dqm_k)
            dgc = [rsum(dqd[h] * qd[h]) - rkd[h] + jnp.where(last_row, jnp.sum(rkd[h]) + dgl[h], 0.0)
                   + rsum(m_tot[h]) + rsum(dkb_w[h] * kb[h]) for h in heads]
            dg = each(lambda a, b: _dot_hi(upper, jnp.where(lane == 1, a - b, 0.0)), dgc, col_sums)
            for h in heads:
                dq_ref[rows, cols[h]] = dq[h]
                dk_ref[rows, cols[h]] = dk_a[h] + dk_l[h] + dkb[h] * beta[h]
                dv_ref[rows, cols[h]] = drhs_u[h] * beta[h]
                dbeta = rsum(drhs_u[h] * v[h]) + rsum(dkb[h] * k[h])
                dhs_ref[h, rows, :] = jnp.where(lane == 0, dbeta, dg[h])
            return carry

        lax.fori_loop(0, n_chunk, chunk, 0)

    wide = hps * LANES
    blk = lambda off: pl.BlockSpec((rb, wide), lambda h, i: (n_blk - 1 - i, off + h))
    head = blk(0)
    hs_spec = pl.BlockSpec((hps, rb, LANES), lambda h, i: (h, n_blk - 1 - i, 0))
    full = jax.ShapeDtypeStruct((s_len, DN_HEADS * LANES), F32)
    return pl.pallas_call(
        body, name="dn_scan_bwd", grid=(groups, n_blk),
        in_specs=[blk(0), blk(groups), blk(2 * groups), head, head,
                  pl.BlockSpec((hps, rb, DN_CHUNK), lambda h, i: (h, n_blk - 1 - i, 0)), hs_spec,
                  pl.BlockSpec((hps, n_chunk, LANES, LANES), lambda h, i: (h, n_blk - 1 - i, 0, 0)), head],
        out_specs=[head, head, head, hs_spec],
        out_shape=[full, full, full, jax.ShapeDtypeStruct((DN_HEADS, s_len, LANES), F32)],
        scratch_shapes=[pltpu.VMEM((hps, LANES, LANES), F32)],
        compiler_params=_cparams(("parallel", "arbitrary")),
    )(qkv, qkv, qkv, u, w, t_inv, hs, states, d_o)


def dn_post(o, z, out_norm):
    def body(i, ins, outs, accs):
        gn = ins[2][...]
        for h in range(DN_HEADS):
            cols = slice(h * LANES, (h + 1) * LANES)
            xhat, _ = _rms_stats(ins[0][:, cols])
            outs[0][:, cols] = (xhat * gn * _silu(ins[1][:, cols])).astype(BF16)

    return row_call("dn_post", body, o.shape[0], ROW_TILE, [(o, "tile"), (z, "tile"), (out_norm, "full")],
                    [(o.shape, BF16)])[0]


def dn_post_bwd(d_og, o, z, out_norm):
    def body(i, ins, outs, accs):
        gn = ins[3][...]
        dgn = jnp.zeros((1, LANES), F32)
        for h in range(DN_HEADS):
            cols = slice(h * LANES, (h + 1) * LANES)
            dy, zh = ins[0][:, cols].astype(F32), ins[2][:, cols]
            xhat, r = _rms_stats(ins[1][:, cols])
            sz = _silu(zh)
            dgn = dgn + jnp.sum(dy * xhat * sz, axis=0, keepdims=True)
            outs[1][:, cols] = (dy * xhat * gn * _silu_grad(zh)).astype(BF16)
            dxhat = dy * gn * sz
            outs[0][:, cols] = r * (dxhat - xhat * jnp.mean(dxhat * xhat, axis=-1, keepdims=True))
        accs[0][...] += dgn

    return row_call("dn_post_bwd", body, o.shape[0], ROW_TILE,
                    [(d_og, "tile"), (o, "tile"), (z, "tile"), (out_norm, "full")],
                    [(o.shape, F32), (o.shape, BF16)], [(1, LANES)])


def dn_pre_bwd(dq, dk, dv, dhs, qkv_raw, ba, w_conv, gate):
    s_len = qkv_raw.shape[0]
    tm = min(DN_ROW_TILE, s_len)

    def body(i, ins, outs, accs):
        dq_ref, dk_ref, dv_ref, dhs_ref, x_ref, xp_ref, ba_ref, w_ref, gate_ref = ins
        dc_ref, dba_ref = outs

        def blk(cb, carry):
            cols = pl.ds(pl.multiple_of(cb * LANES, LANES), LANES)
            hcols = pl.ds(pl.multiple_of((cb & (DN_HEADS - 1)) * LANES, LANES), LANES)
            prev = jnp.where(i > 0, xp_ref[:, cols], 0.0)
            xcat = jnp.concatenate([prev, x_ref[:, cols]], axis=0)
            c = _conv_taps(xcat, w_ref, cols, DN_CONV, DN_HALO, tm)
            y = _silu(c)
            dy = jnp.where(cb < DN_HEADS, dq_ref[:, hcols],
                           jnp.where(cb < 2 * DN_HEADS, dk_ref[:, hcols], dv_ref[:, hcols]))
            rs = lax.rsqrt(jnp.sum(y * y, axis=-1, keepdims=True) + L2_EPS)
            fac = jnp.where(cb < DN_HEADS, DN_HEAD_DIM ** -0.5, 1.0)
            nrm = y * rs
            dn = dy * fac
            dy_norm = rs * (dn - nrm * jnp.sum(dn * nrm, axis=-1, keepdims=True))
            dc_ref[:, cols] = jnp.where(cb < 2 * DN_HEADS, dy_norm, dy) * _silu_grad(c)
            return carry

        lax.fori_loop(0, qkv_raw.shape[1] // LANES, blk, 0)

        lane = _iota((tm, LANES), 1)
        dbeta = jnp.zeros((tm, LANES), F32)
        dg = jnp.zeros((tm, LANES), F32)
        for h in range(DN_HEADS):
            dbeta = dbeta + jnp.where(lane == h, dhs_ref[h, :, 0:1], 0.0)
            dg = dg + jnp.where(lane == DN_HEADS + h, dhs_ref[h, :, 1:2], 0.0)
        bav = ba_ref[...]
        beta = _sigmoid(bav)
        ea = jnp.exp(gate_ref[0:1, :])
        pre = bav + gate_ref[1:2, :]
        g = -ea * _softplus(pre)
        da = dg * (-ea) * _sigmoid(pre)
        dba_ref[...] = (dbeta * beta * (1.0 - beta) + da).astype(BF16)
        rid = _iota((8, LANES), 0)
        accs[0][...] += (jnp.where(rid == 0, jnp.sum(dg * g, axis=0, keepdims=True), 0.0)
                         + jnp.where(rid == 1, jnp.sum(da, axis=0, keepdims=True), 0.0))

    return row_call("dn_pre_bwd", body, s_len, tm,
                    [(dq, "tile"), (dk, "tile"), (dv, "tile"), (dhs, "tile"), (qkv_raw, "tile"),
                     (qkv_raw, ("prev", DN_HALO)), (ba, "tile"), (w_conv, "full"), (gate, "full")],
                    [(qkv_raw.shape, F32), (ba.shape, BF16)], [(8, LANES)])


def dn_conv_bwd(dc, qkv_raw, w_conv):
    s_len = dc.shape[0]
    tm = min(DN_ROW_TILE, s_len)
    nt = s_len // tm

    def body(i, ins, outs, accs):
        dc_ref, dn_ref, x_ref, xp_ref, w_ref = ins

        def blk(cb, carry):
            cols = pl.ds(pl.multiple_of(cb * LANES, LANES), LANES)
            dy = dc_ref[:, cols]
            nxt = jnp.where(i < nt - 1, dn_ref[:, cols], 0.0)
            dcat = jnp.concatenate([dy, nxt], axis=0)
            outs[0][:, cols] = _conv_taps_bwd_x(dcat, w_ref, cols, DN_CONV, DN_HALO, tm).astype(BF16)
            prev = jnp.where(i > 0, xp_ref[:, cols], 0.0)
            xcat = jnp.concatenate([prev, x_ref[:, cols]], axis=0)
            accs[0][:, cols] += _conv_taps_bwd_w(dy, xcat, DN_CONV, DN_HALO, tm, 8)
            return carry

        lax.fori_loop(0, dc.shape[1] // LANES, blk, 0)

    return row_call("dn_conv_bwd", body, s_len, tm,
                    [(dc, "tile"), (dc, ("next", DN_HALO)), (qkv_raw, "tile"), (qkv_raw, ("prev", DN_HALO)),
                     (w_conv, "full")],
                    [(dc.shape, BF16)], [(8, dc.shape[1])])


def _glu(u_ref, cols, d):
    return u_ref[:, cols] * _sigmoid(u_ref[:, pl.ds(pl.multiple_of(d + cols.start, LANES), cols.size)])


def cv_core_fwd(u, w_dw, b_dw, ln_g, ln_b):
    s_len, d = u.shape[0], u.shape[1] // 2
    tm = min(CONV_ROW_TILE, s_len)

    def body(i, ins, outs, accs):
        u_ref, up_ref, w_ref, bdw_ref, g_ref, b_ref = ins
        s_ref, c_ref = outs

        def blk(cb, carry):
            cols = pl.ds(pl.multiple_of(cb * LANES, LANES), LANES)
            prev = jnp.where(i > 0, _glu(up_ref, cols, d), 0.0)
            xcat = jnp.concatenate([prev, _glu(u_ref, cols, d)], axis=0)
            c_ref[:, cols] = _conv_taps(xcat, w_ref, cols, CV_WIDTH, CV_HALO, tm) + bdw_ref[:, cols]
            return carry

        lax.fori_loop(0, d // LANES, blk, 0)
        c = c_ref[...]
        mu = jnp.mean(c, axis=-1, keepdims=True)
        xc = c - mu
        rstd = lax.rsqrt(jnp.mean(xc * xc, axis=-1, keepdims=True) + LN_EPS)
        s_ref[...] = _silu(xc * rstd * g_ref[...] + b_ref[...]).astype(BF16)

    return row_call("cv_core_fwd", body, s_len, tm,
                    [(u, "tile"), (u, ("prev", CV_HALO)), (w_dw, "full"), (b_dw, "full"), (ln_g, "full"),
                     (ln_b, "full")],
                    [((s_len, d), BF16), ((s_len, d), F32)])


def cv_ln_bwd(ds, c, ln_g, ln_b):
    def body(i, ins, outs, accs):
        cv, g = ins[1][...], ins[2][...]
        mu = jnp.mean(cv, axis=-1, keepdims=True)
        xc = cv - mu
        rstd = lax.rsqrt(jnp.mean(xc * xc, axis=-1, keepdims=True) + LN_EPS)
        xhat = xc * rstd
        dl = ins[0][...].astype(F32) * _silu_grad(xhat * g + ins[3][...])
        dxhat = dl * g
        dc = rstd * (dxhat - jnp.mean(dxhat, axis=-1, keepdims=True)
                     - xhat * jnp.mean(dxhat * xhat, axis=-1, keepdims=True))
        outs[0][...] = dc
        rid = _iota((8, cv.shape[1]), 0)
        accs[0][...] += (jnp.where(rid == 0, jnp.sum(dl * xhat, axis=0, keepdims=True), 0.0)
                         + jnp.where(rid == 1, jnp.sum(dl, axis=0, keepdims=True), 0.0)
                         + jnp.where(rid == 2, jnp.sum(dc, axis=0, keepdims=True), 0.0))

    return row_call("cv_ln_bwd", body, c.shape[0], ROW_TILE,
                    [(ds, "tile"), (c, "tile"), (ln_g, "full"), (ln_b, "full")], [(c.shape, F32)], [(8, c.shape[1])])


def cv_conv_bwd(dc, u, w_dw):
    s_len, d = dc.shape
    tm = min(CONV_ROW_TILE, s_len)
    nt = s_len // tm

    def body(i, ins, outs, accs):
        dc_ref, dn_ref, u_ref, up_ref, w_ref = ins

        def blk(cb, carry):
            cols = pl.ds(pl.multiple_of(cb * LANES, LANES), LANES)
            gcols = pl.ds(pl.multiple_of(d + cb * LANES, LANES), LANES)
            dy = dc_ref[:, cols]
            nxt = jnp.where(i < nt - 1, dn_ref[:, cols], 0.0)
            dgl = _conv_taps_bwd_x(jnp.concatenate([dy, nxt], axis=0), w_ref, cols, CV_WIDTH, CV_HALO, tm)
            u1, sg = u_ref[:, cols], _sigmoid(u_ref[:, gcols])
            du1 = dgl * sg
            du2 = dgl * u1 * sg * (1.0 - sg)
            outs[0][:, cols] = du1.astype(BF16)
            outs[0][:, gcols] = du2.astype(BF16)
            accs[1][:, cols] += jnp.sum(du1, axis=0, keepdims=True)
            accs[1][:, gcols] += jnp.sum(du2, axis=0, keepdims=True)
            prev = jnp.where(i > 0, _glu(up_ref, cols, d), 0.0)
            xcat = jnp.concatenate([prev, u1 * sg], axis=0)
            accs[0][:, cols] += _conv_taps_bwd_w(dy, xcat, CV_WIDTH, CV_HALO, tm, CV_HALO)
            return carry

        lax.fori_loop(0, d // LANES, blk, 0)

    return row_call("cv_conv_bwd", body, s_len, tm,
                    [(dc, "tile"), (dc, ("next", CV_HALO)), (u, "tile"), (u, ("prev", CV_HALO)), (w_dw, "full")],
                    [(u.shape, BF16)], [(CV_HALO, d), (1, 2 * d)])


def xa_core_fwd(name, q, kv):
    d = q.shape[1]

    def body(i, ins, outs, accs):
        for h in range(XA_HEADS):
            cols = slice(h * XA_HEAD_DIM, (h + 1) * XA_HEAD_DIM)
            vcols = slice(d + h * XA_HEAD_DIM, d + (h + 1) * XA_HEAD_DIM)
            s = _dot_nt(ins[0][:, cols], ins[1][:, cols]) * (XA_HEAD_DIM ** -0.5)
            e = jnp.exp(s - jnp.max(s, axis=-1, keepdims=True))
            p = e / jnp.sum(e, axis=-1, keepdims=True)
            outs[0][:, cols] = _dot(p, ins[1][:, vcols]).astype(BF16)

    return row_call(name, body, q.shape[0], ROW_TILE, [(q, "tile"), (kv, "full")], [(q.shape, BF16)])[0]


def xa_core_bwd(name, d_o, q, kv):
    d = q.shape[1]

    def body(i, ins, outs, accs):
        for h in range(XA_HEADS):
            cols = slice(h * XA_HEAD_DIM, (h + 1) * XA_HEAD_DIM)
            vcols = slice(d + h * XA_HEAD_DIM, d + (h + 1) * XA_HEAD_DIM)
            qh, kh, vh, doh = ins[1][:, cols], ins[2][:, cols], ins[2][:, vcols], ins[0][:, cols]
            s = _dot_nt(qh, kh) * (XA_HEAD_DIM ** -0.5)
            e = jnp.exp(s - jnp.max(s, axis=-1, keepdims=True))
            p = e / jnp.sum(e, axis=-1, keepdims=True)
            dp = _dot_nt(doh, vh)
            ds = p * (dp - jnp.sum(dp * p, axis=-1, keepdims=True)) * (XA_HEAD_DIM ** -0.5)
            outs[0][:, cols] = _dot(ds, kh).astype(BF16)
            accs[0][:, cols] += _dot_tn(ds, qh)
            accs[0][:, vcols] += _dot_tn(p, doh)

    return row_call(name, body, q.shape[0], ROW_TILE, [(d_o, "tile"), (q, "tile"), (kv, "full")],
                    [(q.shape, BF16)], [kv.shape])


def adamw(name, w, g, m, v):
    def body(i, ins, outs, accs):
        wv, gv = ins[0][...], ins[1][...]
        mn = ADAM_B1 * ins[2][...] + (1.0 - ADAM_B1) * gv
        vn = ADAM_B2 * ins[3][...] + (1.0 - ADAM_B2) * jnp.square(gv)
        m_hat = mn / (1.0 - ADAM_B1 ** ADAM_STEP)
        v_hat = vn / (1.0 - ADAM_B2 ** ADAM_STEP)
        outs[0][...] = -ADAM_LR * (m_hat / (jnp.sqrt(v_hat) + ADAM_EPS) + ADAM_WD * wv)
        outs[1][...] = mn
        outs[2][...] = vn

    return row_call(name, body, w.shape[0], ROW_TILE, [(w, "tile"), (g, "tile"), (m, "tile"), (v, "tile")],
                    [(w.shape, F32)] * 3)


def adamw_halves(name, w, g_mine, g_sibling, m, v, core):
    n_layers = len(g_mine)
    rows, cols = w.shape
    half_rows = rows // n_layers // 2
    tm = _pick_tile(half_rows, ADAMW_ROW_TILE)
    per_half = half_rows // tm

    def body(core_ref, w_ref, *rest):
        g_refs = rest[:2 * n_layers]
        m_ref, v_ref, g_out, d_out, m_out, v_out = rest[2 * n_layers:]
        i = pl.program_id(0)
        mine = ((i // per_half) % 2) == core_ref[0]
        layer = i // (2 * per_half)
        gv = jnp.where(mine, g_refs[0][...], g_refs[n_layers][...])
        for l in range(1, n_layers):
            gv = jnp.where(layer == l, jnp.where(mine, g_refs[l][...], g_refs[n_layers + l][...]), gv)
        mn = ADAM_B1 * m_ref[...] + (1.0 - ADAM_B1) * gv
        vn = ADAM_B2 * v_ref[...] + (1.0 - ADAM_B2) * jnp.square(gv)
        m_hat = mn / (1.0 - ADAM_B1 ** ADAM_STEP)
        v_hat = vn / (1.0 - ADAM_B2 ** ADAM_STEP)
        g_out[...] = gv
        d_out[...] = -ADAM_LR * (m_hat / (jnp.sqrt(v_hat) + ADAM_EPS) + ADAM_WD * w_ref[...])
        m_out[...] = mn
        v_out[...] = vn

    whole = pl.BlockSpec((tm, cols), lambda i, core_ref: (i, 0))

    def half(layer, own):
        def index(i, core_ref):
            used = (i // (2 * per_half) == layer) & ((((i // per_half) % 2) == core_ref[0]) == own)
            return (jnp.where(used, i % per_half, 0), 0)

        return pl.BlockSpec((tm, cols), index)

    halves = [half(l, True) for l in range(n_layers)] + [half(l, False) for l in range(n_layers)]
    return pl.pallas_call(
        body, name=name,
        grid_spec=pltpu.PrefetchScalarGridSpec(
            num_scalar_prefetch=1, grid=(2 * per_half * n_layers,),
            in_specs=[whole] + halves + [whole, whole], out_specs=[whole] * 4),
        out_shape=[jax.ShapeDtypeStruct(w.shape, F32)] * 4,
        compiler_params=_cparams(("parallel",)),
    )(core, w, *g_mine, *g_sibling, m, v)


HBM_SPEC = pl.BlockSpec(memory_space=pltpu.HBM)


def _position():
    return lax.axis_index("x"), lax.axis_index("y"), lax.axis_index("c")


def _other_chips(x, y):
    return [(1 - x, y), (x, 1 - y), (1 - x, 1 - y)]


def _row_chunks(rows):
    return rows // D2D_CHUNK_ROWS if rows % D2D_CHUNK_ROWS == 0 else 1


def _start_chunked(make, rows):
    k = _row_chunks(rows)
    for i in range(k):
        make(i * (rows // k), rows // k).start()


def gather_shards(packs):
    n = len(packs)

    def body(*refs):
        srcs, outs = refs[:n], refs[n:2 * n]
        send_sems, recv_sems = refs[2 * n:]
        x, y, c = _position()
        me = 2 * x + y
        chips = _other_chips(x, y)
        sibling = (x, y, 1 - c)

        def over_ici(a, j):
            px, py = chips[j]
            rows = srcs[a].shape[0] // 2
            return pltpu.make_async_remote_copy(
                src_ref=srcs[a].at[pl.ds(c * rows, rows), :], dst_ref=outs[a].at[me, pl.ds(c * rows, rows), :],
                send_sem=send_sems.at[a, j], recv_sem=recv_sems.at[a, j], device_id=(px, py, c), device_id_type=MESH)

        def landed(a, j):
            px, py = chips[j]
            rows = srcs[a].shape[0] // 2
            part = outs[a].at[2 * px + py, pl.ds(c * rows, rows), :]
            return pltpu.make_async_remote_copy(
                src_ref=part, dst_ref=part, send_sem=send_sems.at[a, j], recv_sem=recv_sems.at[a, j],
                device_id=(px, py, c), device_id_type=MESH)

        def over_d2d(a, j, cc, off, size):
            px, py = chips[j]
            rows = srcs[a].shape[0] // 2
            part = outs[a].at[2 * px + py, pl.ds(cc * rows + off, size), :]
            return pltpu.make_async_remote_copy(
                src_ref=part, dst_ref=part, send_sem=send_sems.at[a, 3 + j], recv_sem=recv_sems.at[a, 3 + j],
                device_id=sibling, device_id_type=MESH)

        for a in range(n):
            for j in range(3):
                over_ici(a, j).start()
        for a in range(n):
            for j in range(3):
                landed(a, j).wait_recv()
                _start_chunked(functools.partial(over_d2d, a, j, c), srcs[a].shape[0] // 2)
        for a in range(n):
            rows = srcs[a].shape[0] // 2
            for j in range(3):
                over_d2d(a, j, 1 - c, 0, rows).wait_recv()
                over_d2d(a, j, c, 0, rows).wait_send()
                over_ici(a, j).wait_send()

    return pl.pallas_call(
        body, name="gather_shards",
        in_specs=[HBM_SPEC] * n, out_specs=[HBM_SPEC] * n,
        out_shape=[jax.ShapeDtypeStruct((N_CHIPS,) + p.shape, p.dtype) for p in packs],
        scratch_shapes=[pltpu.SemaphoreType.DMA((n, 6)), pltpu.SemaphoreType.DMA((n, 6))],
    )(*packs)


def pair_split(name, packs):
    n = len(packs)

    def body(*refs):
        srcs, outs = refs[:n], refs[n:2 * n]
        send_sems, recv_sems = refs[2 * n:]
        x, y, c = _position()

        def remote(a, off, size):
            rows = srcs[a].shape[1] // 2
            return pltpu.make_async_remote_copy(
                src_ref=srcs[a].at[:, pl.ds((1 - c) * rows + off, size), :],
                dst_ref=outs[a].at[:, pl.ds(off, size), :],
                send_sem=send_sems.at[a], recv_sem=recv_sems.at[a], device_id=(x, y, 1 - c), device_id_type=MESH)

        for a in range(n):
            _start_chunked(functools.partial(remote, a), srcs[a].shape[1] // 2)
        for a in range(n):
            remote(a, 0, srcs[a].shape[1] // 2).wait()

    return pl.pallas_call(
        body, name=name, in_specs=[HBM_SPEC] * n, out_specs=[HBM_SPEC] * n,
        out_shape=[jax.ShapeDtypeStruct((p.shape[0], p.shape[1] // 2, p.shape[2]), p.dtype) for p in packs],
        scratch_shapes=[pltpu.SemaphoreType.DMA((n,)), pltpu.SemaphoreType.DMA((n,))],
    )(*packs)


def pair_join(name, halves):
    n = len(halves)

    def body(*refs):
        srcs, outs = refs[:n], refs[n:2 * n]
        send_sems, recv_sems = refs[2 * n:]
        x, y, c = _position()

        def remote(a, off, size):
            return pltpu.make_async_remote_copy(
                src_ref=srcs[a].at[pl.ds(off, size), :], dst_ref=outs[a].at[pl.ds(off, size), :],
                send_sem=send_sems.at[a], recv_sem=recv_sems.at[a], device_id=(x, y, 1 - c), device_id_type=MESH)

        for a in range(n):
            _start_chunked(functools.partial(remote, a), srcs[a].shape[0])
        for a in range(n):
            remote(a, 0, srcs[a].shape[0]).wait()

    return pl.pallas_call(
        body, name=name, in_specs=[HBM_SPEC] * n, out_specs=[HBM_SPEC] * n,
        out_shape=[jax.ShapeDtypeStruct(p.shape, p.dtype) for p in halves],
        scratch_shapes=[pltpu.SemaphoreType.DMA((n,)), pltpu.SemaphoreType.DMA((n,))],
    )(*halves)


SEM_SPEC = pl.BlockSpec(memory_space=pltpu.SEMAPHORE)
DATAFLOW = pltpu.SideEffectType.DATAFLOW_SIDE_EFFECTING


def _ici_copy(kind, srcs, lands, send_sems, recv_sems, a, j):
    x, y, c = _position()
    px, py = _other_chips(x, y)[j]
    if kind == "gather":
        rows = srcs[a].shape[0] // 2
        src = srcs[a].at[pl.ds(c * rows, rows), :]
        dst = lands[a].at[2 * x + y, pl.ds(c * rows, rows), :]
    else:
        src = srcs[a].at[2 * px + py]
        dst = lands[a].at[j]
    return pltpu.make_async_remote_copy(src_ref=src, dst_ref=dst, send_sem=send_sems, recv_sem=recv_sems,
                                        device_id=(px, py, c), device_id_type=MESH)


def ici_start(name, kind, srcs, land_shapes):
    n = len(srcs)
    lands = [pltpu.with_memory_space_constraint(lax.empty(shp, s.dtype), pltpu.HBM) for shp, s in zip(land_shapes, srcs)]

    def body(*refs):
        src_refs, land_refs = refs[:n], refs[n:2 * n]
        send_sems, recv_sems = refs[2 * n], refs[2 * n + 1]
        token = refs[-1]
        for a in range(n):
            for j in range(N_CHIPS - 1):
                _ici_copy(kind, src_refs, land_refs, send_sems, recv_sems, a, j).start()
        token[...] = jnp.zeros_like(token)

    sems = pltpu.SemaphoreType.DMA(())
    res = pl.pallas_call(
        body, name=name,
        out_shape=[sems, sems] + [pltpu.HBM(s.shape, s.dtype) for s in srcs]
        + [pltpu.HBM(l.shape, l.dtype) for l in lands] + [jax.ShapeDtypeStruct((8, LANES), F32)],
        in_specs=[HBM_SPEC] * (2 * n),
        out_specs=[SEM_SPEC, SEM_SPEC] + [HBM_SPEC] * (2 * n) + [pl.BlockSpec(memory_space=pltpu.VMEM)],
        input_output_aliases={i: 2 + i for i in range(2 * n)},
        compiler_params=pltpu.CompilerParams(has_side_effects=DATAFLOW),
    )(*[pltpu.with_memory_space_constraint(s, pltpu.HBM) for s in srcs], *lands)
    return res[0], res[1], list(res[2:2 + n]), list(res[2 + n:2 + 2 * n]), res[-1]


def ici_wait(name, kind, send_sems, recv_sems, srcs, lands, after):
    n = len(srcs)

    def body(*refs):
        src_refs, land_refs = refs[:n], refs[n:2 * n]
        send, recv = refs[2 * n], refs[2 * n + 1]
        for a in range(n):
            for j in range(N_CHIPS - 1):
                cp = _ici_copy(kind, src_refs, land_refs, send, recv, a, j)
                cp.wait_send()
                cp.wait_recv()

    res = pl.pallas_call(
        body, name=name,
        out_shape=[pltpu.HBM(s.shape, s.dtype) for s in srcs] + [pltpu.HBM(l.shape, l.dtype) for l in lands],
        in_specs=[HBM_SPEC] * (2 * n) + [SEM_SPEC, SEM_SPEC, pl.BlockSpec(memory_space=pl.ANY)],
        out_specs=[HBM_SPEC] * (2 * n),
        input_output_aliases={i: i for i in range(2 * n)},
        compiler_params=pltpu.CompilerParams(has_side_effects=DATAFLOW),
    )(*srcs, *lands, send_sems, recv_sems, after)
    return list(res[:n]), list(res[n:])


def pair_forward(gathered):
    n = len(gathered)

    def body(*refs):
        outs = refs[n:2 * n]
        send_sems, recv_sems = refs[2 * n:]
        x, y, c = _position()
        chips = _other_chips(x, y)

        def part(a, j, cc, off, size):
            px, py = chips[j]
            rows = outs[a].shape[1] // 2
            ref = outs[a].at[2 * px + py, pl.ds(cc * rows + off, size), :]
            return pltpu.make_async_remote_copy(
                src_ref=ref, dst_ref=ref, send_sem=send_sems.at[a, j], recv_sem=recv_sems.at[a, j],
                device_id=(x, y, 1 - c), device_id_type=MESH)

        for a in range(n):
            for j in range(N_CHIPS - 1):
                _start_chunked(functools.partial(part, a, j, c), outs[a].shape[1] // 2)
        for a in range(n):
            rows = outs[a].shape[1] // 2
            for j in range(N_CHIPS - 1):
                part(a, j, 1 - c, 0, rows).wait_recv()
                part(a, j, c, 0, rows).wait_send()

    return pl.pallas_call(
        body, name="pair_forward", in_specs=[HBM_SPEC] * n, out_specs=[HBM_SPEC] * n,
        out_shape=[jax.ShapeDtypeStruct(g.shape, g.dtype) for g in gathered],
        input_output_aliases={i: i for i in range(n)},
        scratch_shapes=[pltpu.SemaphoreType.DMA((n, N_CHIPS - 1)), pltpu.SemaphoreType.DMA((n, N_CHIPS - 1))],
    )(*gathered)


def all_sum_small(part):
    n_dev = 8
    rows = part.shape[0]

    def body(src, out, buf, send_sems, recv_sems):
        x, y, c = _position()
        me = 4 * x + 2 * y + c
        buf[me] = src[...]
        copies = []
        for k in range(1, n_dev):
            px, py, pc = x ^ ((k >> 2) & 1), y ^ ((k >> 1) & 1), c ^ (k & 1)
            cp = pltpu.make_async_remote_copy(
                src_ref=src, dst_ref=buf.at[me], send_sem=send_sems.at[k - 1], recv_sem=recv_sems.at[k - 1],
                device_id=(px, py, pc), device_id_type=MESH)
            cp.start()
            copies.append(cp)
        for cp in copies:
            cp.wait()
        acc = buf[0]
        for k in range(1, n_dev):
            acc = acc + buf[k]
        out[...] = acc

    return pl.pallas_call(
        body, name="all_sum_small",
        in_specs=[pl.BlockSpec(memory_space=pltpu.VMEM)], out_specs=pl.BlockSpec(memory_space=pltpu.VMEM),
        out_shape=jax.ShapeDtypeStruct(part.shape, F32),
        scratch_shapes=[pltpu.VMEM((n_dev, rows, part.shape[1]), F32),
                        pltpu.SemaphoreType.DMA((n_dev - 1,)), pltpu.SemaphoreType.DMA((n_dev - 1,))],
    )(part)


def add_pairs(name, src, theirs, core, out_dtype):
    slabs, rows, cols = theirs.shape
    tm = _pick_tile(rows, ROW_TILE)
    nb = rows // tm

    def body(core_ref, a_ref, b_ref, o_ref):
        o_ref[...] = (a_ref[...].astype(F32) + b_ref[...].astype(F32)).astype(out_dtype)

    return pl.pallas_call(
        body, name=name,
        grid_spec=pltpu.PrefetchScalarGridSpec(
            num_scalar_prefetch=1, grid=(slabs, nb),
            in_specs=[pl.BlockSpec((None, tm, cols), lambda s, i, core_ref: (s, core_ref[0] * nb + i, 0)),
                      pl.BlockSpec((None, tm, cols), lambda s, i, core_ref: (s, i, 0))],
            out_specs=pl.BlockSpec((None, tm, cols), lambda s, i, core_ref: (s, i, 0))),
        out_shape=jax.ShapeDtypeStruct(theirs.shape, out_dtype),
        compiler_params=_cparams(("parallel", "parallel")),
    )(core, src, theirs)


def add_four(name, src, theirs, chip):
    _, rows, cols = theirs.shape
    tm = _pick_tile(rows, ROW_TILE)

    def body(chip_ref, a_ref, b_ref, o_ref):
        acc = a_ref[...].astype(F32)
        for j in range(N_CHIPS - 1):
            acc = acc + b_ref[j].astype(F32)
        o_ref[...] = acc

    return pl.pallas_call(
        body, name=name,
        grid_spec=pltpu.PrefetchScalarGridSpec(
            num_scalar_prefetch=1, grid=(rows // tm,),
            in_specs=[pl.BlockSpec((None, tm, cols), lambda i, chip_ref: (chip_ref[0], i, 0)),
                      pl.BlockSpec((N_CHIPS - 1, tm, cols), lambda i, chip_ref: (0, i, 0))],
            out_specs=pl.BlockSpec((tm, cols), lambda i, chip_ref: (i, 0))),
        out_shape=jax.ShapeDtypeStruct((rows, cols), F32),
        compiler_params=_cparams(("parallel",)),
    )(chip, src, theirs)


PACK_COLS = 1024
SMALL_ROW_MULTIPLE = 32
BIG = ["dn_w_in", "dn_w_out", "cv_w_pw1", "cv_w_pw2", "xa_w_q", "xa_w_kv", "xa_w_o", "mlp_w_up", "mlp_w_down"]
SMALL = ["dn_w_conv", "cv_norm", "cv_b_pw1", "cv_w_dw", "cv_b_dw", "cv_ln_g", "cv_ln_b", "cv_b_pw2"]
SHARD_AXIS = {"dn_w_in": 2, "dn_w_conv": 2, "dn_w_out": 1, "cv_norm": 1, "cv_w_pw1": 2, "cv_b_pw1": 1,
              "cv_w_dw": 2, "cv_b_dw": 1, "cv_ln_g": 1, "cv_ln_b": 1, "cv_w_pw2": 1, "cv_b_pw2": 1,
              "xa_w_q": 1, "xa_w_kv": 2, "xa_w_o": 1, "mlp_w_up": 2, "mlp_w_down": 1}
REPLICATED = ["dn_norm", "dn_a_log", "dn_dt_bias", "dn_out_norm", "xa_norm", "xa_mem_norm", "mlp_norm", "final_norm"]


def _pack_rows(size):
    return -(-size // PACK_COLS)


SHARD_SHAPES = {
    "dn_w_in": (1, 1024, 1028), "dn_w_conv": (1, 4, 768), "dn_w_out": (1, 256, 1024), "cv_norm": (1, 256),
    "cv_w_pw1": (1, 1024, 512), "cv_b_pw1": (1, 512), "cv_w_dw": (1, 31, 256), "cv_b_dw": (1, 256),
    "cv_ln_g": (1, 256), "cv_ln_b": (1, 256), "cv_w_pw2": (1, 256, 1024), "cv_b_pw2": (1, 256),
    "xa_w_q": (2, 256, 1024), "xa_w_kv": (2, 1024, 512), "xa_w_o": (2, 256, 1024),
    "mlp_w_up": (2, 1024, 1024), "mlp_w_down": (2, 1024, 1024)}


def _shard_shape(nm):
    return SHARD_SHAPES[nm]


def _pack(tensors, names, dtype, row_multiple):
    pieces = []
    for nm in names:
        t = tensors[nm]
        flat = t.reshape(t.shape[0], -1) if t.ndim > len(_shard_shape(nm)) else t.reshape(1, -1)
        pad = _pack_rows(flat.shape[1]) * PACK_COLS - flat.shape[1]
        pieces.append(jnp.pad(flat.astype(dtype), ((0, 0), (0, pad))))
    cat = jnp.concatenate(pieces, axis=1)
    rows = cat.shape[1] // PACK_COLS
    total = -(-rows // row_multiple) * row_multiple
    cat = jnp.pad(cat, ((0, 0), (0, (total - rows) * PACK_COLS)))
    return cat.reshape(cat.shape[0], total, PACK_COLS)


def _unpack(pack, names):
    lead = pack.shape[:-2]
    flat = pack.reshape(lead + (-1,))
    out, off = {}, 0
    for nm in names:
        shp = _shard_shape(nm)
        size = 1
        for s in shp:
            size *= s
        out[nm] = flat[..., off:off + size].reshape(lead + shp)
        off += _pack_rows(size) * PACK_COLS
    return out


def _to_full(nm, stacked):
    ax = SHARD_AXIS[nm]
    moved = jnp.moveaxis(stacked, 0, ax)
    shp = list(_shard_shape(nm))
    shp[ax] *= N_CHIPS
    return moved.reshape(shp)


def _to_shards(nm, full):
    ax = SHARD_AXIS[nm]
    shp = list(_shard_shape(nm))
    split = full.reshape(shp[:ax] + [N_CHIPS, shp[ax]] + shp[ax + 1:])
    return jnp.moveaxis(split, ax, 0)


def _row(v):
    return v.reshape(1, -1)


class Stacked:
    def __init__(self, arr, split, layer):
        self.arr, self.kw = arr, dict(b_split=split, b_layer=layer)


def _grad_out(split):
    return dict(out_dtype=BF16, out_split=split, out_layer=(0, 1))


def _with_next(res, next_gain):
    return (res[0], res[1]) if next_gain is not None else (res, None)


def mlp_fwd(tag, h, g, w_up, w_down, n=None, next_gain=None):
    if n is None:
        n = rms_fwd(tag + "_norm", h, g)
    act = mm(tag + "_up", n, w_up.arr, out_dtype=BF16, epi=lambda acc: jnp.square(jnp.maximum(acc, 0.0)), **w_up.kw)
    out, n_next = _with_next(mm(tag + "_down", act, w_down.arr, tk=LONG_TILE_K, epi=lambda acc, res: acc + res,
                                epi_tiles=(h,), norm_gain=next_gain, **w_down.kw), next_gain)
    return out, n_next, (n, act)


def mlp_bwd(tag, dh, h, g, w_up, w_down, saved, after=None):
    n, act = saved
    dh, dh16 = dh
    dup = mm(tag + "_d_act", dh16, w_down.arr, tb=True, out_dtype=BF16, after=after,
             epi=lambda acc, t: acc * (2.0 * jnp.sqrt(t.astype(F32))), epi_tiles=(act,), **w_down.kw)
    dw_down = mm(tag + "_dw_down", act, dh16, ta=True, tk=GRAD_TILE_K, **_grad_out("rows"))
    dn = mm(tag + "_dn", dup, w_up.arr, tb=True, out_dtype=BF16, **w_up.kw)
    dw_up = mm(tag + "_dw_up", n, dup, ta=True, tk=GRAD_TILE_K, **_grad_out("cols"))
    dh_in, dh16_in, dg = rms_bwd(tag + "_norm_bwd", dn, h, g, dh)
    return (dh_in, dh16_in), dg, (dw_up, dw_down)


def xa_fwd(tag, h, mem, g, g_mem, w_q, w_kv, w_o, n=None, next_gain=None):
    if n is None:
        n = rms_fwd(tag + "_norm", h, g)
    mem_n = rms_fwd(tag + "_mem_norm", mem, g_mem)
    q = mm(tag + "_q", n, w_q.arr, out_dtype=BF16, **w_q.kw)
    kv = mm(tag + "_kv", mem_n, w_kv.arr, out_dtype=BF16, **w_kv.kw)
    o = xa_core_fwd(tag + "_core", q, kv)
    out, n_next = _with_next(mm(tag + "_o", o, w_o.arr, epi=lambda acc, res: acc + res, epi_tiles=(h,),
                                norm_gain=next_gain, **w_o.kw), next_gain)
    return out, n_next, (n, mem_n, q, kv, o)


def xa_bwd(tag, dh, h, mem, g, g_mem, w_q, w_kv, w_o, saved):
    n, mem_n, q, kv, o = saved
    dh, dh16 = dh
    d_o = mm(tag + "_d_o", dh16, w_o.arr, tb=True, out_dtype=BF16, **w_o.kw)
    dw_o = mm(tag + "_dw_o", o, dh16, ta=True, tk=GRAD_TILE_K, **_grad_out("rows"))
    dq, dkv = xa_core_bwd(tag + "_core_bwd", d_o, q, kv)
    dn = mm(tag + "_dn", dq, w_q.arr, tb=True, out_dtype=BF16, **w_q.kw)
    dw_q = mm(tag + "_dw_q", n, dq, ta=True, tk=GRAD_TILE_K, **_grad_out("rows"))
    dh_in, dh16_in, dg = rms_bwd(tag + "_norm_bwd", dn, h, g, dh)
    dw_kv = mm(tag + "_dw_kv", mem_n, dkv, ta=True, **_grad_out("cols"))
    dmem_n = mm(tag + "_dmem", dkv, w_kv.arr, tb=True, **w_kv.kw)
    dg_mem = mem_norm_bwd(tag + "_mem_norm_bwd", dmem_n, mem, g_mem)
    return (dh_in, dh16_in), dg, dg_mem, (dw_q, dw_kv, dw_o)


def _gate_tile(a_log, dt_bias):
    t = jnp.zeros((8, LANES), F32)
    t = t.at[0, DN_HEADS:2 * DN_HEADS].set(a_log.reshape(-1))
    return t.at[1, DN_HEADS:2 * DN_HEADS].set(dt_bias.reshape(-1))


def dn_fwd(h, g, w_qkv, w_z, w_ba, w_conv, gate, out_norm, w_out, next_gain=None):
    n = rms_fwd("dn_norm", h, g)
    qkv_raw = mm("dn_proj_qkv", n, w_qkv)
    z = mm("dn_proj_z", n, w_z)
    ba = mm("dn_proj_ba", n, w_ba)
    qkv, hs = dn_pre(qkv_raw, ba, w_conv, gate)
    u, w, t_inv = dn_solve(qkv, hs)
    o, states = dn_scan_fwd(qkv, u, w, hs)
    og = dn_post(o, z, out_norm)
    out, n_next = _with_next(mm("dn_out", og, w_out.arr, epi=lambda acc, res: acc + res, epi_tiles=(h,),
                                norm_gain=next_gain, **w_out.kw), next_gain)
    return out, n_next, (n, qkv_raw, z, ba, qkv, hs, u, w, t_inv, o, states, og)


def dn_bwd(dh, h, g, w_qkv, w_z, w_ba, w_conv, gate, out_norm, w_out, saved, after=None):
    n, qkv_raw, z, ba, qkv, hs, u, w, t_inv, o, states, og = saved
    dh, dh16 = dh
    d_og = mm("dn_d_og", dh16, w_out.arr, tb=True, out_dtype=BF16, after=after, **w_out.kw)
    dw_out = mm("dn_dw_out", og, dh16, ta=True, tk=GRAD_TILE_K, **_grad_out("rows"))
    d_o, dz, d_out_norm = dn_post_bwd(d_og, o, z, out_norm)
    dq, dk, dv, dhs = dn_scan_bwd(qkv, u, w, t_inv, hs, states, d_o)
    dc, dba, d_gate = dn_pre_bwd(dq, dk, dv, dhs, qkv_raw, ba, w_conv, gate)
    dqkv_raw, dw_conv = dn_conv_bwd(dc, qkv_raw, w_conv)
    dn = mm("dn_dn_qkv", dqkv_raw, w_qkv, tb=True, tk=w_qkv.shape[1])
    dn = mm("dn_dn_z", dz, w_z, tb=True, epi=lambda acc, t: acc + t, epi_tiles=(dn,))
    dn = mm("dn_dn_ba", dba, w_ba, tb=True, epi=lambda acc, t: acc + t, epi_tiles=(dn,))
    dw_qkv = mm("dn_dw_qkv", n, dqkv_raw, ta=True, tk=GRAD_TILE_K)
    dw_z = mm("dn_dw_z", n, dz, ta=True, tk=GRAD_TILE_K)
    dw_ba = mm("dn_dw_ba", n, dba, ta=True, tk=GRAD_TILE_K)
    dh_in, _, dg = rms_bwd("dn_norm_bwd", dn, h, g, dh)
    return dh_in, dg, dw_qkv, dw_z, dw_ba, dw_conv, d_gate, d_out_norm, dw_out


def cv_fwd(h, g, w_pw1, b_pw1, w_dw, b_dw, ln_g, ln_b, w_pw2, b_pw2, n=None, next_gain=None):
    if n is None:
        n = rms_fwd("cv_norm", h, g)
    u = mm("cv_pw1", n, w_pw1.arr, epi=lambda acc, b: acc + b, epi_rows=(b_pw1,), **w_pw1.kw)
    s, c = cv_core_fwd(u, w_dw, b_dw, ln_g, ln_b)
    out, n_next = _with_next(mm("cv_pw2", s, w_pw2.arr, epi=lambda acc, res, b: acc + res + b, epi_tiles=(h,),
                                epi_rows=(b_pw2,), norm_gain=next_gain, **w_pw2.kw), next_gain)
    return out, n_next, (n, u, s, c)


def cv_bwd(dh, h, g, w_pw1, w_dw, ln_g, ln_b, w_pw2, saved):
    n, u, s, c = saved
    dh, dh16 = dh
    ds = mm("cv_d_s", dh16, w_pw2.arr, tb=True, out_dtype=BF16, **w_pw2.kw)
    dw_pw2 = mm("cv_dw_pw2", s, dh16, ta=True, tk=GRAD_TILE_K, **_grad_out("rows"))
    db_pw2 = col_sum("cv_db_pw2", dh)
    dc, ln_acc = cv_ln_bwd(ds, c, ln_g, ln_b)
    du, dw_dw, db_pw1 = cv_conv_bwd(dc, u, w_dw)
    dn = mm("cv_dn", du, w_pw1.arr, tb=True, out_dtype=BF16, **w_pw1.kw)
    dw_pw1 = mm("cv_dw_pw1", n, du, ta=True, tk=GRAD_TILE_K, **_grad_out("cols"))
    dh_in, dh16_in, dg = rms_bwd("cv_norm_bwd", dn, h, g, dh)
    return (dh_in, dh16_in), dg, dw_pw1, db_pw1, dw_dw, ln_acc, dw_pw2, db_pw2


WEIGHTS = ["dn_norm", "dn_w_in", "dn_w_conv", "dn_a_log", "dn_dt_bias", "dn_out_norm", "dn_w_out", "cv_norm",
           "cv_w_pw1", "cv_b_pw1", "cv_w_dw", "cv_b_dw", "cv_ln_g", "cv_ln_b", "cv_w_pw2", "cv_b_pw2", "xa_norm",
           "xa_mem_norm", "xa_w_q", "xa_w_kv", "xa_w_o", "mlp_norm", "mlp_w_up", "mlp_w_down", "final_norm"]


def _as_2d(t):
    if t.ndim == 1:
        return t.reshape(1, -1)
    return t.reshape(-1, t.shape[-1])


def kernel(x, mem, dn_norm, dn_w_in, dn_w_conv, dn_a_log, dn_dt_bias, dn_out_norm, dn_w_out, cv_norm, cv_w_pw1, cv_b_pw1, cv_w_dw, cv_b_dw, cv_ln_g, cv_ln_b, cv_w_pw2, cv_b_pw2, xa_norm, xa_mem_norm, xa_w_q, xa_w_kv, xa_w_o, mlp_norm, mlp_w_up, mlp_w_down, final_norm, loss_target, m_dn_norm, m_dn_w_in, m_dn_w_conv, m_dn_a_log, m_dn_dt_bias, m_dn_out_norm, m_dn_w_out, m_cv_norm, m_cv_w_pw1, m_cv_b_pw1, m_cv_w_dw, m_cv_b_dw, m_cv_ln_g, m_cv_ln_b, m_cv_w_pw2, m_cv_b_pw2, m_xa_norm, m_xa_mem_norm, m_xa_w_q, m_xa_w_kv, m_xa_w_o, m_mlp_norm, m_mlp_w_up, m_mlp_w_down, m_final_norm, v_dn_norm, v_dn_w_in, v_dn_w_conv, v_dn_a_log, v_dn_dt_bias, v_dn_out_norm, v_dn_w_out, v_cv_norm, v_cv_w_pw1, v_cv_b_pw1, v_cv_w_dw, v_cv_b_dw, v_cv_ln_g, v_cv_ln_b, v_cv_w_pw2, v_cv_b_pw2, v_xa_norm, v_xa_mem_norm, v_xa_w_q, v_xa_w_kv, v_xa_w_o, v_mlp_norm, v_mlp_w_up, v_mlp_w_down, v_final_norm):
    args = dict(locals())
    wts = {nm: args[nm] for nm in WEIGHTS}
    mom = {nm: args["m_" + nm] for nm in WEIGHTS}
    var = {nm: args["v_" + nm] for nm in WEIGHTS}
    core = lax.axis_index("c").astype(jnp.int32).reshape(1)
    chip = (2 * lax.axis_index("x") + lax.axis_index("y")).astype(jnp.int32)
    def own_slab(got, src):
        return lax.dynamic_update_slice(got, src[None], (chip, 0, 0))

    shard2d = {nm: wts[nm].astype(BF16).reshape(-1, wts[nm].shape[-1]) for nm in BIG}
    first = ["dn_w_in", "dn_w_out"]
    later = [nm for nm in BIG if nm not in first]
    sources = [shard2d[nm] for nm in first] + [_pack(wts, SMALL, F32, SMALL_ROW_MULTIPLE)[0]]
    gathered = [own_slab(got, src) for got, src in zip(gather_shards(sources), sources)]
    stacked = {"dn_w_out": gathered[1].reshape((N_CHIPS,) + SHARD_SHAPES["dn_w_out"])}
    full = {nm: _to_full(nm, t) for nm, t in _unpack(gathered[2], SMALL).items()}
    full["dn_w_in"] = _to_full("dn_w_in", gathered[0].reshape((N_CHIPS,) + SHARD_SHAPES["dn_w_in"]))
    full.update({nm: wts[nm] for nm in REPLICATED})
    later_src = [shard2d[nm] for nm in later]
    g_send, g_recv, later_src, g_lands, started = ici_start(
        "gather_start", "gather", later_src, [(N_CHIPS,) + s.shape for s in later_src])
    full["dn_norm"] = full["dn_norm"] + started[0, 0]

    def rest_weights(after):
        srcs, lands = ici_wait("gather_wait", "gather", g_send, g_recv, later_src, g_lands, after)
        return {nm: own_slab(land, src).reshape((N_CHIPS,) + SHARD_SHAPES[nm])
                for nm, land, src in zip(later, pair_forward(lands), srcs)}

    pending = []

    def on_grads(items):
        tag = "_".join(sorted({str(layer) for _, layer, _ in items}))
        parts = [g.reshape(N_CHIPS, -1, g.shape[-1]) for _, _, g in items]
        theirs = pair_split("pair_split_" + tag, parts)
        pairs = [add_pairs("pair_add_%s%d" % (nm, layer), p, t, core, BF16)
                 for (nm, layer, _), p, t in zip(items, parts, theirs)]
        send, recv, pairs, lands, token = ici_start(
            "scatter_start_" + tag, "scatter", pairs, [(N_CHIPS - 1,) + p.shape[1:] for p in pairs])
        pending.append((tag, items, send, recv, pairs, lands))
        return token

    dh, grads, rep = local_step(x[0], mem[0], loss_target[0], stacked, full, rest_weights, on_grads)

    halves = {}
    last = [("dn_w_in", 0, _to_shards("dn_w_in", grads["dn_w_in"]).astype(BF16)), ("dn_w_out", 0, grads["dn_w_out"][0]),
            ("small", 0, _pack({nm: _to_shards(nm, grads[nm]) for nm in SMALL}, SMALL, F32, SMALL_ROW_MULTIPLE))]
    parts = [g.reshape(N_CHIPS, -1, g.shape[-1]) for _, _, g in last]
    theirs = pair_split("pair_split_last", parts)
    pairs = [add_pairs("pair_add_" + nm, p, t, core, p.dtype) for (nm, _, _), p, t in zip(last, parts, theirs)]
    l_send, l_recv, l_pairs, l_lands, l_started = ici_start(
        "scatter_start_last", "scatter", pairs, [(N_CHIPS - 1,) + p.shape[1:] for p in pairs])
    for tag, items, send, recv, pairs, lands in pending:
        pairs, lands = ici_wait("scatter_wait_" + tag, "scatter", send, recv, pairs, lands, l_started)
        for (nm, layer, _), p, o in zip(items, pairs, lands):
            halves[nm, layer] = add_four("chip_add_%s%d" % (nm, layer), p, o, chip.reshape(1))
    keys = sorted(halves)
    siblings = dict(zip(keys, pair_join("pair_join_early", [halves[k] for k in keys])))

    delta, new_m, new_v, red = {}, {}, {}, {}

    def big_adamw(nm):
        layers = range(wts[nm].shape[0])
        res = adamw_halves("adamw_" + nm, _as_2d(wts[nm]), [halves[nm, l] for l in layers],
                           [siblings[nm, l] for l in layers], _as_2d(mom[nm]), _as_2d(var[nm]), core)
        red[nm], delta[nm], new_m[nm], new_v[nm] = (r.reshape(wts[nm].shape) for r in res)

    early = [nm for nm in BIG if (nm, 0) in halves]
    for nm in early:
        big_adamw(nm)
    done = jnp.concatenate([new_v[nm].reshape(-1)[:1] for nm in early])
    l_pairs, l_lands = ici_wait("scatter_wait_last", "scatter", l_send, l_recv, l_pairs, l_lands, done)
    for (nm, layer, _), p, o in zip(last, l_pairs, l_lands):
        halves[nm, layer] = add_four("chip_add_" + nm, p, o, chip.reshape(1))
    keys = [(nm, layer) for nm, layer, _ in last]
    siblings.update(zip(keys, pair_join("pair_join_last", [halves[k] for k in keys])))
    south = core[0] == 0
    mine, theirs = halves["small", 0], siblings["small", 0]
    red.update(_unpack(jnp.concatenate([jnp.where(south, mine, theirs), jnp.where(south, theirs, mine)], axis=0),
                       SMALL))

    rep = all_sum_small(rep)
    red["dn_norm"] = rep[0:1]
    red["dn_a_log"] = rep[1:2, DN_HEADS:2 * DN_HEADS]
    red["dn_dt_bias"] = rep[2:3, DN_HEADS:2 * DN_HEADS]
    red["dn_out_norm"] = rep[3:4, :LANES]
    red["xa_norm"], red["xa_mem_norm"], red["mlp_norm"] = rep[4:6], rep[6:8], rep[8:10]
    red["final_norm"] = rep[10]
    loss = rep[11, 0]

    for nm in WEIGHTS:
        shp = wts[nm].shape
        if nm in early:
            continue
        if nm in BIG:
            big_adamw(nm)
            continue
        res = adamw("adamw_" + nm, _as_2d(wts[nm]), _as_2d(red[nm].reshape(shp)), _as_2d(mom[nm]), _as_2d(var[nm]))
        delta[nm], new_m[nm], new_v[nm] = (r.reshape(shp) for r in res)
        red[nm] = red[nm].reshape(shp)

    grad_x = dh[None]
    return (loss, grad_x, *[red[nm] for nm in WEIGHTS], *[delta[nm] for nm in WEIGHTS],
            *[new_m[nm] for nm in WEIGHTS], *[new_v[nm] for nm in WEIGHTS])


def local_step(h0, mem0, target, stacked, full, rest_weights=None, on_grads=None):
    d = h0.shape[1]
    dn_norm, dn_a_log, dn_dt_bias, dn_out_norm = (full[nm] for nm in REPLICATED[:4])
    xa_norm, xa_mem_norm, mlp_norm, final_norm = (full[nm] for nm in REPLICATED[4:])
    inner = DN_HEADS * DN_HEAD_DIM
    w_in = full["dn_w_in"][0]
    w_qkv, w_z = w_in[:, :3 * inner], w_in[:, 3 * inner:4 * inner]
    w_ba = jnp.pad(w_in[:, 4 * inner:], ((0, 0), (0, LANES - 2 * DN_HEADS)))
    w_conv = jnp.pad(full["dn_w_conv"][0], ((0, 8 - DN_CONV), (0, 0)))
    gate = _gate_tile(dn_a_log, dn_dt_bias)
    w_dw = jnp.pad(full["cv_w_dw"][0], ((0, CV_HALO - CV_WIDTH), (0, 0)))

    def sw(nm, layer):
        return Stacked(stacked[nm], "rows" if SHARD_AXIS[nm] == 1 else "cols", layer)

    dn_args = (_row(dn_norm), w_qkv, w_z, w_ba, w_conv, gate, _row(dn_out_norm), sw("dn_w_out", 0))
    h1, n, dn_saved = dn_fwd(h0, *dn_args, next_gain=_row(xa_norm[0]))
    if rest_weights is not None:
        stacked = {**stacked, **rest_weights(h1)}
    xa_args = [(_row(xa_norm[l]), _row(xa_mem_norm[l]), sw("xa_w_q", l), sw("xa_w_kv", l), sw("xa_w_o", l))
               for l in range(2)]
    mlp_args = [(_row(mlp_norm[l]), sw("mlp_w_up", l), sw("mlp_w_down", l)) for l in range(2)]
    cv_args = (_row(full["cv_norm"][0]), sw("cv_w_pw1", 0), full["cv_b_pw1"], w_dw, full["cv_b_dw"],
               full["cv_ln_g"], full["cv_ln_b"], sw("cv_w_pw2", 0), full["cv_b_pw2"])
    h2, n, xa0_saved = xa_fwd("xa0", h1, mem0, *xa_args[0], n=n, next_gain=mlp_args[0][0])
    h3, n, mlp0_saved = mlp_fwd("mlp0", h2, *mlp_args[0], n=n, next_gain=cv_args[0])
    h4, n, cv_saved = cv_fwd(h3, *cv_args, n=n, next_gain=xa_args[1][0])
    h5, n, xa1_saved = xa_fwd("xa1", h4, mem0, *xa_args[1], n=n, next_gain=mlp_args[1][0])
    h6, _, mlp1_saved = mlp_fwd("mlp1", h5, *mlp_args[1], n=n)

    dh32, dh16, loss_tile, d_final = loss_head("loss_head", h6, _row(final_norm), target)
    dh = (dh32, dh16)
    grads = {}
    dg_mlp, dg_xa, dg_xa_mem = [None, None], [None, None], [None, None]
    dw_mlp, dw_xa = [None, None], [None, None]
    mlp_names, xa_names = ("mlp_w_up", "mlp_w_down"), ("xa_w_q", "xa_w_kv", "xa_w_o")

    def announce(items):
        return None if on_grads is None else on_grads(items)

    dh, dg_mlp[1], dw_mlp[1] = mlp_bwd("mlp1", dh, h5, *mlp_args[1], mlp1_saved)
    dh, dg_xa[1], dg_xa_mem[1], dw_xa[1] = xa_bwd("xa1", dh, h4, mem0, *xa_args[1], xa1_saved)
    (dh, grads["cv_norm"], dw_pw1, grads["cv_b_pw1"], dw_dw, ln_acc, dw_pw2,
     grads["cv_b_pw2"]) = cv_bwd(dh, h3, cv_args[0], cv_args[1], w_dw, cv_args[5], cv_args[6], cv_args[7], cv_saved)
    after = announce([(nm, 1, g) for nm, g in zip(mlp_names + xa_names, dw_mlp[1] + dw_xa[1])]
                     + [("cv_w_pw1", 0, dw_pw1), ("cv_w_pw2", 0, dw_pw2)])
    dh, dg_mlp[0], dw_mlp[0] = mlp_bwd("mlp0", dh, h2, *mlp_args[0], mlp0_saved, after=after)
    dh, dg_xa[0], dg_xa_mem[0], dw_xa[0] = xa_bwd("xa0", dh, h1, mem0, *xa_args[0], xa0_saved)
    after = announce([(nm, 0, g) for nm, g in zip(mlp_names + xa_names, dw_mlp[0] + dw_xa[0])])
    dh, dg_dn, dw_qkv, dw_z, dw_ba, dw_conv, d_gate, d_out_norm, dw_out = dn_bwd(dh, h0, *dn_args, dn_saved,
                                                                                 after=after)

    grads["dn_w_in"] = jnp.concatenate([dw_qkv, dw_z, dw_ba[:, :2 * DN_HEADS]], axis=1)[None]
    grads["dn_w_conv"] = dw_conv[None, :DN_CONV]
    grads["dn_w_out"], grads["cv_w_pw1"], grads["cv_w_pw2"] = [dw_out], [dw_pw1], [dw_pw2]
    grads["cv_w_dw"] = dw_dw[None, :CV_WIDTH]
    grads["cv_ln_g"], grads["cv_ln_b"], grads["cv_b_dw"] = ln_acc[0:1], ln_acc[1:2], ln_acc[2:3]
    for i, nm in enumerate(mlp_names):
        grads[nm] = [dw_mlp[0][i], dw_mlp[1][i]]
    for i, nm in enumerate(xa_names):
        grads[nm] = [dw_xa[0][i], dw_xa[1][i]]

    rep = jnp.zeros((16, d), F32)
    rep = rep.at[0].set(dg_dn[0])
    rep = rep.at[1, :LANES].set(d_gate[0])
    rep = rep.at[2, :LANES].set(d_gate[1])
    rep = rep.at[3, :LANES].set(d_out_norm[0])
    rep = rep.at[4].set(dg_xa[0][0]).at[5].set(dg_xa[1][0])
    rep = rep.at[6].set(dg_xa_mem[0][0]).at[7].set(dg_xa_mem[1][0])
    rep = rep.at[8].set(dg_mlp[0][0]).at[9].set(dg_mlp[1][0])
    rep = rep.at[10].set(d_final[0])
    rep = rep.at[11, :LANES].set(loss_tile[0])
    return dh, grads, rep
```

```python
import functools

import jax
import jax.numpy as jnp
from jax import lax
from jax.experimental import pallas as pl
from jax.experimental.pallas import tpu as pltpu

F32 = jnp.float32
BF16 = jnp.bfloat16
HIGHEST = lax.Precision.HIGHEST
MESH = pl.DeviceIdType.MESH

D_MODEL = 1024
DN_HEADS = 8
DN_HEAD_DIM = 128
DN_CONV = 4
DN_CHUNK = 64
CV_WIDTH = 31
XA_HEADS = 4
XA_HEAD_DIM = 256
RMS_EPS = 1e-6
LN_EPS = 1e-5
L2_EPS = 1e-6

ADAM_LR = 0.001
ADAM_B1 = 0.9
ADAM_B2 = 0.999
ADAM_EPS = 1e-08
ADAM_WD = 0.01
ADAM_STEP = 10

LANES = 128
ROW_TILE = 512
CONV_ROW_TILE = 256
MM_TILE = 1024
GRAD_TILE_K = 4096
LONG_TILE_K = 2048
ADAMW_ROW_TILE = 256
DN_ROW_TILE = 256
CHUNK_SHIFT = 6
SOLVE_INTERLEAVE = 8
FWD_HEADS_PER_STEP = 8
BWD_HEADS_PER_STEP = 8
BWD_SCAN_ROWS = 256
DN_HALO = 8
CV_HALO = 32
VMEM_LIMIT = 48 * 1024 * 1024
N_CHIPS = 4
D2D_CHUNK_ROWS = 256


def _cparams(sem):
    return pltpu.CompilerParams(dimension_semantics=sem, vmem_limit_bytes=VMEM_LIMIT)


def _dot(a, b, dims=(((1,), (0,)), ((), ()))):
    return lax.dot_general(a.astype(BF16), b.astype(BF16), dims, preferred_element_type=F32)


def _dot_nt(a, b):
    return _dot(a, b, (((1,), (1,)), ((), ())))


def _dot_tn(a, b):
    return _dot(a, b, (((0,), (0,)), ((), ())))


def _dot_hi(a, b, dims=(((1,), (0,)), ((), ()))):
    return lax.dot_general(a.astype(F32), b.astype(F32), dims, precision=HIGHEST, preferred_element_type=F32)


def _dot_x3(a, b, dims=(((1,), (0,)), ((), ()))):
    a_hi, b_hi = a.astype(BF16), b.astype(BF16)
    a_lo = (a - a_hi.astype(F32)).astype(BF16)
    b_lo = (b - b_hi.astype(F32)).astype(BF16)

    def dot(p, q):
        return lax.dot_general(p, q, dims, preferred_element_type=F32)

    return dot(a_hi, b_hi) + (dot(a_hi, b_lo) + dot(a_lo, b_hi))


def _sigmoid(x):
    return 1.0 / (1.0 + jnp.exp(-x))


def _silu(x):
    return x * _sigmoid(x)


def _silu_grad(x):
    s = _sigmoid(x)
    return s * (1.0 + x * (1.0 - s))


def _softplus(x):
    return jnp.maximum(x, 0.0) + jnp.log(1.0 + jnp.exp(-jnp.abs(x)))


def _iota(shape, dim):
    return lax.broadcasted_iota(jnp.int32, shape, dim)


def _lane_col(vals, lane, idx):
    return jnp.sum(jnp.where(lane == idx, vals, 0.0), axis=1, keepdims=True)


def _pick_tile(rows, cap):
    best = rows
    for t in range(16, min(rows, cap) + 1, 16):
        if rows % t == 0:
            best = t
    return best


def _stacked_spec(shape, split, layer, rows, cols, block_index):
    r_shard, c_shard = shape[-2], shape[-1]
    if split == "rows" and rows > r_shard:
        assert rows % r_shard == 0 and c_shard % cols == 0
        chips = rows // r_shard

        def slabs(i, j, kk):
            bi, bj = block_index(i, j, kk)
            return (bi, layer, 0, bj)

        return pl.BlockSpec((chips, None, r_shard, cols), slabs), chips
    assert r_shard % rows == 0 and c_shard % cols == 0
    per_chip = (r_shard // rows) if split == "rows" else (c_shard // cols)

    def index(i, j, kk):
        bi, bj = block_index(i, j, kk)
        if split == "rows":
            return (bi // per_chip, layer, bi % per_chip, bj)
        return (bj // per_chip, layer, bi, bj % per_chip)

    return pl.BlockSpec((None, None, rows, cols), index), 1


def mm(name, a, b, *, ta=False, tb=False, out_dtype=F32, pro=None, epi=None, epi_tiles=(), epi_rows=(),
       tm=MM_TILE, tn=MM_TILE, tk=MM_TILE, b_split=None, b_layer=None, out_split=None, out_layer=None,
       after=None, norm_gain=None, norm_bwd=None):
    m, k = (a.shape[1], a.shape[0]) if ta else a.shape
    b_rows, b_cols = b.shape[-2], b.shape[-1]
    if b_split == "rows":
        b_rows *= N_CHIPS
    elif b_split == "cols":
        b_cols *= N_CHIPS
    n = b_rows if tb else b_cols
    assert (b_cols if tb else b_rows) == k
    tm, tn, tk = min(tm, m), min(tn, n), min(tk, k)
    if b_split == "cols":
        if tb:
            tk = min(tk, b.shape[-1])
        else:
            tn = min(tn, b.shape[-1])
    if out_split == "cols":
        tn = min(tn, n // N_CHIPS)
    assert m % tm == 0 and n % tn == 0 and k % tk == 0
    nk = k // tk
    a_spec = pl.BlockSpec((tk, tm), lambda i, j, kk: (kk, i)) if ta else pl.BlockSpec((tm, tk), lambda i, j, kk: (i, kk))
    b_block = (tn, tk) if tb else (tk, tn)
    b_index = (lambda i, j, kk: (j, kk)) if tb else (lambda i, j, kk: (kk, j))
    b_chips = o_chips = 1
    if b_split is None:
        b_spec = pl.BlockSpec(b_block, b_index)
    else:
        b_spec, b_chips = _stacked_spec(b.shape, b_split, b_layer, b_block[0], b_block[1], b_index)
    in_specs = [a_spec, b_spec]
    in_specs += [pl.BlockSpec((tm, tn), lambda i, j, kk: (i, j)) for _ in epi_tiles]
    in_specs += [pl.BlockSpec((1, tn), lambda i, j, kk: (0, j)) for _ in epi_rows]
    n_t, n_r = len(epi_tiles), len(epi_rows)
    dims = (((0 if ta else 1,), (1 if tb else 0,)), ((), ()))
    if out_split is None:
        out_shape = jax.ShapeDtypeStruct((m, n), out_dtype)
        out_spec = pl.BlockSpec((tm, tn), lambda i, j, kk: (i, j))
    else:
        shard = (m // N_CHIPS, n) if out_split == "rows" else (m, n // N_CHIPS)
        out_shape = jax.ShapeDtypeStruct((N_CHIPS, out_layer[1]) + shard, out_dtype)
        out_spec, o_chips = _stacked_spec(out_shape.shape, out_split, out_layer[0], tm, tn, lambda i, j, kk: (i, j))
    single_pass = nk == 1 and norm_gain is None and norm_bwd is None and b_chips == 1 and o_chips == 1
    extra = []
    row_spec = pl.BlockSpec((1, n), lambda i, j, kk: (0, 0))
    tile_spec = pl.BlockSpec((tm, tn), lambda i, j, kk: (i, j))
    if norm_gain is not None:
        assert tn == n and out_split is None
        extra.append(norm_gain)
        in_specs.append(row_spec)
        out_shape = [out_shape, jax.ShapeDtypeStruct((m, n), BF16)]
        out_spec = [out_spec, tile_spec]
    if norm_bwd is not None:
        assert tn == n and out_split is None and norm_gain is None
        extra += list(norm_bwd)
        in_specs += [tile_spec, row_spec, tile_spec]
        out_shape = [jax.ShapeDtypeStruct((m, n), F32), jax.ShapeDtypeStruct((m, n), BF16),
                     jax.ShapeDtypeStruct((1, n), F32)]
        out_spec = [tile_spec, tile_spec, row_spec]
    if after is not None:
        extra.append(after)
        in_specs.append(pl.BlockSpec(memory_space=pl.ANY))

    def body(a_ref, b_ref, *rest):
        tiles = rest[:n_t]
        rows = rest[n_t:n_t + n_r]
        gain_ref = rest[n_t + n_r] if norm_gain is not None else None
        bwd_refs = rest[n_t + n_r:n_t + n_r + 3] if norm_bwd is not None else None
        rest = rest[n_t + n_r + len(extra):]
        o_ref, acc_ref = rest[0], rest[-1]
        av = a_ref[...]
        if pro is not None:
            av = pro(av)
        if single_pass:
            out = _dot(av, b_ref[...], dims)
            if epi is not None:
                out = epi(out, *[t[...] for t in tiles], *[r[...] for r in rows])
            o_ref[...] = out.astype(out_dtype)
            return
        kk = pl.program_id(2)

        @pl.when(kk == 0)
        def _():
            acc_ref[...] = jnp.zeros_like(acc_ref)

        bv = b_ref[...]
        if b_chips > 1:
            bv = bv.reshape(b_block)
        acc_ref[...] += _dot(av, bv, dims)

        @pl.when(kk == nk - 1)
        def _():
            out = acc_ref[...]
            if epi is not None:
                out = epi(out, *[t[...] for t in tiles], *[r[...] for r in rows])
            if gain_ref is not None:
                rest[1][...] = (_rms_stats(out)[0] * gain_ref[...]).astype(BF16)
            if bwd_refs is not None:
                h_ref, g_ref, dres_ref = bwd_refs
                dh, dg = _rms_bwd_tile(out, h_ref[...], g_ref[...])
                total = dres_ref[...] + dh
                o_ref[...] = total
                rest[1][...] = total.astype(BF16)
                first = pl.program_id(0) == 0

                @pl.when(first)
                def _():
                    rest[2][...] = dg

                @pl.when(jnp.logical_not(first))
                def _():
                    rest[2][...] += dg

                return
            out = out.astype(out_dtype)
            o_ref[...] = out.reshape(o_chips, tm // o_chips, tn) if o_chips > 1 else out

    outer = "arbitrary" if norm_bwd is not None else "parallel"
    return pl.pallas_call(
        body, name=name, grid=(m // tm, n // tn, nk),
        in_specs=in_specs, out_specs=out_spec, out_shape=out_shape,
        scratch_shapes=[] if single_pass else [pltpu.VMEM((tm, tn), F32)],
        compiler_params=_cparams((outer, outer, "arbitrary")),
    )(a, b, *epi_tiles, *epi_rows, *extra)


def row_call(name, body, n_rows, tm, ins, outs, accs=()):
    tm = _pick_tile(n_rows, tm)
    in_specs = []
    for arr, kind in ins:
        if kind == "tile":
            if arr.ndim == 2:
                in_specs.append(pl.BlockSpec((tm, arr.shape[1]), lambda i: (i, 0)))
            else:
                in_specs.append(pl.BlockSpec((arr.shape[0], tm, arr.shape[2]), lambda i: (0, i, 0)))
        elif kind == "full":
            in_specs.append(pl.BlockSpec(arr.shape, functools.partial(lambda i, nd: (0,) * nd, nd=arr.ndim)))
        else:
            where, h = kind
            per = tm // h
            if where == "prev":
                in_specs.append(pl.BlockSpec((h, arr.shape[1]), functools.partial(
                    lambda i, per: (jnp.maximum(i * per - 1, 0), 0), per=per)))
            else:
                last = n_rows // h - 1
                in_specs.append(pl.BlockSpec((h, arr.shape[1]), functools.partial(
                    lambda i, per, last: (jnp.minimum((i + 1) * per, last), 0), per=per, last=last)))
    out_shape, out_specs = [], []
    for shape, dtype in outs:
        out_shape.append(jax.ShapeDtypeStruct(shape, dtype))
        if len(shape) == 2:
            out_specs.append(pl.BlockSpec((tm, shape[1]), lambda i: (i, 0)))
        else:
            out_specs.append(pl.BlockSpec((shape[0], tm, shape[2]), lambda i: (0, i, 0)))
    for shape in accs:
        out_shape.append(jax.ShapeDtypeStruct(shape, F32))
        out_specs.append(pl.BlockSpec(shape, lambda i: (0, 0)))
    n_in, n_out, n_acc = len(ins), len(outs), len(accs)

    def kern(*refs):
        i = pl.program_id(0)
        in_refs = refs[:n_in]
        out_refs = refs[n_in:n_in + n_out]
        acc_refs = refs[n_in + n_out:n_in + n_out + n_acc]
        if n_acc:
            @pl.when(i == 0)
            def _():
                for r in acc_refs:
                    r[...] = jnp.zeros_like(r)
        body(i, in_refs, out_refs, acc_refs)

    res = pl.pallas_call(
        kern, name=name, grid=(n_rows // tm,), in_specs=in_specs, out_specs=out_specs, out_shape=out_shape,
        compiler_params=_cparams(("arbitrary",) if n_acc else ("parallel",)),
    )(*[a for a, _ in ins])
    return list(res)


def _rms_stats(h):
    r = lax.rsqrt(jnp.mean(h * h, axis=-1, keepdims=True) + RMS_EPS)
    return h * r, r


def rms_fwd(name, h, g):
    def body(i, ins, outs, accs):
        xhat, _ = _rms_stats(ins[0][...])
        outs[0][...] = (xhat * ins[1][...]).astype(BF16)

    return row_call(name, body, h.shape[0], ROW_TILE, [(h, "tile"), (g, "full")], [(h.shape, BF16)])[0]


def _rms_bwd_tile(dn, h, g):
    xhat, r = _rms_stats(h)
    dxhat = dn * g
    dh = r * (dxhat - xhat * jnp.mean(dxhat * xhat, axis=-1, keepdims=True))
    dg = jnp.sum(dn * xhat, axis=0, keepdims=True)
    return dh, dg


def mem_norm_bwd(name, dn, mem, g):
    def body(i, ins, outs, accs):
        _, dg = _rms_bwd_tile(ins[0][...].astype(F32), ins[1][...], ins[2][...])
        accs[0][...] += dg

    return row_call(name, body, mem.shape[0], ROW_TILE, [(dn, "tile"), (mem, "tile"), (g, "full")], [],
                    [(1, mem.shape[1])])[0]


def loss_head(name, h, g, target):
    d = h.shape[1]

    def body(i, ins, outs, accs):
        hv, gv = ins[0][...], ins[1][...]
        xhat, _ = _rms_stats(hv)
        err = xhat * gv - ins[2][...]
        dy = err * (1.0 / d)
        dh, dg = _rms_bwd_tile(dy, hv, gv)
        outs[0][...] = dh
        outs[1][...] = dh.astype(BF16)
        accs[0][...] += jnp.full((8, LANES), 0.5 / d, F32) * jnp.sum(err * err)
        accs[1][...] += dg

    dh, dh16, loss, dg = row_call(name, body, h.shape[0], ROW_TILE, [(h, "tile"), (g, "full"), (target, "tile")],
                                  [(h.shape, F32), (h.shape, BF16)], [(8, LANES), (1, d)])
    return dh, dh16, loss, dg


def col_sum(name, x):
    def body(i, ins, outs, accs):
        accs[0][...] += jnp.sum(ins[0][...].astype(F32), axis=0, keepdims=True)

    return row_call(name, body, x.shape[0], ROW_TILE, [(x, "tile")], [], [(1, x.shape[1])])[0]


def _conv_taps(xcat, w_ref, cols, width, halo, tm):
    rows = halo + tm
    acc = None
    for j in range(width):
        s = width - 1 - j
        xs = xcat if s == 0 else pltpu.roll(xcat, s, 0)
        term = xs[halo:rows] * w_ref[j:j + 1, cols]
        acc = term if acc is None else acc + term
    return acc


def _conv_taps_bwd_x(dcat, w_ref, cols, width, halo, tm):
    rows = halo + tm
    acc = None
    for j in range(width):
        s = width - 1 - j
        ds = dcat if s == 0 else pltpu.roll(dcat, rows - s, 0)
        term = ds[0:tm] * w_ref[j:j + 1, cols]
        acc = term if acc is None else acc + term
    return acc


def _conv_taps_bwd_w(dy, xcat, width, halo, tm, wrows):
    rows = halo + tm
    rid = _iota((wrows, dy.shape[1]), 0)
    out = jnp.zeros((wrows, dy.shape[1]), F32)
    for j in range(width):
        s = width - 1 - j
        xs = xcat if s == 0 else pltpu.roll(xcat, s, 0)
        v = jnp.sum(dy * xs[halo:rows], axis=0, keepdims=True)
        out = out + jnp.where(rid == j, v, 0.0)
    return out


def dn_pre(qkv_raw, ba, w_conv, gate):
    s_len = qkv_raw.shape[0]
    tm = min(DN_ROW_TILE, s_len)
    n_blk = qkv_raw.shape[1] // LANES

    def body(i, ins, outs, accs):
        x_ref, xp_ref, ba_ref, w_ref, gate_ref = ins
        qkv_ref, hs_ref = outs

        def blk(cb, carry):
            cols = pl.ds(pl.multiple_of(cb * LANES, LANES), LANES)
            prev = jnp.where(i > 0, xp_ref[:, cols], 0.0)
            xcat = jnp.concatenate([prev, x_ref[:, cols]], axis=0)
            c = _conv_taps(xcat, w_ref, cols, DN_CONV, DN_HALO, tm)
            y = _silu(c)
            rs = lax.rsqrt(jnp.sum(y * y, axis=-1, keepdims=True) + L2_EPS)
            fac = jnp.where(cb < DN_HEADS, DN_HEAD_DIM ** -0.5, 1.0)
            qkv_ref[:, cols] = jnp.where(cb < 2 * DN_HEADS, y * (rs * fac), y)
            return carry

        lax.fori_loop(0, n_blk, blk, 0)

        bav = ba_ref[...]
        beta = _sigmoid(bav)
        g = -jnp.exp(gate_ref[0:1, :]) * _softplus(bav + gate_ref[1:2, :])
        lane = _iota((tm, LANES), 1)
        g = jnp.where((lane >= DN_HEADS) & (lane < 2 * DN_HEADS), g, 0.0)
        r = _iota((tm, tm), 0)
        c = _iota((tm, tm), 1)
        tri = jnp.where((r >= c) & ((r >> CHUNK_SHIFT) == (c >> CHUNK_SHIFT)), 1.0, 0.0)
        gc = _dot_hi(tri, g)
        for h in range(DN_HEADS):
            hs_ref[h] = jnp.where(lane == 0, _lane_col(beta, lane, h),
                                  jnp.where(lane == 1, _lane_col(g, lane, DN_HEADS + h),
                                            jnp.where(lane == 2, _lane_col(gc, lane, DN_HEADS + h), 0.0)))

    return row_call("dn_pre", body, s_len, tm,
                    [(qkv_raw, "tile"), (qkv_raw, ("prev", DN_HALO)), (ba, "tile"), (w_conv, "full"), (gate, "full")],
                    [(qkv_raw.shape, F32), ((DN_HEADS, s_len, LANES), F32)])


def _chunk_masks():
    r = _iota((DN_CHUNK, DN_CHUNK), 0)
    c = _iota((DN_CHUNK, DN_CHUNK), 1)
    return r, c


def _decay_matrix(gc, r, c):
    gc_row = jnp.sum(jnp.where(r == c, gc, 0.0), axis=0, keepdims=True)
    causal = r >= c
    return jnp.where(causal, jnp.exp(jnp.where(causal, gc - gc_row, 0.0)), 0.0)


def _tri_inverse(lows, r, c):
    eye = jnp.where(r == c, 1.0, 0.0)
    ts = [eye for _ in lows]
    b = 1
    while b < DN_CHUNK:
        shift = b.bit_length()
        sel = ((r >> shift) == (c >> shift)) & ((r & b) != 0) & ((c & b) == 0)
        lms = [jnp.where(sel, low, 0.0) for low in lows]
        if b == 1:
            ts = [t - lm for t, lm in zip(ts, lms)]
        else:
            t_lm = [_dot_x3(t, lm) for t, lm in zip(ts, lms)]
            t_lm_t = [_dot_x3(x, t) for x, t in zip(t_lm, ts)]
            ts = [t - x for t, x in zip(ts, t_lm_t)]
        b *= 2
    return ts


def dn_solve(qkv, hs):
    s_len = qkv.shape[0]
    rb = min(ROW_TILE, s_len)
    n_chunk = rb // DN_CHUNK
    interleave = min(SOLVE_INTERLEAVE, n_chunk)

    def body(k_ref, v_ref, hs_ref, u_ref, w_ref, t_ref):
        r, c = _chunk_masks()

        def group(gi, carry):
            rows = [pl.ds(pl.multiple_of((gi * interleave + j) * DN_CHUNK, DN_CHUNK), DN_CHUNK)
                    for j in range(interleave)]
            k = [k_ref[rw, :] for rw in rows]
            beta = [hs_ref[rw, 0:1] for rw in rows]
            gc = [hs_ref[rw, 2:3] for rw in rows]
            kb = [a * b for a, b in zip(k, beta)]
            decay = [_decay_matrix(g, r, c) for g in gc]
            lows = [jnp.where(r > c, _dot_nt(a, b) * d, 0.0) for a, b, d in zip(kb, k, decay)]
            ts = _tri_inverse(lows, r, c)
            us = [_dot_x3(t, v_ref[rw, :] * b) for t, rw, b in zip(ts, rows, beta)]
            ws = [_dot_x3(t, a * jnp.exp(g)) for t, a, g in zip(ts, kb, gc)]
            for j, rw in enumerate(rows):
                u_ref[rw, :] = us[j]
                w_ref[rw, :] = ws[j].astype(BF16)
                t_ref[rw, :] = ts[j]
            return carry

        lax.fori_loop(0, n_chunk // interleave, group, 0)

    return pl.pallas_call(
        body, name="dn_solve", grid=(DN_HEADS, s_len // rb),
        in_specs=[pl.BlockSpec((rb, LANES), lambda h, i: (i, DN_HEADS + h)),
                  pl.BlockSpec((rb, LANES), lambda h, i: (i, 2 * DN_HEADS + h)),
                  pl.BlockSpec((None, rb, LANES), lambda h, i: (h, i, 0))],
        out_specs=[pl.BlockSpec((rb, LANES), lambda h, i: (i, h)),
                   pl.BlockSpec((rb, LANES), lambda h, i: (i, h)),
                   pl.BlockSpec((None, rb, DN_CHUNK), lambda h, i: (h, i, 0))],
        out_shape=[jax.ShapeDtypeStruct((s_len, DN_HEADS * LANES), F32),
                   jax.ShapeDtypeStruct((s_len, DN_HEADS * LANES), BF16),
                   jax.ShapeDtypeStruct((DN_HEADS, s_len, DN_CHUNK), F32)],
        compiler_params=_cparams(("parallel", "parallel")),
    )(qkv, qkv, hs)


def dn_scan_fwd(qkv, u, w, hs):
    s_len = qkv.shape[0]
    rb = min(ROW_TILE, s_len)
    n_chunk = rb // DN_CHUNK
    total_chunks = s_len // DN_CHUNK

    hps = FWD_HEADS_PER_STEP
    groups = DN_HEADS // hps

    def body(q_ref, k_ref, u_ref, w_ref, hs_ref, o_ref, st_ref, state):
        @pl.when(pl.program_id(1) == 0)
        def _():
            state[...] = jnp.zeros_like(state)

        r, c = _chunk_masks()

        def chunk(n, carry):
            rows = pl.ds(pl.multiple_of(n * DN_CHUNK, DN_CHUNK), DN_CHUNK)
            heads = range(hps)
            cols = [slice(h * LANES, (h + 1) * LANES) for h in heads]
            each = lambda f, *xs: [f(*a) for a in zip(*xs)]
            q = [q_ref[rows, cl] for cl in cols]
            k = [k_ref[rows, cl] for cl in cols]
            gc = [hs_ref[h, rows, 2:3] for h in heads]
            st = [state[h] for h in heads]
            for h in heads:
                st_ref[h, n] = st[h]
            gl = each(lambda g: jnp.min(g, axis=0, keepdims=True), gc)
            decay = each(lambda g: _decay_matrix(g, r, c), gc)
            w_st = [_dot(w_ref[rows, cols[h]], st[h]) for h in heads]
            qk = each(_dot_nt, q, k)
            q_st = each(lambda a, g, s: _dot(a * jnp.exp(g), s), q, gc, st)
            vn = [u_ref[rows, cols[h]] - w_st[h] for h in heads]
            ai_vn = each(lambda a, d, b: _dot(a * d, b), qk, decay, vn)
            kd_vn = each(lambda a, g0, g, b: _dot_tn(a * jnp.exp(g0 - g), b), k, gl, gc, vn)
            for h in heads:
                o_ref[rows, cols[h]] = q_st[h] + ai_vn[h]
                state[h] = st[h] * jnp.exp(gl[h]) + kd_vn[h]
            return carry

        lax.fori_loop(0, n_chunk, chunk, 0)

    wide = hps * LANES
    blk = lambda off: pl.BlockSpec((rb, wide), lambda h, i: (i, off + h))
    return pl.pallas_call(
        body, name="dn_scan_fwd", grid=(groups, s_len // rb),
        in_specs=[blk(0), blk(groups), blk(0), blk(0),
                  pl.BlockSpec((hps, rb, LANES), lambda h, i: (h, i, 0))],
        out_specs=[blk(0),
                   pl.BlockSpec((hps, n_chunk, LANES, LANES), lambda h, i: (h, i, 0, 0))],
        out_shape=[jax.ShapeDtypeStruct((s_len, DN_HEADS * LANES), F32),
                   jax.ShapeDtypeStruct((DN_HEADS, total_chunks, LANES, LANES), F32)],
        scratch_shapes=[pltpu.VMEM((hps, LANES, LANES), F32)],
        compiler_params=_cparams(("parallel", "arbitrary")),
    )(qkv, qkv, u, w, hs)


def dn_scan_bwd(qkv, u, w, t_inv, hs, states, d_o):
    s_len = qkv.shape[0]
    rb = min(BWD_SCAN_ROWS, s_len)
    n_chunk = rb // DN_CHUNK
    n_blk = s_len // rb
    hps = BWD_HEADS_PER_STEP
    groups = DN_HEADS // hps

    def body(q_ref, k_ref, v_ref, u_ref, w_ref, t_ref, hs_ref, st_ref, do_ref,
             dq_ref, dk_ref, dv_ref, dhs_ref, dstate):
        @pl.when(pl.program_id(1) == 0)
        def _():
            dstate[...] = jnp.zeros_like(dstate)

        r, c = _chunk_masks()
        causal = r >= c
        strict = r > c
        lane = _iota((DN_CHUNK, LANES), 1)
        upper = jnp.where(r <= c, 1.0, 0.0)
        last_row = _iota((DN_CHUNK, 1), 0) == DN_CHUNK - 1

        def chunk(m, carry):
            n = n_chunk - 1 - m
            rows = pl.ds(pl.multiple_of(n * DN_CHUNK, DN_CHUNK), DN_CHUNK)
            heads = range(hps)
            cols = [slice(h * LANES, (h + 1) * LANES) for h in heads]
            each = lambda f, *xs: [f(*a) for a in zip(*xs)]
            rsum = lambda x: jnp.sum(x, axis=-1, keepdims=True)
            dims_tn = (((0,), (0,)), ((), ()))
            q = [q_ref[rows, cl] for cl in cols]
            k = [k_ref[rows, cl] for cl in cols]
            v = [v_ref[rows, cl] for cl in cols]
            uu = [u_ref[rows, cl] for cl in cols]
            ww = [w_ref[rows, cl] for cl in cols]
            do = [do_ref[rows, cl] for cl in cols]
            tt = [t_ref[h, rows, :] for h in heads]
            beta = [hs_ref[h, rows, 0:1] for h in heads]
            gc = [hs_ref[h, rows, 2:3] for h in heads]
            st = [st_ref[h, n] for h in heads]
            dst = [dstate[h] for h in heads]
            gl = each(lambda g: jnp.min(g, axis=0, keepdims=True), gc)
            egc = each(jnp.exp, gc)
            egl = each(jnp.exp, gl)
            ekd = each(lambda a, b: jnp.exp(a - b), gl, gc)
            decay = each(lambda g: _decay_matrix(g, r, c), gc)
            qd = each(jnp.multiply, q, egc)
            kd = each(jnp.multiply, k, ekd)
            kb = each(jnp.multiply, k, beta)
            w_st = each(_dot, ww, st)
            qk = each(_dot_nt, q, k)
            dqd = each(_dot_nt, do, st)
            kd_dst = each(_dot, kd, dst)
            qd_do = each(_dot_tn, qd, do)
            kbk = each(_dot_nt, kb, k)
            vn = each(jnp.subtract, uu, w_st)
            ai = each(jnp.multiply, qk, decay)
            low = each(lambda a, d: jnp.where(strict, a * d, 0.0), kbk, decay)
            dai = each(lambda a, b: jnp.where(causal, _dot_nt(a, b), 0.0), do, vn)
            ai_do = each(_dot_tn, ai, do)
            dkd = each(_dot_nt, vn, dst)
            dvn = each(jnp.add, ai_do, kd_dst)
            dp = each(jnp.multiply, dai, decay)
            dw = each(lambda a, b: -_dot_nt(a, b), dvn, st)
            w_dvn = each(_dot_tn, ww, dvn)
            dp_k = each(_dot, dp, k)
            dp_q = each(_dot_tn, dp, q)
            drhs_u = each(lambda a, b: _dot_x3(a, b, dims_tn), tt, dvn)
            dgl = each(lambda a, b, e: jnp.sum(a * b) * e, dst, st, egl)
            for h in heads:
                dstate[h] = dst[h] * egl[h] + qd_do[h] - w_dvn[h]
            dq = each(lambda a, e, b: a * e + b, dqd, egc, dp_k)
            dk_a = each(lambda a, e, b: a * e + b, dkd, ekd, dp_q)
            rkd = each(lambda a, b: rsum(a * b), dkd, kd)
            drhs_w = each(lambda a, b: _dot_x3(a, b, dims_tn), tt, dw)
            dl_u = each(_dot_nt, drhs_u, uu)
            dl_w = each(_dot_nt, drhs_w, ww)
            dlow = each(lambda a, b: jnp.where(strict, -(a + b), 0.0), dl_u, dl_w)
            dqm = each(jnp.multiply, dlow, decay)
            m_tot = each(lambda a, b, d, e: a * b + d * e, dai, ai, dlow, low)
            dqm_k = each(_dot, dqm, k)
            dk_l = each(_dot_tn, dqm, kb)
            col_rows = each(lambda m: jnp.sum(m, axis=0, keepdims=True), m_tot)
            col_sums = each(lambda rw: jnp.sum(jnp.where(r == c, rw, 0.0), axis=1, keepdims=True), col_rows)
            dkb_w = each(jnp.multiply, drhs_w, egc)
            dkb = each(jnp.add, dkb_w, dqm_k)
            dgc = [rsum(dqd[h] * qd[h]) - rkd[h] + jnp.where(last_row, jnp.sum(rkd[h]) + dgl[h], 0.0)
                   + rsum(m_tot[h]) + rsum(dkb_w[h] * kb[h]) for h in heads]
            dg = each(lambda a, b: _dot_hi(upper, jnp.where(lane == 1, a - b, 0.0)), dgc, col_sums)
            for h in heads:
                dq_ref[rows, cols[h]] = dq[h]
                dk_ref[rows, cols[h]] = dk_a[h] + dk_l[h] + dkb[h] * beta[h]
                dv_ref[rows, cols[h]] = drhs_u[h] * beta[h]
                dbeta = rsum(drhs_u[h] * v[h]) + rsum(dkb[h] * k[h])
                dhs_ref[h, rows, :] = jnp.where(lane == 0, dbeta, dg[h])
            return carry

        lax.fori_loop(0, n_chunk, chunk, 0)

    wide = hps * LANES
    blk = lambda off: pl.BlockSpec((rb, wide), lambda h, i: (n_blk - 1 - i, off + h))
    head = blk(0)
    hs_spec = pl.BlockSpec((hps, rb, LANES), lambda h, i: (h, n_blk - 1 - i, 0))
    full = jax.ShapeDtypeStruct((s_len, DN_HEADS * LANES), F32)
    return pl.pallas_call(
        body, name="dn_scan_bwd", grid=(groups, n_blk),
        in_specs=[blk(0), blk(groups), blk(2 * groups), head, head,
                  pl.BlockSpec((hps, rb, DN_CHUNK), lambda h, i: (h, n_blk - 1 - i, 0)), hs_spec,
                  pl.BlockSpec((hps, n_chunk, LANES, LANES), lambda h, i: (h, n_blk - 1 - i, 0, 0)), head],
        out_specs=[head, head, head, hs_spec],
        out_shape=[full, full, full, jax.ShapeDtypeStruct((DN_HEADS, s_len, LANES), F32)],
        scratch_shapes=[pltpu.VMEM((hps, LANES, LANES), F32)],
        compiler_params=_cparams(("parallel", "arbitrary")),
    )(qkv, qkv, qkv, u, w, t_inv, hs, states, d_o)


def dn_post(o, z, out_norm):
    def body(i, ins, outs, accs):
        gn = ins[2][...]
        for h in range(DN_HEADS):
            cols = slice(h * LANES, (h + 1) * LANES)
            xhat, _ = _rms_stats(ins[0][:, cols])
            outs[0][:, cols] = (xhat * gn * _silu(ins[1][:, cols])).astype(BF16)

    return row_call("dn_post", body, o.shape[0], ROW_TILE, [(o, "tile"), (z, "tile"), (out_norm, "full")],
                    [(o.shape, BF16)])[0]


def dn_post_bwd(d_og, o, z, out_norm):
    def body(i, ins, outs, accs):
        gn = ins[3][...]
        dgn = jnp.zeros((1, LANES), F32)
        for h in range(DN_HEADS):
            cols = slice(h * LANES, (h + 1) * LANES)
            dy, zh = ins[0][:, cols].astype(F32), ins[2][:, cols]
            xhat, r = _rms_stats(ins[1][:, cols])
            sz = _silu(zh)
            dgn = dgn + jnp.sum(dy * xhat * sz, axis=0, keepdims=True)
            outs[1][:, cols] = (dy * xhat * gn * _silu_grad(zh)).astype(BF16)
            dxhat = dy * gn * sz
            outs[0][:, cols] = r * (dxhat - xhat * jnp.mean(dxhat * xhat, axis=-1, keepdims=True))
        accs[0][...] += dgn

    return row_call("dn_post_bwd", body, o.shape[0], ROW_TILE,
                    [(d_og, "tile"), (o, "tile"), (z, "tile"), (out_norm, "full")],
                    [(o.shape, F32), (o.shape, BF16)], [(1, LANES)])


def dn_pre_bwd(dq, dk, dv, dhs, qkv_raw, ba, w_conv, gate):
    s_len = qkv_raw.shape[0]
    tm = min(DN_ROW_TILE, s_len)

    def body(i, ins, outs, accs):
        dq_ref, dk_ref, dv_ref, dhs_ref, x_ref, xp_ref, ba_ref, w_ref, gate_ref = ins
        dc_ref, dba_ref = outs

        def blk(cb, carry):
            cols = pl.ds(pl.multiple_of(cb * LANES, LANES), LANES)
            hcols = pl.ds(pl.multiple_of((cb & (DN_HEADS - 1)) * LANES, LANES), LANES)
            prev = jnp.where(i > 0, xp_ref[:, cols], 0.0)
            xcat = jnp.concatenate([prev, x_ref[:, cols]], axis=0)
            c = _conv_taps(xcat, w_ref, cols, DN_CONV, DN_HALO, tm)
            y = _silu(c)
            dy = jnp.where(cb < DN_HEADS, dq_ref[:, hcols],
                           jnp.where(cb < 2 * DN_HEADS, dk_ref[:, hcols], dv_ref[:, hcols]))
            rs = lax.rsqrt(jnp.sum(y * y, axis=-1, keepdims=True) + L2_EPS)
            fac = jnp.where(cb < DN_HEADS, DN_HEAD_DIM ** -0.5, 1.0)
            nrm = y * rs
            dn = dy * fac
            dy_norm = rs * (dn - nrm * jnp.sum(dn * nrm, axis=-1, keepdims=True))
            dc_ref[:, cols] = jnp.where(cb < 2 * DN_HEADS, dy_norm, dy) * _silu_grad(c)
            return carry

        lax.fori_loop(0, qkv_raw.shape[1] // LANES, blk, 0)

        lane = _iota((tm, LANES), 1)
        dbeta = jnp.zeros((tm, LANES), F32)
        dg = jnp.zeros((tm, LANES), F32)
        for h in range(DN_HEADS):
            dbeta = dbeta + jnp.where(lane == h, dhs_ref[h, :, 0:1], 0.0)
            dg = dg + jnp.where(lane == DN_HEADS + h, dhs_ref[h, :, 1:2], 0.0)
        bav = ba_ref[...]
        beta = _sigmoid(bav)
        ea = jnp.exp(gate_ref[0:1, :])
        pre = bav + gate_ref[1:2, :]
        g = -ea * _softplus(pre)
        da = dg * (-ea) * _sigmoid(pre)
        dba_ref[...] = (dbeta * beta * (1.0 - beta) + da).astype(BF16)
        rid = _iota((8, LANES), 0)
        accs[0][...] += (jnp.where(rid == 0, jnp.sum(dg * g, axis=0, keepdims=True), 0.0)
                         + jnp.where(rid == 1, jnp.sum(da, axis=0, keepdims=True), 0.0))

    return row_call("dn_pre_bwd", body, s_len, tm,
                    [(dq, "tile"), (dk, "tile"), (dv, "tile"), (dhs, "tile"), (qkv_raw, "tile"),
                     (qkv_raw, ("prev", DN_HALO)), (ba, "tile"), (w_conv, "full"), (gate, "full")],
                    [(qkv_raw.shape, F32), (ba.shape, BF16)], [(8, LANES)])


def dn_conv_bwd(dc, qkv_raw, w_conv):
    s_len = dc.shape[0]
    tm = min(DN_ROW_TILE, s_len)
    nt = s_len // tm

    def body(i, ins, outs, accs):
        dc_ref, dn_ref, x_ref, xp_ref, w_ref = ins

        def blk(cb, carry):
            cols = pl.ds(pl.multiple_of(cb * LANES, LANES), LANES)
            dy = dc_ref[:, cols]
            nxt = jnp.where(i < nt - 1, dn_ref[:, cols], 0.0)
            dcat = jnp.concatenate([dy, nxt], axis=0)
            outs[0][:, cols] = _conv_taps_bwd_x(dcat, w_ref, cols, DN_CONV, DN_HALO, tm).astype(BF16)
            prev = jnp.where(i > 0, xp_ref[:, cols], 0.0)
            xcat = jnp.concatenate([prev, x_ref[:, cols]], axis=0)
            accs[0][:, cols] += _conv_taps_bwd_w(dy, xcat, DN_CONV, DN_HALO, tm, 8)
            return carry

        lax.fori_loop(0, dc.shape[1] // LANES, blk, 0)

    return row_call("dn_conv_bwd", body, s_len, tm,
                    [(dc, "tile"), (dc, ("next", DN_HALO)), (qkv_raw, "tile"), (qkv_raw, ("prev", DN_HALO)),
                     (w_conv, "full")],
                    [(dc.shape, BF16)], [(8, dc.shape[1])])


def _glu(u_ref, cols, d):
    return u_ref[:, cols] * _sigmoid(u_ref[:, pl.ds(pl.multiple_of(d + cols.start, LANES), cols.size)])


def cv_core_fwd(u, w_dw, b_dw, ln_g, ln_b):
    s_len, d = u.shape[0], u.shape[1] // 2
    tm = min(CONV_ROW_TILE, s_len)

    def body(i, ins, outs, accs):
        u_ref, up_ref, w_ref, bdw_ref, g_ref, b_ref = ins
        s_ref, c_ref = outs

        def blk(cb, carry):
            cols = pl.ds(pl.multiple_of(cb * LANES, LANES), LANES)
            prev = jnp.where(i > 0, _glu(up_ref, cols, d), 0.0)
            xcat = jnp.concatenate([prev, _glu(u_ref, cols, d)], axis=0)
            c_ref[:, cols] = _conv_taps(xcat, w_ref, cols, CV_WIDTH, CV_HALO, tm) + bdw_ref[:, cols]
            return carry

        lax.fori_loop(0, d // LANES, blk, 0)
        c = c_ref[...]
        mu = jnp.mean(c, axis=-1, keepdims=True)
        xc = c - mu
        rstd = lax.rsqrt(jnp.mean(xc * xc, axis=-1, keepdims=True) + LN_EPS)
        s_ref[...] = _silu(xc * rstd * g_ref[...] + b_ref[...]).astype(BF16)

    return row_call("cv_core_fwd", body, s_len, tm,
                    [(u, "tile"), (u, ("prev", CV_HALO)), (w_dw, "full"), (b_dw, "full"), (ln_g, "full"),
                     (ln_b, "full")],
                    [((s_len, d), BF16), ((s_len, d), F32)])


def cv_ln_bwd(ds, c, ln_g, ln_b):
    def body(i, ins, outs, accs):
        cv, g = ins[1][...], ins[2][...]
        mu = jnp.mean(cv, axis=-1, keepdims=True)
        xc = cv - mu
        rstd = lax.rsqrt(jnp.mean(xc * xc, axis=-1, keepdims=True) + LN_EPS)
        xhat = xc * rstd
        dl = ins[0][...].astype(F32) * _silu_grad(xhat * g + ins[3][...])
        dxhat = dl * g
        dc = rstd * (dxhat - jnp.mean(dxhat, axis=-1, keepdims=True)
                     - xhat * jnp.mean(dxhat * xhat, axis=-1, keepdims=True))
        outs[0][...] = dc
        rid = _iota((8, cv.shape[1]), 0)
        accs[0][...] += (jnp.where(rid == 0, jnp.sum(dl * xhat, axis=0, keepdims=True), 0.0)
                         + jnp.where(rid == 1, jnp.sum(dl, axis=0, keepdims=True), 0.0)
                         + jnp.where(rid == 2, jnp.sum(dc, axis=0, keepdims=True), 0.0))

    return row_call("cv_ln_bwd", body, c.shape[0], ROW_TILE,
                    [(ds, "tile"), (c, "tile"), (ln_g, "full"), (ln_b, "full")], [(c.shape, F32)], [(8, c.shape[1])])


def cv_conv_bwd(dc, u, w_dw):
    s_len, d = dc.shape
    tm = min(CONV_ROW_TILE, s_len)
    nt = s_len // tm

    def body(i, ins, outs, accs):
        dc_ref, dn_ref, u_ref, up_ref, w_ref = ins

        def blk(cb, carry):
            cols = pl.ds(pl.multiple_of(cb * LANES, LANES), LANES)
            gcols = pl.ds(pl.multiple_of(d + cb * LANES, LANES), LANES)
            dy = dc_ref[:, cols]
            nxt = jnp.where(i < nt - 1, dn_ref[:, cols], 0.0)
            dgl = _conv_taps_bwd_x(jnp.concatenate([dy, nxt], axis=0), w_ref, cols, CV_WIDTH, CV_HALO, tm)
            u1, sg = u_ref[:, cols], _sigmoid(u_ref[:, gcols])
            du1 = dgl * sg
            du2 = dgl * u1 * sg * (1.0 - sg)
            outs[0][:, cols] = du1.astype(BF16)
            outs[0][:, gcols] = du2.astype(BF16)
            accs[1][:, cols] += jnp.sum(du1, axis=0, keepdims=True)
            accs[1][:, gcols] += jnp.sum(du2, axis=0, keepdims=True)
            prev = jnp.where(i > 0, _glu(up_ref, cols, d), 0.0)
            xcat = jnp.concatenate([prev, u1 * sg], axis=0)
            accs[0][:, cols] += _conv_taps_bwd_w(dy, xcat, CV_WIDTH, CV_HALO, tm, CV_HALO)
            return carry

        lax.fori_loop(0, d // LANES, blk, 0)

    return row_call("cv_conv_bwd", body, s_len, tm,
                    [(dc, "tile"), (dc, ("next", CV_HALO)), (u, "tile"), (u, ("prev", CV_HALO)), (w_dw, "full")],
                    [(u.shape, BF16)], [(CV_HALO, d), (1, 2 * d)])


def xa_core_fwd(name, q, kv):
    d = q.shape[1]

    def body(i, ins, outs, accs):
        for h in range(XA_HEADS):
            cols = slice(h * XA_HEAD_DIM, (h + 1) * XA_HEAD_DIM)
            vcols = slice(d + h * XA_HEAD_DIM, d + (h + 1) * XA_HEAD_DIM)
            s = _dot_nt(ins[0][:, cols], ins[1][:, cols]) * (XA_HEAD_DIM ** -0.5)
            e = jnp.exp(s - jnp.max(s, axis=-1, keepdims=True))
            p = e / jnp.sum(e, axis=-1, keepdims=True)
            outs[0][:, cols] = _dot(p, ins[1][:, vcols]).astype(BF16)

    return row_call(name, body, q.shape[0], ROW_TILE, [(q, "tile"), (kv, "full")], [(q.shape, BF16)])[0]


def xa_core_bwd(name, d_o, q, kv):
    d = q.shape[1]

    def body(i, ins, outs, accs):
        for h in range(XA_HEADS):
            cols = slice(h * XA_HEAD_DIM, (h + 1) * XA_HEAD_DIM)
            vcols = slice(d + h * XA_HEAD_DIM, d + (h + 1) * XA_HEAD_DIM)
            qh, kh, vh, doh = ins[1][:, cols], ins[2][:, cols], ins[2][:, vcols], ins[0][:, cols]
            s = _dot_nt(qh, kh) * (XA_HEAD_DIM ** -0.5)
            e = jnp.exp(s - jnp.max(s, axis=-1, keepdims=True))
            p = e / jnp.sum(e, axis=-1, keepdims=True)
            dp = _dot_nt(doh, vh)
            ds = p * (dp - jnp.sum(dp * p, axis=-1, keepdims=True)) * (XA_HEAD_DIM ** -0.5)
            outs[0][:, cols] = _dot(ds, kh).astype(BF16)
            accs[0][:, cols] += _dot_tn(ds, qh)
            accs[0][:, vcols] += _dot_tn(p, doh)

    return row_call(name, body, q.shape[0], ROW_TILE, [(d_o, "tile"), (q, "tile"), (kv, "full")],
                    [(q.shape, BF16)], [kv.shape])


def adamw(name, w, g, m, v):
    def body(i, ins, outs, accs):
        wv, gv = ins[0][...], ins[1][...]
        mn = ADAM_B1 * ins[2][...] + (1.0 - ADAM_B1) * gv
        vn = ADAM_B2 * ins[3][...] + (1.0 - ADAM_B2) * jnp.square(gv)
        m_hat = mn / (1.0 - ADAM_B1 ** ADAM_STEP)
        v_hat = vn / (1.0 - ADAM_B2 ** ADAM_STEP)
        outs[0][...] = -ADAM_LR * (m_hat / (jnp.sqrt(v_hat) + ADAM_EPS) + ADAM_WD * wv)
        outs[1][...] = mn
        outs[2][...] = vn

    return row_call(name, body, w.shape[0], ROW_TILE, [(w, "tile"), (g, "tile"), (m, "tile"), (v, "tile")],
                    [(w.shape, F32)] * 3)


def adamw_halves(name, w, g_mine, g_sibling, m, v, core):
    n_layers = len(g_mine)
    rows, cols = w.shape
    half_rows = rows // n_layers // 2
    tm = _pick_tile(half_rows, ADAMW_ROW_TILE)
    per_half = half_rows // tm

    def body(core_ref, w_ref, *rest):
        g_refs = rest[:2 * n_layers]
        m_ref, v_ref, g_out, d_out, m_out, v_out = rest[2 * n_layers:]
        i = pl.program_id(0)
        mine = ((i // per_half) % 2) == core_ref[0]
        layer = i // (2 * per_half)
        gv = jnp.where(mine, g_refs[0][...], g_refs[n_layers][...])
        for l in range(1, n_layers):
            gv = jnp.where(layer == l, jnp.where(mine, g_refs[l][...], g_refs[n_layers + l][...]), gv)
        mn = ADAM_B1 * m_ref[...] + (1.0 - ADAM_B1) * gv
        vn = ADAM_B2 * v_ref[...] + (1.0 - ADAM_B2) * jnp.square(gv)
        m_hat = mn / (1.0 - ADAM_B1 ** ADAM_STEP)
        v_hat = vn / (1.0 - ADAM_B2 ** ADAM_STEP)
        g_out[...] = gv
        d_out[...] = -ADAM_LR * (m_hat / (jnp.sqrt(v_hat) + ADAM_EPS) + ADAM_WD * w_ref[...])
        m_out[...] = mn
        v_out[...] = vn

    whole = pl.BlockSpec((tm, cols), lambda i, core_ref: (i, 0))

    def half(layer, own):
        def index(i, core_ref):
            used = (i // (2 * per_half) == layer) & ((((i // per_half) % 2) == core_ref[0]) == own)
            return (jnp.where(used, i % per_half, 0), 0)

        return pl.BlockSpec((tm, cols), index)

    halves = [half(l, True) for l in range(n_layers)] + [half(l, False) for l in range(n_layers)]
    return pl.pallas_call(
        body, name=name,
        grid_spec=pltpu.PrefetchScalarGridSpec(
            num_scalar_prefetch=1, grid=(2 * per_half * n_layers,),
            in_specs=[whole] + halves + [whole, whole], out_specs=[whole] * 4),
        out_shape=[jax.ShapeDtypeStruct(w.shape, F32)] * 4,
        compiler_params=_cparams(("parallel",)),
    )(core, w, *g_mine, *g_sibling, m, v)


HBM_SPEC = pl.BlockSpec(memory_space=pltpu.HBM)


def _position():
    return lax.axis_index("x"), lax.axis_index("y"), lax.axis_index("c")


def _other_chips(x, y):
    return [(1 - x, y), (x, 1 - y), (1 - x, 1 - y)]


def _row_chunks(rows):
    return rows // D2D_CHUNK_ROWS if rows % D2D_CHUNK_ROWS == 0 else 1


def _start_chunked(make, rows):
    k = _row_chunks(rows)
    for i in range(k):
        make(i * (rows // k), rows // k).start()


def gather_shards(packs):
    n = len(packs)

    def body(*refs):
        srcs, outs = refs[:n], refs[n:2 * n]
        send_sems, recv_sems = refs[2 * n:]
        x, y, c = _position()
        me = 2 * x + y
        chips = _other_chips(x, y)
        sibling = (x, y, 1 - c)

        def over_ici(a, j):
            px, py = chips[j]
            rows = srcs[a].shape[0] // 2
            return pltpu.make_async_remote_copy(
                src_ref=srcs[a].at[pl.ds(c * rows, rows), :], dst_ref=outs[a].at[me, pl.ds(c * rows, rows), :],
                send_sem=send_sems.at[a, j], recv_sem=recv_sems.at[a, j], device_id=(px, py, c), device_id_type=MESH)

        def landed(a, j):
            px, py = chips[j]
            rows = srcs[a].shape[0] // 2
            part = outs[a].at[2 * px + py, pl.ds(c * rows, rows), :]
            return pltpu.make_async_remote_copy(
                src_ref=part, dst_ref=part, send_sem=send_sems.at[a, j], recv_sem=recv_sems.at[a, j],
                device_id=(px, py, c), device_id_type=MESH)

        def over_d2d(a, j, cc, off, size):
            px, py = chips[j]
            rows = srcs[a].shape[0] // 2
            part = outs[a].at[2 * px + py, pl.ds(cc * rows + off, size), :]
            return pltpu.make_async_remote_copy(
                src_ref=part, dst_ref=part, send_sem=send_sems.at[a, 3 + j], recv_sem=recv_sems.at[a, 3 + j],
                device_id=sibling, device_id_type=MESH)

        for a in range(n):
            for j in range(3):
                over_ici(a, j).start()
        for a in range(n):
            for j in range(3):
                landed(a, j).wait_recv()
                _start_chunked(functools.partial(over_d2d, a, j, c), srcs[a].shape[0] // 2)
        for a in range(n):
            rows = srcs[a].shape[0] // 2
            for j in range(3):
                over_d2d(a, j, 1 - c, 0, rows).wait_recv()
                over_d2d(a, j, c, 0, rows).wait_send()
                over_ici(a, j).wait_send()

    return pl.pallas_call(
        body, name="gather_shards",
        in_specs=[HBM_SPEC] * n, out_specs=[HBM_SPEC] * n,
        out_shape=[jax.ShapeDtypeStruct((N_CHIPS,) + p.shape, p.dtype) for p in packs],
        scratch_shapes=[pltpu.SemaphoreType.DMA((n, 6)), pltpu.SemaphoreType.DMA((n, 6))],
    )(*packs)


def pair_split(name, packs):
    n = len(packs)

    def body(*refs):
        srcs, outs = refs[:n], refs[n:2 * n]
        send_sems, recv_sems = refs[2 * n:]
        x, y, c = _position()

        def remote(a, off, size):
            rows = srcs[a].shape[1] // 2
            return pltpu.make_async_remote_copy(
                src_ref=srcs[a].at[:, pl.ds((1 - c) * rows + off, size), :],
                dst_ref=outs[a].at[:, pl.ds(off, size), :],
                send_sem=send_sems.at[a], recv_sem=recv_sems.at[a], device_id=(x, y, 1 - c), device_id_type=MESH)

        for a in range(n):
            _start_chunked(functools.partial(remote, a), srcs[a].shape[1] // 2)
        for a in range(n):
            remote(a, 0, srcs[a].shape[1] // 2).wait()

    return pl.pallas_call(
        body, name=name, in_specs=[HBM_SPEC] * n, out_specs=[HBM_SPEC] * n,
        out_shape=[jax.ShapeDtypeStruct((p.shape[0], p.shape[1] // 2, p.shape[2]), p.dtype) for p in packs],
        scratch_shapes=[pltpu.SemaphoreType.DMA((n,)), pltpu.SemaphoreType.DMA((n,))],
    )(*packs)


def pair_join(name, halves):
    n = len(halves)

    def body(*refs):
        srcs, outs = refs[:n], refs[n:2 * n]
        send_sems, recv_sems = refs[2 * n:]
        x, y, c = _position()

        def remote(a, off, size):
            return pltpu.make_async_remote_copy(
                src_ref=srcs[a].at[pl.ds(off, size), :], dst_ref=outs[a].at[pl.ds(off, size), :],
                send_sem=send_sems.at[a], recv_sem=recv_sems.at[a], device_id=(x, y, 1 - c), device_id_type=MESH)

        for a in range(n):
            _start_chunked(functools.partial(remote, a), srcs[a].shape[0])
        for a in range(n):
            remote(a, 0, srcs[a].shape[0]).wait()

    return pl.pallas_call(
        body, name=name, in_specs=[HBM_SPEC] * n, out_specs=[HBM_SPEC] * n,
        out_shape=[jax.ShapeDtypeStruct(p.shape, p.dtype) for p in halves],
        scratch_shapes=[pltpu.SemaphoreType.DMA((n,)), pltpu.SemaphoreType.DMA((n,))],
    )(*halves)


SEM_SPEC = pl.BlockSpec(memory_space=pltpu.SEMAPHORE)
DATAFLOW = pltpu.SideEffectType.DATAFLOW_SIDE_EFFECTING


def _ici_copy(kind, srcs, lands, send_sems, recv_sems, a, j):
    x, y, c = _position()
    px, py = _other_chips(x, y)[j]
    if kind == "gather":
        rows = srcs[a].shape[0] // 2
        src = srcs[a].at[pl.ds(c * rows, rows), :]
        dst = lands[a].at[2 * x + y, pl.ds(c * rows, rows), :]
    else:
        src = srcs[a].at[2 * px + py]
        dst = lands[a].at[j]
    return pltpu.make_async_remote_copy(src_ref=src, dst_ref=dst, send_sem=send_sems, recv_sem=recv_sems,
                                        device_id=(px, py, c), device_id_type=MESH)


def ici_start(name, kind, srcs, land_shapes):
    n = len(srcs)
    lands = [pltpu.with_memory_space_constraint(lax.empty(shp, s.dtype), pltpu.HBM) for shp, s in zip(land_shapes, srcs)]

    def body(*refs):
        src_refs, land_refs = refs[:n], refs[n:2 * n]
        send_sems, recv_sems = refs[2 * n], refs[2 * n + 1]
        token = refs[-1]
        for a in range(n):
            for j in range(N_CHIPS - 1):
                _ici_copy(kind, src_refs, land_refs, send_sems, recv_sems, a, j).start()
        token[...] = jnp.zeros_like(token)

    sems = pltpu.SemaphoreType.DMA(())
    res = pl.pallas_call(
        body, name=name,
        out_shape=[sems, sems] + [pltpu.HBM(s.shape, s.dtype) for s in srcs]
        + [pltpu.HBM(l.shape, l.dtype) for l in lands] + [jax.ShapeDtypeStruct((8, LANES), F32)],
        in_specs=[HBM_SPEC] * (2 * n),
        out_specs=[SEM_SPEC, SEM_SPEC] + [HBM_SPEC] * (2 * n) + [pl.BlockSpec(memory_space=pltpu.VMEM)],
        input_output_aliases={i: 2 + i for i in range(2 * n)},
        compiler_params=pltpu.CompilerParams(has_side_effects=DATAFLOW),
    )(*[pltpu.with_memory_space_constraint(s, pltpu.HBM) for s in srcs], *lands)
    return res[0], res[1], list(res[2:2 + n]), list(res[2 + n:2 + 2 * n]), res[-1]


def ici_wait(name, kind, send_sems, recv_sems, srcs, lands, after):
    n = len(srcs)

    def body(*refs):
        src_refs, land_refs = refs[:n], refs[n:2 * n]
        send, recv = refs[2 * n], refs[2 * n + 1]
        for a in range(n):
            for j in range(N_CHIPS - 1):
                cp = _ici_copy(kind, src_refs, land_refs, send, recv, a, j)
                cp.wait_send()
                cp.wait_recv()

    res = pl.pallas_call(
        body, name=name,
        out_shape=[pltpu.HBM(s.shape, s.dtype) for s in srcs] + [pltpu.HBM(l.shape, l.dtype) for l in lands],
        in_specs=[HBM_SPEC] * (2 * n) + [SEM_SPEC, SEM_SPEC, pl.BlockSpec(memory_space=pl.ANY)],
        out_specs=[HBM_SPEC] * (2 * n),
        input_output_aliases={i: i for i in range(2 * n)},
        compiler_params=pltpu.CompilerParams(has_side_effects=DATAFLOW),
    )(*srcs, *lands, send_sems, recv_sems, after)
    return list(res[:n]), list(res[n:])


def pair_forward(gathered):
    n = len(gathered)

    def body(*refs):
        outs = refs[n:2 * n]
        send_sems, recv_sems = refs[2 * n:]
        x, y, c = _position()
        chips = _other_chips(x, y)

        def part(a, j, cc, off, size):
            px, py = chips[j]
            rows = outs[a].shape[1] // 2
            ref = outs[a].at[2 * px + py, pl.ds(cc * rows + off, size), :]
            return pltpu.make_async_remote_copy(
                src_ref=ref, dst_ref=ref, send_sem=send_sems.at[a, j], recv_sem=recv_sems.at[a, j],
                device_id=(x, y, 1 - c), device_id_type=MESH)

        for a in range(n):
            for j in range(N_CHIPS - 1):
                _start_chunked(functools.partial(part, a, j, c), outs[a].shape[1] // 2)
        for a in range(n):
            rows = outs[a].shape[1] // 2
            for j in range(N_CHIPS - 1):
                part(a, j, 1 - c, 0, rows).wait_recv()
                part(a, j, c, 0, rows).wait_send()

    return pl.pallas_call(
        body, name="pair_forward", in_specs=[HBM_SPEC] * n, out_specs=[HBM_SPEC] * n,
        out_shape=[jax.ShapeDtypeStruct(g.shape, g.dtype) for g in gathered],
        input_output_aliases={i: i for i in range(n)},
        scratch_shapes=[pltpu.SemaphoreType.DMA((n, N_CHIPS - 1)), pltpu.SemaphoreType.DMA((n, N_CHIPS - 1))],
    )(*gathered)


def all_sum_small(part):
    n_dev = 8
    rows = part.shape[0]

    def body(src, out, buf, send_sems, recv_sems):
        x, y, c = _position()
        me = 4 * x + 2 * y + c
        buf[me] = src[...]
        copies = []
        for k in range(1, n_dev):
            px, py, pc = x ^ ((k >> 2) & 1), y ^ ((k >> 1) & 1), c ^ (k & 1)
            cp = pltpu.make_async_remote_copy(
                src_ref=src, dst_ref=buf.at[me], send_sem=send_sems.at[k - 1], recv_sem=recv_sems.at[k - 1],
                device_id=(px, py, pc), device_id_type=MESH)
            cp.start()
            copies.append(cp)
        for cp in copies:
            cp.wait()
        acc = buf[0]
        for k in range(1, n_dev):
            acc = acc + buf[k]
        out[...] = acc

    return pl.pallas_call(
        body, name="all_sum_small",
        in_specs=[pl.BlockSpec(memory_space=pltpu.VMEM)], out_specs=pl.BlockSpec(memory_space=pltpu.VMEM),
        out_shape=jax.ShapeDtypeStruct(part.shape, F32),
        scratch_shapes=[pltpu.VMEM((n_dev, rows, part.shape[1]), F32),
                        pltpu.SemaphoreType.DMA((n_dev - 1,)), pltpu.SemaphoreType.DMA((n_dev - 1,))],
    )(part)


def add_pairs(name, src, theirs, core, out_dtype):
    slabs, rows, cols = theirs.shape
    tm = _pick_tile(rows, ROW_TILE)
    nb = rows // tm

    def body(core_ref, a_ref, b_ref, o_ref):
        o_ref[...] = (a_ref[...].astype(F32) + b_ref[...].astype(F32)).astype(out_dtype)

    return pl.pallas_call(
        body, name=name,
        grid_spec=pltpu.PrefetchScalarGridSpec(
            num_scalar_prefetch=1, grid=(slabs, nb),
            in_specs=[pl.BlockSpec((None, tm, cols), lambda s, i, core_ref: (s, core_ref[0] * nb + i, 0)),
                      pl.BlockSpec((None, tm, cols), lambda s, i, core_ref: (s, i, 0))],
            out_specs=pl.BlockSpec((None, tm, cols), lambda s, i, core_ref: (s, i, 0))),
        out_shape=jax.ShapeDtypeStruct(theirs.shape, out_dtype),
        compiler_params=_cparams(("parallel", "parallel")),
    )(core, src, theirs)


def add_four(name, src, theirs, chip):
    _, rows, cols = theirs.shape
    tm = _pick_tile(rows, ROW_TILE)

    def body(chip_ref, a_ref, b_ref, o_ref):
        acc = a_ref[...].astype(F32)
        for j in range(N_CHIPS - 1):
            acc = acc + b_ref[j].astype(F32)
        o_ref[...] = acc

    return pl.pallas_call(
        body, name=name,
        grid_spec=pltpu.PrefetchScalarGridSpec(
            num_scalar_prefetch=1, grid=(rows // tm,),
            in_specs=[pl.BlockSpec((None, tm, cols), lambda i, chip_ref: (chip_ref[0], i, 0)),
                      pl.BlockSpec((N_CHIPS - 1, tm, cols), lambda i, chip_ref: (0, i, 0))],
            out_specs=pl.BlockSpec((tm, cols), lambda i, chip_ref: (i, 0))),
        out_shape=jax.ShapeDtypeStruct((rows, cols), F32),
        compiler_params=_cparams(("parallel",)),
    )(chip, src, theirs)


PACK_COLS = 1024
SMALL_ROW_MULTIPLE = 32
BIG = ["dn_w_in", "dn_w_out", "cv_w_pw1", "cv_w_pw2", "xa_w_q", "xa_w_kv", "xa_w_o", "mlp_w_up", "mlp_w_down"]
SMALL = ["dn_w_conv", "cv_norm", "cv_b_pw1", "cv_w_dw", "cv_b_dw", "cv_ln_g", "cv_ln_b", "cv_b_pw2"]
SHARD_AXIS = {"dn_w_in": 2, "dn_w_conv": 2, "dn_w_out": 1, "cv_norm": 1, "cv_w_pw1": 2, "cv_b_pw1": 1,
              "cv_w_dw": 2, "cv_b_dw": 1, "cv_ln_g": 1, "cv_ln_b": 1, "cv_w_pw2": 1, "cv_b_pw2": 1,
              "xa_w_q": 1, "xa_w_kv": 2, "xa_w_o": 1, "mlp_w_up": 2, "mlp_w_down": 1}
REPLICATED = ["dn_norm", "dn_a_log", "dn_dt_bias", "dn_out_norm", "xa_norm", "xa_mem_norm", "mlp_norm", "final_norm"]


def _pack_rows(size):
    return -(-size // PACK_COLS)


SHARD_SHAPES = {
    "dn_w_in": (1, 1024, 1028), "dn_w_conv": (1, 4, 768), "dn_w_out": (1, 256, 1024), "cv_norm": (1, 256),
    "cv_w_pw1": (1, 1024, 512), "cv_b_pw1": (1, 512), "cv_w_dw": (1, 31, 256), "cv_b_dw": (1, 256),
    "cv_ln_g": (1, 256), "cv_ln_b": (1, 256), "cv_w_pw2": (1, 256, 1024), "cv_b_pw2": (1, 256),
    "xa_w_q": (2, 256, 1024), "xa_w_kv": (2, 1024, 512), "xa_w_o": (2, 256, 1024),
    "mlp_w_up": (2, 1024, 1024), "mlp_w_down": (2, 1024, 1024)}


def _shard_shape(nm):
    return SHARD_SHAPES[nm]


def _pack(tensors, names, dtype, row_multiple):
    pieces = []
    for nm in names:
        t = tensors[nm]
        flat = t.reshape(t.shape[0], -1) if t.ndim > len(_shard_shape(nm)) else t.reshape(1, -1)
        pad = _pack_rows(flat.shape[1]) * PACK_COLS - flat.shape[1]
        pieces.append(jnp.pad(flat.astype(dtype), ((0, 0), (0, pad))))
    cat = jnp.concatenate(pieces, axis=1)
    rows = cat.shape[1] // PACK_COLS
    total = -(-rows // row_multiple) * row_multiple
    cat = jnp.pad(cat, ((0, 0), (0, (total - rows) * PACK_COLS)))
    return cat.reshape(cat.shape[0], total, PACK_COLS)


def _unpack(pack, names):
    lead = pack.shape[:-2]
    flat = pack.reshape(lead + (-1,))
    out, off = {}, 0
    for nm in names:
        shp = _shard_shape(nm)
        size = 1
        for s in shp:
            size *= s
        out[nm] = flat[..., off:off + size].reshape(lead + shp)
        off += _pack_rows(size) * PACK_COLS
    return out


def _to_full(nm, stacked):
    ax = SHARD_AXIS[nm]
    moved = jnp.moveaxis(stacked, 0, ax)
    shp = list(_shard_shape(nm))
    shp[ax] *= N_CHIPS
    return moved.reshape(shp)


def _to_shards(nm, full):
    ax = SHARD_AXIS[nm]
    shp = list(_shard_shape(nm))
    split = full.reshape(shp[:ax] + [N_CHIPS, shp[ax]] + shp[ax + 1:])
    return jnp.moveaxis(split, ax, 0)


def _row(v):
    return v.reshape(1, -1)


class Stacked:
    def __init__(self, arr, split, layer):
        self.arr, self.kw = arr, dict(b_split=split, b_layer=layer)


def _grad_out(split):
    return dict(out_dtype=BF16, out_split=split, out_layer=(0, 1))


def _with_next(res, next_gain):
    return (res[0], res[1]) if next_gain is not None else (res, None)


def mlp_fwd(tag, h, g, w_up, w_down, n=None, next_gain=None):
    if n is None:
        n = rms_fwd(tag + "_norm", h, g)
    act = mm(tag + "_up", n, w_up.arr, out_dtype=BF16, epi=lambda acc: jnp.square(jnp.maximum(acc, 0.0)), **w_up.kw)
    out, n_next = _with_next(mm(tag + "_down", act, w_down.arr, tk=LONG_TILE_K, epi=lambda acc, res: acc + res,
                                epi_tiles=(h,), norm_gain=next_gain, **w_down.kw), next_gain)
    return out, n_next, (n, act)


def mlp_bwd(tag, dh, h, g, w_up, w_down, saved, after=None):
    n, act = saved
    dh, dh16 = dh
    dup = mm(tag + "_d_act", dh16, w_down.arr, tb=True, out_dtype=BF16, after=after,
             epi=lambda acc, t: acc * (2.0 * jnp.sqrt(t.astype(F32))), epi_tiles=(act,), **w_down.kw)
    dw_down = mm(tag + "_dw_down", act, dh16, ta=True, tk=GRAD_TILE_K, **_grad_out("rows"))
    dh_in, dh16_in, dg = mm(tag + "_dn", dup, w_up.arr, tb=True, norm_bwd=(h, g, dh), **w_up.kw)
    dw_up = mm(tag + "_dw_up", n, dup, ta=True, tk=GRAD_TILE_K, **_grad_out("cols"))
    return (dh_in, dh16_in), dg, (dw_up, dw_down)


def xa_fwd(tag, h, mem, g, g_mem, w_q, w_kv, w_o, n=None, next_gain=None):
    if n is None:
        n = rms_fwd(tag + "_norm", h, g)
    mem_n = rms_fwd(tag + "_mem_norm", mem, g_mem)
    q = mm(tag + "_q", n, w_q.arr, out_dtype=BF16, **w_q.kw)
    kv = mm(tag + "_kv", mem_n, w_kv.arr, out_dtype=BF16, **w_kv.kw)
    o = xa_core_fwd(tag + "_core", q, kv)
    out, n_next = _with_next(mm(tag + "_o", o, w_o.arr, epi=lambda acc, res: acc + res, epi_tiles=(h,),
                                norm_gain=next_gain, **w_o.kw), next_gain)
    return out, n_next, (n, mem_n, q, kv, o)


def xa_bwd(tag, dh, h, mem, g, g_mem, w_q, w_kv, w_o, saved):
    n, mem_n, q, kv, o = saved
    dh, dh16 = dh
    d_o = mm(tag + "_d_o", dh16, w_o.arr, tb=True, out_dtype=BF16, **w_o.kw)
    dw_o = mm(tag + "_dw_o", o, dh16, ta=True, tk=GRAD_TILE_K, **_grad_out("rows"))
    dq, dkv = xa_core_bwd(tag + "_core_bwd", d_o, q, kv)
    dh_in, dh16_in, dg = mm(tag + "_dn", dq, w_q.arr, tb=True, norm_bwd=(h, g, dh), **w_q.kw)
    dw_q = mm(tag + "_dw_q", n, dq, ta=True, tk=GRAD_TILE_K, **_grad_out("rows"))
    dw_kv = mm(tag + "_dw_kv", mem_n, dkv, ta=True, **_grad_out("cols"))
    dmem_n = mm(tag + "_dmem", dkv, w_kv.arr, tb=True, **w_kv.kw)
    dg_mem = mem_norm_bwd(tag + "_mem_norm_bwd", dmem_n, mem, g_mem)
    return (dh_in, dh16_in), dg, dg_mem, (dw_q, dw_kv, dw_o)


def _gate_tile(a_log, dt_bias):
    t = jnp.zeros((8, LANES), F32)
    t = t.at[0, DN_HEADS:2 * DN_HEADS].set(a_log.reshape(-1))
    return t.at[1, DN_HEADS:2 * DN_HEADS].set(dt_bias.reshape(-1))


def dn_fwd(h, g, w_qkv, w_z, w_ba, w_conv, gate, out_norm, w_out, next_gain=None):
    n = rms_fwd("dn_norm", h, g)
    qkv_raw = mm("dn_proj_qkv", n, w_qkv)
    z = mm("dn_proj_z", n, w_z)
    ba = mm("dn_proj_ba", n, w_ba)
    qkv, hs = dn_pre(qkv_raw, ba, w_conv, gate)
    u, w, t_inv = dn_solve(qkv, hs)
    o, states = dn_scan_fwd(qkv, u, w, hs)
    og = dn_post(o, z, out_norm)
    out, n_next = _with_next(mm("dn_out", og, w_out.arr, epi=lambda acc, res: acc + res, epi_tiles=(h,),
                                norm_gain=next_gain, **w_out.kw), next_gain)
    return out, n_next, (n, qkv_raw, z, ba, qkv, hs, u, w, t_inv, o, states, og)


def dn_bwd(dh, h, g, w_qkv, w_z, w_ba, w_conv, gate, out_norm, w_out, saved, after=None):
    n, qkv_raw, z, ba, qkv, hs, u, w, t_inv, o, states, og = saved
    dh, dh16 = dh
    d_og = mm("dn_d_og", dh16, w_out.arr, tb=True, out_dtype=BF16, after=after, **w_out.kw)
    dw_out = mm("dn_dw_out", og, dh16, ta=True, tk=GRAD_TILE_K, **_grad_out("rows"))
    d_o, dz, d_out_norm = dn_post_bwd(d_og, o, z, out_norm)
    dq, dk, dv, dhs = dn_scan_bwd(qkv, u, w, t_inv, hs, states, d_o)
    dc, dba, d_gate = dn_pre_bwd(dq, dk, dv, dhs, qkv_raw, ba, w_conv, gate)
    dqkv_raw, dw_conv = dn_conv_bwd(dc, qkv_raw, w_conv)
    dn = mm("dn_dn_qkv", dqkv_raw, w_qkv, tb=True, tk=w_qkv.shape[1])
    dn = mm("dn_dn_z", dz, w_z, tb=True, epi=lambda acc, t: acc + t, epi_tiles=(dn,))
    dh_in, _, dg = mm("dn_dn_ba", dba, w_ba, tb=True, epi=lambda acc, t: acc + t, epi_tiles=(dn,),
                      norm_bwd=(h, g, dh))
    dw_qkv = mm("dn_dw_qkv", n, dqkv_raw, ta=True, tk=GRAD_TILE_K)
    dw_z = mm("dn_dw_z", n, dz, ta=True, tk=GRAD_TILE_K)
    dw_ba = mm("dn_dw_ba", n, dba, ta=True, tk=GRAD_TILE_K)
    return dh_in, dg, dw_qkv, dw_z, dw_ba, dw_conv, d_gate, d_out_norm, dw_out


def cv_fwd(h, g, w_pw1, b_pw1, w_dw, b_dw, ln_g, ln_b, w_pw2, b_pw2, n=None, next_gain=None):
    if n is None:
        n = rms_fwd("cv_norm", h, g)
    u = mm("cv_pw1", n, w_pw1.arr, epi=lambda acc, b: acc + b, epi_rows=(b_pw1,), **w_pw1.kw)
    s, c = cv_core_fwd(u, w_dw, b_dw, ln_g, ln_b)
    out, n_next = _with_next(mm("cv_pw2", s, w_pw2.arr, epi=lambda acc, res, b: acc + res + b, epi_tiles=(h,),
                                epi_rows=(b_pw2,), norm_gain=next_gain, **w_pw2.kw), next_gain)
    return out, n_next, (n, u, s, c)


def cv_bwd(dh, h, g, w_pw1, w_dw, ln_g, ln_b, w_pw2, saved):
    n, u, s, c = saved
    dh, dh16 = dh
    ds = mm("cv_d_s", dh16, w_pw2.arr, tb=True, out_dtype=BF16, **w_pw2.kw)
    dw_pw2 = mm("cv_dw_pw2", s, dh16, ta=True, tk=GRAD_TILE_K, **_grad_out("rows"))
    db_pw2 = col_sum("cv_db_pw2", dh)
    dc, ln_acc = cv_ln_bwd(ds, c, ln_g, ln_b)
    du, dw_dw, db_pw1 = cv_conv_bwd(dc, u, w_dw)
    dh_in, dh16_in, dg = mm("cv_dn", du, w_pw1.arr, tb=True, norm_bwd=(h, g, dh), **w_pw1.kw)
    dw_pw1 = mm("cv_dw_pw1", n, du, ta=True, tk=GRAD_TILE_K, **_grad_out("cols"))
    return (dh_in, dh16_in), dg, dw_pw1, db_pw1, dw_dw, ln_acc, dw_pw2, db_pw2


WEIGHTS = ["dn_norm", "dn_w_in", "dn_w_conv", "dn_a_log", "dn_dt_bias", "dn_out_norm", "dn_w_out", "cv_norm",
           "cv_w_pw1", "cv_b_pw1", "cv_w_dw", "cv_b_dw", "cv_ln_g", "cv_ln_b", "cv_w_pw2", "cv_b_pw2", "xa_norm",
           "xa_mem_norm", "xa_w_q", "xa_w_kv", "xa_w_o", "mlp_norm", "mlp_w_up", "mlp_w_down", "final_norm"]


def _as_2d(t):
    if t.ndim == 1:
        return t.reshape(1, -1)
    return t.reshape(-1, t.shape[-1])


def kernel(x, mem, dn_norm, dn_w_in, dn_w_conv, dn_a_log, dn_dt_bias, dn_out_norm, dn_w_out, cv_norm, cv_w_pw1, cv_b_pw1, cv_w_dw, cv_b_dw, cv_ln_g, cv_ln_b, cv_w_pw2, cv_b_pw2, xa_norm, xa_mem_norm, xa_w_q, xa_w_kv, xa_w_o, mlp_norm, mlp_w_up, mlp_w_down, final_norm, loss_target, m_dn_norm, m_dn_w_in, m_dn_w_conv, m_dn_a_log, m_dn_dt_bias, m_dn_out_norm, m_dn_w_out, m_cv_norm, m_cv_w_pw1, m_cv_b_pw1, m_cv_w_dw, m_cv_b_dw, m_cv_ln_g, m_cv_ln_b, m_cv_w_pw2, m_cv_b_pw2, m_xa_norm, m_xa_mem_norm, m_xa_w_q, m_xa_w_kv, m_xa_w_o, m_mlp_norm, m_mlp_w_up, m_mlp_w_down, m_final_norm, v_dn_norm, v_dn_w_in, v_dn_w_conv, v_dn_a_log, v_dn_dt_bias, v_dn_out_norm, v_dn_w_out, v_cv_norm, v_cv_w_pw1, v_cv_b_pw1, v_cv_w_dw, v_cv_b_dw, v_cv_ln_g, v_cv_ln_b, v_cv_w_pw2, v_cv_b_pw2, v_xa_norm, v_xa_mem_norm, v_xa_w_q, v_xa_w_kv, v_xa_w_o, v_mlp_norm, v_mlp_w_up, v_mlp_w_down, v_final_norm):
    args = dict(locals())
    wts = {nm: args[nm] for nm in WEIGHTS}
    mom = {nm: args["m_" + nm] for nm in WEIGHTS}
    var = {nm: args["v_" + nm] for nm in WEIGHTS}
    core = lax.axis_index("c").astype(jnp.int32).reshape(1)
    chip = (2 * lax.axis_index("x") + lax.axis_index("y")).astype(jnp.int32)
    def own_slab(got, src):
        return lax.dynamic_update_slice(got, src[None], (chip, 0, 0))

    shard2d = {nm: wts[nm].astype(BF16).reshape(-1, wts[nm].shape[-1]) for nm in BIG}
    first = ["dn_w_in", "dn_w_out"]
    later = [nm for nm in BIG if nm not in first]
    sources = [shard2d[nm] for nm in first] + [_pack(wts, SMALL, F32, SMALL_ROW_MULTIPLE)[0]]
    gathered = [own_slab(got, src) for got, src in zip(gather_shards(sources), sources)]
    stacked = {"dn_w_out": gathered[1].reshape((N_CHIPS,) + SHARD_SHAPES["dn_w_out"])}
    full = {nm: _to_full(nm, t) for nm, t in _unpack(gathered[2], SMALL).items()}
    full["dn_w_in"] = _to_full("dn_w_in", gathered[0].reshape((N_CHIPS,) + SHARD_SHAPES["dn_w_in"]))
    full.update({nm: wts[nm] for nm in REPLICATED})
    later_src = [shard2d[nm] for nm in later]
    g_send, g_recv, later_src, g_lands, started = ici_start(
        "gather_start", "gather", later_src, [(N_CHIPS,) + s.shape for s in later_src])
    full["dn_norm"] = full["dn_norm"] + started[0, 0]

    def rest_weights(after):
        srcs, lands = ici_wait("gather_wait", "gather", g_send, g_recv, later_src, g_lands, after)
        return {nm: own_slab(land, src).reshape((N_CHIPS,) + SHARD_SHAPES[nm])
                for nm, land, src in zip(later, pair_forward(lands), srcs)}

    pending = []

    def on_grads(items):
        tag = "_".join(sorted({str(layer) for _, layer, _ in items}))
        parts = [g.reshape(N_CHIPS, -1, g.shape[-1]) for _, _, g in items]
        theirs = pair_split("pair_split_" + tag, parts)
        pairs = [add_pairs("pair_add_%s%d" % (nm, layer), p, t, core, BF16)
                 for (nm, layer, _), p, t in zip(items, parts, theirs)]
        send, recv, pairs, lands, token = ici_start(
            "scatter_start_" + tag, "scatter", pairs, [(N_CHIPS - 1,) + p.shape[1:] for p in pairs])
        pending.append((tag, items, send, recv, pairs, lands))
        return token

    dh, grads, rep = local_step(x[0], mem[0], loss_target[0], stacked, full, rest_weights, on_grads)

    halves = {}
    last = [("dn_w_in", 0, _to_shards("dn_w_in", grads["dn_w_in"]).astype(BF16)), ("dn_w_out", 0, grads["dn_w_out"][0]),
            ("small", 0, _pack({nm: _to_shards(nm, grads[nm]) for nm in SMALL}, SMALL, F32, SMALL_ROW_MULTIPLE))]
    parts = [g.reshape(N_CHIPS, -1, g.shape[-1]) for _, _, g in last]
    theirs = pair_split("pair_split_last", parts)
    pairs = [add_pairs("pair_add_" + nm, p, t, core, p.dtype) for (nm, _, _), p, t in zip(last, parts, theirs)]
    l_send, l_recv, l_pairs, l_lands, l_started = ici_start(
        "scatter_start_last", "scatter", pairs, [(N_CHIPS - 1,) + p.shape[1:] for p in pairs])
    for tag, items, send, recv, pairs, lands in pending:
        pairs, lands = ici_wait("scatter_wait_" + tag, "scatter", send, recv, pairs, lands, l_started)
        for (nm, layer, _), p, o in zip(items, pairs, lands):
            halves[nm, layer] = add_four("chip_add_%s%d" % (nm, layer), p, o, chip.reshape(1))
    keys = sorted(halves)
    siblings = dict(zip(keys, pair_join("pair_join_early", [halves[k] for k in keys])))

    delta, new_m, new_v, red = {}, {}, {}, {}

    def big_adamw(nm):
        layers = range(wts[nm].shape[0])
        res = adamw_halves("adamw_" + nm, _as_2d(wts[nm]), [halves[nm, l] for l in layers],
                           [siblings[nm, l] for l in layers], _as_2d(mom[nm]), _as_2d(var[nm]), core)
        red[nm], delta[nm], new_m[nm], new_v[nm] = (r.reshape(wts[nm].shape) for r in res)

    early = [nm for nm in BIG if (nm, 0) in halves]
    for nm in early:
        big_adamw(nm)
    done = jnp.concatenate([new_v[nm].reshape(-1)[:1] for nm in early])
    l_pairs, l_lands = ici_wait("scatter_wait_last", "scatter", l_send, l_recv, l_pairs, l_lands, done)
    for (nm, layer, _), p, o in zip(last, l_pairs, l_lands):
        halves[nm, layer] = add_four("chip_add_" + nm, p, o, chip.reshape(1))
    keys = [(nm, layer) for nm, layer, _ in last]
    siblings.update(zip(keys, pair_join("pair_join_last", [halves[k] for k in keys])))
    south = core[0] == 0
    mine, theirs = halves["small", 0], siblings["small", 0]
    red.update(_unpack(jnp.concatenate([jnp.where(south, mine, theirs), jnp.where(south, theirs, mine)], axis=0),
                       SMALL))

    rep = all_sum_small(rep)
    red["dn_norm"] = rep[0:1]
    red["dn_a_log"] = rep[1:2, DN_HEADS:2 * DN_HEADS]
    red["dn_dt_bias"] = rep[2:3, DN_HEADS:2 * DN_HEADS]
    red["dn_out_norm"] = rep[3:4, :LANES]
    red["xa_norm"], red["xa_mem_norm"], red["mlp_norm"] = rep[4:6], rep[6:8], rep[8:10]
    red["final_norm"] = rep[10]
    loss = rep[11, 0]

    for nm in WEIGHTS:
        shp = wts[nm].shape
        if nm in early:
            continue
        if nm in BIG:
            big_adamw(nm)
            continue
        res = adamw("adamw_" + nm, _as_2d(wts[nm]), _as_2d(red[nm].reshape(shp)), _as_2d(mom[nm]), _as_2d(var[nm]))
        delta[nm], new_m[nm], new_v[nm] = (r.reshape(shp) for r in res)
        red[nm] = red[nm].reshape(shp)

    grad_x = dh[None]
    return (loss, grad_x, *[red[nm] for nm in WEIGHTS], *[delta[nm] for nm in WEIGHTS],
            *[new_m[nm] for nm in WEIGHTS], *[new_v[nm] for nm in WEIGHTS])


def local_step(h0, mem0, target, stacked, full, rest_weights=None, on_grads=None):
    d = h0.shape[1]
    dn_norm, dn_a_log, dn_dt_bias, dn_out_norm = (full[nm] for nm in REPLICATED[:4])
    xa_norm, xa_mem_norm, mlp_norm, final_norm = (full[nm] for nm in REPLICATED[4:])
    inner = DN_HEADS * DN_HEAD_DIM
    w_in = full["dn_w_in"][0]
    w_qkv, w_z = w_in[:, :3 * inner], w_in[:, 3 * inner:4 * inner]
    w_ba = jnp.pad(w_in[:, 4 * inner:], ((0, 0), (0, LANES - 2 * DN_HEADS)))
    w_conv = jnp.pad(full["dn_w_conv"][0], ((0, 8 - DN_CONV), (0, 0)))
    gate = _gate_tile(dn_a_log, dn_dt_bias)
    w_dw = jnp.pad(full["cv_w_dw"][0], ((0, CV_HALO - CV_WIDTH), (0, 0)))

    def sw(nm, layer):
        return Stacked(stacked[nm], "rows" if SHARD_AXIS[nm] == 1 else "cols", layer)

    dn_args = (_row(dn_norm), w_qkv, w_z, w_ba, w_conv, gate, _row(dn_out_norm), sw("dn_w_out", 0))
    h1, n, dn_saved = dn_fwd(h0, *dn_args, next_gain=_row(xa_norm[0]))
    if rest_weights is not None:
        stacked = {**stacked, **rest_weights(h1)}
    xa_args = [(_row(xa_norm[l]), _row(xa_mem_norm[l]), sw("xa_w_q", l), sw("xa_w_kv", l), sw("xa_w_o", l))
               for l in range(2)]
    mlp_args = [(_row(mlp_norm[l]), sw("mlp_w_up", l), sw("mlp_w_down", l)) for l in range(2)]
    cv_args = (_row(full["cv_norm"][0]), sw("cv_w_pw1", 0), full["cv_b_pw1"], w_dw, full["cv_b_dw"],
               full["cv_ln_g"], full["cv_ln_b"], sw("cv_w_pw2", 0), full["cv_b_pw2"])
    h2, n, xa0_saved = xa_fwd("xa0", h1, mem0, *xa_args[0], n=n, next_gain=mlp_args[0][0])
    h3, n, mlp0_saved = mlp_fwd("mlp0", h2, *mlp_args[0], n=n, next_gain=cv_args[0])
    h4, n, cv_saved = cv_fwd(h3, *cv_args, n=n, next_gain=xa_args[1][0])
    h5, n, xa1_saved = xa_fwd("xa1", h4, mem0, *xa_args[1], n=n, next_gain=mlp_args[1][0])
    h6, _, mlp1_saved = mlp_fwd("mlp1", h5, *mlp_args[1], n=n)

    dh32, dh16, loss_tile, d_final = loss_head("loss_head", h6, _row(final_norm), target)
    dh = (dh32, dh16)
    grads = {}
    dg_mlp, dg_xa, dg_xa_mem = [None, None], [None, None], [None, None]
    dw_mlp, dw_xa = [None, None], [None, None]
    mlp_names, xa_names = ("mlp_w_up", "mlp_w_down"), ("xa_w_q", "xa_w_kv", "xa_w_o")

    def announce(items):
        return None if on_grads is None else on_grads(items)

    dh, dg_mlp[1], dw_mlp[1] = mlp_bwd("mlp1", dh, h5, *mlp_args[1], mlp1_saved)
    dh, dg_xa[1], dg_xa_mem[1], dw_xa[1] = xa_bwd("xa1", dh, h4, mem0, *xa_args[1], xa1_saved)
    (dh, grads["cv_norm"], dw_pw1, grads["cv_b_pw1"], dw_dw, ln_acc, dw_pw2,
     grads["cv_b_pw2"]) = cv_bwd(dh, h3, cv_args[0], cv_args[1], w_dw, cv_args[5], cv_args[6], cv_args[7], cv_saved)
    after = announce([(nm, 1, g) for nm, g in zip(mlp_names + xa_names, dw_mlp[1] + dw_xa[1])]
                     + [("cv_w_pw1", 0, dw_pw1), ("cv_w_pw2", 0, dw_pw2)])
    dh, dg_mlp[0], dw_mlp[0] = mlp_bwd("mlp0", dh, h2, *mlp_args[0], mlp0_saved, after=after)
    dh, dg_xa[0], dg_xa_mem[0], dw_xa[0] = xa_bwd("xa0", dh, h1, mem0, *xa_args[0], xa0_saved)
    after = announce([(nm, 0, g) for nm, g in zip(mlp_names + xa_names, dw_mlp[0] + dw_xa[0])])
    dh, dg_dn, dw_qkv, dw_z, dw_ba, dw_conv, d_gate, d_out_norm, dw_out = dn_bwd(dh, h0, *dn_args, dn_saved,
                                                                                 after=after)

    grads["dn_w_in"] = jnp.concatenate([dw_qkv, dw_z, dw_ba[:, :2 * DN_HEADS]], axis=1)[None]
    grads["dn_w_conv"] = dw_conv[None, :DN_CONV]
    grads["dn_w_out"], grads["cv_w_pw1"], grads["cv_w_pw2"] = [dw_out], [dw_pw1], [dw_pw2]
    grads["cv_w_dw"] = dw_dw[None, :CV_WIDTH]
    grads["cv_ln_g"], grads["cv_ln_b"], grads["cv_b_dw"] = ln_acc[0:1], ln_acc[1:2], ln_acc[2:3]
    for i, nm in enumerate(mlp_names):
        grads[nm] = [dw_mlp[0][i], dw_mlp[1][i]]
    for i, nm in enumerate(xa_names):
        grads[nm] = [dw_xa[0][i], dw_xa[1][i]]

    rep = jnp.zeros((16, d), F32)
    rep = rep.at[0].set(dg_dn[0])
    rep = rep.at[1, :LANES].set(d_gate[0])
    rep = rep.at[2, :LANES].set(d_gate[1])
    rep = rep.at[3, :LANES].set(d_out_norm[0])
    rep = rep.at[4].set(dg_xa[0][0]).at[5].set(dg_xa[1][0])
    rep = rep.at[6].set(dg_xa_mem[0][0]).at[7].set(dg_xa_mem[1][0])
    rep = rep.at[8].set(dg_mlp[0][0]).at[9].set(dg_mlp[1][0])
    rep = rep.at[10].set(d_final[0])
    rep = rep.at[11, :LANES].set(loss_tile[0])
    return dh, grads, rep
```

```python
import functools

import jax
import jax.numpy as jnp
from jax import lax
from jax.experimental import pallas as pl
from jax.experimental.pallas import tpu as pltpu

F32 = jnp.float32
BF16 = jnp.bfloat16
HIGHEST = lax.Precision.HIGHEST
MESH = pl.DeviceIdType.MESH

D_MODEL = 1024
DN_HEADS = 8
DN_HEAD_DIM = 128
DN_CONV = 4
DN_CHUNK = 64
CV_WIDTH = 31
XA_HEADS = 4
XA_HEAD_DIM = 256
RMS_EPS = 1e-6
LN_EPS = 1e-5
L2_EPS = 1e-6

ADAM_LR = 0.001
ADAM_B1 = 0.9
ADAM_B2 = 0.999
ADAM_EPS = 1e-08
ADAM_WD = 0.01
ADAM_STEP = 10

LANES = 128
ROW_TILE = 512
CONV_ROW_TILE = 256
MM_TILE = 1024
GRAD_TILE_K = 4096
LONG_TILE_K = 2048
ADAMW_ROW_TILE = 256
DN_ROW_TILE = 256
CHUNK_SHIFT = 6
SOLVE_INTERLEAVE = 8
FWD_HEADS_PER_STEP = 8
BWD_HEADS_PER_STEP = 8
BWD_SCAN_ROWS = 256
DN_HALO = 8
CV_HALO = 32
VMEM_LIMIT = 48 * 1024 * 1024
N_CHIPS = 4
D2D_CHUNK_ROWS = 256


def _cparams(sem):
    return pltpu.CompilerParams(dimension_semantics=sem, vmem_limit_bytes=VMEM_LIMIT)


def _dot(a, b, dims=(((1,), (0,)), ((), ()))):
    return lax.dot_general(a.astype(BF16), b.astype(BF16), dims, preferred_element_type=F32)


def _dot_nt(a, b):
    return _dot(a, b, (((1,), (1,)), ((), ())))


def _dot_tn(a, b):
    return _dot(a, b, (((0,), (0,)), ((), ())))


def _dot_hi(a, b, dims=(((1,), (0,)), ((), ()))):
    return lax.dot_general(a.astype(F32), b.astype(F32), dims, precision=HIGHEST, preferred_element_type=F32)


def _dot_x3(a, b, dims=(((1,), (0,)), ((), ()))):
    a_hi, b_hi = a.astype(BF16), b.astype(BF16)
    a_lo = (a - a_hi.astype(F32)).astype(BF16)
    b_lo = (b - b_hi.astype(F32)).astype(BF16)

    def dot(p, q):
        return lax.dot_general(p, q, dims, preferred_element_type=F32)

    return dot(a_hi, b_hi) + (dot(a_hi, b_lo) + dot(a_lo, b_hi))


def _sigmoid(x):
    return 1.0 / (1.0 + jnp.exp(-x))


def _silu(x):
    return x * _sigmoid(x)


def _silu_grad(x):
    s = _sigmoid(x)
    return s * (1.0 + x * (1.0 - s))


def _softplus(x):
    return jnp.maximum(x, 0.0) + jnp.log(1.0 + jnp.exp(-jnp.abs(x)))


def _iota(shape, dim):
    return lax.broadcasted_iota(jnp.int32, shape, dim)


def _lane_col(vals, lane, idx):
    return jnp.sum(jnp.where(lane == idx, vals, 0.0), axis=1, keepdims=True)


def _pick_tile(rows, cap):
    best = rows
    for t in range(16, min(rows, cap) + 1, 16):
        if rows % t == 0:
            best = t
    return best


def _stacked_spec(shape, split, layer, rows, cols, block_index):
    r_shard, c_shard = shape[-2], shape[-1]
    if split == "rows" and rows > r_shard:
        assert rows % r_shard == 0 and c_shard % cols == 0
        chips = rows // r_shard

        def slabs(i, j, kk):
            bi, bj = block_index(i, j, kk)
            return (bi, layer, 0, bj)

        return pl.BlockSpec((chips, None, r_shard, cols), slabs), chips
    assert r_shard % rows == 0 and c_shard % cols == 0
    per_chip = (r_shard // rows) if split == "rows" else (c_shard // cols)

    def index(i, j, kk):
        bi, bj = block_index(i, j, kk)
        if split == "rows":
            return (bi // per_chip, layer, bi % per_chip, bj)
        return (bj // per_chip, layer, bi, bj % per_chip)

    return pl.BlockSpec((None, None, rows, cols), index), 1


def mm(name, a, b, *, ta=False, tb=False, out_dtype=F32, pro=None, epi=None, epi_tiles=(), epi_rows=(),
       tm=MM_TILE, tn=MM_TILE, tk=MM_TILE, b_split=None, b_layer=None, out_split=None, out_layer=None,
       after=None, norm_gain=None, norm_bwd=None):
    m, k = (a.shape[1], a.shape[0]) if ta else a.shape
    b_rows, b_cols = b.shape[-2], b.shape[-1]
    if b_split == "rows":
        b_rows *= N_CHIPS
    elif b_split == "cols":
        b_cols *= N_CHIPS
    n = b_rows if tb else b_cols
    assert (b_cols if tb else b_rows) == k
    tm, tn, tk = min(tm, m), min(tn, n), min(tk, k)
    if b_split == "cols":
        if tb:
            tk = min(tk, b.shape[-1])
        else:
            tn = min(tn, b.shape[-1])
    if out_split == "cols":
        tn = min(tn, n // N_CHIPS)
    assert m % tm == 0 and n % tn == 0 and k % tk == 0
    nk = k // tk
    a_spec = pl.BlockSpec((tk, tm), lambda i, j, kk: (kk, i)) if ta else pl.BlockSpec((tm, tk), lambda i, j, kk: (i, kk))
    b_block = (tn, tk) if tb else (tk, tn)
    b_index = (lambda i, j, kk: (j, kk)) if tb else (lambda i, j, kk: (kk, j))
    b_chips = o_chips = 1
    if b_split is None:
        b_spec = pl.BlockSpec(b_block, b_index)
    else:
        b_spec, b_chips = _stacked_spec(b.shape, b_split, b_layer, b_block[0], b_block[1], b_index)
    in_specs = [a_spec, b_spec]
    in_specs += [pl.BlockSpec((tm, tn), lambda i, j, kk: (i, j)) for _ in epi_tiles]
    in_specs += [pl.BlockSpec((1, tn), lambda i, j, kk: (0, j)) for _ in epi_rows]
    n_t, n_r = len(epi_tiles), len(epi_rows)
    dims = (((0 if ta else 1,), (1 if tb else 0,)), ((), ()))
    if out_split is None:
        out_shape = jax.ShapeDtypeStruct((m, n), out_dtype)
        out_spec = pl.BlockSpec((tm, tn), lambda i, j, kk: (i, j))
    else:
        shard = (m // N_CHIPS, n) if out_split == "rows" else (m, n // N_CHIPS)
        out_shape = jax.ShapeDtypeStruct((N_CHIPS, out_layer[1]) + shard, out_dtype)
        out_spec, o_chips = _stacked_spec(out_shape.shape, out_split, out_layer[0], tm, tn, lambda i, j, kk: (i, j))
    single_pass = nk == 1 and norm_gain is None and norm_bwd is None and b_chips == 1 and o_chips == 1
    extra = []
    row_spec = pl.BlockSpec((1, n), lambda i, j, kk: (0, 0))
    tile_spec = pl.BlockSpec((tm, tn), lambda i, j, kk: (i, j))
    if norm_gain is not None:
        assert tn == n and out_split is None
        extra.append(norm_gain)
        in_specs.append(row_spec)
        out_shape = [out_shape, jax.ShapeDtypeStruct((m, n), BF16)]
        out_spec = [out_spec, tile_spec]
    if norm_bwd is not None:
        assert tn == n and out_split is None and norm_gain is None
        extra += list(norm_bwd)
        in_specs += [tile_spec, row_spec, tile_spec]
        out_shape = [jax.ShapeDtypeStruct((m, n), F32), jax.ShapeDtypeStruct((m, n), BF16),
                     jax.ShapeDtypeStruct((1, n), F32)]
        out_spec = [tile_spec, tile_spec, row_spec]
    if after is not None:
        extra.append(after)
        in_specs.append(pl.BlockSpec(memory_space=pl.ANY))

    def body(a_ref, b_ref, *rest):
        tiles = rest[:n_t]
        rows = rest[n_t:n_t + n_r]
        gain_ref = rest[n_t + n_r] if norm_gain is not None else None
        bwd_refs = rest[n_t + n_r:n_t + n_r + 3] if norm_bwd is not None else None
        rest = rest[n_t + n_r + len(extra):]
        o_ref, acc_ref = rest[0], rest[-1]
        av = a_ref[...]
        if pro is not None:
            av = pro(av)
        if single_pass:
            out = _dot(av, b_ref[...], dims)
            if epi is not None:
                out = epi(out, *[t[...] for t in tiles], *[r[...] for r in rows])
            o_ref[...] = out.astype(out_dtype)
            return
        kk = pl.program_id(2)

        @pl.when(kk == 0)
        def _():
            acc_ref[...] = jnp.zeros_like(acc_ref)

        bv = b_ref[...]
        if b_chips > 1:
            bv = bv.reshape(b_block)
        acc_ref[...] += _dot(av, bv, dims)

        @pl.when(kk == nk - 1)
        def _():
            out = acc_ref[...]
            if epi is not None:
                out = epi(out, *[t[...] for t in tiles], *[r[...] for r in rows])
            if gain_ref is not None:
                rest[1][...] = (_rms_stats(out)[0] * gain_ref[...]).astype(BF16)
            if bwd_refs is not None:
                h_ref, g_ref, dres_ref = bwd_refs
                dh, dg = _rms_bwd_tile(out, h_ref[...], g_ref[...])
                total = dres_ref[...] + dh
                o_ref[...] = total
                rest[1][...] = total.astype(BF16)
                first = pl.program_id(0) == 0

                @pl.when(first)
                def _():
                    rest[2][...] = dg

                @pl.when(jnp.logical_not(first))
                def _():
                    rest[2][...] += dg

                return
            out = out.astype(out_dtype)
            o_ref[...] = out.reshape(o_chips, tm // o_chips, tn) if o_chips > 1 else out

    outer = "arbitrary" if norm_bwd is not None else "parallel"
    return pl.pallas_call(
        body, name=name, grid=(m // tm, n // tn, nk),
        in_specs=in_specs, out_specs=out_spec, out_shape=out_shape,
        scratch_shapes=[] if single_pass else [pltpu.VMEM((tm, tn), F32)],
        compiler_params=_cparams((outer, outer, "arbitrary")),
    )(a, b, *epi_tiles, *epi_rows, *extra)


def row_call(name, body, n_rows, tm, ins, outs, accs=()):
    tm = _pick_tile(n_rows, tm)
    in_specs = []
    for arr, kind in ins:
        if kind == "tile":
            if arr.ndim == 2:
                in_specs.append(pl.BlockSpec((tm, arr.shape[1]), lambda i: (i, 0)))
            else:
                in_specs.append(pl.BlockSpec((arr.shape[0], tm, arr.shape[2]), lambda i: (0, i, 0)))
        elif kind == "full":
            in_specs.append(pl.BlockSpec(arr.shape, functools.partial(lambda i, nd: (0,) * nd, nd=arr.ndim)))
        else:
            where, h = kind
            per = tm // h
            if where == "prev":
                in_specs.append(pl.BlockSpec((h, arr.shape[1]), functools.partial(
                    lambda i, per: (jnp.maximum(i * per - 1, 0), 0), per=per)))
            else:
                last = n_rows // h - 1
                in_specs.append(pl.BlockSpec((h, arr.shape[1]), functools.partial(
                    lambda i, per, last: (jnp.minimum((i + 1) * per, last), 0), per=per, last=last)))
    out_shape, out_specs = [], []
    for shape, dtype in outs:
        out_shape.append(jax.ShapeDtypeStruct(shape, dtype))
        if len(shape) == 2:
            out_specs.append(pl.BlockSpec((tm, shape[1]), lambda i: (i, 0)))
        else:
            out_specs.append(pl.BlockSpec((shape[0], tm, shape[2]), lambda i: (0, i, 0)))
    for shape in accs:
        out_shape.append(jax.ShapeDtypeStruct(shape, F32))
        out_specs.append(pl.BlockSpec(shape, lambda i: (0, 0)))
    n_in, n_out, n_acc = len(ins), len(outs), len(accs)

    def kern(*refs):
        i = pl.program_id(0)
        in_refs = refs[:n_in]
        out_refs = refs[n_in:n_in + n_out]
        acc_refs = refs[n_in + n_out:n_in + n_out + n_acc]
        if n_acc:
            @pl.when(i == 0)
            def _():
                for r in acc_refs:
                    r[...] = jnp.zeros_like(r)
        body(i, in_refs, out_refs, acc_refs)

    res = pl.pallas_call(
        kern, name=name, grid=(n_rows // tm,), in_specs=in_specs, out_specs=out_specs, out_shape=out_shape,
        compiler_params=_cparams(("arbitrary",) if n_acc else ("parallel",)),
    )(*[a for a, _ in ins])
    return list(res)


def _rms_stats(h):
    r = lax.rsqrt(jnp.mean(h * h, axis=-1, keepdims=True) + RMS_EPS)
    return h * r, r


def rms_fwd(name, h, g):
    def body(i, ins, outs, accs):
        xhat, _ = _rms_stats(ins[0][...])
        outs[0][...] = (xhat * ins[1][...]).astype(BF16)

    return row_call(name, body, h.shape[0], ROW_TILE, [(h, "tile"), (g, "full")], [(h.shape, BF16)])[0]


def _rms_bwd_tile(dn, h, g):
    xhat, r = _rms_stats(h)
    dxhat = dn * g
    dh = r * (dxhat - xhat * jnp.mean(dxhat * xhat, axis=-1, keepdims=True))
    dg = jnp.sum(dn * xhat, axis=0, keepdims=True)
    return dh, dg


def mem_norm_bwd(name, dn, mem, g):
    def body(i, ins, outs, accs):
        _, dg = _rms_bwd_tile(ins[0][...].astype(F32), ins[1][...], ins[2][...])
        accs[0][...] += dg

    return row_call(name, body, mem.shape[0], ROW_TILE, [(dn, "tile"), (mem, "tile"), (g, "full")], [],
                    [(1, mem.shape[1])])[0]


def loss_head(name, h, g, target):
    d = h.shape[1]

    def body(i, ins, outs, accs):
        hv, gv = ins[0][...], ins[1][...]
        xhat, _ = _rms_stats(hv)
        err = xhat * gv - ins[2][...]
        dy = err * (1.0 / d)
        dh, dg = _rms_bwd_tile(dy, hv, gv)
        outs[0][...] = dh
        outs[1][...] = dh.astype(BF16)
        accs[0][...] += jnp.full((8, LANES), 0.5 / d, F32) * jnp.sum(err * err)
        accs[1][...] += dg

    dh, dh16, loss, dg = row_call(name, body, h.shape[0], ROW_TILE, [(h, "tile"), (g, "full"), (target, "tile")],
                                  [(h.shape, F32), (h.shape, BF16)], [(8, LANES), (1, d)])
    return dh, dh16, loss, dg


def col_sum(name, x):
    def body(i, ins, outs, accs):
        accs[0][...] += jnp.sum(ins[0][...].astype(F32), axis=0, keepdims=True)

    return row_call(name, body, x.shape[0], ROW_TILE, [(x, "tile")], [], [(1, x.shape[1])])[0]


def _conv_taps(xcat, w_ref, cols, width, halo, tm):
    rows = halo + tm
    acc = None
    for j in range(width):
        s = width - 1 - j
        xs = xcat if s == 0 else pltpu.roll(xcat, s, 0)
        term = xs[halo:rows] * w_ref[j:j + 1, cols]
        acc = term if acc is None else acc + term
    return acc


def _conv_taps_bwd_x(dcat, w_ref, cols, width, halo, tm):
    rows = halo + tm
    acc = None
    for j in range(width):
        s = width - 1 - j
        ds = dcat if s == 0 else pltpu.roll(dcat, rows - s, 0)
        term = ds[0:tm] * w_ref[j:j + 1, cols]
        acc = term if acc is None else acc + term
    return acc


def _conv_taps_bwd_w(dy, xcat, width, halo, tm, wrows):
    rows = halo + tm
    rid = _iota((wrows, dy.shape[1]), 0)
    out = jnp.zeros((wrows, dy.shape[1]), F32)
    for j in range(width):
        s = width - 1 - j
        xs = xcat if s == 0 else pltpu.roll(xcat, s, 0)
        v = jnp.sum(dy * xs[halo:rows], axis=0, keepdims=True)
        out = out + jnp.where(rid == j, v, 0.0)
    return out


def dn_pre(qkv_raw, ba, w_conv, gate):
    s_len = qkv_raw.shape[0]
    tm = min(DN_ROW_TILE, s_len)
    n_blk = qkv_raw.shape[1] // LANES

    def body(i, ins, outs, accs):
        x_ref, xp_ref, ba_ref, w_ref, gate_ref = ins
        qkv_ref, hs_ref = outs

        def blk(cb, carry):
            cols = pl.ds(pl.multiple_of(cb * LANES, LANES), LANES)
            prev = jnp.where(i > 0, xp_ref[:, cols], 0.0)
            xcat = jnp.concatenate([prev, x_ref[:, cols]], axis=0)
            c = _conv_taps(xcat, w_ref, cols, DN_CONV, DN_HALO, tm)
            y = _silu(c)
            rs = lax.rsqrt(jnp.sum(y * y, axis=-1, keepdims=True) + L2_EPS)
            fac = jnp.where(cb < DN_HEADS, DN_HEAD_DIM ** -0.5, 1.0)
            qkv_ref[:, cols] = jnp.where(cb < 2 * DN_HEADS, y * (rs * fac), y)
            return carry

        lax.fori_loop(0, n_blk, blk, 0, unroll=2)

        bav = ba_ref[...]
        beta = _sigmoid(bav)
        g = -jnp.exp(gate_ref[0:1, :]) * _softplus(bav + gate_ref[1:2, :])
        lane = _iota((tm, LANES), 1)
        g = jnp.where((lane >= DN_HEADS) & (lane < 2 * DN_HEADS), g, 0.0)
        r = _iota((tm, tm), 0)
        c = _iota((tm, tm), 1)
        tri = jnp.where((r >= c) & ((r >> CHUNK_SHIFT) == (c >> CHUNK_SHIFT)), 1.0, 0.0)
        gc = _dot_hi(tri, g)
        for h in range(DN_HEADS):
            hs_ref[h] = jnp.where(lane == 0, _lane_col(beta, lane, h),
                                  jnp.where(lane == 1, _lane_col(g, lane, DN_HEADS + h),
                                            jnp.where(lane == 2, _lane_col(gc, lane, DN_HEADS + h), 0.0)))

    return row_call("dn_pre", body, s_len, tm,
                    [(qkv_raw, "tile"), (qkv_raw, ("prev", DN_HALO)), (ba, "tile"), (w_conv, "full"), (gate, "full")],
                    [(qkv_raw.shape, F32), ((DN_HEADS, s_len, LANES), F32)])


def _chunk_masks():
    r = _iota((DN_CHUNK, DN_CHUNK), 0)
    c = _iota((DN_CHUNK, DN_CHUNK), 1)
    return r, c


def _decay_matrix(gc, r, c):
    gc_row = jnp.sum(jnp.where(r == c, gc, 0.0), axis=0, keepdims=True)
    causal = r >= c
    return jnp.where(causal, jnp.exp(jnp.where(causal, gc - gc_row, 0.0)), 0.0)


def _tri_inverse(lows, r, c):
    eye = jnp.where(r == c, 1.0, 0.0)
    ts = [eye for _ in lows]
    b = 1
    while b < DN_CHUNK:
        shift = b.bit_length()
        sel = ((r >> shift) == (c >> shift)) & ((r & b) != 0) & ((c & b) == 0)
        lms = [jnp.where(sel, low, 0.0) for low in lows]
        if b == 1:
            ts = [t - lm for t, lm in zip(ts, lms)]
        else:
            t_lm = [_dot_x3(t, lm) for t, lm in zip(ts, lms)]
            t_lm_t = [_dot_x3(x, t) for x, t in zip(t_lm, ts)]
            ts = [t - x for t, x in zip(ts, t_lm_t)]
        b *= 2
    return ts


def dn_solve(qkv, hs):
    s_len = qkv.shape[0]
    rb = min(ROW_TILE, s_len)
    n_chunk = rb // DN_CHUNK
    interleave = min(SOLVE_INTERLEAVE, n_chunk)

    def body(k_ref, v_ref, hs_ref, u_ref, w_ref, t_ref):
        r, c = _chunk_masks()

        def group(gi, carry):
            rows = [pl.ds(pl.multiple_of((gi * interleave + j) * DN_CHUNK, DN_CHUNK), DN_CHUNK)
                    for j in range(interleave)]
            k = [k_ref[rw, :] for rw in rows]
            beta = [hs_ref[rw, 0:1] for rw in rows]
            gc = [hs_ref[rw, 2:3] for rw in rows]
            kb = [a * b for a, b in zip(k, beta)]
            decay = [_decay_matrix(g, r, c) for g in gc]
            lows = [jnp.where(r > c, _dot_nt(a, b) * d, 0.0) for a, b, d in zip(kb, k, decay)]
            ts = _tri_inverse(lows, r, c)
            us = [_dot_x3(t, v_ref[rw, :] * b) for t, rw, b in zip(ts, rows, beta)]
            ws = [_dot_x3(t, a * jnp.exp(g)) for t, a, g in zip(ts, kb, gc)]
            for j, rw in enumerate(rows):
                u_ref[rw, :] = us[j]
                w_ref[rw, :] = ws[j].astype(BF16)
                t_ref[rw, :] = ts[j]
            return carry

        lax.fori_loop(0, n_chunk // interleave, group, 0)

    return pl.pallas_call(
        body, name="dn_solve", grid=(DN_HEADS, s_len // rb),
        in_specs=[pl.BlockSpec((rb, LANES), lambda h, i: (i, DN_HEADS + h)),
                  pl.BlockSpec((rb, LANES), lambda h, i: (i, 2 * DN_HEADS + h)),
                  pl.BlockSpec((None, rb, LANES), lambda h, i: (h, i, 0))],
        out_specs=[pl.BlockSpec((rb, LANES), lambda h, i: (i, h)),
                   pl.BlockSpec((rb, LANES), lambda h, i: (i, h)),
                   pl.BlockSpec((None, rb, DN_CHUNK), lambda h, i: (h, i, 0))],
        out_shape=[jax.ShapeDtypeStruct((s_len, DN_HEADS * LANES), F32),
                   jax.ShapeDtypeStruct((s_len, DN_HEADS * LANES), BF16),
                   jax.ShapeDtypeStruct((DN_HEADS, s_len, DN_CHUNK), F32)],
        compiler_params=_cparams(("parallel", "parallel")),
    )(qkv, qkv, hs)


def dn_scan_fwd(qkv, u, w, hs):
    s_len = qkv.shape[0]
    rb = min(ROW_TILE, s_len)
    n_chunk = rb // DN_CHUNK
    total_chunks = s_len // DN_CHUNK

    hps = FWD_HEADS_PER_STEP
    groups = DN_HEADS // hps

    def body(q_ref, k_ref, u_ref, w_ref, hs_ref, o_ref, st_ref, state):
        @pl.when(pl.program_id(1) == 0)
        def _():
            state[...] = jnp.zeros_like(state)

        r, c = _chunk_masks()

        def chunk(n, carry):
            rows = pl.ds(pl.multiple_of(n * DN_CHUNK, DN_CHUNK), DN_CHUNK)
            heads = range(hps)
            cols = [slice(h * LANES, (h + 1) * LANES) for h in heads]
            each = lambda f, *xs: [f(*a) for a in zip(*xs)]
            q = [q_ref[rows, cl] for cl in cols]
            k = [k_ref[rows, cl] for cl in cols]
            gc = [hs_ref[h, rows, 2:3] for h in heads]
            st = [state[h] for h in heads]
            for h in heads:
                st_ref[h, n] = st[h]
            gl = each(lambda g: jnp.min(g, axis=0, keepdims=True), gc)
            decay = each(lambda g: _decay_matrix(g, r, c), gc)
            w_st = [_dot(w_ref[rows, cols[h]], st[h]) for h in heads]
            qk = each(_dot_nt, q, k)
            q_st = each(lambda a, g, s: _dot(a * jnp.exp(g), s), q, gc, st)
            vn = [u_ref[rows, cols[h]] - w_st[h] for h in heads]
            ai_vn = each(lambda a, d, b: _dot(a * d, b), qk, decay, vn)
            kd_vn = each(lambda a, g0, g, b: _dot_tn(a * jnp.exp(g0 - g), b), k, gl, gc, vn)
            for h in heads:
                o_ref[rows, cols[h]] = q_st[h] + ai_vn[h]
                state[h] = st[h] * jnp.exp(gl[h]) + kd_vn[h]
            return carry

        lax.fori_loop(0, n_chunk, chunk, 0)

    wide = hps * LANES
    blk = lambda off: pl.BlockSpec((rb, wide), lambda h, i: (i, off + h))
    return pl.pallas_call(
        body, name="dn_scan_fwd", grid=(groups, s_len // rb),
        in_specs=[blk(0), blk(groups), blk(0), blk(0),
                  pl.BlockSpec((hps, rb, LANES), lambda h, i: (h, i, 0))],
        out_specs=[blk(0),
                   pl.BlockSpec((hps, n_chunk, LANES, LANES), lambda h, i: (h, i, 0, 0))],
        out_shape=[jax.ShapeDtypeStruct((s_len, DN_HEADS * LANES), F32),
                   jax.ShapeDtypeStruct((DN_HEADS, total_chunks, LANES, LANES), F32)],
        scratch_shapes=[pltpu.VMEM((hps, LANES, LANES), F32)],
        compiler_params=_cparams(("parallel", "arbitrary")),
    )(qkv, qkv, u, w, hs)


def dn_scan_bwd(qkv, u, w, t_inv, hs, states, d_o):
    s_len = qkv.shape[0]
    rb = min(BWD_SCAN_ROWS, s_len)
    n_chunk = rb // DN_CHUNK
    n_blk = s_len // rb
    hps = BWD_HEADS_PER_STEP
    groups = DN_HEADS // hps

    def body(q_ref, k_ref, v_ref, u_ref, w_ref, t_ref, hs_ref, st_ref, do_ref,
             dq_ref, dk_ref, dv_ref, dhs_ref, dstate):
        @pl.when(pl.program_id(1) == 0)
        def _():
            dstate[...] = jnp.zeros_like(dstate)

        r, c = _chunk_masks()
        causal = r >= c
        strict = r > c
        lane = _iota((DN_CHUNK, LANES), 1)
        upper = jnp.where(r <= c, 1.0, 0.0)
        last_row = _iota((DN_CHUNK, 1), 0) == DN_CHUNK - 1

        def chunk(m, carry):
            n = n_chunk - 1 - m
            rows = pl.ds(pl.multiple_of(n * DN_CHUNK, DN_CHUNK), DN_CHUNK)
            heads = range(hps)
            cols = [slice(h * LANES, (h + 1) * LANES) for h in heads]
            each = lambda f, *xs: [f(*a) for a in zip(*xs)]
            rsum = lambda x: jnp.sum(x, axis=-1, keepdims=True)
            dims_tn = (((0,), (0,)), ((), ()))
            q = [q_ref[rows, cl] for cl in cols]
            k = [k_ref[rows, cl] for cl in cols]
            v = [v_ref[rows, cl] for cl in cols]
            uu = [u_ref[rows, cl] for cl in cols]
            ww = [w_ref[rows, cl] for cl in cols]
            do = [do_ref[rows, cl] for cl in cols]
            tt = [t_ref[h, rows, :] for h in heads]
            beta = [hs_ref[h, rows, 0:1] for h in heads]
            gc = [hs_ref[h, rows, 2:3] for h in heads]
            st = [st_ref[h, n] for h in heads]
            dst = [dstate[h] for h in heads]
            gl = each(lambda g: jnp.min(g, axis=0, keepdims=True), gc)
            egc = each(jnp.exp, gc)
            egl = each(jnp.exp, gl)
            ekd = each(lambda a, b: jnp.exp(a - b), gl, gc)
            decay = each(lambda g: _decay_matrix(g, r, c), gc)
            qd = each(jnp.multiply, q, egc)
            kd = each(jnp.multiply, k, ekd)
            kb = each(jnp.multiply, k, beta)
            w_st = each(_dot, ww, st)
            qk = each(_dot_nt, q, k)
            dqd = each(_dot_nt, do, st)
            kd_dst = each(_dot, kd, dst)
            qd_do = each(_dot_tn, qd, do)
            kbk = each(_dot_nt, kb, k)
            vn = each(jnp.subtract, uu, w_st)
            ai = each(jnp.multiply, qk, decay)
            low = each(lambda a, d: jnp.where(strict, a * d, 0.0), kbk, decay)
            dai = each(lambda a, b: jnp.where(causal, _dot_nt(a, b), 0.0), do, vn)
            ai_do = each(_dot_tn, ai, do)
            dkd = each(_dot_nt, vn, dst)
            dvn = each(jnp.add, ai_do, kd_dst)
            dp = each(jnp.multiply, dai, decay)
            dw = each(lambda a, b: -_dot_nt(a, b), dvn, st)
            w_dvn = each(_dot_tn, ww, dvn)
            dp_k = each(_dot, dp, k)
            dp_q = each(_dot_tn, dp, q)
            drhs_u = each(lambda a, b: _dot_x3(a, b, dims_tn), tt, dvn)
            dgl = each(lambda a, b, e: jnp.sum(a * b) * e, dst, st, egl)
            for h in heads:
                dstate[h] = dst[h] * egl[h] + qd_do[h] - w_dvn[h]
            dq = each(lambda a, e, b: a * e + b, dqd, egc, dp_k)
            dk_a = each(lambda a, e, b: a * e + b, dkd, ekd, dp_q)
            rkd = each(lambda a, b: rsum(a * b), dkd, kd)
            drhs_w = each(lambda a, b: _dot_x3(a, b, dims_tn), tt, dw)
            dl_u = each(_dot_nt, drhs_u, uu)
            dl_w = each(_dot_nt, drhs_w, ww)
            dlow = each(lambda a, b: jnp.where(strict, -(a + b), 0.0), dl_u, dl_w)
            dqm = each(jnp.multiply, dlow, decay)
            m_tot = each(lambda a, b, d, e: a * b + d * e, dai, ai, dlow, low)
            dqm_k = each(_dot, dqm, k)
            dk_l = each(_dot_tn, dqm, kb)
            col_rows = each(lambda m: jnp.sum(m, axis=0, keepdims=True), m_tot)
            col_sums = each(lambda rw: jnp.sum(jnp.where(r == c, rw, 0.0), axis=1, keepdims=True), col_rows)
            dkb_w = each(jnp.multiply, drhs_w, egc)
            dkb = each(jnp.add, dkb_w, dqm_k)
            dgc = [rsum(dqd[h] * qd[h]) - rkd[h] + jnp.where(last_row, jnp.sum(rkd[h]) + dgl[h], 0.0)
                   + rsum(m_tot[h]) + rsum(dkb_w[h] * kb[h]) for h in heads]
            dg = each(lambda a, b: _dot_hi(upper, jnp.where(lane == 1, a - b, 0.0)), dgc, col_sums)
            for h in heads:
                dq_ref[rows, cols[h]] = dq[h]
                dk_ref[rows, cols[h]] = dk_a[h] + dk_l[h] + dkb[h] * beta[h]
                dv_ref[rows, cols[h]] = drhs_u[h] * beta[h]
                dbeta = rsum(drhs_u[h] * v[h]) + rsum(dkb[h] * k[h])
                dhs_ref[h, rows, :] = jnp.where(lane == 0, dbeta, dg[h])
            return carry

        lax.fori_loop(0, n_chunk, chunk, 0)

    wide = hps * LANES
    blk = lambda off: pl.BlockSpec((rb, wide), lambda h, i: (n_blk - 1 - i, off + h))
    head = blk(0)
    hs_spec = pl.BlockSpec((hps, rb, LANES), lambda h, i: (h, n_blk - 1 - i, 0))
    full = jax.ShapeDtypeStruct((s_len, DN_HEADS * LANES), F32)
    return pl.pallas_call(
        body, name="dn_scan_bwd", grid=(groups, n_blk),
        in_specs=[blk(0), blk(groups), blk(2 * groups), head, head,
                  pl.BlockSpec((hps, rb, DN_CHUNK), lambda h, i: (h, n_blk - 1 - i, 0)), hs_spec,
                  pl.BlockSpec((hps, n_chunk, LANES, LANES), lambda h, i: (h, n_blk - 1 - i, 0, 0)), head],
        out_specs=[head, head, head, hs_spec],
        out_shape=[full, full, full, jax.ShapeDtypeStruct((DN_HEADS, s_len, LANES), F32)],
        scratch_shapes=[pltpu.VMEM((hps, LANES, LANES), F32)],
        compiler_params=_cparams(("parallel", "arbitrary")),
    )(qkv, qkv, qkv, u, w, t_inv, hs, states, d_o)


def dn_post(o, z, out_norm):
    def body(i, ins, outs, accs):
        gn = ins[2][...]
        for h in range(DN_HEADS):
            cols = slice(h * LANES, (h + 1) * LANES)
            xhat, _ = _rms_stats(ins[0][:, cols])
            outs[0][:, cols] = (xhat * gn * _silu(ins[1][:, cols])).astype(BF16)

    return row_call("dn_post", body, o.shape[0], ROW_TILE, [(o, "tile"), (z, "tile"), (out_norm, "full")],
                    [(o.shape, BF16)])[0]


def dn_post_bwd(d_og, o, z, out_norm):
    def body(i, ins, outs, accs):
        gn = ins[3][...]
        dgn = jnp.zeros((1, LANES), F32)
        for h in range(DN_HEADS):
            cols = slice(h * LANES, (h + 1) * LANES)
            dy, zh = ins[0][:, cols].astype(F32), ins[2][:, cols]
            xhat, r = _rms_stats(ins[1][:, cols])
            sz = _silu(zh)
            dgn = dgn + jnp.sum(dy * xhat * sz, axis=0, keepdims=True)
            outs[1][:, cols] = (dy * xhat * gn * _silu_grad(zh)).astype(BF16)
            dxhat = dy * gn * sz
            outs[0][:, cols] = r * (dxhat - xhat * jnp.mean(dxhat * xhat, axis=-1, keepdims=True))
        accs[0][...] += dgn

    return row_call("dn_post_bwd", body, o.shape[0], ROW_TILE,
                    [(d_og, "tile"), (o, "tile"), (z, "tile"), (out_norm, "full")],
                    [(o.shape, F32), (o.shape, BF16)], [(1, LANES)])


def dn_pre_bwd(dq, dk, dv, dhs, qkv_raw, ba, w_conv, gate):
    s_len = qkv_raw.shape[0]
    tm = min(DN_ROW_TILE, s_len)

    def body(i, ins, outs, accs):
        dq_ref, dk_ref, dv_ref, dhs_ref, x_ref, xp_ref, ba_ref, w_ref, gate_ref = ins
        dc_ref, dba_ref = outs

        def blk(cb, carry):
            cols = pl.ds(pl.multiple_of(cb * LANES, LANES), LANES)
            hcols = pl.ds(pl.multiple_of((cb & (DN_HEADS - 1)) * LANES, LANES), LANES)
            prev = jnp.where(i > 0, xp_ref[:, cols], 0.0)
            xcat = jnp.concatenate([prev, x_ref[:, cols]], axis=0)
            c = _conv_taps(xcat, w_ref, cols, DN_CONV, DN_HALO, tm)
            y = _silu(c)
            dy = jnp.where(cb < DN_HEADS, dq_ref[:, hcols],
                           jnp.where(cb < 2 * DN_HEADS, dk_ref[:, hcols], dv_ref[:, hcols]))
            rs = lax.rsqrt(jnp.sum(y * y, axis=-1, keepdims=True) + L2_EPS)
            fac = jnp.where(cb < DN_HEADS, DN_HEAD_DIM ** -0.5, 1.0)
            nrm = y * rs
            dn = dy * fac
            dy_norm = rs * (dn - nrm * jnp.sum(dn * nrm, axis=-1, keepdims=True))
            dc_ref[:, cols] = jnp.where(cb < 2 * DN_HEADS, dy_norm, dy) * _silu_grad(c)
            return carry

        lax.fori_loop(0, qkv_raw.shape[1] // LANES, blk, 0, unroll=2)

        lane = _iota((tm, LANES), 1)
        dbeta = jnp.zeros((tm, LANES), F32)
        dg = jnp.zeros((tm, LANES), F32)
        for h in range(DN_HEADS):
            dbeta = dbeta + jnp.where(lane == h, dhs_ref[h, :, 0:1], 0.0)
            dg = dg + jnp.where(lane == DN_HEADS + h, dhs_ref[h, :, 1:2], 0.0)
        bav = ba_ref[...]
        beta = _sigmoid(bav)
        ea = jnp.exp(gate_ref[0:1, :])
        pre = bav + gate_ref[1:2, :]
        g = -ea * _softplus(pre)
        da = dg * (-ea) * _sigmoid(pre)
        dba_ref[...] = (dbeta * beta * (1.0 - beta) + da).astype(BF16)
        rid = _iota((8, LANES), 0)
        accs[0][...] += (jnp.where(rid == 0, jnp.sum(dg * g, axis=0, keepdims=True), 0.0)
                         + jnp.where(rid == 1, jnp.sum(da, axis=0, keepdims=True), 0.0))

    return row_call("dn_pre_bwd", body, s_len, tm,
                    [(dq, "tile"), (dk, "tile"), (dv, "tile"), (dhs, "tile"), (qkv_raw, "tile"),
                     (qkv_raw, ("prev", DN_HALO)), (ba, "tile"), (w_conv, "full"), (gate, "full")],
                    [(qkv_raw.shape, F32), (ba.shape, BF16)], [(8, LANES)])


def dn_conv_bwd(dc, qkv_raw, w_conv):
    s_len = dc.shape[0]
    tm = min(DN_ROW_TILE, s_len)
    nt = s_len // tm

    def body(i, ins, outs, accs):
        dc_ref, dn_ref, x_ref, xp_ref, w_ref = ins

        def blk(cb, carry):
            cols = pl.ds(pl.multiple_of(cb * LANES, LANES), LANES)
            dy = dc_ref[:, cols]
            nxt = jnp.where(i < nt - 1, dn_ref[:, cols], 0.0)
            dcat = jnp.concatenate([dy, nxt], axis=0)
            outs[0][:, cols] = _conv_taps_bwd_x(dcat, w_ref, cols, DN_CONV, DN_HALO, tm).astype(BF16)
            prev = jnp.where(i > 0, xp_ref[:, cols], 0.0)
            xcat = jnp.concatenate([prev, x_ref[:, cols]], axis=0)
            accs[0][:, cols] += _conv_taps_bwd_w(dy, xcat, DN_CONV, DN_HALO, tm, 8)
            return carry

        lax.fori_loop(0, dc.shape[1] // LANES, blk, 0, unroll=2)

    return row_call("dn_conv_bwd", body, s_len, tm,
                    [(dc, "tile"), (dc, ("next", DN_HALO)), (qkv_raw, "tile"), (qkv_raw, ("prev", DN_HALO)),
                     (w_conv, "full")],
                    [(dc.shape, BF16)], [(8, dc.shape[1])])


def _glu(u_ref, cols, d):
    return u_ref[:, cols] * _sigmoid(u_ref[:, pl.ds(pl.multiple_of(d + cols.start, LANES), cols.size)])


def cv_core_fwd(u, w_dw, b_dw, ln_g, ln_b):
    s_len, d = u.shape[0], u.shape[1] // 2
    tm = min(CONV_ROW_TILE, s_len)

    def body(i, ins, outs, accs):
        u_ref, up_ref, w_ref, bdw_ref, g_ref, b_ref = ins
        s_ref, c_ref = outs

        def blk(cb, carry):
            cols = pl.ds(pl.multiple_of(cb * LANES, LANES), LANES)
            prev = jnp.where(i > 0, _glu(up_ref, cols, d), 0.0)
            xcat = jnp.concatenate([prev, _glu(u_ref, cols, d)], axis=0)
            c_ref[:, cols] = _conv_taps(xcat, w_ref, cols, CV_WIDTH, CV_HALO, tm) + bdw_ref[:, cols]
            return carry

        lax.fori_loop(0, d // LANES, blk, 0)
        c = c_ref[...]
        mu = jnp.mean(c, axis=-1, keepdims=True)
        xc = c - mu
        rstd = lax.rsqrt(jnp.mean(xc * xc, axis=-1, keepdims=True) + LN_EPS)
        s_ref[...] = _silu(xc * rstd * g_ref[...] + b_ref[...]).astype(BF16)

    return row_call("cv_core_fwd", body, s_len, tm,
                    [(u, "tile"), (u, ("prev", CV_HALO)), (w_dw, "full"), (b_dw, "full"), (ln_g, "full"),
                     (ln_b, "full")],
                    [((s_len, d), BF16), ((s_len, d), F32)])


def cv_ln_bwd(ds, c, ln_g, ln_b):
    def body(i, ins, outs, accs):
        cv, g = ins[1][...], ins[2][...]
        mu = jnp.mean(cv, axis=-1, keepdims=True)
        xc = cv - mu
        rstd = lax.rsqrt(jnp.mean(xc * xc, axis=-1, keepdims=True) + LN_EPS)
        xhat = xc * rstd
        dl = ins[0][...].astype(F32) * _silu_grad(xhat * g + ins[3][...])
        dxhat = dl * g
        dc = rstd * (dxhat - jnp.mean(dxhat, axis=-1, keepdims=True)
                     - xhat * jnp.mean(dxhat * xhat, axis=-1, keepdims=True))
        outs[0][...] = dc
        rid = _iota((8, cv.shape[1]), 0)
        accs[0][...] += (jnp.where(rid == 0, jnp.sum(dl * xhat, axis=0, keepdims=True), 0.0)
                         + jnp.where(rid == 1, jnp.sum(dl, axis=0, keepdims=True), 0.0)
                         + jnp.where(rid == 2, jnp.sum(dc, axis=0, keepdims=True), 0.0))

    return row_call("cv_ln_bwd", body, c.shape[0], ROW_TILE,
                    [(ds, "tile"), (c, "tile"), (ln_g, "full"), (ln_b, "full")], [(c.shape, F32)], [(8, c.shape[1])])


def cv_conv_bwd(dc, u, w_dw):
    s_len, d = dc.shape
    tm = min(CONV_ROW_TILE, s_len)
    nt = s_len // tm

    def body(i, ins, outs, accs):
        dc_ref, dn_ref, u_ref, up_ref, w_ref = ins

        def blk(cb, carry):
            cols = pl.ds(pl.multiple_of(cb * LANES, LANES), LANES)
            gcols = pl.ds(pl.multiple_of(d + cb * LANES, LANES), LANES)
            dy = dc_ref[:, cols]
            nxt = jnp.where(i < nt - 1, dn_ref[:, cols], 0.0)
            dgl = _conv_taps_bwd_x(jnp.concatenate([dy, nxt], axis=0), w_ref, cols, CV_WIDTH, CV_HALO, tm)
            u1, sg = u_ref[:, cols], _sigmoid(u_ref[:, gcols])
            du1 = dgl * sg
            du2 = dgl * u1 * sg * (1.0 - sg)
            outs[0][:, cols] = du1.astype(BF16)
            outs[0][:, gcols] = du2.astype(BF16)
            accs[1][:, cols] += jnp.sum(du1, axis=0, keepdims=True)
            accs[1][:, gcols] += jnp.sum(du2, axis=0, keepdims=True)
            prev = jnp.where(i > 0, _glu(up_ref, cols, d), 0.0)
            xcat = jnp.concatenate([prev, u1 * sg], axis=0)
            accs[0][:, cols] += _conv_taps_bwd_w(dy, xcat, CV_WIDTH, CV_HALO, tm, CV_HALO)
            return carry

        lax.fori_loop(0, d // LANES, blk, 0)

    return row_call("cv_conv_bwd", body, s_len, tm,
                    [(dc, "tile"), (dc, ("next", CV_HALO)), (u, "tile"), (u, ("prev", CV_HALO)), (w_dw, "full")],
                    [(u.shape, BF16)], [(CV_HALO, d), (1, 2 * d)])


def xa_core_fwd(name, q, kv):
    d = q.shape[1]

    def body(i, ins, outs, accs):
        for h in range(XA_HEADS):
            cols = slice(h * XA_HEAD_DIM, (h + 1) * XA_HEAD_DIM)
            vcols = slice(d + h * XA_HEAD_DIM, d + (h + 1) * XA_HEAD_DIM)
            s = _dot_nt(ins[0][:, cols], ins[1][:, cols]) * (XA_HEAD_DIM ** -0.5)
            e = jnp.exp(s - jnp.max(s, axis=-1, keepdims=True))
            p = e / jnp.sum(e, axis=-1, keepdims=True)
            outs[0][:, cols] = _dot(p, ins[1][:, vcols]).astype(BF16)

    return row_call(name, body, q.shape[0], ROW_TILE, [(q, "tile"), (kv, "full")], [(q.shape, BF16)])[0]


def xa_core_bwd(name, d_o, q, kv):
    d = q.shape[1]

    def body(i, ins, outs, accs):
        for h in range(XA_HEADS):
            cols = slice(h * XA_HEAD_DIM, (h + 1) * XA_HEAD_DIM)
            vcols = slice(d + h * XA_HEAD_DIM, d + (h + 1) * XA_HEAD_DIM)
            qh, kh, vh, doh = ins[1][:, cols], ins[2][:, cols], ins[2][:, vcols], ins[0][:, cols]
            s = _dot_nt(qh, kh) * (XA_HEAD_DIM ** -0.5)
            e = jnp.exp(s - jnp.max(s, axis=-1, keepdims=True))
            p = e / jnp.sum(e, axis=-1, keepdims=True)
            dp = _dot_nt(doh, vh)
            ds = p * (dp - jnp.sum(dp * p, axis=-1, keepdims=True)) * (XA_HEAD_DIM ** -0.5)
            outs[0][:, cols] = _dot(ds, kh).astype(BF16)
            accs[0][:, cols] += _dot_tn(ds, qh)
            accs[0][:, vcols] += _dot_tn(p, doh)

    return row_call(name, body, q.shape[0], ROW_TILE, [(d_o, "tile"), (q, "tile"), (kv, "full")],
                    [(q.shape, BF16)], [kv.shape])


def adamw(name, w, g, m, v):
    def body(i, ins, outs, accs):
        wv, gv = ins[0][...], ins[1][...]
        mn = ADAM_B1 * ins[2][...] + (1.0 - ADAM_B1) * gv
        vn = ADAM_B2 * ins[3][...] + (1.0 - ADAM_B2) * jnp.square(gv)
        m_hat = mn / (1.0 - ADAM_B1 ** ADAM_STEP)
        v_hat = vn / (1.0 - ADAM_B2 ** ADAM_STEP)
        outs[0][...] = -ADAM_LR * (m_hat / (jnp.sqrt(v_hat) + ADAM_EPS) + ADAM_WD * wv)
        outs[1][...] = mn
        outs[2][...] = vn

    return row_call(name, body, w.shape[0], ROW_TILE, [(w, "tile"), (g, "tile"), (m, "tile"), (v, "tile")],
                    [(w.shape, F32)] * 3)


def adamw_halves(name, w, g_mine, g_sibling, m, v, core):
    n_layers = len(g_mine)
    rows, cols = w.shape
    half_rows = rows // n_layers // 2
    tm = _pick_tile(half_rows, ADAMW_ROW_TILE)
    per_half = half_rows // tm

    def body(core_ref, w_ref, *rest):
        g_refs = rest[:2 * n_layers]
        m_ref, v_ref, g_out, d_out, m_out, v_out = rest[2 * n_layers:]
        i = pl.program_id(0)
        mine = ((i // per_half) % 2) == core_ref[0]
        layer = i // (2 * per_half)
        gv = jnp.where(mine, g_refs[0][...], g_refs[n_layers][...])
        for l in range(1, n_layers):
            gv = jnp.where(layer == l, jnp.where(mine, g_refs[l][...], g_refs[n_layers + l][...]), gv)
        mn = ADAM_B1 * m_ref[...] + (1.0 - ADAM_B1) * gv
        vn = ADAM_B2 * v_ref[...] + (1.0 - ADAM_B2) * jnp.square(gv)
        m_hat = mn / (1.0 - ADAM_B1 ** ADAM_STEP)
        v_hat = vn / (1.0 - ADAM_B2 ** ADAM_STEP)
        g_out[...] = gv
        d_out[...] = -ADAM_LR * (m_hat / (jnp.sqrt(v_hat) + ADAM_EPS) + ADAM_WD * w_ref[...])
        m_out[...] = mn
        v_out[...] = vn

    whole = pl.BlockSpec((tm, cols), lambda i, core_ref: (i, 0))

    def half(layer, own):
        def index(i, core_ref):
            used = (i // (2 * per_half) == layer) & ((((i // per_half) % 2) == core_ref[0]) == own)
            return (jnp.where(used, i % per_half, 0), 0)

        return pl.BlockSpec((tm, cols), index)

    halves = [half(l, True) for l in range(n_layers)] + [half(l, False) for l in range(n_layers)]
    return pl.pallas_call(
        body, name=name,
        grid_spec=pltpu.PrefetchScalarGridSpec(
            num_scalar_prefetch=1, grid=(2 * per_half * n_layers,),
            in_specs=[whole] + halves + [whole, whole], out_specs=[whole] * 4),
        out_shape=[jax.ShapeDtypeStruct(w.shape, F32)] * 4,
        compiler_params=_cparams(("parallel",)),
    )(core, w, *g_mine, *g_sibling, m, v)


HBM_SPEC = pl.BlockSpec(memory_space=pltpu.HBM)


def _position():
    return lax.axis_index("x"), lax.axis_index("y"), lax.axis_index("c")


def _other_chips(x, y):
    return [(1 - x, y), (x, 1 - y), (1 - x, 1 - y)]


def _row_chunks(rows):
    return rows // D2D_CHUNK_ROWS if rows % D2D_CHUNK_ROWS == 0 else 1


def _start_chunked(make, rows):
    k = _row_chunks(rows)
    for i in range(k):
        make(i * (rows // k), rows // k).start()


def gather_shards(packs):
    n = len(packs)

    def body(*refs):
        srcs, outs = refs[:n], refs[n:2 * n]
        send_sems, recv_sems = refs[2 * n:]
        x, y, c = _position()
        me = 2 * x + y
        chips = _other_chips(x, y)
        sibling = (x, y, 1 - c)

        def over_ici(a, j):
            px, py = chips[j]
            rows = srcs[a].shape[0] // 2
            return pltpu.make_async_remote_copy(
                src_ref=srcs[a].at[pl.ds(c * rows, rows), :], dst_ref=outs[a].at[me, pl.ds(c * rows, rows), :],
                send_sem=send_sems.at[a, j], recv_sem=recv_sems.at[a, j], device_id=(px, py, c), device_id_type=MESH)

        def landed(a, j):
            px, py = chips[j]
            rows = srcs[a].shape[0] // 2
            part = outs[a].at[2 * px + py, pl.ds(c * rows, rows), :]
            return pltpu.make_async_remote_copy(
                src_ref=part, dst_ref=part, send_sem=send_sems.at[a, j], recv_sem=recv_sems.at[a, j],
                device_id=(px, py, c), device_id_type=MESH)

        def over_d2d(a, j, cc, off, size):
            px, py = chips[j]
            rows = srcs[a].shape[0] // 2
            part = outs[a].at[2 * px + py, pl.ds(cc * rows + off, size), :]
            return pltpu.make_async_remote_copy(
                src_ref=part, dst_ref=part, send_sem=send_sems.at[a, 3 + j], recv_sem=recv_sems.at[a, 3 + j],
                device_id=sibling, device_id_type=MESH)

        for a in range(n):
            for j in range(3):
                over_ici(a, j).start()
        for a in range(n):
            for j in range(3):
                landed(a, j).wait_recv()
                _start_chunked(functools.partial(over_d2d, a, j, c), srcs[a].shape[0] // 2)
        for a in range(n):
            rows = srcs[a].shape[0] // 2
            for j in range(3):
                over_d2d(a, j, 1 - c, 0, rows).wait_recv()
                over_d2d(a, j, c, 0, rows).wait_send()
                over_ici(a, j).wait_send()

    return pl.pallas_call(
        body, name="gather_shards",
        in_specs=[HBM_SPEC] * n, out_specs=[HBM_SPEC] * n,
        out_shape=[jax.ShapeDtypeStruct((N_CHIPS,) + p.shape, p.dtype) for p in packs],
        scratch_shapes=[pltpu.SemaphoreType.DMA((n, 6)), pltpu.SemaphoreType.DMA((n, 6))],
    )(*packs)


def pair_split(name, packs):
    n = len(packs)

    def body(*refs):
        srcs, outs = refs[:n], refs[n:2 * n]
        send_sems, recv_sems = refs[2 * n:]
        x, y, c = _position()

        def remote(a, off, size):
            rows = srcs[a].shape[1] // 2
            return pltpu.make_async_remote_copy(
                src_ref=srcs[a].at[:, pl.ds((1 - c) * rows + off, size), :],
                dst_ref=outs[a].at[:, pl.ds(off, size), :],
                send_sem=send_sems.at[a], recv_sem=recv_sems.at[a], device_id=(x, y, 1 - c), device_id_type=MESH)

        for a in range(n):
            _start_chunked(functools.partial(remote, a), srcs[a].shape[1] // 2)
        for a in range(n):
            remote(a, 0, srcs[a].shape[1] // 2).wait()

    return pl.pallas_call(
        body, name=name, in_specs=[HBM_SPEC] * n, out_specs=[HBM_SPEC] * n,
        out_shape=[jax.ShapeDtypeStruct((p.shape[0], p.shape[1] // 2, p.shape[2]), p.dtype) for p in packs],
        scratch_shapes=[pltpu.SemaphoreType.DMA((n,)), pltpu.SemaphoreType.DMA((n,))],
    )(*packs)


def pair_join(name, halves):
    n = len(halves)

    def body(*refs):
        srcs, outs = refs[:n], refs[n:2 * n]
        send_sems, recv_sems = refs[2 * n:]
        x, y, c = _position()

        def remote(a, off, size):
            return pltpu.make_async_remote_copy(
                src_ref=srcs[a].at[pl.ds(off, size), :], dst_ref=outs[a].at[pl.ds(off, size), :],
                send_sem=send_sems.at[a], recv_sem=recv_sems.at[a], device_id=(x, y, 1 - c), device_id_type=MESH)

        for a in range(n):
            _start_chunked(functools.partial(remote, a), srcs[a].shape[0])
        for a in range(n):
            remote(a, 0, srcs[a].shape[0]).wait()

    return pl.pallas_call(
        body, name=name, in_specs=[HBM_SPEC] * n, out_specs=[HBM_SPEC] * n,
        out_shape=[jax.ShapeDtypeStruct(p.shape, p.dtype) for p in halves],
        scratch_shapes=[pltpu.SemaphoreType.DMA((n,)), pltpu.SemaphoreType.DMA((n,))],
    )(*halves)


SEM_SPEC = pl.BlockSpec(memory_space=pltpu.SEMAPHORE)
DATAFLOW = pltpu.SideEffectType.DATAFLOW_SIDE_EFFECTING


def _ici_copy(kind, srcs, lands, send_sems, recv_sems, a, j):
    x, y, c = _position()
    px, py = _other_chips(x, y)[j]
    if kind == "gather":
        rows = srcs[a].shape[0] // 2
        src = srcs[a].at[pl.ds(c * rows, rows), :]
        dst = lands[a].at[2 * x + y, pl.ds(c * rows, rows), :]
    else:
        src = srcs[a].at[2 * px + py]
        dst = lands[a].at[j]
    return pltpu.make_async_remote_copy(src_ref=src, dst_ref=dst, send_sem=send_sems, recv_sem=recv_sems,
                                        device_id=(px, py, c), device_id_type=MESH)


def ici_start(name, kind, srcs, land_shapes):
    n = len(srcs)
    lands = [pltpu.with_memory_space_constraint(lax.empty(shp, s.dtype), pltpu.HBM) for shp, s in zip(land_shapes, srcs)]

    def body(*refs):
        src_refs, land_refs = refs[:n], refs[n:2 * n]
        send_sems, recv_sems = refs[2 * n], refs[2 * n + 1]
        token = refs[-1]
        for a in range(n):
            for j in range(N_CHIPS - 1):
                _ici_copy(kind, src_refs, land_refs, send_sems, recv_sems, a, j).start()
        token[...] = jnp.zeros_like(token)

    sems = pltpu.SemaphoreType.DMA(())
    res = pl.pallas_call(
        body, name=name,
        out_shape=[sems, sems] + [pltpu.HBM(s.shape, s.dtype) for s in srcs]
        + [pltpu.HBM(l.shape, l.dtype) for l in lands] + [jax.ShapeDtypeStruct((8, LANES), F32)],
        in_specs=[HBM_SPEC] * (2 * n),
        out_specs=[SEM_SPEC, SEM_SPEC] + [HBM_SPEC] * (2 * n) + [pl.BlockSpec(memory_space=pltpu.VMEM)],
        input_output_aliases={i: 2 + i for i in range(2 * n)},
        compiler_params=pltpu.CompilerParams(has_side_effects=DATAFLOW),
    )(*[pltpu.with_memory_space_constraint(s, pltpu.HBM) for s in srcs], *lands)
    return res[0], res[1], list(res[2:2 + n]), list(res[2 + n:2 + 2 * n]), res[-1]


def ici_wait(name, kind, send_sems, recv_sems, srcs, lands, after):
    n = len(srcs)

    def body(*refs):
        src_refs, land_refs = refs[:n], refs[n:2 * n]
        send, recv = refs[2 * n], refs[2 * n + 1]
        for a in range(n):
            for j in range(N_CHIPS - 1):
                cp = _ici_copy(kind, src_refs, land_refs, send, recv, a, j)
                cp.wait_send()
                cp.wait_recv()

    res = pl.pallas_call(
        body, name=name,
        out_shape=[pltpu.HBM(s.shape, s.dtype) for s in srcs] + [pltpu.HBM(l.shape, l.dtype) for l in lands],
        in_specs=[HBM_SPEC] * (2 * n) + [SEM_SPEC, SEM_SPEC, pl.BlockSpec(memory_space=pl.ANY)],
        out_specs=[HBM_SPEC] * (2 * n),
        input_output_aliases={i: i for i in range(2 * n)},
        compiler_params=pltpu.CompilerParams(has_side_effects=DATAFLOW),
    )(*srcs, *lands, send_sems, recv_sems, after)
    return list(res[:n]), list(res[n:])


def pair_forward(gathered):
    n = len(gathered)

    def body(*refs):
        outs = refs[n:2 * n]
        send_sems, recv_sems = refs[2 * n:]
        x, y, c = _position()
        chips = _other_chips(x, y)

        def part(a, j, cc, off, size):
            px, py = chips[j]
            rows = outs[a].shape[1] // 2
            ref = outs[a].at[2 * px + py, pl.ds(cc * rows + off, size), :]
            return pltpu.make_async_remote_copy(
                src_ref=ref, dst_ref=ref, send_sem=send_sems.at[a, j], recv_sem=recv_sems.at[a, j],
                device_id=(x, y, 1 - c), device_id_type=MESH)

        for a in range(n):
            for j in range(N_CHIPS - 1):
                _start_chunked(functools.partial(part, a, j, c), outs[a].shape[1] // 2)
        for a in range(n):
            rows = outs[a].shape[1] // 2
            for j in range(N_CHIPS - 1):
                part(a, j, 1 - c, 0, rows).wait_recv()
                part(a, j, c, 0, rows).wait_send()

    return pl.pallas_call(
        body, name="pair_forward", in_specs=[HBM_SPEC] * n, out_specs=[HBM_SPEC] * n,
        out_shape=[jax.ShapeDtypeStruct(g.shape, g.dtype) for g in gathered],
        input_output_aliases={i: i for i in range(n)},
        scratch_shapes=[pltpu.SemaphoreType.DMA((n, N_CHIPS - 1)), pltpu.SemaphoreType.DMA((n, N_CHIPS - 1))],
    )(*gathered)


def all_sum_small(part):
    n_dev = 8
    rows = part.shape[0]

    def body(src, out, buf, send_sems, recv_sems):
        x, y, c = _position()
        me = 4 * x + 2 * y + c
        buf[me] = src[...]
        copies = []
        for k in range(1, n_dev):
            px, py, pc = x ^ ((k >> 2) & 1), y ^ ((k >> 1) & 1), c ^ (k & 1)
            cp = pltpu.make_async_remote_copy(
                src_ref=src, dst_ref=buf.at[me], send_sem=send_sems.at[k - 1], recv_sem=recv_sems.at[k - 1],
                device_id=(px, py, pc), device_id_type=MESH)
            cp.start()
            copies.append(cp)
        for cp in copies:
            cp.wait()
        acc = buf[0]
        for k in range(1, n_dev):
            acc = acc + buf[k]
        out[...] = acc

    return pl.pallas_call(
        body, name="all_sum_small",
        in_specs=[pl.BlockSpec(memory_space=pltpu.VMEM)], out_specs=pl.BlockSpec(memory_space=pltpu.VMEM),
        out_shape=jax.ShapeDtypeStruct(part.shape, F32),
        scratch_shapes=[pltpu.VMEM((n_dev, rows, part.shape[1]), F32),
                        pltpu.SemaphoreType.DMA((n_dev - 1,)), pltpu.SemaphoreType.DMA((n_dev - 1,))],
    )(part)


def add_pairs(name, src, theirs, core, out_dtype):
    slabs, rows, cols = theirs.shape
    tm = _pick_tile(rows, ROW_TILE)
    nb = rows // tm

    def body(core_ref, a_ref, b_ref, o_ref):
        o_ref[...] = (a_ref[...].astype(F32) + b_ref[...].astype(F32)).astype(out_dtype)

    return pl.pallas_call(
        body, name=name,
        grid_spec=pltpu.PrefetchScalarGridSpec(
            num_scalar_prefetch=1, grid=(slabs, nb),
            in_specs=[pl.BlockSpec((None, tm, cols), lambda s, i, core_ref: (s, core_ref[0] * nb + i, 0)),
                      pl.BlockSpec((None, tm, cols), lambda s, i, core_ref: (s, i, 0))],
            out_specs=pl.BlockSpec((None, tm, cols), lambda s, i, core_ref: (s, i, 0))),
        out_shape=jax.ShapeDtypeStruct(theirs.shape, out_dtype),
        compiler_params=_cparams(("parallel", "parallel")),
    )(core, src, theirs)


def add_four(name, src, theirs, chip):
    _, rows, cols = theirs.shape
    tm = _pick_tile(rows, ROW_TILE)

    def body(chip_ref, a_ref, b_ref, o_ref):
        acc = a_ref[...].astype(F32)
        for j in range(N_CHIPS - 1):
            acc = acc + b_ref[j].astype(F32)
        o_ref[...] = acc

    return pl.pallas_call(
        body, name=name,
        grid_spec=pltpu.PrefetchScalarGridSpec(
            num_scalar_prefetch=1, grid=(rows // tm,),
            in_specs=[pl.BlockSpec((None, tm, cols), lambda i, chip_ref: (chip_ref[0], i, 0)),
                      pl.BlockSpec((N_CHIPS - 1, tm, cols), lambda i, chip_ref: (0, i, 0))],
            out_specs=pl.BlockSpec((tm, cols), lambda i, chip_ref: (i, 0))),
        out_shape=jax.ShapeDtypeStruct((rows, cols), F32),
        compiler_params=_cparams(("parallel",)),
    )(chip, src, theirs)


PACK_COLS = 1024
SMALL_ROW_MULTIPLE = 32
BIG = ["dn_w_in", "dn_w_out", "cv_w_pw1", "cv_w_pw2", "xa_w_q", "xa_w_kv", "xa_w_o", "mlp_w_up", "mlp_w_down"]
SMALL = ["dn_w_conv", "cv_norm", "cv_b_pw1", "cv_w_dw", "cv_b_dw", "cv_ln_g", "cv_ln_b", "cv_b_pw2"]
SHARD_AXIS = {"dn_w_in": 2, "dn_w_conv": 2, "dn_w_out": 1, "cv_norm": 1, "cv_w_pw1": 2, "cv_b_pw1": 1,
              "cv_w_dw": 2, "cv_b_dw": 1, "cv_ln_g": 1, "cv_ln_b": 1, "cv_w_pw2": 1, "cv_b_pw2": 1,
              "xa_w_q": 1, "xa_w_kv": 2, "xa_w_o": 1, "mlp_w_up": 2, "mlp_w_down": 1}
REPLICATED = ["dn_norm", "dn_a_log", "dn_dt_bias", "dn_out_norm", "xa_norm", "xa_mem_norm", "mlp_norm", "final_norm"]


def _pack_rows(size):
    return -(-size // PACK_COLS)


SHARD_SHAPES = {
    "dn_w_in": (1, 1024, 1028), "dn_w_conv": (1, 4, 768), "dn_w_out": (1, 256, 1024), "cv_norm": (1, 256),
    "cv_w_pw1": (1, 1024, 512), "cv_b_pw1": (1, 512), "cv_w_dw": (1, 31, 256), "cv_b_dw": (1, 256),
    "cv_ln_g": (1, 256), "cv_ln_b": (1, 256), "cv_w_pw2": (1, 256, 1024), "cv_b_pw2": (1, 256),
    "xa_w_q": (2, 256, 1024), "xa_w_kv": (2, 1024, 512), "xa_w_o": (2, 256, 1024),
    "mlp_w_up": (2, 1024, 1024), "mlp_w_down": (2, 1024, 1024)}


def _shard_shape(nm):
    return SHARD_SHAPES[nm]


def _pack(tensors, names, dtype, row_multiple):
    pieces = []
    for nm in names:
        t = tensors[nm]
        flat = t.reshape(t.shape[0], -1) if t.ndim > len(_shard_shape(nm)) else t.reshape(1, -1)
        pad = _pack_rows(flat.shape[1]) * PACK_COLS - flat.shape[1]
        pieces.append(jnp.pad(flat.astype(dtype), ((0, 0), (0, pad))))
    cat = jnp.concatenate(pieces, axis=1)
    rows = cat.shape[1] // PACK_COLS
    total = -(-rows // row_multiple) * row_multiple
    cat = jnp.pad(cat, ((0, 0), (0, (total - rows) * PACK_COLS)))
    return cat.reshape(cat.shape[0], total, PACK_COLS)


def _unpack(pack, names):
    lead = pack.shape[:-2]
    flat = pack.reshape(lead + (-1,))
    out, off = {}, 0
    for nm in names:
        shp = _shard_shape(nm)
        size = 1
        for s in shp:
            size *= s
        out[nm] = flat[..., off:off + size].reshape(lead + shp)
        off += _pack_rows(size) * PACK_COLS
    return out


def _to_full(nm, stacked):
    ax = SHARD_AXIS[nm]
    moved = jnp.moveaxis(stacked, 0, ax)
    shp = list(_shard_shape(nm))
    shp[ax] *= N_CHIPS
    return moved.reshape(shp)


def _to_shards(nm, full):
    ax = SHARD_AXIS[nm]
    shp = list(_shard_shape(nm))
    split = full.reshape(shp[:ax] + [N_CHIPS, shp[ax]] + shp[ax + 1:])
    return jnp.moveaxis(split, ax, 0)


def _row(v):
    return v.reshape(1, -1)


class Stacked:
    def __init__(self, arr, split, layer):
        self.arr, self.kw = arr, dict(b_split=split, b_layer=layer)


def _grad_out(split):
    return dict(out_dtype=BF16, out_split=split, out_layer=(0, 1))


def _with_next(res, next_gain):
    return (res[0], res[1]) if next_gain is not None else (res, None)


def mlp_fwd(tag, h, g, w_up, w_down, n=None, next_gain=None):
    if n is None:
        n = rms_fwd(tag + "_norm", h, g)
    act = mm(tag + "_up", n, w_up.arr, out_dtype=BF16, epi=lambda acc: jnp.square(jnp.maximum(acc, 0.0)), **w_up.kw)
    out, n_next = _with_next(mm(tag + "_down", act, w_down.arr, tk=LONG_TILE_K, epi=lambda acc, res: acc + res,
                                epi_tiles=(h,), norm_gain=next_gain, **w_down.kw), next_gain)
    return out, n_next, (n, act)


def mlp_bwd(tag, dh, h, g, w_up, w_down, saved, after=None):
    n, act = saved
    dh, dh16 = dh
    dup = mm(tag + "_d_act", dh16, w_down.arr, tb=True, out_dtype=BF16, after=after,
             epi=lambda acc, t: acc * (2.0 * jnp.sqrt(t.astype(F32))), epi_tiles=(act,), **w_down.kw)
    dw_down = mm(tag + "_dw_down", act, dh16, ta=True, tk=GRAD_TILE_K, **_grad_out("rows"))
    dh_in, dh16_in, dg = mm(tag + "_dn", dup, w_up.arr, tb=True, norm_bwd=(h, g, dh), **w_up.kw)
    dw_up = mm(tag + "_dw_up", n, dup, ta=True, tk=GRAD_TILE_K, **_grad_out("cols"))
    return (dh_in, dh16_in), dg, (dw_up, dw_down)


def xa_fwd(tag, h, mem, g, g_mem, w_q, w_kv, w_o, n=None, next_gain=None):
    if n is None:
        n = rms_fwd(tag + "_norm", h, g)
    mem_n = rms_fwd(tag + "_mem_norm", mem, g_mem)
    q = mm(tag + "_q", n, w_q.arr, out_dtype=BF16, **w_q.kw)
    kv = mm(tag + "_kv", mem_n, w_kv.arr, out_dtype=BF16, **w_kv.kw)
    o = xa_core_fwd(tag + "_core", q, kv)
    out, n_next = _with_next(mm(tag + "_o", o, w_o.arr, epi=lambda acc, res: acc + res, epi_tiles=(h,),
                                norm_gain=next_gain, **w_o.kw), next_gain)
    return out, n_next, (n, mem_n, q, kv, o)


def xa_bwd(tag, dh, h, mem, g, g_mem, w_q, w_kv, w_o, saved):
    n, mem_n, q, kv, o = saved
    dh, dh16 = dh
    d_o = mm(tag + "_d_o", dh16, w_o.arr, tb=True, out_dtype=BF16, **w_o.kw)
    dw_o = mm(tag + "_dw_o", o, dh16, ta=True, tk=GRAD_TILE_K, **_grad_out("rows"))
    dq, dkv = xa_core_bwd(tag + "_core_bwd", d_o, q, kv)
    dh_in, dh16_in, dg = mm(tag + "_dn", dq, w_q.arr, tb=True, norm_bwd=(h, g, dh), **w_q.kw)
    dw_q = mm(tag + "_dw_q", n, dq, ta=True, tk=GRAD_TILE_K, **_grad_out("rows"))
    dw_kv = mm(tag + "_dw_kv", mem_n, dkv, ta=True, **_grad_out("cols"))
    dmem_n = mm(tag + "_dmem", dkv, w_kv.arr, tb=True, **w_kv.kw)
    dg_mem = mem_norm_bwd(tag + "_mem_norm_bwd", dmem_n, mem, g_mem)
    return (dh_in, dh16_in), dg, dg_mem, (dw_q, dw_kv, dw_o)


def _gate_tile(a_log, dt_bias):
    t = jnp.zeros((8, LANES), F32)
    t = t.at[0, DN_HEADS:2 * DN_HEADS].set(a_log.reshape(-1))
    return t.at[1, DN_HEADS:2 * DN_HEADS].set(dt_bias.reshape(-1))


def dn_fwd(h, g, w_qkv, w_z, w_ba, w_conv, gate, out_norm, w_out, next_gain=None):
    n = rms_fwd("dn_norm", h, g)
    qkv_raw = mm("dn_proj_qkv", n, w_qkv)
    z = mm("dn_proj_z", n, w_z)
    ba = mm("dn_proj_ba", n, w_ba)
    qkv, hs = dn_pre(qkv_raw, ba, w_conv, gate)
    u, w, t_inv = dn_solve(qkv, hs)
    o, states = dn_scan_fwd(qkv, u, w, hs)
    og = dn_post(o, z, out_norm)
    out, n_next = _with_next(mm("dn_out", og, w_out.arr, epi=lambda acc, res: acc + res, epi_tiles=(h,),
                                norm_gain=next_gain, **w_out.kw), next_gain)
    return out, n_next, (n, qkv_raw, z, ba, qkv, hs, u, w, t_inv, o, states, og)


def dn_bwd(dh, h, g, w_qkv, w_z, w_ba, w_conv, gate, out_norm, w_out, saved, after=None):
    n, qkv_raw, z, ba, qkv, hs, u, w, t_inv, o, states, og = saved
    dh, dh16 = dh
    d_og = mm("dn_d_og", dh16, w_out.arr, tb=True, out_dtype=BF16, after=after, **w_out.kw)
    dw_out = mm("dn_dw_out", og, dh16, ta=True, tk=GRAD_TILE_K, **_grad_out("rows"))
    d_o, dz, d_out_norm = dn_post_bwd(d_og, o, z, out_norm)
    dq, dk, dv, dhs = dn_scan_bwd(qkv, u, w, t_inv, hs, states, d_o)
    dc, dba, d_gate = dn_pre_bwd(dq, dk, dv, dhs, qkv_raw, ba, w_conv, gate)
    dqkv_raw, dw_conv = dn_conv_bwd(dc, qkv_raw, w_conv)
    dn = mm("dn_dn_qkv", dqkv_raw, w_qkv, tb=True, tk=w_qkv.shape[1])
    dn = mm("dn_dn_z", dz, w_z, tb=True, epi=lambda acc, t: acc + t, epi_tiles=(dn,))
    dh_in, _, dg = mm("dn_dn_ba", dba, w_ba, tb=True, epi=lambda acc, t: acc + t, epi_tiles=(dn,),
                      norm_bwd=(h, g, dh))
    dw_qkv = mm("dn_dw_qkv", n, dqkv_raw, ta=True, tk=GRAD_TILE_K)
    dw_z = mm("dn_dw_z", n, dz, ta=True, tk=GRAD_TILE_K)
    dw_ba = mm("dn_dw_ba", n, dba, ta=True, tk=GRAD_TILE_K)
    return dh_in, dg, dw_qkv, dw_z, dw_ba, dw_conv, d_gate, d_out_norm, dw_out


def cv_fwd(h, g, w_pw1, b_pw1, w_dw, b_dw, ln_g, ln_b, w_pw2, b_pw2, n=None, next_gain=None):
    if n is None:
        n = rms_fwd("cv_norm", h, g)
    u = mm("cv_pw1", n, w_pw1.arr, epi=lambda acc, b: acc + b, epi_rows=(b_pw1,), **w_pw1.kw)
    s, c = cv_core_fwd(u, w_dw, b_dw, ln_g, ln_b)
    out, n_next = _with_next(mm("cv_pw2", s, w_pw2.arr, epi=lambda acc, res, b: acc + res + b, epi_tiles=(h,),
                                epi_rows=(b_pw2,), norm_gain=next_gain, **w_pw2.kw), next_gain)
    return out, n_next, (n, u, s, c)


def cv_bwd(dh, h, g, w_pw1, w_dw, ln_g, ln_b, w_pw2, saved):
    n, u, s, c = saved
    dh, dh16 = dh
    ds = mm("cv_d_s", dh16, w_pw2.arr, tb=True, out_dtype=BF16, **w_pw2.kw)
    dw_pw2 = mm("cv_dw_pw2", s, dh16, ta=True, tk=GRAD_TILE_K, **_grad_out("rows"))
    db_pw2 = col_sum("cv_db_pw2", dh)
    dc, ln_acc = cv_ln_bwd(ds, c, ln_g, ln_b)
    du, dw_dw, db_pw1 = cv_conv_bwd(dc, u, w_dw)
    dh_in, dh16_in, dg = mm("cv_dn", du, w_pw1.arr, tb=True, norm_bwd=(h, g, dh), **w_pw1.kw)
    dw_pw1 = mm("cv_dw_pw1", n, du, ta=True, tk=GRAD_TILE_K, **_grad_out("cols"))
    return (dh_in, dh16_in), dg, dw_pw1, db_pw1, dw_dw, ln_acc, dw_pw2, db_pw2


WEIGHTS = ["dn_norm", "dn_w_in", "dn_w_conv", "dn_a_log", "dn_dt_bias", "dn_out_norm", "dn_w_out", "cv_norm",
           "cv_w_pw1", "cv_b_pw1", "cv_w_dw", "cv_b_dw", "cv_ln_g", "cv_ln_b", "cv_w_pw2", "cv_b_pw2", "xa_norm",
           "xa_mem_norm", "xa_w_q", "xa_w_kv", "xa_w_o", "mlp_norm", "mlp_w_up", "mlp_w_down", "final_norm"]


def _as_2d(t):
    if t.ndim == 1:
        return t.reshape(1, -1)
    return t.reshape(-1, t.shape[-1])


def kernel(x, mem, dn_norm, dn_w_in, dn_w_conv, dn_a_log, dn_dt_bias, dn_out_norm, dn_w_out, cv_norm, cv_w_pw1, cv_b_pw1, cv_w_dw, cv_b_dw, cv_ln_g, cv_ln_b, cv_w_pw2, cv_b_pw2, xa_norm, xa_mem_norm, xa_w_q, xa_w_kv, xa_w_o, mlp_norm, mlp_w_up, mlp_w_down, final_norm, loss_target, m_dn_norm, m_dn_w_in, m_dn_w_conv, m_dn_a_log, m_dn_dt_bias, m_dn_out_norm, m_dn_w_out, m_cv_norm, m_cv_w_pw1, m_cv_b_pw1, m_cv_w_dw, m_cv_b_dw, m_cv_ln_g, m_cv_ln_b, m_cv_w_pw2, m_cv_b_pw2, m_xa_norm, m_xa_mem_norm, m_xa_w_q, m_xa_w_kv, m_xa_w_o, m_mlp_norm, m_mlp_w_up, m_mlp_w_down, m_final_norm, v_dn_norm, v_dn_w_in, v_dn_w_conv, v_dn_a_log, v_dn_dt_bias, v_dn_out_norm, v_dn_w_out, v_cv_norm, v_cv_w_pw1, v_cv_b_pw1, v_cv_w_dw, v_cv_b_dw, v_cv_ln_g, v_cv_ln_b, v_cv_w_pw2, v_cv_b_pw2, v_xa_norm, v_xa_mem_norm, v_xa_w_q, v_xa_w_kv, v_xa_w_o, v_mlp_norm, v_mlp_w_up, v_mlp_w_down, v_final_norm):
    args = dict(locals())
    wts = {nm: args[nm] for nm in WEIGHTS}
    mom = {nm: args["m_" + nm] for nm in WEIGHTS}
    var = {nm: args["v_" + nm] for nm in WEIGHTS}
    core = lax.axis_index("c").astype(jnp.int32).reshape(1)
    chip = (2 * lax.axis_index("x") + lax.axis_index("y")).astype(jnp.int32)
    def own_slab(got, src):
        return lax.dynamic_update_slice(got, src[None], (chip, 0, 0))

    shard2d = {nm: wts[nm].astype(BF16).reshape(-1, wts[nm].shape[-1]) for nm in BIG}
    first = ["dn_w_in", "dn_w_out"]
    later = [nm for nm in BIG if nm not in first]
    sources = [shard2d[nm] for nm in first] + [_pack(wts, SMALL, F32, SMALL_ROW_MULTIPLE)[0]]
    gathered = [own_slab(got, src) for got, src in zip(gather_shards(sources), sources)]
    stacked = {"dn_w_out": gathered[1].reshape((N_CHIPS,) + SHARD_SHAPES["dn_w_out"])}
    full = {nm: _to_full(nm, t) for nm, t in _unpack(gathered[2], SMALL).items()}
    full["dn_w_in"] = _to_full("dn_w_in", gathered[0].reshape((N_CHIPS,) + SHARD_SHAPES["dn_w_in"]))
    full.update({nm: wts[nm] for nm in REPLICATED})
    later_src = [shard2d[nm] for nm in later]
    g_send, g_recv, later_src, g_lands, started = ici_start(
        "gather_start", "gather", later_src, [(N_CHIPS,) + s.shape for s in later_src])
    full["dn_norm"] = full["dn_norm"] + started[0, 0]

    def rest_weights(after):
        srcs, lands = ici_wait("gather_wait", "gather", g_send, g_recv, later_src, g_lands, after)
        return {nm: own_slab(land, src).reshape((N_CHIPS,) + SHARD_SHAPES[nm])
                for nm, land, src in zip(later, pair_forward(lands), srcs)}

    pending = []

    def on_grads(items):
        tag = "_".join(sorted({str(layer) for _, layer, _ in items}))
        parts = [g.reshape(N_CHIPS, -1, g.shape[-1]) for _, _, g in items]
        theirs = pair_split("pair_split_" + tag, parts)
        pairs = [add_pairs("pair_add_%s%d" % (nm, layer), p, t, core, BF16)
                 for (nm, layer, _), p, t in zip(items, parts, theirs)]
        send, recv, pairs, lands, token = ici_start(
            "scatter_start_" + tag, "scatter", pairs, [(N_CHIPS - 1,) + p.shape[1:] for p in pairs])
        pending.append((tag, items, send, recv, pairs, lands))
        return token

    dh, grads, rep = local_step(x[0], mem[0], loss_target[0], stacked, full, rest_weights, on_grads)

    halves = {}
    last = [("dn_w_in", 0, _to_shards("dn_w_in", grads["dn_w_in"]).astype(BF16)), ("dn_w_out", 0, grads["dn_w_out"][0]),
            ("small", 0, _pack({nm: _to_shards(nm, grads[nm]) for nm in SMALL}, SMALL, F32, SMALL_ROW_MULTIPLE))]
    parts = [g.reshape(N_CHIPS, -1, g.shape[-1]) for _, _, g in last]
    theirs = pair_split("pair_split_last", parts)
    pairs = [add_pairs("pair_add_" + nm, p, t, core, p.dtype) for (nm, _, _), p, t in zip(last, parts, theirs)]
    l_send, l_recv, l_pairs, l_lands, l_started = ici_start(
        "scatter_start_last", "scatter", pairs, [(N_CHIPS - 1,) + p.shape[1:] for p in pairs])
    for tag, items, send, recv, pairs, lands in pending:
        pairs, lands = ici_wait("scatter_wait_" + tag, "scatter", send, recv, pairs, lands, l_started)
        for (nm, layer, _), p, o in zip(items, pairs, lands):
            halves[nm, layer] = add_four("chip_add_%s%d" % (nm, layer), p, o, chip.reshape(1))
    keys = sorted(halves)
    siblings = dict(zip(keys, pair_join("pair_join_early", [halves[k] for k in keys])))

    delta, new_m, new_v, red = {}, {}, {}, {}

    def big_adamw(nm):
        layers = range(wts[nm].shape[0])
        res = adamw_halves("adamw_" + nm, _as_2d(wts[nm]), [halves[nm, l] for l in layers],
                           [siblings[nm, l] for l in layers], _as_2d(mom[nm]), _as_2d(var[nm]), core)
        red[nm], delta[nm], new_m[nm], new_v[nm] = (r.reshape(wts[nm].shape) for r in res)

    early = [nm for nm in BIG if (nm, 0) in halves]
    for nm in early:
        big_adamw(nm)
    done = jnp.concatenate([new_v[nm].reshape(-1)[:1] for nm in early])
    l_pairs, l_lands = ici_wait("scatter_wait_last", "scatter", l_send, l_recv, l_pairs, l_lands, done)
    for (nm, layer, _), p, o in zip(last, l_pairs, l_lands):
        halves[nm, layer] = add_four("chip_add_" + nm, p, o, chip.reshape(1))
    keys = [(nm, layer) for nm, layer, _ in last]
    siblings.update(zip(keys, pair_join("pair_join_last", [halves[k] for k in keys])))
    south = core[0] == 0
    mine, theirs = halves["small", 0], siblings["small", 0]
    red.update(_unpack(jnp.concatenate([jnp.where(south, mine, theirs), jnp.where(south, theirs, mine)], axis=0),
                       SMALL))

    rep = all_sum_small(rep)
    red["dn_norm"] = rep[0:1]
    red["dn_a_log"] = rep[1:2, DN_HEADS:2 * DN_HEADS]
    red["dn_dt_bias"] = rep[2:3, DN_HEADS:2 * DN_HEADS]
    red["dn_out_norm"] = rep[3:4, :LANES]
    red["xa_norm"], red["xa_mem_norm"], red["mlp_norm"] = rep[4:6], rep[6:8], rep[8:10]
    red["final_norm"] = rep[10]
    loss = rep[11, 0]

    for nm in WEIGHTS:
        shp = wts[nm].shape
        if nm in early:
            continue
        if nm in BIG:
            big_adamw(nm)
            continue
        res = adamw("adamw_" + nm, _as_2d(wts[nm]), _as_2d(red[nm].reshape(shp)), _as_2d(mom[nm]), _as_2d(var[nm]))
        delta[nm], new_m[nm], new_v[nm] = (r.reshape(shp) for r in res)
        red[nm] = red[nm].reshape(shp)

    grad_x = dh[None]
    return (loss, grad_x, *[red[nm] for nm in WEIGHTS], *[delta[nm] for nm in WEIGHTS],
            *[new_m[nm] for nm in WEIGHTS], *[new_v[nm] for nm in WEIGHTS])


def local_step(h0, mem0, target, stacked, full, rest_weights=None, on_grads=None):
    d = h0.shape[1]
    dn_norm, dn_a_log, dn_dt_bias, dn_out_norm = (full[nm] for nm in REPLICATED[:4])
    xa_norm, xa_mem_norm, mlp_norm, final_norm = (full[nm] for nm in REPLICATED[4:])
    inner = DN_HEADS * DN_HEAD_DIM
    w_in = full["dn_w_in"][0]
    w_qkv, w_z = w_in[:, :3 * inner], w_in[:, 3 * inner:4 * inner]
    w_ba = jnp.pad(w_in[:, 4 * inner:], ((0, 0), (0, LANES - 2 * DN_HEADS)))
    w_conv = jnp.pad(full["dn_w_conv"][0], ((0, 8 - DN_CONV), (0, 0)))
    gate = _gate_tile(dn_a_log, dn_dt_bias)
    w_dw = jnp.pad(full["cv_w_dw"][0], ((0, CV_HALO - CV_WIDTH), (0, 0)))

    def sw(nm, layer):
        return Stacked(stacked[nm], "rows" if SHARD_AXIS[nm] == 1 else "cols", layer)

    dn_args = (_row(dn_norm), w_qkv, w_z, w_ba, w_conv, gate, _row(dn_out_norm), sw("dn_w_out", 0))
    h1, n, dn_saved = dn_fwd(h0, *dn_args, next_gain=_row(xa_norm[0]))
    if rest_weights is not None:
        stacked = {**stacked, **rest_weights(h1)}
    xa_args = [(_row(xa_norm[l]), _row(xa_mem_norm[l]), sw("xa_w_q", l), sw("xa_w_kv", l), sw("xa_w_o", l))
               for l in range(2)]
    mlp_args = [(_row(mlp_norm[l]), sw("mlp_w_up", l), sw("mlp_w_down", l)) for l in range(2)]
    cv_args = (_row(full["cv_norm"][0]), sw("cv_w_pw1", 0), full["cv_b_pw1"], w_dw, full["cv_b_dw"],
               full["cv_ln_g"], full["cv_ln_b"], sw("cv_w_pw2", 0), full["cv_b_pw2"])
    h2, n, xa0_saved = xa_fwd("xa0", h1, mem0, *xa_args[0], n=n, next_gain=mlp_args[0][0])
    h3, n, mlp0_saved = mlp_fwd("mlp0", h2, *mlp_args[0], n=n, next_gain=cv_args[0])
    h4, n, cv_saved = cv_fwd(h3, *cv_args, n=n, next_gain=xa_args[1][0])
    h5, n, xa1_saved = xa_fwd("xa1", h4, mem0, *xa_args[1], n=n, next_gain=mlp_args[1][0])
    h6, _, mlp1_saved = mlp_fwd("mlp1", h5, *mlp_args[1], n=n)

    dh32, dh16, loss_tile, d_final = loss_head("loss_head", h6, _row(final_norm), target)
    dh = (dh32, dh16)
    grads = {}
    dg_mlp, dg_xa, dg_xa_mem = [None, None], [None, None], [None, None]
    dw_mlp, dw_xa = [None, None], [None, None]
    mlp_names, xa_names = ("mlp_w_up", "mlp_w_down"), ("xa_w_q", "xa_w_kv", "xa_w_o")

    def announce(items):
        return None if on_grads is None else on_grads(items)

    dh, dg_mlp[1], dw_mlp[1] = mlp_bwd("mlp1", dh, h5, *mlp_args[1], mlp1_saved)
    dh, dg_xa[1], dg_xa_mem[1], dw_xa[1] = xa_bwd("xa1", dh, h4, mem0, *xa_args[1], xa1_saved)
    (dh, grads["cv_norm"], dw_pw1, grads["cv_b_pw1"], dw_dw, ln_acc, dw_pw2,
     grads["cv_b_pw2"]) = cv_bwd(dh, h3, cv_args[0], cv_args[1], w_dw, cv_args[5], cv_args[6], cv_args[7], cv_saved)
    after = announce([(nm, 1, g) for nm, g in zip(mlp_names + xa_names, dw_mlp[1] + dw_xa[1])]
                     + [("cv_w_pw1", 0, dw_pw1), ("cv_w_pw2", 0, dw_pw2)])
    dh, dg_mlp[0], dw_mlp[0] = mlp_bwd("mlp0", dh, h2, *mlp_args[0], mlp0_saved, after=after)
    dh, dg_xa[0], dg_xa_mem[0], dw_xa[0] = xa_bwd("xa0", dh, h1, mem0, *xa_args[0], xa0_saved)
    after = announce([(nm, 0, g) for nm, g in zip(mlp_names + xa_names, dw_mlp[0] + dw_xa[0])])
    dh, dg_dn, dw_qkv, dw_z, dw_ba, dw_conv, d_gate, d_out_norm, dw_out = dn_bwd(dh, h0, *dn_args, dn_saved,
                                                                                 after=after)

    grads["dn_w_in"] = jnp.concatenate([dw_qkv, dw_z, dw_ba[:, :2 * DN_HEADS]], axis=1)[None]
    grads["dn_w_conv"] = dw_conv[None, :DN_CONV]
    grads["dn_w_out"], grads["cv_w_pw1"], grads["cv_w_pw2"] = [dw_out], [dw_pw1], [dw_pw2]
    grads["cv_w_dw"] = dw_dw[None, :CV_WIDTH]
    grads["cv_ln_g"], grads["cv_ln_b"], grads["cv_b_dw"] = ln_acc[0:1], ln_acc[1:2], ln_acc[2:3]
    for i, nm in enumerate(mlp_names):
        grads[nm] = [dw_mlp[0][i], dw_mlp[1][i]]
    for i, nm in enumerate(xa_names):
        grads[nm] = [dw_xa[0][i], dw_xa[1][i]]

    rep = jnp.zeros((16, d), F32)
    rep = rep.at[0].set(dg_dn[0])
    rep = rep.at[1, :LANES].set(d_gate[0])
    rep = rep.at[2, :LANES].set(d_gate[1])
    rep = rep.at[3, :LANES].set(d_out_norm[0])
    rep = rep.at[4].set(dg_xa[0][0]).at[5].set(dg_xa[1][0])
    rep = rep.at[6].set(dg_xa_mem[0][0]).at[7].set(dg_xa_mem[1][0])
    rep = rep.at[8].set(dg_mlp[0][0]).at[9].set(dg_mlp[1][0])
    rep = rep.at[10].set(d_final[0])
    rep = rep.at[11, :LANES].set(loss_tile[0])
    return dh, grads, rep
```

```python
import functools

import jax
import jax.numpy as jnp
from jax import lax
from jax.experimental import pallas as pl
from jax.experimental.pallas import tpu as pltpu

F32 = jnp.float32
BF16 = jnp.bfloat16
HIGHEST = lax.Precision.HIGHEST
MESH = pl.DeviceIdType.MESH

D_MODEL = 1024
DN_HEADS = 8
DN_HEAD_DIM = 128
DN_CONV = 4
DN_CHUNK = 64
CV_WIDTH = 31
XA_HEADS = 4
XA_HEAD_DIM = 256
RMS_EPS = 1e-6
LN_EPS = 1e-5
L2_EPS = 1e-6

ADAM_LR = 0.001
ADAM_B1 = 0.9
ADAM_B2 = 0.999
ADAM_EPS = 1e-08
ADAM_WD = 0.01
ADAM_STEP = 10

LANES = 128
ROW_TILE = 512
CONV_ROW_TILE = 256
MM_TILE = 1024
GRAD_TILE_K = 4096
LONG_TILE_K = 2048
ADAMW_ROW_TILE = 256
DN_ROW_TILE = 256
CHUNK_SHIFT = 6
SOLVE_INTERLEAVE = 8
FWD_HEADS_PER_STEP = 8
BWD_HEADS_PER_STEP = 8
BWD_SCAN_ROWS = 256
DN_HALO = 8
CV_HALO = 32
VMEM_LIMIT = 48 * 1024 * 1024
N_CHIPS = 4
D2D_CHUNK_ROWS = 256


def _cparams(sem):
    return pltpu.CompilerParams(dimension_semantics=sem, vmem_limit_bytes=VMEM_LIMIT)


def _dot(a, b, dims=(((1,), (0,)), ((), ()))):
    return lax.dot_general(a.astype(BF16), b.astype(BF16), dims, preferred_element_type=F32)


def _dot_nt(a, b):
    return _dot(a, b, (((1,), (1,)), ((), ())))


def _dot_tn(a, b):
    return _dot(a, b, (((0,), (0,)), ((), ())))


def _dot_hi(a, b, dims=(((1,), (0,)), ((), ()))):
    return lax.dot_general(a.astype(F32), b.astype(F32), dims, precision=HIGHEST, preferred_element_type=F32)


def _dot_x3(a, b, dims=(((1,), (0,)), ((), ()))):
    a_hi, b_hi = a.astype(BF16), b.astype(BF16)
    a_lo = (a - a_hi.astype(F32)).astype(BF16)
    b_lo = (b - b_hi.astype(F32)).astype(BF16)

    def dot(p, q):
        return lax.dot_general(p, q, dims, preferred_element_type=F32)

    return dot(a_hi, b_hi) + (dot(a_hi, b_lo) + dot(a_lo, b_hi))


def _sigmoid(x):
    return 1.0 / (1.0 + jnp.exp(-x))


def _silu(x):
    return x * _sigmoid(x)


def _silu_grad(x):
    s = _sigmoid(x)
    return s * (1.0 + x * (1.0 - s))


def _softplus(x):
    return jnp.maximum(x, 0.0) + jnp.log(1.0 + jnp.exp(-jnp.abs(x)))


def _iota(shape, dim):
    return lax.broadcasted_iota(jnp.int32, shape, dim)


def _lane_col(vals, lane, idx):
    return jnp.sum(jnp.where(lane == idx, vals, 0.0), axis=1, keepdims=True)


def _pick_tile(rows, cap):
    best = rows
    for t in range(16, min(rows, cap) + 1, 16):
        if rows % t == 0:
            best = t
    return best


def _stacked_spec(shape, split, layer, rows, cols, block_index):
    r_shard, c_shard = shape[-2], shape[-1]
    if split == "rows" and rows > r_shard:
        assert rows % r_shard == 0 and c_shard % cols == 0
        chips = rows // r_shard

        def slabs(i, j, kk):
            bi, bj = block_index(i, j, kk)
            return (bi, layer, 0, bj)

        return pl.BlockSpec((chips, None, r_shard, cols), slabs), chips
    assert r_shard % rows == 0 and c_shard % cols == 0
    per_chip = (r_shard // rows) if split == "rows" else (c_shard // cols)

    def index(i, j, kk):
        bi, bj = block_index(i, j, kk)
        if split == "rows":
            return (bi // per_chip, layer, bi % per_chip, bj)
        return (bj // per_chip, layer, bi, bj % per_chip)

    return pl.BlockSpec((None, None, rows, cols), index), 1


def mm(name, a, b, *, ta=False, tb=False, out_dtype=F32, pro=None, epi=None, epi_tiles=(), epi_rows=(),
       tm=MM_TILE, tn=MM_TILE, tk=MM_TILE, b_split=None, b_layer=None, out_split=None, out_layer=None,
       after=None, norm_gain=None, norm_bwd=None):
    m, k = (a.shape[1], a.shape[0]) if ta else a.shape
    b_rows, b_cols = b.shape[-2], b.shape[-1]
    if b_split == "rows":
        b_rows *= N_CHIPS
    elif b_split == "cols":
        b_cols *= N_CHIPS
    n = b_rows if tb else b_cols
    assert (b_cols if tb else b_rows) == k
    tm, tn, tk = min(tm, m), min(tn, n), min(tk, k)
    if b_split == "cols":
        if tb:
            tk = min(tk, b.shape[-1])
        else:
            tn = min(tn, b.shape[-1])
    if out_split == "cols":
        tn = min(tn, n // N_CHIPS)
    assert m % tm == 0 and n % tn == 0 and k % tk == 0
    nk = k // tk
    a_spec = pl.BlockSpec((tk, tm), lambda i, j, kk: (kk, i)) if ta else pl.BlockSpec((tm, tk), lambda i, j, kk: (i, kk))
    b_block = (tn, tk) if tb else (tk, tn)
    b_index = (lambda i, j, kk: (j, kk)) if tb else (lambda i, j, kk: (kk, j))
    b_chips = o_chips = 1
    if b_split is None:
        b_spec = pl.BlockSpec(b_block, b_index)
    else:
        b_spec, b_chips = _stacked_spec(b.shape, b_split, b_layer, b_block[0], b_block[1], b_index)
    in_specs = [a_spec, b_spec]
    in_specs += [pl.BlockSpec((tm, tn), lambda i, j, kk: (i, j)) for _ in epi_tiles]
    in_specs += [pl.BlockSpec((1, tn), lambda i, j, kk: (0, j)) for _ in epi_rows]
    n_t, n_r = len(epi_tiles), len(epi_rows)
    dims = (((0 if ta else 1,), (1 if tb else 0,)), ((), ()))
    if out_split is None:
        out_shape = jax.ShapeDtypeStruct((m, n), out_dtype)
        out_spec = pl.BlockSpec((tm, tn), lambda i, j, kk: (i, j))
    else:
        shard = (m // N_CHIPS, n) if out_split == "rows" else (m, n // N_CHIPS)
        out_shape = jax.ShapeDtypeStruct((N_CHIPS, out_layer[1]) + shard, out_dtype)
        out_spec, o_chips = _stacked_spec(out_shape.shape, out_split, out_layer[0], tm, tn, lambda i, j, kk: (i, j))
    single_pass = nk == 1 and norm_gain is None and norm_bwd is None and b_chips == 1 and o_chips == 1
    extra = []
    row_spec = pl.BlockSpec((1, n), lambda i, j, kk: (0, 0))
    tile_spec = pl.BlockSpec((tm, tn), lambda i, j, kk: (i, j))
    if norm_gain is not None:
        assert tn == n and out_split is None
        extra.append(norm_gain)
        in_specs.append(row_spec)
        out_shape = [out_shape, jax.ShapeDtypeStruct((m, n), BF16)]
        out_spec = [out_spec, tile_spec]
    if norm_bwd is not None:
        assert tn == n and out_split is None and norm_gain is None
        extra += list(norm_bwd)
        in_specs += [tile_spec, row_spec, tile_spec]
        out_shape = [jax.ShapeDtypeStruct((m, n), F32), jax.ShapeDtypeStruct((m, n), BF16),
                     jax.ShapeDtypeStruct((1, n), F32)]
        out_spec = [tile_spec, tile_spec, row_spec]
    if after is not None:
        extra.append(after)
        in_specs.append(pl.BlockSpec(memory_space=pl.ANY))

    def body(a_ref, b_ref, *rest):
        tiles = rest[:n_t]
        rows = rest[n_t:n_t + n_r]
        gain_ref = rest[n_t + n_r] if norm_gain is not None else None
        bwd_refs = rest[n_t + n_r:n_t + n_r + 3] if norm_bwd is not None else None
        rest = rest[n_t + n_r + len(extra):]
        o_ref, acc_ref = rest[0], rest[-1]
        av = a_ref[...]
        if pro is not None:
            av = pro(av)
        if single_pass:
            out = _dot(av, b_ref[...], dims)
            if epi is not None:
                out = epi(out, *[t[...] for t in tiles], *[r[...] for r in rows])
            o_ref[...] = out.astype(out_dtype)
            return
        kk = pl.program_id(2)

        @pl.when(kk == 0)
        def _():
            acc_ref[...] = jnp.zeros_like(acc_ref)

        bv = b_ref[...]
        if b_chips > 1:
            bv = bv.reshape(b_block)
        acc_ref[...] += _dot(av, bv, dims)

        @pl.when(kk == nk - 1)
        def _():
            out = acc_ref[...]
            if epi is not None:
                out = epi(out, *[t[...] for t in tiles], *[r[...] for r in rows])
            if gain_ref is not None:
                rest[1][...] = (_rms_stats(out)[0] * gain_ref[...]).astype(BF16)
            if bwd_refs is not None:
                h_ref, g_ref, dres_ref = bwd_refs
                dh, dg = _rms_bwd_tile(out, h_ref[...], g_ref[...])
                total = dres_ref[...] + dh
                o_ref[...] = total
                rest[1][...] = total.astype(BF16)
                first = pl.program_id(0) == 0

                @pl.when(first)
                def _():
                    rest[2][...] = dg

                @pl.when(jnp.logical_not(first))
                def _():
                    rest[2][...] += dg

                return
            out = out.astype(out_dtype)
            o_ref[...] = out.reshape(o_chips, tm // o_chips, tn) if o_chips > 1 else out

    outer = "arbitrary" if norm_bwd is not None else "parallel"
    return pl.pallas_call(
        body, name=name, grid=(m // tm, n // tn, nk),
        in_specs=in_specs, out_specs=out_spec, out_shape=out_shape,
        scratch_shapes=[] if single_pass else [pltpu.VMEM((tm, tn), F32)],
        compiler_params=_cparams((outer, outer, "arbitrary")),
    )(a, b, *epi_tiles, *epi_rows, *extra)


def row_call(name, body, n_rows, tm, ins, outs, accs=()):
    tm = _pick_tile(n_rows, tm)
    in_specs = []
    for arr, kind in ins:
        if kind == "tile":
            if arr.ndim == 2:
                in_specs.append(pl.BlockSpec((tm, arr.shape[1]), lambda i: (i, 0)))
            else:
                in_specs.append(pl.BlockSpec((arr.shape[0], tm, arr.shape[2]), lambda i: (0, i, 0)))
        elif kind == "full":
            in_specs.append(pl.BlockSpec(arr.shape, functools.partial(lambda i, nd: (0,) * nd, nd=arr.ndim)))
        else:
            where, h = kind
            per = tm // h
            if where == "prev":
                in_specs.append(pl.BlockSpec((h, arr.shape[1]), functools.partial(
                    lambda i, per: (jnp.maximum(i * per - 1, 0), 0), per=per)))
            else:
                last = n_rows // h - 1
                in_specs.append(pl.BlockSpec((h, arr.shape[1]), functools.partial(
                    lambda i, per, last: (jnp.minimum((i + 1) * per, last), 0), per=per, last=last)))
    out_shape, out_specs = [], []
    for shape, dtype in outs:
        out_shape.append(jax.ShapeDtypeStruct(shape, dtype))
        if len(shape) == 2:
            out_specs.append(pl.BlockSpec((tm, shape[1]), lambda i: (i, 0)))
        else:
            out_specs.append(pl.BlockSpec((shape[0], tm, shape[2]), lambda i: (0, i, 0)))
    for shape in accs:
        out_shape.append(jax.ShapeDtypeStruct(shape, F32))
        out_specs.append(pl.BlockSpec(shape, lambda i: (0, 0)))
    n_in, n_out, n_acc = len(ins), len(outs), len(accs)

    def kern(*refs):
        i = pl.program_id(0)
        in_refs = refs[:n_in]
        out_refs = refs[n_in:n_in + n_out]
        acc_refs = refs[n_in + n_out:n_in + n_out + n_acc]
        if n_acc:
            @pl.when(i == 0)
            def _():
                for r in acc_refs:
                    r[...] = jnp.zeros_like(r)
        body(i, in_refs, out_refs, acc_refs)

    res = pl.pallas_call(
        kern, name=name, grid=(n_rows // tm,), in_specs=in_specs, out_specs=out_specs, out_shape=out_shape,
        compiler_params=_cparams(("arbitrary",) if n_acc else ("parallel",)),
    )(*[a for a, _ in ins])
    return list(res)


def _rms_stats(h):
    r = lax.rsqrt(jnp.mean(h * h, axis=-1, keepdims=True) + RMS_EPS)
    return h * r, r


def rms_fwd(name, h, g):
    def body(i, ins, outs, accs):
        xhat, _ = _rms_stats(ins[0][...])
        outs[0][...] = (xhat * ins[1][...]).astype(BF16)

    return row_call(name, body, h.shape[0], ROW_TILE, [(h, "tile"), (g, "full")], [(h.shape, BF16)])[0]


def _rms_bwd_tile(dn, h, g):
    xhat, r = _rms_stats(h)
    dxhat = dn * g
    dh = r * (dxhat - xhat * jnp.mean(dxhat * xhat, axis=-1, keepdims=True))
    dg = jnp.sum(dn * xhat, axis=0, keepdims=True)
    return dh, dg


def mem_norm_bwd(name, dn, mem, g):
    def body(i, ins, outs, accs):
        _, dg = _rms_bwd_tile(ins[0][...].astype(F32), ins[1][...], ins[2][...])
        accs[0][...] += dg

    return row_call(name, body, mem.shape[0], ROW_TILE, [(dn, "tile"), (mem, "tile"), (g, "full")], [],
                    [(1, mem.shape[1])])[0]


def loss_head(name, h, g, target):
    d = h.shape[1]

    def body(i, ins, outs, accs):
        hv, gv = ins[0][...], ins[1][...]
        xhat, _ = _rms_stats(hv)
        err = xhat * gv - ins[2][...]
        dy = err * (1.0 / d)
        dh, dg = _rms_bwd_tile(dy, hv, gv)
        outs[0][...] = dh
        outs[1][...] = dh.astype(BF16)
        accs[0][...] += jnp.full((8, LANES), 0.5 / d, F32) * jnp.sum(err * err)
        accs[1][...] += dg

    dh, dh16, loss, dg = row_call(name, body, h.shape[0], ROW_TILE, [(h, "tile"), (g, "full"), (target, "tile")],
                                  [(h.shape, F32), (h.shape, BF16)], [(8, LANES), (1, d)])
    return dh, dh16, loss, dg


def col_sum(name, x):
    def body(i, ins, outs, accs):
        accs[0][...] += jnp.sum(ins[0][...].astype(F32), axis=0, keepdims=True)

    return row_call(name, body, x.shape[0], ROW_TILE, [(x, "tile")], [], [(1, x.shape[1])])[0]


def _conv_taps(xcat, w_ref, cols, width, halo, tm):
    rows = halo + tm
    acc = None
    for j in range(width):
        s = width - 1 - j
        xs = xcat if s == 0 else pltpu.roll(xcat, s, 0)
        term = xs[halo:rows] * w_ref[j:j + 1, cols]
        acc = term if acc is None else acc + term
    return acc


def _conv_taps_bwd_x(dcat, w_ref, cols, width, halo, tm):
    rows = halo + tm
    acc = None
    for j in range(width):
        s = width - 1 - j
        ds = dcat if s == 0 else pltpu.roll(dcat, rows - s, 0)
        term = ds[0:tm] * w_ref[j:j + 1, cols]
        acc = term if acc is None else acc + term
    return acc


def _conv_taps_bwd_w(dy, xcat, width, halo, tm, wrows):
    rows = halo + tm
    rid = _iota((wrows, dy.shape[1]), 0)
    out = jnp.zeros((wrows, dy.shape[1]), F32)
    for j in range(width):
        s = width - 1 - j
        xs = xcat if s == 0 else pltpu.roll(xcat, s, 0)
        v = jnp.sum(dy * xs[halo:rows], axis=0, keepdims=True)
        out = out + jnp.where(rid == j, v, 0.0)
    return out


def dn_pre(qkv_raw, ba, w_conv, gate):
    s_len = qkv_raw.shape[0]
    tm = min(DN_ROW_TILE, s_len)
    n_blk = qkv_raw.shape[1] // LANES

    def body(i, ins, outs, accs):
        x_ref, xp_ref, ba_ref, w_ref, gate_ref = ins
        qkv_ref, hs_ref = outs

        def blk(cb, carry):
            cols = pl.ds(pl.multiple_of(cb * LANES, LANES), LANES)
            prev = jnp.where(i > 0, xp_ref[:, cols], 0.0)
            xcat = jnp.concatenate([prev, x_ref[:, cols]], axis=0)
            c = _conv_taps(xcat, w_ref, cols, DN_CONV, DN_HALO, tm)
            y = _silu(c)
            rs = lax.rsqrt(jnp.sum(y * y, axis=-1, keepdims=True) + L2_EPS)
            fac = jnp.where(cb < DN_HEADS, DN_HEAD_DIM ** -0.5, 1.0)
            qkv_ref[:, cols] = jnp.where(cb < 2 * DN_HEADS, y * (rs * fac), y)
            return carry

        lax.fori_loop(0, n_blk, blk, 0, unroll=4)

        bav = ba_ref[...]
        beta = _sigmoid(bav)
        g = -jnp.exp(gate_ref[0:1, :]) * _softplus(bav + gate_ref[1:2, :])
        lane = _iota((tm, LANES), 1)
        g = jnp.where((lane >= DN_HEADS) & (lane < 2 * DN_HEADS), g, 0.0)
        r = _iota((tm, tm), 0)
        c = _iota((tm, tm), 1)
        tri = jnp.where((r >= c) & ((r >> CHUNK_SHIFT) == (c >> CHUNK_SHIFT)), 1.0, 0.0)
        gc = _dot_hi(tri, g)
        for h in range(DN_HEADS):
            hs_ref[h] = jnp.where(lane == 0, _lane_col(beta, lane, h),
                                  jnp.where(lane == 1, _lane_col(g, lane, DN_HEADS + h),
                                            jnp.where(lane == 2, _lane_col(gc, lane, DN_HEADS + h), 0.0)))

    return row_call("dn_pre", body, s_len, tm,
                    [(qkv_raw, "tile"), (qkv_raw, ("prev", DN_HALO)), (ba, "tile"), (w_conv, "full"), (gate, "full")],
                    [(qkv_raw.shape, F32), ((DN_HEADS, s_len, LANES), F32)])


def _chunk_masks():
    r = _iota((DN_CHUNK, DN_CHUNK), 0)
    c = _iota((DN_CHUNK, DN_CHUNK), 1)
    return r, c


def _decay_matrix(gc, r, c):
    gc_row = jnp.sum(jnp.where(r == c, gc, 0.0), axis=0, keepdims=True)
    causal = r >= c
    return jnp.where(causal, jnp.exp(jnp.where(causal, gc - gc_row, 0.0)), 0.0)


def _tri_inverse(lows, r, c):
    eye = jnp.where(r == c, 1.0, 0.0)
    ts = [eye for _ in lows]
    b = 1
    while b < DN_CHUNK:
        shift = b.bit_length()
        sel = ((r >> shift) == (c >> shift)) & ((r & b) != 0) & ((c & b) == 0)
        lms = [jnp.where(sel, low, 0.0) for low in lows]
        if b == 1:
            ts = [t - lm for t, lm in zip(ts, lms)]
        else:
            t_lm = [_dot_x3(t, lm) for t, lm in zip(ts, lms)]
            t_lm_t = [_dot_x3(x, t) for x, t in zip(t_lm, ts)]
            ts = [t - x for t, x in zip(ts, t_lm_t)]
        b *= 2
    return ts


def dn_solve(qkv, hs):
    s_len = qkv.shape[0]
    rb = min(ROW_TILE, s_len)
    n_chunk = rb // DN_CHUNK
    interleave = min(SOLVE_INTERLEAVE, n_chunk)

    def body(k_ref, v_ref, hs_ref, u_ref, w_ref, t_ref):
        r, c = _chunk_masks()

        def group(gi, carry):
            rows = [pl.ds(pl.multiple_of((gi * interleave + j) * DN_CHUNK, DN_CHUNK), DN_CHUNK)
                    for j in range(interleave)]
            k = [k_ref[rw, :] for rw in rows]
            beta = [hs_ref[rw, 0:1] for rw in rows]
            gc = [hs_ref[rw, 2:3] for rw in rows]
            kb = [a * b for a, b in zip(k, beta)]
            decay = [_decay_matrix(g, r, c) for g in gc]
            lows = [jnp.where(r > c, _dot_nt(a, b) * d, 0.0) for a, b, d in zip(kb, k, decay)]
            ts = _tri_inverse(lows, r, c)
            us = [_dot_x3(t, v_ref[rw, :] * b) for t, rw, b in zip(ts, rows, beta)]
            ws = [_dot_x3(t, a * jnp.exp(g)) for t, a, g in zip(ts, kb, gc)]
            for j, rw in enumerate(rows):
                u_ref[rw, :] = us[j]
                w_ref[rw, :] = ws[j].astype(BF16)
                t_ref[rw, :] = ts[j]
            return carry

        lax.fori_loop(0, n_chunk // interleave, group, 0)

    return pl.pallas_call(
        body, name="dn_solve", grid=(DN_HEADS, s_len // rb),
        in_specs=[pl.BlockSpec((rb, LANES), lambda h, i: (i, DN_HEADS + h)),
                  pl.BlockSpec((rb, LANES), lambda h, i: (i, 2 * DN_HEADS + h)),
                  pl.BlockSpec((None, rb, LANES), lambda h, i: (h, i, 0))],
        out_specs=[pl.BlockSpec((rb, LANES), lambda h, i: (i, h)),
                   pl.BlockSpec((rb, LANES), lambda h, i: (i, h)),
                   pl.BlockSpec((None, rb, DN_CHUNK), lambda h, i: (h, i, 0))],
        out_shape=[jax.ShapeDtypeStruct((s_len, DN_HEADS * LANES), F32),
                   jax.ShapeDtypeStruct((s_len, DN_HEADS * LANES), BF16),
                   jax.ShapeDtypeStruct((DN_HEADS, s_len, DN_CHUNK), F32)],
        compiler_params=_cparams(("parallel", "parallel")),
    )(qkv, qkv, hs)


def dn_scan_fwd(qkv, u, w, hs):
    s_len = qkv.shape[0]
    rb = min(ROW_TILE, s_len)
    n_chunk = rb // DN_CHUNK
    total_chunks = s_len // DN_CHUNK

    hps = FWD_HEADS_PER_STEP
    groups = DN_HEADS // hps

    def body(q_ref, k_ref, u_ref, w_ref, hs_ref, o_ref, st_ref, state):
        @pl.when(pl.program_id(1) == 0)
        def _():
            state[...] = jnp.zeros_like(state)

        r, c = _chunk_masks()

        def chunk(n, carry):
            rows = pl.ds(pl.multiple_of(n * DN_CHUNK, DN_CHUNK), DN_CHUNK)
            heads = range(hps)
            cols = [slice(h * LANES, (h + 1) * LANES) for h in heads]
            each = lambda f, *xs: [f(*a) for a in zip(*xs)]
            q = [q_ref[rows, cl] for cl in cols]
            k = [k_ref[rows, cl] for cl in cols]
            gc = [hs_ref[h, rows, 2:3] for h in heads]
            st = [state[h] for h in heads]
            for h in heads:
                st_ref[h, n] = st[h]
            gl = each(lambda g: jnp.min(g, axis=0, keepdims=True), gc)
            decay = each(lambda g: _decay_matrix(g, r, c), gc)
            w_st = [_dot(w_ref[rows, cols[h]], st[h]) for h in heads]
            qk = each(_dot_nt, q, k)
            q_st = each(lambda a, g, s: _dot(a * jnp.exp(g), s), q, gc, st)
            vn = [u_ref[rows, cols[h]] - w_st[h] for h in heads]
            ai_vn = each(lambda a, d, b: _dot(a * d, b), qk, decay, vn)
            kd_vn = each(lambda a, g0, g, b: _dot_tn(a * jnp.exp(g0 - g), b), k, gl, gc, vn)
            for h in heads:
                o_ref[rows, cols[h]] = q_st[h] + ai_vn[h]
                state[h] = st[h] * jnp.exp(gl[h]) + kd_vn[h]
            return carry

        lax.fori_loop(0, n_chunk, chunk, 0)

    wide = hps * LANES
    blk = lambda off: pl.BlockSpec((rb, wide), lambda h, i: (i, off + h))
    return pl.pallas_call(
        body, name="dn_scan_fwd", grid=(groups, s_len // rb),
        in_specs=[blk(0), blk(groups), blk(0), blk(0),
                  pl.BlockSpec((hps, rb, LANES), lambda h, i: (h, i, 0))],
        out_specs=[blk(0),
                   pl.BlockSpec((hps, n_chunk, LANES, LANES), lambda h, i: (h, i, 0, 0))],
        out_shape=[jax.ShapeDtypeStruct((s_len, DN_HEADS * LANES), F32),
                   jax.ShapeDtypeStruct((DN_HEADS, total_chunks, LANES, LANES), F32)],
        scratch_shapes=[pltpu.VMEM((hps, LANES, LANES), F32)],
        compiler_params=_cparams(("parallel", "arbitrary")),
    )(qkv, qkv, u, w, hs)


def dn_scan_bwd(qkv, u, w, t_inv, hs, states, d_o):
    s_len = qkv.shape[0]
    rb = min(BWD_SCAN_ROWS, s_len)
    n_chunk = rb // DN_CHUNK
    n_blk = s_len // rb
    hps = BWD_HEADS_PER_STEP
    groups = DN_HEADS // hps

    def body(q_ref, k_ref, v_ref, u_ref, w_ref, t_ref, hs_ref, st_ref, do_ref,
             dq_ref, dk_ref, dv_ref, dhs_ref, dstate):
        @pl.when(pl.program_id(1) == 0)
        def _():
            dstate[...] = jnp.zeros_like(dstate)

        r, c = _chunk_masks()
        causal = r >= c
        strict = r > c
        lane = _iota((DN_CHUNK, LANES), 1)
        upper = jnp.where(r <= c, 1.0, 0.0)
        last_row = _iota((DN_CHUNK, 1), 0) == DN_CHUNK - 1

        def chunk(m, carry):
            n = n_chunk - 1 - m
            rows = pl.ds(pl.multiple_of(n * DN_CHUNK, DN_CHUNK), DN_CHUNK)
            heads = range(hps)
            cols = [slice(h * LANES, (h + 1) * LANES) for h in heads]
            each = lambda f, *xs: [f(*a) for a in zip(*xs)]
            rsum = lambda x: jnp.sum(x, axis=-1, keepdims=True)
            dims_tn = (((0,), (0,)), ((), ()))
            q = [q_ref[rows, cl] for cl in cols]
            k = [k_ref[rows, cl] for cl in cols]
            v = [v_ref[rows, cl] for cl in cols]
            uu = [u_ref[rows, cl] for cl in cols]
            ww = [w_ref[rows, cl] for cl in cols]
            do = [do_ref[rows, cl] for cl in cols]
            tt = [t_ref[h, rows, :] for h in heads]
            beta = [hs_ref[h, rows, 0:1] for h in heads]
            gc = [hs_ref[h, rows, 2:3] for h in heads]
            st = [st_ref[h, n] for h in heads]
            dst = [dstate[h] for h in heads]
            gl = each(lambda g: jnp.min(g, axis=0, keepdims=True), gc)
            egc = each(jnp.exp, gc)
            egl = each(jnp.exp, gl)
            ekd = each(lambda a, b: jnp.exp(a - b), gl, gc)
            decay = each(lambda g: _decay_matrix(g, r, c), gc)
            qd = each(jnp.multiply, q, egc)
            kd = each(jnp.multiply, k, ekd)
            kb = each(jnp.multiply, k, beta)
            w_st = each(_dot, ww, st)
            qk = each(_dot_nt, q, k)
            dqd = each(_dot_nt, do, st)
            kd_dst = each(_dot, kd, dst)
            qd_do = each(_dot_tn, qd, do)
            kbk = each(_dot_nt, kb, k)
            vn = each(jnp.subtract, uu, w_st)
            ai = each(jnp.multiply, qk, decay)
            low = each(lambda a, d: jnp.where(strict, a * d, 0.0), kbk, decay)
            dai = each(lambda a, b: jnp.where(causal, _dot_nt(a, b), 0.0), do, vn)
            ai_do = each(_dot_tn, ai, do)
            dkd = each(_dot_nt, vn, dst)
            dvn = each(jnp.add, ai_do, kd_dst)
            dp = each(jnp.multiply, dai, decay)
            dw = each(lambda a, b: -_dot_nt(a, b), dvn, st)
            w_dvn = each(_dot_tn, ww, dvn)
            dp_k = each(_dot, dp, k)
            dp_q = each(_dot_tn, dp, q)
            drhs_u = each(lambda a, b: _dot_x3(a, b, dims_tn), tt, dvn)
            dgl = each(lambda a, b, e: jnp.sum(a * b) * e, dst, st, egl)
            for h in heads:
                dstate[h] = dst[h] * egl[h] + qd_do[h] - w_dvn[h]
            dq = each(lambda a, e, b: a * e + b, dqd, egc, dp_k)
            dk_a = each(lambda a, e, b: a * e + b, dkd, ekd, dp_q)
            rkd = each(lambda a, b: rsum(a * b), dkd, kd)
            drhs_w = each(lambda a, b: _dot_x3(a, b, dims_tn), tt, dw)
            dl_u = each(_dot_nt, drhs_u, uu)
            dl_w = each(_dot_nt, drhs_w, ww)
            dlow = each(lambda a, b: jnp.where(strict, -(a + b), 0.0), dl_u, dl_w)
            dqm = each(jnp.multiply, dlow, decay)
            m_tot = each(lambda a, b, d, e: a * b + d * e, dai, ai, dlow, low)
            dqm_k = each(_dot, dqm, k)
            dk_l = each(_dot_tn, dqm, kb)
            col_rows = each(lambda m: jnp.sum(m, axis=0, keepdims=True), m_tot)
            col_sums = each(lambda rw: jnp.sum(jnp.where(r == c, rw, 0.0), axis=1, keepdims=True), col_rows)
            dkb_w = each(jnp.multiply, drhs_w, egc)
            dkb = each(jnp.add, dkb_w, dqm_k)
            dgc = [rsum(dqd[h] * qd[h]) - rkd[h] + jnp.where(last_row, jnp.sum(rkd[h]) + dgl[h], 0.0)
                   + rsum(m_tot[h]) + rsum(dkb_w[h] * kb[h]) for h in heads]
            dg = each(lambda a, b: _dot_hi(upper, jnp.where(lane == 1, a - b, 0.0)), dgc, col_sums)
            for h in heads:
                dq_ref[rows, cols[h]] = dq[h]
                dk_ref[rows, cols[h]] = dk_a[h] + dk_l[h] + dkb[h] * beta[h]
                dv_ref[rows, cols[h]] = drhs_u[h] * beta[h]
                dbeta = rsum(drhs_u[h] * v[h]) + rsum(dkb[h] * k[h])
                dhs_ref[h, rows, :] = jnp.where(lane == 0, dbeta, dg[h])
            return carry

        lax.fori_loop(0, n_chunk, chunk, 0)

    wide = hps * LANES
    blk = lambda off: pl.BlockSpec((rb, wide), lambda h, i: (n_blk - 1 - i, off + h))
    head = blk(0)
    hs_spec = pl.BlockSpec((hps, rb, LANES), lambda h, i: (h, n_blk - 1 - i, 0))
    full = jax.ShapeDtypeStruct((s_len, DN_HEADS * LANES), F32)
    return pl.pallas_call(
        body, name="dn_scan_bwd", grid=(groups, n_blk),
        in_specs=[blk(0), blk(groups), blk(2 * groups), head, head,
                  pl.BlockSpec((hps, rb, DN_CHUNK), lambda h, i: (h, n_blk - 1 - i, 0)), hs_spec,
                  pl.BlockSpec((hps, n_chunk, LANES, LANES), lambda h, i: (h, n_blk - 1 - i, 0, 0)), head],
        out_specs=[head, head, head, hs_spec],
        out_shape=[full, full, full, jax.ShapeDtypeStruct((DN_HEADS, s_len, LANES), F32)],
        scratch_shapes=[pltpu.VMEM((hps, LANES, LANES), F32)],
        compiler_params=_cparams(("parallel", "arbitrary")),
    )(qkv, qkv, qkv, u, w, t_inv, hs, states, d_o)


def dn_post(o, z, out_norm):
    def body(i, ins, outs, accs):
        gn = ins[2][...]
        for h in range(DN_HEADS):
            cols = slice(h * LANES, (h + 1) * LANES)
            xhat, _ = _rms_stats(ins[0][:, cols])
            outs[0][:, cols] = (xhat * gn * _silu(ins[1][:, cols])).astype(BF16)

    return row_call("dn_post", body, o.shape[0], ROW_TILE, [(o, "tile"), (z, "tile"), (out_norm, "full")],
                    [(o.shape, BF16)])[0]


def dn_post_bwd(d_og, o, z, out_norm):
    def body(i, ins, outs, accs):
        gn = ins[3][...]
        dgn = jnp.zeros((1, LANES), F32)
        for h in range(DN_HEADS):
            cols = slice(h * LANES, (h + 1) * LANES)
            dy, zh = ins[0][:, cols].astype(F32), ins[2][:, cols]
            xhat, r = _rms_stats(ins[1][:, cols])
            sz = _silu(zh)
            dgn = dgn + jnp.sum(dy * xhat * sz, axis=0, keepdims=True)
            outs[1][:, cols] = (dy * xhat * gn * _silu_grad(zh)).astype(BF16)
            dxhat = dy * gn * sz
            outs[0][:, cols] = r * (dxhat - xhat * jnp.mean(dxhat * xhat, axis=-1, keepdims=True))
        accs[0][...] += dgn

    return row_call("dn_post_bwd", body, o.shape[0], ROW_TILE,
                    [(d_og, "tile"), (o, "tile"), (z, "tile"), (out_norm, "full")],
                    [(o.shape, F32), (o.shape, BF16)], [(1, LANES)])


def dn_pre_bwd(dq, dk, dv, dhs, qkv_raw, ba, w_conv, gate):
    s_len = qkv_raw.shape[0]
    tm = min(DN_ROW_TILE, s_len)

    def body(i, ins, outs, accs):
        dq_ref, dk_ref, dv_ref, dhs_ref, x_ref, xp_ref, ba_ref, w_ref, gate_ref = ins
        dc_ref, dba_ref = outs

        def blk(cb, carry):
            cols = pl.ds(pl.multiple_of(cb * LANES, LANES), LANES)
            hcols = pl.ds(pl.multiple_of((cb & (DN_HEADS - 1)) * LANES, LANES), LANES)
            prev = jnp.where(i > 0, xp_ref[:, cols], 0.0)
            xcat = jnp.concatenate([prev, x_ref[:, cols]], axis=0)
            c = _conv_taps(xcat, w_ref, cols, DN_CONV, DN_HALO, tm)
            y = _silu(c)
            dy = jnp.where(cb < DN_HEADS, dq_ref[:, hcols],
                           jnp.where(cb < 2 * DN_HEADS, dk_ref[:, hcols], dv_ref[:, hcols]))
            rs = lax.rsqrt(jnp.sum(y * y, axis=-1, keepdims=True) + L2_EPS)
            fac = jnp.where(cb < DN_HEADS, DN_HEAD_DIM ** -0.5, 1.0)
            nrm = y * rs
            dn = dy * fac
            dy_norm = rs * (dn - nrm * jnp.sum(dn * nrm, axis=-1, keepdims=True))
            dc_ref[:, cols] = jnp.where(cb < 2 * DN_HEADS, dy_norm, dy) * _silu_grad(c)
            return carry

        lax.fori_loop(0, qkv_raw.shape[1] // LANES, blk, 0, unroll=4)

        lane = _iota((tm, LANES), 1)
        dbeta = jnp.zeros((tm, LANES), F32)
        dg = jnp.zeros((tm, LANES), F32)
        for h in range(DN_HEADS):
            dbeta = dbeta + jnp.where(lane == h, dhs_ref[h, :, 0:1], 0.0)
            dg = dg + jnp.where(lane == DN_HEADS + h, dhs_ref[h, :, 1:2], 0.0)
        bav = ba_ref[...]
        beta = _sigmoid(bav)
        ea = jnp.exp(gate_ref[0:1, :])
        pre = bav + gate_ref[1:2, :]
        g = -ea * _softplus(pre)
        da = dg * (-ea) * _sigmoid(pre)
        dba_ref[...] = (dbeta * beta * (1.0 - beta) + da).astype(BF16)
        rid = _iota((8, LANES), 0)
        accs[0][...] += (jnp.where(rid == 0, jnp.sum(dg * g, axis=0, keepdims=True), 0.0)
                         + jnp.where(rid == 1, jnp.sum(da, axis=0, keepdims=True), 0.0))

    return row_call("dn_pre_bwd", body, s_len, tm,
                    [(dq, "tile"), (dk, "tile"), (dv, "tile"), (dhs, "tile"), (qkv_raw, "tile"),
                     (qkv_raw, ("prev", DN_HALO)), (ba, "tile"), (w_conv, "full"), (gate, "full")],
                    [(qkv_raw.shape, F32), (ba.shape, BF16)], [(8, LANES)])


def dn_conv_bwd(dc, qkv_raw, w_conv):
    s_len = dc.shape[0]
    tm = min(DN_ROW_TILE, s_len)
    nt = s_len // tm

    def body(i, ins, outs, accs):
        dc_ref, dn_ref, x_ref, xp_ref, w_ref = ins

        def blk(cb, carry):
            cols = pl.ds(pl.multiple_of(cb * LANES, LANES), LANES)
            dy = dc_ref[:, cols]
            nxt = jnp.where(i < nt - 1, dn_ref[:, cols], 0.0)
            dcat = jnp.concatenate([dy, nxt], axis=0)
            outs[0][:, cols] = _conv_taps_bwd_x(dcat, w_ref, cols, DN_CONV, DN_HALO, tm).astype(BF16)
            prev = jnp.where(i > 0, xp_ref[:, cols], 0.0)
            xcat = jnp.concatenate([prev, x_ref[:, cols]], axis=0)
            accs[0][:, cols] += _conv_taps_bwd_w(dy, xcat, DN_CONV, DN_HALO, tm, 8)
            return carry

        lax.fori_loop(0, dc.shape[1] // LANES, blk, 0)

    return row_call("dn_conv_bwd", body, s_len, tm,
                    [(dc, "tile"), (dc, ("next", DN_HALO)), (qkv_raw, "tile"), (qkv_raw, ("prev", DN_HALO)),
                     (w_conv, "full")],
                    [(dc.shape, BF16)], [(8, dc.shape[1])])


def _glu(u_ref, cols, d):
    return u_ref[:, cols] * _sigmoid(u_ref[:, pl.ds(pl.multiple_of(d + cols.start, LANES), cols.size)])


def cv_core_fwd(u, w_dw, b_dw, ln_g, ln_b):
    s_len, d = u.shape[0], u.shape[1] // 2
    tm = min(CONV_ROW_TILE, s_len)

    def body(i, ins, outs, accs):
        u_ref, up_ref, w_ref, bdw_ref, g_ref, b_ref = ins
        s_ref, c_ref = outs

        def blk(cb, carry):
            cols = pl.ds(pl.multiple_of(cb * LANES, LANES), LANES)
            prev = jnp.where(i > 0, _glu(up_ref, cols, d), 0.0)
            xcat = jnp.concatenate([prev, _glu(u_ref, cols, d)], axis=0)
            c_ref[:, cols] = _conv_taps(xcat, w_ref, cols, CV_WIDTH, CV_HALO, tm) + bdw_ref[:, cols]
            return carry

        lax.fori_loop(0, d // LANES, blk, 0)
        c = c_ref[...]
        mu = jnp.mean(c, axis=-1, keepdims=True)
        xc = c - mu
        rstd = lax.rsqrt(jnp.mean(xc * xc, axis=-1, keepdims=True) + LN_EPS)
        s_ref[...] = _silu(xc * rstd * g_ref[...] + b_ref[...]).astype(BF16)

    return row_call("cv_core_fwd", body, s_len, tm,
                    [(u, "tile"), (u, ("prev", CV_HALO)), (w_dw, "full"), (b_dw, "full"), (ln_g, "full"),
                     (ln_b, "full")],
                    [((s_len, d), BF16), ((s_len, d), F32)])


def cv_ln_bwd(ds, c, ln_g, ln_b):
    def body(i, ins, outs, accs):
        cv, g = ins[1][...], ins[2][...]
        mu = jnp.mean(cv, axis=-1, keepdims=True)
        xc = cv - mu
        rstd = lax.rsqrt(jnp.mean(xc * xc, axis=-1, keepdims=True) + LN_EPS)
        xhat = xc * rstd
        dl = ins[0][...].astype(F32) * _silu_grad(xhat * g + ins[3][...])
        dxhat = dl * g
        dc = rstd * (dxhat - jnp.mean(dxhat, axis=-1, keepdims=True)
                     - xhat * jnp.mean(dxhat * xhat, axis=-1, keepdims=True))
        outs[0][...] = dc
        rid = _iota((8, cv.shape[1]), 0)
        accs[0][...] += (jnp.where(rid == 0, jnp.sum(dl * xhat, axis=0, keepdims=True), 0.0)
                         + jnp.where(rid == 1, jnp.sum(dl, axis=0, keepdims=True), 0.0)
                         + jnp.where(rid == 2, jnp.sum(dc, axis=0, keepdims=True), 0.0))

    return row_call("cv_ln_bwd", body, c.shape[0], ROW_TILE,
                    [(ds, "tile"), (c, "tile"), (ln_g, "full"), (ln_b, "full")], [(c.shape, F32)], [(8, c.shape[1])])


def cv_conv_bwd(dc, u, w_dw):
    s_len, d = dc.shape
    tm = min(CONV_ROW_TILE, s_len)
    nt = s_len // tm

    def body(i, ins, outs, accs):
        dc_ref, dn_ref, u_ref, up_ref, w_ref = ins

        def blk(cb, carry):
            cols = pl.ds(pl.multiple_of(cb * LANES, LANES), LANES)
            gcols = pl.ds(pl.multiple_of(d + cb * LANES, LANES), LANES)
            dy = dc_ref[:, cols]
            nxt = jnp.where(i < nt - 1, dn_ref[:, cols], 0.0)
            dgl = _conv_taps_bwd_x(jnp.concatenate([dy, nxt], axis=0), w_ref, cols, CV_WIDTH, CV_HALO, tm)
            u1, sg = u_ref[:, cols], _sigmoid(u_ref[:, gcols])
            du1 = dgl * sg
            du2 = dgl * u1 * sg * (1.0 - sg)
            outs[0][:, cols] = du1.astype(BF16)
            outs[0][:, gcols] = du2.astype(BF16)
            accs[1][:, cols] += jnp.sum(du1, axis=0, keepdims=True)
            accs[1][:, gcols] += jnp.sum(du2, axis=0, keepdims=True)
            prev = jnp.where(i > 0, _glu(up_ref, cols, d), 0.0)
            xcat = jnp.concatenate([prev, u1 * sg], axis=0)
            accs[0][:, cols] += _conv_taps_bwd_w(dy, xcat, CV_WIDTH, CV_HALO, tm, CV_HALO)
            return carry

        lax.fori_loop(0, d // LANES, blk, 0)

    return row_call("cv_conv_bwd", body, s_len, tm,
                    [(dc, "tile"), (dc, ("next", CV_HALO)), (u, "tile"), (u, ("prev", CV_HALO)), (w_dw, "full")],
                    [(u.shape, BF16)], [(CV_HALO, d), (1, 2 * d)])


def xa_core_fwd(name, q, kv):
    d = q.shape[1]

    def body(i, ins, outs, accs):
        for h in range(XA_HEADS):
            cols = slice(h * XA_HEAD_DIM, (h + 1) * XA_HEAD_DIM)
            vcols = slice(d + h * XA_HEAD_DIM, d + (h + 1) * XA_HEAD_DIM)
            s = _dot_nt(ins[0][:, cols], ins[1][:, cols]) * (XA_HEAD_DIM ** -0.5)
            e = jnp.exp(s - jnp.max(s, axis=-1, keepdims=True))
            p = e / jnp.sum(e, axis=-1, keepdims=True)
            outs[0][:, cols] = _dot(p, ins[1][:, vcols]).astype(BF16)

    return row_call(name, body, q.shape[0], ROW_TILE, [(q, "tile"), (kv, "full")], [(q.shape, BF16)])[0]


def xa_core_bwd(name, d_o, q, kv):
    d = q.shape[1]

    def body(i, ins, outs, accs):
        for h in range(XA_HEADS):
            cols = slice(h * XA_HEAD_DIM, (h + 1) * XA_HEAD_DIM)
            vcols = slice(d + h * XA_HEAD_DIM, d + (h + 1) * XA_HEAD_DIM)
            qh, kh, vh, doh = ins[1][:, cols], ins[2][:, cols], ins[2][:, vcols], ins[0][:, cols]
            s = _dot_nt(qh, kh) * (XA_HEAD_DIM ** -0.5)
            e = jnp.exp(s - jnp.max(s, axis=-1, keepdims=True))
            p = e / jnp.sum(e, axis=-1, keepdims=True)
            dp = _dot_nt(doh, vh)
            ds = p * (dp - jnp.sum(dp * p, axis=-1, keepdims=True)) * (XA_HEAD_DIM ** -0.5)
            outs[0][:, cols] = _dot(ds, kh).astype(BF16)
            accs[0][:, cols] += _dot_tn(ds, qh)
            accs[0][:, vcols] += _dot_tn(p, doh)

    return row_call(name, body, q.shape[0], ROW_TILE, [(d_o, "tile"), (q, "tile"), (kv, "full")],
                    [(q.shape, BF16)], [kv.shape])


def adamw(name, w, g, m, v):
    def body(i, ins, outs, accs):
        wv, gv = ins[0][...], ins[1][...]
        mn = ADAM_B1 * ins[2][...] + (1.0 - ADAM_B1) * gv
        vn = ADAM_B2 * ins[3][...] + (1.0 - ADAM_B2) * jnp.square(gv)
        m_hat = mn / (1.0 - ADAM_B1 ** ADAM_STEP)
        v_hat = vn / (1.0 - ADAM_B2 ** ADAM_STEP)
        outs[0][...] = -ADAM_LR * (m_hat / (jnp.sqrt(v_hat) + ADAM_EPS) + ADAM_WD * wv)
        outs[1][...] = mn
        outs[2][...] = vn

    return row_call(name, body, w.shape[0], ROW_TILE, [(w, "tile"), (g, "tile"), (m, "tile"), (v, "tile")],
                    [(w.shape, F32)] * 3)


def adamw_halves(name, w, g_mine, g_sibling, m, v, core):
    n_layers = len(g_mine)
    rows, cols = w.shape
    half_rows = rows // n_layers // 2
    tm = _pick_tile(half_rows, ADAMW_ROW_TILE)
    per_half = half_rows // tm

    def body(core_ref, w_ref, *rest):
        g_refs = rest[:2 * n_layers]
        m_ref, v_ref, g_out, d_out, m_out, v_out = rest[2 * n_layers:]
        i = pl.program_id(0)
        mine = ((i // per_half) % 2) == core_ref[0]
        layer = i // (2 * per_half)
        gv = jnp.where(mine, g_refs[0][...], g_refs[n_layers][...])
        for l in range(1, n_layers):
            gv = jnp.where(layer == l, jnp.where(mine, g_refs[l][...], g_refs[n_layers + l][...]), gv)
        mn = ADAM_B1 * m_ref[...] + (1.0 - ADAM_B1) * gv
        vn = ADAM_B2 * v_ref[...] + (1.0 - ADAM_B2) * jnp.square(gv)
        m_hat = mn / (1.0 - ADAM_B1 ** ADAM_STEP)
        v_hat = vn / (1.0 - ADAM_B2 ** ADAM_STEP)
        g_out[...] = gv
        d_out[...] = -ADAM_LR * (m_hat / (jnp.sqrt(v_hat) + ADAM_EPS) + ADAM_WD * w_ref[...])
        m_out[...] = mn
        v_out[...] = vn

    whole = pl.BlockSpec((tm, cols), lambda i, core_ref: (i, 0))

    def half(layer, own):
        def index(i, core_ref):
            used = (i // (2 * per_half) == layer) & ((((i // per_half) % 2) == core_ref[0]) == own)
            return (jnp.where(used, i % per_half, 0), 0)

        return pl.BlockSpec((tm, cols), index)

    halves = [half(l, True) for l in range(n_layers)] + [half(l, False) for l in range(n_layers)]
    return pl.pallas_call(
        body, name=name,
        grid_spec=pltpu.PrefetchScalarGridSpec(
            num_scalar_prefetch=1, grid=(2 * per_half * n_layers,),
            in_specs=[whole] + halves + [whole, whole], out_specs=[whole] * 4),
        out_shape=[jax.ShapeDtypeStruct(w.shape, F32)] * 4,
        compiler_params=_cparams(("parallel",)),
    )(core, w, *g_mine, *g_sibling, m, v)


HBM_SPEC = pl.BlockSpec(memory_space=pltpu.HBM)


def _position():
    return lax.axis_index("x"), lax.axis_index("y"), lax.axis_index("c")


def _other_chips(x, y):
    return [(1 - x, y), (x, 1 - y), (1 - x, 1 - y)]


def _row_chunks(rows):
    return rows // D2D_CHUNK_ROWS if rows % D2D_CHUNK_ROWS == 0 else 1


def _start_chunked(make, rows):
    k = _row_chunks(rows)
    for i in range(k):
        make(i * (rows // k), rows // k).start()


def gather_shards(packs):
    n = len(packs)

    def body(*refs):
        srcs, outs = refs[:n], refs[n:2 * n]
        send_sems, recv_sems = refs[2 * n:]
        x, y, c = _position()
        me = 2 * x + y
        chips = _other_chips(x, y)
        sibling = (x, y, 1 - c)

        def over_ici(a, j):
            px, py = chips[j]
            rows = srcs[a].shape[0] // 2
            return pltpu.make_async_remote_copy(
                src_ref=srcs[a].at[pl.ds(c * rows, rows), :], dst_ref=outs[a].at[me, pl.ds(c * rows, rows), :],
                send_sem=send_sems.at[a, j], recv_sem=recv_sems.at[a, j], device_id=(px, py, c), device_id_type=MESH)

        def landed(a, j):
            px, py = chips[j]
            rows = srcs[a].shape[0] // 2
            part = outs[a].at[2 * px + py, pl.ds(c * rows, rows), :]
            return pltpu.make_async_remote_copy(
                src_ref=part, dst_ref=part, send_sem=send_sems.at[a, j], recv_sem=recv_sems.at[a, j],
                device_id=(px, py, c), device_id_type=MESH)

        def over_d2d(a, j, cc, off, size):
            px, py = chips[j]
            rows = srcs[a].shape[0] // 2
            part = outs[a].at[2 * px + py, pl.ds(cc * rows + off, size), :]
            return pltpu.make_async_remote_copy(
                src_ref=part, dst_ref=part, send_sem=send_sems.at[a, 3 + j], recv_sem=recv_sems.at[a, 3 + j],
                device_id=sibling, device_id_type=MESH)

        for a in range(n):
            for j in range(3):
                over_ici(a, j).start()
        for a in range(n):
            for j in range(3):
                landed(a, j).wait_recv()
                _start_chunked(functools.partial(over_d2d, a, j, c), srcs[a].shape[0] // 2)
        for a in range(n):
            rows = srcs[a].shape[0] // 2
            for j in range(3):
                over_d2d(a, j, 1 - c, 0, rows).wait_recv()
                over_d2d(a, j, c, 0, rows).wait_send()
                over_ici(a, j).wait_send()

    return pl.pallas_call(
        body, name="gather_shards",
        in_specs=[HBM_SPEC] * n, out_specs=[HBM_SPEC] * n,
        out_shape=[jax.ShapeDtypeStruct((N_CHIPS,) + p.shape, p.dtype) for p in packs],
        scratch_shapes=[pltpu.SemaphoreType.DMA((n, 6)), pltpu.SemaphoreType.DMA((n, 6))],
    )(*packs)


def pair_split(name, packs):
    n = len(packs)

    def body(*refs):
        srcs, outs = refs[:n], refs[n:2 * n]
        send_sems, recv_sems = refs[2 * n:]
        x, y, c = _position()

        def remote(a, off, size):
            rows = srcs[a].shape[1] // 2
            return pltpu.make_async_remote_copy(
                src_ref=srcs[a].at[:, pl.ds((1 - c) * rows + off, size), :],
                dst_ref=outs[a].at[:, pl.ds(off, size), :],
                send_sem=send_sems.at[a], recv_sem=recv_sems.at[a], device_id=(x, y, 1 - c), device_id_type=MESH)

        for a in range(n):
            _start_chunked(functools.partial(remote, a), srcs[a].shape[1] // 2)
        for a in range(n):
            remote(a, 0, srcs[a].shape[1] // 2).wait()

    return pl.pallas_call(
        body, name=name, in_specs=[HBM_SPEC] * n, out_specs=[HBM_SPEC] * n,
        out_shape=[jax.ShapeDtypeStruct((p.shape[0], p.shape[1] // 2, p.shape[2]), p.dtype) for p in packs],
        scratch_shapes=[pltpu.SemaphoreType.DMA((n,)), pltpu.SemaphoreType.DMA((n,))],
    )(*packs)


def pair_join(name, halves):
    n = len(halves)

    def body(*refs):
        srcs, outs = refs[:n], refs[n:2 * n]
        send_sems, recv_sems = refs[2 * n:]
        x, y, c = _position()

        def remote(a, off, size):
            return pltpu.make_async_remote_copy(
                src_ref=srcs[a].at[pl.ds(off, size), :], dst_ref=outs[a].at[pl.ds(off, size), :],
                send_sem=send_sems.at[a], recv_sem=recv_sems.at[a], device_id=(x, y, 1 - c), device_id_type=MESH)

        for a in range(n):
            _start_chunked(functools.partial(remote, a), srcs[a].shape[0])
        for a in range(n):
            remote(a, 0, srcs[a].shape[0]).wait()

    return pl.pallas_call(
        body, name=name, in_specs=[HBM_SPEC] * n, out_specs=[HBM_SPEC] * n,
        out_shape=[jax.ShapeDtypeStruct(p.shape, p.dtype) for p in halves],
        scratch_shapes=[pltpu.SemaphoreType.DMA((n,)), pltpu.SemaphoreType.DMA((n,))],
    )(*halves)


SEM_SPEC = pl.BlockSpec(memory_space=pltpu.SEMAPHORE)
DATAFLOW = pltpu.SideEffectType.DATAFLOW_SIDE_EFFECTING


def _ici_copy(kind, srcs, lands, send_sems, recv_sems, a, j):
    x, y, c = _position()
    px, py = _other_chips(x, y)[j]
    if kind == "gather":
        rows = srcs[a].shape[0] // 2
        src = srcs[a].at[pl.ds(c * rows, rows), :]
        dst = lands[a].at[2 * x + y, pl.ds(c * rows, rows), :]
    else:
        src = srcs[a].at[2 * px + py]
        dst = lands[a].at[j]
    return pltpu.make_async_remote_copy(src_ref=src, dst_ref=dst, send_sem=send_sems, recv_sem=recv_sems,
                                        device_id=(px, py, c), device_id_type=MESH)


def ici_start(name, kind, srcs, land_shapes):
    n = len(srcs)
    lands = [pltpu.with_memory_space_constraint(lax.empty(shp, s.dtype), pltpu.HBM) for shp, s in zip(land_shapes, srcs)]

    def body(*refs):
        src_refs, land_refs = refs[:n], refs[n:2 * n]
        send_sems, recv_sems = refs[2 * n], refs[2 * n + 1]
        token = refs[-1]
        for a in range(n):
            for j in range(N_CHIPS - 1):
                _ici_copy(kind, src_refs, land_refs, send_sems, recv_sems, a, j).start()
        token[...] = jnp.zeros_like(token)

    sems = pltpu.SemaphoreType.DMA(())
    res = pl.pallas_call(
        body, name=name,
        out_shape=[sems, sems] + [pltpu.HBM(s.shape, s.dtype) for s in srcs]
        + [pltpu.HBM(l.shape, l.dtype) for l in lands] + [jax.ShapeDtypeStruct((8, LANES), F32)],
        in_specs=[HBM_SPEC] * (2 * n),
        out_specs=[SEM_SPEC, SEM_SPEC] + [HBM_SPEC] * (2 * n) + [pl.BlockSpec(memory_space=pltpu.VMEM)],
        input_output_aliases={i: 2 + i for i in range(2 * n)},
        compiler_params=pltpu.CompilerParams(has_side_effects=DATAFLOW),
    )(*[pltpu.with_memory_space_constraint(s, pltpu.HBM) for s in srcs], *lands)
    return res[0], res[1], list(res[2:2 + n]), list(res[2 + n:2 + 2 * n]), res[-1]


def ici_wait(name, kind, send_sems, recv_sems, srcs, lands, after):
    n = len(srcs)

    def body(*refs):
        src_refs, land_refs = refs[:n], refs[n:2 * n]
        send, recv = refs[2 * n], refs[2 * n + 1]
        for a in range(n):
            for j in range(N_CHIPS - 1):
                cp = _ici_copy(kind, src_refs, land_refs, send, recv, a, j)
                cp.wait_send()
                cp.wait_recv()

    res = pl.pallas_call(
        body, name=name,
        out_shape=[pltpu.HBM(s.shape, s.dtype) for s in srcs] + [pltpu.HBM(l.shape, l.dtype) for l in lands],
        in_specs=[HBM_SPEC] * (2 * n) + [SEM_SPEC, SEM_SPEC, pl.BlockSpec(memory_space=pl.ANY)],
        out_specs=[HBM_SPEC] * (2 * n),
        input_output_aliases={i: i for i in range(2 * n)},
        compiler_params=pltpu.CompilerParams(has_side_effects=DATAFLOW),
    )(*srcs, *lands, send_sems, recv_sems, after)
    return list(res[:n]), list(res[n:])


def pair_forward(gathered):
    n = len(gathered)

    def body(*refs):
        outs = refs[n:2 * n]
        send_sems, recv_sems = refs[2 * n:]
        x, y, c = _position()
        chips = _other_chips(x, y)

        def part(a, j, cc, off, size):
            px, py = chips[j]
            rows = outs[a].shape[1] // 2
            ref = outs[a].at[2 * px + py, pl.ds(cc * rows + off, size), :]
            return pltpu.make_async_remote_copy(
                src_ref=ref, dst_ref=ref, send_sem=send_sems.at[a, j], recv_sem=recv_sems.at[a, j],
                device_id=(x, y, 1 - c), device_id_type=MESH)

        for a in range(n):
            for j in range(N_CHIPS - 1):
                _start_chunked(functools.partial(part, a, j, c), outs[a].shape[1] // 2)
        for a in range(n):
            rows = outs[a].shape[1] // 2
            for j in range(N_CHIPS - 1):
                part(a, j, 1 - c, 0, rows).wait_recv()
                part(a, j, c, 0, rows).wait_send()

    return pl.pallas_call(
        body, name="pair_forward", in_specs=[HBM_SPEC] * n, out_specs=[HBM_SPEC] * n,
        out_shape=[jax.ShapeDtypeStruct(g.shape, g.dtype) for g in gathered],
        input_output_aliases={i: i for i in range(n)},
        scratch_shapes=[pltpu.SemaphoreType.DMA((n, N_CHIPS - 1)), pltpu.SemaphoreType.DMA((n, N_CHIPS - 1))],
    )(*gathered)


def all_sum_small(part):
    n_dev = 8
    rows = part.shape[0]

    def body(src, out, buf, send_sems, recv_sems):
        x, y, c = _position()
        me = 4 * x + 2 * y + c
        buf[me] = src[...]
        copies = []
        for k in range(1, n_dev):
            px, py, pc = x ^ ((k >> 2) & 1), y ^ ((k >> 1) & 1), c ^ (k & 1)
            cp = pltpu.make_async_remote_copy(
                src_ref=src, dst_ref=buf.at[me], send_sem=send_sems.at[k - 1], recv_sem=recv_sems.at[k - 1],
                device_id=(px, py, pc), device_id_type=MESH)
            cp.start()
            copies.append(cp)
        for cp in copies:
            cp.wait()
        acc = buf[0]
        for k in range(1, n_dev):
            acc = acc + buf[k]
        out[...] = acc

    return pl.pallas_call(
        body, name="all_sum_small",
        in_specs=[pl.BlockSpec(memory_space=pltpu.VMEM)], out_specs=pl.BlockSpec(memory_space=pltpu.VMEM),
        out_shape=jax.ShapeDtypeStruct(part.shape, F32),
        scratch_shapes=[pltpu.VMEM((n_dev, rows, part.shape[1]), F32),
                        pltpu.SemaphoreType.DMA((n_dev - 1,)), pltpu.SemaphoreType.DMA((n_dev - 1,))],
    )(part)


def add_pairs(name, src, theirs, core, out_dtype):
    slabs, rows, cols = theirs.shape
    tm = _pick_tile(rows, ROW_TILE)
    nb = rows // tm

    def body(core_ref, a_ref, b_ref, o_ref):
        o_ref[...] = (a_ref[...].astype(F32) + b_ref[...].astype(F32)).astype(out_dtype)

    return pl.pallas_call(
        body, name=name,
        grid_spec=pltpu.PrefetchScalarGridSpec(
            num_scalar_prefetch=1, grid=(slabs, nb),
            in_specs=[pl.BlockSpec((None, tm, cols), lambda s, i, core_ref: (s, core_ref[0] * nb + i, 0)),
                      pl.BlockSpec((None, tm, cols), lambda s, i, core_ref: (s, i, 0))],
            out_specs=pl.BlockSpec((None, tm, cols), lambda s, i, core_ref: (s, i, 0))),
        out_shape=jax.ShapeDtypeStruct(theirs.shape, out_dtype),
        compiler_params=_cparams(("parallel", "parallel")),
    )(core, src, theirs)


def add_four(name, src, theirs, chip):
    _, rows, cols = theirs.shape
    tm = _pick_tile(rows, ROW_TILE)

    def body(chip_ref, a_ref, b_ref, o_ref):
        acc = a_ref[...].astype(F32)
        for j in range(N_CHIPS - 1):
            acc = acc + b_ref[j].astype(F32)
        o_ref[...] = acc

    return pl.pallas_call(
        body, name=name,
        grid_spec=pltpu.PrefetchScalarGridSpec(
            num_scalar_prefetch=1, grid=(rows // tm,),
            in_specs=[pl.BlockSpec((None, tm, cols), lambda i, chip_ref: (chip_ref[0], i, 0)),
                      pl.BlockSpec((N_CHIPS - 1, tm, cols), lambda i, chip_ref: (0, i, 0))],
            out_specs=pl.BlockSpec((tm, cols), lambda i, chip_ref: (i, 0))),
        out_shape=jax.ShapeDtypeStruct((rows, cols), F32),
        compiler_params=_cparams(("parallel",)),
    )(chip, src, theirs)


PACK_COLS = 1024
SMALL_ROW_MULTIPLE = 32
BIG = ["dn_w_in", "dn_w_out", "cv_w_pw1", "cv_w_pw2", "xa_w_q", "xa_w_kv", "xa_w_o", "mlp_w_up", "mlp_w_down"]
SMALL = ["dn_w_conv", "cv_norm", "cv_b_pw1", "cv_w_dw", "cv_b_dw", "cv_ln_g", "cv_ln_b", "cv_b_pw2"]
SHARD_AXIS = {"dn_w_in": 2, "dn_w_conv": 2, "dn_w_out": 1, "cv_norm": 1, "cv_w_pw1": 2, "cv_b_pw1": 1,
              "cv_w_dw": 2, "cv_b_dw": 1, "cv_ln_g": 1, "cv_ln_b": 1, "cv_w_pw2": 1, "cv_b_pw2": 1,
              "xa_w_q": 1, "xa_w_kv": 2, "xa_w_o": 1, "mlp_w_up": 2, "mlp_w_down": 1}
REPLICATED = ["dn_norm", "dn_a_log", "dn_dt_bias", "dn_out_norm", "xa_norm", "xa_mem_norm", "mlp_norm", "final_norm"]


def _pack_rows(size):
    return -(-size // PACK_COLS)


SHARD_SHAPES = {
    "dn_w_in": (1, 1024, 1028), "dn_w_conv": (1, 4, 768), "dn_w_out": (1, 256, 1024), "cv_norm": (1, 256),
    "cv_w_pw1": (1, 1024, 512), "cv_b_pw1": (1, 512), "cv_w_dw": (1, 31, 256), "cv_b_dw": (1, 256),
    "cv_ln_g": (1, 256), "cv_ln_b": (1, 256), "cv_w_pw2": (1, 256, 1024), "cv_b_pw2": (1, 256),
    "xa_w_q": (2, 256, 1024), "xa_w_kv": (2, 1024, 512), "xa_w_o": (2, 256, 1024),
    "mlp_w_up": (2, 1024, 1024), "mlp_w_down": (2, 1024, 1024)}


def _shard_shape(nm):
    return SHARD_SHAPES[nm]


def _pack(tensors, names, dtype, row_multiple):
    pieces = []
    for nm in names:
        t = tensors[nm]
        flat = t.reshape(t.shape[0], -1) if t.ndim > len(_shard_shape(nm)) else t.reshape(1, -1)
        pad = _pack_rows(flat.shape[1]) * PACK_COLS - flat.shape[1]
        pieces.append(jnp.pad(flat.astype(dtype), ((0, 0), (0, pad))))
    cat = jnp.concatenate(pieces, axis=1)
    rows = cat.shape[1] // PACK_COLS
    total = -(-rows // row_multiple) * row_multiple
    cat = jnp.pad(cat, ((0, 0), (0, (total - rows) * PACK_COLS)))
    return cat.reshape(cat.shape[0], total, PACK_COLS)


def _unpack(pack, names):
    lead = pack.shape[:-2]
    flat = pack.reshape(lead + (-1,))
    out, off = {}, 0
    for nm in names:
        shp = _shard_shape(nm)
        size = 1
        for s in shp:
            size *= s
        out[nm] = flat[..., off:off + size].reshape(lead + shp)
        off += _pack_rows(size) * PACK_COLS
    return out


def _to_full(nm, stacked):
    ax = SHARD_AXIS[nm]
    moved = jnp.moveaxis(stacked, 0, ax)
    shp = list(_shard_shape(nm))
    shp[ax] *= N_CHIPS
    return moved.reshape(shp)


def _to_shards(nm, full):
    ax = SHARD_AXIS[nm]
    shp = list(_shard_shape(nm))
    split = full.reshape(shp[:ax] + [N_CHIPS, shp[ax]] + shp[ax + 1:])
    return jnp.moveaxis(split, ax, 0)


def _row(v):
    return v.reshape(1, -1)


class Stacked:
    def __init__(self, arr, split, layer):
        self.arr, self.kw = arr, dict(b_split=split, b_layer=layer)


def _grad_out(split):
    return dict(out_dtype=BF16, out_split=split, out_layer=(0, 1))


def _with_next(res, next_gain):
    return (res[0], res[1]) if next_gain is not None else (res, None)


def mlp_fwd(tag, h, g, w_up, w_down, n=None, next_gain=None):
    if n is None:
        n = rms_fwd(tag + "_norm", h, g)
    act = mm(tag + "_up", n, w_up.arr, out_dtype=BF16, epi=lambda acc: jnp.square(jnp.maximum(acc, 0.0)), **w_up.kw)
    out, n_next = _with_next(mm(tag + "_down", act, w_down.arr, tk=LONG_TILE_K, epi=lambda acc, res: acc + res,
                                epi_tiles=(h,), norm_gain=next_gain, **w_down.kw), next_gain)
    return out, n_next, (n, act)


def mlp_bwd(tag, dh, h, g, w_up, w_down, saved, after=None):
    n, act = saved
    dh, dh16 = dh
    dup = mm(tag + "_d_act", dh16, w_down.arr, tb=True, out_dtype=BF16, after=after,
             epi=lambda acc, t: acc * (2.0 * jnp.sqrt(t.astype(F32))), epi_tiles=(act,), **w_down.kw)
    dw_down = mm(tag + "_dw_down", act, dh16, ta=True, tk=GRAD_TILE_K, **_grad_out("rows"))
    dh_in, dh16_in, dg = mm(tag + "_dn", dup, w_up.arr, tb=True, norm_bwd=(h, g, dh), **w_up.kw)
    dw_up = mm(tag + "_dw_up", n, dup, ta=True, tk=GRAD_TILE_K, **_grad_out("cols"))
    return (dh_in, dh16_in), dg, (dw_up, dw_down)


def xa_fwd(tag, h, mem, g, g_mem, w_q, w_kv, w_o, n=None, next_gain=None):
    if n is None:
        n = rms_fwd(tag + "_norm", h, g)
    mem_n = rms_fwd(tag + "_mem_norm", mem, g_mem)
    q = mm(tag + "_q", n, w_q.arr, out_dtype=BF16, **w_q.kw)
    kv = mm(tag + "_kv", mem_n, w_kv.arr, out_dtype=BF16, **w_kv.kw)
    o = xa_core_fwd(tag + "_core", q, kv)
    out, n_next = _with_next(mm(tag + "_o", o, w_o.arr, epi=lambda acc, res: acc + res, epi_tiles=(h,),
                                norm_gain=next_gain, **w_o.kw), next_gain)
    return out, n_next, (n, mem_n, q, kv, o)


def xa_bwd(tag, dh, h, mem, g, g_mem, w_q, w_kv, w_o, saved):
    n, mem_n, q, kv, o = saved
    dh, dh16 = dh
    d_o = mm(tag + "_d_o", dh16, w_o.arr, tb=True, out_dtype=BF16, **w_o.kw)
    dw_o = mm(tag + "_dw_o", o, dh16, ta=True, tk=GRAD_TILE_K, **_grad_out("rows"))
    dq, dkv = xa_core_bwd(tag + "_core_bwd", d_o, q, kv)
    dh_in, dh16_in, dg = mm(tag + "_dn", dq, w_q.arr, tb=True, norm_bwd=(h, g, dh), **w_q.kw)
    dw_q = mm(tag + "_dw_q", n, dq, ta=True, tk=GRAD_TILE_K, **_grad_out("rows"))
    dw_kv = mm(tag + "_dw_kv", mem_n, dkv, ta=True, **_grad_out("cols"))
    dmem_n = mm(tag + "_dmem", dkv, w_kv.arr, tb=True, **w_kv.kw)
    dg_mem = mem_norm_bwd(tag + "_mem_norm_bwd", dmem_n, mem, g_mem)
    return (dh_in, dh16_in), dg, dg_mem, (dw_q, dw_kv, dw_o)


def _gate_tile(a_log, dt_bias):
    t = jnp.zeros((8, LANES), F32)
    t = t.at[0, DN_HEADS:2 * DN_HEADS].set(a_log.reshape(-1))
    return t.at[1, DN_HEADS:2 * DN_HEADS].set(dt_bias.reshape(-1))


def dn_fwd(h, g, w_qkv, w_z, w_ba, w_conv, gate, out_norm, w_out, next_gain=None):
    n = rms_fwd("dn_norm", h, g)
    qkv_raw = mm("dn_proj_qkv", n, w_qkv)
    z = mm("dn_proj_z", n, w_z)
    ba = mm("dn_proj_ba", n, w_ba)
    qkv, hs = dn_pre(qkv_raw, ba, w_conv, gate)
    u, w, t_inv = dn_solve(qkv, hs)
    o, states = dn_scan_fwd(qkv, u, w, hs)
    og = dn_post(o, z, out_norm)
    out, n_next = _with_next(mm("dn_out", og, w_out.arr, epi=lambda acc, res: acc + res, epi_tiles=(h,),
                                norm_gain=next_gain, **w_out.kw), next_gain)
    return out, n_next, (n, qkv_raw, z, ba, qkv, hs, u, w, t_inv, o, states, og)


def dn_bwd(dh, h, g, w_qkv, w_z, w_ba, w_conv, gate, out_norm, w_out, saved, after=None):
    n, qkv_raw, z, ba, qkv, hs, u, w, t_inv, o, states, og = saved
    dh, dh16 = dh
    d_og = mm("dn_d_og", dh16, w_out.arr, tb=True, out_dtype=BF16, after=after, **w_out.kw)
    dw_out = mm("dn_dw_out", og, dh16, ta=True, tk=GRAD_TILE_K, **_grad_out("rows"))
    d_o, dz, d_out_norm = dn_post_bwd(d_og, o, z, out_norm)
    dq, dk, dv, dhs = dn_scan_bwd(qkv, u, w, t_inv, hs, states, d_o)
    dc, dba, d_gate = dn_pre_bwd(dq, dk, dv, dhs, qkv_raw, ba, w_conv, gate)
    dqkv_raw, dw_conv = dn_conv_bwd(dc, qkv_raw, w_conv)
    dn = mm("dn_dn_qkv", dqkv_raw, w_qkv, tb=True, tk=w_qkv.shape[1])
    dn = mm("dn_dn_z", dz, w_z, tb=True, epi=lambda acc, t: acc + t, epi_tiles=(dn,))
    dh_in, _, dg = mm("dn_dn_ba", dba, w_ba, tb=True, epi=lambda acc, t: acc + t, epi_tiles=(dn,),
                      norm_bwd=(h, g, dh))
    dw_qkv = mm("dn_dw_qkv", n, dqkv_raw, ta=True, tk=GRAD_TILE_K)
    dw_z = mm("dn_dw_z", n, dz, ta=True, tk=GRAD_TILE_K)
    dw_ba = mm("dn_dw_ba", n, dba, ta=True, tk=GRAD_TILE_K)
    return dh_in, dg, dw_qkv, dw_z, dw_ba, dw_conv, d_gate, d_out_norm, dw_out


def cv_fwd(h, g, w_pw1, b_pw1, w_dw, b_dw, ln_g, ln_b, w_pw2, b_pw2, n=None, next_gain=None):
    if n is None:
        n = rms_fwd("cv_norm", h, g)
    u = mm("cv_pw1", n, w_pw1.arr, epi=lambda acc, b: acc + b, epi_rows=(b_pw1,), **w_pw1.kw)
    s, c = cv_core_fwd(u, w_dw, b_dw, ln_g, ln_b)
    out, n_next = _with_next(mm("cv_pw2", s, w_pw2.arr, epi=lambda acc, res, b: acc + res + b, epi_tiles=(h,),
                                epi_rows=(b_pw2,), norm_gain=next_gain, **w_pw2.kw), next_gain)
    return out, n_next, (n, u, s, c)


def cv_bwd(dh, h, g, w_pw1, w_dw, ln_g, ln_b, w_pw2, saved):
    n, u, s, c = saved
    dh, dh16 = dh
    ds = mm("cv_d_s", dh16, w_pw2.arr, tb=True, out_dtype=BF16, **w_pw2.kw)
    dw_pw2 = mm("cv_dw_pw2", s, dh16, ta=True, tk=GRAD_TILE_K, **_grad_out("rows"))
    db_pw2 = col_sum("cv_db_pw2", dh)
    dc, ln_acc = cv_ln_bwd(ds, c, ln_g, ln_b)
    du, dw_dw, db_pw1 = cv_conv_bwd(dc, u, w_dw)
    dh_in, dh16_in, dg = mm("cv_dn", du, w_pw1.arr, tb=True, norm_bwd=(h, g, dh), **w_pw1.kw)
    dw_pw1 = mm("cv_dw_pw1", n, du, ta=True, tk=GRAD_TILE_K, **_grad_out("cols"))
    return (dh_in, dh16_in), dg, dw_pw1, db_pw1, dw_dw, ln_acc, dw_pw2, db_pw2


WEIGHTS = ["dn_norm", "dn_w_in", "dn_w_conv", "dn_a_log", "dn_dt_bias", "dn_out_norm", "dn_w_out", "cv_norm",
           "cv_w_pw1", "cv_b_pw1", "cv_w_dw", "cv_b_dw", "cv_ln_g", "cv_ln_b", "cv_w_pw2", "cv_b_pw2", "xa_norm",
           "xa_mem_norm", "xa_w_q", "xa_w_kv", "xa_w_o", "mlp_norm", "mlp_w_up", "mlp_w_down", "final_norm"]


def _as_2d(t):
    if t.ndim == 1:
        return t.reshape(1, -1)
    return t.reshape(-1, t.shape[-1])


def kernel(x, mem, dn_norm, dn_w_in, dn_w_conv, dn_a_log, dn_dt_bias, dn_out_norm, dn_w_out, cv_norm, cv_w_pw1, cv_b_pw1, cv_w_dw, cv_b_dw, cv_ln_g, cv_ln_b, cv_w_pw2, cv_b_pw2, xa_norm, xa_mem_norm, xa_w_q, xa_w_kv, xa_w_o, mlp_norm, mlp_w_up, mlp_w_down, final_norm, loss_target, m_dn_norm, m_dn_w_in, m_dn_w_conv, m_dn_a_log, m_dn_dt_bias, m_dn_out_norm, m_dn_w_out, m_cv_norm, m_cv_w_pw1, m_cv_b_pw1, m_cv_w_dw, m_cv_b_dw, m_cv_ln_g, m_cv_ln_b, m_cv_w_pw2, m_cv_b_pw2, m_xa_norm, m_xa_mem_norm, m_xa_w_q, m_xa_w_kv, m_xa_w_o, m_mlp_norm, m_mlp_w_up, m_mlp_w_down, m_final_norm, v_dn_norm, v_dn_w_in, v_dn_w_conv, v_dn_a_log, v_dn_dt_bias, v_dn_out_norm, v_dn_w_out, v_cv_norm, v_cv_w_pw1, v_cv_b_pw1, v_cv_w_dw, v_cv_b_dw, v_cv_ln_g, v_cv_ln_b, v_cv_w_pw2, v_cv_b_pw2, v_xa_norm, v_xa_mem_norm, v_xa_w_q, v_xa_w_kv, v_xa_w_o, v_mlp_norm, v_mlp_w_up, v_mlp_w_down, v_final_norm):
    args = dict(locals())
    wts = {nm: args[nm] for nm in WEIGHTS}
    mom = {nm: args["m_" + nm] for nm in WEIGHTS}
    var = {nm: args["v_" + nm] for nm in WEIGHTS}
    core = lax.axis_index("c").astype(jnp.int32).reshape(1)
    chip = (2 * lax.axis_index("x") + lax.axis_index("y")).astype(jnp.int32)
    def own_slab(got, src):
        return lax.dynamic_update_slice(got, src[None], (chip, 0, 0))

    shard2d = {nm: wts[nm].astype(BF16).reshape(-1, wts[nm].shape[-1]) for nm in BIG}
    first = ["dn_w_in", "dn_w_out"]
    later = [nm for nm in BIG if nm not in first]
    sources = [shard2d[nm] for nm in first] + [_pack(wts, SMALL, F32, SMALL_ROW_MULTIPLE)[0]]
    gathered = [own_slab(got, src) for got, src in zip(gather_shards(sources), sources)]
    stacked = {"dn_w_out": gathered[1].reshape((N_CHIPS,) + SHARD_SHAPES["dn_w_out"])}
    full = {nm: _to_full(nm, t) for nm, t in _unpack(gathered[2], SMALL).items()}
    full["dn_w_in"] = _to_full("dn_w_in", gathered[0].reshape((N_CHIPS,) + SHARD_SHAPES["dn_w_in"]))
    full.update({nm: wts[nm] for nm in REPLICATED})
    later_src = [shard2d[nm] for nm in later]
    g_send, g_recv, later_src, g_lands, started = ici_start(
        "gather_start", "gather", later_src, [(N_CHIPS,) + s.shape for s in later_src])
    full["dn_norm"] = full["dn_norm"] + started[0, 0]

    def rest_weights(after):
        srcs, lands = ici_wait("gather_wait", "gather", g_send, g_recv, later_src, g_lands, after)
        return {nm: own_slab(land, src).reshape((N_CHIPS,) + SHARD_SHAPES[nm])
                for nm, land, src in zip(later, pair_forward(lands), srcs)}

    pending = []

    def on_grads(items):
        tag = "_".join(sorted({str(layer) for _, layer, _ in items}))
        parts = [g.reshape(N_CHIPS, -1, g.shape[-1]) for _, _, g in items]
        theirs = pair_split("pair_split_" + tag, parts)
        pairs = [add_pairs("pair_add_%s%d" % (nm, layer), p, t, core, BF16)
                 for (nm, layer, _), p, t in zip(items, parts, theirs)]
        send, recv, pairs, lands, token = ici_start(
            "scatter_start_" + tag, "scatter", pairs, [(N_CHIPS - 1,) + p.shape[1:] for p in pairs])
        pending.append((tag, items, send, recv, pairs, lands))
        return token

    dh, grads, rep = local_step(x[0], mem[0], loss_target[0], stacked, full, rest_weights, on_grads)

    halves = {}
    last = [("dn_w_in", 0, _to_shards("dn_w_in", grads["dn_w_in"]).astype(BF16)), ("dn_w_out", 0, grads["dn_w_out"][0]),
            ("small", 0, _pack({nm: _to_shards(nm, grads[nm]) for nm in SMALL}, SMALL, F32, SMALL_ROW_MULTIPLE))]
    parts = [g.reshape(N_CHIPS, -1, g.shape[-1]) for _, _, g in last]
    theirs = pair_split("pair_split_last", parts)
    pairs = [add_pairs("pair_add_" + nm, p, t, core, p.dtype) for (nm, _, _), p, t in zip(last, parts, theirs)]
    l_send, l_recv, l_pairs, l_lands, l_started = ici_start(
        "scatter_start_last", "scatter", pairs, [(N_CHIPS - 1,) + p.shape[1:] for p in pairs])
    for tag, items, send, recv, pairs, lands in pending:
        pairs, lands = ici_wait("scatter_wait_" + tag, "scatter", send, recv, pairs, lands, l_started)
        for (nm, layer, _), p, o in zip(items, pairs, lands):
            halves[nm, layer] = add_four("chip_add_%s%d" % (nm, layer), p, o, chip.reshape(1))
    keys = sorted(halves)
    siblings = dict(zip(keys, pair_join("pair_join_early", [halves[k] for k in keys])))

    delta, new_m, new_v, red = {}, {}, {}, {}

    def big_adamw(nm):
        layers = range(wts[nm].shape[0])
        res = adamw_halves("adamw_" + nm, _as_2d(wts[nm]), [halves[nm, l] for l in layers],
                           [siblings[nm, l] for l in layers], _as_2d(mom[nm]), _as_2d(var[nm]), core)
        red[nm], delta[nm], new_m[nm], new_v[nm] = (r.reshape(wts[nm].shape) for r in res)

    early = [nm for nm in BIG if (nm, 0) in halves]
    for nm in early:
        big_adamw(nm)
    done = jnp.concatenate([new_v[nm].reshape(-1)[:1] for nm in early])
    l_pairs, l_lands = ici_wait("scatter_wait_last", "scatter", l_send, l_recv, l_pairs, l_lands, done)
    for (nm, layer, _), p, o in zip(last, l_pairs, l_lands):
        halves[nm, layer] = add_four("chip_add_" + nm, p, o, chip.reshape(1))
    keys = [(nm, layer) for nm, layer, _ in last]
    siblings.update(zip(keys, pair_join("pair_join_last", [halves[k] for k in keys])))
    south = core[0] == 0
    mine, theirs = halves["small", 0], siblings["small", 0]
    red.update(_unpack(jnp.concatenate([jnp.where(south, mine, theirs), jnp.where(south, theirs, mine)], axis=0),
                       SMALL))

    rep = all_sum_small(rep)
    red["dn_norm"] = rep[0:1]
    red["dn_a_log"] = rep[1:2, DN_HEADS:2 * DN_HEADS]
    red["dn_dt_bias"] = rep[2:3, DN_HEADS:2 * DN_HEADS]
    red["dn_out_norm"] = rep[3:4, :LANES]
    red["xa_norm"], red["xa_mem_norm"], red["mlp_norm"] = rep[4:6], rep[6:8], rep[8:10]
    red["final_norm"] = rep[10]
    loss = rep[11, 0]

    for nm in WEIGHTS:
        shp = wts[nm].shape
        if nm in early:
            continue
        if nm in BIG:
            big_adamw(nm)
            continue
        res = adamw("adamw_" + nm, _as_2d(wts[nm]), _as_2d(red[nm].reshape(shp)), _as_2d(mom[nm]), _as_2d(var[nm]))
        delta[nm], new_m[nm], new_v[nm] = (r.reshape(shp) for r in res)
        red[nm] = red[nm].reshape(shp)

    grad_x = dh[None]
    return (loss, grad_x, *[red[nm] for nm in WEIGHTS], *[delta[nm] for nm in WEIGHTS],
            *[new_m[nm] for nm in WEIGHTS], *[new_v[nm] for nm in WEIGHTS])


def local_step(h0, mem0, target, stacked, full, rest_weights=None, on_grads=None):
    d = h0.shape[1]
    dn_norm, dn_a_log, dn_dt_bias, dn_out_norm = (full[nm] for nm in REPLICATED[:4])
    xa_norm, xa_mem_norm, mlp_norm, final_norm = (full[nm] for nm in REPLICATED[4:])
    inner = DN_HEADS * DN_HEAD_DIM
    w_in = full["dn_w_in"][0]
    w_qkv, w_z = w_in[:, :3 * inner], w_in[:, 3 * inner:4 * inner]
    w_ba = jnp.pad(w_in[:, 4 * inner:], ((0, 0), (0, LANES - 2 * DN_HEADS)))
    w_conv = jnp.pad(full["dn_w_conv"][0], ((0, 8 - DN_CONV), (0, 0)))
    gate = _gate_tile(dn_a_log, dn_dt_bias)
    w_dw = jnp.pad(full["cv_w_dw"][0], ((0, CV_HALO - CV_WIDTH), (0, 0)))

    def sw(nm, layer):
        return Stacked(stacked[nm], "rows" if SHARD_AXIS[nm] == 1 else "cols", layer)

    dn_args = (_row(dn_norm), w_qkv, w_z, w_ba, w_conv, gate, _row(dn_out_norm), sw("dn_w_out", 0))
    h1, n, dn_saved = dn_fwd(h0, *dn_args, next_gain=_row(xa_norm[0]))
    if rest_weights is not None:
        stacked = {**stacked, **rest_weights(h1)}
    xa_args = [(_row(xa_norm[l]), _row(xa_mem_norm[l]), sw("xa_w_q", l), sw("xa_w_kv", l), sw("xa_w_o", l))
               for l in range(2)]
    mlp_args = [(_row(mlp_norm[l]), sw("mlp_w_up", l), sw("mlp_w_down", l)) for l in range(2)]
    cv_args = (_row(full["cv_norm"][0]), sw("cv_w_pw1", 0), full["cv_b_pw1"], w_dw, full["cv_b_dw"],
               full["cv_ln_g"], full["cv_ln_b"], sw("cv_w_pw2", 0), full["cv_b_pw2"])
    h2, n, xa0_saved = xa_fwd("xa0", h1, mem0, *xa_args[0], n=n, next_gain=mlp_args[0][0])
    h3, n, mlp0_saved = mlp_fwd("mlp0", h2, *mlp_args[0], n=n, next_gain=cv_args[0])
    h4, n, cv_saved = cv_fwd(h3, *cv_args, n=n, next_gain=xa_args[1][0])
    h5, n, xa1_saved = xa_fwd("xa1", h4, mem0, *xa_args[1], n=n, next_gain=mlp_args[1][0])
    h6, _, mlp1_saved = mlp_fwd("mlp1", h5, *mlp_args[1], n=n)

    dh32, dh16, loss_tile, d_final = loss_head("loss_head", h6, _row(final_norm), target)
    dh = (dh32, dh16)
    grads = {}
    dg_mlp, dg_xa, dg_xa_mem = [None, None], [None, None], [None, None]
    dw_mlp, dw_xa = [None, None], [None, None]
    mlp_names, xa_names = ("mlp_w_up", "mlp_w_down"), ("xa_w_q", "xa_w_kv", "xa_w_o")

    def announce(items):
        return None if on_grads is None else on_grads(items)

    dh, dg_mlp[1], dw_mlp[1] = mlp_bwd("mlp1", dh, h5, *mlp_args[1], mlp1_saved)
    dh, dg_xa[1], dg_xa_mem[1], dw_xa[1] = xa_bwd("xa1", dh, h4, mem0, *xa_args[1], xa1_saved)
    (dh, grads["cv_norm"], dw_pw1, grads["cv_b_pw1"], dw_dw, ln_acc, dw_pw2,
     grads["cv_b_pw2"]) = cv_bwd(dh, h3, cv_args[0], cv_args[1], w_dw, cv_args[5], cv_args[6], cv_args[7], cv_saved)
    after = announce([(nm, 1, g) for nm, g in zip(mlp_names + xa_names, dw_mlp[1] + dw_xa[1])]
                     + [("cv_w_pw1", 0, dw_pw1), ("cv_w_pw2", 0, dw_pw2)])
    dh, dg_mlp[0], dw_mlp[0] = mlp_bwd("mlp0", dh, h2, *mlp_args[0], mlp0_saved, after=after)
    dh, dg_xa[0], dg_xa_mem[0], dw_xa[0] = xa_bwd("xa0", dh, h1, mem0, *xa_args[0], xa0_saved)
    after = announce([(nm, 0, g) for nm, g in zip(mlp_names + xa_names, dw_mlp[0] + dw_xa[0])])
    dh, dg_dn, dw_qkv, dw_z, dw_ba, dw_conv, d_gate, d_out_norm, dw_out = dn_bwd(dh, h0, *dn_args, dn_saved,
                                                                                 after=after)

    grads["dn_w_in"] = jnp.concatenate([dw_qkv, dw_z, dw_ba[:, :2 * DN_HEADS]], axis=1)[None]
    grads["dn_w_conv"] = dw_conv[None, :DN_CONV]
    grads["dn_w_out"], grads["cv_w_pw1"], grads["cv_w_pw2"] = [dw_out], [dw_pw1], [dw_pw2]
    grads["cv_w_dw"] = dw_dw[None, :CV_WIDTH]
    grads["cv_ln_g"], grads["cv_ln_b"], grads["cv_b_dw"] = ln_acc[0:1], ln_acc[1:2], ln_acc[2:3]
    for i, nm in enumerate(mlp_names):
        grads[nm] = [dw_mlp[0][i], dw_mlp[1][i]]
    for i, nm in enumerate(xa_names):
        grads[nm] = [dw_xa[0][i], dw_xa[1][i]]

    rep = jnp.zeros((16, d), F32)
    rep = rep.at[0].set(dg_dn[0])
    rep = rep.at[1, :LANES].set(d_gate[0])
    rep = rep.at[2, :LANES].set(d_gate[1])
    rep = rep.at[3, :LANES].set(d_out_norm[0])
    rep = rep.at[4].set(dg_xa[0][0]).at[5].set(dg_xa[1][0])
    rep = rep.at[6].set(dg_xa_mem[0][0]).at[7].set(dg_xa_mem[1][0])
    rep = rep.at[8].set(dg_mlp[0][0]).at[9].set(dg_mlp[1][0])
    rep = rep.at[10].set(d_final[0])
    rep = rep.at[11, :LANES].set(loss_tile[0])
    return dh, grads, rep
```

```python
import functools

import jax
import jax.numpy as jnp
from jax import lax
from jax.experimental import pallas as pl
from jax.experimental.pallas import tpu as pltpu

F32 = jnp.float32
BF16 = jnp.bfloat16
HIGHEST = lax.Precision.HIGHEST
MESH = pl.DeviceIdType.MESH

D_MODEL = 1024
DN_HEADS = 8
DN_HEAD_DIM = 128
DN_CONV = 4
DN_CHUNK = 64
CV_WIDTH = 31
XA_HEADS = 4
XA_HEAD_DIM = 256
RMS_EPS = 1e-6
LN_EPS = 1e-5
L2_EPS = 1e-6

ADAM_LR = 0.001
ADAM_B1 = 0.9
ADAM_B2 = 0.999
ADAM_EPS = 1e-08
ADAM_WD = 0.01
ADAM_STEP = 10

LANES = 128
ROW_TILE = 512
CONV_ROW_TILE = 256
MM_TILE = 1024
GRAD_TILE_K = 4096
LONG_TILE_K = 2048
ADAMW_ROW_TILE = 256
DN_ROW_TILE = 256
CHUNK_SHIFT = 6
SOLVE_INTERLEAVE = 8
FWD_HEADS_PER_STEP = 8
BWD_HEADS_PER_STEP = 8
BWD_SCAN_ROWS = 256
DN_HALO = 8
CV_HALO = 32
VMEM_LIMIT = 48 * 1024 * 1024
N_CHIPS = 4
D2D_CHUNK_ROWS = 256


def _cparams(sem):
    return pltpu.CompilerParams(dimension_semantics=sem, vmem_limit_bytes=VMEM_LIMIT)


def _dot(a, b, dims=(((1,), (0,)), ((), ()))):
    return lax.dot_general(a.astype(BF16), b.astype(BF16), dims, preferred_element_type=F32)


def _dot_nt(a, b):
    return _dot(a, b, (((1,), (1,)), ((), ())))


def _dot_tn(a, b):
    return _dot(a, b, (((0,), (0,)), ((), ())))


def _dot_hi(a, b, dims=(((1,), (0,)), ((), ()))):
    return lax.dot_general(a.astype(F32), b.astype(F32), dims, precision=HIGHEST, preferred_element_type=F32)


def _dot_x3(a, b, dims=(((1,), (0,)), ((), ()))):
    a_hi, b_hi = a.astype(BF16), b.astype(BF16)
    a_lo = (a - a_hi.astype(F32)).astype(BF16)
    b_lo = (b - b_hi.astype(F32)).astype(BF16)

    def dot(p, q):
        return lax.dot_general(p, q, dims, preferred_element_type=F32)

    return dot(a_hi, b_hi) + (dot(a_hi, b_lo) + dot(a_lo, b_hi))


def _sigmoid(x):
    return 1.0 / (1.0 + jnp.exp(-x))


def _silu(x):
    return x * _sigmoid(x)


def _silu_grad(x):
    s = _sigmoid(x)
    return s * (1.0 + x * (1.0 - s))


def _softplus(x):
    return jnp.maximum(x, 0.0) + jnp.log(1.0 + jnp.exp(-jnp.abs(x)))


def _iota(shape, dim):
    return lax.broadcasted_iota(jnp.int32, shape, dim)


def _lane_col(vals, lane, idx):
    return jnp.sum(jnp.where(lane == idx, vals, 0.0), axis=1, keepdims=True)


def _pick_tile(rows, cap):
    best = rows
    for t in range(16, min(rows, cap) + 1, 16):
        if rows % t == 0:
            best = t
    return best


def _stacked_spec(shape, split, layer, rows, cols, block_index):
    r_shard, c_shard = shape[-2], shape[-1]
    if split == "rows" and rows > r_shard:
        assert rows % r_shard == 0 and c_shard % cols == 0
        chips = rows // r_shard

        def slabs(i, j, kk):
            bi, bj = block_index(i, j, kk)
            return (bi, layer, 0, bj)

        return pl.BlockSpec((chips, None, r_shard, cols), slabs), chips
    assert r_shard % rows == 0 and c_shard % cols == 0
    per_chip = (r_shard // rows) if split == "rows" else (c_shard // cols)

    def index(i, j, kk):
        bi, bj = block_index(i, j, kk)
        if split == "rows":
            return (bi // per_chip, layer, bi % per_chip, bj)
        return (bj // per_chip, layer, bi, bj % per_chip)

    return pl.BlockSpec((None, None, rows, cols), index), 1


def mm(name, a, b, *, ta=False, tb=False, out_dtype=F32, pro=None, epi=None, epi_tiles=(), epi_rows=(),
       tm=MM_TILE, tn=MM_TILE, tk=MM_TILE, b_split=None, b_layer=None, out_split=None, out_layer=None,
       after=None, norm_gain=None, norm_bwd=None):
    m, k = (a.shape[1], a.shape[0]) if ta else a.shape
    b_rows, b_cols = b.shape[-2], b.shape[-1]
    if b_split == "rows":
        b_rows *= N_CHIPS
    elif b_split == "cols":
        b_cols *= N_CHIPS
    n = b_rows if tb else b_cols
    assert (b_cols if tb else b_rows) == k
    tm, tn, tk = min(tm, m), min(tn, n), min(tk, k)
    if b_split == "cols":
        if tb:
            tk = min(tk, b.shape[-1])
        else:
            tn = min(tn, b.shape[-1])
    if out_split == "cols":
        tn = min(tn, n // N_CHIPS)
    assert m % tm == 0 and n % tn == 0 and k % tk == 0
    nk = k // tk
    a_spec = pl.BlockSpec((tk, tm), lambda i, j, kk: (kk, i)) if ta else pl.BlockSpec((tm, tk), lambda i, j, kk: (i, kk))
    b_block = (tn, tk) if tb else (tk, tn)
    b_index = (lambda i, j, kk: (j, kk)) if tb else (lambda i, j, kk: (kk, j))
    b_chips = o_chips = 1
    if b_split is None:
        b_spec = pl.BlockSpec(b_block, b_index)
    else:
        b_spec, b_chips = _stacked_spec(b.shape, b_split, b_layer, b_block[0], b_block[1], b_index)
    in_specs = [a_spec, b_spec]
    in_specs += [pl.BlockSpec((tm, tn), lambda i, j, kk: (i, j)) for _ in epi_tiles]
    in_specs += [pl.BlockSpec((1, tn), lambda i, j, kk: (0, j)) for _ in epi_rows]
    n_t, n_r = len(epi_tiles), len(epi_rows)
    dims = (((0 if ta else 1,), (1 if tb else 0,)), ((), ()))
    if out_split is None:
        out_shape = jax.ShapeDtypeStruct((m, n), out_dtype)
        out_spec = pl.BlockSpec((tm, tn), lambda i, j, kk: (i, j))
    else:
        shard = (m // N_CHIPS, n) if out_split == "rows" else (m, n // N_CHIPS)
        out_shape = jax.ShapeDtypeStruct((N_CHIPS, out_layer[1]) + shard, out_dtype)
        out_spec, o_chips = _stacked_spec(out_shape.shape, out_split, out_layer[0], tm, tn, lambda i, j, kk: (i, j))
    single_pass = nk == 1 and norm_gain is None and norm_bwd is None and b_chips == 1 and o_chips == 1
    extra = []
    row_spec = pl.BlockSpec((1, n), lambda i, j, kk: (0, 0))
    tile_spec = pl.BlockSpec((tm, tn), lambda i, j, kk: (i, j))
    if norm_gain is not None:
        assert tn == n and out_split is None
        extra.append(norm_gain)
        in_specs.append(row_spec)
        out_shape = [out_shape, jax.ShapeDtypeStruct((m, n), BF16)]
        out_spec = [out_spec, tile_spec]
    if norm_bwd is not None:
        assert tn == n and out_split is None and norm_gain is None
        extra += list(norm_bwd)
        in_specs += [tile_spec, row_spec, tile_spec]
        out_shape = [jax.ShapeDtypeStruct((m, n), F32), jax.ShapeDtypeStruct((m, n), BF16),
                     jax.ShapeDtypeStruct((1, n), F32)]
        out_spec = [tile_spec, tile_spec, row_spec]
    if after is not None:
        extra.append(after)
        in_specs.append(pl.BlockSpec(memory_space=pl.ANY))

    def body(a_ref, b_ref, *rest):
        tiles = rest[:n_t]
        rows = rest[n_t:n_t + n_r]
        gain_ref = rest[n_t + n_r] if norm_gain is not None else None
        bwd_refs = rest[n_t + n_r:n_t + n_r + 3] if norm_bwd is not None else None
        rest = rest[n_t + n_r + len(extra):]
        o_ref, acc_ref = rest[0], rest[-1]
        av = a_ref[...]
        if pro is not None:
            av = pro(av)
        if single_pass:
            out = _dot(av, b_ref[...], dims)
            if epi is not None:
                out = epi(out, *[t[...] for t in tiles], *[r[...] for r in rows])
            o_ref[...] = out.astype(out_dtype)
            return
        kk = pl.program_id(2)

        @pl.when(kk == 0)
        def _():
            acc_ref[...] = jnp.zeros_like(acc_ref)

        bv = b_ref[...]
        if b_chips > 1:
            bv = bv.reshape(b_block)
        acc_ref[...] += _dot(av, bv, dims)

        @pl.when(kk == nk - 1)
        def _():
            out = acc_ref[...]
            if epi is not None:
                out = epi(out, *[t[...] for t in tiles], *[r[...] for r in rows])
            if gain_ref is not None:
                rest[1][...] = (_rms_stats(out)[0] * gain_ref[...]).astype(BF16)
            if bwd_refs is not None:
                h_ref, g_ref, dres_ref = bwd_refs
                dh, dg = _rms_bwd_tile(out, h_ref[...], g_ref[...])
                total = dres_ref[...] + dh
                o_ref[...] = total
                rest[1][...] = total.astype(BF16)
                first = pl.program_id(0) == 0

                @pl.when(first)
                def _():
                    rest[2][...] = dg

                @pl.when(jnp.logical_not(first))
                def _():
                    rest[2][...] += dg

                return
            out = out.astype(out_dtype)
            o_ref[...] = out.reshape(o_chips, tm // o_chips, tn) if o_chips > 1 else out

    outer = "arbitrary" if norm_bwd is not None else "parallel"
    return pl.pallas_call(
        body, name=name, grid=(m // tm, n // tn, nk),
        in_specs=in_specs, out_specs=out_spec, out_shape=out_shape,
        scratch_shapes=[] if single_pass else [pltpu.VMEM((tm, tn), F32)],
        compiler_params=_cparams((outer, outer, "arbitrary")),
    )(a, b, *epi_tiles, *epi_rows, *extra)


def row_call(name, body, n_rows, tm, ins, outs, accs=()):
    tm = _pick_tile(n_rows, tm)
    in_specs = []
    for arr, kind in ins:
        if kind == "tile":
            if arr.ndim == 2:
                in_specs.append(pl.BlockSpec((tm, arr.shape[1]), lambda i: (i, 0)))
            else:
                in_specs.append(pl.BlockSpec((arr.shape[0], tm, arr.shape[2]), lambda i: (0, i, 0)))
        elif kind == "full":
            in_specs.append(pl.BlockSpec(arr.shape, functools.partial(lambda i, nd: (0,) * nd, nd=arr.ndim)))
        else:
            where, h = kind
            per = tm // h
            if where == "prev":
                in_specs.append(pl.BlockSpec((h, arr.shape[1]), functools.partial(
                    lambda i, per: (jnp.maximum(i * per - 1, 0), 0), per=per)))
            else:
                last = n_rows // h - 1
                in_specs.append(pl.BlockSpec((h, arr.shape[1]), functools.partial(
                    lambda i, per, last: (jnp.minimum((i + 1) * per, last), 0), per=per, last=last)))
    out_shape, out_specs = [], []
    for shape, dtype in outs:
        out_shape.append(jax.ShapeDtypeStruct(shape, dtype))
        if len(shape) == 2:
            out_specs.append(pl.BlockSpec((tm, shape[1]), lambda i: (i, 0)))
        else:
            out_specs.append(pl.BlockSpec((shape[0], tm, shape[2]), lambda i: (0, i, 0)))
    for shape in accs:
        out_shape.append(jax.ShapeDtypeStruct(shape, F32))
        out_specs.append(pl.BlockSpec(shape, lambda i: (0, 0)))
    n_in, n_out, n_acc = len(ins), len(outs), len(accs)

    def kern(*refs):
        i = pl.program_id(0)
        in_refs = refs[:n_in]
        out_refs = refs[n_in:n_in + n_out]
        acc_refs = refs[n_in + n_out:n_in + n_out + n_acc]
        if n_acc:
            @pl.when(i == 0)
            def _():
                for r in acc_refs:
                    r[...] = jnp.zeros_like(r)
        body(i, in_refs, out_refs, acc_refs)

    res = pl.pallas_call(
        kern, name=name, grid=(n_rows // tm,), in_specs=in_specs, out_specs=out_specs, out_shape=out_shape,
        compiler_params=_cparams(("arbitrary",) if n_acc else ("parallel",)),
    )(*[a for a, _ in ins])
    return list(res)


def _rms_stats(h):
    r = lax.rsqrt(jnp.mean(h * h, axis=-1, keepdims=True) + RMS_EPS)
    return h * r, r


def rms_fwd(name, h, g):
    def body(i, ins, outs, accs):
        xhat, _ = _rms_stats(ins[0][...])
        outs[0][...] = (xhat * ins[1][...]).astype(BF16)

    return row_call(name, body, h.shape[0], ROW_TILE, [(h, "tile"), (g, "full")], [(h.shape, BF16)])[0]


def _rms_bwd_tile(dn, h, g):
    xhat, r = _rms_stats(h)
    dxhat = dn * g
    dh = r * (dxhat - xhat * jnp.mean(dxhat * xhat, axis=-1, keepdims=True))
    dg = jnp.sum(dn * xhat, axis=0, keepdims=True)
    return dh, dg


def mem_norm_bwd(name, dn, mem, g):
    def body(i, ins, outs, accs):
        _, dg = _rms_bwd_tile(ins[0][...].astype(F32), ins[1][...], ins[2][...])
        accs[0][...] += dg

    return row_call(name, body, mem.shape[0], ROW_TILE, [(dn, "tile"), (mem, "tile"), (g, "full")], [],
                    [(1, mem.shape[1])])[0]


def loss_head(name, h, g, target):
    d = h.shape[1]

    def body(i, ins, outs, accs):
        hv, gv = ins[0][...], ins[1][...]
        xhat, _ = _rms_stats(hv)
        err = xhat * gv - ins[2][...]
        dy = err * (1.0 / d)
        dh, dg = _rms_bwd_tile(dy, hv, gv)
        outs[0][...] = dh
        outs[1][...] = dh.astype(BF16)
        accs[0][...] += jnp.full((8, LANES), 0.5 / d, F32) * jnp.sum(err * err)
        accs[1][...] += dg

    dh, dh16, loss, dg = row_call(name, body, h.shape[0], ROW_TILE, [(h, "tile"), (g, "full"), (target, "tile")],
                                  [(h.shape, F32), (h.shape, BF16)], [(8, LANES), (1, d)])
    return dh, dh16, loss, dg


def col_sum(name, x):
    def body(i, ins, outs, accs):
        accs[0][...] += jnp.sum(ins[0][...].astype(F32), axis=0, keepdims=True)

    return row_call(name, body, x.shape[0], ROW_TILE, [(x, "tile")], [], [(1, x.shape[1])])[0]


def _conv_taps(xcat, w_ref, cols, width, halo, tm):
    rows = halo + tm
    acc = None
    for j in range(width):
        s = width - 1 - j
        xs = xcat if s == 0 else pltpu.roll(xcat, s, 0)
        term = xs[halo:rows] * w_ref[j:j + 1, cols]
        acc = term if acc is None else acc + term
    return acc


def _conv_taps_bwd_x(dcat, w_ref, cols, width, halo, tm):
    rows = halo + tm
    acc = None
    for j in range(width):
        s = width - 1 - j
        ds = dcat if s == 0 else pltpu.roll(dcat, rows - s, 0)
        term = ds[0:tm] * w_ref[j:j + 1, cols]
        acc = term if acc is None else acc + term
    return acc


def _conv_taps_bwd_w(dy, xcat, width, halo, tm, wrows):
    rows = halo + tm
    rid = _iota((wrows, dy.shape[1]), 0)
    out = jnp.zeros((wrows, dy.shape[1]), F32)
    for j in range(width):
        s = width - 1 - j
        xs = xcat if s == 0 else pltpu.roll(xcat, s, 0)
        v = jnp.sum(dy * xs[halo:rows], axis=0, keepdims=True)
        out = out + jnp.where(rid == j, v, 0.0)
    return out


def dn_pre(qkv_raw, ba, w_conv, gate):
    s_len = qkv_raw.shape[0]
    tm = min(DN_ROW_TILE, s_len)
    n_blk = qkv_raw.shape[1] // LANES

    def body(i, ins, outs, accs):
        x_ref, xp_ref, ba_ref, w_ref, gate_ref = ins
        qkv_ref, hs_ref = outs

        def blk(cb, carry):
            cols = pl.ds(pl.multiple_of(cb * LANES, LANES), LANES)
            prev = jnp.where(i > 0, xp_ref[:, cols], 0.0)
            xcat = jnp.concatenate([prev, x_ref[:, cols]], axis=0)
            c = _conv_taps(xcat, w_ref, cols, DN_CONV, DN_HALO, tm)
            y = _silu(c)
            rs = lax.rsqrt(jnp.sum(y * y, axis=-1, keepdims=True) + L2_EPS)
            fac = jnp.where(cb < DN_HEADS, DN_HEAD_DIM ** -0.5, 1.0)
            qkv_ref[:, cols] = jnp.where(cb < 2 * DN_HEADS, y * (rs * fac), y)
            return carry

        lax.fori_loop(0, n_blk, blk, 0, unroll=4)

        bav = ba_ref[...]
        beta = _sigmoid(bav)
        g = -jnp.exp(gate_ref[0:1, :]) * _softplus(bav + gate_ref[1:2, :])
        lane = _iota((tm, LANES), 1)
        g = jnp.where((lane >= DN_HEADS) & (lane < 2 * DN_HEADS), g, 0.0)
        r = _iota((tm, tm), 0)
        c = _iota((tm, tm), 1)
        tri = jnp.where((r >= c) & ((r >> CHUNK_SHIFT) == (c >> CHUNK_SHIFT)), 1.0, 0.0)
        gc = _dot_hi(tri, g)
        for h in range(DN_HEADS):
            hs_ref[h] = jnp.where(lane == 0, _lane_col(beta, lane, h),
                                  jnp.where(lane == 1, _lane_col(g, lane, DN_HEADS + h),
                                            jnp.where(lane == 2, _lane_col(gc, lane, DN_HEADS + h), 0.0)))

    return row_call("dn_pre", body, s_len, tm,
                    [(qkv_raw, "tile"), (qkv_raw, ("prev", DN_HALO)), (ba, "tile"), (w_conv, "full"), (gate, "full")],
                    [(qkv_raw.shape, F32), ((DN_HEADS, s_len, LANES), F32)])


def _chunk_masks():
    r = _iota((DN_CHUNK, DN_CHUNK), 0)
    c = _iota((DN_CHUNK, DN_CHUNK), 1)
    return r, c


def _decay_matrix(gc, r, c):
    gc_row = jnp.sum(jnp.where(r == c, gc, 0.0), axis=0, keepdims=True)
    causal = r >= c
    return jnp.where(causal, jnp.exp(jnp.where(causal, gc - gc_row, 0.0)), 0.0)


def _tri_inverse(lows, r, c):
    eye = jnp.where(r == c, 1.0, 0.0)
    ts = [eye for _ in lows]
    b = 1
    while b < DN_CHUNK:
        shift = b.bit_length()
        sel = ((r >> shift) == (c >> shift)) & ((r & b) != 0) & ((c & b) == 0)
        lms = [jnp.where(sel, low, 0.0) for low in lows]
        if b == 1:
            ts = [t - lm for t, lm in zip(ts, lms)]
        else:
            t_lm = [_dot_x3(t, lm) for t, lm in zip(ts, lms)]
            t_lm_t = [_dot_x3(x, t) for x, t in zip(t_lm, ts)]
            ts = [t - x for t, x in zip(ts, t_lm_t)]
        b *= 2
    return ts


def dn_solve(qkv, hs):
    s_len = qkv.shape[0]
    rb = min(ROW_TILE, s_len)
    n_chunk = rb // DN_CHUNK
    interleave = min(SOLVE_INTERLEAVE, n_chunk)

    def body(k_ref, v_ref, hs_ref, u_ref, w_ref, t_ref):
        r, c = _chunk_masks()

        def group(gi, carry):
            rows = [pl.ds(pl.multiple_of((gi * interleave + j) * DN_CHUNK, DN_CHUNK), DN_CHUNK)
                    for j in range(interleave)]
            k = [k_ref[rw, :] for rw in rows]
            beta = [hs_ref[rw, 0:1] for rw in rows]
            gc = [hs_ref[rw, 2:3] for rw in rows]
            kb = [a * b for a, b in zip(k, beta)]
            decay = [_decay_matrix(g, r, c) for g in gc]
            lows = [jnp.where(r > c, _dot_nt(a, b) * d, 0.0) for a, b, d in zip(kb, k, decay)]
            ts = _tri_inverse(lows, r, c)
            us = [_dot_x3(t, v_ref[rw, :] * b) for t, rw, b in zip(ts, rows, beta)]
            ws = [_dot_x3(t, a * jnp.exp(g)) for t, a, g in zip(ts, kb, gc)]
            for j, rw in enumerate(rows):
                u_ref[rw, :] = us[j]
                w_ref[rw, :] = ws[j].astype(BF16)
                t_ref[rw, :] = ts[j]
            return carry

        lax.fori_loop(0, n_chunk // interleave, group, 0)

    return pl.pallas_call(
        body, name="dn_solve", grid=(DN_HEADS, s_len // rb),
        in_specs=[pl.BlockSpec((rb, LANES), lambda h, i: (i, DN_HEADS + h)),
                  pl.BlockSpec((rb, LANES), lambda h, i: (i, 2 * DN_HEADS + h)),
                  pl.BlockSpec((None, rb, LANES), lambda h, i: (h, i, 0))],
        out_specs=[pl.BlockSpec((rb, LANES), lambda h, i: (i, h)),
                   pl.BlockSpec((rb, LANES), lambda h, i: (i, h)),
                   pl.BlockSpec((None, rb, DN_CHUNK), lambda h, i: (h, i, 0))],
        out_shape=[jax.ShapeDtypeStruct((s_len, DN_HEADS * LANES), F32),
                   jax.ShapeDtypeStruct((s_len, DN_HEADS * LANES), BF16),
                   jax.ShapeDtypeStruct((DN_HEADS, s_len, DN_CHUNK), F32)],
        compiler_params=_cparams(("parallel", "parallel")),
    )(qkv, qkv, hs)


def dn_scan_fwd(qkv, u, w, hs):
    s_len = qkv.shape[0]
    rb = min(ROW_TILE, s_len)
    n_chunk = rb // DN_CHUNK
    total_chunks = s_len // DN_CHUNK

    hps = FWD_HEADS_PER_STEP
    groups = DN_HEADS // hps

    def body(q_ref, k_ref, u_ref, w_ref, hs_ref, o_ref, st_ref, state):
        @pl.when(pl.program_id(1) == 0)
        def _():
            state[...] = jnp.zeros_like(state)

        r, c = _chunk_masks()

        def chunk(n, carry):
            rows = pl.ds(pl.multiple_of(n * DN_CHUNK, DN_CHUNK), DN_CHUNK)
            heads = range(hps)
            cols = [slice(h * LANES, (h + 1) * LANES) for h in heads]
            each = lambda f, *xs: [f(*a) for a in zip(*xs)]
            q = [q_ref[rows, cl] for cl in cols]
            k = [k_ref[rows, cl] for cl in cols]
            gc = [hs_ref[h, rows, 2:3] for h in heads]
            st = [state[h] for h in heads]
            for h in heads:
                st_ref[h, n] = st[h]
            gl = each(lambda g: jnp.min(g, axis=0, keepdims=True), gc)
            decay = each(lambda g: _decay_matrix(g, r, c), gc)
            w_st = [_dot(w_ref[rows, cols[h]], st[h]) for h in heads]
            qk = each(_dot_nt, q, k)
            q_st = each(lambda a, g, s: _dot(a * jnp.exp(g), s), q, gc, st)
            vn = [u_ref[rows, cols[h]] - w_st[h] for h in heads]
            ai_vn = each(lambda a, d, b: _dot(a * d, b), qk, decay, vn)
            kd_vn = each(lambda a, g0, g, b: _dot_tn(a * jnp.exp(g0 - g), b), k, gl, gc, vn)
            for h in heads:
                o_ref[rows, cols[h]] = q_st[h] + ai_vn[h]
                state[h] = st[h] * jnp.exp(gl[h]) + kd_vn[h]
            return carry

        lax.fori_loop(0, n_chunk, chunk, 0, unroll=2)

    wide = hps * LANES
    blk = lambda off: pl.BlockSpec((rb, wide), lambda h, i: (i, off + h))
    return pl.pallas_call(
        body, name="dn_scan_fwd", grid=(groups, s_len // rb),
        in_specs=[blk(0), blk(groups), blk(0), blk(0),
                  pl.BlockSpec((hps, rb, LANES), lambda h, i: (h, i, 0))],
        out_specs=[blk(0),
                   pl.BlockSpec((hps, n_chunk, LANES, LANES), lambda h, i: (h, i, 0, 0))],
        out_shape=[jax.ShapeDtypeStruct((s_len, DN_HEADS * LANES), F32),
                   jax.ShapeDtypeStruct((DN_HEADS, total_chunks, LANES, LANES), F32)],
        scratch_shapes=[pltpu.VMEM((hps, LANES, LANES), F32)],
        compiler_params=_cparams(("parallel", "arbitrary")),
    )(qkv, qkv, u, w, hs)


def dn_scan_bwd(qkv, u, w, t_inv, hs, states, d_o):
    s_len = qkv.shape[0]
    rb = min(BWD_SCAN_ROWS, s_len)
    n_chunk = rb // DN_CHUNK
    n_blk = s_len // rb
    hps = BWD_HEADS_PER_STEP
    groups = DN_HEADS // hps

    def body(q_ref, k_ref, v_ref, u_ref, w_ref, t_ref, hs_ref, st_ref, do_ref,
             dq_ref, dk_ref, dv_ref, dhs_ref, dstate):
        @pl.when(pl.program_id(1) == 0)
        def _():
            dstate[...] = jnp.zeros_like(dstate)

        r, c = _chunk_masks()
        causal = r >= c
        strict = r > c
        lane = _iota((DN_CHUNK, LANES), 1)
        upper = jnp.where(r <= c, 1.0, 0.0)
        last_row = _iota((DN_CHUNK, 1), 0) == DN_CHUNK - 1

        def chunk(m, carry):
            n = n_chunk - 1 - m
            rows = pl.ds(pl.multiple_of(n * DN_CHUNK, DN_CHUNK), DN_CHUNK)
            heads = range(hps)
            cols = [slice(h * LANES, (h + 1) * LANES) for h in heads]
            each = lambda f, *xs: [f(*a) for a in zip(*xs)]
            rsum = lambda x: jnp.sum(x, axis=-1, keepdims=True)
            dims_tn = (((0,), (0,)), ((), ()))
            q = [q_ref[rows, cl] for cl in cols]
            k = [k_ref[rows, cl] for cl in cols]
            v = [v_ref[rows, cl] for cl in cols]
            uu = [u_ref[rows, cl] for cl in cols]
            ww = [w_ref[rows, cl] for cl in cols]
            do = [do_ref[rows, cl] for cl in cols]
            tt = [t_ref[h, rows, :] for h in heads]
            beta = [hs_ref[h, rows, 0:1] for h in heads]
            gc = [hs_ref[h, rows, 2:3] for h in heads]
            st = [st_ref[h, n] for h in heads]
            dst = [dstate[h] for h in heads]
            gl = each(lambda g: jnp.min(g, axis=0, keepdims=True), gc)
            egc = each(jnp.exp, gc)
            egl = each(jnp.exp, gl)
            ekd = each(lambda a, b: jnp.exp(a - b), gl, gc)
            decay = each(lambda g: _decay_matrix(g, r, c), gc)
            qd = each(jnp.multiply, q, egc)
            kd = each(jnp.multiply, k, ekd)
            kb = each(jnp.multiply, k, beta)
            w_st = each(_dot, ww, st)
            qk = each(_dot_nt, q, k)
            dqd = each(_dot_nt, do, st)
            kd_dst = each(_dot, kd, dst)
            qd_do = each(_dot_tn, qd, do)
            kbk = each(_dot_nt, kb, k)
            vn = each(jnp.subtract, uu, w_st)
            ai = each(jnp.multiply, qk, decay)
            low = each(lambda a, d: jnp.where(strict, a * d, 0.0), kbk, decay)
            dai = each(lambda a, b: jnp.where(causal, _dot_nt(a, b), 0.0), do, vn)
            ai_do = each(_dot_tn, ai, do)
            dkd = each(_dot_nt, vn, dst)
            dvn = each(jnp.add, ai_do, kd_dst)
            dp = each(jnp.multiply, dai, decay)
            dw = each(lambda a, b: -_dot_nt(a, b), dvn, st)
            w_dvn = each(_dot_tn, ww, dvn)
            dp_k = each(_dot, dp, k)
            dp_q = each(_dot_tn, dp, q)
            drhs_u = each(lambda a, b: _dot_x3(a, b, dims_tn), tt, dvn)
            dgl = each(lambda a, b, e: jnp.sum(a * b) * e, dst, st, egl)
            for h in heads:
                dstate[h] = dst[h] * egl[h] + qd_do[h] - w_dvn[h]
            dq = each(lambda a, e, b: a * e + b, dqd, egc, dp_k)
            dk_a = each(lambda a, e, b: a * e + b, dkd, ekd, dp_q)
            rkd = each(lambda a, b: rsum(a * b), dkd, kd)
            drhs_w = each(lambda a, b: _dot_x3(a, b, dims_tn), tt, dw)
            dl_u = each(_dot_nt, drhs_u, uu)
            dl_w = each(_dot_nt, drhs_w, ww)
            dlow = each(lambda a, b: jnp.where(strict, -(a + b), 0.0), dl_u, dl_w)
            dqm = each(jnp.multiply, dlow, decay)
            m_tot = each(lambda a, b, d, e: a * b + d * e, dai, ai, dlow, low)
            dqm_k = each(_dot, dqm, k)
            dk_l = each(_dot_tn, dqm, kb)
            col_rows = each(lambda m: jnp.sum(m, axis=0, keepdims=True), m_tot)
            col_sums = each(lambda rw: jnp.sum(jnp.where(r == c, rw, 0.0), axis=1, keepdims=True), col_rows)
            dkb_w = each(jnp.multiply, drhs_w, egc)
            dkb = each(jnp.add, dkb_w, dqm_k)
            dgc = [rsum(dqd[h] * qd[h]) - rkd[h] + jnp.where(last_row, jnp.sum(rkd[h]) + dgl[h], 0.0)
                   + rsum(m_tot[h]) + rsum(dkb_w[h] * kb[h]) for h in heads]
            dg = each(lambda a, b: _dot_hi(upper, jnp.where(lane == 1, a - b, 0.0)), dgc, col_sums)
            for h in heads:
                dq_ref[rows, cols[h]] = dq[h]
                dk_ref[rows, cols[h]] = dk_a[h] + dk_l[h] + dkb[h] * beta[h]
                dv_ref[rows, cols[h]] = drhs_u[h] * beta[h]
                dbeta = rsum(drhs_u[h] * v[h]) + rsum(dkb[h] * k[h])
                dhs_ref[h, rows, :] = jnp.where(lane == 0, dbeta, dg[h])
            return carry

        lax.fori_loop(0, n_chunk, chunk, 0, unroll=2)

    wide = hps * LANES
    blk = lambda off: pl.BlockSpec((rb, wide), lambda h, i: (n_blk - 1 - i, off + h))
    head = blk(0)
    hs_spec = pl.BlockSpec((hps, rb, LANES), lambda h, i: (h, n_blk - 1 - i, 0))
    full = jax.ShapeDtypeStruct((s_len, DN_HEADS * LANES), F32)
    return pl.pallas_call(
        body, name="dn_scan_bwd", grid=(groups, n_blk),
        in_specs=[blk(0), blk(groups), blk(2 * groups), head, head,
                  pl.BlockSpec((hps, rb, DN_CHUNK), lambda h, i: (h, n_blk - 1 - i, 0)), hs_spec,
                  pl.BlockSpec((hps, n_chunk, LANES, LANES), lambda h, i: (h, n_blk - 1 - i, 0, 0)), head],
        out_specs=[head, head, head, hs_spec],
        out_shape=[full, full, full, jax.ShapeDtypeStruct((DN_HEADS, s_len, LANES), F32)],
        scratch_shapes=[pltpu.VMEM((hps, LANES, LANES), F32)],
        compiler_params=_cparams(("parallel", "arbitrary")),
    )(qkv, qkv, qkv, u, w, t_inv, hs, states, d_o)


def dn_post(o, z, out_norm):
    def body(i, ins, outs, accs):
        gn = ins[2][...]
        for h in range(DN_HEADS):
            cols = slice(h * LANES, (h + 1) * LANES)
            xhat, _ = _rms_stats(ins[0][:, cols])
            outs[0][:, cols] = (xhat * gn * _silu(ins[1][:, cols])).astype(BF16)

    return row_call("dn_post", body, o.shape[0], ROW_TILE, [(o, "tile"), (z, "tile"), (out_norm, "full")],
                    [(o.shape, BF16)])[0]


def dn_post_bwd(d_og, o, z, out_norm):
    def body(i, ins, outs, accs):
        gn = ins[3][...]
        dgn = jnp.zeros((1, LANES), F32)
        for h in range(DN_HEADS):
            cols = slice(h * LANES, (h + 1) * LANES)
            dy, zh = ins[0][:, cols].astype(F32), ins[2][:, cols]
            xhat, r = _rms_stats(ins[1][:, cols])
            sz = _silu(zh)
            dgn = dgn + jnp.sum(dy * xhat * sz, axis=0, keepdims=True)
            outs[1][:, cols] = (dy * xhat * gn * _silu_grad(zh)).astype(BF16)
            dxhat = dy * gn * sz
            outs[0][:, cols] = r * (dxhat - xhat * jnp.mean(dxhat * xhat, axis=-1, keepdims=True))
        accs[0][...] += dgn

    return row_call("dn_post_bwd", body, o.shape[0], ROW_TILE,
                    [(d_og, "tile"), (o, "tile"), (z, "tile"), (out_norm, "full")],
                    [(o.shape, F32), (o.shape, BF16)], [(1, LANES)])


def dn_pre_bwd(dq, dk, dv, dhs, qkv_raw, ba, w_conv, gate):
    s_len = qkv_raw.shape[0]
    tm = min(DN_ROW_TILE, s_len)

    def body(i, ins, outs, accs):
        dq_ref, dk_ref, dv_ref, dhs_ref, x_ref, xp_ref, ba_ref, w_ref, gate_ref = ins
        dc_ref, dba_ref = outs

        def blk(cb, carry):
            cols = pl.ds(pl.multiple_of(cb * LANES, LANES), LANES)
            hcols = pl.ds(pl.multiple_of((cb & (DN_HEADS - 1)) * LANES, LANES), LANES)
            prev = jnp.where(i > 0, xp_ref[:, cols], 0.0)
            xcat = jnp.concatenate([prev, x_ref[:, cols]], axis=0)
            c = _conv_taps(xcat, w_ref, cols, DN_CONV, DN_HALO, tm)
            y = _silu(c)
            dy = jnp.where(cb < DN_HEADS, dq_ref[:, hcols],
                           jnp.where(cb < 2 * DN_HEADS, dk_ref[:, hcols], dv_ref[:, hcols]))
            rs = lax.rsqrt(jnp.sum(y * y, axis=-1, keepdims=True) + L2_EPS)
            fac = jnp.where(cb < DN_HEADS, DN_HEAD_DIM ** -0.5, 1.0)
            nrm = y * rs
            dn = dy * fac
            dy_norm = rs * (dn - nrm * jnp.sum(dn * nrm, axis=-1, keepdims=True))
            dc_ref[:, cols] = jnp.where(cb < 2 * DN_HEADS, dy_norm, dy) * _silu_grad(c)
            return carry

        lax.fori_loop(0, qkv_raw.shape[1] // LANES, blk, 0, unroll=4)

        lane = _iota((tm, LANES), 1)
        dbeta = jnp.zeros((tm, LANES), F32)
        dg = jnp.zeros((tm, LANES), F32)
        for h in range(DN_HEADS):
            dbeta = dbeta + jnp.where(lane == h, dhs_ref[h, :, 0:1], 0.0)
            dg = dg + jnp.where(lane == DN_HEADS + h, dhs_ref[h, :, 1:2], 0.0)
        bav = ba_ref[...]
        beta = _sigmoid(bav)
        ea = jnp.exp(gate_ref[0:1, :])
        pre = bav + gate_ref[1:2, :]
        g = -ea * _softplus(pre)
        da = dg * (-ea) * _sigmoid(pre)
        dba_ref[...] = (dbeta * beta * (1.0 - beta) + da).astype(BF16)
        rid = _iota((8, LANES), 0)
        accs[0][...] += (jnp.where(rid == 0, jnp.sum(dg * g, axis=0, keepdims=True), 0.0)
                         + jnp.where(rid == 1, jnp.sum(da, axis=0, keepdims=True), 0.0))

    return row_call("dn_pre_bwd", body, s_len, tm,
                    [(dq, "tile"), (dk, "tile"), (dv, "tile"), (dhs, "tile"), (qkv_raw, "tile"),
                     (qkv_raw, ("prev", DN_HALO)), (ba, "tile"), (w_conv, "full"), (gate, "full")],
                    [(qkv_raw.shape, F32), (ba.shape, BF16)], [(8, LANES)])


def dn_conv_bwd(dc, qkv_raw, w_conv):
    s_len = dc.shape[0]
    tm = min(DN_ROW_TILE, s_len)
    nt = s_len // tm

    def body(i, ins, outs, accs):
        dc_ref, dn_ref, x_ref, xp_ref, w_ref = ins

        def blk(cb, carry):
            cols = pl.ds(pl.multiple_of(cb * LANES, LANES), LANES)
            dy = dc_ref[:, cols]
            nxt = jnp.where(i < nt - 1, dn_ref[:, cols], 0.0)
            dcat = jnp.concatenate([dy, nxt], axis=0)
            outs[0][:, cols] = _conv_taps_bwd_x(dcat, w_ref, cols, DN_CONV, DN_HALO, tm).astype(BF16)
            prev = jnp.where(i > 0, xp_ref[:, cols], 0.0)
            xcat = jnp.concatenate([prev, x_ref[:, cols]], axis=0)
            accs[0][:, cols] += _conv_taps_bwd_w(dy, xcat, DN_CONV, DN_HALO, tm, 8)
            return carry

        lax.fori_loop(0, dc.shape[1] // LANES, blk, 0)

    return row_call("dn_conv_bwd", body, s_len, tm,
                    [(dc, "tile"), (dc, ("next", DN_HALO)), (qkv_raw, "tile"), (qkv_raw, ("prev", DN_HALO)),
                     (w_conv, "full")],
                    [(dc.shape, BF16)], [(8, dc.shape[1])])


def _glu(u_ref, cols, d):
    return u_ref[:, cols] * _sigmoid(u_ref[:, pl.ds(pl.multiple_of(d + cols.start, LANES), cols.size)])


def cv_core_fwd(u, w_dw, b_dw, ln_g, ln_b):
    s_len, d = u.shape[0], u.shape[1] // 2
    tm = min(CONV_ROW_TILE, s_len)

    def body(i, ins, outs, accs):
        u_ref, up_ref, w_ref, bdw_ref, g_ref, b_ref = ins
        s_ref, c_ref = outs

        def blk(cb, carry):
            cols = pl.ds(pl.multiple_of(cb * LANES, LANES), LANES)
            prev = jnp.where(i > 0, _glu(up_ref, cols, d), 0.0)
            xcat = jnp.concatenate([prev, _glu(u_ref, cols, d)], axis=0)
            c_ref[:, cols] = _conv_taps(xcat, w_ref, cols, CV_WIDTH, CV_HALO, tm) + bdw_ref[:, cols]
            return carry

        lax.fori_loop(0, d // LANES, blk, 0)
        c = c_ref[...]
        mu = jnp.mean(c, axis=-1, keepdims=True)
        xc = c - mu
        rstd = lax.rsqrt(jnp.mean(xc * xc, axis=-1, keepdims=True) + LN_EPS)
        s_ref[...] = _silu(xc * rstd * g_ref[...] + b_ref[...]).astype(BF16)

    return row_call("cv_core_fwd", body, s_len, tm,
                    [(u, "tile"), (u, ("prev", CV_HALO)), (w_dw, "full"), (b_dw, "full"), (ln_g, "full"),
                     (ln_b, "full")],
                    [((s_len, d), BF16), ((s_len, d), F32)])


def cv_ln_bwd(ds, c, ln_g, ln_b):
    def body(i, ins, outs, accs):
        cv, g = ins[1][...], ins[2][...]
        mu = jnp.mean(cv, axis=-1, keepdims=True)
        xc = cv - mu
        rstd = lax.rsqrt(jnp.mean(xc * xc, axis=-1, keepdims=True) + LN_EPS)
        xhat = xc * rstd
        dl = ins[0][...].astype(F32) * _silu_grad(xhat * g + ins[3][...])
        dxhat = dl * g
        dc = rstd * (dxhat - jnp.mean(dxhat, axis=-1, keepdims=True)
                     - xhat * jnp.mean(dxhat * xhat, axis=-1, keepdims=True))
        outs[0][...] = dc
        rid = _iota((8, cv.shape[1]), 0)
        accs[0][...] += (jnp.where(rid == 0, jnp.sum(dl * xhat, axis=0, keepdims=True), 0.0)
                         + jnp.where(rid == 1, jnp.sum(dl, axis=0, keepdims=True), 0.0)
                         + jnp.where(rid == 2, jnp.sum(dc, axis=0, keepdims=True), 0.0))

    return row_call("cv_ln_bwd", body, c.shape[0], ROW_TILE,
                    [(ds, "tile"), (c, "tile"), (ln_g, "full"), (ln_b, "full")], [(c.shape, F32)], [(8, c.shape[1])])


def cv_conv_bwd(dc, u, w_dw):
    s_len, d = dc.shape
    tm = min(CONV_ROW_TILE, s_len)
    nt = s_len // tm

    def body(i, ins, outs, accs):
        dc_ref, dn_ref, u_ref, up_ref, w_ref = ins

        def blk(cb, carry):
            cols = pl.ds(pl.multiple_of(cb * LANES, LANES), LANES)
            gcols = pl.ds(pl.multiple_of(d + cb * LANES, LANES), LANES)
            dy = dc_ref[:, cols]
            nxt = jnp.where(i < nt - 1, dn_ref[:, cols], 0.0)
            dgl = _conv_taps_bwd_x(jnp.concatenate([dy, nxt], axis=0), w_ref, cols, CV_WIDTH, CV_HALO, tm)
            u1, sg = u_ref[:, cols], _sigmoid(u_ref[:, gcols])
            du1 = dgl * sg
            du2 = dgl * u1 * sg * (1.0 - sg)
            outs[0][:, cols] = du1.astype(BF16)
            outs[0][:, gcols] = du2.astype(BF16)
            accs[1][:, cols] += jnp.sum(du1, axis=0, keepdims=True)
            accs[1][:, gcols] += jnp.sum(du2, axis=0, keepdims=True)
            prev = jnp.where(i > 0, _glu(up_ref, cols, d), 0.0)
            xcat = jnp.concatenate([prev, u1 * sg], axis=0)
            accs[0][:, cols] += _conv_taps_bwd_w(dy, xcat, CV_WIDTH, CV_HALO, tm, CV_HALO)
            return carry

        lax.fori_loop(0, d // LANES, blk, 0)

    return row_call("cv_conv_bwd", body, s_len, tm,
                    [(dc, "tile"), (dc, ("next", CV_HALO)), (u, "tile"), (u, ("prev", CV_HALO)), (w_dw, "full")],
                    [(u.shape, BF16)], [(CV_HALO, d), (1, 2 * d)])


def xa_core_fwd(name, q, kv):
    d = q.shape[1]

    def body(i, ins, outs, accs):
        for h in range(XA_HEADS):
            cols = slice(h * XA_HEAD_DIM, (h + 1) * XA_HEAD_DIM)
            vcols = slice(d + h * XA_HEAD_DIM, d + (h + 1) * XA_HEAD_DIM)
            s = _dot_nt(ins[0][:, cols], ins[1][:, cols]) * (XA_HEAD_DIM ** -0.5)
            e = jnp.exp(s - jnp.max(s, axis=-1, keepdims=True))
            p = e / jnp.sum(e, axis=-1, keepdims=True)
            outs[0][:, cols] = _dot(p, ins[1][:, vcols]).astype(BF16)

    return row_call(name, body, q.shape[0], ROW_TILE, [(q, "tile"), (kv, "full")], [(q.shape, BF16)])[0]


def xa_core_bwd(name, d_o, q, kv):
    d = q.shape[1]

    def body(i, ins, outs, accs):
        for h in range(XA_HEADS):
            cols = slice(h * XA_HEAD_DIM, (h + 1) * XA_HEAD_DIM)
            vcols = slice(d + h * XA_HEAD_DIM, d + (h + 1) * XA_HEAD_DIM)
            qh, kh, vh, doh = ins[1][:, cols], ins[2][:, cols], ins[2][:, vcols], ins[0][:, cols]
            s = _dot_nt(qh, kh) * (XA_HEAD_DIM ** -0.5)
            e = jnp.exp(s - jnp.max(s, axis=-1, keepdims=True))
            p = e / jnp.sum(e, axis=-1, keepdims=True)
            dp = _dot_nt(doh, vh)
            ds = p * (dp - jnp.sum(dp * p, axis=-1, keepdims=True)) * (XA_HEAD_DIM ** -0.5)
            outs[0][:, cols] = _dot(ds, kh).astype(BF16)
            accs[0][:, cols] += _dot_tn(ds, qh)
            accs[0][:, vcols] += _dot_tn(p, doh)

    return row_call(name, body, q.shape[0], ROW_TILE, [(d_o, "tile"), (q, "tile"), (kv, "full")],
                    [(q.shape, BF16)], [kv.shape])


def adamw(name, w, g, m, v):
    def body(i, ins, outs, accs):
        wv, gv = ins[0][...], ins[1][...]
        mn = ADAM_B1 * ins[2][...] + (1.0 - ADAM_B1) * gv
        vn = ADAM_B2 * ins[3][...] + (1.0 - ADAM_B2) * jnp.square(gv)
        m_hat = mn / (1.0 - ADAM_B1 ** ADAM_STEP)
        v_hat = vn / (1.0 - ADAM_B2 ** ADAM_STEP)
        outs[0][...] = -ADAM_LR * (m_hat / (jnp.sqrt(v_hat) + ADAM_EPS) + ADAM_WD * wv)
        outs[1][...] = mn
        outs[2][...] = vn

    return row_call(name, body, w.shape[0], ROW_TILE, [(w, "tile"), (g, "tile"), (m, "tile"), (v, "tile")],
                    [(w.shape, F32)] * 3)


def adamw_halves(name, w, g_mine, g_sibling, m, v, core):
    n_layers = len(g_mine)
    rows, cols = w.shape
    half_rows = rows // n_layers // 2
    tm = _pick_tile(half_rows, ADAMW_ROW_TILE)
    per_half = half_rows // tm

    def body(core_ref, w_ref, *rest):
        g_refs = rest[:2 * n_layers]
        m_ref, v_ref, g_out, d_out, m_out, v_out = rest[2 * n_layers:]
        i = pl.program_id(0)
        mine = ((i // per_half) % 2) == core_ref[0]
        layer = i // (2 * per_half)
        gv = jnp.where(mine, g_refs[0][...], g_refs[n_layers][...])
        for l in range(1, n_layers):
            gv = jnp.where(layer == l, jnp.where(mine, g_refs[l][...], g_refs[n_layers + l][...]), gv)
        mn = ADAM_B1 * m_ref[...] + (1.0 - ADAM_B1) * gv
        vn = ADAM_B2 * v_ref[...] + (1.0 - ADAM_B2) * jnp.square(gv)
        m_hat = mn / (1.0 - ADAM_B1 ** ADAM_STEP)
        v_hat = vn / (1.0 - ADAM_B2 ** ADAM_STEP)
        g_out[...] = gv
        d_out[...] = -ADAM_LR * (m_hat / (jnp.sqrt(v_hat) + ADAM_EPS) + ADAM_WD * w_ref[...])
        m_out[...] = mn
        v_out[...] = vn

    whole = pl.BlockSpec((tm, cols), lambda i, core_ref: (i, 0))

    def half(layer, own):
        def index(i, core_ref):
            used = (i // (2 * per_half) == layer) & ((((i // per_half) % 2) == core_ref[0]) == own)
            return (jnp.where(used, i % per_half, 0), 0)

        return pl.BlockSpec((tm, cols), index)

    halves = [half(l, True) for l in range(n_layers)] + [half(l, False) for l in range(n_layers)]
    return pl.pallas_call(
        body, name=name,
        grid_spec=pltpu.PrefetchScalarGridSpec(
            num_scalar_prefetch=1, grid=(2 * per_half * n_layers,),
            in_specs=[whole] + halves + [whole, whole], out_specs=[whole] * 4),
        out_shape=[jax.ShapeDtypeStruct(w.shape, F32)] * 4,
        compiler_params=_cparams(("parallel",)),
    )(core, w, *g_mine, *g_sibling, m, v)


HBM_SPEC = pl.BlockSpec(memory_space=pltpu.HBM)


def _position():
    return lax.axis_index("x"), lax.axis_index("y"), lax.axis_index("c")


def _other_chips(x, y):
    return [(1 - x, y), (x, 1 - y), (1 - x, 1 - y)]


def _row_chunks(rows):
    return rows // D2D_CHUNK_ROWS if rows % D2D_CHUNK_ROWS == 0 else 1


def _start_chunked(make, rows):
    k = _row_chunks(rows)
    for i in range(k):
        make(i * (rows // k), rows // k).start()


def gather_shards(packs):
    n = len(packs)

    def body(*refs):
        srcs, outs = refs[:n], refs[n:2 * n]
        send_sems, recv_sems = refs[2 * n:]
        x, y, c = _position()
        me = 2 * x + y
        chips = _other_chips(x, y)
        sibling = (x, y, 1 - c)

        def over_ici(a, j):
            px, py = chips[j]
            rows = srcs[a].shape[0] // 2
            return pltpu.make_async_remote_copy(
                src_ref=srcs[a].at[pl.ds(c * rows, rows), :], dst_ref=outs[a].at[me, pl.ds(c * rows, rows), :],
                send_sem=send_sems.at[a, j], recv_sem=recv_sems.at[a, j], device_id=(px, py, c), device_id_type=MESH)

        def landed(a, j):
            px, py = chips[j]
            rows = srcs[a].shape[0] // 2
            part = outs[a].at[2 * px + py, pl.ds(c * rows, rows), :]
            return pltpu.make_async_remote_copy(
                src_ref=part, dst_ref=part, send_sem=send_sems.at[a, j], recv_sem=recv_sems.at[a, j],
                device_id=(px, py, c), device_id_type=MESH)

        def over_d2d(a, j, cc, off, size):
            px, py = chips[j]
            rows = srcs[a].shape[0] // 2
            part = outs[a].at[2 * px + py, pl.ds(cc * rows + off, size), :]
            return pltpu.make_async_remote_copy(
                src_ref=part, dst_ref=part, send_sem=send_sems.at[a, 3 + j], recv_sem=recv_sems.at[a, 3 + j],
                device_id=sibling, device_id_type=MESH)

        for a in range(n):
            for j in range(3):
                over_ici(a, j).start()
        for a in range(n):
            for j in range(3):
                landed(a, j).wait_recv()
                _start_chunked(functools.partial(over_d2d, a, j, c), srcs[a].shape[0] // 2)
        for a in range(n):
            rows = srcs[a].shape[0] // 2
            for j in range(3):
                over_d2d(a, j, 1 - c, 0, rows).wait_recv()
                over_d2d(a, j, c, 0, rows).wait_send()
                over_ici(a, j).wait_send()

    return pl.pallas_call(
        body, name="gather_shards",
        in_specs=[HBM_SPEC] * n, out_specs=[HBM_SPEC] * n,
        out_shape=[jax.ShapeDtypeStruct((N_CHIPS,) + p.shape, p.dtype) for p in packs],
        scratch_shapes=[pltpu.SemaphoreType.DMA((n, 6)), pltpu.SemaphoreType.DMA((n, 6))],
    )(*packs)


def pair_split(name, packs):
    n = len(packs)

    def body(*refs):
        srcs, outs = refs[:n], refs[n:2 * n]
        send_sems, recv_sems = refs[2 * n:]
        x, y, c = _position()

        def remote(a, off, size):
            rows = srcs[a].shape[1] // 2
            return pltpu.make_async_remote_copy(
                src_ref=srcs[a].at[:, pl.ds((1 - c) * rows + off, size), :],
                dst_ref=outs[a].at[:, pl.ds(off, size), :],
                send_sem=send_sems.at[a], recv_sem=recv_sems.at[a], device_id=(x, y, 1 - c), device_id_type=MESH)

        for a in range(n):
            _start_chunked(functools.partial(remote, a), srcs[a].shape[1] // 2)
        for a in range(n):
            remote(a, 0, srcs[a].shape[1] // 2).wait()

    return pl.pallas_call(
        body, name=name, in_specs=[HBM_SPEC] * n, out_specs=[HBM_SPEC] * n,
        out_shape=[jax.ShapeDtypeStruct((p.shape[0], p.shape[1] // 2, p.shape[2]), p.dtype) for p in packs],
        scratch_shapes=[pltpu.SemaphoreType.DMA((n,)), pltpu.SemaphoreType.DMA((n,))],
    )(*packs)


def pair_join(name, halves):
    n = len(halves)

    def body(*refs):
        srcs, outs = refs[:n], refs[n:2 * n]
        send_sems, recv_sems = refs[2 * n:]
        x, y, c = _position()

        def remote(a, off, size):
            return pltpu.make_async_remote_copy(
                src_ref=srcs[a].at[pl.ds(off, size), :], dst_ref=outs[a].at[pl.ds(off, size), :],
                send_sem=send_sems.at[a], recv_sem=recv_sems.at[a], device_id=(x, y, 1 - c), device_id_type=MESH)

        for a in range(n):
            _start_chunked(functools.partial(remote, a), srcs[a].shape[0])
        for a in range(n):
            remote(a, 0, srcs[a].shape[0]).wait()

    return pl.pallas_call(
        body, name=name, in_specs=[HBM_SPEC] * n, out_specs=[HBM_SPEC] * n,
        out_shape=[jax.ShapeDtypeStruct(p.shape, p.dtype) for p in halves],
        scratch_shapes=[pltpu.SemaphoreType.DMA((n,)), pltpu.SemaphoreType.DMA((n,))],
    )(*halves)


SEM_SPEC = pl.BlockSpec(memory_space=pltpu.SEMAPHORE)
DATAFLOW = pltpu.SideEffectType.DATAFLOW_SIDE_EFFECTING


def _ici_copy(kind, srcs, lands, send_sems, recv_sems, a, j):
    x, y, c = _position()
    px, py = _other_chips(x, y)[j]
    if kind == "gather":
        rows = srcs[a].shape[0] // 2
        src = srcs[a].at[pl.ds(c * rows, rows), :]
        dst = lands[a].at[2 * x + y, pl.ds(c * rows, rows), :]
    else:
        src = srcs[a].at[2 * px + py]
        dst = lands[a].at[j]
    return pltpu.make_async_remote_copy(src_ref=src, dst_ref=dst, send_sem=send_sems, recv_sem=recv_sems,
                                        device_id=(px, py, c), device_id_type=MESH)


def ici_start(name, kind, srcs, land_shapes):
    n = len(srcs)
    lands = [pltpu.with_memory_space_constraint(lax.empty(shp, s.dtype), pltpu.HBM) for shp, s in zip(land_shapes, srcs)]

    def body(*refs):
        src_refs, land_refs = refs[:n], refs[n:2 * n]
        send_sems, recv_sems = refs[2 * n], refs[2 * n + 1]
        token = refs[-1]
        for a in range(n):
            for j in range(N_CHIPS - 1):
                _ici_copy(kind, src_refs, land_refs, send_sems, recv_sems, a, j).start()
        token[...] = jnp.zeros_like(token)

    sems = pltpu.SemaphoreType.DMA(())
    res = pl.pallas_call(
        body, name=name,
        out_shape=[sems, sems] + [pltpu.HBM(s.shape, s.dtype) for s in srcs]
        + [pltpu.HBM(l.shape, l.dtype) for l in lands] + [jax.ShapeDtypeStruct((8, LANES), F32)],
        in_specs=[HBM_SPEC] * (2 * n),
        out_specs=[SEM_SPEC, SEM_SPEC] + [HBM_SPEC] * (2 * n) + [pl.BlockSpec(memory_space=pltpu.VMEM)],
        input_output_aliases={i: 2 + i for i in range(2 * n)},
        compiler_params=pltpu.CompilerParams(has_side_effects=DATAFLOW),
    )(*[pltpu.with_memory_space_constraint(s, pltpu.HBM) for s in srcs], *lands)
    return res[0], res[1], list(res[2:2 + n]), list(res[2 + n:2 + 2 * n]), res[-1]


def ici_wait(name, kind, send_sems, recv_sems, srcs, lands, after):
    n = len(srcs)

    def body(*refs):
        src_refs, land_refs = refs[:n], refs[n:2 * n]
        send, recv = refs[2 * n], refs[2 * n + 1]
        for a in range(n):
            for j in range(N_CHIPS - 1):
                cp = _ici_copy(kind, src_refs, land_refs, send, recv, a, j)
                cp.wait_send()
                cp.wait_recv()

    res = pl.pallas_call(
        body, name=name,
        out_shape=[pltpu.HBM(s.shape, s.dtype) for s in srcs] + [pltpu.HBM(l.shape, l.dtype) for l in lands],
        in_specs=[HBM_SPEC] * (2 * n) + [SEM_SPEC, SEM_SPEC, pl.BlockSpec(memory_space=pl.ANY)],
        out_specs=[HBM_SPEC] * (2 * n),
        input_output_aliases={i: i for i in range(2 * n)},
        compiler_params=pltpu.CompilerParams(has_side_effects=DATAFLOW),
    )(*srcs, *lands, send_sems, recv_sems, after)
    return list(res[:n]), list(res[n:])


def pair_forward(gathered):
    n = len(gathered)

    def body(*refs):
        outs = refs[n:2 * n]
        send_sems, recv_sems = refs[2 * n:]
        x, y, c = _position()
        chips = _other_chips(x, y)

        def part(a, j, cc, off, size):
            px, py = chips[j]
            rows = outs[a].shape[1] // 2
            ref = outs[a].at[2 * px + py, pl.ds(cc * rows + off, size), :]
            return pltpu.make_async_remote_copy(
                src_ref=ref, dst_ref=ref, send_sem=send_sems.at[a, j], recv_sem=recv_sems.at[a, j],
                device_id=(x, y, 1 - c), device_id_type=MESH)

        for a in range(n):
            for j in range(N_CHIPS - 1):
                _start_chunked(functools.partial(part, a, j, c), outs[a].shape[1] // 2)
        for a in range(n):
            rows = outs[a].shape[1] // 2
            for j in range(N_CHIPS - 1):
                part(a, j, 1 - c, 0, rows).wait_recv()
                part(a, j, c, 0, rows).wait_send()

    return pl.pallas_call(
        body, name="pair_forward", in_specs=[HBM_SPEC] * n, out_specs=[HBM_SPEC] * n,
        out_shape=[jax.ShapeDtypeStruct(g.shape, g.dtype) for g in gathered],
        input_output_aliases={i: i for i in range(n)},
        scratch_shapes=[pltpu.SemaphoreType.DMA((n, N_CHIPS - 1)), pltpu.SemaphoreType.DMA((n, N_CHIPS - 1))],
    )(*gathered)


def all_sum_small(part):
    n_dev = 8
    rows = part.shape[0]

    def body(src, out, buf, send_sems, recv_sems):
        x, y, c = _position()
        me = 4 * x + 2 * y + c
        buf[me] = src[...]
        copies = []
        for k in range(1, n_dev):
            px, py, pc = x ^ ((k >> 2) & 1), y ^ ((k >> 1) & 1), c ^ (k & 1)
            cp = pltpu.make_async_remote_copy(
                src_ref=src, dst_ref=buf.at[me], send_sem=send_sems.at[k - 1], recv_sem=recv_sems.at[k - 1],
                device_id=(px, py, pc), device_id_type=MESH)
            cp.start()
            copies.append(cp)
        for cp in copies:
            cp.wait()
        acc = buf[0]
        for k in range(1, n_dev):
            acc = acc + buf[k]
        out[...] = acc

    return pl.pallas_call(
        body, name="all_sum_small",
        in_specs=[pl.BlockSpec(memory_space=pltpu.VMEM)], out_specs=pl.BlockSpec(memory_space=pltpu.VMEM),
        out_shape=jax.ShapeDtypeStruct(part.shape, F32),
        scratch_shapes=[pltpu.VMEM((n_dev, rows, part.shape[1]), F32),
                        pltpu.SemaphoreType.DMA((n_dev - 1,)), pltpu.SemaphoreType.DMA((n_dev - 1,))],
    )(part)


def add_pairs(name, src, theirs, core, out_dtype):
    slabs, rows, cols = theirs.shape
    tm = _pick_tile(rows, ROW_TILE)
    nb = rows // tm

    def body(core_ref, a_ref, b_ref, o_ref):
        o_ref[...] = (a_ref[...].astype(F32) + b_ref[...].astype(F32)).astype(out_dtype)

    return pl.pallas_call(
        body, name=name,
        grid_spec=pltpu.PrefetchScalarGridSpec(
            num_scalar_prefetch=1, grid=(slabs, nb),
            in_specs=[pl.BlockSpec((None, tm, cols), lambda s, i, core_ref: (s, core_ref[0] * nb + i, 0)),
                      pl.BlockSpec((None, tm, cols), lambda s, i, core_ref: (s, i, 0))],
            out_specs=pl.BlockSpec((None, tm, cols), lambda s, i, core_ref: (s, i, 0))),
        out_shape=jax.ShapeDtypeStruct(theirs.shape, out_dtype),
        compiler_params=_cparams(("parallel", "parallel")),
    )(core, src, theirs)


def add_four(name, src, theirs, chip):
    _, rows, cols = theirs.shape
    tm = _pick_tile(rows, ROW_TILE)

    def body(chip_ref, a_ref, b_ref, o_ref):
        acc = a_ref[...].astype(F32)
        for j in range(N_CHIPS - 1):
            acc = acc + b_ref[j].astype(F32)
        o_ref[...] = acc

    return pl.pallas_call(
        body, name=name,
        grid_spec=pltpu.PrefetchScalarGridSpec(
            num_scalar_prefetch=1, grid=(rows // tm,),
            in_specs=[pl.BlockSpec((None, tm, cols), lambda i, chip_ref: (chip_ref[0], i, 0)),
                      pl.BlockSpec((N_CHIPS - 1, tm, cols), lambda i, chip_ref: (0, i, 0))],
            out_specs=pl.BlockSpec((tm, cols), lambda i, chip_ref: (i, 0))),
        out_shape=jax.ShapeDtypeStruct((rows, cols), F32),
        compiler_params=_cparams(("parallel",)),
    )(chip, src, theirs)


PACK_COLS = 1024
SMALL_ROW_MULTIPLE = 32
BIG = ["dn_w_in", "dn_w_out", "cv_w_pw1", "cv_w_pw2", "xa_w_q", "xa_w_kv", "xa_w_o", "mlp_w_up", "mlp_w_down"]
SMALL = ["dn_w_conv", "cv_norm", "cv_b_pw1", "cv_w_dw", "cv_b_dw", "cv_ln_g", "cv_ln_b", "cv_b_pw2"]
SHARD_AXIS = {"dn_w_in": 2, "dn_w_conv": 2, "dn_w_out": 1, "cv_norm": 1, "cv_w_pw1": 2, "cv_b_pw1": 1,
              "cv_w_dw": 2, "cv_b_dw": 1, "cv_ln_g": 1, "cv_ln_b": 1, "cv_w_pw2": 1, "cv_b_pw2": 1,
              "xa_w_q": 1, "xa_w_kv": 2, "xa_w_o": 1, "mlp_w_up": 2, "mlp_w_down": 1}
REPLICATED = ["dn_norm", "dn_a_log", "dn_dt_bias", "dn_out_norm", "xa_norm", "xa_mem_norm", "mlp_norm", "final_norm"]


def _pack_rows(size):
    return -(-size // PACK_COLS)


SHARD_SHAPES = {
    "dn_w_in": (1, 1024, 1028), "dn_w_conv": (1, 4, 768), "dn_w_out": (1, 256, 1024), "cv_norm": (1, 256),
    "cv_w_pw1": (1, 1024, 512), "cv_b_pw1": (1, 512), "cv_w_dw": (1, 31, 256), "cv_b_dw": (1, 256),
    "cv_ln_g": (1, 256), "cv_ln_b": (1, 256), "cv_w_pw2": (1, 256, 1024), "cv_b_pw2": (1, 256),
    "xa_w_q": (2, 256, 1024), "xa_w_kv": (2, 1024, 512), "xa_w_o": (2, 256, 1024),
    "mlp_w_up": (2, 1024, 1024), "mlp_w_down": (2, 1024, 1024)}


def _shard_shape(nm):
    return SHARD_SHAPES[nm]


def _pack(tensors, names, dtype, row_multiple):
    pieces = []
    for nm in names:
        t = tensors[nm]
        flat = t.reshape(t.shape[0], -1) if t.ndim > len(_shard_shape(nm)) else t.reshape(1, -1)
        pad = _pack_rows(flat.shape[1]) * PACK_COLS - flat.shape[1]
        pieces.append(jnp.pad(flat.astype(dtype), ((0, 0), (0, pad))))
    cat = jnp.concatenate(pieces, axis=1)
    rows = cat.shape[1] // PACK_COLS
    total = -(-rows // row_multiple) * row_multiple
    cat = jnp.pad(cat, ((0, 0), (0, (total - rows) * PACK_COLS)))
    return cat.reshape(cat.shape[0], total, PACK_COLS)


def _unpack(pack, names):
    lead = pack.shape[:-2]
    flat = pack.reshape(lead + (-1,))
    out, off = {}, 0
    for nm in names:
        shp = _shard_shape(nm)
        size = 1
        for s in shp:
            size *= s
        out[nm] = flat[..., off:off + size].reshape(lead + shp)
        off += _pack_rows(size) * PACK_COLS
    return out


def _to_full(nm, stacked):
    ax = SHARD_AXIS[nm]
    moved = jnp.moveaxis(stacked, 0, ax)
    shp = list(_shard_shape(nm))
    shp[ax] *= N_CHIPS
    return moved.reshape(shp)


def _to_shards(nm, full):
    ax = SHARD_AXIS[nm]
    shp = list(_shard_shape(nm))
    split = full.reshape(shp[:ax] + [N_CHIPS, shp[ax]] + shp[ax + 1:])
    return jnp.moveaxis(split, ax, 0)


def _row(v):
    return v.reshape(1, -1)


class Stacked:
    def __init__(self, arr, split, layer):
        self.arr, self.kw = arr, dict(b_split=split, b_layer=layer)


def _grad_out(split):
    return dict(out_dtype=BF16, out_split=split, out_layer=(0, 1))


def _with_next(res, next_gain):
    return (res[0], res[1]) if next_gain is not None else (res, None)


def mlp_fwd(tag, h, g, w_up, w_down, n=None, next_gain=None):
    if n is None:
        n = rms_fwd(tag + "_norm", h, g)
    act = mm(tag + "_up", n, w_up.arr, out_dtype=BF16, epi=lambda acc: jnp.square(jnp.maximum(acc, 0.0)), **w_up.kw)
    out, n_next = _with_next(mm(tag + "_down", act, w_down.arr, tk=LONG_TILE_K, epi=lambda acc, res: acc + res,
                                epi_tiles=(h,), norm_gain=next_gain, **w_down.kw), next_gain)
    return out, n_next, (n, act)


def mlp_bwd(tag, dh, h, g, w_up, w_down, saved, after=None):
    n, act = saved
    dh, dh16 = dh
    dup = mm(tag + "_d_act", dh16, w_down.arr, tb=True, out_dtype=BF16, after=after,
             epi=lambda acc, t: acc * (2.0 * jnp.sqrt(t.astype(F32))), epi_tiles=(act,), **w_down.kw)
    dw_down = mm(tag + "_dw_down", act, dh16, ta=True, tk=GRAD_TILE_K, **_grad_out("rows"))
    dh_in, dh16_in, dg = mm(tag + "_dn", dup, w_up.arr, tb=True, norm_bwd=(h, g, dh), **w_up.kw)
    dw_up = mm(tag + "_dw_up", n, dup, ta=True, tk=GRAD_TILE_K, **_grad_out("cols"))
    return (dh_in, dh16_in), dg, (dw_up, dw_down)


def xa_fwd(tag, h, mem, g, g_mem, w_q, w_kv, w_o, n=None, next_gain=None):
    if n is None:
        n = rms_fwd(tag + "_norm", h, g)
    mem_n = rms_fwd(tag + "_mem_norm", mem, g_mem)
    q = mm(tag + "_q", n, w_q.arr, out_dtype=BF16, **w_q.kw)
    kv = mm(tag + "_kv", mem_n, w_kv.arr, out_dtype=BF16, **w_kv.kw)
    o = xa_core_fwd(tag + "_core", q, kv)
    out, n_next = _with_next(mm(tag + "_o", o, w_o.arr, epi=lambda acc, res: acc + res, epi_tiles=(h,),
                                norm_gain=next_gain, **w_o.kw), next_gain)
    return out, n_next, (n, mem_n, q, kv, o)


def xa_bwd(tag, dh, h, mem, g, g_mem, w_q, w_kv, w_o, saved):
    n, mem_n, q, kv, o = saved
    dh, dh16 = dh
    d_o = mm(tag + "_d_o", dh16, w_o.arr, tb=True, out_dtype=BF16, **w_o.kw)
    dw_o = mm(tag + "_dw_o", o, dh16, ta=True, tk=GRAD_TILE_K, **_grad_out("rows"))
    dq, dkv = xa_core_bwd(tag + "_core_bwd", d_o, q, kv)
    dh_in, dh16_in, dg = mm(tag + "_dn", dq, w_q.arr, tb=True, norm_bwd=(h, g, dh), **w_q.kw)
    dw_q = mm(tag + "_dw_q", n, dq, ta=True, tk=GRAD_TILE_K, **_grad_out("rows"))
    dw_kv = mm(tag + "_dw_kv", mem_n, dkv, ta=True, **_grad_out("cols"))
    dmem_n = mm(tag + "_dmem", dkv, w_kv.arr, tb=True, **w_kv.kw)
    dg_mem = mem_norm_bwd(tag + "_mem_norm_bwd", dmem_n, mem, g_mem)
    return (dh_in, dh16_in), dg, dg_mem, (dw_q, dw_kv, dw_o)


def _gate_tile(a_log, dt_bias):
    t = jnp.zeros((8, LANES), F32)
    t = t.at[0, DN_HEADS:2 * DN_HEADS].set(a_log.reshape(-1))
    return t.at[1, DN_HEADS:2 * DN_HEADS].set(dt_bias.reshape(-1))


def dn_fwd(h, g, w_qkv, w_z, w_ba, w_conv, gate, out_norm, w_out, next_gain=None):
    n = rms_fwd("dn_norm", h, g)
    qkv_raw = mm("dn_proj_qkv", n, w_qkv)
    z = mm("dn_proj_z", n, w_z)
    ba = mm("dn_proj_ba", n, w_ba)
    qkv, hs = dn_pre(qkv_raw, ba, w_conv, gate)
    u, w, t_inv = dn_solve(qkv, hs)
    o, states = dn_scan_fwd(qkv, u, w, hs)
    og = dn_post(o, z, out_norm)
    out, n_next = _with_next(mm("dn_out", og, w_out.arr, epi=lambda acc, res: acc + res, epi_tiles=(h,),
                                norm_gain=next_gain, **w_out.kw), next_gain)
    return out, n_next, (n, qkv_raw, z, ba, qkv, hs, u, w, t_inv, o, states, og)


def dn_bwd(dh, h, g, w_qkv, w_z, w_ba, w_conv, gate, out_norm, w_out, saved, after=None):
    n, qkv_raw, z, ba, qkv, hs, u, w, t_inv, o, states, og = saved
    dh, dh16 = dh
    d_og = mm("dn_d_og", dh16, w_out.arr, tb=True, out_dtype=BF16, after=after, **w_out.kw)
    dw_out = mm("dn_dw_out", og, dh16, ta=True, tk=GRAD_TILE_K, **_grad_out("rows"))
    d_o, dz, d_out_norm = dn_post_bwd(d_og, o, z, out_norm)
    dq, dk, dv, dhs = dn_scan_bwd(qkv, u, w, t_inv, hs, states, d_o)
    dc, dba, d_gate = dn_pre_bwd(dq, dk, dv, dhs, qkv_raw, ba, w_conv, gate)
    dqkv_raw, dw_conv = dn_conv_bwd(dc, qkv_raw, w_conv)
    dn = mm("dn_dn_qkv", dqkv_raw, w_qkv, tb=True, tk=w_qkv.shape[1])
    dn = mm("dn_dn_z", dz, w_z, tb=True, epi=lambda acc, t: acc + t, epi_tiles=(dn,))
    dh_in, _, dg = mm("dn_dn_ba", dba, w_ba, tb=True, epi=lambda acc, t: acc + t, epi_tiles=(dn,),
                      norm_bwd=(h, g, dh))
    dw_qkv = mm("dn_dw_qkv", n, dqkv_raw, ta=True, tk=GRAD_TILE_K)
    dw_z = mm("dn_dw_z", n, dz, ta=True, tk=GRAD_TILE_K)
    dw_ba = mm("dn_dw_ba", n, dba, ta=True, tk=GRAD_TILE_K)
    return dh_in, dg, dw_qkv, dw_z, dw_ba, dw_conv, d_gate, d_out_norm, dw_out


def cv_fwd(h, g, w_pw1, b_pw1, w_dw, b_dw, ln_g, ln_b, w_pw2, b_pw2, n=None, next_gain=None):
    if n is None:
        n = rms_fwd("cv_norm", h, g)
    u = mm("cv_pw1", n, w_pw1.arr, epi=lambda acc, b: acc + b, epi_rows=(b_pw1,), **w_pw1.kw)
    s, c = cv_core_fwd(u, w_dw, b_dw, ln_g, ln_b)
    out, n_next = _with_next(mm("cv_pw2", s, w_pw2.arr, epi=lambda acc, res, b: acc + res + b, epi_tiles=(h,),
                                epi_rows=(b_pw2,), norm_gain=next_gain, **w_pw2.kw), next_gain)
    return out, n_next, (n, u, s, c)


def cv_bwd(dh, h, g, w_pw1, w_dw, ln_g, ln_b, w_pw2, saved):
    n, u, s, c = saved
    dh, dh16 = dh
    ds = mm("cv_d_s", dh16, w_pw2.arr, tb=True, out_dtype=BF16, **w_pw2.kw)
    dw_pw2 = mm("cv_dw_pw2", s, dh16, ta=True, tk=GRAD_TILE_K, **_grad_out("rows"))
    db_pw2 = col_sum("cv_db_pw2", dh)
    dc, ln_acc = cv_ln_bwd(ds, c, ln_g, ln_b)
    du, dw_dw, db_pw1 = cv_conv_bwd(dc, u, w_dw)
    dh_in, dh16_in, dg = mm("cv_dn", du, w_pw1.arr, tb=True, norm_bwd=(h, g, dh), **w_pw1.kw)
    dw_pw1 = mm("cv_dw_pw1", n, du, ta=True, tk=GRAD_TILE_K, **_grad_out("cols"))
    return (dh_in, dh16_in), dg, dw_pw1, db_pw1, dw_dw, ln_acc, dw_pw2, db_pw2


WEIGHTS = ["dn_norm", "dn_w_in", "dn_w_conv", "dn_a_log", "dn_dt_bias", "dn_out_norm", "dn_w_out", "cv_norm",
           "cv_w_pw1", "cv_b_pw1", "cv_w_dw", "cv_b_dw", "cv_ln_g", "cv_ln_b", "cv_w_pw2", "cv_b_pw2", "xa_norm",
           "xa_mem_norm", "xa_w_q", "xa_w_kv", "xa_w_o", "mlp_norm", "mlp_w_up", "mlp_w_down", "final_norm"]


def _as_2d(t):
    if t.ndim == 1:
        return t.reshape(1, -1)
    return t.reshape(-1, t.shape[-1])


def kernel(x, mem, dn_norm, dn_w_in, dn_w_conv, dn_a_log, dn_dt_bias, dn_out_norm, dn_w_out, cv_norm, cv_w_pw1, cv_b_pw1, cv_w_dw, cv_b_dw, cv_ln_g, cv_ln_b, cv_w_pw2, cv_b_pw2, xa_norm, xa_mem_norm, xa_w_q, xa_w_kv, xa_w_o, mlp_norm, mlp_w_up, mlp_w_down, final_norm, loss_target, m_dn_norm, m_dn_w_in, m_dn_w_conv, m_dn_a_log, m_dn_dt_bias, m_dn_out_norm, m_dn_w_out, m_cv_norm, m_cv_w_pw1, m_cv_b_pw1, m_cv_w_dw, m_cv_b_dw, m_cv_ln_g, m_cv_ln_b, m_cv_w_pw2, m_cv_b_pw2, m_xa_norm, m_xa_mem_norm, m_xa_w_q, m_xa_w_kv, m_xa_w_o, m_mlp_norm, m_mlp_w_up, m_mlp_w_down, m_final_norm, v_dn_norm, v_dn_w_in, v_dn_w_conv, v_dn_a_log, v_dn_dt_bias, v_dn_out_norm, v_dn_w_out, v_cv_norm, v_cv_w_pw1, v_cv_b_pw1, v_cv_w_dw, v_cv_b_dw, v_cv_ln_g, v_cv_ln_b, v_cv_w_pw2, v_cv_b_pw2, v_xa_norm, v_xa_mem_norm, v_xa_w_q, v_xa_w_kv, v_xa_w_o, v_mlp_norm, v_mlp_w_up, v_mlp_w_down, v_final_norm):
    args = dict(locals())
    wts = {nm: args[nm] for nm in WEIGHTS}
    mom = {nm: args["m_" + nm] for nm in WEIGHTS}
    var = {nm: args["v_" + nm] for nm in WEIGHTS}
    core = lax.axis_index("c").astype(jnp.int32).reshape(1)
    chip = (2 * lax.axis_index("x") + lax.axis_index("y")).astype(jnp.int32)
    def own_slab(got, src):
        return lax.dynamic_update_slice(got, src[None], (chip, 0, 0))

    shard2d = {nm: wts[nm].astype(BF16).reshape(-1, wts[nm].shape[-1]) for nm in BIG}
    first = ["dn_w_in", "dn_w_out"]
    later = [nm for nm in BIG if nm not in first]
    sources = [shard2d[nm] for nm in first] + [_pack(wts, SMALL, F32, SMALL_ROW_MULTIPLE)[0]]
    gathered = [own_slab(got, src) for got, src in zip(gather_shards(sources), sources)]
    stacked = {"dn_w_out": gathered[1].reshape((N_CHIPS,) + SHARD_SHAPES["dn_w_out"])}
    full = {nm: _to_full(nm, t) for nm, t in _unpack(gathered[2], SMALL).items()}
    full["dn_w_in"] = _to_full("dn_w_in", gathered[0].reshape((N_CHIPS,) + SHARD_SHAPES["dn_w_in"]))
    full.update({nm: wts[nm] for nm in REPLICATED})
    later_src = [shard2d[nm] for nm in later]
    g_send, g_recv, later_src, g_lands, started = ici_start(
        "gather_start", "gather", later_src, [(N_CHIPS,) + s.shape for s in later_src])
    full["dn_norm"] = full["dn_norm"] + started[0, 0]

    def rest_weights(after):
        srcs, lands = ici_wait("gather_wait", "gather", g_send, g_recv, later_src, g_lands, after)
        return {nm: own_slab(land, src).reshape((N_CHIPS,) + SHARD_SHAPES[nm])
                for nm, land, src in zip(later, pair_forward(lands), srcs)}

    pending = []

    def on_grads(items):
        tag = "_".join(sorted({str(layer) for _, layer, _ in items}))
        parts = [g.reshape(N_CHIPS, -1, g.shape[-1]) for _, _, g in items]
        theirs = pair_split("pair_split_" + tag, parts)
        pairs = [add_pairs("pair_add_%s%d" % (nm, layer), p, t, core, BF16)
                 for (nm, layer, _), p, t in zip(items, parts, theirs)]
        send, recv, pairs, lands, token = ici_start(
            "scatter_start_" + tag, "scatter", pairs, [(N_CHIPS - 1,) + p.shape[1:] for p in pairs])
        pending.append((tag, items, send, recv, pairs, lands))
        return token

    dh, grads, rep = local_step(x[0], mem[0], loss_target[0], stacked, full, rest_weights, on_grads)

    halves = {}
    last = [("dn_w_in", 0, _to_shards("dn_w_in", grads["dn_w_in"]).astype(BF16)), ("dn_w_out", 0, grads["dn_w_out"][0]),
            ("small", 0, _pack({nm: _to_shards(nm, grads[nm]) for nm in SMALL}, SMALL, F32, SMALL_ROW_MULTIPLE))]
    parts = [g.reshape(N_CHIPS, -1, g.shape[-1]) for _, _, g in last]
    theirs = pair_split("pair_split_last", parts)
    pairs = [add_pairs("pair_add_" + nm, p, t, core, p.dtype) for (nm, _, _), p, t in zip(last, parts, theirs)]
    l_send, l_recv, l_pairs, l_lands, l_started = ici_start(
        "scatter_start_last", "scatter", pairs, [(N_CHIPS - 1,) + p.shape[1:] for p in pairs])
    for tag, items, send, recv, pairs, lands in pending:
        pairs, lands = ici_wait("scatter_wait_" + tag, "scatter", send, recv, pairs, lands, l_started)
        for (nm, layer, _), p, o in zip(items, pairs, lands):
            halves[nm, layer] = add_four("chip_add_%s%d" % (nm, layer), p, o, chip.reshape(1))
    keys = sorted(halves)
    siblings = dict(zip(keys, pair_join("pair_join_early", [halves[k] for k in keys])))

    delta, new_m, new_v, red = {}, {}, {}, {}

    def big_adamw(nm):
        layers = range(wts[nm].shape[0])
        res = adamw_halves("adamw_" + nm, _as_2d(wts[nm]), [halves[nm, l] for l in layers],
                           [siblings[nm, l] for l in layers], _as_2d(mom[nm]), _as_2d(var[nm]), core)
        red[nm], delta[nm], new_m[nm], new_v[nm] = (r.reshape(wts[nm].shape) for r in res)

    early = [nm for nm in BIG if (nm, 0) in halves]
    for nm in early:
        big_adamw(nm)
    done = jnp.concatenate([new_v[nm].reshape(-1)[:1] for nm in early])
    l_pairs, l_lands = ici_wait("scatter_wait_last", "scatter", l_send, l_recv, l_pairs, l_lands, done)
    for (nm, layer, _), p, o in zip(last, l_pairs, l_lands):
        halves[nm, layer] = add_four("chip_add_" + nm, p, o, chip.reshape(1))
    keys = [(nm, layer) for nm, layer, _ in last]
    siblings.update(zip(keys, pair_join("pair_join_last", [halves[k] for k in keys])))
    south = core[0] == 0
    mine, theirs = halves["small", 0], siblings["small", 0]
    red.update(_unpack(jnp.concatenate([jnp.where(south, mine, theirs), jnp.where(south, theirs, mine)], axis=0),
                       SMALL))

    rep = all_sum_small(rep)
    red["dn_norm"] = rep[0:1]
    red["dn_a_log"] = rep[1:2, DN_HEADS:2 * DN_HEADS]
    red["dn_dt_bias"] = rep[2:3, DN_HEADS:2 * DN_HEADS]
    red["dn_out_norm"] = rep[3:4, :LANES]
    red["xa_norm"], red["xa_mem_norm"], red["mlp_norm"] = rep[4:6], rep[6:8], rep[8:10]
    red["final_norm"] = rep[10]
    loss = rep[11, 0]

    for nm in WEIGHTS:
        shp = wts[nm].shape
        if nm in early:
            continue
        if nm in BIG:
            big_adamw(nm)
            continue
        res = adamw("adamw_" + nm, _as_2d(wts[nm]), _as_2d(red[nm].reshape(shp)), _as_2d(mom[nm]), _as_2d(var[nm]))
        delta[nm], new_m[nm], new_v[nm] = (r.reshape(shp) for r in res)
        red[nm] = red[nm].reshape(shp)

    grad_x = dh[None]
    return (loss, grad_x, *[red[nm] for nm in WEIGHTS], *[delta[nm] for nm in WEIGHTS],
            *[new_m[nm] for nm in WEIGHTS], *[new_v[nm] for nm in WEIGHTS])


def local_step(h0, mem0, target, stacked, full, rest_weights=None, on_grads=None):
    d = h0.shape[1]
    dn_norm, dn_a_log, dn_dt_bias, dn_out_norm = (full[nm] for nm in REPLICATED[:4])
    xa_norm, xa_mem_norm, mlp_norm, final_norm = (full[nm] for nm in REPLICATED[4:])
    inner = DN_HEADS * DN_HEAD_DIM
    w_in = full["dn_w_in"][0]
    w_qkv, w_z = w_in[:, :3 * inner], w_in[:, 3 * inner:4 * inner]
    w_ba = jnp.pad(w_in[:, 4 * inner:], ((0, 0), (0, LANES - 2 * DN_HEADS)))
    w_conv = jnp.pad(full["dn_w_conv"][0], ((0, 8 - DN_CONV), (0, 0)))
    gate = _gate_tile(dn_a_log, dn_dt_bias)
    w_dw = jnp.pad(full["cv_w_dw"][0], ((0, CV_HALO - CV_WIDTH), (0, 0)))

    def sw(nm, layer):
        return Stacked(stacked[nm], "rows" if SHARD_AXIS[nm] == 1 else "cols", layer)

    dn_args = (_row(dn_norm), w_qkv, w_z, w_ba, w_conv, gate, _row(dn_out_norm), sw("dn_w_out", 0))
    h1, n, dn_saved = dn_fwd(h0, *dn_args, next_gain=_row(xa_norm[0]))
    if rest_weights is not None:
        stacked = {**stacked, **rest_weights(h1)}
    xa_args = [(_row(xa_norm[l]), _row(xa_mem_norm[l]), sw("xa_w_q", l), sw("xa_w_kv", l), sw("xa_w_o", l))
               for l in range(2)]
    mlp_args = [(_row(mlp_norm[l]), sw("mlp_w_up", l), sw("mlp_w_down", l)) for l in range(2)]
    cv_args = (_row(full["cv_norm"][0]), sw("cv_w_pw1", 0), full["cv_b_pw1"], w_dw, full["cv_b_dw"],
               full["cv_ln_g"], full["cv_ln_b"], sw("cv_w_pw2", 0), full["cv_b_pw2"])
    h2, n, xa0_saved = xa_fwd("xa0", h1, mem0, *xa_args[0], n=n, next_gain=mlp_args[0][0])
    h3, n, mlp0_saved = mlp_fwd("mlp0", h2, *mlp_args[0], n=n, next_gain=cv_args[0])
    h4, n, cv_saved = cv_fwd(h3, *cv_args, n=n, next_gain=xa_args[1][0])
    h5, n, xa1_saved = xa_fwd("xa1", h4, mem0, *xa_args[1], n=n, next_gain=mlp_args[1][0])
    h6, _, mlp1_saved = mlp_fwd("mlp1", h5, *mlp_args[1], n=n)

    dh32, dh16, loss_tile, d_final = loss_head("loss_head", h6, _row(final_norm), target)
    dh = (dh32, dh16)
    grads = {}
    dg_mlp, dg_xa, dg_xa_mem = [None, None], [None, None], [None, None]
    dw_mlp, dw_xa = [None, None], [None, None]
    mlp_names, xa_names = ("mlp_w_up", "mlp_w_down"), ("xa_w_q", "xa_w_kv", "xa_w_o")

    def announce(items):
        return None if on_grads is None else on_grads(items)

    dh, dg_mlp[1], dw_mlp[1] = mlp_bwd("mlp1", dh, h5, *mlp_args[1], mlp1_saved)
    dh, dg_xa[1], dg_xa_mem[1], dw_xa[1] = xa_bwd("xa1", dh, h4, mem0, *xa_args[1], xa1_saved)
    (dh, grads["cv_norm"], dw_pw1, grads["cv_b_pw1"], dw_dw, ln_acc, dw_pw2,
     grads["cv_b_pw2"]) = cv_bwd(dh, h3, cv_args[0], cv_args[1], w_dw, cv_args[5], cv_args[6], cv_args[7], cv_saved)
    after = announce([(nm, 1, g) for nm, g in zip(mlp_names + xa_names, dw_mlp[1] + dw_xa[1])]
                     + [("cv_w_pw1", 0, dw_pw1), ("cv_w_pw2", 0, dw_pw2)])
    dh, dg_mlp[0], dw_mlp[0] = mlp_bwd("mlp0", dh, h2, *mlp_args[0], mlp0_saved, after=after)
    dh, dg_xa[0], dg_xa_mem[0], dw_xa[0] = xa_bwd("xa0", dh, h1, mem0, *xa_args[0], xa0_saved)
    after = announce([(nm, 0, g) for nm, g in zip(mlp_names + xa_names, dw_mlp[0] + dw_xa[0])])
    dh, dg_dn, dw_qkv, dw_z, dw_ba, dw_conv, d_gate, d_out_norm, dw_out = dn_bwd(dh, h0, *dn_args, dn_saved,
                                                                                 after=after)

    grads["dn_w_in"] = jnp.concatenate([dw_qkv, dw_z, dw_ba[:, :2 * DN_HEADS]], axis=1)[None]
    grads["dn_w_conv"] = dw_conv[None, :DN_CONV]
    grads["dn_w_out"], grads["cv_w_pw1"], grads["cv_w_pw2"] = [dw_out], [dw_pw1], [dw_pw2]
    grads["cv_w_dw"] = dw_dw[None, :CV_WIDTH]
    grads["cv_ln_g"], grads["cv_ln_b"], grads["cv_b_dw"] = ln_acc[0:1], ln_acc[1:2], ln_acc[2:3]
    for i, nm in enumerate(mlp_names):
        grads[nm] = [dw_mlp[0][i], dw_mlp[1][i]]
    for i, nm in enumerate(xa_names):
        grads[nm] = [dw_xa[0][i], dw_xa[1][i]]

    rep = jnp.zeros((16, d), F32)
    rep = rep.at[0].set(dg_dn[0])
    rep = rep.at[1, :LANES].set(d_gate[0])
    rep = rep.at[2, :LANES].set(d_gate[1])
    rep = rep.at[3, :LANES].set(d_out_norm[0])
    rep = rep.at[4].set(dg_xa[0][0]).at[5].set(dg_xa[1][0])
    rep = rep.at[6].set(dg_xa_mem[0][0]).at[7].set(dg_xa_mem[1][0])
    rep = rep.at[8].set(dg_mlp[0][0]).at[9].set(dg_mlp[1][0])
    rep = rep.at[10].set(d_final[0])
    rep = rep.at[11, :LANES].set(loss_tile[0])
    return dh, grads, rep
```

```python
import functools

import jax
import jax.numpy as jnp
from jax import lax
from jax.experimental import pallas as pl
from jax.experimental.pallas import tpu as pltpu

F32 = jnp.float32
BF16 = jnp.bfloat16
HIGHEST = lax.Precision.HIGHEST
MESH = pl.DeviceIdType.MESH

D_MODEL = 1024
DN_HEADS = 8
DN_HEAD_DIM = 128
DN_CONV = 4
DN_CHUNK = 64
CV_WIDTH = 31
XA_HEADS = 4
XA_HEAD_DIM = 256
RMS_EPS = 1e-6
LN_EPS = 1e-5
L2_EPS = 1e-6

ADAM_LR = 0.001
ADAM_B1 = 0.9
ADAM_B2 = 0.999
ADAM_EPS = 1e-08
ADAM_WD = 0.01
ADAM_STEP = 10

LANES = 128
ROW_TILE = 512
CONV_ROW_TILE = 256
MM_TILE = 1024
GRAD_TILE_K = 4096
LONG_TILE_K = 2048
ADAMW_ROW_TILE = 256
DN_ROW_TILE = 256
CHUNK_SHIFT = 6
SOLVE_INTERLEAVE = 8
FWD_HEADS_PER_STEP = 8
BWD_HEADS_PER_STEP = 8
BWD_SCAN_ROWS = 256
DN_HALO = 8
CV_HALO = 32
VMEM_LIMIT = 48 * 1024 * 1024
N_CHIPS = 4
D2D_CHUNK_ROWS = 256


def _cparams(sem):
    return pltpu.CompilerParams(dimension_semantics=sem, vmem_limit_bytes=VMEM_LIMIT)


def _dot(a, b, dims=(((1,), (0,)), ((), ()))):
    return lax.dot_general(a.astype(BF16), b.astype(BF16), dims, preferred_element_type=F32)


def _dot_nt(a, b):
    return _dot(a, b, (((1,), (1,)), ((), ())))


def _dot_tn(a, b):
    return _dot(a, b, (((0,), (0,)), ((), ())))


def _dot_hi(a, b, dims=(((1,), (0,)), ((), ()))):
    return lax.dot_general(a.astype(F32), b.astype(F32), dims, precision=HIGHEST, preferred_element_type=F32)


def _dot_x3(a, b, dims=(((1,), (0,)), ((), ()))):
    a_hi, b_hi = a.astype(BF16), b.astype(BF16)
    a_lo = (a - a_hi.astype(F32)).astype(BF16)
    b_lo = (b - b_hi.astype(F32)).astype(BF16)

    def dot(p, q):
        return lax.dot_general(p, q, dims, preferred_element_type=F32)

    return dot(a_hi, b_hi) + (dot(a_hi, b_lo) + dot(a_lo, b_hi))


def _sigmoid(x):
    return 1.0 / (1.0 + jnp.exp(-x))


def _silu(x):
    return x * _sigmoid(x)


def _silu_grad(x):
    s = _sigmoid(x)
    return s * (1.0 + x * (1.0 - s))


def _softplus(x):
    return jnp.maximum(x, 0.0) + jnp.log(1.0 + jnp.exp(-jnp.abs(x)))


def _iota(shape, dim):
    return lax.broadcasted_iota(jnp.int32, shape, dim)


def _lane_col(vals, lane, idx):
    return jnp.sum(jnp.where(lane == idx, vals, 0.0), axis=1, keepdims=True)


def _pick_tile(rows, cap):
    best = rows
    for t in range(16, min(rows, cap) + 1, 16):
        if rows % t == 0:
            best = t
    return best


def _stacked_spec(shape, split, layer, rows, cols, block_index):
    r_shard, c_shard = shape[-2], shape[-1]
    if split == "rows" and rows > r_shard:
        assert rows % r_shard == 0 and c_shard % cols == 0
        chips = rows // r_shard

        def slabs(i, j, kk):
            bi, bj = block_index(i, j, kk)
            return (bi, layer, 0, bj)

        return pl.BlockSpec((chips, None, r_shard, cols), slabs), chips
    assert r_shard % rows == 0 and c_shard % cols == 0
    per_chip = (r_shard // rows) if split == "rows" else (c_shard // cols)

    def index(i, j, kk):
        bi, bj = block_index(i, j, kk)
        if split == "rows":
            return (bi // per_chip, layer, bi % per_chip, bj)
        return (bj // per_chip, layer, bi, bj % per_chip)

    return pl.BlockSpec((None, None, rows, cols), index), 1


def mm(name, a, b, *, ta=False, tb=False, out_dtype=F32, pro=None, epi=None, epi_tiles=(), epi_rows=(),
       tm=MM_TILE, tn=MM_TILE, tk=MM_TILE, b_split=None, b_layer=None, out_split=None, out_layer=None,
       after=None, norm_gain=None, norm_bwd=None):
    m, k = (a.shape[1], a.shape[0]) if ta else a.shape
    b_rows, b_cols = b.shape[-2], b.shape[-1]
    if b_split == "rows":
        b_rows *= N_CHIPS
    elif b_split == "cols":
        b_cols *= N_CHIPS
    n = b_rows if tb else b_cols
    assert (b_cols if tb else b_rows) == k
    tm, tn, tk = min(tm, m), min(tn, n), min(tk, k)
    if b_split == "cols":
        if tb:
            tk = min(tk, b.shape[-1])
        else:
            tn = min(tn, b.shape[-1])
    if out_split == "cols":
        tn = min(tn, n // N_CHIPS)
    assert m % tm == 0 and n % tn == 0 and k % tk == 0
    nk = k // tk
    a_spec = pl.BlockSpec((tk, tm), lambda i, j, kk: (kk, i)) if ta else pl.BlockSpec((tm, tk), lambda i, j, kk: (i, kk))
    b_block = (tn, tk) if tb else (tk, tn)
    b_index = (lambda i, j, kk: (j, kk)) if tb else (lambda i, j, kk: (kk, j))
    b_chips = o_chips = 1
    if b_split is None:
        b_spec = pl.BlockSpec(b_block, b_index)
    else:
        b_spec, b_chips = _stacked_spec(b.shape, b_split, b_layer, b_block[0], b_block[1], b_index)
    in_specs = [a_spec, b_spec]
    in_specs += [pl.BlockSpec((tm, tn), lambda i, j, kk: (i, j)) for _ in epi_tiles]
    in_specs += [pl.BlockSpec((1, tn), lambda i, j, kk: (0, j)) for _ in epi_rows]
    n_t, n_r = len(epi_tiles), len(epi_rows)
    dims = (((0 if ta else 1,), (1 if tb else 0,)), ((), ()))
    if out_split is None:
        out_shape = jax.ShapeDtypeStruct((m, n), out_dtype)
        out_spec = pl.BlockSpec((tm, tn), lambda i, j, kk: (i, j))
    else:
        shard = (m // N_CHIPS, n) if out_split == "rows" else (m, n // N_CHIPS)
        out_shape = jax.ShapeDtypeStruct((N_CHIPS, out_layer[1]) + shard, out_dtype)
        out_spec, o_chips = _stacked_spec(out_shape.shape, out_split, out_layer[0], tm, tn, lambda i, j, kk: (i, j))
    single_pass = nk == 1 and norm_gain is None and norm_bwd is None and b_chips == 1 and o_chips == 1
    extra = []
    row_spec = pl.BlockSpec((1, n), lambda i, j, kk: (0, 0))
    tile_spec = pl.BlockSpec((tm, tn), lambda i, j, kk: (i, j))
    if norm_gain is not None:
        assert tn == n and out_split is None
        extra.append(norm_gain)
        in_specs.append(row_spec)
        out_shape = [out_shape, jax.ShapeDtypeStruct((m, n), BF16)]
        out_spec = [out_spec, tile_spec]
    if norm_bwd is not None:
        assert tn == n and out_split is None and norm_gain is None
        extra += list(norm_bwd)
        in_specs += [tile_spec, row_spec, tile_spec]
        out_shape = [jax.ShapeDtypeStruct((m, n), F32), jax.ShapeDtypeStruct((m, n), BF16),
                     jax.ShapeDtypeStruct((1, n), F32)]
        out_spec = [tile_spec, tile_spec, row_spec]
    if after is not None:
        extra.append(after)
        in_specs.append(pl.BlockSpec(memory_space=pl.ANY))

    def body(a_ref, b_ref, *rest):
        tiles = rest[:n_t]
        rows = rest[n_t:n_t + n_r]
        gain_ref = rest[n_t + n_r] if norm_gain is not None else None
        bwd_refs = rest[n_t + n_r:n_t + n_r + 3] if norm_bwd is not None else None
        rest = rest[n_t + n_r + len(extra):]
        o_ref, acc_ref = rest[0], rest[-1]
        av = a_ref[...]
        if pro is not None:
            av = pro(av)
        if single_pass:
            out = _dot(av, b_ref[...], dims)
            if epi is not None:
                out = epi(out, *[t[...] for t in tiles], *[r[...] for r in rows])
            o_ref[...] = out.astype(out_dtype)
            return
        kk = pl.program_id(2)

        @pl.when(kk == 0)
        def _():
            acc_ref[...] = jnp.zeros_like(acc_ref)

        bv = b_ref[...]
        if b_chips > 1:
            bv = bv.reshape(b_block)
        acc_ref[...] += _dot(av, bv, dims)

        @pl.when(kk == nk - 1)
        def _():
            out = acc_ref[...]
            if epi is not None:
                out = epi(out, *[t[...] for t in tiles], *[r[...] for r in rows])
            if gain_ref is not None:
                rest[1][...] = (_rms_stats(out)[0] * gain_ref[...]).astype(BF16)
            if bwd_refs is not None:
                h_ref, g_ref, dres_ref = bwd_refs
                dh, dg = _rms_bwd_tile(out, h_ref[...], g_ref[...])
                total = dres_ref[...] + dh
                o_ref[...] = total
                rest[1][...] = total.astype(BF16)
                first = pl.program_id(0) == 0

                @pl.when(first)
                def _():
                    rest[2][...] = dg

                @pl.when(jnp.logical_not(first))
                def _():
                    rest[2][...] += dg

                return
            out = out.astype(out_dtype)
            o_ref[...] = out.reshape(o_chips, tm // o_chips, tn) if o_chips > 1 else out

    outer = "arbitrary" if norm_bwd is not None else "parallel"
    return pl.pallas_call(
        body, name=name, grid=(m // tm, n // tn, nk),
        in_specs=in_specs, out_specs=out_spec, out_shape=out_shape,
        scratch_shapes=[] if single_pass else [pltpu.VMEM((tm, tn), F32)],
        compiler_params=_cparams((outer, outer, "arbitrary")),
    )(a, b, *epi_tiles, *epi_rows, *extra)


def row_call(name, body, n_rows, tm, ins, outs, accs=()):
    tm = _pick_tile(n_rows, tm)
    in_specs = []
    for arr, kind in ins:
        if kind == "tile":
            if arr.ndim == 2:
                in_specs.append(pl.BlockSpec((tm, arr.shape[1]), lambda i: (i, 0)))
            else:
                in_specs.append(pl.BlockSpec((arr.shape[0], tm, arr.shape[2]), lambda i: (0, i, 0)))
        elif kind == "full":
            in_specs.append(pl.BlockSpec(arr.shape, functools.partial(lambda i, nd: (0,) * nd, nd=arr.ndim)))
        else:
            where, h = kind
            per = tm // h
            if where == "prev":
                in_specs.append(pl.BlockSpec((h, arr.shape[1]), functools.partial(
                    lambda i, per: (jnp.maximum(i * per - 1, 0), 0), per=per)))
            else:
                last = n_rows // h - 1
                in_specs.append(pl.BlockSpec((h, arr.shape[1]), functools.partial(
                    lambda i, per, last: (jnp.minimum((i + 1) * per, last), 0), per=per, last=last)))
    out_shape, out_specs = [], []
    for shape, dtype in outs:
        out_shape.append(jax.ShapeDtypeStruct(shape, dtype))
        if len(shape) == 2:
            out_specs.append(pl.BlockSpec((tm, shape[1]), lambda i: (i, 0)))
        else:
            out_specs.append(pl.BlockSpec((shape[0], tm, shape[2]), lambda i: (0, i, 0)))
    for shape in accs:
        out_shape.append(jax.ShapeDtypeStruct(shape, F32))
        out_specs.append(pl.BlockSpec(shape, lambda i: (0, 0)))
    n_in, n_out, n_acc = len(ins), len(outs), len(accs)

    def kern(*refs):
        i = pl.program_id(0)
        in_refs = refs[:n_in]
        out_refs = refs[n_in:n_in + n_out]
        acc_refs = refs[n_in + n_out:n_in + n_out + n_acc]
        if n_acc:
            @pl.when(i == 0)
            def _():
                for r in acc_refs:
                    r[...] = jnp.zeros_like(r)
        body(i, in_refs, out_refs, acc_refs)

    res = pl.pallas_call(
        kern, name=name, grid=(n_rows // tm,), in_specs=in_specs, out_specs=out_specs, out_shape=out_shape,
        compiler_params=_cparams(("arbitrary",) if n_acc else ("parallel",)),
    )(*[a for a, _ in ins])
    return list(res)


def _rms_stats(h):
    r = lax.rsqrt(jnp.mean(h * h, axis=-1, keepdims=True) + RMS_EPS)
    return h * r, r


def rms_fwd(name, h, g):
    def body(i, ins, outs, accs):
        xhat, _ = _rms_stats(ins[0][...])
        outs[0][...] = (xhat * ins[1][...]).astype(BF16)

    return row_call(name, body, h.shape[0], ROW_TILE, [(h, "tile"), (g, "full")], [(h.shape, BF16)])[0]


def _rms_bwd_tile(dn, h, g):
    xhat, r = _rms_stats(h)
    dxhat = dn * g
    dh = r * (dxhat - xhat * jnp.mean(dxhat * xhat, axis=-1, keepdims=True))
    dg = jnp.sum(dn * xhat, axis=0, keepdims=True)
    return dh, dg


def mem_norm_bwd(name, dn, mem, g):
    def body(i, ins, outs, accs):
        _, dg = _rms_bwd_tile(ins[0][...].astype(F32), ins[1][...], ins[2][...])
        accs[0][...] += dg

    return row_call(name, body, mem.shape[0], ROW_TILE, [(dn, "tile"), (mem, "tile"), (g, "full")], [],
                    [(1, mem.shape[1])])[0]


def loss_head(name, h, g, target):
    d = h.shape[1]

    def body(i, ins, outs, accs):
        hv, gv = ins[0][...], ins[1][...]
        xhat, _ = _rms_stats(hv)
        err = xhat * gv - ins[2][...]
        dy = err * (1.0 / d)
        dh, dg = _rms_bwd_tile(dy, hv, gv)
        outs[0][...] = dh
        outs[1][...] = dh.astype(BF16)
        accs[0][...] += jnp.full((8, LANES), 0.5 / d, F32) * jnp.sum(err * err)
        accs[1][...] += dg

    dh, dh16, loss, dg = row_call(name, body, h.shape[0], ROW_TILE, [(h, "tile"), (g, "full"), (target, "tile")],
                                  [(h.shape, F32), (h.shape, BF16)], [(8, LANES), (1, d)])
    return dh, dh16, loss, dg


def col_sum(name, x):
    def body(i, ins, outs, accs):
        accs[0][...] += jnp.sum(ins[0][...].astype(F32), axis=0, keepdims=True)

    return row_call(name, body, x.shape[0], ROW_TILE, [(x, "tile")], [], [(1, x.shape[1])])[0]


def _conv_taps(xcat, w_ref, cols, width, halo, tm):
    rows = halo + tm
    acc = None
    for j in range(width):
        s = width - 1 - j
        xs = xcat if s == 0 else pltpu.roll(xcat, s, 0)
        term = xs[halo:rows] * w_ref[j:j + 1, cols]
        acc = term if acc is None else acc + term
    return acc


def _conv_taps_bwd_x(dcat, w_ref, cols, width, halo, tm):
    rows = halo + tm
    acc = None
    for j in range(width):
        s = width - 1 - j
        ds = dcat if s == 0 else pltpu.roll(dcat, rows - s, 0)
        term = ds[0:tm] * w_ref[j:j + 1, cols]
        acc = term if acc is None else acc + term
    return acc


def _conv_taps_bwd_w(dy, xcat, width, halo, tm, wrows):
    rows = halo + tm
    rid = _iota((wrows, dy.shape[1]), 0)
    out = jnp.zeros((wrows, dy.shape[1]), F32)
    for j in range(width):
        s = width - 1 - j
        xs = xcat if s == 0 else pltpu.roll(xcat, s, 0)
        v = jnp.sum(dy * xs[halo:rows], axis=0, keepdims=True)
        out = out + jnp.where(rid == j, v, 0.0)
    return out


def dn_pre(qkv_raw, ba, w_conv, gate):
    s_len = qkv_raw.shape[0]
    tm = min(DN_ROW_TILE, s_len)
    n_blk = qkv_raw.shape[1] // LANES

    def body(i, ins, outs, accs):
        x_ref, xp_ref, ba_ref, w_ref, gate_ref = ins
        qkv_ref, hs_ref = outs

        def blk(cb, carry):
            cols = pl.ds(pl.multiple_of(cb * LANES, LANES), LANES)
            prev = jnp.where(i > 0, xp_ref[:, cols], 0.0)
            xcat = jnp.concatenate([prev, x_ref[:, cols]], axis=0)
            c = _conv_taps(xcat, w_ref, cols, DN_CONV, DN_HALO, tm)
            y = _silu(c)
            rs = lax.rsqrt(jnp.sum(y * y, axis=-1, keepdims=True) + L2_EPS)
            fac = jnp.where(cb < DN_HEADS, DN_HEAD_DIM ** -0.5, 1.0)
            qkv_ref[:, cols] = jnp.where(cb < 2 * DN_HEADS, y * (rs * fac), y)
            return carry

        lax.fori_loop(0, n_blk, blk, 0, unroll=4)

        bav = ba_ref[...]
        beta = _sigmoid(bav)
        g = -jnp.exp(gate_ref[0:1, :]) * _softplus(bav + gate_ref[1:2, :])
        lane = _iota((tm, LANES), 1)
        g = jnp.where((lane >= DN_HEADS) & (lane < 2 * DN_HEADS), g, 0.0)
        r = _iota((tm, tm), 0)
        c = _iota((tm, tm), 1)
        tri = jnp.where((r >= c) & ((r >> CHUNK_SHIFT) == (c >> CHUNK_SHIFT)), 1.0, 0.0)
        gc = _dot_hi(tri, g)
        for h in range(DN_HEADS):
            hs_ref[h] = jnp.where(lane == 0, _lane_col(beta, lane, h),
                                  jnp.where(lane == 1, _lane_col(g, lane, DN_HEADS + h),
                                            jnp.where(lane == 2, _lane_col(gc, lane, DN_HEADS + h), 0.0)))

    return row_call("dn_pre", body, s_len, tm,
                    [(qkv_raw, "tile"), (qkv_raw, ("prev", DN_HALO)), (ba, "tile"), (w_conv, "full"), (gate, "full")],
                    [(qkv_raw.shape, F32), ((DN_HEADS, s_len, LANES), F32)])


def _chunk_masks():
    r = _iota((DN_CHUNK, DN_CHUNK), 0)
    c = _iota((DN_CHUNK, DN_CHUNK), 1)
    return r, c


def _decay_matrix(gc, r, c):
    gc_row = jnp.sum(jnp.where(r == c, gc, 0.0), axis=0, keepdims=True)
    causal = r >= c
    return jnp.where(causal, jnp.exp(jnp.where(causal, gc - gc_row, 0.0)), 0.0)


def _tri_inverse(lows, r, c):
    eye = jnp.where(r == c, 1.0, 0.0)
    ts = [eye for _ in lows]
    b = 1
    while b < DN_CHUNK:
        shift = b.bit_length()
        sel = ((r >> shift) == (c >> shift)) & ((r & b) != 0) & ((c & b) == 0)
        lms = [jnp.where(sel, low, 0.0) for low in lows]
        if b == 1:
            ts = [t - lm for t, lm in zip(ts, lms)]
        else:
            t_lm = [_dot_x3(t, lm) for t, lm in zip(ts, lms)]
            t_lm_t = [_dot_x3(x, t) for x, t in zip(t_lm, ts)]
            ts = [t - x for t, x in zip(ts, t_lm_t)]
        b *= 2
    return ts


def dn_solve(qkv, hs):
    s_len = qkv.shape[0]
    rb = min(ROW_TILE, s_len)
    n_chunk = rb // DN_CHUNK
    interleave = min(SOLVE_INTERLEAVE, n_chunk)

    def body(k_ref, v_ref, hs_ref, u_ref, w_ref, t_ref):
        r, c = _chunk_masks()

        def group(gi, carry):
            rows = [pl.ds(pl.multiple_of((gi * interleave + j) * DN_CHUNK, DN_CHUNK), DN_CHUNK)
                    for j in range(interleave)]
            k = [k_ref[rw, :] for rw in rows]
            beta = [hs_ref[rw, 0:1] for rw in rows]
            gc = [hs_ref[rw, 2:3] for rw in rows]
            kb = [a * b for a, b in zip(k, beta)]
            decay = [_decay_matrix(g, r, c) for g in gc]
            lows = [jnp.where(r > c, _dot_nt(a, b) * d, 0.0) for a, b, d in zip(kb, k, decay)]
            ts = _tri_inverse(lows, r, c)
            us = [_dot_x3(t, v_ref[rw, :] * b) for t, rw, b in zip(ts, rows, beta)]
            ws = [_dot_x3(t, a * jnp.exp(g)) for t, a, g in zip(ts, kb, gc)]
            for j, rw in enumerate(rows):
                u_ref[rw, :] = us[j]
                w_ref[rw, :] = ws[j].astype(BF16)
                t_ref[rw, :] = ts[j]
            return carry

        lax.fori_loop(0, n_chunk // interleave, group, 0)

    return pl.pallas_call(
        body, name="dn_solve", grid=(DN_HEADS, s_len // rb),
        in_specs=[pl.BlockSpec((rb, LANES), lambda h, i: (i, DN_HEADS + h)),
                  pl.BlockSpec((rb, LANES), lambda h, i: (i, 2 * DN_HEADS + h)),
                  pl.BlockSpec((None, rb, LANES), lambda h, i: (h, i, 0))],
        out_specs=[pl.BlockSpec((rb, LANES), lambda h, i: (i, h)),
                   pl.BlockSpec((rb, LANES), lambda h, i: (i, h)),
                   pl.BlockSpec((None, rb, DN_CHUNK), lambda h, i: (h, i, 0))],
        out_shape=[jax.ShapeDtypeStruct((s_len, DN_HEADS * LANES), F32),
                   jax.ShapeDtypeStruct((s_len, DN_HEADS * LANES), BF16),
                   jax.ShapeDtypeStruct((DN_HEADS, s_len, DN_CHUNK), F32)],
        compiler_params=_cparams(("parallel", "parallel")),
    )(qkv, qkv, hs)


def dn_scan_fwd(qkv, u, w, hs):
    s_len = qkv.shape[0]
    rb = min(ROW_TILE, s_len)
    n_chunk = rb // DN_CHUNK
    total_chunks = s_len // DN_CHUNK

    hps = FWD_HEADS_PER_STEP
    groups = DN_HEADS // hps

    def body(q_ref, k_ref, u_ref, w_ref, hs_ref, o_ref, st_ref, state):
        @pl.when(pl.program_id(1) == 0)
        def _():
            state[...] = jnp.zeros_like(state)

        r, c = _chunk_masks()

        def chunk(n, carry):
            rows = pl.ds(pl.multiple_of(n * DN_CHUNK, DN_CHUNK), DN_CHUNK)
            heads = range(hps)
            cols = [slice(h * LANES, (h + 1) * LANES) for h in heads]
            each = lambda f, *xs: [f(*a) for a in zip(*xs)]
            q = [q_ref[rows, cl] for cl in cols]
            k = [k_ref[rows, cl] for cl in cols]
            gc = [hs_ref[h, rows, 2:3] for h in heads]
            st = [state[h] for h in heads]
            for h in heads:
                st_ref[h, n] = st[h]
            gl = each(lambda g: jnp.min(g, axis=0, keepdims=True), gc)
            decay = each(lambda g: _decay_matrix(g, r, c), gc)
            w_st = [_dot(w_ref[rows, cols[h]], st[h]) for h in heads]
            qk = each(_dot_nt, q, k)
            q_st = each(lambda a, g, s: _dot(a * jnp.exp(g), s), q, gc, st)
            vn = [u_ref[rows, cols[h]] - w_st[h] for h in heads]
            ai_vn = each(lambda a, d, b: _dot(a * d, b), qk, decay, vn)
            kd_vn = each(lambda a, g0, g, b: _dot_tn(a * jnp.exp(g0 - g), b), k, gl, gc, vn)
            for h in heads:
                o_ref[rows, cols[h]] = q_st[h] + ai_vn[h]
                state[h] = st[h] * jnp.exp(gl[h]) + kd_vn[h]
            return carry

        lax.fori_loop(0, n_chunk, chunk, 0, unroll=2)

    wide = hps * LANES
    blk = lambda off: pl.BlockSpec((rb, wide), lambda h, i: (i, off + h))
    return pl.pallas_call(
        body, name="dn_scan_fwd", grid=(groups, s_len // rb),
        in_specs=[blk(0), blk(groups), blk(0), blk(0),
                  pl.BlockSpec((hps, rb, LANES), lambda h, i: (h, i, 0))],
        out_specs=[blk(0),
                   pl.BlockSpec((hps, n_chunk, LANES, LANES), lambda h, i: (h, i, 0, 0))],
        out_shape=[jax.ShapeDtypeStruct((s_len, DN_HEADS * LANES), F32),
                   jax.ShapeDtypeStruct((DN_HEADS, total_chunks, LANES, LANES), F32)],
        scratch_shapes=[pltpu.VMEM((hps, LANES, LANES), F32)],
        compiler_params=_cparams(("parallel", "arbitrary")),
    )(qkv, qkv, u, w, hs)


def dn_scan_bwd(qkv, u, w, t_inv, hs, states, d_o):
    s_len = qkv.shape[0]
    rb = min(BWD_SCAN_ROWS, s_len)
    n_chunk = rb // DN_CHUNK
    n_blk = s_len // rb
    hps = BWD_HEADS_PER_STEP
    groups = DN_HEADS // hps

    def body(q_ref, k_ref, v_ref, u_ref, w_ref, t_ref, hs_ref, st_ref, do_ref,
             dq_ref, dk_ref, dv_ref, dhs_ref, dstate):
        @pl.when(pl.program_id(1) == 0)
        def _():
            dstate[...] = jnp.zeros_like(dstate)

        r, c = _chunk_masks()
        causal = r >= c
        strict = r > c
        lane = _iota((DN_CHUNK, LANES), 1)
        upper = jnp.where(r <= c, 1.0, 0.0)
        last_row = _iota((DN_CHUNK, 1), 0) == DN_CHUNK - 1

        def chunk(m, carry):
            n = n_chunk - 1 - m
            rows = pl.ds(pl.multiple_of(n * DN_CHUNK, DN_CHUNK), DN_CHUNK)
            heads = range(hps)
            cols = [slice(h * LANES, (h + 1) * LANES) for h in heads]
            each = lambda f, *xs: [f(*a) for a in zip(*xs)]
            rsum = lambda x: jnp.sum(x, axis=-1, keepdims=True)
            dims_tn = (((0,), (0,)), ((), ()))
            q = [q_ref[rows, cl] for cl in cols]
            k = [k_ref[rows, cl] for cl in cols]
            v = [v_ref[rows, cl] for cl in cols]
            uu = [u_ref[rows, cl] for cl in cols]
            ww = [w_ref[rows, cl] for cl in cols]
            do = [do_ref[rows, cl] for cl in cols]
            tt = [t_ref[h, rows, :] for h in heads]
            beta = [hs_ref[h, rows, 0:1] for h in heads]
            gc = [hs_ref[h, rows, 2:3] for h in heads]
            st = [st_ref[h, n] for h in heads]
            dst = [dstate[h] for h in heads]
            gl = each(lambda g: jnp.min(g, axis=0, keepdims=True), gc)
            egc = each(jnp.exp, gc)
            egl = each(jnp.exp, gl)
            ekd = each(lambda a, b: jnp.exp(a - b), gl, gc)
            decay = each(lambda g: _decay_matrix(g, r, c), gc)
            qd = each(jnp.multiply, q, egc)
            kd = each(jnp.multiply, k, ekd)
            kb = each(jnp.multiply, k, beta)
            w_st = each(_dot, ww, st)
            qk = each(_dot_nt, q, k)
            dqd = each(_dot_nt, do, st)
            kd_dst = each(_dot, kd, dst)
            qd_do = each(_dot_tn, qd, do)
            kbk = each(_dot_nt, kb, k)
            vn = each(jnp.subtract, uu, w_st)
            ai = each(jnp.multiply, qk, decay)
            low = each(lambda a, d: jnp.where(strict, a * d, 0.0), kbk, decay)
            dai = each(lambda a, b: jnp.where(causal, _dot_nt(a, b), 0.0), do, vn)
            ai_do = each(_dot_tn, ai, do)
            dkd = each(_dot_nt, vn, dst)
            dvn = each(jnp.add, ai_do, kd_dst)
            dp = each(jnp.multiply, dai, decay)
            dw = each(lambda a, b: -_dot_nt(a, b), dvn, st)
            w_dvn = each(_dot_tn, ww, dvn)
            dp_k = each(_dot, dp, k)
            dp_q = each(_dot_tn, dp, q)
            drhs_u = each(lambda a, b: _dot_x3(a, b, dims_tn), tt, dvn)
            dgl = each(lambda a, b, e: jnp.sum(a * b) * e, dst, st, egl)
            for h in heads:
                dstate[h] = dst[h] * egl[h] + qd_do[h] - w_dvn[h]
            dq = each(lambda a, e, b: a * e + b, dqd, egc, dp_k)
            dk_a = each(lambda a, e, b: a * e + b, dkd, ekd, dp_q)
            rkd = each(lambda a, b: rsum(a * b), dkd, kd)
            drhs_w = each(lambda a, b: _dot_x3(a, b, dims_tn), tt, dw)
            dl_u = each(_dot_nt, drhs_u, uu)
            dl_w = each(_dot_nt, drhs_w, ww)
            dlow = each(lambda a, b: jnp.where(strict, -(a + b), 0.0), dl_u, dl_w)
            dqm = each(jnp.multiply, dlow, decay)
            m_tot = each(lambda a, b, d, e: a * b + d * e, dai, ai, dlow, low)
            dqm_k = each(_dot, dqm, k)
            dk_l = each(_dot_tn, dqm, kb)
            col_rows = each(lambda m: jnp.sum(m, axis=0, keepdims=True), m_tot)
            col_sums = each(lambda rw: jnp.sum(jnp.where(r == c, rw, 0.0), axis=1, keepdims=True), col_rows)
            dkb_w = each(jnp.multiply, drhs_w, egc)
            dkb = each(jnp.add, dkb_w, dqm_k)
            dgc = [rsum(dqd[h] * qd[h]) - rkd[h] + jnp.where(last_row, jnp.sum(rkd[h]) + dgl[h], 0.0)
                   + rsum(m_tot[h]) + rsum(dkb_w[h] * kb[h]) for h in heads]
            dg = each(lambda a, b: _dot_hi(upper, jnp.where(lane == 1, a - b, 0.0)), dgc, col_sums)
            for h in heads:
                dq_ref[rows, cols[h]] = dq[h]
                dk_ref[rows, cols[h]] = dk_a[h] + dk_l[h] + dkb[h] * beta[h]
                dv_ref[rows, cols[h]] = drhs_u[h] * beta[h]
                dbeta = rsum(drhs_u[h] * v[h]) + rsum(dkb[h] * k[h])
                dhs_ref[h, rows, :] = jnp.where(lane == 0, dbeta, dg[h])
            return carry

        lax.fori_loop(0, n_chunk, chunk, 0, unroll=2)

    wide = hps * LANES
    blk = lambda off: pl.BlockSpec((rb, wide), lambda h, i: (n_blk - 1 - i, off + h))
    head = blk(0)
    hs_spec = pl.BlockSpec((hps, rb, LANES), lambda h, i: (h, n_blk - 1 - i, 0))
    full = jax.ShapeDtypeStruct((s_len, DN_HEADS * LANES), F32)
    return pl.pallas_call(
        body, name="dn_scan_bwd", grid=(groups, n_blk),
        in_specs=[blk(0), blk(groups), blk(2 * groups), head, head,
                  pl.BlockSpec((hps, rb, DN_CHUNK), lambda h, i: (h, n_blk - 1 - i, 0)), hs_spec,
                  pl.BlockSpec((hps, n_chunk, LANES, LANES), lambda h, i: (h, n_blk - 1 - i, 0, 0)), head],
        out_specs=[head, head, head, hs_spec],
        out_shape=[full, full, full, jax.ShapeDtypeStruct((DN_HEADS, s_len, LANES), F32)],
        scratch_shapes=[pltpu.VMEM((hps, LANES, LANES), F32)],
        compiler_params=_cparams(("parallel", "arbitrary")),
    )(qkv, qkv, qkv, u, w, t_inv, hs, states, d_o)


def dn_post(o, z, out_norm):
    def body(i, ins, outs, accs):
        gn = ins[2][...]
        for h in range(DN_HEADS):
            cols = slice(h * LANES, (h + 1) * LANES)
            xhat, _ = _rms_stats(ins[0][:, cols])
            outs[0][:, cols] = (xhat * gn * _silu(ins[1][:, cols])).astype(BF16)

    return row_call("dn_post", body, o.shape[0], ROW_TILE, [(o, "tile"), (z, "tile"), (out_norm, "full")],
                    [(o.shape, BF16)])[0]


def dn_post_bwd(d_og, o, z, out_norm):
    def body(i, ins, outs, accs):
        gn = ins[3][...]
        dgn = jnp.zeros((1, LANES), F32)
        for h in range(DN_HEADS):
            cols = slice(h * LANES, (h + 1) * LANES)
            dy, zh = ins[0][:, cols].astype(F32), ins[2][:, cols]
            xhat, r = _rms_stats(ins[1][:, cols])
            sz = _silu(zh)
            dgn = dgn + jnp.sum(dy * xhat * sz, axis=0, keepdims=True)
            outs[1][:, cols] = (dy * xhat * gn * _silu_grad(zh)).astype(BF16)
            dxhat = dy * gn * sz
            outs[0][:, cols] = r * (dxhat - xhat * jnp.mean(dxhat * xhat, axis=-1, keepdims=True))
        accs[0][...] += dgn

    return row_call("dn_post_bwd", body, o.shape[0], ROW_TILE,
                    [(d_og, "tile"), (o, "tile"), (z, "tile"), (out_norm, "full")],
                    [(o.shape, F32), (o.shape, BF16)], [(1, LANES)])


def dn_pre_bwd(dq, dk, dv, dhs, qkv_raw, ba, w_conv, gate):
    s_len = qkv_raw.shape[0]
    tm = min(DN_ROW_TILE, s_len)

    def body(i, ins, outs, accs):
        dq_ref, dk_ref, dv_ref, dhs_ref, x_ref, xp_ref, ba_ref, w_ref, gate_ref = ins
        dc_ref, dba_ref = outs

        def blk(cb, carry):
            cols = pl.ds(pl.multiple_of(cb * LANES, LANES), LANES)
            hcols = pl.ds(pl.multiple_of((cb & (DN_HEADS - 1)) * LANES, LANES), LANES)
            prev = jnp.where(i > 0, xp_ref[:, cols], 0.0)
            xcat = jnp.concatenate([prev, x_ref[:, cols]], axis=0)
            c = _conv_taps(xcat, w_ref, cols, DN_CONV, DN_HALO, tm)
            y = _silu(c)
            dy = jnp.where(cb < DN_HEADS, dq_ref[:, hcols],
                           jnp.where(cb < 2 * DN_HEADS, dk_ref[:, hcols], dv_ref[:, hcols]))
            rs = lax.rsqrt(jnp.sum(y * y, axis=-1, keepdims=True) + L2_EPS)
            fac = jnp.where(cb < DN_HEADS, DN_HEAD_DIM ** -0.5, 1.0)
            nrm = y * rs
            dn = dy * fac
            dy_norm = rs * (dn - nrm * jnp.sum(dn * nrm, axis=-1, keepdims=True))
            dc_ref[:, cols] = jnp.where(cb < 2 * DN_HEADS, dy_norm, dy) * _silu_grad(c)
            return carry

        lax.fori_loop(0, qkv_raw.shape[1] // LANES, blk, 0, unroll=4)

        lane = _iota((tm, LANES), 1)
        dbeta = jnp.zeros((tm, LANES), F32)
        dg = jnp.zeros((tm, LANES), F32)
        for h in range(DN_HEADS):
            dbeta = dbeta + jnp.where(lane == h, dhs_ref[h, :, 0:1], 0.0)
            dg = dg + jnp.where(lane == DN_HEADS + h, dhs_ref[h, :, 1:2], 0.0)
        bav = ba_ref[...]
        beta = _sigmoid(bav)
        ea = jnp.exp(gate_ref[0:1, :])
        pre = bav + gate_ref[1:2, :]
        g = -ea * _softplus(pre)
        da = dg * (-ea) * _sigmoid(pre)
        dba_ref[...] = (dbeta * beta * (1.0 - beta) + da).astype(BF16)
        rid = _iota((8, LANES), 0)
        accs[0][...] += (jnp.where(rid == 0, jnp.sum(dg * g, axis=0, keepdims=True), 0.0)
                         + jnp.where(rid == 1, jnp.sum(da, axis=0, keepdims=True), 0.0))

    return row_call("dn_pre_bwd", body, s_len, tm,
                    [(dq, "tile"), (dk, "tile"), (dv, "tile"), (dhs, "tile"), (qkv_raw, "tile"),
                     (qkv_raw, ("prev", DN_HALO)), (ba, "tile"), (w_conv, "full"), (gate, "full")],
                    [(qkv_raw.shape, F32), (ba.shape, BF16)], [(8, LANES)])


def dn_conv_bwd(dc, qkv_raw, w_conv):
    s_len = dc.shape[0]
    tm = min(DN_ROW_TILE, s_len)
    nt = s_len // tm

    def body(i, ins, outs, accs):
        dc_ref, dn_ref, x_ref, xp_ref, w_ref = ins

        def blk(cb, carry):
            cols = pl.ds(pl.multiple_of(cb * LANES, LANES), LANES)
            dy = dc_ref[:, cols]
            nxt = jnp.where(i < nt - 1, dn_ref[:, cols], 0.0)
            dcat = jnp.concatenate([dy, nxt], axis=0)
            outs[0][:, cols] = _conv_taps_bwd_x(dcat, w_ref, cols, DN_CONV, DN_HALO, tm).astype(BF16)
            prev = jnp.where(i > 0, xp_ref[:, cols], 0.0)
            xcat = jnp.concatenate([prev, x_ref[:, cols]], axis=0)
            accs[0][:, cols] += _conv_taps_bwd_w(dy, xcat, DN_CONV, DN_HALO, tm, 8)
            return carry

        lax.fori_loop(0, dc.shape[1] // LANES, blk, 0)

    return row_call("dn_conv_bwd", body, s_len, tm,
                    [(dc, "tile"), (dc, ("next", DN_HALO)), (qkv_raw, "tile"), (qkv_raw, ("prev", DN_HALO)),
                     (w_conv, "full")],
                    [(dc.shape, BF16)], [(8, dc.shape[1])])


def _glu(u_ref, cols, d):
    return u_ref[:, cols] * _sigmoid(u_ref[:, pl.ds(pl.multiple_of(d + cols.start, LANES), cols.size)])


def cv_core_fwd(u, w_dw, b_dw, ln_g, ln_b):
    s_len, d = u.shape[0], u.shape[1] // 2
    tm = min(CONV_ROW_TILE, s_len)

    def body(i, ins, outs, accs):
        u_ref, up_ref, w_ref, bdw_ref, g_ref, b_ref = ins
        s_ref, c_ref = outs

        def blk(cb, carry):
            cols = pl.ds(pl.multiple_of(cb * LANES, LANES), LANES)
            prev = jnp.where(i > 0, _glu(up_ref, cols, d), 0.0)
            xcat = jnp.concatenate([prev, _glu(u_ref, cols, d)], axis=0)
            c_ref[:, cols] = _conv_taps(xcat, w_ref, cols, CV_WIDTH, CV_HALO, tm) + bdw_ref[:, cols]
            return carry

        lax.fori_loop(0, d // LANES, blk, 0)
        c = c_ref[...]
        mu = jnp.mean(c, axis=-1, keepdims=True)
        xc = c - mu
        rstd = lax.rsqrt(jnp.mean(xc * xc, axis=-1, keepdims=True) + LN_EPS)
        s_ref[...] = _silu(xc * rstd * g_ref[...] + b_ref[...]).astype(BF16)

    return row_call("cv_core_fwd", body, s_len, tm,
                    [(u, "tile"), (u, ("prev", CV_HALO)), (w_dw, "full"), (b_dw, "full"), (ln_g, "full"),
                     (ln_b, "full")],
                    [((s_len, d), BF16), ((s_len, d), F32)])


def cv_ln_bwd(ds, c, ln_g, ln_b):
    def body(i, ins, outs, accs):
        cv, g = ins[1][...], ins[2][...]
        mu = jnp.mean(cv, axis=-1, keepdims=True)
        xc = cv - mu
        rstd = lax.rsqrt(jnp.mean(xc * xc, axis=-1, keepdims=True) + LN_EPS)
        xhat = xc * rstd
        dl = ins[0][...].astype(F32) * _silu_grad(xhat * g + ins[3][...])
        dxhat = dl * g
        dc = rstd * (dxhat - jnp.mean(dxhat, axis=-1, keepdims=True)
                     - xhat * jnp.mean(dxhat * xhat, axis=-1, keepdims=True))
        outs[0][...] = dc
        rid = _iota((8, cv.shape[1]), 0)
        accs[0][...] += (jnp.where(rid == 0, jnp.sum(dl * xhat, axis=0, keepdims=True), 0.0)
                         + jnp.where(rid == 1, jnp.sum(dl, axis=0, keepdims=True), 0.0)
                         + jnp.where(rid == 2, jnp.sum(dc, axis=0, keepdims=True), 0.0))

    return row_call("cv_ln_bwd", body, c.shape[0], ROW_TILE,
                    [(ds, "tile"), (c, "tile"), (ln_g, "full"), (ln_b, "full")], [(c.shape, F32)], [(8, c.shape[1])])


def cv_conv_bwd(dc, u, w_dw):
    s_len, d = dc.shape
    tm = min(CONV_ROW_TILE, s_len)
    nt = s_len // tm

    def body(i, ins, outs, accs):
        dc_ref, dn_ref, u_ref, up_ref, w_ref = ins

        def blk(cb, carry):
            cols = pl.ds(pl.multiple_of(cb * LANES, LANES), LANES)
            gcols = pl.ds(pl.multiple_of(d + cb * LANES, LANES), LANES)
            dy = dc_ref[:, cols]
            nxt = jnp.where(i < nt - 1, dn_ref[:, cols], 0.0)
            dgl = _conv_taps_bwd_x(jnp.concatenate([dy, nxt], axis=0), w_ref, cols, CV_WIDTH, CV_HALO, tm)
            u1, sg = u_ref[:, cols], _sigmoid(u_ref[:, gcols])
            du1 = dgl * sg
            du2 = dgl * u1 * sg * (1.0 - sg)
            outs[0][:, cols] = du1.astype(BF16)
            outs[0][:, gcols] = du2.astype(BF16)
            accs[1][:, cols] += jnp.sum(du1, axis=0, keepdims=True)
            accs[1][:, gcols] += jnp.sum(du2, axis=0, keepdims=True)
            prev = jnp.where(i > 0, _glu(up_ref, cols, d), 0.0)
            xcat = jnp.concatenate([prev, u1 * sg], axis=0)
            accs[0][:, cols] += _conv_taps_bwd_w(dy, xcat, CV_WIDTH, CV_HALO, tm, CV_HALO)
            return carry

        lax.fori_loop(0, d // LANES, blk, 0)

    return row_call("cv_conv_bwd", body, s_len, tm,
                    [(dc, "tile"), (dc, ("next", CV_HALO)), (u, "tile"), (u, ("prev", CV_HALO)), (w_dw, "full")],
                    [(u.shape, BF16)], [(CV_HALO, d), (1, 2 * d)])


def xa_core_fwd(name, q, kv):
    d = q.shape[1]

    def body(i, ins, outs, accs):
        for h in range(XA_HEADS):
            cols = slice(h * XA_HEAD_DIM, (h + 1) * XA_HEAD_DIM)
            vcols = slice(d + h * XA_HEAD_DIM, d + (h + 1) * XA_HEAD_DIM)
            s = _dot_nt(ins[0][:, cols], ins[1][:, cols]) * (XA_HEAD_DIM ** -0.5)
            e = jnp.exp(s - jnp.max(s, axis=-1, keepdims=True))
            p = e / jnp.sum(e, axis=-1, keepdims=True)
            outs[0][:, cols] = _dot(p, ins[1][:, vcols]).astype(BF16)

    return row_call(name, body, q.shape[0], ROW_TILE, [(q, "tile"), (kv, "full")], [(q.shape, BF16)])[0]


def xa_core_bwd(name, d_o, q, kv):
    d = q.shape[1]

    def body(i, ins, outs, accs):
        for h in range(XA_HEADS):
            cols = slice(h * XA_HEAD_DIM, (h + 1) * XA_HEAD_DIM)
            vcols = slice(d + h * XA_HEAD_DIM, d + (h + 1) * XA_HEAD_DIM)
            qh, kh, vh, doh = ins[1][:, cols], ins[2][:, cols], ins[2][:, vcols], ins[0][:, cols]
            s = _dot_nt(qh, kh) * (XA_HEAD_DIM ** -0.5)
            e = jnp.exp(s - jnp.max(s, axis=-1, keepdims=True))
            p = e / jnp.sum(e, axis=-1, keepdims=True)
            dp = _dot_nt(doh, vh)
            ds = p * (dp - jnp.sum(dp * p, axis=-1, keepdims=True)) * (XA_HEAD_DIM ** -0.5)
            outs[0][:, cols] = _dot(ds, kh).astype(BF16)
            accs[0][:, cols] += _dot_tn(ds, qh)
            accs[0][:, vcols] += _dot_tn(p, doh)

    return row_call(name, body, q.shape[0], ROW_TILE, [(d_o, "tile"), (q, "tile"), (kv, "full")],
                    [(q.shape, BF16)], [kv.shape])


def adamw(name, w, g, m, v):
    def body(i, ins, outs, accs):
        wv, gv = ins[0][...], ins[1][...]
        mn = ADAM_B1 * ins[2][...] + (1.0 - ADAM_B1) * gv
        vn = ADAM_B2 * ins[3][...] + (1.0 - ADAM_B2) * jnp.square(gv)
        m_hat = mn / (1.0 - ADAM_B1 ** ADAM_STEP)
        v_hat = vn / (1.0 - ADAM_B2 ** ADAM_STEP)
        outs[0][...] = -ADAM_LR * (m_hat / (jnp.sqrt(v_hat) + ADAM_EPS) + ADAM_WD * wv)
        outs[1][...] = mn
        outs[2][...] = vn

    return row_call(name, body, w.shape[0], ROW_TILE, [(w, "tile"), (g, "tile"), (m, "tile"), (v, "tile")],
                    [(w.shape, F32)] * 3)


def adamw_halves(name, w, g_mine, g_sibling, m, v, core):
    n_layers = len(g_mine)
    rows, cols = w.shape
    half_rows = rows // n_layers // 2
    tm = _pick_tile(half_rows, ADAMW_ROW_TILE)
    per_half = half_rows // tm

    def body(core_ref, w_ref, *rest):
        g_refs = rest[:2 * n_layers]
        m_ref, v_ref, g_out, d_out, m_out, v_out = rest[2 * n_layers:]
        i = pl.program_id(0)
        mine = ((i // per_half) % 2) == core_ref[0]
        layer = i // (2 * per_half)
        gv = jnp.where(mine, g_refs[0][...], g_refs[n_layers][...])
        for l in range(1, n_layers):
            gv = jnp.where(layer == l, jnp.where(mine, g_refs[l][...], g_refs[n_layers + l][...]), gv)
        mn = ADAM_B1 * m_ref[...] + (1.0 - ADAM_B1) * gv
        vn = ADAM_B2 * v_ref[...] + (1.0 - ADAM_B2) * jnp.square(gv)
        m_hat = mn / (1.0 - ADAM_B1 ** ADAM_STEP)
        v_hat = vn / (1.0 - ADAM_B2 ** ADAM_STEP)
        g_out[...] = gv
        d_out[...] = -ADAM_LR * (m_hat / (jnp.sqrt(v_hat) + ADAM_EPS) + ADAM_WD * w_ref[...])
        m_out[...] = mn
        v_out[...] = vn

    whole = pl.BlockSpec((tm, cols), lambda i, core_ref: (i, 0))

    def half(layer, own):
        def index(i, core_ref):
            used = (i // (2 * per_half) == layer) & ((((i // per_half) % 2) == core_ref[0]) == own)
            return (jnp.where(used, i % per_half, 0), 0)

        return pl.BlockSpec((tm, cols), index)

    halves = [half(l, True) for l in range(n_layers)] + [half(l, False) for l in range(n_layers)]
    return pl.pallas_call(
        body, name=name,
        grid_spec=pltpu.PrefetchScalarGridSpec(
            num_scalar_prefetch=1, grid=(2 * per_half * n_layers,),
            in_specs=[whole] + halves + [whole, whole], out_specs=[whole] * 4),
        out_shape=[jax.ShapeDtypeStruct(w.shape, F32)] * 4,
        compiler_params=_cparams(("parallel",)),
    )(core, w, *g_mine, *g_sibling, m, v)


HBM_SPEC = pl.BlockSpec(memory_space=pltpu.HBM)


def _position():
    return lax.axis_index("x"), lax.axis_index("y"), lax.axis_index("c")


def _other_chips(x, y):
    return [(1 - x, y), (x, 1 - y), (1 - x, 1 - y)]


def _row_chunks(rows):
    return rows // D2D_CHUNK_ROWS if rows % D2D_CHUNK_ROWS == 0 else 1


def _start_chunked(make, rows):
    k = _row_chunks(rows)
    for i in range(k):
        make(i * (rows // k), rows // k).start()


def gather_shards(packs):
    n = len(packs)

    def body(*refs):
        srcs, outs = refs[:n], refs[n:2 * n]
        send_sems, recv_sems = refs[2 * n:]
        x, y, c = _position()
        me = 2 * x + y
        chips = _other_chips(x, y)
        sibling = (x, y, 1 - c)

        def over_ici(a, j):
            px, py = chips[j]
            rows = srcs[a].shape[0] // 2
            return pltpu.make_async_remote_copy(
                src_ref=srcs[a].at[pl.ds(c * rows, rows), :], dst_ref=outs[a].at[me, pl.ds(c * rows, rows), :],
                send_sem=send_sems.at[a, j], recv_sem=recv_sems.at[a, j], device_id=(px, py, c), device_id_type=MESH)

        def landed(a, j):
            px, py = chips[j]
            rows = srcs[a].shape[0] // 2
            part = outs[a].at[2 * px + py, pl.ds(c * rows, rows), :]
            return pltpu.make_async_remote_copy(
                src_ref=part, dst_ref=part, send_sem=send_sems.at[a, j], recv_sem=recv_sems.at[a, j],
                device_id=(px, py, c), device_id_type=MESH)

        def over_d2d(a, j, cc, off, size):
            px, py = chips[j]
            rows = srcs[a].shape[0] // 2
            part = outs[a].at[2 * px + py, pl.ds(cc * rows + off, size), :]
            return pltpu.make_async_remote_copy(
                src_ref=part, dst_ref=part, send_sem=send_sems.at[a, 3 + j], recv_sem=recv_sems.at[a, 3 + j],
                device_id=sibling, device_id_type=MESH)

        for a in range(n):
            for j in range(3):
                over_ici(a, j).start()
        for a in range(n):
            for j in range(3):
                landed(a, j).wait_recv()
                _start_chunked(functools.partial(over_d2d, a, j, c), srcs[a].shape[0] // 2)
        for a in range(n):
            rows = srcs[a].shape[0] // 2
            for j in range(3):
                over_d2d(a, j, 1 - c, 0, rows).wait_recv()
                over_d2d(a, j, c, 0, rows).wait_send()
                over_ici(a, j).wait_send()

    return pl.pallas_call(
        body, name="gather_shards",
        in_specs=[HBM_SPEC] * n, out_specs=[HBM_SPEC] * n,
        out_shape=[jax.ShapeDtypeStruct((N_CHIPS,) + p.shape, p.dtype) for p in packs],
        scratch_shapes=[pltpu.SemaphoreType.DMA((n, 6)), pltpu.SemaphoreType.DMA((n, 6))],
    )(*packs)


def pair_split(name, packs):
    n = len(packs)

    def body(*refs):
        srcs, outs = refs[:n], refs[n:2 * n]
        send_sems, recv_sems = refs[2 * n:]
        x, y, c = _position()

        def remote(a, off, size):
            rows = srcs[a].shape[1] // 2
            return pltpu.make_async_remote_copy(
                src_ref=srcs[a].at[:, pl.ds((1 - c) * rows + off, size), :],
                dst_ref=outs[a].at[:, pl.ds(off, size), :],
                send_sem=send_sems.at[a], recv_sem=recv_sems.at[a], device_id=(x, y, 1 - c), device_id_type=MESH)

        for a in range(n):
            _start_chunked(functools.partial(remote, a), srcs[a].shape[1] // 2)
        for a in range(n):
            remote(a, 0, srcs[a].shape[1] // 2).wait()

    return pl.pallas_call(
        body, name=name, in_specs=[HBM_SPEC] * n, out_specs=[HBM_SPEC] * n,
        out_shape=[jax.ShapeDtypeStruct((p.shape[0], p.shape[1] // 2, p.shape[2]), p.dtype) for p in packs],
        scratch_shapes=[pltpu.SemaphoreType.DMA((n,)), pltpu.SemaphoreType.DMA((n,))],
    )(*packs)


def pair_join(name, halves):
    n = len(halves)

    def body(*refs):
        srcs, outs = refs[:n], refs[n:2 * n]
        send_sems, recv_sems = refs[2 * n:]
        x, y, c = _position()

        def remote(a, off, size):
            return pltpu.make_async_remote_copy(
                src_ref=srcs[a].at[pl.ds(off, size), :], dst_ref=outs[a].at[pl.ds(off, size), :],
                send_sem=send_sems.at[a], recv_sem=recv_sems.at[a], device_id=(x, y, 1 - c), device_id_type=MESH)

        for a in range(n):
            _start_chunked(functools.partial(remote, a), srcs[a].shape[0])
        for a in range(n):
            remote(a, 0, srcs[a].shape[0]).wait()

    return pl.pallas_call(
        body, name=name, in_specs=[HBM_SPEC] * n, out_specs=[HBM_SPEC] * n,
        out_shape=[jax.ShapeDtypeStruct(p.shape, p.dtype) for p in halves],
        scratch_shapes=[pltpu.SemaphoreType.DMA((n,)), pltpu.SemaphoreType.DMA((n,))],
    )(*halves)


SEM_SPEC = pl.BlockSpec(memory_space=pltpu.SEMAPHORE)
DATAFLOW = pltpu.SideEffectType.DATAFLOW_SIDE_EFFECTING


def _ici_copy(kind, srcs, lands, send_sems, recv_sems, a, j):
    x, y, c = _position()
    if kind == "pair":
        rows = srcs[a].shape[1] // 2
        return pltpu.make_async_remote_copy(
            src_ref=srcs[a].at[:, pl.ds((1 - c) * rows, rows), :], dst_ref=lands[a], send_sem=send_sems,
            recv_sem=recv_sems, device_id=(x, y, 1 - c), device_id_type=MESH)
    px, py = _other_chips(x, y)[j]
    if kind == "gather":
        rows = srcs[a].shape[0] // 2
        src = srcs[a].at[pl.ds(c * rows, rows), :]
        dst = lands[a].at[2 * x + y, pl.ds(c * rows, rows), :]
    else:
        src = srcs[a].at[2 * px + py]
        dst = lands[a].at[j]
    return pltpu.make_async_remote_copy(src_ref=src, dst_ref=dst, send_sem=send_sems, recv_sem=recv_sems,
                                        device_id=(px, py, c), device_id_type=MESH)


def ici_start(name, kind, srcs, land_shapes):
    n = len(srcs)
    lands = [pltpu.with_memory_space_constraint(lax.empty(shp, s.dtype), pltpu.HBM) for shp, s in zip(land_shapes, srcs)]

    def body(*refs):
        src_refs, land_refs = refs[:n], refs[n:2 * n]
        send_sems, recv_sems = refs[2 * n], refs[2 * n + 1]
        token = refs[-1]
        for a in range(n):
            for j in range(1 if kind == "pair" else N_CHIPS - 1):
                _ici_copy(kind, src_refs, land_refs, send_sems, recv_sems, a, j).start()
        token[...] = jnp.zeros_like(token)

    sems = pltpu.SemaphoreType.DMA(())
    res = pl.pallas_call(
        body, name=name,
        out_shape=[sems, sems] + [pltpu.HBM(s.shape, s.dtype) for s in srcs]
        + [pltpu.HBM(l.shape, l.dtype) for l in lands] + [jax.ShapeDtypeStruct((8, LANES), F32)],
        in_specs=[HBM_SPEC] * (2 * n),
        out_specs=[SEM_SPEC, SEM_SPEC] + [HBM_SPEC] * (2 * n) + [pl.BlockSpec(memory_space=pltpu.VMEM)],
        input_output_aliases={i: 2 + i for i in range(2 * n)},
        compiler_params=pltpu.CompilerParams(has_side_effects=DATAFLOW),
    )(*[pltpu.with_memory_space_constraint(s, pltpu.HBM) for s in srcs], *lands)
    return res[0], res[1], list(res[2:2 + n]), list(res[2 + n:2 + 2 * n]), res[-1]


def ici_wait(name, kind, send_sems, recv_sems, srcs, lands, after):
    n = len(srcs)

    def body(*refs):
        src_refs, land_refs = refs[:n], refs[n:2 * n]
        send, recv = refs[2 * n], refs[2 * n + 1]
        for a in range(n):
            for j in range(1 if kind == "pair" else N_CHIPS - 1):
                cp = _ici_copy(kind, src_refs, land_refs, send, recv, a, j)
                cp.wait_send()
                cp.wait_recv()

    res = pl.pallas_call(
        body, name=name,
        out_shape=[pltpu.HBM(s.shape, s.dtype) for s in srcs] + [pltpu.HBM(l.shape, l.dtype) for l in lands],
        in_specs=[HBM_SPEC] * (2 * n) + [SEM_SPEC, SEM_SPEC, pl.BlockSpec(memory_space=pl.ANY)],
        out_specs=[HBM_SPEC] * (2 * n),
        input_output_aliases={i: i for i in range(2 * n)},
        compiler_params=pltpu.CompilerParams(has_side_effects=DATAFLOW),
    )(*srcs, *lands, send_sems, recv_sems, after)
    return list(res[:n]), list(res[n:])


def pair_forward(gathered):
    n = len(gathered)

    def body(*refs):
        outs = refs[n:2 * n]
        send_sems, recv_sems = refs[2 * n:]
        x, y, c = _position()
        chips = _other_chips(x, y)

        def part(a, j, cc, off, size):
            px, py = chips[j]
            rows = outs[a].shape[1] // 2
            ref = outs[a].at[2 * px + py, pl.ds(cc * rows + off, size), :]
            return pltpu.make_async_remote_copy(
                src_ref=ref, dst_ref=ref, send_sem=send_sems.at[a, j], recv_sem=recv_sems.at[a, j],
                device_id=(x, y, 1 - c), device_id_type=MESH)

        for a in range(n):
            for j in range(N_CHIPS - 1):
                _start_chunked(functools.partial(part, a, j, c), outs[a].shape[1] // 2)
        for a in range(n):
            rows = outs[a].shape[1] // 2
            for j in range(N_CHIPS - 1):
                part(a, j, 1 - c, 0, rows).wait_recv()
                part(a, j, c, 0, rows).wait_send()

    return pl.pallas_call(
        body, name="pair_forward", in_specs=[HBM_SPEC] * n, out_specs=[HBM_SPEC] * n,
        out_shape=[jax.ShapeDtypeStruct(g.shape, g.dtype) for g in gathered],
        input_output_aliases={i: i for i in range(n)},
        scratch_shapes=[pltpu.SemaphoreType.DMA((n, N_CHIPS - 1)), pltpu.SemaphoreType.DMA((n, N_CHIPS - 1))],
    )(*gathered)


def all_sum_small(part):
    n_dev = 8
    rows = part.shape[0]

    def body(src, out, buf, send_sems, recv_sems):
        x, y, c = _position()
        me = 4 * x + 2 * y + c
        buf[me] = src[...]
        copies = []
        for k in range(1, n_dev):
            px, py, pc = x ^ ((k >> 2) & 1), y ^ ((k >> 1) & 1), c ^ (k & 1)
            cp = pltpu.make_async_remote_copy(
                src_ref=src, dst_ref=buf.at[me], send_sem=send_sems.at[k - 1], recv_sem=recv_sems.at[k - 1],
                device_id=(px, py, pc), device_id_type=MESH)
            cp.start()
            copies.append(cp)
        for cp in copies:
            cp.wait()
        acc = buf[0]
        for k in range(1, n_dev):
            acc = acc + buf[k]
        out[...] = acc

    return pl.pallas_call(
        body, name="all_sum_small",
        in_specs=[pl.BlockSpec(memory_space=pltpu.VMEM)], out_specs=pl.BlockSpec(memory_space=pltpu.VMEM),
        out_shape=jax.ShapeDtypeStruct(part.shape, F32),
        scratch_shapes=[pltpu.VMEM((n_dev, rows, part.shape[1]), F32),
                        pltpu.SemaphoreType.DMA((n_dev - 1,)), pltpu.SemaphoreType.DMA((n_dev - 1,))],
    )(part)


def add_pairs(name, src, theirs, core, out_dtype):
    slabs, rows, cols = theirs.shape
    tm = _pick_tile(rows, ROW_TILE)
    nb = rows // tm

    def body(core_ref, a_ref, b_ref, o_ref):
        o_ref[...] = (a_ref[...].astype(F32) + b_ref[...].astype(F32)).astype(out_dtype)

    return pl.pallas_call(
        body, name=name,
        grid_spec=pltpu.PrefetchScalarGridSpec(
            num_scalar_prefetch=1, grid=(slabs, nb),
            in_specs=[pl.BlockSpec((None, tm, cols), lambda s, i, core_ref: (s, core_ref[0] * nb + i, 0)),
                      pl.BlockSpec((None, tm, cols), lambda s, i, core_ref: (s, i, 0))],
            out_specs=pl.BlockSpec((None, tm, cols), lambda s, i, core_ref: (s, i, 0))),
        out_shape=jax.ShapeDtypeStruct(theirs.shape, out_dtype),
        compiler_params=_cparams(("parallel", "parallel")),
    )(core, src, theirs)


def add_four(name, src, theirs, chip):
    _, rows, cols = theirs.shape
    tm = _pick_tile(rows, ROW_TILE)

    def body(chip_ref, a_ref, b_ref, o_ref):
        acc = a_ref[...].astype(F32)
        for j in range(N_CHIPS - 1):
            acc = acc + b_ref[j].astype(F32)
        o_ref[...] = acc

    return pl.pallas_call(
        body, name=name,
        grid_spec=pltpu.PrefetchScalarGridSpec(
            num_scalar_prefetch=1, grid=(rows // tm,),
            in_specs=[pl.BlockSpec((None, tm, cols), lambda i, chip_ref: (chip_ref[0], i, 0)),
                      pl.BlockSpec((N_CHIPS - 1, tm, cols), lambda i, chip_ref: (0, i, 0))],
            out_specs=pl.BlockSpec((tm, cols), lambda i, chip_ref: (i, 0))),
        out_shape=jax.ShapeDtypeStruct((rows, cols), F32),
        compiler_params=_cparams(("parallel",)),
    )(chip, src, theirs)


PACK_COLS = 1024
SMALL_ROW_MULTIPLE = 32
BIG = ["dn_w_in", "dn_w_out", "cv_w_pw1", "cv_w_pw2", "xa_w_q", "xa_w_kv", "xa_w_o", "mlp_w_up", "mlp_w_down"]
SMALL = ["dn_w_conv", "cv_norm", "cv_b_pw1", "cv_w_dw", "cv_b_dw", "cv_ln_g", "cv_ln_b", "cv_b_pw2"]
SHARD_AXIS = {"dn_w_in": 2, "dn_w_conv": 2, "dn_w_out": 1, "cv_norm": 1, "cv_w_pw1": 2, "cv_b_pw1": 1,
              "cv_w_dw": 2, "cv_b_dw": 1, "cv_ln_g": 1, "cv_ln_b": 1, "cv_w_pw2": 1, "cv_b_pw2": 1,
              "xa_w_q": 1, "xa_w_kv": 2, "xa_w_o": 1, "mlp_w_up": 2, "mlp_w_down": 1}
REPLICATED = ["dn_norm", "dn_a_log", "dn_dt_bias", "dn_out_norm", "xa_norm", "xa_mem_norm", "mlp_norm", "final_norm"]


def _pack_rows(size):
    return -(-size // PACK_COLS)


SHARD_SHAPES = {
    "dn_w_in": (1, 1024, 1028), "dn_w_conv": (1, 4, 768), "dn_w_out": (1, 256, 1024), "cv_norm": (1, 256),
    "cv_w_pw1": (1, 1024, 512), "cv_b_pw1": (1, 512), "cv_w_dw": (1, 31, 256), "cv_b_dw": (1, 256),
    "cv_ln_g": (1, 256), "cv_ln_b": (1, 256), "cv_w_pw2": (1, 256, 1024), "cv_b_pw2": (1, 256),
    "xa_w_q": (2, 256, 1024), "xa_w_kv": (2, 1024, 512), "xa_w_o": (2, 256, 1024),
    "mlp_w_up": (2, 1024, 1024), "mlp_w_down": (2, 1024, 1024)}


def _shard_shape(nm):
    return SHARD_SHAPES[nm]


def _pack(tensors, names, dtype, row_multiple):
    pieces = []
    for nm in names:
        t = tensors[nm]
        flat = t.reshape(t.shape[0], -1) if t.ndim > len(_shard_shape(nm)) else t.reshape(1, -1)
        pad = _pack_rows(flat.shape[1]) * PACK_COLS - flat.shape[1]
        pieces.append(jnp.pad(flat.astype(dtype), ((0, 0), (0, pad))))
    cat = jnp.concatenate(pieces, axis=1)
    rows = cat.shape[1] // PACK_COLS
    total = -(-rows // row_multiple) * row_multiple
    cat = jnp.pad(cat, ((0, 0), (0, (total - rows) * PACK_COLS)))
    return cat.reshape(cat.shape[0], total, PACK_COLS)


def _unpack(pack, names):
    lead = pack.shape[:-2]
    flat = pack.reshape(lead + (-1,))
    out, off = {}, 0
    for nm in names:
        shp = _shard_shape(nm)
        size = 1
        for s in shp:
            size *= s
        out[nm] = flat[..., off:off + size].reshape(lead + shp)
        off += _pack_rows(size) * PACK_COLS
    return out


def _to_full(nm, stacked):
    ax = SHARD_AXIS[nm]
    moved = jnp.moveaxis(stacked, 0, ax)
    shp = list(_shard_shape(nm))
    shp[ax] *= N_CHIPS
    return moved.reshape(shp)


def _to_shards(nm, full):
    ax = SHARD_AXIS[nm]
    shp = list(_shard_shape(nm))
    split = full.reshape(shp[:ax] + [N_CHIPS, shp[ax]] + shp[ax + 1:])
    return jnp.moveaxis(split, ax, 0)


def _row(v):
    return v.reshape(1, -1)


class Stacked:
    def __init__(self, arr, split, layer):
        self.arr, self.kw = arr, dict(b_split=split, b_layer=layer)


def _grad_out(split):
    return dict(out_dtype=BF16, out_split=split, out_layer=(0, 1))


def _with_next(res, next_gain):
    return (res[0], res[1]) if next_gain is not None else (res, None)


def mlp_fwd(tag, h, g, w_up, w_down, n=None, next_gain=None):
    if n is None:
        n = rms_fwd(tag + "_norm", h, g)
    act = mm(tag + "_up", n, w_up.arr, out_dtype=BF16, epi=lambda acc: jnp.square(jnp.maximum(acc, 0.0)), **w_up.kw)
    out, n_next = _with_next(mm(tag + "_down", act, w_down.arr, tk=LONG_TILE_K, epi=lambda acc, res: acc + res,
                                epi_tiles=(h,), norm_gain=next_gain, **w_down.kw), next_gain)
    return out, n_next, (n, act)


def mlp_bwd(tag, dh, h, g, w_up, w_down, saved, after=None):
    n, act = saved
    dh, dh16 = dh
    dup = mm(tag + "_d_act", dh16, w_down.arr, tb=True, out_dtype=BF16, after=after,
             epi=lambda acc, t: acc * (2.0 * jnp.sqrt(t.astype(F32))), epi_tiles=(act,), **w_down.kw)
    dw_down = mm(tag + "_dw_down", act, dh16, ta=True, tk=GRAD_TILE_K, **_grad_out("rows"))
    dh_in, dh16_in, dg = mm(tag + "_dn", dup, w_up.arr, tb=True, norm_bwd=(h, g, dh), **w_up.kw)
    dw_up = mm(tag + "_dw_up", n, dup, ta=True, tk=GRAD_TILE_K, **_grad_out("cols"))
    return (dh_in, dh16_in), dg, (dw_up, dw_down)


def xa_fwd(tag, h, mem, g, g_mem, w_q, w_kv, w_o, n=None, next_gain=None):
    if n is None:
        n = rms_fwd(tag + "_norm", h, g)
    mem_n = rms_fwd(tag + "_mem_norm", mem, g_mem)
    q = mm(tag + "_q", n, w_q.arr, out_dtype=BF16, **w_q.kw)
    kv = mm(tag + "_kv", mem_n, w_kv.arr, out_dtype=BF16, **w_kv.kw)
    o = xa_core_fwd(tag + "_core", q, kv)
    out, n_next = _with_next(mm(tag + "_o", o, w_o.arr, epi=lambda acc, res: acc + res, epi_tiles=(h,),
                                norm_gain=next_gain, **w_o.kw), next_gain)
    return out, n_next, (n, mem_n, q, kv, o)


def xa_bwd(tag, dh, h, mem, g, g_mem, w_q, w_kv, w_o, saved):
    n, mem_n, q, kv, o = saved
    dh, dh16 = dh
    d_o = mm(tag + "_d_o", dh16, w_o.arr, tb=True, out_dtype=BF16, **w_o.kw)
    dw_o = mm(tag + "_dw_o", o, dh16, ta=True, tk=GRAD_TILE_K, **_grad_out("rows"))
    dq, dkv = xa_core_bwd(tag + "_core_bwd", d_o, q, kv)
    dh_in, dh16_in, dg = mm(tag + "_dn", dq, w_q.arr, tb=True, norm_bwd=(h, g, dh), **w_q.kw)
    dw_q = mm(tag + "_dw_q", n, dq, ta=True, tk=GRAD_TILE_K, **_grad_out("rows"))
    dw_kv = mm(tag + "_dw_kv", mem_n, dkv, ta=True, **_grad_out("cols"))
    dmem_n = mm(tag + "_dmem", dkv, w_kv.arr, tb=True, **w_kv.kw)
    dg_mem = mem_norm_bwd(tag + "_mem_norm_bwd", dmem_n, mem, g_mem)
    return (dh_in, dh16_in), dg, dg_mem, (dw_q, dw_kv, dw_o)


def _gate_tile(a_log, dt_bias):
    t = jnp.zeros((8, LANES), F32)
    t = t.at[0, DN_HEADS:2 * DN_HEADS].set(a_log.reshape(-1))
    return t.at[1, DN_HEADS:2 * DN_HEADS].set(dt_bias.reshape(-1))


def dn_fwd(h, g, w_qkv, w_z, w_ba, w_conv, gate, out_norm, w_out, next_gain=None):
    n = rms_fwd("dn_norm", h, g)
    qkv_raw = mm("dn_proj_qkv", n, w_qkv)
    z = mm("dn_proj_z", n, w_z)
    ba = mm("dn_proj_ba", n, w_ba)
    qkv, hs = dn_pre(qkv_raw, ba, w_conv, gate)
    u, w, t_inv = dn_solve(qkv, hs)
    o, states = dn_scan_fwd(qkv, u, w, hs)
    og = dn_post(o, z, out_norm)
    out, n_next = _with_next(mm("dn_out", og, w_out.arr, epi=lambda acc, res: acc + res, epi_tiles=(h,),
                                norm_gain=next_gain, **w_out.kw), next_gain)
    return out, n_next, (n, qkv_raw, z, ba, qkv, hs, u, w, t_inv, o, states, og)


def dn_bwd(dh, h, g, w_qkv, w_z, w_ba, w_conv, gate, out_norm, w_out, saved, after=None):
    n, qkv_raw, z, ba, qkv, hs, u, w, t_inv, o, states, og = saved
    dh, dh16 = dh
    d_og = mm("dn_d_og", dh16, w_out.arr, tb=True, out_dtype=BF16, after=after, **w_out.kw)
    dw_out = mm("dn_dw_out", og, dh16, ta=True, tk=GRAD_TILE_K, **_grad_out("rows"))
    d_o, dz, d_out_norm = dn_post_bwd(d_og, o, z, out_norm)
    dq, dk, dv, dhs = dn_scan_bwd(qkv, u, w, t_inv, hs, states, d_o)
    dc, dba, d_gate = dn_pre_bwd(dq, dk, dv, dhs, qkv_raw, ba, w_conv, gate)
    dqkv_raw, dw_conv = dn_conv_bwd(dc, qkv_raw, w_conv)
    dn = mm("dn_dn_qkv", dqkv_raw, w_qkv, tb=True, tk=w_qkv.shape[1])
    dn = mm("dn_dn_z", dz, w_z, tb=True, epi=lambda acc, t: acc + t, epi_tiles=(dn,))
    dh_in, _, dg = mm("dn_dn_ba", dba, w_ba, tb=True, epi=lambda acc, t: acc + t, epi_tiles=(dn,),
                      norm_bwd=(h, g, dh))
    dw_qkv = mm("dn_dw_qkv", n, dqkv_raw, ta=True, tk=GRAD_TILE_K)
    dw_z = mm("dn_dw_z", n, dz, ta=True, tk=GRAD_TILE_K)
    dw_ba = mm("dn_dw_ba", n, dba, ta=True, tk=GRAD_TILE_K)
    return dh_in, dg, dw_qkv, dw_z, dw_ba, dw_conv, d_gate, d_out_norm, dw_out


def cv_fwd(h, g, w_pw1, b_pw1, w_dw, b_dw, ln_g, ln_b, w_pw2, b_pw2, n=None, next_gain=None):
    if n is None:
        n = rms_fwd("cv_norm", h, g)
    u = mm("cv_pw1", n, w_pw1.arr, epi=lambda acc, b: acc + b, epi_rows=(b_pw1,), **w_pw1.kw)
    s, c = cv_core_fwd(u, w_dw, b_dw, ln_g, ln_b)
    out, n_next = _with_next(mm("cv_pw2", s, w_pw2.arr, epi=lambda acc, res, b: acc + res + b, epi_tiles=(h,),
                                epi_rows=(b_pw2,), norm_gain=next_gain, **w_pw2.kw), next_gain)
    return out, n_next, (n, u, s, c)


def cv_bwd(dh, h, g, w_pw1, w_dw, ln_g, ln_b, w_pw2, saved):
    n, u, s, c = saved
    dh, dh16 = dh
    ds = mm("cv_d_s", dh16, w_pw2.arr, tb=True, out_dtype=BF16, **w_pw2.kw)
    dw_pw2 = mm("cv_dw_pw2", s, dh16, ta=True, tk=GRAD_TILE_K, **_grad_out("rows"))
    db_pw2 = col_sum("cv_db_pw2", dh)
    dc, ln_acc = cv_ln_bwd(ds, c, ln_g, ln_b)
    du, dw_dw, db_pw1 = cv_conv_bwd(dc, u, w_dw)
    dh_in, dh16_in, dg = mm("cv_dn", du, w_pw1.arr, tb=True, norm_bwd=(h, g, dh), **w_pw1.kw)
    dw_pw1 = mm("cv_dw_pw1", n, du, ta=True, tk=GRAD_TILE_K, **_grad_out("cols"))
    return (dh_in, dh16_in), dg, dw_pw1, db_pw1, dw_dw, ln_acc, dw_pw2, db_pw2


WEIGHTS = ["dn_norm", "dn_w_in", "dn_w_conv", "dn_a_log", "dn_dt_bias", "dn_out_norm", "dn_w_out", "cv_norm",
           "cv_w_pw1", "cv_b_pw1", "cv_w_dw", "cv_b_dw", "cv_ln_g", "cv_ln_b", "cv_w_pw2", "cv_b_pw2", "xa_norm",
           "xa_mem_norm", "xa_w_q", "xa_w_kv", "xa_w_o", "mlp_norm", "mlp_w_up", "mlp_w_down", "final_norm"]


def _as_2d(t):
    if t.ndim == 1:
        return t.reshape(1, -1)
    return t.reshape(-1, t.shape[-1])


def kernel(x, mem, dn_norm, dn_w_in, dn_w_conv, dn_a_log, dn_dt_bias, dn_out_norm, dn_w_out, cv_norm, cv_w_pw1, cv_b_pw1, cv_w_dw, cv_b_dw, cv_ln_g, cv_ln_b, cv_w_pw2, cv_b_pw2, xa_norm, xa_mem_norm, xa_w_q, xa_w_kv, xa_w_o, mlp_norm, mlp_w_up, mlp_w_down, final_norm, loss_target, m_dn_norm, m_dn_w_in, m_dn_w_conv, m_dn_a_log, m_dn_dt_bias, m_dn_out_norm, m_dn_w_out, m_cv_norm, m_cv_w_pw1, m_cv_b_pw1, m_cv_w_dw, m_cv_b_dw, m_cv_ln_g, m_cv_ln_b, m_cv_w_pw2, m_cv_b_pw2, m_xa_norm, m_xa_mem_norm, m_xa_w_q, m_xa_w_kv, m_xa_w_o, m_mlp_norm, m_mlp_w_up, m_mlp_w_down, m_final_norm, v_dn_norm, v_dn_w_in, v_dn_w_conv, v_dn_a_log, v_dn_dt_bias, v_dn_out_norm, v_dn_w_out, v_cv_norm, v_cv_w_pw1, v_cv_b_pw1, v_cv_w_dw, v_cv_b_dw, v_cv_ln_g, v_cv_ln_b, v_cv_w_pw2, v_cv_b_pw2, v_xa_norm, v_xa_mem_norm, v_xa_w_q, v_xa_w_kv, v_xa_w_o, v_mlp_norm, v_mlp_w_up, v_mlp_w_down, v_final_norm):
    args = dict(locals())
    wts = {nm: args[nm] for nm in WEIGHTS}
    mom = {nm: args["m_" + nm] for nm in WEIGHTS}
    var = {nm: args["v_" + nm] for nm in WEIGHTS}
    core = lax.axis_index("c").astype(jnp.int32).reshape(1)
    chip = (2 * lax.axis_index("x") + lax.axis_index("y")).astype(jnp.int32)
    def own_slab(got, src):
        return lax.dynamic_update_slice(got, src[None], (chip, 0, 0))

    shard2d = {nm: wts[nm].astype(BF16).reshape(-1, wts[nm].shape[-1]) for nm in BIG}
    first = ["dn_w_in", "dn_w_out"]
    later = [nm for nm in BIG if nm not in first]
    sources = [shard2d[nm] for nm in first] + [_pack(wts, SMALL, F32, SMALL_ROW_MULTIPLE)[0]]
    gathered = [own_slab(got, src) for got, src in zip(gather_shards(sources), sources)]
    stacked = {"dn_w_out": gathered[1].reshape((N_CHIPS,) + SHARD_SHAPES["dn_w_out"])}
    full = {nm: _to_full(nm, t) for nm, t in _unpack(gathered[2], SMALL).items()}
    full["dn_w_in"] = _to_full("dn_w_in", gathered[0].reshape((N_CHIPS,) + SHARD_SHAPES["dn_w_in"]))
    full.update({nm: wts[nm] for nm in REPLICATED})
    later_src = [shard2d[nm] for nm in later]
    g_send, g_recv, later_src, g_lands, started = ici_start(
        "gather_start", "gather", later_src, [(N_CHIPS,) + s.shape for s in later_src])
    full["dn_norm"] = full["dn_norm"] + started[0, 0]

    def rest_weights(after):
        srcs, lands = ici_wait("gather_wait", "gather", g_send, g_recv, later_src, g_lands, after)
        return {nm: own_slab(land, src).reshape((N_CHIPS,) + SHARD_SHAPES[nm])
                for nm, land, src in zip(later, pair_forward(lands), srcs)}

    pending = []

    flying = []

    def scatter(tag, items, parts, theirs):
        pairs = [add_pairs("pair_add_%s%d" % (nm, layer), p, t, core, BF16)
                 for (nm, layer, _), p, t in zip(items, parts, theirs)]
        send, recv, pairs, lands, token = ici_start(
            "scatter_start_" + tag, "scatter", pairs, [(N_CHIPS - 1,) + p.shape[1:] for p in pairs])
        pending.append((tag, items, send, recv, pairs, lands))
        return token

    def on_grads(items):
        tag = "_".join(sorted({str(layer) for _, layer, _ in items}))
        parts = [g.reshape(N_CHIPS, -1, g.shape[-1]) for _, _, g in items]
        if not flying and not pending:
            send, recv, parts, theirs, token = ici_start(
                "pair_start_" + tag, "pair", parts, [(p.shape[0], p.shape[1] // 2, p.shape[2]) for p in parts])
            flying.append((tag, items, send, recv, parts, theirs))
            return token
        for f_tag, f_items, send, recv, f_parts, f_theirs in flying:
            f_parts, f_theirs = ici_wait("pair_wait_" + f_tag, "pair", send, recv, f_parts, f_theirs, parts[0])
            scatter(f_tag, f_items, f_parts, f_theirs)
        return scatter(tag, items, parts, pair_split("pair_split_" + tag, parts))

    dh, grads, rep = local_step(x[0], mem[0], loss_target[0], stacked, full, rest_weights, on_grads)

    halves = {}
    last = [("dn_w_in", 0, _to_shards("dn_w_in", grads["dn_w_in"]).astype(BF16)), ("dn_w_out", 0, grads["dn_w_out"][0]),
            ("small", 0, _pack({nm: _to_shards(nm, grads[nm]) for nm in SMALL}, SMALL, F32, SMALL_ROW_MULTIPLE))]
    parts = [g.reshape(N_CHIPS, -1, g.shape[-1]) for _, _, g in last]
    theirs = pair_split("pair_split_last", parts)
    pairs = [add_pairs("pair_add_" + nm, p, t, core, p.dtype) for (nm, _, _), p, t in zip(last, parts, theirs)]
    l_send, l_recv, l_pairs, l_lands, l_started = ici_start(
        "scatter_start_last", "scatter", pairs, [(N_CHIPS - 1,) + p.shape[1:] for p in pairs])
    for tag, items, send, recv, pairs, lands in pending:
        pairs, lands = ici_wait("scatter_wait_" + tag, "scatter", send, recv, pairs, lands, l_started)
        for (nm, layer, _), p, o in zip(items, pairs, lands):
            halves[nm, layer] = add_four("chip_add_%s%d" % (nm, layer), p, o, chip.reshape(1))
    keys = sorted(halves)
    siblings = dict(zip(keys, pair_join("pair_join_early", [halves[k] for k in keys])))

    delta, new_m, new_v, red = {}, {}, {}, {}

    def big_adamw(nm):
        layers = range(wts[nm].shape[0])
        res = adamw_halves("adamw_" + nm, _as_2d(wts[nm]), [halves[nm, l] for l in layers],
                           [siblings[nm, l] for l in layers], _as_2d(mom[nm]), _as_2d(var[nm]), core)
        red[nm], delta[nm], new_m[nm], new_v[nm] = (r.reshape(wts[nm].shape) for r in res)

    early = [nm for nm in BIG if (nm, 0) in halves]
    for nm in early:
        big_adamw(nm)
    done = jnp.concatenate([new_v[nm].reshape(-1)[:1] for nm in early])
    l_pairs, l_lands = ici_wait("scatter_wait_last", "scatter", l_send, l_recv, l_pairs, l_lands, done)
    for (nm, layer, _), p, o in zip(last, l_pairs, l_lands):
        halves[nm, layer] = add_four("chip_add_" + nm, p, o, chip.reshape(1))
    keys = [(nm, layer) for nm, layer, _ in last]
    siblings.update(zip(keys, pair_join("pair_join_last", [halves[k] for k in keys])))
    south = core[0] == 0
    mine, theirs = halves["small", 0], siblings["small", 0]
    red.update(_unpack(jnp.concatenate([jnp.where(south, mine, theirs), jnp.where(south, theirs, mine)], axis=0),
                       SMALL))

    rep = all_sum_small(rep)
    red["dn_norm"] = rep[0:1]
    red["dn_a_log"] = rep[1:2, DN_HEADS:2 * DN_HEADS]
    red["dn_dt_bias"] = rep[2:3, DN_HEADS:2 * DN_HEADS]
    red["dn_out_norm"] = rep[3:4, :LANES]
    red["xa_norm"], red["xa_mem_norm"], red["mlp_norm"] = rep[4:6], rep[6:8], rep[8:10]
    red["final_norm"] = rep[10]
    loss = rep[11, 0]

    for nm in WEIGHTS:
        shp = wts[nm].shape
        if nm in early:
            continue
        if nm in BIG:
            big_adamw(nm)
            continue
        res = adamw("adamw_" + nm, _as_2d(wts[nm]), _as_2d(red[nm].reshape(shp)), _as_2d(mom[nm]), _as_2d(var[nm]))
        delta[nm], new_m[nm], new_v[nm] = (r.reshape(shp) for r in res)
        red[nm] = red[nm].reshape(shp)

    grad_x = dh[None]
    return (loss, grad_x, *[red[nm] for nm in WEIGHTS], *[delta[nm] for nm in WEIGHTS],
            *[new_m[nm] for nm in WEIGHTS], *[new_v[nm] for nm in WEIGHTS])


def local_step(h0, mem0, target, stacked, full, rest_weights=None, on_grads=None):
    d = h0.shape[1]
    dn_norm, dn_a_log, dn_dt_bias, dn_out_norm = (full[nm] for nm in REPLICATED[:4])
    xa_norm, xa_mem_norm, mlp_norm, final_norm = (full[nm] for nm in REPLICATED[4:])
    inner = DN_HEADS * DN_HEAD_DIM
    w_in = full["dn_w_in"][0]
    w_qkv, w_z = w_in[:, :3 * inner], w_in[:, 3 * inner:4 * inner]
    w_ba = jnp.pad(w_in[:, 4 * inner:], ((0, 0), (0, LANES - 2 * DN_HEADS)))
    w_conv = jnp.pad(full["dn_w_conv"][0], ((0, 8 - DN_CONV), (0, 0)))
    gate = _gate_tile(dn_a_log, dn_dt_bias)
    w_dw = jnp.pad(full["cv_w_dw"][0], ((0, CV_HALO - CV_WIDTH), (0, 0)))

    def sw(nm, layer):
        return Stacked(stacked[nm], "rows" if SHARD_AXIS[nm] == 1 else "cols", layer)

    dn_args = (_row(dn_norm), w_qkv, w_z, w_ba, w_conv, gate, _row(dn_out_norm), sw("dn_w_out", 0))
    h1, n, dn_saved = dn_fwd(h0, *dn_args, next_gain=_row(xa_norm[0]))
    if rest_weights is not None:
        stacked = {**stacked, **rest_weights(h1)}
    xa_args = [(_row(xa_norm[l]), _row(xa_mem_norm[l]), sw("xa_w_q", l), sw("xa_w_kv", l), sw("xa_w_o", l))
               for l in range(2)]
    mlp_args = [(_row(mlp_norm[l]), sw("mlp_w_up", l), sw("mlp_w_down", l)) for l in range(2)]
    cv_args = (_row(full["cv_norm"][0]), sw("cv_w_pw1", 0), full["cv_b_pw1"], w_dw, full["cv_b_dw"],
               full["cv_ln_g"], full["cv_ln_b"], sw("cv_w_pw2", 0), full["cv_b_pw2"])
    h2, n, xa0_saved = xa_fwd("xa0", h1, mem0, *xa_args[0], n=n, next_gain=mlp_args[0][0])
    h3, n, mlp0_saved = mlp_fwd("mlp0", h2, *mlp_args[0], n=n, next_gain=cv_args[0])
    h4, n, cv_saved = cv_fwd(h3, *cv_args, n=n, next_gain=xa_args[1][0])
    h5, n, xa1_saved = xa_fwd("xa1", h4, mem0, *xa_args[1], n=n, next_gain=mlp_args[1][0])
    h6, _, mlp1_saved = mlp_fwd("mlp1", h5, *mlp_args[1], n=n)

    dh32, dh16, loss_tile, d_final = loss_head("loss_head", h6, _row(final_norm), target)
    dh = (dh32, dh16)
    grads = {}
    dg_mlp, dg_xa, dg_xa_mem = [None, None], [None, None], [None, None]
    dw_mlp, dw_xa = [None, None], [None, None]
    mlp_names, xa_names = ("mlp_w_up", "mlp_w_down"), ("xa_w_q", "xa_w_kv", "xa_w_o")

    def announce(items):
        return None if on_grads is None else on_grads(items)

    dh, dg_mlp[1], dw_mlp[1] = mlp_bwd("mlp1", dh, h5, *mlp_args[1], mlp1_saved)
    dh, dg_xa[1], dg_xa_mem[1], dw_xa[1] = xa_bwd("xa1", dh, h4, mem0, *xa_args[1], xa1_saved)
    (dh, grads["cv_norm"], dw_pw1, grads["cv_b_pw1"], dw_dw, ln_acc, dw_pw2,
     grads["cv_b_pw2"]) = cv_bwd(dh, h3, cv_args[0], cv_args[1], w_dw, cv_args[5], cv_args[6], cv_args[7], cv_saved)
    after = announce([(nm, 1, g) for nm, g in zip(mlp_names + xa_names, dw_mlp[1] + dw_xa[1])]
                     + [("cv_w_pw1", 0, dw_pw1), ("cv_w_pw2", 0, dw_pw2)])
    dh, dg_mlp[0], dw_mlp[0] = mlp_bwd("mlp0", dh, h2, *mlp_args[0], mlp0_saved, after=after)
    dh, dg_xa[0], dg_xa_mem[0], dw_xa[0] = xa_bwd("xa0", dh, h1, mem0, *xa_args[0], xa0_saved)
    after = announce([(nm, 0, g) for nm, g in zip(mlp_names + xa_names, dw_mlp[0] + dw_xa[0])])
    dh, dg_dn, dw_qkv, dw_z, dw_ba, dw_conv, d_gate, d_out_norm, dw_out = dn_bwd(dh, h0, *dn_args, dn_saved,
                                                                                 after=after)

    grads["dn_w_in"] = jnp.concatenate([dw_qkv, dw_z, dw_ba[:, :2 * DN_HEADS]], axis=1)[None]
    grads["dn_w_conv"] = dw_conv[None, :DN_CONV]
    grads["dn_w_out"], grads["cv_w_pw1"], grads["cv_w_pw2"] = [dw_out], [dw_pw1], [dw_pw2]
    grads["cv_w_dw"] = dw_dw[None, :CV_WIDTH]
    grads["cv_ln_g"], grads["cv_ln_b"], grads["cv_b_dw"] = ln_acc[0:1], ln_acc[1:2], ln_acc[2:3]
    for i, nm in enumerate(mlp_names):
        grads[nm] = [dw_mlp[0][i], dw_mlp[1][i]]
    for i, nm in enumerate(xa_names):
        grads[nm] = [dw_xa[0][i], dw_xa[1][i]]

    rep = jnp.zeros((16, d), F32)
    rep = rep.at[0].set(dg_dn[0])
    rep = rep.at[1, :LANES].set(d_gate[0])
    rep = rep.at[2, :LANES].set(d_gate[1])
    rep = rep.at[3, :LANES].set(d_out_norm[0])
    rep = rep.at[4].set(dg_xa[0][0]).at[5].set(dg_xa[1][0])
    rep = rep.at[6].set(dg_xa_mem[0][0]).at[7].set(dg_xa_mem[1][0])
    rep = rep.at[8].set(dg_mlp[0][0]).at[9].set(dg_mlp[1][0])
    rep = rep.at[10].set(d_final[0])
    rep = rep.at[11, :LANES].set(loss_tile[0])
    return dh, grads, rep
```
